```python
import jax, jax.numpy as jnp
from jax import lax
import numpy as np

D_MODEL = 1024
BATCH = 8
SEQ = 4096
DEPTH = 2

N_MIXERS = 4
MIX_WIDTH = D_MODEL // 4
HEAD_DIM = 64
N_HEADS = MIX_WIDTH // HEAD_DIM
CHUNK = 64
ROPE_THETA = 10000.0
NORM_EPS = 1e-6
NEG_INF = -1e30

CONV_W = 4
NSA_KV_GROUPS = 2
NSA_HPG = N_HEADS // NSA_KV_GROUPS
KV_WIDTH = NSA_KV_GROUPS * HEAD_DIM
CMP_LEN = 32
CMP_STRIDE = 16
SEL_BLOCK = 64
SEL_TOPK = 16
WINDOW = 512
WIN_QBLOCK = 128
SEL_QBLOCK = 32
FORCE_BONUS = 1e3
PEER_HEADS = 8
PEER_NKEYS = 128
PEER_EXPERTS = PEER_NKEYS * PEER_NKEYS
PEER_QDIM = 256
PEER_TOPK = 16
PEER_TCHUNK = 128

IN_WIDTHS = (
    MIX_WIDTH, MIX_WIDTH, MIX_WIDTH, MIX_WIDTH,
    MIX_WIDTH, MIX_WIDTH, MIX_WIDTH, N_HEADS, N_HEADS, MIX_WIDTH,
    MIX_WIDTH, KV_WIDTH, KV_WIDTH, KV_WIDTH, KV_WIDTH, KV_WIDTH, KV_WIDTH, 3 * N_HEADS,
    MIX_WIDTH, MIX_WIDTH, MIX_WIDTH, MIX_WIDTH,
)
IN_OFFSETS = tuple(int(v) for v in np.cumsum(IN_WIDTHS)[:-1])
N_IN = sum(IN_WIDTHS)

kernel_name = 'hybrid_hgrn2_mlstm_nsa_retnet_peer'


def rms_norm(x, g):
    xf = x.astype(jnp.float32)
    y = xf * lax.rsqrt(jnp.mean(xf * xf, axis=-1, keepdims=True) + NORM_EPS)
    return (y * g.astype(jnp.float32)).astype(x.dtype)


def rope_tables(seq_len):
    inv = 1.0 / (ROPE_THETA ** (jnp.arange(0, HEAD_DIM, 2, dtype=jnp.float32) / HEAD_DIM))
    ang = jnp.arange(seq_len, dtype=jnp.float32)[:, None] * inv[None, :]
    return jnp.cos(ang), jnp.sin(ang)


def apply_rope(x, cos, sin):
    x1, x2 = jnp.split(x, 2, axis=-1)
    c = cos[None, :, None, :].astype(x.dtype)
    s = sin[None, :, None, :].astype(x.dtype)
    return jnp.concatenate([x1 * c - x2 * s, x1 * s + x2 * c], axis=-1)


def to_chunks(x):
    b, s, h, d = x.shape
    return x.reshape(b, s // CHUNK, CHUNK, h, d).transpose(1, 0, 3, 2, 4)


def from_chunks(y):
    n, b, h, c, d = y.shape
    return y.transpose(1, 0, 3, 2, 4).reshape(b, n * c, h, d)


def gate_chunks(g):
    b, s, h = g.shape
    return g.reshape(b, s // CHUNK, CHUNK, h).transpose(1, 0, 3, 2)


def masked_softmax(s, mask):
    s = jnp.where(mask, s.astype(jnp.float32), NEG_INF)
    return jnp.where(mask, jax.nn.softmax(s, axis=-1), 0.0)


def causal_dwconv(x, w, b):
    y = lax.conv_general_dilated(x, w[:, None, :].astype(x.dtype), window_strides=(1,),
                                 padding=[(w.shape[0] - 1, 0)],
                                 dimension_numbers=('NWC', 'WIO', 'NWC'),
                                 feature_group_count=x.shape[-1])
    return y + b.astype(x.dtype)


def hgrn2_mixer(q_pre, f_pre, i_pre, g_pre, lb, onorm_g):
    bsz, seq, _ = q_pre.shape
    heads = lambda t: t.reshape(bsz, seq, N_HEADS, HEAD_DIM).astype(jnp.float32)
    q = jax.nn.silu(heads(q_pre))
    fl = heads(f_pre)
    lbh = lb.reshape(N_HEADS, HEAD_DIM).astype(jnp.float32)
    log_f = jnp.logaddexp(jnp.log(lbh), jnp.log1p(-lbh) + jax.nn.log_sigmoid(fl))
    k = (1.0 - lbh) * jax.nn.sigmoid(-fl)
    v = heads(i_pre)
    causal = jnp.tril(jnp.ones((CHUNK, CHUNK), dtype=bool))[:, :, None]

    def step(state, xs):
        qc, kc, vc, gc = xs
        b = jnp.cumsum(gc, axis=2)
        diff = b[:, :, :, None, :] - b[:, :, None, :, :]
        decay = jnp.exp(jnp.where(causal, diff, -jnp.inf))
        a = jnp.einsum('bhtd,bhsd,bhtsd->bhts', qc, kc, decay)
        o = jnp.einsum('bhts,bhse->bhte', a, vc) + jnp.einsum('bhtd,bhde->bhte', qc * jnp.exp(b), state)
        b_last = b[:, :, -1:, :]
        state = jnp.exp(b_last[:, :, 0, :])[..., None] * state + \
            jnp.einsum('bhsd,bhse->bhde', kc * jnp.exp(b_last - b), vc)
        return state, o

    s0 = jnp.zeros((bsz, N_HEADS, HEAD_DIM, HEAD_DIM), jnp.float32)
    _, o = lax.scan(step, s0, (to_chunks(q), to_chunks(k), to_chunks(v), to_chunks(log_f)))
    o = rms_norm(from_chunks(o), onorm_g.reshape(N_HEADS, HEAD_DIM)) * jax.nn.silu(heads(g_pre))
    return o.reshape(bsz, seq, MIX_WIDTH).astype(q_pre.dtype)


def mlstm_mixer(q_pre, k_pre, v_pre, i_pre, f_pre, o_pre, conv_w, conv_b, gate_b, onorm_g):
    bsz, seq, _ = q_pre.shape
    heads = lambda t: t.reshape(bsz, seq, N_HEADS, HEAD_DIM).astype(jnp.float32)
    qk = jax.nn.silu(causal_dwconv(jnp.concatenate([q_pre, k_pre], axis=-1), conv_w, conv_b))
    q_c, k_c = jnp.split(qk, 2, axis=-1)
    q = heads(q_c)
    k = heads(k_c) * (HEAD_DIM ** -0.5)
    v = heads(v_pre)
    log_i = (i_pre + gate_b[:N_HEADS]).astype(jnp.float32)
    log_f = jax.nn.log_sigmoid((f_pre + gate_b[N_HEADS:]).astype(jnp.float32))
    causal = jnp.tril(jnp.ones((CHUNK, CHUNK), dtype=bool))

    def step(carry, xs):
        c_st, n_st, m_st = carry
        qc, kc, vc, li, lf = xs
        b = jnp.cumsum(lf, axis=-1)
        dmat = jnp.where(causal, b[..., :, None] - b[..., None, :] + li[..., None, :], -jnp.inf)
        inter = b + m_st[..., None]
        m = jnp.maximum(inter, jnp.max(dmat, axis=-1))
        w = jnp.exp(dmat - m[..., None]) * jnp.einsum('bhtd,bhsd->bhts', qc, kc)
        s_inter = jnp.exp(inter - m)
        num = s_inter[..., None] * jnp.einsum('bhtd,bhde->bhte', qc, c_st) + jnp.einsum('bhts,bhse->bhte', w, vc)
        den = s_inter * jnp.einsum('bhtd,bhd->bht', qc, n_st) + jnp.sum(w, axis=-1)
        h = num / jnp.maximum(jnp.abs(den), jnp.exp(-m))[..., None]
        b_last = b[..., -1]
        lw = b_last[..., None] - b + li
        m_new = jnp.maximum(b_last + m_st, jnp.max(lw, axis=-1))
        wk = jnp.exp(lw - m_new[..., None])
        decay = jnp.exp(b_last + m_st - m_new)
        c_new = decay[..., None, None] * c_st + jnp.einsum('bhs,bhsd,bhse->bhde', wk, kc, vc)
        n_new = decay[..., None] * n_st + jnp.einsum('bhs,bhsd->bhd', wk, kc)
        return (c_new, n_new, m_new), h

    init = (jnp.zeros((bsz, N_HEADS, HEAD_DIM, HEAD_DIM), jnp.float32),
            jnp.zeros((bsz, N_HEADS, HEAD_DIM), jnp.float32),
            jnp.zeros((bsz, N_HEADS), jnp.float32))
    _, h = lax.scan(step, init, (to_chunks(q), to_chunks(k), to_chunks(v), gate_chunks(log_i), gate_chunks(log_f)))
    h = rms_norm(from_chunks(h), onorm_g.reshape(N_HEADS, HEAD_DIM)) * jax.nn.sigmoid(heads(o_pre))
    return h.reshape(bsz, seq, MIX_WIDTH).astype(q_pre.dtype)


def nsa_mixer(q_pre, kc_pre, vc_pre, ks_pre, vs_pre, kw_pre, vw_pre, gate_pre, cos, sin,
              qnorm_g, knorm_g, cmp_pe, cmp_w):
    bsz, seq, _ = q_pre.shape
    G, HG, DH = NSA_KV_GROUPS, NSA_HPG, HEAD_DIM
    scale = DH ** -0.5
    kvh = lambda t: t.reshape(bsz, seq, G, DH)
    q = rms_norm(q_pre.reshape(bsz, seq, N_HEADS, DH), qnorm_g)
    q_cmp = q.reshape(bsz, seq, G, HG, DH)
    q_rot = apply_rope(q, cos, sin).reshape(bsz, seq, G, HG, DH)
    pos = jnp.arange(seq)

    n_cmp = (seq - CMP_LEN) // CMP_STRIDE + 1
    cmp_start = jnp.arange(n_cmp) * CMP_STRIDE
    cidx = cmp_start[:, None] + jnp.arange(CMP_LEN)[None, :]

    def compress(t, pe, w):
        blocks = t[:, cidx] + pe[None, None, :, None, :].astype(t.dtype)
        return jnp.einsum('bnlgd,lde->bnge', blocks, w.reshape(CMP_LEN, DH, DH))

    k_cmp = rms_norm(compress(kvh(kc_pre), cmp_pe[0], cmp_w[0]), knorm_g[0])
    v_cmp = compress(kvh(vc_pre), cmp_pe[1], cmp_w[1])
    s_cmp = jnp.einsum('bsghd,bngd->bghsn', q_cmp, k_cmp).astype(jnp.float32) * scale
    cmp_mask = (cmp_start + CMP_LEN - 1)[None, :] <= pos[:, None]
    p_cmp = masked_softmax(s_cmp, cmp_mask)
    o_cmp = jnp.einsum('bghsn,bngd->bsghd', p_cmp, v_cmp)

    n_sel = seq // SEL_BLOCK
    n_top = min(SEL_TOPK, n_sel)
    sel_start = jnp.arange(n_sel) * SEL_BLOCK
    overlap = ((cmp_start[:, None] < sel_start[None, :] + SEL_BLOCK) &
               (cmp_start[:, None] + CMP_LEN > sel_start[None, :])).astype(jnp.float32)
    imp = jnp.einsum('bghsn,nj->bgsj', p_cmp, overlap)
    blk = jnp.arange(n_sel)[None, :]
    cur = (pos // SEL_BLOCK)[:, None]
    forced = (blk == 0) | (blk == cur) | (blk == cur - 1)
    imp = jnp.where(blk <= cur, imp + FORCE_BONUS * forced.astype(jnp.float32), NEG_INF)
    top_val, sel_idx = lax.top_k(imp, n_top)
    sel_ok = top_val > 0.5 * NEG_INF

    k_sel = apply_rope(rms_norm(kvh(ks_pre), knorm_g[1]), cos, sin)
    k_blocks = k_sel.reshape(bsz, n_sel, SEL_BLOCK, G, DH).transpose(0, 3, 1, 2, 4)
    v_blocks = kvh(vs_pre).reshape(bsz, n_sel, SEL_BLOCK, G, DH).transpose(0, 3, 1, 2, 4)
    nqb = seq // SEL_QBLOCK
    q_b = q_rot.reshape(bsz, nqb, SEL_QBLOCK, G, HG, DH).transpose(1, 0, 2, 3, 4, 5)
    idx_b = sel_idx.reshape(bsz, G, nqb, SEL_QBLOCK, n_top).transpose(2, 0, 1, 3, 4)
    ok_b = sel_ok.reshape(bsz, G, nqb, SEL_QBLOCK, n_top).transpose(2, 0, 1, 3, 4)
    pos_b = pos.reshape(nqb, SEL_QBLOCK)
    gather = jax.vmap(jax.vmap(lambda blocks, ids: blocks[ids]))
    n_tok = n_top * SEL_BLOCK

    def sel_block(args):
        qb, ib, okb, pb = args
        kg = gather(k_blocks, ib)
        vg = gather(v_blocks, ib)
        s = jnp.einsum('bqghd,bgqkld->bghqkl', qb, kg).astype(jnp.float32) * scale
        tok = ib[..., None] * SEL_BLOCK + jnp.arange(SEL_BLOCK)
        m = okb[..., None] & (tok <= pb[None, None, :, None, None])
        p = masked_softmax(s.reshape(bsz, G, HG, SEL_QBLOCK, n_tok),
                           m.reshape(bsz, G, 1, SEL_QBLOCK, n_tok))
        return jnp.einsum('bghqn,bgqnd->bqghd', p, vg.reshape(bsz, G, SEL_QBLOCK, n_tok, DH))

    o_sel = lax.map(sel_block, (q_b, idx_b, ok_b, pos_b))
    o_sel = o_sel.transpose(1, 0, 2, 3, 4, 5).reshape(bsz, seq, G, HG, DH)

    k_win = apply_rope(rms_norm(kvh(kw_pre), knorm_g[2]), cos, sin)
    v_win = kvh(vw_pre)
    nwb = seq // WIN_QBLOCK
    span = WINDOW + WIN_QBLOCK
    kp = jnp.pad(k_win, ((0, 0), (WINDOW, 0), (0, 0), (0, 0)))
    vp = jnp.pad(v_win, ((0, 0), (WINDOW, 0), (0, 0), (0, 0)))
    kidx = jnp.arange(nwb)[:, None] * WIN_QBLOCK + jnp.arange(span)[None, :]
    kb = kp[:, kidx]
    vb = vp[:, kidx]
    qw = q_rot.reshape(bsz, nwb, WIN_QBLOCK, G, HG, DH)
    s_w = jnp.einsum('bnqghd,bnkgd->bghnqk', qw, kb).astype(jnp.float32) * scale
    qpos = pos.reshape(nwb, WIN_QBLOCK)[:, :, None]
    kpos = (kidx - WINDOW)[:, None, :]
    wmask = (kpos <= qpos) & (kpos > qpos - WINDOW) & (kpos >= 0)
    p_w = masked_softmax(s_w, wmask)
    o_win = jnp.einsum('bghnqk,bnkgd->bnqghd', p_w, vb).reshape(bsz, seq, G, HG, DH)

    g = jax.nn.sigmoid(gate_pre.astype(jnp.float32)).reshape(bsz, seq, G, HG, 3, 1)
    o = g[..., 0, :] * o_cmp + g[..., 1, :] * o_sel + g[..., 2, :] * o_win
    return o.reshape(bsz, seq, MIX_WIDTH).astype(q_pre.dtype)


def retention_mixer(q_pre, k_pre, v_pre, g_pre, cos, sin, onorm_g):
    bsz, seq, _ = q_pre.shape
    heads = lambda t: t.reshape(bsz, seq, N_HEADS, HEAD_DIM)
    q = apply_rope(heads(q_pre), cos, sin).astype(jnp.float32)
    k = apply_rope(heads(k_pre), cos, sin).astype(jnp.float32) * (HEAD_DIM ** -0.5)
    v = heads(v_pre).astype(jnp.float32)
    log_gamma = jnp.log1p(-jnp.exp2(-5.0 - jnp.arange(N_HEADS, dtype=jnp.float32)))
    t = jnp.arange(CHUNK, dtype=jnp.float32)
    causal = jnp.tril(jnp.ones((CHUNK, CHUNK), dtype=bool))
    dec_in = jnp.exp(jnp.where(causal, log_gamma[:, None, None] * (t[:, None] - t[None, :]), -jnp.inf))
    dec_q = jnp.exp(log_gamma[:, None] * (t + 1.0))[:, :, None]
    dec_k = jnp.exp(log_gamma[:, None] * (CHUNK - 1.0 - t))[:, :, None]
    dec_state = jnp.exp(log_gamma * CHUNK)[:, None, None]

    def step(r, xs):
        qc, kc, vc = xs
        a = jnp.einsum('bhtd,bhsd->bhts', qc, kc) * dec_in
        o = jnp.einsum('bhts,bhse->bhte', a, vc) + jnp.einsum('bhtd,bhde->bhte', qc, r) * dec_q
        r = dec_state * r + jnp.einsum('bhsd,bhse->bhde', kc * dec_k, vc)
        return r, o

    r0 = jnp.zeros((bsz, N_HEADS, HEAD_DIM, HEAD_DIM), jnp.float32)
    _, o = lax.scan(step, r0, (to_chunks(q), to_chunks(k), to_chunks(v)))
    o = rms_norm(from_chunks(o), onorm_g.reshape(N_HEADS, HEAD_DIM)) * jax.nn.silu(heads(g_pre).astype(jnp.float32))
    return o.reshape(bsz, seq, MIX_WIDTH).astype(q_pre.dtype)


def peer_ffn(x, wq, sub_keys, u_tab, v_tab):
    bsz, seq, d = x.shape
    xt = x.reshape(bsz * seq // PEER_TCHUNK, PEER_TCHUNK, d)

    def chunk(xc):
        q = (xc @ wq).reshape(PEER_TCHUNK, PEER_HEADS, 2, PEER_QDIM // 2)
        s = jnp.einsum('thpd,hpnd->thpn', q, sub_keys).astype(jnp.float32)
        v1, i1 = lax.top_k(s[:, :, 0], PEER_TOPK)
        v2, i2 = lax.top_k(s[:, :, 1], PEER_TOPK)
        cand = (v1[..., :, None] + v2[..., None, :]).reshape(PEER_TCHUNK, PEER_HEADS, PEER_TOPK * PEER_TOPK)
        vals, pos = lax.top_k(cand, PEER_TOPK)
        e = jnp.take_along_axis(i1, pos // PEER_TOPK, axis=-1) * PEER_NKEYS + \
            jnp.take_along_axis(i2, pos % PEER_TOPK, axis=-1)
        g = jax.nn.softmax(vals, axis=-1)
        act = jax.nn.gelu(jnp.einsum('td,thkd->thk', xc, u_tab[e]).astype(jnp.float32), approximate=False)
        return jnp.einsum('thk,thkd->td', (g * act).astype(xc.dtype), v_tab[e])

    return lax.map(chunk, xt).reshape(bsz, seq, d)


def setup_inputs(seed: int = 0) -> dict:
    key = jax.random.key(seed)
    ks = jax.random.split(key, 24)
    nrm = lambda k, shape, s: jax.random.normal(k, shape, jnp.float32) * s
    gain = lambda k, shape: 1.0 + 0.02 * jax.random.normal(k, shape, jnp.float32)
    mlstm_gate_b = jnp.concatenate([
        nrm(ks[6], (DEPTH, N_HEADS), 0.1),
        jnp.linspace(3.0, 6.0, N_HEADS, dtype=jnp.float32)[None, :] + nrm(ks[7], (DEPTH, N_HEADS), 0.1)], axis=-1)
    return {
        'x': jax.random.normal(ks[0], (BATCH, SEQ, D_MODEL), jnp.float32),
        'norm1_g': gain(ks[1], (DEPTH, D_MODEL)),
        'w_in': nrm(ks[2], (DEPTH, D_MODEL, N_IN), D_MODEL ** -0.5),
        'hgrn_lb': nrm(ks[3], (DEPTH, MIX_WIDTH), 1.0),
        'hgrn_onorm_g': gain(ks[4], (DEPTH, MIX_WIDTH)),
        'mlstm_conv_w': nrm(ks[5], (DEPTH, CONV_W, 2 * MIX_WIDTH), CONV_W ** -0.5),
        'mlstm_conv_b': nrm(ks[8], (DEPTH, 2 * MIX_WIDTH), 0.01),
        'mlstm_gate_b': mlstm_gate_b,
        'mlstm_onorm_g': gain(ks[9], (DEPTH, MIX_WIDTH)),
        'nsa_qnorm_g': gain(ks[10], (DEPTH, HEAD_DIM)),
        'nsa_knorm_g': gain(ks[11], (DEPTH, 3, HEAD_DIM)),
        'nsa_cmp_pe': nrm(ks[12], (DEPTH, 2, CMP_LEN, HEAD_DIM), 0.1),
        'nsa_cmp_w': nrm(ks[13], (DEPTH, 2, CMP_LEN * HEAD_DIM, HEAD_DIM), (CMP_LEN * HEAD_DIM) ** -0.5),
        'ret_onorm_g': gain(ks[14], (DEPTH, MIX_WIDTH)),
        'w_up': nrm(ks[15], (DEPTH, N_MIXERS, MIX_WIDTH, D_MODEL), MIX_WIDTH ** -0.5),
        'w_gate': nrm(ks[16], (DEPTH, N_MIXERS, D_MODEL, D_MODEL), D_MODEL ** -0.5),
        'w_out': nrm(ks[17], (DEPTH, D_MODEL, D_MODEL), 0.5 * D_MODEL ** -0.5),
        'norm2_g': gain(ks[18], (DEPTH, D_MODEL)),
        'peer_wq': nrm(ks[19], (DEPTH, D_MODEL, PEER_HEADS * PEER_QDIM), D_MODEL ** -0.5),
        'peer_keys': nrm(ks[20], (DEPTH, PEER_HEADS, 2, PEER_NKEYS, PEER_QDIM // 2), (PEER_QDIM // 2) ** -0.5),
        'peer_u': nrm(ks[21], (DEPTH, PEER_EXPERTS, D_MODEL), D_MODEL ** -0.5),
        'peer_v': nrm(ks[22], (DEPTH, PEER_EXPERTS, D_MODEL), 0.2),
    }


def reference(x, norm1_g, w_in, hgrn_lb, hgrn_onorm_g, mlstm_conv_w, mlstm_conv_b, mlstm_gate_b,
              mlstm_onorm_g, nsa_qnorm_g, nsa_knorm_g, nsa_cmp_pe, nsa_cmp_w, ret_onorm_g, w_up,
              w_gate, w_out, norm2_g, peer_wq, peer_keys, peer_u, peer_v):
    seq = x.shape[1]
    cos, sin = rope_tables(seq)
    lb_cum = jnp.cumsum(jax.nn.softmax(hgrn_lb.astype(jnp.float32), axis=0), axis=0)
    lb_all = lb_cum - lb_cum[0:1]
    for l in range(DEPTH):
        xn = rms_norm(x, norm1_g[l])
        (hq, hf, hi, hg, mq, mk, mv, mi, mf, mo, nq, nkc, nvc, nks, nvs, nkw, nvw, ng,
         rq, rk, rv, rg) = jnp.split(xn @ w_in[l], IN_OFFSETS, axis=-1)
        outs = (
            hgrn2_mixer(hq, hf, hi, hg, lb_all[l], hgrn_onorm_g[l]),
            mlstm_mixer(mq, mk, mv, mi, mf, mo, mlstm_conv_w[l], mlstm_conv_b[l], mlstm_gate_b[l], mlstm_onorm_g[l]),
            nsa_mixer(nq, nkc, nvc, nks, nvs, nkw, nvw, ng, cos, sin, nsa_qnorm_g[l], nsa_knorm_g[l],
                      nsa_cmp_pe[l], nsa_cmp_w[l]),
            retention_mixer(rq, rk, rv, rg, cos, sin, ret_onorm_g[l]),
        )
        merged = sum(jax.nn.sigmoid(xn @ w_gate[l, m]) * (outs[m] @ w_up[l, m]) for m in range(N_MIXERS))
        x = x + merged @ w_out[l]
        x = x + peer_ffn(rms_norm(x, norm2_g[l]), peer_wq[l], peer_keys[l], peer_u[l], peer_v[l])
    return x
```

```python
import functools
import math

import numpy as np
import jax
import jax.numpy as jnp
from jax import lax
from jax.experimental import pallas as pl
from jax.experimental.pallas import tpu as pltpu

F32 = jnp.float32
BF16 = jnp.bfloat16

HEAD_DIM = 64
N_HEADS = 4
MIX_WIDTH = N_HEADS * HEAD_DIM
CHUNK = 64
NORM_EPS = 1e-6
NEG_INF = -1e30
ROPE_THETA = 10000.0
CONV_W = 4
NSA_GROUPS = 2
CMP_LEN = 32
CMP_STRIDE = 16
SEL_BLOCK = 64
SEL_TOPK = 16
WINDOW = 512
FORCE_BONUS = 1e3
PEER_HEADS = 8
PEER_NKEYS = 128
PEER_TOPK = 16
PEER_KDIM = 128

LANES = 128
VMEM_LIMIT = 48 * 1024 * 1024

OFF_H, OFF_M, OFF_R, OFF_KD, OFF_NQ, OFF_V, OFF_MG, OFF_NG = 0, 1024, 2048, 3072, 3584, 3840, 4096, 4224
N_MAIN = 4352


def _cp(*sem):
    return pltpu.CompilerParams(dimension_semantics=sem, vmem_limit_bytes=VMEM_LIMIT)


def _dot(a, b):
    return jnp.dot(a, b, preferred_element_type=F32)


def _dot_nt(a, b):
    return lax.dot_general(a, b, (((1,), (1,)), ((), ())), preferred_element_type=F32)


def _dot_tn(a, b):
    return lax.dot_general(a, b, (((0,), (0,)), ((), ())), preferred_element_type=F32)


def _split3(x):
    hi = x.astype(BF16)
    r1 = x - hi.astype(F32)
    mid = r1.astype(BF16)
    lo = (r1 - mid.astype(F32)).astype(BF16)
    return hi, mid, lo


def _dot01_l(m01, x):
    hi, mid, lo = _split3(x)
    return _dot(m01, hi) + _dot(m01, mid) + _dot(m01, lo)


def _dot01_r(x, m01):
    hi, mid, lo = _split3(x)
    return _dot(hi, m01) + _dot(mid, m01) + _dot(lo, m01)


def _head_of_lane(shape, axis):
    return lax.broadcasted_iota(jnp.int32, shape, axis) // HEAD_DIM


def _block_ones(n, dtype=BF16):
    r = lax.broadcasted_iota(jnp.int32, (n, n), 0) // HEAD_DIM
    c = lax.broadcasted_iota(jnp.int32, (n, n), 1) // HEAD_DIM
    return (r == c).astype(dtype)


def _group_sum(x, ones_bd):
    hi = x.astype(BF16)
    lo = (x - hi.astype(F32)).astype(BF16)
    return _dot(hi, ones_bd) + _dot(lo, ones_bd)


def _head_rms(x, gain, ones_bd):
    ms = _group_sum(x * x, ones_bd) * (1.0 / HEAD_DIM)
    return x * lax.rsqrt(ms + NORM_EPS) * gain


def _sigmoid(x):
    return 1.0 / (1.0 + jnp.exp(-x))


def _silu(x):
    return x * _sigmoid(x)


def _log_sigmoid(x):
    return jnp.minimum(x, 0.0) - jnp.log(1.0 + jnp.exp(-jnp.abs(x)))


def _stack_heads(x, n_heads=N_HEADS):
    hl = _head_of_lane(x.shape, 1)
    return jnp.concatenate([jnp.where(hl == h, x, jnp.zeros_like(x)) for h in range(n_heads)], axis=0)


def _unstack_heads(r, c, n_heads=N_HEADS):
    hl = _head_of_lane((c, r.shape[1]), 1)
    out = jnp.zeros((c, r.shape[1]), F32)
    for h in range(n_heads):
        out = jnp.where(hl == h, r[h * c:(h + 1) * c, :], out)
    return out


def _rope(x, cos_t, sin_t):
    n = x.shape[1]
    first = (lax.broadcasted_iota(jnp.int32, x.shape, 1) % HEAD_DIM) < (HEAD_DIM // 2)
    partner = jnp.where(first, pltpu.roll(x, n - HEAD_DIM // 2, 1), pltpu.roll(x, HEAD_DIM // 2, 1))
    return x * cos_t + partner * sin_t


def _norm_matmul_kernel(x_ref, g_ref, w_ref, o_ref, xn_ref, *, act):
    @pl.when(pl.program_id(1) == 0)
    def _():
        x = x_ref[...]
        ms = jnp.mean(x * x, axis=-1, keepdims=True)
        xn_ref[...] = (x * lax.rsqrt(ms + NORM_EPS) * g_ref[...]).astype(BF16)

    y = _dot(xn_ref[...], w_ref[...])
    if act == "sigmoid":
        y = _sigmoid(y)
    o_ref[...] = y.astype(o_ref.dtype)


def _norm_matmul(x, g, w, *, act=None, out_dtype=F32, tm=1024, tn=512):
    t, d = x.shape
    n = w.shape[1]
    tm = min(tm, t)
    tn = next(c for c in (tn, 256, 128) if n % c == 0)
    assert t % tm == 0
    return pl.pallas_call(
        functools.partial(_norm_matmul_kernel, act=act),
        grid=(t // tm, n // tn),
        in_specs=[pl.BlockSpec((tm, d), lambda i, j: (i, 0)),
                  pl.BlockSpec((1, d), lambda i, j: (0, 0)),
                  pl.BlockSpec((d, tn), lambda i, j: (0, j))],
        out_specs=pl.BlockSpec((tm, tn), lambda i, j: (i, j)),
        out_shape=jax.ShapeDtypeStruct((t, n), out_dtype),
        scratch_shapes=[pltpu.VMEM((tm, d), BF16)],
        compiler_params=_cp("parallel", "arbitrary"),
    )(x, g.reshape(1, d), w)


def _merge_kernel(x_ref, gate_ref, oh_ref, om_ref, on_ref, or_ref, wup_ref, wout_ref, o_ref):
    d = x_ref.shape[1]
    acc = None
    for m, r in enumerate((oh_ref, om_ref, on_ref, or_ref)):
        up = _dot(r[...].astype(BF16), wup_ref[m])
        term = gate_ref[:, m * d:(m + 1) * d].astype(F32) * up
        acc = term if acc is None else acc + term
    o_ref[...] = x_ref[...] + _dot(acc.astype(BF16), wout_ref[...])


def _merge(x, gates, outs, w_up, w_out, tm=512):
    t, d = x.shape
    tm = min(tm, t)
    mix = pl.BlockSpec((tm, MIX_WIDTH), lambda i: (i, 0))
    return pl.pallas_call(
        _merge_kernel,
        grid=(t // tm,),
        in_specs=[pl.BlockSpec((tm, d), lambda i: (i, 0)),
                  pl.BlockSpec((tm, 4 * d), lambda i: (i, 0)),
                  mix, mix, mix, mix,
                  pl.BlockSpec((4, MIX_WIDTH, d), lambda i: (0, 0, 0)),
                  pl.BlockSpec((d, d), lambda i: (0, 0))],
        out_specs=pl.BlockSpec((tm, d), lambda i: (i, 0)),
        out_shape=jax.ShapeDtypeStruct((t, d), F32),
        compiler_params=_cp("parallel"),
    )(x, gates, *outs, w_up, w_out)


REC_BLOCK = 256


def _chunk_consts():
    t = lax.broadcasted_iota(jnp.int32, (CHUNK, CHUNK), 0)
    s = lax.broadcasted_iota(jnp.int32, (CHUNK, CHUNK), 1)
    return t, s


def _hgrn_levels():
    t = np.arange(CHUNK)
    rows = []
    masks = []
    h = CHUNK // 2
    while h >= 1:
        ref = (t // (2 * h)) * (2 * h) + h
        p = np.zeros((CHUNK, CHUNK), np.float32)
        p[t, np.minimum(ref, CHUNK - 1)] = 1.0
        rows.append(p)
        same = (t[:, None] // (2 * h)) == (t[None, :] // (2 * h))
        m = same & ((t[:, None] // h) % 2 == 1) & ((t[None, :] // h) % 2 == 0)
        masks.append(m.astype(np.float32))
        h //= 2
    masks.append(np.eye(CHUNK, dtype=np.float32))
    return np.concatenate(rows, 0), np.stack(masks, 0)


def _hgrn_kernel(p_ref, lb_ref, g_ref, psel_ref, lmask_ref, o_ref, st_ref):
    @pl.when(pl.program_id(1) == 0)
    def _():
        st_ref[...] = jnp.zeros_like(st_ref)

    c = CHUNK
    w = MIX_WIDTH
    ones_bd = _block_ones(w)
    bd_mask = _block_ones(w, F32)
    tri = (lax.broadcasted_iota(jnp.int32, (c, c), 0) >= lax.broadcasted_iota(jnp.int32, (c, c), 1)).astype(BF16)
    psel = psel_ref[...]
    n_lv = lmask_ref.shape[0]
    log_lb, log_1mlb, one_mlb = lb_ref[0:1, :], lb_ref[1:2, :], lb_ref[2:3, :]
    gain = g_ref[...]

    def chunk(ci, carry):
        r0 = pl.multiple_of(ci * c, c)
        q = _silu(p_ref[pl.ds(r0, c), 0:w])
        fl = p_ref[pl.ds(r0, c), w:2 * w]
        v = p_ref[pl.ds(r0, c), 2 * w:3 * w]
        gp = p_ref[pl.ds(r0, c), 3 * w:4 * w]
        a1 = jnp.broadcast_to(log_lb, fl.shape)
        a2 = log_1mlb + _log_sigmoid(fl)
        mx = jnp.maximum(a1, a2)
        log_f = mx + jnp.log(jnp.exp(a1 - mx) + jnp.exp(a2 - mx))
        k = one_mlb * _sigmoid(-fl)
        b = _dot01_l(tri, log_f)
        bref = _dot01_l(psel, b)
        vb = v.astype(BF16)
        a = jnp.zeros((N_HEADS * c, c), F32)
        for lv in range(n_lv):
            if lv < n_lv - 1:
                br = bref[lv * c:(lv + 1) * c, :]
                qs = q * jnp.exp(jnp.minimum(b - br, 0.0))
                ks = k * jnp.exp(jnp.minimum(br - b, 0.0))
            else:
                qs, ks = q, k
            s_lv = _dot_nt(_stack_heads(qs).astype(BF16), ks.astype(BF16))
            a = a + jnp.concatenate([lmask_ref[lv]] * N_HEADS, axis=0) * s_lv
        o = _unstack_heads(_dot(a.astype(BF16), vb), c)
        st = st_ref[...]
        o = o + _dot_nt((q * jnp.exp(b)).astype(BF16), st.astype(BF16))
        b_last = b[c - 1:c, :]
        kb = k * jnp.exp(b_last - b)
        st_ref[...] = st * jnp.exp(b_last) + bd_mask * _dot_tn(vb, kb.astype(BF16))
        y = _head_rms(o, gain, ones_bd) * _silu(gp)
        o_ref[pl.ds(r0, c), :] = y
        return carry

    lax.fori_loop(0, p_ref.shape[0] // c, chunk, 0)


def _hgrn(proj, lb_rows, gain, bsz, seq):
    psel, lmask = _hgrn_levels()
    tb = min(REC_BLOCK, seq)
    nb = seq // tb
    return pl.pallas_call(
        _hgrn_kernel,
        grid=(bsz, nb),
        in_specs=[pl.BlockSpec((tb, 4 * MIX_WIDTH), lambda b, i: (b * nb + i, OFF_H // (4 * MIX_WIDTH))),
                  pl.BlockSpec((8, MIX_WIDTH), lambda b, i: (0, 0)),
                  pl.BlockSpec((1, MIX_WIDTH), lambda b, i: (0, 0)),
                  pl.BlockSpec(psel.shape, lambda b, i: (0, 0)),
                  pl.BlockSpec(lmask.shape, lambda b, i: (0, 0, 0))],
        out_specs=pl.BlockSpec((tb, MIX_WIDTH), lambda b, i: (b * nb + i, 0)),
        out_shape=jax.ShapeDtypeStruct((bsz * seq, MIX_WIDTH), F32),
        scratch_shapes=[pltpu.VMEM((MIX_WIDTH, MIX_WIDTH), F32)],
        compiler_params=_cp("parallel", "arbitrary"),
    )(proj, lb_rows, gain.reshape(1, MIX_WIDTH), jnp.asarray(psel, BF16), jnp.asarray(lmask, F32))


def _ret_kernel(p_ref, cos_ref, sin_ref, dec_ref, decin_ref, g_ref, o_ref, st_ref):
    @pl.when(pl.program_id(1) == 0)
    def _():
        st_ref[...] = jnp.zeros_like(st_ref)

    c = CHUNK
    w = MIX_WIDTH
    ones_bd = _block_ones(w)
    bd_mask = _block_ones(w, F32)
    gain = g_ref[...]
    dec_q = dec_ref[0:c, :]
    dec_k = dec_ref[c:2 * c, :]
    dec_state = dec_ref[2 * c:2 * c + 1, :]
    dec_in = decin_ref[...]

    def chunk(ci, carry):
        r0 = pl.multiple_of(ci * c, c)
        cos_t = cos_ref[pl.ds(r0, c), :]
        sin_t = sin_ref[pl.ds(r0, c), :]
        q = _rope(p_ref[pl.ds(r0, c), 0:w], cos_t, sin_t)
        k = _rope(p_ref[pl.ds(r0, c), w:2 * w], cos_t, sin_t) * (HEAD_DIM ** -0.5)
        v = p_ref[pl.ds(r0, c), 2 * w:3 * w]
        gp = p_ref[pl.ds(r0, c), 3 * w:4 * w]
        vb = v.astype(BF16)
        a = _dot_nt(_stack_heads(q).astype(BF16), k.astype(BF16)) * dec_in
        o = _unstack_heads(_dot(a.astype(BF16), vb), c)
        st = st_ref[...]
        o = o + _dot_nt(q.astype(BF16), st.astype(BF16)) * dec_q
        st_ref[...] = st * dec_state + bd_mask * _dot_tn(vb, (k * dec_k).astype(BF16))
        o_ref[pl.ds(r0, c), :] = _head_rms(o, gain, ones_bd) * _silu(gp)
        return carry

    lax.fori_loop(0, p_ref.shape[0] // c, chunk, 0)


def _ret_consts():
    log_gamma = np.log1p(-np.exp2(-5.0 - np.arange(N_HEADS, dtype=np.float64)))
    t = np.arange(CHUNK, dtype=np.float64)
    lane_h = np.arange(MIX_WIDTH) // HEAD_DIM
    dec_q = np.exp(log_gamma[lane_h][None, :] * (t[:, None] + 1.0))
    dec_k = np.exp(log_gamma[lane_h][None, :] * (CHUNK - 1.0 - t[:, None]))
    dec_state = np.exp(log_gamma[lane_h] * CHUNK)[None, :]
    dec = np.concatenate([dec_q, dec_k, np.broadcast_to(dec_state, (8, MIX_WIDTH))], 0)
    diff = t[:, None] - t[None, :]
    dec_in = np.concatenate([np.where(diff >= 0, np.exp(log_gamma[h] * diff), 0.0) for h in range(N_HEADS)], 0)
    return dec.astype(np.float32), dec_in.astype(np.float32)


def _ret(proj, cos4, sin4, gain, bsz, seq):
    dec, dec_in = _ret_consts()
    tb = min(REC_BLOCK, seq)
    nb = seq // tb
    return pl.pallas_call(
        _ret_kernel,
        grid=(bsz, nb),
        in_specs=[pl.BlockSpec((tb, 4 * MIX_WIDTH), lambda b, i: (b * nb + i, OFF_R // (4 * MIX_WIDTH))),
                  pl.BlockSpec((tb, MIX_WIDTH), lambda b, i: (i, 0)),
                  pl.BlockSpec((tb, MIX_WIDTH), lambda b, i: (i, 0)),
                  pl.BlockSpec(dec.shape, lambda b, i: (0, 0)),
                  pl.BlockSpec(dec_in.shape, lambda b, i: (0, 0)),
                  pl.BlockSpec((1, MIX_WIDTH), lambda b, i: (0, 0))],
        out_specs=pl.BlockSpec((tb, MIX_WIDTH), lambda b, i: (b * nb + i, 0)),
        out_shape=jax.ShapeDtypeStruct((bsz * seq, MIX_WIDTH), F32),
        scratch_shapes=[pltpu.VMEM((MIX_WIDTH, MIX_WIDTH), F32)],
        compiler_params=_cp("parallel", "arbitrary"),
    )(proj, cos4, sin4, jnp.asarray(dec), jnp.asarray(dec_in), gain.reshape(1, MIX_WIDTH))


def _hgrn_lb_rows(lb):
    lb = lb.astype(F32)
    rows = jnp.stack([jnp.log(lb), jnp.log1p(-lb), 1.0 - lb], 0)
    return jnp.concatenate([rows, jnp.zeros((5, lb.shape[0]), F32)], 0)


def _rope_lane_tables(seq):
    inv = 1.0 / (ROPE_THETA ** (jnp.arange(0, HEAD_DIM, 2, dtype=F32) / HEAD_DIM))
    ang = jnp.arange(seq, dtype=F32)[:, None] * inv[None, :]
    cos, sin = jnp.cos(ang), jnp.sin(ang)
    cos_t = jnp.tile(cos, (1, LANES // (HEAD_DIM // 2)))
    sin_t = jnp.tile(jnp.concatenate([-sin, sin], axis=1), (1, LANES // HEAD_DIM))
    return cos_t, sin_t


def _expand_heads(cols, shape):
    hl = _head_of_lane(shape, 1)
    out = jnp.broadcast_to(cols[-1], shape)
    for h in range(len(cols) - 2, -1, -1):
        out = jnp.where(hl == h, jnp.broadcast_to(cols[h], shape), out)
    return out


def _mlstm_kernel(p_ref, gcol_ref, grow_ref, cw_ref, cb_ref, gbr_ref, gbc_ref, g_ref, o_ref,
                  ct_ref, n_ref, m_ref, hist_ref, cbuf_ref, qk_ref):
    c = CHUNK
    w = MIX_WIDTH
    tb = p_ref.shape[0]

    @pl.when(pl.program_id(1) == 0)
    def _():
        ct_ref[...] = jnp.zeros_like(ct_ref)
        n_ref[...] = jnp.zeros_like(n_ref)
        m_ref[...] = jnp.zeros_like(m_ref)
        hist_ref[...] = jnp.zeros_like(hist_ref)

    cbuf_ref[0:8, :] = hist_ref[...]
    cbuf_ref[8:, :] = p_ref[:, 0:2 * w]
    hist_ref[...] = p_ref[tb - 8:tb, 0:2 * w]
    acc = jnp.broadcast_to(cb_ref[...], (tb, 2 * w))
    for j in range(CONV_W):
        acc = acc + cw_ref[j:j + 1, :] * cbuf_ref[pl.ds(8 - (CONV_W - 1) + j, tb), :]
    qk_ref[...] = _silu(acc)

    ones_bd = _block_ones(w)
    bd_mask = _block_ones(w, F32)
    ti = lax.broadcasted_iota(jnp.int32, (c, c), 0)
    si = lax.broadcasted_iota(jnp.int32, (c, c), 1)
    causal = ti >= si
    tri = causal.astype(BF16)
    tri_t = (ti <= si).astype(BF16)
    gain = g_ref[...]
    ones_ext = jnp.ones((c, LANES), BF16)

    def chunk(ci, carry):
        r0 = pl.multiple_of(ci * c, c)
        q = qk_ref[pl.ds(r0, c), 0:w]
        k = qk_ref[pl.ds(r0, c), w:2 * w] * (HEAD_DIM ** -0.5)
        v = p_ref[pl.ds(r0, c), 2 * w:3 * w]
        op = p_ref[pl.ds(r0, c), 3 * w:4 * w]
        gc = gcol_ref[pl.ds(r0, c), :] + gbr_ref[...]
        gr = grow_ref[ci] + gbc_ref[...]
        b_c = _dot01_l(tri, _log_sigmoid(gc))
        b_r = _dot01_r(_log_sigmoid(gr), tri_t)
        wd, s_inter, em, wk, decay = [], [], [], [], []
        for h in range(N_HEADS):
            bc = b_c[:, N_HEADS + h:N_HEADS + h + 1]
            lic = gc[:, h:h + 1]
            br = b_r[N_HEADS + h:N_HEADS + h + 1, :]
            lir = gr[h:h + 1, :]
            dmat = jnp.where(causal, bc - br + lir, -jnp.inf)
            m_prev = m_ref[h:h + 1, 0:1]
            inter = bc + m_prev
            mrow = jnp.maximum(inter, jnp.max(dmat, axis=1, keepdims=True))
            wd.append(jnp.exp(dmat - mrow))
            s_inter.append(jnp.exp(inter - mrow))
            em.append(jnp.exp(-mrow))
            b_last = br[:, c - 1:c]
            m_new = jnp.maximum(b_last + m_prev, jnp.max(b_last - br + lir, axis=1, keepdims=True))
            wk.append(jnp.exp(b_last - bc + lic - m_new))
            decay.append(jnp.exp(b_last + m_prev - m_new))
            m_ref[h:h + 1, :] = jnp.broadcast_to(m_new, (1, LANES))
        s_inter_l = _expand_heads(s_inter, (c, w))
        em_l = _expand_heads(em, (c, w))
        wk_l = _expand_heads(wk, (c, w))
        decay_l = _expand_heads(decay, (1, w))
        qk = _dot_nt(_stack_heads(q).astype(BF16), k.astype(BF16))
        wmat = jnp.concatenate(wd, axis=0) * qk
        vb = v.astype(BF16)
        r = _dot(wmat.astype(BF16), jnp.concatenate([vb, ones_ext], axis=1))
        num_intra = _unstack_heads(r[:, 0:w], c)
        rs_l = _expand_heads([r[h * c:(h + 1) * c, w:w + 1] for h in range(N_HEADS)], (c, w))
        ct = ct_ref[...]
        nrow = n_ref[0:1, :]
        num = s_inter_l * _dot_nt(q.astype(BF16), ct.astype(BF16)) + num_intra
        den = s_inter_l * _group_sum(q * nrow, ones_bd) + rs_l
        hval = num / jnp.maximum(jnp.abs(den), em_l)
        kw = wk_l * k
        ct_ref[...] = ct * decay_l + bd_mask * _dot_tn(vb, kw.astype(BF16))
        n_ref[0:1, :] = nrow * decay_l + jnp.sum(kw, axis=0, keepdims=True)
        o_ref[pl.ds(r0, c), :] = _head_rms(hval, gain, ones_bd) * _sigmoid(op)
        return carry

    lax.fori_loop(0, tb // c, chunk, 0)


def _mlstm(proj, conv_w, conv_b, gate_b, gain, bsz, seq):
    t = bsz * seq
    w = MIX_WIDTH
    tb = min(REC_BLOCK, seq)
    nb = seq // tb
    ncb = tb // CHUNK
    grow = proj[:, OFF_MG:OFF_MG + 8].reshape(t // CHUNK, CHUNK, 8).transpose(0, 2, 1)
    gb_row = jnp.zeros((1, LANES), F32).at[0, 0:8].set(gate_b.astype(F32))
    gb_col = gate_b.astype(F32).reshape(8, 1)
    return pl.pallas_call(
        _mlstm_kernel,
        grid=(bsz, nb),
        in_specs=[pl.BlockSpec((tb, 4 * w), lambda b, i: (b * nb + i, OFF_M // (4 * w))),
                  pl.BlockSpec((tb, LANES), lambda b, i: (b * nb + i, OFF_MG // LANES)),
                  pl.BlockSpec((ncb, 8, CHUNK), lambda b, i: (b * nb + i, 0, 0)),
                  pl.BlockSpec((CONV_W, 2 * w), lambda b, i: (0, 0)),
                  pl.BlockSpec((1, 2 * w), lambda b, i: (0, 0)),
                  pl.BlockSpec((1, LANES), lambda b, i: (0, 0)),
                  pl.BlockSpec((8, 1), lambda b, i: (0, 0)),
                  pl.BlockSpec((1, w), lambda b, i: (0, 0))],
        out_specs=pl.BlockSpec((tb, w), lambda b, i: (b * nb + i, 0)),
        out_shape=jax.ShapeDtypeStruct((t, w), F32),
        scratch_shapes=[pltpu.VMEM((w, w), F32), pltpu.VMEM((8, w), F32), pltpu.VMEM((8, LANES), F32),
                        pltpu.VMEM((8, 2 * w), F32), pltpu.VMEM((tb + 8, 2 * w), F32),
                        pltpu.VMEM((tb, 2 * w), F32)],
        compiler_params=_cp("parallel", "arbitrary"),
    )(proj, proj, grow, conv_w.astype(F32), conv_b.astype(F32).reshape(1, 2 * w), gb_row, gb_col,
      gain.reshape(1, w))


NSA_TQ = 128
GW = 2 * HEAD_DIM


def _nsa_prep_kernel(pq_ref, pk_ref, cos_ref, sin_ref, qg_ref, kg_ref, qn_ref, qr_ref, ks_ref, kw_ref):
    w = MIX_WIDTH
    ones_bd = _block_ones(w)
    cos_t, sin_t = cos_ref[...], sin_ref[...]
    scale = HEAD_DIM ** -0.5
    qh = _head_rms(pq_ref[...], qg_ref[...], ones_bd)
    qn_ref[...] = (qh * scale).astype(BF16)
    qr_ref[...] = (_rope(qh, cos_t, sin_t) * scale).astype(BF16)
    ks_ref[...] = _rope(_head_rms(pk_ref[:, 0:w], kg_ref[1:2, :], ones_bd), cos_t, sin_t).astype(BF16)
    kw_ref[...] = _rope(_head_rms(pk_ref[:, w:2 * w], kg_ref[2:3, :], ones_bd), cos_t, sin_t).astype(BF16)


def _nsa_prep(proj, cos4, sin4, qnorm_g, knorm_g, bsz, seq, tm=512):
    t = bsz * seq
    w = MIX_WIDTH
    tm = min(tm, seq)
    ns = seq // tm
    qg = jnp.tile(qnorm_g.astype(F32), w // HEAD_DIM).reshape(1, w)
    kg = jnp.concatenate([jnp.tile(knorm_g.astype(F32), (1, w // HEAD_DIM)), jnp.zeros((5, w), F32)], axis=0)
    out = jax.ShapeDtypeStruct((t, w), BF16)
    row = pl.BlockSpec((tm, w), lambda i: (i, 0))
    return pl.pallas_call(
        _nsa_prep_kernel,
        grid=(t // tm,),
        in_specs=[pl.BlockSpec((tm, w), lambda i: (i, OFF_NQ // w)),
                  pl.BlockSpec((tm, 2 * w), lambda i: (i, OFF_KD // (2 * w))),
                  pl.BlockSpec((tm, w), lambda i: (i % ns, 0)),
                  pl.BlockSpec((tm, w), lambda i: (i % ns, 0)),
                  pl.BlockSpec((1, w), lambda i: (0, 0)),
                  pl.BlockSpec((8, w), lambda i: (0, 0))],
        out_specs=[row, row, row, row],
        out_shape=[out, out, out, out],
        compiler_params=_cp("parallel"),
    )(proj, proj, cos4, sin4, qg, kg)


def _nsa_cmp_kernel(xr_ref, pe_ref, w0_ref, w1_ref, kg_ref, ovt_ref, qn_ref, ocmp_ref, sel_ref,
                    kc_ref, vc_ref, v_ref, *, n_top):
    tq = qn_ref.shape[0]
    nr = xr_ref.shape[0]
    nsel = sel_ref.shape[2]
    w = MIX_WIDTH

    @pl.when(pl.program_id(1) == 0)
    def _():
        xr = xr_ref[...]
        y0 = _dot((xr + pe_ref[0]).astype(BF16), w0_ref[...])
        y1 = _dot((xr + pe_ref[1]).astype(BF16), w1_ref[...])
        kv = y0 + pltpu.roll(y1, nr - 1, 0)
        kc_ref[...] = _head_rms(kv[:, 0:w], kg_ref[...], _block_ones(w)).astype(BF16)
        vc_ref[...] = kv[:, w:2 * w].astype(BF16)

    pos0 = pl.program_id(1) * tq
    hl = _head_of_lane((tq, GW), 1)
    pos_r = pos0 + lax.broadcasted_iota(jnp.int32, (tq, nr), 0)
    valid = lax.broadcasted_iota(jnp.int32, (tq, nr), 1) * CMP_STRIDE + (CMP_LEN - 1) <= pos_r
    pos_c = pos0 + lax.broadcasted_iota(jnp.int32, (nr, tq), 1)
    valid_t = lax.broadcasted_iota(jnp.int32, (nr, tq), 0) * CMP_STRIDE + (CMP_LEN - 1) <= pos_c
    jrow = lax.broadcasted_iota(jnp.int32, (nsel, tq), 0)
    cur = (pos0 + lax.broadcasted_iota(jnp.int32, (nsel, tq), 1)) // SEL_BLOCK
    forced = (jrow == 0) | (jrow == cur) | (jrow == cur - 1)
    ovt = ovt_ref[...]

    for g in range(NSA_GROUPS):
        qg = qn_ref[:, g * GW:(g + 1) * GW]
        kg = kc_ref[:, g * GW:(g + 1) * GW]
        vg = vc_ref[:, g * GW:(g + 1) * GW]
        o_g = jnp.zeros((tq, GW), F32)
        pt_sum = jnp.zeros((nr, tq), F32)
        for hh in range(2):
            qm = jnp.where(hl == hh, qg, jnp.zeros_like(qg))
            s = jnp.where(valid, _dot_nt(qm, kg), NEG_INF)
            e = jnp.exp(s - jnp.max(s, axis=1, keepdims=True))
            p = jnp.where(valid, e / jnp.sum(e, axis=1, keepdims=True), 0.0)
            o_g = jnp.where(hl == hh, _dot(p.astype(BF16), vg), o_g)
            st = jnp.where(valid_t, _dot_nt(kg, qm), NEG_INF)
            et = jnp.exp(st - jnp.max(st, axis=0, keepdims=True))
            pt_sum = pt_sum + jnp.where(valid_t, et / jnp.sum(et, axis=0, keepdims=True), 0.0)
        ocmp_ref[:, g * GW:(g + 1) * GW] = o_g
        p_hi = pt_sum.astype(BF16)
        p_lo = (pt_sum - p_hi.astype(F32)).astype(BF16)
        imp = _dot(ovt, p_hi) + _dot(ovt, p_lo)
        val = jnp.where(jrow <= cur, imp + FORCE_BONUS * forced.astype(F32), NEG_INF)
        v_ref[...] = val

        def rank(jp, cnt):
            row = v_ref[pl.ds(jp, 1), :]
            tie = jnp.where(jrow > jp, 1.0, 0.0)
            return cnt + jnp.where(row > val, 1.0, jnp.where(row == val, tie, 0.0))

        cnt = lax.fori_loop(0, nsel, rank, jnp.zeros((nsel, tq), F32))
        sel_ref[0, g] = ((cnt < n_top) & (jrow <= cur)).astype(F32)


def _nsa_cmp_weights(cmp_pe, cmp_w):
    half = CMP_LEN // 2
    wl = cmp_w.astype(F32).reshape(2, 2, half, HEAD_DIM, HEAD_DIM)
    eye2 = jnp.eye(2, dtype=F32)
    w2 = jnp.einsum('kardz,kK,gG,h->arkgdKGhz', wl, eye2, eye2, jnp.ones((2,), F32))
    w2 = w2.reshape(2, half * 4 * HEAD_DIM, 8 * HEAD_DIM)
    pl_ = cmp_pe.astype(F32).reshape(2, 2, half, HEAD_DIM)
    pe2 = jnp.broadcast_to(pl_.transpose(1, 2, 0, 3)[:, :, :, None, :], (2, half, 2, 2, HEAD_DIM))
    return w2.astype(BF16), pe2.reshape(2, 1, half * 4 * HEAD_DIM)


def _nsa_cmp(kcvc, qn, cmp_pe, cmp_w, knorm0, bsz, seq, tq=512):
    t = bsz * seq
    w = MIX_WIDTH
    tq = min(tq, seq)
    nq = seq // tq
    nr = seq // CMP_STRIDE
    nsel = seq // SEL_BLOCK
    n_top = min(SEL_TOPK, nsel)
    w2, pe2 = _nsa_cmp_weights(cmp_pe, cmp_w)
    xr = kcvc.reshape(t // CMP_STRIDE, CMP_STRIDE * w)
    kg = jnp.tile(knorm0.astype(F32), w // HEAD_DIM).reshape(1, w)
    n_i = np.arange(nr)[:, None] * CMP_STRIDE
    j_i = np.arange(nsel)[None, :] * SEL_BLOCK
    ov = ((n_i < j_i + SEL_BLOCK) & (n_i + CMP_LEN > j_i)).astype(np.float32)
    ov[nr - 1, :] = 0.0
    kin = CMP_STRIDE * w
    return pl.pallas_call(
        functools.partial(_nsa_cmp_kernel, n_top=n_top),
        grid=(bsz, nq),
        in_specs=[pl.BlockSpec((nr, kin), lambda b, i: (b, 0)),
                  pl.BlockSpec((2, 1, kin), lambda b, i: (0, 0, 0)),
                  pl.BlockSpec((None, kin, 2 * w), lambda b, i: (0, 0, 0)),
                  pl.BlockSpec((None, kin, 2 * w), lambda b, i: (1, 0, 0)),
                  pl.BlockSpec((1, w), lambda b, i: (0, 0)),
                  pl.BlockSpec((nsel, nr), lambda b, i: (0, 0)),
                  pl.BlockSpec((tq, w), lambda b, i: (b * nq + i, 0))],
        out_specs=[pl.BlockSpec((tq, w), lambda b, i: (b * nq + i, 0)),
                   pl.BlockSpec((1, NSA_GROUPS, nsel, tq), lambda b, i: (b, 0, 0, i))],
        out_shape=[jax.ShapeDtypeStruct((t, w), F32),
                   jax.ShapeDtypeStruct((bsz, NSA_GROUPS, nsel, seq), F32)],
        scratch_shapes=[pltpu.VMEM((nr, w), BF16), pltpu.VMEM((nr, w), BF16), pltpu.VMEM((nsel, tq), F32)],
        compiler_params=_cp("parallel", "arbitrary"),
    )(xr, pe2, w2, w2, kg, jnp.asarray(ov.T, BF16), qn)


def _nsa_attn_kernel(qr_ref, ks_ref, kw_ref, vs_ref, vw_ref, sel_ref, ocmp_ref, gate_ref, o_ref):
    tq = qr_ref.shape[0]
    tk = tq
    i = pl.program_id(2)
    g = pl.program_id(1)
    hl = _head_of_lane((tq, GW), 1)
    q = qr_ref[...]
    qs = jnp.concatenate([jnp.where(hl == 0, q, jnp.zeros_like(q)), jnp.where(hl == 1, q, jnp.zeros_like(q))], axis=0)
    qpos = i * tq + lax.broadcasted_iota(jnp.int32, (tk, 2 * tq), 1) % tq
    krow = lax.broadcasted_iota(jnp.int32, (tk, 2 * tq), 0)
    nb = tk // SEL_BLOCK

    def flash(k_ref, vt_ref, lo, hi, mask_fn):
        def body(kt, carry):
            m, l, acc = carry
            k0 = pl.multiple_of(kt * tk, tk)
            st = _dot_nt(k_ref[pl.ds(k0, tk), :], qs)
            msk = mask_fn(kt, k0 + krow)
            st = jnp.where(msk, st, NEG_INF)
            m_new = jnp.maximum(m, jnp.max(st, axis=0, keepdims=True))
            p = jnp.where(msk, jnp.exp(st - m_new), 0.0)
            alpha = jnp.exp(m - m_new)
            l = l * alpha + jnp.sum(p, axis=0, keepdims=True)
            acc = acc * alpha + _dot(vt_ref[0, 0, kt], p.astype(BF16))
            return m_new, l, acc

        init = (jnp.full((1, 2 * tq), NEG_INF, F32), jnp.zeros((1, 2 * tq), F32), jnp.zeros((GW, 2 * tq), F32))
        m, l, acc = lax.fori_loop(lo, hi, body, init)
        ot = (acc / l).T
        return jnp.where(hl == 0, ot[0:tq, :], ot[tq:2 * tq, :])

    def sel_mask(kt, kpos):
        rows = []
        for r in range(nb):
            srow = sel_ref[0, 0, pl.ds(kt * nb + r, 1), :]
            rows.append(jnp.broadcast_to(jnp.concatenate([srow, srow], axis=1), (SEL_BLOCK, 2 * tq)))
        return (jnp.concatenate(rows, axis=0) > 0.5) & (kpos <= qpos)

    def win_mask(kt, kpos):
        return (kpos <= qpos) & (kpos > qpos - WINDOW)

    o_sel = flash(ks_ref, vs_ref, 0, i + 1, sel_mask)
    o_win = flash(kw_ref, vw_ref, jnp.maximum(i - WINDOW // tk, 0), i + 1, win_mask)
    gb = _sigmoid(gate_ref[...])

    def gate(branch):
        cols = []
        for hh in range(2):
            c0 = gb[:, hh * 3 + branch:hh * 3 + branch + 1]
            c1 = gb[:, (2 + hh) * 3 + branch:(2 + hh) * 3 + branch + 1]
            cols.append(jnp.where(g == 0, c0, c1))
        return _expand_heads(cols, (tq, GW))

    o_ref[...] = gate(0) * ocmp_ref[...] + gate(1) * o_sel + gate(2) * o_win


def _nsa_attn(proj, qr, ks, kw, sel, o_cmp, bsz, seq):
    t = bsz * seq
    w = MIX_WIDTH
    tq = min(NSA_TQ, seq)
    nq = seq // tq
    nsel = seq // SEL_BLOCK

    def vt(off):
        v = proj[:, off:off + GW].astype(BF16).reshape(bsz, nq, tq, NSA_GROUPS, HEAD_DIM)
        v = v.transpose(0, 3, 1, 4, 2)
        return jnp.concatenate([v, v], axis=3)

    kspec = pl.BlockSpec((seq, GW), lambda b, g, i: (b, g))
    vspec = pl.BlockSpec((1, 1, nq, GW, tq), lambda b, g, i: (b, g, 0, 0, 0))
    return pl.pallas_call(
        _nsa_attn_kernel,
        grid=(bsz, NSA_GROUPS, nq),
        in_specs=[pl.BlockSpec((tq, GW), lambda b, g, i: (b * nq + i, g)),
                  kspec, kspec, vspec, vspec,
                  pl.BlockSpec((1, 1, nsel, tq), lambda b, g, i: (b, g, 0, i)),
                  pl.BlockSpec((tq, GW), lambda b, g, i: (b * nq + i, g)),
                  pl.BlockSpec((tq, LANES), lambda b, g, i: (b * nq + i, OFF_NG // LANES))],
        out_specs=pl.BlockSpec((tq, GW), lambda b, g, i: (b * nq + i, g)),
        out_shape=jax.ShapeDtypeStruct((t, w), F32),
        compiler_params=_cp("parallel", "parallel", "arbitrary"),
    )(qr, ks, kw, vt(OFF_V), vt(OFF_V + GW), sel, o_cmp, proj)


def _nsa(proj, kcvc, cos4, sin4, qnorm_g, knorm_g, cmp_pe, cmp_w, bsz, seq):
    qn, qr, ks, kw = _nsa_prep(proj, cos4, sin4, qnorm_g, knorm_g, bsz, seq)
    o_cmp, sel = _nsa_cmp(kcvc, qn, cmp_pe, cmp_w, knorm_g[0], bsz, seq)
    return _nsa_attn(proj, qr, ks, kw, sel, o_cmp, bsz, seq)


PEER_TT = 128
PEER_CT = 8
HALF_D = 512


def _top_rows(s, n_out, row_iota, payload=None):
    vals, picks = [], []
    big = s.shape[0]
    for _ in range(n_out):
        m = jnp.max(s, axis=0, keepdims=True)
        idx = jnp.min(jnp.where(s == m, row_iota, big), axis=0, keepdims=True)
        hit = row_iota == idx
        vals.append(m)
        picks.append(idx if payload is None else jnp.sum(jnp.where(hit, payload, 0), axis=0, keepdims=True))
        s = jnp.where(hit, -jnp.inf, s)
    return vals, picks


def _peer_pairs():
    return [(a, b) for a in range(PEER_TOPK) for b in range(PEER_TOPK) if (a + 1) * (b + 1) <= PEER_TOPK]


def _peer_route_kernel(q_ref, key_ref, e_ref, g_ref, v2_ref, i2_ref, cand_ref, ce_ref):
    tt = q_ref.shape[0]
    nk = PEER_NKEYS
    row_iota = lax.broadcasted_iota(jnp.int32, (nk, tt), 0)
    tops = []
    for p in range(2):
        st = _dot_nt(key_ref[0, p], q_ref[:, p * PEER_KDIM:(p + 1) * PEER_KDIM])
        tops.append(_top_rows(st, PEER_TOPK, row_iota))
    (v1, i1), (v2, i2) = tops
    for b in range(PEER_TOPK):
        v2_ref[b:b + 1, :] = v2[b]
        i2_ref[b:b + 1, :] = i2[b]
    pairs = _peer_pairs()
    n_cand = cand_ref.shape[0]
    cand_ref[...] = jnp.full(cand_ref.shape, -jnp.inf, F32)
    ce_ref[...] = jnp.zeros(ce_ref.shape, jnp.int32)
    off = 0
    for a in range(PEER_TOPK):
        nb = sum(1 for (aa, _) in pairs if aa == a)
        cand_ref[off:off + nb, :] = v1[a] + v2_ref[0:nb, :]
        ce_ref[off:off + nb, :] = i1[a] * nk + i2_ref[0:nb, :]
        off += nb
    crow = lax.broadcasted_iota(jnp.int32, (n_cand, tt), 0)
    vals, es = _top_rows(cand_ref[...], PEER_TOPK, crow, payload=ce_ref[...])
    ex = [jnp.exp(v - vals[0]) for v in vals]
    tot = ex[0]
    for k in range(1, PEER_TOPK):
        tot = tot + ex[k]
    for k in range(PEER_TOPK):
        e_ref[0, k:k + 1, :] = es[k]
        g_ref[0, k:k + 1, :] = ex[k] / tot


def _peer_route(qp, keys):
    t = qp.shape[0]
    tt = min(PEER_TT, t)
    n_cand = -(-len(_peer_pairs()) // 8) * 8
    return pl.pallas_call(
        _peer_route_kernel,
        grid=(t // tt, PEER_HEADS),
        in_specs=[pl.BlockSpec((tt, 2 * PEER_KDIM), lambda i, h: (i, h)),
                  pl.BlockSpec((1, 2, PEER_NKEYS, PEER_KDIM), lambda i, h: (h, 0, 0, 0))],
        out_specs=[pl.BlockSpec((1, PEER_TOPK, tt), lambda i, h: (h, 0, i)),
                   pl.BlockSpec((1, PEER_TOPK, tt), lambda i, h: (h, 0, i))],
        out_shape=[jax.ShapeDtypeStruct((PEER_HEADS, PEER_TOPK, t), jnp.int32),
                   jax.ShapeDtypeStruct((PEER_HEADS, PEER_TOPK, t), F32)],
        scratch_shapes=[pltpu.VMEM((PEER_TOPK, tt), F32), pltpu.VMEM((PEER_TOPK, tt), jnp.int32),
                        pltpu.VMEM((n_cand, tt), F32), pltpu.VMEM((n_cand, tt), jnp.int32)],
        compiler_params=_cp("parallel", "arbitrary"),
    )(qp, keys)


def _pack_rows(tab):
    b = lax.bitcast_convert_type(tab.astype(BF16), jnp.uint16).astype(jnp.uint32)
    h = tab.shape[1] // 2
    return lax.bitcast_convert_type(b[:, :h] | (b[:, h:] << 16), jnp.int32)


def _unpack_rows(wd):
    lo = lax.bitcast_convert_type(lax.shift_left(wd, 16), F32)
    hi = lax.bitcast_convert_type(lax.bitwise_and(wd, jnp.int32(-65536)), F32)
    return lo, hi


SC_WINDOW = 32


def _sc_gather(table, idx):
    from jax.experimental.pallas import tpu_sc as plsc
    n = idx.shape[0]
    width = table.shape[1]
    mesh = plsc.VectorSubcoreMesh(core_axis_name="core", subcore_axis_name="subcore")

    @functools.partial(pl.kernel, out_type=jax.ShapeDtypeStruct((n, width), table.dtype), mesh=mesh)
    def gather(tab_hbm, idx_hbm, out_hbm):
        def body(idx_vmem, out_vmem):
            pltpu.sync_copy(tab_hbm.at[idx_vmem.at[0, pl.ds(0, SC_WINDOW)]], out_vmem)

        pltpu.emit_pipeline(
            body,
            grid=(n // SC_WINDOW,),
            in_specs=[pl.BlockSpec((1, LANES), lambda i: (0, i))],
            out_specs=[pl.BlockSpec((SC_WINDOW, width), lambda i: (i, 0))],
            core_axis_name=("core", "subcore"),
            dimension_semantics=(pltpu.PARALLEL,),
        )(idx_hbm, out_hbm)

    idx_pad = jnp.pad(idx.reshape(n // SC_WINDOW, SC_WINDOW), ((0, 0), (0, LANES - SC_WINDOW)))
    return gather(table, idx_pad.reshape(1, (n // SC_WINDOW) * LANES))


def _peer_combine_kernel(x_ref, g2_ref, rows_ref, gate_ref, o_ref):
    ne = PEER_HEADS * PEER_TOPK
    x = x_ref[...]
    xn = x * lax.rsqrt(jnp.mean(x * x, axis=-1, keepdims=True) + NORM_EPS) * g2_ref[...]
    for j in range(x.shape[0]):
        u_lo, u_hi = _unpack_rows(rows_ref[j * ne:(j + 1) * ne, 0:HALF_D])
        xr = xn[j:j + 1, :]
        h = jnp.sum(u_lo * xr[:, 0:HALF_D] + u_hi * xr[:, HALF_D:2 * HALF_D], axis=1, keepdims=True)
        act = 0.5 * h * (1.0 + lax.erf(h * (2.0 ** -0.5)))
        wgt = gate_ref[0, :, j:j + 1] * act
        v_lo, v_hi = _unpack_rows(rows_ref[j * ne:(j + 1) * ne, HALF_D:2 * HALF_D])
        o_ref[j:j + 1, 0:HALF_D] = x[j:j + 1, 0:HALF_D] + jnp.sum(wgt * v_lo, axis=0, keepdims=True)
        o_ref[j:j + 1, HALF_D:2 * HALF_D] = x[j:j + 1, HALF_D:2 * HALF_D] + jnp.sum(wgt * v_hi, axis=0, keepdims=True)


def _peer_combine(x, g2, rows, gates):
    t, d = x.shape
    ne = PEER_HEADS * PEER_TOPK
    ct = PEER_CT
    return pl.pallas_call(
        _peer_combine_kernel,
        grid=(t // ct,),
        in_specs=[pl.BlockSpec((ct, d), lambda i: (i, 0)),
                  pl.BlockSpec((1, d), lambda i: (0, 0)),
                  pl.BlockSpec((ct * ne, d), lambda i: (i, 0)),
                  pl.BlockSpec((1, ne, ct), lambda i: (i, 0, 0))],
        out_specs=pl.BlockSpec((ct, d), lambda i: (i, 0)),
        out_shape=jax.ShapeDtypeStruct((t, d), F32),
        compiler_params=_cp("parallel"),
    )(x, g2.reshape(1, d), rows, gates)


PEER_TOKENS_PER_GATHER = 4096


def _peer(x, g2, wq, keys, u_tab, v_tab, gather_fn):
    t, d = x.shape
    ne = PEER_HEADS * PEER_TOPK
    qp = _norm_matmul(x, g2, wq.astype(BF16), out_dtype=BF16)
    keys_b = keys.astype(BF16)
    e_t, g_t = _peer_route(qp, keys_b)
    idx = e_t.reshape(ne, t).T.reshape(t * ne)
    gates = g_t.reshape(ne, t // PEER_CT, PEER_CT).transpose(1, 0, 2)
    table = jnp.concatenate([_pack_rows(u_tab), _pack_rows(v_tab)], axis=1)
    tc = min(PEER_TOKENS_PER_GATHER, t)
    outs = []
    for c in range(t // tc):
        rows = gather_fn(table, idx[c * tc * ne:(c + 1) * tc * ne])
        outs.append(_peer_combine(x[c * tc:(c + 1) * tc], g2, rows,
                                  gates[c * tc // PEER_CT:(c + 1) * tc // PEER_CT]))
    return jnp.concatenate(outs, axis=0) if len(outs) > 1 else outs[0]


_IN_WIDTHS = (256, 256, 256, 256, 256, 256, 256, 4, 4, 256, 256, 128, 128, 128, 128, 128, 128, 12,
              256, 256, 256, 256)


def _dup_groups(wcols):
    g0, g1 = wcols[:, :HEAD_DIM], wcols[:, HEAD_DIM:]
    return jnp.concatenate([g0, g0, g1, g1], axis=1)


def _layout_w_in(w_in):
    offs = np.cumsum((0,) + _IN_WIDTHS)
    cols = [w_in[:, offs[i]:offs[i + 1]] for i in range(len(_IN_WIDTHS))]
    (hq, hf, hi, hg, mq, mk, mv, mi, mf, mo, nq, nkc, nvc, nks, nvs, nkw, nvw, ng, rq, rk, rv, rg) = cols
    d = w_in.shape[0]
    pad = lambda c, n: jnp.concatenate([c, jnp.zeros((d, n - c.shape[1]), w_in.dtype)], axis=1)
    main = jnp.concatenate([hq, hf, hi, hg, mq, mk, mv, mo, rq, rk, rv, rg,
                            _dup_groups(nks), _dup_groups(nkw), nq, nvs, nvw,
                            pad(jnp.concatenate([mi, mf], axis=1), LANES), pad(ng, LANES)], axis=1)
    assert main.shape[1] == N_MAIN
    kcvc = jnp.concatenate([nkc, nvc], axis=1)
    return main.astype(BF16), kcvc.astype(BF16)


def kernel(x, norm1_g, w_in, hgrn_lb, hgrn_onorm_g, mlstm_conv_w, mlstm_conv_b, mlstm_gate_b, mlstm_onorm_g, nsa_qnorm_g, nsa_knorm_g, nsa_cmp_pe, nsa_cmp_w, ret_onorm_g, w_up, w_gate, w_out, norm2_g, peer_wq, peer_keys, peer_u, peer_v):
    bsz, seq, d = x.shape
    t = bsz * seq
    depth = w_in.shape[0]
    cos_t, sin_t = _rope_lane_tables(seq)
    cos4, sin4 = jnp.tile(cos_t, (1, 2)), jnp.tile(sin_t, (1, 2))
    lb_cum = jnp.cumsum(jax.nn.softmax(hgrn_lb.astype(F32), axis=0), axis=0)
    lb_all = lb_cum - lb_cum[0:1]
    xf = x.reshape(t, d)
    for l in range(depth):
        w_main, w_kcvc = _layout_w_in(w_in[l])
        proj = _norm_matmul(xf, norm1_g[l], w_main)
        gates = _norm_matmul(xf, norm1_g[l], jnp.concatenate(list(w_gate[l]), axis=1).astype(BF16),
                             act="sigmoid", out_dtype=BF16)
        o_h = _hgrn(proj, _hgrn_lb_rows(lb_all[l]), hgrn_onorm_g[l], bsz, seq)
        o_m = _mlstm(proj, mlstm_conv_w[l], mlstm_conv_b[l], mlstm_gate_b[l], mlstm_onorm_g[l], bsz, seq)
        o_r = _ret(proj, cos4, sin4, ret_onorm_g[l], bsz, seq)
        kcvc = _norm_matmul(xf, norm1_g[l], w_kcvc)
        o_n = _nsa(proj, kcvc, cos4, sin4, nsa_qnorm_g[l], nsa_knorm_g[l], nsa_cmp_pe[l], nsa_cmp_w[l], bsz, seq)
        xf = _merge(xf, gates, (o_h, o_m, o_n, o_r), w_up[l].astype(BF16), w_out[l].astype(BF16))
        xf = _peer(xf, norm2_g[l], peer_wq[l], peer_keys[l], peer_u[l], peer_v[l], _sc_gather)
    return xf.reshape(bsz, seq, d)
```

```python
import functools
import math

import numpy as np
import jax
import jax.numpy as jnp
from jax import lax
from jax.experimental import pallas as pl
from jax.experimental.pallas import tpu as pltpu

F32 = jnp.float32
BF16 = jnp.bfloat16

HEAD_DIM = 64
N_HEADS = 4
MIX_WIDTH = N_HEADS * HEAD_DIM
CHUNK = 64
NORM_EPS = 1e-6
NEG_INF = -1e30
ROPE_THETA = 10000.0
CONV_W = 4
NSA_GROUPS = 2
CMP_LEN = 32
CMP_STRIDE = 16
SEL_BLOCK = 64
SEL_TOPK = 16
WINDOW = 512
FORCE_BONUS = 1e3
PEER_HEADS = 8
PEER_NKEYS = 128
PEER_TOPK = 16
PEER_KDIM = 128

LANES = 128
VMEM_LIMIT = 48 * 1024 * 1024

OFF_H, OFF_M, OFF_R, OFF_KD, OFF_NQ, OFF_V, OFF_MG, OFF_NG = 0, 1024, 2048, 3072, 3584, 3840, 4096, 4224
N_MAIN = 4352


def _cp(*sem):
    return pltpu.CompilerParams(dimension_semantics=sem, vmem_limit_bytes=VMEM_LIMIT)


def _dot(a, b):
    return jnp.dot(a, b, preferred_element_type=F32)


def _dot_nt(a, b):
    return lax.dot_general(a, b, (((1,), (1,)), ((), ())), preferred_element_type=F32)


def _dot_tn(a, b):
    return lax.dot_general(a, b, (((0,), (0,)), ((), ())), preferred_element_type=F32)


def _split3(x):
    hi = x.astype(BF16)
    r1 = x - hi.astype(F32)
    mid = r1.astype(BF16)
    lo = (r1 - mid.astype(F32)).astype(BF16)
    return hi, mid, lo


def _dot01_l(m01, x):
    hi, mid, lo = _split3(x)
    return _dot(m01, hi) + _dot(m01, mid) + _dot(m01, lo)


def _dot01_r(x, m01):
    hi, mid, lo = _split3(x)
    return _dot(hi, m01) + _dot(mid, m01) + _dot(lo, m01)


def _head_of_lane(shape, axis):
    return lax.broadcasted_iota(jnp.int32, shape, axis) // HEAD_DIM


def _block_ones(n, dtype=BF16):
    r = lax.broadcasted_iota(jnp.int32, (n, n), 0) // HEAD_DIM
    c = lax.broadcasted_iota(jnp.int32, (n, n), 1) // HEAD_DIM
    return (r == c).astype(dtype)


def _group_sum(x, ones_bd):
    hi = x.astype(BF16)
    lo = (x - hi.astype(F32)).astype(BF16)
    return _dot(hi, ones_bd) + _dot(lo, ones_bd)


def _head_rms(x, gain, ones_bd):
    ms = _group_sum(x * x, ones_bd) * (1.0 / HEAD_DIM)
    return x * lax.rsqrt(ms + NORM_EPS) * gain


def _sigmoid(x):
    return 1.0 / (1.0 + jnp.exp(-x))


def _silu(x):
    return x * _sigmoid(x)


def _log_sigmoid(x):
    return jnp.minimum(x, 0.0) - jnp.log(1.0 + jnp.exp(-jnp.abs(x)))


def _stack_heads(x, n_heads=N_HEADS):
    hl = _head_of_lane(x.shape, 1)
    return jnp.concatenate([jnp.where(hl == h, x, jnp.zeros_like(x)) for h in range(n_heads)], axis=0)


def _unstack_heads(r, c, n_heads=N_HEADS):
    hl = _head_of_lane((c, r.shape[1]), 1)
    out = jnp.zeros((c, r.shape[1]), F32)
    for h in range(n_heads):
        out = jnp.where(hl == h, r[h * c:(h + 1) * c, :], out)
    return out


def _rope(x, cos_t, sin_t):
    n = x.shape[1]
    first = (lax.broadcasted_iota(jnp.int32, x.shape, 1) % HEAD_DIM) < (HEAD_DIM // 2)
    partner = jnp.where(first, pltpu.roll(x, n - HEAD_DIM // 2, 1), pltpu.roll(x, HEAD_DIM // 2, 1))
    return x * cos_t + partner * sin_t


def _norm_matmul_kernel(x_ref, g_ref, w_ref, o_ref, xn_ref, *, act):
    @pl.when(pl.program_id(1) == 0)
    def _():
        x = x_ref[...]
        ms = jnp.mean(x * x, axis=-1, keepdims=True)
        xn_ref[...] = (x * lax.rsqrt(ms + NORM_EPS) * g_ref[...]).astype(BF16)

    y = _dot(xn_ref[...], w_ref[...])
    if act == "sigmoid":
        y = _sigmoid(y)
    o_ref[...] = y.astype(o_ref.dtype)


def _norm_matmul(x, g, w, *, act=None, out_dtype=F32, tm=1024, tn=512):
    t, d = x.shape
    w3 = w if w.ndim == 3 else w[None]
    n_per = w3.shape[2]
    tm = min(tm, t)
    tn = next(c for c in (tn, 256, 128) if n_per % c == 0)
    per = n_per // tn
    n = w3.shape[0] * n_per
    assert t % tm == 0
    return pl.pallas_call(
        functools.partial(_norm_matmul_kernel, act=act),
        grid=(t // tm, n // tn),
        in_specs=[pl.BlockSpec((tm, d), lambda i, j: (i, 0)),
                  pl.BlockSpec((1, d), lambda i, j: (0, 0)),
                  pl.BlockSpec((None, d, tn), lambda i, j: (j // per, 0, j % per))],
        out_specs=pl.BlockSpec((tm, tn), lambda i, j: (i, j)),
        out_shape=jax.ShapeDtypeStruct((t, n), out_dtype),
        scratch_shapes=[pltpu.VMEM((tm, d), BF16)],
        compiler_params=_cp("parallel", "arbitrary"),
        name="norm_matmul",
    )(x, g.reshape(1, d), w3)


def _merge_kernel(x_ref, gate_ref, oh_ref, om_ref, on_ref, or_ref, wup_ref, wout_ref, o_ref):
    d = x_ref.shape[1]
    acc = None
    for m, r in enumerate((oh_ref, om_ref, on_ref, or_ref)):
        up = _dot(r[...].astype(BF16), wup_ref[m])
        term = gate_ref[:, m * d:(m + 1) * d].astype(F32) * up
        acc = term if acc is None else acc + term
    o_ref[...] = x_ref[...] + _dot(acc.astype(BF16), wout_ref[...])


def _merge(x, gates, outs, w_up, w_out, tm=512):
    t, d = x.shape
    tm = min(tm, t)
    mix = pl.BlockSpec((tm, MIX_WIDTH), lambda i: (i, 0))
    return pl.pallas_call(
        _merge_kernel,
        grid=(t // tm,),
        in_specs=[pl.BlockSpec((tm, d), lambda i: (i, 0)),
                  pl.BlockSpec((tm, 4 * d), lambda i: (i, 0)),
                  mix, mix, mix, mix,
                  pl.BlockSpec((4, MIX_WIDTH, d), lambda i: (0, 0, 0)),
                  pl.BlockSpec((d, d), lambda i: (0, 0))],
        out_specs=pl.BlockSpec((tm, d), lambda i: (i, 0)),
        out_shape=jax.ShapeDtypeStruct((t, d), F32),
        compiler_params=_cp("parallel"),
        name="merge",
    )(x, gates, *outs, w_up, w_out)


REC_BLOCK = 256


def _chunk_consts():
    t = lax.broadcasted_iota(jnp.int32, (CHUNK, CHUNK), 0)
    s = lax.broadcasted_iota(jnp.int32, (CHUNK, CHUNK), 1)
    return t, s


def _hgrn_levels():
    t = np.arange(CHUNK)
    rows = []
    masks = []
    h = CHUNK // 2
    while h >= 1:
        ref = (t // (2 * h)) * (2 * h) + h
        p = np.zeros((CHUNK, CHUNK), np.float32)
        p[t, np.minimum(ref, CHUNK - 1)] = 1.0
        rows.append(p)
        same = (t[:, None] // (2 * h)) == (t[None, :] // (2 * h))
        m = same & ((t[:, None] // h) % 2 == 1) & ((t[None, :] // h) % 2 == 0)
        masks.append(m.astype(np.float32))
        h //= 2
    masks.append(np.eye(CHUNK, dtype=np.float32))
    return np.concatenate(rows, 0), np.stack(masks, 0)


def _hgrn_kernel(p_ref, lb_ref, g_ref, psel_ref, lmask_ref, o_ref, st_ref):
    @pl.when(pl.program_id(1) == 0)
    def _():
        st_ref[...] = jnp.zeros_like(st_ref)

    c = CHUNK
    w = MIX_WIDTH
    ones_bd = _block_ones(w)
    bd_mask = _block_ones(w, F32)
    tri = (lax.broadcasted_iota(jnp.int32, (c, c), 0) >= lax.broadcasted_iota(jnp.int32, (c, c), 1)).astype(BF16)
    psel = psel_ref[...]
    n_lv = lmask_ref.shape[0]
    log_lb, log_1mlb, one_mlb = lb_ref[0:1, :], lb_ref[1:2, :], lb_ref[2:3, :]
    gain = g_ref[...]

    def chunk(ci, carry):
        r0 = pl.multiple_of(ci * c, c)
        q = _silu(p_ref[pl.ds(r0, c), 0:w])
        fl = p_ref[pl.ds(r0, c), w:2 * w]
        v = p_ref[pl.ds(r0, c), 2 * w:3 * w]
        gp = p_ref[pl.ds(r0, c), 3 * w:4 * w]
        a1 = jnp.broadcast_to(log_lb, fl.shape)
        a2 = log_1mlb + _log_sigmoid(fl)
        mx = jnp.maximum(a1, a2)
        log_f = mx + jnp.log(jnp.exp(a1 - mx) + jnp.exp(a2 - mx))
        k = one_mlb * _sigmoid(-fl)
        b = _dot01_l(tri, log_f)
        bref = _dot01_l(psel, b)
        vb = v.astype(BF16)
        a = jnp.zeros((N_HEADS * c, c), F32)
        for lv in range(n_lv):
            if lv < n_lv - 1:
                br = bref[lv * c:(lv + 1) * c, :]
                qs = q * jnp.exp(jnp.minimum(b - br, 0.0))
                ks = k * jnp.exp(jnp.minimum(br - b, 0.0))
            else:
                qs, ks = q, k
            s_lv = _dot_nt(_stack_heads(qs).astype(BF16), ks.astype(BF16))
            a = a + jnp.concatenate([lmask_ref[lv]] * N_HEADS, axis=0) * s_lv
        o = _unstack_heads(_dot(a.astype(BF16), vb), c)
        st = st_ref[...]
        o = o + _dot_nt((q * jnp.exp(b)).astype(BF16), st.astype(BF16))
        b_last = b[c - 1:c, :]
        kb = k * jnp.exp(b_last - b)
        st_ref[...] = st * jnp.exp(b_last) + bd_mask * _dot_tn(vb, kb.astype(BF16))
        y = _head_rms(o, gain, ones_bd) * _silu(gp)
        o_ref[pl.ds(r0, c), :] = y
        return carry

    lax.fori_loop(0, p_ref.shape[0] // c, chunk, 0)


def _hgrn(proj, lb_rows, gain, bsz, seq):
    psel, lmask = _hgrn_levels()
    tb = min(REC_BLOCK, seq)
    nb = seq // tb
    return pl.pallas_call(
        _hgrn_kernel,
        grid=(bsz, nb),
        in_specs=[pl.BlockSpec((tb, 4 * MIX_WIDTH), lambda b, i: (b * nb + i, OFF_H // (4 * MIX_WIDTH))),
                  pl.BlockSpec((8, MIX_WIDTH), lambda b, i: (0, 0)),
                  pl.BlockSpec((1, MIX_WIDTH), lambda b, i: (0, 0)),
                  pl.BlockSpec(psel.shape, lambda b, i: (0, 0)),
                  pl.BlockSpec(lmask.shape, lambda b, i: (0, 0, 0))],
        out_specs=pl.BlockSpec((tb, MIX_WIDTH), lambda b, i: (b * nb + i, 0)),
        out_shape=jax.ShapeDtypeStruct((bsz * seq, MIX_WIDTH), F32),
        scratch_shapes=[pltpu.VMEM((MIX_WIDTH, MIX_WIDTH), F32)],
        compiler_params=_cp("parallel", "arbitrary"),
        name="hgrn2",
    )(proj, lb_rows, gain.reshape(1, MIX_WIDTH), jnp.asarray(psel, BF16), jnp.asarray(lmask, F32))


def _ret_kernel(p_ref, cos_ref, sin_ref, dec_ref, decin_ref, g_ref, o_ref, st_ref):
    @pl.when(pl.program_id(1) == 0)
    def _():
        st_ref[...] = jnp.zeros_like(st_ref)

    c = CHUNK
    w = MIX_WIDTH
    ones_bd = _block_ones(w)
    bd_mask = _block_ones(w, F32)
    gain = g_ref[...]
    dec_q = dec_ref[0:c, :]
    dec_k = dec_ref[c:2 * c, :]
    dec_state = dec_ref[2 * c:2 * c + 1, :]
    dec_in = decin_ref[...]

    def chunk(ci, carry):
        r0 = pl.multiple_of(ci * c, c)
        cos_t = cos_ref[pl.ds(r0, c), :]
        sin_t = sin_ref[pl.ds(r0, c), :]
        q = _rope(p_ref[pl.ds(r0, c), 0:w], cos_t, sin_t)
        k = _rope(p_ref[pl.ds(r0, c), w:2 * w], cos_t, sin_t) * (HEAD_DIM ** -0.5)
        v = p_ref[pl.ds(r0, c), 2 * w:3 * w]
        gp = p_ref[pl.ds(r0, c), 3 * w:4 * w]
        vb = v.astype(BF16)
        a = _dot_nt(_stack_heads(q).astype(BF16), k.astype(BF16)) * dec_in
        o = _unstack_heads(_dot(a.astype(BF16), vb), c)
        st = st_ref[...]
        o = o + _dot_nt(q.astype(BF16), st.astype(BF16)) * dec_q
        st_ref[...] = st * dec_state + bd_mask * _dot_tn(vb, (k * dec_k).astype(BF16))
        o_ref[pl.ds(r0, c), :] = _head_rms(o, gain, ones_bd) * _silu(gp)
        return carry

    lax.fori_loop(0, p_ref.shape[0] // c, chunk, 0)


def _ret_consts():
    log_gamma = np.log1p(-np.exp2(-5.0 - np.arange(N_HEADS, dtype=np.float64)))
    t = np.arange(CHUNK, dtype=np.float64)
    lane_h = np.arange(MIX_WIDTH) // HEAD_DIM
    dec_q = np.exp(log_gamma[lane_h][None, :] * (t[:, None] + 1.0))
    dec_k = np.exp(log_gamma[lane_h][None, :] * (CHUNK - 1.0 - t[:, None]))
    dec_state = np.exp(log_gamma[lane_h] * CHUNK)[None, :]
    dec = np.concatenate([dec_q, dec_k, np.broadcast_to(dec_state, (8, MIX_WIDTH))], 0)
    diff = t[:, None] - t[None, :]
    dec_in = np.concatenate([np.where(diff >= 0, np.exp(log_gamma[h] * diff), 0.0) for h in range(N_HEADS)], 0)
    return dec.astype(np.float32), dec_in.astype(np.float32)


def _ret(proj, cos4, sin4, gain, bsz, seq):
    dec, dec_in = _ret_consts()
    tb = min(REC_BLOCK, seq)
    nb = seq // tb
    return pl.pallas_call(
        _ret_kernel,
        grid=(bsz, nb),
        in_specs=[pl.BlockSpec((tb, 4 * MIX_WIDTH), lambda b, i: (b * nb + i, OFF_R // (4 * MIX_WIDTH))),
                  pl.BlockSpec((tb, MIX_WIDTH), lambda b, i: (i, 0)),
                  pl.BlockSpec((tb, MIX_WIDTH), lambda b, i: (i, 0)),
                  pl.BlockSpec(dec.shape, lambda b, i: (0, 0)),
                  pl.BlockSpec(dec_in.shape, lambda b, i: (0, 0)),
                  pl.BlockSpec((1, MIX_WIDTH), lambda b, i: (0, 0))],
        out_specs=pl.BlockSpec((tb, MIX_WIDTH), lambda b, i: (b * nb + i, 0)),
        out_shape=jax.ShapeDtypeStruct((bsz * seq, MIX_WIDTH), F32),
        scratch_shapes=[pltpu.VMEM((MIX_WIDTH, MIX_WIDTH), F32)],
        compiler_params=_cp("parallel", "arbitrary"),
        name="retention",
    )(proj, cos4, sin4, jnp.asarray(dec), jnp.asarray(dec_in), gain.reshape(1, MIX_WIDTH))


def _hgrn_lb_rows(lb):
    lb = lb.astype(F32)
    rows = jnp.stack([jnp.log(lb), jnp.log1p(-lb), 1.0 - lb], 0)
    return jnp.concatenate([rows, jnp.zeros((5, lb.shape[0]), F32)], 0)


def _rope_lane_tables(seq):
    inv = 1.0 / (ROPE_THETA ** (jnp.arange(0, HEAD_DIM, 2, dtype=F32) / HEAD_DIM))
    ang = jnp.arange(seq, dtype=F32)[:, None] * inv[None, :]
    cos, sin = jnp.cos(ang), jnp.sin(ang)
    cos_t = jnp.tile(cos, (1, LANES // (HEAD_DIM // 2)))
    sin_t = jnp.tile(jnp.concatenate([-sin, sin], axis=1), (1, LANES // HEAD_DIM))
    return cos_t, sin_t


def _expand_heads(cols, shape):
    hl = _head_of_lane(shape, 1)
    out = jnp.broadcast_to(cols[-1], shape)
    for h in range(len(cols) - 2, -1, -1):
        out = jnp.where(hl == h, jnp.broadcast_to(cols[h], shape), out)
    return out


def _mlstm_kernel(p_ref, gcol_ref, grow_ref, cw_ref, cb_ref, gbr_ref, gbc_ref, g_ref, o_ref,
                  ct_ref, n_ref, m_ref, hist_ref, cbuf_ref, qk_ref):
    c = CHUNK
    w = MIX_WIDTH
    tb = p_ref.shape[0]

    @pl.when(pl.program_id(1) == 0)
    def _():
        ct_ref[...] = jnp.zeros_like(ct_ref)
        n_ref[...] = jnp.zeros_like(n_ref)
        m_ref[...] = jnp.zeros_like(m_ref)
        hist_ref[...] = jnp.zeros_like(hist_ref)

    cbuf_ref[0:8, :] = hist_ref[...]
    cbuf_ref[8:, :] = p_ref[:, 0:2 * w]
    hist_ref[...] = p_ref[tb - 8:tb, 0:2 * w]
    acc = jnp.broadcast_to(cb_ref[...], (tb, 2 * w))
    for j in range(CONV_W):
        acc = acc + cw_ref[j:j + 1, :] * cbuf_ref[pl.ds(8 - (CONV_W - 1) + j, tb), :]
    qk_ref[...] = _silu(acc)

    ones_bd = _block_ones(w)
    bd_mask = _block_ones(w, F32)
    ti = lax.broadcasted_iota(jnp.int32, (c, c), 0)
    si = lax.broadcasted_iota(jnp.int32, (c, c), 1)
    causal = ti >= si
    tri = causal.astype(BF16)
    tri_t = (ti <= si).astype(BF16)
    gain = g_ref[...]
    ones_ext = jnp.ones((c, LANES), BF16)

    def chunk(ci, carry):
        r0 = pl.multiple_of(ci * c, c)
        q = qk_ref[pl.ds(r0, c), 0:w]
        k = qk_ref[pl.ds(r0, c), w:2 * w] * (HEAD_DIM ** -0.5)
        v = p_ref[pl.ds(r0, c), 2 * w:3 * w]
        op = p_ref[pl.ds(r0, c), 3 * w:4 * w]
        gc = gcol_ref[pl.ds(r0, c), :] + gbr_ref[...]
        gr = grow_ref[ci] + gbc_ref[...]
        b_c = _dot01_l(tri, _log_sigmoid(gc))
        b_r = _dot01_r(_log_sigmoid(gr), tri_t)
        wd, s_inter, em, wk, decay = [], [], [], [], []
        for h in range(N_HEADS):
            bc = b_c[:, N_HEADS + h:N_HEADS + h + 1]
            lic = gc[:, h:h + 1]
            br = b_r[N_HEADS + h:N_HEADS + h + 1, :]
            lir = gr[h:h + 1, :]
            dmat = jnp.where(causal, bc - br + lir, -jnp.inf)
            m_prev = m_ref[h:h + 1, 0:1]
            inter = bc + m_prev
            mrow = jnp.maximum(inter, jnp.max(dmat, axis=1, keepdims=True))
            wd.append(jnp.exp(dmat - mrow))
            s_inter.append(jnp.exp(inter - mrow))
            em.append(jnp.exp(-mrow))
            b_last = br[:, c - 1:c]
            m_new = jnp.maximum(b_last + m_prev, jnp.max(b_last - br + lir, axis=1, keepdims=True))
            wk.append(jnp.exp(b_last - bc + lic - m_new))
            decay.append(jnp.exp(b_last + m_prev - m_new))
            m_ref[h:h + 1, :] = jnp.broadcast_to(m_new, (1, LANES))
        s_inter_l = _expand_heads(s_inter, (c, w))
        em_l = _expand_heads(em, (c, w))
        wk_l = _expand_heads(wk, (c, w))
        decay_l = _expand_heads(decay, (1, w))
        qk = _dot_nt(_stack_heads(q).astype(BF16), k.astype(BF16))
        wmat = jnp.concatenate(wd, axis=0) * qk
        vb = v.astype(BF16)
        r = _dot(wmat.astype(BF16), jnp.concatenate([vb, ones_ext], axis=1))
        num_intra = _unstack_heads(r[:, 0:w], c)
        rs_l = _expand_heads([r[h * c:(h + 1) * c, w:w + 1] for h in range(N_HEADS)], (c, w))
        ct = ct_ref[...]
        nrow = n_ref[0:1, :]
        num = s_inter_l * _dot_nt(q.astype(BF16), ct.astype(BF16)) + num_intra
        den = s_inter_l * _group_sum(q * nrow, ones_bd) + rs_l
        hval = num / jnp.maximum(jnp.abs(den), em_l)
        kw = wk_l * k
        ct_ref[...] = ct * decay_l + bd_mask * _dot_tn(vb, kw.astype(BF16))
        n_ref[0:1, :] = nrow * decay_l + jnp.sum(kw, axis=0, keepdims=True)
        o_ref[pl.ds(r0, c), :] = _head_rms(hval, gain, ones_bd) * _sigmoid(op)
        return carry

    lax.fori_loop(0, tb // c, chunk, 0)


def _mlstm(proj, conv_w, conv_b, gate_b, gain, bsz, seq):
    t = bsz * seq
    w = MIX_WIDTH
    tb = min(REC_BLOCK, seq)
    nb = seq // tb
    ncb = tb // CHUNK
    grow = proj[:, OFF_MG:OFF_MG + 8].reshape(t // CHUNK, CHUNK, 8).transpose(0, 2, 1)
    gb_row = jnp.zeros((1, LANES), F32).at[0, 0:8].set(gate_b.astype(F32))
    gb_col = gate_b.astype(F32).reshape(8, 1)
    return pl.pallas_call(
        _mlstm_kernel,
        grid=(bsz, nb),
        in_specs=[pl.BlockSpec((tb, 4 * w), lambda b, i: (b * nb + i, OFF_M // (4 * w))),
                  pl.BlockSpec((tb, LANES), lambda b, i: (b * nb + i, OFF_MG // LANES)),
                  pl.BlockSpec((ncb, 8, CHUNK), lambda b, i: (b * nb + i, 0, 0)),
                  pl.BlockSpec((CONV_W, 2 * w), lambda b, i: (0, 0)),
                  pl.BlockSpec((1, 2 * w), lambda b, i: (0, 0)),
                  pl.BlockSpec((1, LANES), lambda b, i: (0, 0)),
                  pl.BlockSpec((8, 1), lambda b, i: (0, 0)),
                  pl.BlockSpec((1, w), lambda b, i: (0, 0))],
        out_specs=pl.BlockSpec((tb, w), lambda b, i: (b * nb + i, 0)),
        out_shape=jax.ShapeDtypeStruct((t, w), F32),
        scratch_shapes=[pltpu.VMEM((w, w), F32), pltpu.VMEM((8, w), F32), pltpu.VMEM((8, LANES), F32),
                        pltpu.VMEM((8, 2 * w), F32), pltpu.VMEM((tb + 8, 2 * w), F32),
                        pltpu.VMEM((tb, 2 * w), F32)],
        compiler_params=_cp("parallel", "arbitrary"),
        name="mlstm",
    )(proj, proj, grow, conv_w.astype(F32), conv_b.astype(F32).reshape(1, 2 * w), gb_row, gb_col,
      gain.reshape(1, w))


NSA_TQ = 128
GW = 2 * HEAD_DIM


def _nsa_prep_kernel(pq_ref, pk_ref, cos_ref, sin_ref, qg_ref, kg_ref, qn_ref, qr_ref, ks_ref, kw_ref):
    w = MIX_WIDTH
    ones_bd = _block_ones(w)
    cos_t, sin_t = cos_ref[...], sin_ref[...]
    scale = HEAD_DIM ** -0.5
    qh = _head_rms(pq_ref[...], qg_ref[...], ones_bd)
    qn_ref[...] = (qh * scale).astype(BF16)
    qr_ref[...] = (_rope(qh, cos_t, sin_t) * scale).astype(BF16)
    ks_ref[...] = _rope(_head_rms(pk_ref[:, 0:w], kg_ref[1:2, :], ones_bd), cos_t, sin_t).astype(BF16)
    kw_ref[...] = _rope(_head_rms(pk_ref[:, w:2 * w], kg_ref[2:3, :], ones_bd), cos_t, sin_t).astype(BF16)


def _nsa_prep(proj, cos4, sin4, qnorm_g, knorm_g, bsz, seq, tm=512):
    t = bsz * seq
    w = MIX_WIDTH
    tm = min(tm, seq)
    ns = seq // tm
    qg = jnp.tile(qnorm_g.astype(F32), w // HEAD_DIM).reshape(1, w)
    kg = jnp.concatenate([jnp.tile(knorm_g.astype(F32), (1, w // HEAD_DIM)), jnp.zeros((5, w), F32)], axis=0)
    out = jax.ShapeDtypeStruct((t, w), BF16)
    row = pl.BlockSpec((tm, w), lambda i: (i, 0))
    return pl.pallas_call(
        _nsa_prep_kernel,
        grid=(t // tm,),
        in_specs=[pl.BlockSpec((tm, w), lambda i: (i, OFF_NQ // w)),
                  pl.BlockSpec((tm, 2 * w), lambda i: (i, OFF_KD // (2 * w))),
                  pl.BlockSpec((tm, w), lambda i: (i % ns, 0)),
                  pl.BlockSpec((tm, w), lambda i: (i % ns, 0)),
                  pl.BlockSpec((1, w), lambda i: (0, 0)),
                  pl.BlockSpec((8, w), lambda i: (0, 0))],
        out_specs=[row, row, row, row],
        out_shape=[out, out, out, out],
        compiler_params=_cp("parallel"),
        name="nsa_prep",
    )(proj, proj, cos4, sin4, qg, kg)


def _nsa_cmp_kernel(xr_ref, pe_ref, w0_ref, w1_ref, kg_ref, ovt_ref, qn_ref, ocmp_ref, sel_ref,
                    kc_ref, vc_ref, v_ref, *, n_top):
    tq = qn_ref.shape[0]
    nr = xr_ref.shape[0]
    nsel = sel_ref.shape[2]
    w = MIX_WIDTH

    @pl.when(pl.program_id(1) == 0)
    def _():
        xr = xr_ref[...]
        y0 = _dot((xr + pe_ref[0]).astype(BF16), w0_ref[...])
        y1 = _dot((xr + pe_ref[1]).astype(BF16), w1_ref[...])
        kv = y0 + pltpu.roll(y1, nr - 1, 0)
        kc_ref[...] = _head_rms(kv[:, 0:w], kg_ref[...], _block_ones(w)).astype(BF16)
        vc_ref[...] = kv[:, w:2 * w].astype(BF16)

    pos0 = pl.program_id(1) * tq
    hl = _head_of_lane((tq, GW), 1)
    pos_r = pos0 + lax.broadcasted_iota(jnp.int32, (tq, nr), 0)
    valid = lax.broadcasted_iota(jnp.int32, (tq, nr), 1) * CMP_STRIDE + (CMP_LEN - 1) <= pos_r
    pos_c = pos0 + lax.broadcasted_iota(jnp.int32, (nr, tq), 1)
    valid_t = lax.broadcasted_iota(jnp.int32, (nr, tq), 0) * CMP_STRIDE + (CMP_LEN - 1) <= pos_c
    jrow = lax.broadcasted_iota(jnp.int32, (nsel, tq), 0)
    cur = (pos0 + lax.broadcasted_iota(jnp.int32, (nsel, tq), 1)) // SEL_BLOCK
    forced = (jrow == 0) | (jrow == cur) | (jrow == cur - 1)
    ovt = ovt_ref[...]

    for g in range(NSA_GROUPS):
        qg = qn_ref[:, g * GW:(g + 1) * GW]
        kg = kc_ref[:, g * GW:(g + 1) * GW]
        vg = vc_ref[:, g * GW:(g + 1) * GW]
        o_g = jnp.zeros((tq, GW), F32)
        pt_sum = jnp.zeros((nr, tq), F32)
        for hh in range(2):
            qm = jnp.where(hl == hh, qg, jnp.zeros_like(qg))
            s = jnp.where(valid, _dot_nt(qm, kg), NEG_INF)
            e = jnp.exp(s - jnp.max(s, axis=1, keepdims=True))
            p = jnp.where(valid, e / jnp.sum(e, axis=1, keepdims=True), 0.0)
            o_g = jnp.where(hl == hh, _dot(p.astype(BF16), vg), o_g)
            st = jnp.where(valid_t, _dot_nt(kg, qm), NEG_INF)
            et = jnp.exp(st - jnp.max(st, axis=0, keepdims=True))
            pt_sum = pt_sum + jnp.where(valid_t, et / jnp.sum(et, axis=0, keepdims=True), 0.0)
        ocmp_ref[:, g * GW:(g + 1) * GW] = o_g
        p_hi = pt_sum.astype(BF16)
        p_lo = (pt_sum - p_hi.astype(F32)).astype(BF16)
        imp = _dot(ovt, p_hi) + _dot(ovt, p_lo)
        val = jnp.where(jrow <= cur, imp + FORCE_BONUS * forced.astype(F32), NEG_INF)
        v_ref[...] = val

        def rank(jp, cnt):
            row = v_ref[pl.ds(jp, 1), :]
            tie = jnp.where(jrow > jp, 1.0, 0.0)
            return cnt + jnp.where(row > val, 1.0, jnp.where(row == val, tie, 0.0))

        cnt = lax.fori_loop(0, nsel, rank, jnp.zeros((nsel, tq), F32))
        sel_ref[0, g] = ((cnt < n_top) & (jrow <= cur)).astype(F32)


def _nsa_cmp_weights(cmp_pe, cmp_w):
    half = CMP_LEN // 2
    wl = cmp_w.astype(F32).reshape(2, 2, half, HEAD_DIM, HEAD_DIM)
    eye2 = jnp.eye(2, dtype=F32)
    w2 = jnp.einsum('kardz,kK,gG,h->arkgdKGhz', wl, eye2, eye2, jnp.ones((2,), F32))
    w2 = w2.reshape(2, half * 4 * HEAD_DIM, 8 * HEAD_DIM)
    pl_ = cmp_pe.astype(F32).reshape(2, 2, half, HEAD_DIM)
    pe2 = jnp.broadcast_to(pl_.transpose(1, 2, 0, 3)[:, :, :, None, :], (2, half, 2, 2, HEAD_DIM))
    return w2.astype(BF16), pe2.reshape(2, 1, half * 4 * HEAD_DIM)


def _nsa_cmp(kcvc, qn, cmp_pe, cmp_w, knorm0, bsz, seq, tq=512):
    t = bsz * seq
    w = MIX_WIDTH
    tq = min(tq, seq)
    nq = seq // tq
    nr = seq // CMP_STRIDE
    nsel = seq // SEL_BLOCK
    n_top = min(SEL_TOPK, nsel)
    w2, pe2 = _nsa_cmp_weights(cmp_pe, cmp_w)
    xr = kcvc.reshape(t // CMP_STRIDE, CMP_STRIDE * w)
    kg = jnp.tile(knorm0.astype(F32), w // HEAD_DIM).reshape(1, w)
    n_i = np.arange(nr)[:, None] * CMP_STRIDE
    j_i = np.arange(nsel)[None, :] * SEL_BLOCK
    ov = ((n_i < j_i + SEL_BLOCK) & (n_i + CMP_LEN > j_i)).astype(np.float32)
    ov[nr - 1, :] = 0.0
    kin = CMP_STRIDE * w
    return pl.pallas_call(
        functools.partial(_nsa_cmp_kernel, n_top=n_top),
        grid=(bsz, nq),
        in_specs=[pl.BlockSpec((nr, kin), lambda b, i: (b, 0)),
                  pl.BlockSpec((2, 1, kin), lambda b, i: (0, 0, 0)),
                  pl.BlockSpec((None, kin, 2 * w), lambda b, i: (0, 0, 0)),
                  pl.BlockSpec((None, kin, 2 * w), lambda b, i: (1, 0, 0)),
                  pl.BlockSpec((1, w), lambda b, i: (0, 0)),
                  pl.BlockSpec((nsel, nr), lambda b, i: (0, 0)),
                  pl.BlockSpec((tq, w), lambda b, i: (b * nq + i, 0))],
        out_specs=[pl.BlockSpec((tq, w), lambda b, i: (b * nq + i, 0)),
                   pl.BlockSpec((1, NSA_GROUPS, nsel, tq), lambda b, i: (b, 0, 0, i))],
        out_shape=[jax.ShapeDtypeStruct((t, w), F32),
                   jax.ShapeDtypeStruct((bsz, NSA_GROUPS, nsel, seq), F32)],
        scratch_shapes=[pltpu.VMEM((nr, w), BF16), pltpu.VMEM((nr, w), BF16), pltpu.VMEM((nsel, tq), F32)],
        compiler_params=_cp("parallel", "arbitrary"),
        name="nsa_cmp",
    )(xr, pe2, w2, w2, kg, jnp.asarray(ov.T, BF16), qn)


NSA_KC = 512


def _nsa_attn_kernel(qr_ref, ks_ref, kw_ref, vs_ref, vw_ref, sel_ref, ocmp_ref, gate_ref, o_ref, *, kc, wt):
    tq = qr_ref.shape[0]
    i = pl.program_id(2)
    g = pl.program_id(1)
    hl = _head_of_lane((tq, GW), 1)
    q = qr_ref[...]
    qs = jnp.concatenate([jnp.where(hl == 0, q, jnp.zeros_like(q)), jnp.where(hl == 1, q, jnp.zeros_like(q))], axis=0)
    nbk = kc // SEL_BLOCK

    def lane_qpos(rows):
        return i * tq + lax.broadcasted_iota(jnp.int32, (rows, 2 * tq), 1) % tq

    def finish(acc, l):
        ot = (acc / l).T
        return jnp.where(hl == 0, ot[0:tq, :], ot[tq:2 * tq, :])

    qpos_s = lane_qpos(kc)
    krow_s = lax.broadcasted_iota(jnp.int32, (kc, 2 * tq), 0)

    def sel_body(c, carry):
        m, l, acc = carry
        k0 = pl.multiple_of(c * kc, kc)
        st = _dot_nt(ks_ref[pl.ds(k0, kc), :], qs)
        srows = sel_ref[0, 0, pl.ds(pl.multiple_of(c * nbk, nbk), nbk), :]
        srows = jnp.concatenate([srows, srows], axis=1)
        smask = jnp.concatenate([jnp.broadcast_to(srows[r:r + 1, :], (SEL_BLOCK, 2 * tq)) for r in range(nbk)],
                                axis=0)
        msk = (smask > 0.5) & (k0 + krow_s <= qpos_s)
        st = jnp.where(msk, st, NEG_INF)
        m_new = jnp.maximum(m, jnp.max(st, axis=0, keepdims=True))
        p = jnp.where(msk, jnp.exp(st - m_new), 0.0)
        alpha = jnp.exp(m - m_new)
        l = l * alpha + jnp.sum(p, axis=0, keepdims=True)
        acc = acc * alpha + _dot(vs_ref[0, 0, c], p.astype(BF16))
        return m_new, l, acc

    init = (jnp.full((1, 2 * tq), NEG_INF, F32), jnp.zeros((1, 2 * tq), F32), jnp.zeros((GW, 2 * tq), F32))
    _, l_s, acc_s = lax.fori_loop(0, ((i + 1) * tq + kc - 1) // kc, sel_body, init)
    o_sel = finish(acc_s, l_s)

    j0 = jnp.maximum(i - (wt - 1), 0)
    k0 = pl.multiple_of(j0 * tq, tq)
    span = wt * tq
    st = _dot_nt(kw_ref[pl.ds(k0, span), :], qs)
    kpos = k0 + lax.broadcasted_iota(jnp.int32, (span, 2 * tq), 0)
    qpos_w = lane_qpos(span)
    msk = (kpos <= qpos_w) & (kpos > qpos_w - WINDOW)
    st = jnp.where(msk, st, NEG_INF)
    p = jnp.where(msk, jnp.exp(st - jnp.max(st, axis=0, keepdims=True)), 0.0)
    vt = jnp.concatenate([vw_ref[0, 0, j0 + r] for r in range(wt)], axis=1)
    o_win = finish(_dot(vt, p.astype(BF16)), jnp.sum(p, axis=0, keepdims=True))

    gb = _sigmoid(gate_ref[...])

    def gate(branch):
        cols = []
        for hh in range(2):
            c0 = gb[:, hh * 3 + branch:hh * 3 + branch + 1]
            c1 = gb[:, (2 + hh) * 3 + branch:(2 + hh) * 3 + branch + 1]
            cols.append(jnp.where(g == 0, c0, c1))
        return _expand_heads(cols, (tq, GW))

    o_ref[...] = gate(0) * ocmp_ref[...] + gate(1) * o_sel + gate(2) * o_win


def _nsa_attn(proj, qr, ks, kw, sel, o_cmp, bsz, seq):
    t = bsz * seq
    w = MIX_WIDTH
    tq = min(NSA_TQ, seq)
    nq = seq // tq
    nsel = seq // SEL_BLOCK
    kc = min(NSA_KC, seq)
    wt = min(WINDOW // tq + 1, nq)

    def vt(off, tk):
        v = proj[:, off:off + GW].astype(BF16).reshape(bsz, seq // tk, tk, NSA_GROUPS, HEAD_DIM)
        v = v.transpose(0, 3, 1, 4, 2)
        return jnp.concatenate([v, v], axis=3)

    kspec = pl.BlockSpec((seq, GW), lambda b, g, i: (b, g))
    return pl.pallas_call(
        functools.partial(_nsa_attn_kernel, kc=kc, wt=wt),
        grid=(bsz, NSA_GROUPS, nq),
        in_specs=[pl.BlockSpec((tq, GW), lambda b, g, i: (b * nq + i, g)),
                  kspec, kspec,
                  pl.BlockSpec((1, 1, seq // kc, GW, kc), lambda b, g, i: (b, g, 0, 0, 0)),
                  pl.BlockSpec((1, 1, nq, GW, tq), lambda b, g, i: (b, g, 0, 0, 0)),
                  pl.BlockSpec((1, 1, nsel, tq), lambda b, g, i: (b, g, 0, i)),
                  pl.BlockSpec((tq, GW), lambda b, g, i: (b * nq + i, g)),
                  pl.BlockSpec((tq, LANES), lambda b, g, i: (b * nq + i, OFF_NG // LANES))],
        out_specs=pl.BlockSpec((tq, GW), lambda b, g, i: (b * nq + i, g)),
        out_shape=jax.ShapeDtypeStruct((t, w), F32),
        compiler_params=_cp("parallel", "parallel", "arbitrary"),
        name="nsa_attn",
    )(qr, ks, kw, vt(OFF_V, kc), vt(OFF_V + GW, tq), sel, o_cmp, proj)


def _nsa(proj, kcvc, cos4, sin4, qnorm_g, knorm_g, cmp_pe, cmp_w, bsz, seq):
    qn, qr, ks, kw = _nsa_prep(proj, cos4, sin4, qnorm_g, knorm_g, bsz, seq)
    o_cmp, sel = _nsa_cmp(kcvc, qn, cmp_pe, cmp_w, knorm_g[0], bsz, seq)
    return _nsa_attn(proj, qr, ks, kw, sel, o_cmp, bsz, seq)


PEER_TT = 128
PEER_CT = 8
HALF_D = 512


SUBLANES = 8
CODE_BITS = 127
FAR_BELOW = -3.0e38


def _with_code(x, code):
    bits = lax.bitcast_convert_type(x, jnp.int32)
    return lax.bitcast_convert_type((bits & ~CODE_BITS) | code, F32)


def _split_code(x):
    bits = lax.bitcast_convert_type(x, jnp.int32)
    return lax.bitcast_convert_type(bits & ~CODE_BITS, F32), bits & CODE_BITS


def _sort16_desc(xs):
    xs = list(xs)
    n = len(xs)
    k = 2
    while k <= n:
        j = k // 2
        while j >= 1:
            for i in range(n):
                l = i ^ j
                if l > i:
                    hi, lo = jnp.maximum(xs[i], xs[l]), jnp.minimum(xs[i], xs[l])
                    xs[i], xs[l] = (hi, lo) if (i & k) == 0 else (lo, hi)
            j //= 2
        k *= 2
    return xs


def _merge16_desc(xs):
    xs = list(xs)
    j = len(xs) // 2
    while j >= 1:
        for i in range(len(xs)):
            l = i ^ j
            if l > i:
                xs[i], xs[l] = jnp.maximum(xs[i], xs[l]), jnp.minimum(xs[i], xs[l])
        j //= 2
    return xs


def _top16_columns(x):
    n = PEER_TOPK
    xs = _sort16_desc([x[SUBLANES * j:SUBLANES * (j + 1), :] for j in range(n)])
    shift = SUBLANES // 2
    while shift >= 1:
        rolled = [pltpu.roll(a, shift, 0) for a in xs]
        xs = _merge16_desc([jnp.maximum(xs[i], rolled[n - 1 - i]) for i in range(n)])
        shift //= 2
    return xs


_PEER_CAND_TILES = ((0, 0, 8), (0, 1, 8), (1, 0, 8), (2, 0, 5), (3, 0, 4), (4, 0, 3), (5, 0, 2), (6, 0, 2), (7, 0, 2))


def _peer_route_kernel(q_ref, key_ref, e_ref, g_ref):
    tt = q_ref.shape[0]
    nk = PEER_NKEYS
    n = PEER_TOPK
    row = lax.broadcasted_iota(jnp.int32, (nk, tt), 0)
    sub = lax.broadcasted_iota(jnp.int32, (SUBLANES, tt), 0)
    vals, ids = [], []
    for p in range(2):
        st = _dot_nt(key_ref[0, p], q_ref[:, p * PEER_KDIM:(p + 1) * PEER_KDIM])
        top = [_split_code(a) for a in _top16_columns(_with_code(st, (nk - 1) - row))]
        vals.append([v for v, _ in top])
        ids.append([(nk - 1) - c for _, c in top])
    (v1, v2), (i1, i2) = vals, ids

    def stack(xs, lo):
        out = xs[lo]
        for s in range(1, SUBLANES):
            out = jnp.where(sub == s, xs[lo + s], out)
        return out

    v2t, i2t = (stack(v2, 0), stack(v2, SUBLANES)), (stack(i2, 0), stack(i2, SUBLANES))
    cand, cexp = [], []
    for a, tile, nvalid in _PEER_CAND_TILES:
        v = v1[a] + v2t[tile]
        cand.append(v if nvalid == SUBLANES else jnp.where(sub < nvalid, v, FAR_BELOW))
        cexp.append(i1[a] * nk + i2t[tile])
    cand.append(stack(v1, SUBLANES) + v2[0])
    cexp.append(stack(i1, SUBLANES) * nk + i2[0])
    n_tiles = len(cand)
    slot_code = [(nk - 1) - (c * SUBLANES + sub) for c in range(n_tiles)]
    coded = [_with_code(v, sc) for v, sc in zip(cand, slot_code)]
    coded += [jnp.full((SUBLANES, tt), FAR_BELOW, F32)] * (n - n_tiles)
    top = [_split_code(a) for a in _top16_columns(jnp.concatenate(coded, axis=0))]
    call = jnp.concatenate(cexp, axis=0)
    slot = (nk - 1) - lax.broadcasted_iota(jnp.int32, call.shape, 0)
    ex = [jnp.exp(v - top[0][0]) for v, _ in top]
    tot = ex[0]
    for k in range(1, n):
        tot = tot + ex[k]
    for k in range(n):
        hit = slot == jnp.concatenate([top[k][1]] * n_tiles, axis=0)
        e_ref[0, k:k + 1, :] = jnp.sum(jnp.where(hit, call, 0), axis=0, keepdims=True)
        g_ref[0, k:k + 1, :] = (ex[k] / tot)[0:1, :]


def _peer_route(qp, keys):
    t = qp.shape[0]
    tt = min(PEER_TT, t)
    return pl.pallas_call(
        _peer_route_kernel,
        grid=(t // tt, PEER_HEADS),
        in_specs=[pl.BlockSpec((tt, 2 * PEER_KDIM), lambda i, h: (i, h)),
                  pl.BlockSpec((1, 2, PEER_NKEYS, PEER_KDIM), lambda i, h: (h, 0, 0, 0))],
        out_specs=[pl.BlockSpec((1, PEER_TOPK, tt), lambda i, h: (h, 0, i)),
                   pl.BlockSpec((1, PEER_TOPK, tt), lambda i, h: (h, 0, i))],
        out_shape=[jax.ShapeDtypeStruct((PEER_HEADS, PEER_TOPK, t), jnp.int32),
                   jax.ShapeDtypeStruct((PEER_HEADS, PEER_TOPK, t), F32)],
        compiler_params=_cp("parallel", "arbitrary"),
        name="peer_route",
    )(qp, keys)


def _pack_tables_kernel(u_ref, v_ref, o_ref):
    def pack(x):
        lo = lax.bitcast_convert_type(x[:, 0:HALF_D].astype(BF16).astype(F32), jnp.int32)
        hi = lax.bitcast_convert_type(x[:, HALF_D:2 * HALF_D].astype(BF16).astype(F32), jnp.int32)
        return lax.shift_right_logical(lo, 16) | (hi & jnp.int32(-65536))

    o_ref[:, 0:HALF_D] = pack(u_ref[...])
    o_ref[:, HALF_D:2 * HALF_D] = pack(v_ref[...])


def _pack_tables(u_tab, v_tab, tr=512):
    e, d = u_tab.shape
    assert d == 2 * HALF_D
    spec = pl.BlockSpec((tr, d), lambda i: (i, 0))
    return pl.pallas_call(
        _pack_tables_kernel,
        grid=(e // tr,),
        in_specs=[spec, spec],
        out_specs=spec,
        out_shape=jax.ShapeDtypeStruct((e, d), jnp.int32),
        compiler_params=_cp("parallel"),
        name="peer_pack",
    )(u_tab, v_tab)


def _unpack_rows(wd):
    lo = lax.bitcast_convert_type(lax.shift_left(wd, 16), F32)
    hi = lax.bitcast_convert_type(lax.bitwise_and(wd, jnp.int32(-65536)), F32)
    return lo, hi


SC_WINDOW = 32


def _sc_gather(table, idx):
    from jax.experimental.pallas import tpu_sc as plsc
    n = idx.shape[0]
    width = table.shape[1]
    mesh = plsc.VectorSubcoreMesh(core_axis_name="core", subcore_axis_name="subcore")

    @functools.partial(pl.kernel, out_type=jax.ShapeDtypeStruct((n, width), table.dtype), mesh=mesh)
    def gather(tab_hbm, idx_hbm, out_hbm):
        def body(idx_vmem, out_vmem):
            pltpu.sync_copy(tab_hbm.at[idx_vmem.at[0, pl.ds(0, SC_WINDOW)]], out_vmem)

        pltpu.emit_pipeline(
            body,
            grid=(n // SC_WINDOW,),
            in_specs=[pl.BlockSpec((1, LANES), lambda i: (0, i))],
            out_specs=[pl.BlockSpec((SC_WINDOW, width), lambda i: (i, 0))],
            core_axis_name=("core", "subcore"),
            dimension_semantics=(pltpu.PARALLEL,),
            trace_scopes=False,
        )(idx_hbm, out_hbm)

    idx_pad = jnp.pad(idx.reshape(n // SC_WINDOW, SC_WINDOW), ((0, 0), (0, LANES - SC_WINDOW)))
    return gather(table, idx_pad.reshape(1, (n // SC_WINDOW) * LANES))


def _peer_combine_kernel(x_ref, g2_ref, rows_ref, gate_ref, o_ref):
    ne = PEER_HEADS * PEER_TOPK
    x = x_ref[...]
    xn = x * lax.rsqrt(jnp.mean(x * x, axis=-1, keepdims=True) + NORM_EPS) * g2_ref[...]
    for j in range(x.shape[0]):
        u_lo, u_hi = _unpack_rows(rows_ref[j * ne:(j + 1) * ne, 0:HALF_D])
        xr = xn[j:j + 1, :]
        h = jnp.sum(u_lo * xr[:, 0:HALF_D] + u_hi * xr[:, HALF_D:2 * HALF_D], axis=1, keepdims=True)
        act = 0.5 * h * (1.0 + lax.erf(h * (2.0 ** -0.5)))
        wgt = gate_ref[0, :, j:j + 1] * act
        v_lo, v_hi = _unpack_rows(rows_ref[j * ne:(j + 1) * ne, HALF_D:2 * HALF_D])
        o_ref[j:j + 1, 0:HALF_D] = x[j:j + 1, 0:HALF_D] + jnp.sum(wgt * v_lo, axis=0, keepdims=True)
        o_ref[j:j + 1, HALF_D:2 * HALF_D] = x[j:j + 1, HALF_D:2 * HALF_D] + jnp.sum(wgt * v_hi, axis=0, keepdims=True)


def _peer_combine(x, g2, rows, gates):
    t, d = x.shape
    ne = PEER_HEADS * PEER_TOPK
    ct = PEER_CT
    return pl.pallas_call(
        _peer_combine_kernel,
        grid=(t // ct,),
        in_specs=[pl.BlockSpec((ct, d), lambda i: (i, 0)),
                  pl.BlockSpec((1, d), lambda i: (0, 0)),
                  pl.BlockSpec((ct * ne, d), lambda i: (i, 0)),
                  pl.BlockSpec((1, ne, ct), lambda i: (i, 0, 0))],
        out_specs=pl.BlockSpec((ct, d), lambda i: (i, 0)),
        out_shape=jax.ShapeDtypeStruct((t, d), F32),
        compiler_params=_cp("parallel"),
        name="peer_combine",
    )(x, g2.reshape(1, d), rows, gates)


PEER_TOKENS_PER_GATHER = 2048


def _peer(x, g2, wq, keys, u_tab, v_tab, gather_fn):
    t, d = x.shape
    ne = PEER_HEADS * PEER_TOPK
    qp = _norm_matmul(x, g2, wq.astype(BF16), out_dtype=BF16)
    keys_b = keys.astype(BF16)
    e_t, g_t = _peer_route(qp, keys_b)
    idx = e_t.reshape(ne, t).T.reshape(t * ne)
    gates = g_t.reshape(ne, t // PEER_CT, PEER_CT).transpose(1, 0, 2)
    table = _pack_tables(u_tab, v_tab)
    tc = min(PEER_TOKENS_PER_GATHER, t)
    outs = []
    for c in range(t // tc):
        rows = gather_fn(table, idx[c * tc * ne:(c + 1) * tc * ne])
        outs.append(_peer_combine(x[c * tc:(c + 1) * tc], g2, rows,
                                  gates[c * tc // PEER_CT:(c + 1) * tc // PEER_CT]))
    return jnp.concatenate(outs, axis=0) if len(outs) > 1 else outs[0]


_IN_WIDTHS = (256, 256, 256, 256, 256, 256, 256, 4, 4, 256, 256, 128, 128, 128, 128, 128, 128, 12,
              256, 256, 256, 256)


def _dup_groups(wcols):
    g0, g1 = wcols[:, :HEAD_DIM], wcols[:, HEAD_DIM:]
    return jnp.concatenate([g0, g0, g1, g1], axis=1)


def _layout_w_in(w_in):
    offs = np.cumsum((0,) + _IN_WIDTHS)
    cols = [w_in[:, offs[i]:offs[i + 1]] for i in range(len(_IN_WIDTHS))]
    (hq, hf, hi, hg, mq, mk, mv, mi, mf, mo, nq, nkc, nvc, nks, nvs, nkw, nvw, ng, rq, rk, rv, rg) = cols
    d = w_in.shape[0]
    pad = lambda c, n: jnp.concatenate([c, jnp.zeros((d, n - c.shape[1]), w_in.dtype)], axis=1)
    main = jnp.concatenate([hq, hf, hi, hg, mq, mk, mv, mo, rq, rk, rv, rg,
                            _dup_groups(nks), _dup_groups(nkw), nq, nvs, nvw,
                            pad(jnp.concatenate([mi, mf], axis=1), LANES), pad(ng, LANES)], axis=1)
    assert main.shape[1] == N_MAIN
    kcvc = jnp.concatenate([nkc, nvc], axis=1)
    return main.astype(BF16), kcvc.astype(BF16)


def kernel(x, norm1_g, w_in, hgrn_lb, hgrn_onorm_g, mlstm_conv_w, mlstm_conv_b, mlstm_gate_b, mlstm_onorm_g, nsa_qnorm_g, nsa_knorm_g, nsa_cmp_pe, nsa_cmp_w, ret_onorm_g, w_up, w_gate, w_out, norm2_g, peer_wq, peer_keys, peer_u, peer_v):
    bsz, seq, d = x.shape
    t = bsz * seq
    depth = w_in.shape[0]
    cos_t, sin_t = _rope_lane_tables(seq)
    cos4, sin4 = jnp.tile(cos_t, (1, 2)), jnp.tile(sin_t, (1, 2))
    lb_cum = jnp.cumsum(jax.nn.softmax(hgrn_lb.astype(F32), axis=0), axis=0)
    lb_all = lb_cum - lb_cum[0:1]
    xf = x.reshape(t, d)
    for l in range(depth):
        w_main, w_kcvc = _layout_w_in(w_in[l])
        proj = _norm_matmul(xf, norm1_g[l], w_main)
        gates = _norm_matmul(xf, norm1_g[l], w_gate[l].astype(BF16), act="sigmoid", out_dtype=BF16)
        o_h = _hgrn(proj, _hgrn_lb_rows(lb_all[l]), hgrn_onorm_g[l], bsz, seq)
        o_m = _mlstm(proj, mlstm_conv_w[l], mlstm_conv_b[l], mlstm_gate_b[l], mlstm_onorm_g[l], bsz, seq)
        o_r = _ret(proj, cos4, sin4, ret_onorm_g[l], bsz, seq)
        kcvc = _norm_matmul(xf, norm1_g[l], w_kcvc)
        o_n = _nsa(proj, kcvc, cos4, sin4, nsa_qnorm_g[l], nsa_knorm_g[l], nsa_cmp_pe[l], nsa_cmp_w[l], bsz, seq)
        xf = _merge(xf, gates, (o_h, o_m, o_n, o_r), w_up[l].astype(BF16), w_out[l].astype(BF16))
        xf = _peer(xf, norm2_g[l], peer_wq[l], peer_keys[l], peer_u[l], peer_v[l], _sc_gather)
    return xf.reshape(bsz, seq, d)
```

```python
import functools
import math

import numpy as np
import jax
import jax.numpy as jnp
from jax import lax
from jax.experimental import pallas as pl
from jax.experimental.pallas import tpu as pltpu

F32 = jnp.float32
BF16 = jnp.bfloat16

HEAD_DIM = 64
N_HEADS = 4
MIX_WIDTH = N_HEADS * HEAD_DIM
CHUNK = 64
NORM_EPS = 1e-6
NEG_INF = -1e30
ROPE_THETA = 10000.0
CONV_W = 4
NSA_GROUPS = 2
CMP_LEN = 32
CMP_STRIDE = 16
SEL_BLOCK = 64
SEL_TOPK = 16
WINDOW = 512
FORCE_BONUS = 1e3
PEER_HEADS = 8
PEER_NKEYS = 128
PEER_TOPK = 16
PEER_KDIM = 128

LANES = 128
VMEM_LIMIT = 48 * 1024 * 1024

OFF_H, OFF_M, OFF_R, OFF_KD, OFF_NQ, OFF_V, OFF_MG, OFF_NG = 0, 1024, 2048, 3072, 3584, 3840, 4096, 4224
N_MAIN = 4352


def _cp(*sem):
    return pltpu.CompilerParams(dimension_semantics=sem, vmem_limit_bytes=VMEM_LIMIT)


def _dot(a, b):
    return jnp.dot(a, b, preferred_element_type=F32)


def _dot_nt(a, b):
    return lax.dot_general(a, b, (((1,), (1,)), ((), ())), preferred_element_type=F32)


def _dot_tn(a, b):
    return lax.dot_general(a, b, (((0,), (0,)), ((), ())), preferred_element_type=F32)


def _split3(x):
    hi = x.astype(BF16)
    r1 = x - hi.astype(F32)
    mid = r1.astype(BF16)
    lo = (r1 - mid.astype(F32)).astype(BF16)
    return hi, mid, lo


def _dot01_l(m01, x):
    hi, mid, lo = _split3(x)
    return _dot(m01, hi) + _dot(m01, mid) + _dot(m01, lo)


def _dot01_r(x, m01):
    hi, mid, lo = _split3(x)
    return _dot(hi, m01) + _dot(mid, m01) + _dot(lo, m01)


def _head_of_lane(shape, axis):
    return lax.broadcasted_iota(jnp.int32, shape, axis) // HEAD_DIM


def _block_ones(n, dtype=BF16):
    r = lax.broadcasted_iota(jnp.int32, (n, n), 0) // HEAD_DIM
    c = lax.broadcasted_iota(jnp.int32, (n, n), 1) // HEAD_DIM
    return (r == c).astype(dtype)


def _group_sum(x, ones_bd):
    hi = x.astype(BF16)
    lo = (x - hi.astype(F32)).astype(BF16)
    return _dot(hi, ones_bd) + _dot(lo, ones_bd)


def _head_rms(x, gain, ones_bd):
    ms = _group_sum(x * x, ones_bd) * (1.0 / HEAD_DIM)
    return x * lax.rsqrt(ms + NORM_EPS) * gain


def _sigmoid(x):
    return 1.0 / (1.0 + jnp.exp(-x))


def _silu(x):
    return x * _sigmoid(x)


def _log_sigmoid(x):
    return jnp.minimum(x, 0.0) - jnp.log(1.0 + jnp.exp(-jnp.abs(x)))


def _stack_heads(x, n_heads=N_HEADS):
    hl = _head_of_lane(x.shape, 1)
    return jnp.concatenate([jnp.where(hl == h, x, jnp.zeros_like(x)) for h in range(n_heads)], axis=0)


def _unstack_heads(r, c, n_heads=N_HEADS):
    hl = _head_of_lane((c, r.shape[1]), 1)
    out = jnp.zeros((c, r.shape[1]), F32)
    for h in range(n_heads):
        out = jnp.where(hl == h, r[h * c:(h + 1) * c, :], out)
    return out


def _rope(x, cos_t, sin_t):
    n = x.shape[1]
    first = (lax.broadcasted_iota(jnp.int32, x.shape, 1) % HEAD_DIM) < (HEAD_DIM // 2)
    partner = jnp.where(first, pltpu.roll(x, n - HEAD_DIM // 2, 1), pltpu.roll(x, HEAD_DIM // 2, 1))
    return x * cos_t + partner * sin_t


def _norm_matmul_kernel(x_ref, g_ref, w_ref, o_ref, xn_ref, *, act):
    @pl.when(pl.program_id(1) == 0)
    def _():
        x = x_ref[...]
        ms = jnp.mean(x * x, axis=-1, keepdims=True)
        xn_ref[...] = (x * lax.rsqrt(ms + NORM_EPS) * g_ref[...]).astype(BF16)

    y = _dot(xn_ref[...], w_ref[...])
    if act == "sigmoid":
        y = _sigmoid(y)
    o_ref[...] = y.astype(o_ref.dtype)


def _norm_matmul(x, g, w, *, act=None, out_dtype=F32, tm=1024, tn=512):
    t, d = x.shape
    w3 = w if w.ndim == 3 else w[None]
    n_per = w3.shape[2]
    tm = min(tm, t)
    tn = next(c for c in (tn, 256, 128) if n_per % c == 0)
    per = n_per // tn
    n = w3.shape[0] * n_per
    assert t % tm == 0
    return pl.pallas_call(
        functools.partial(_norm_matmul_kernel, act=act),
        grid=(t // tm, n // tn),
        in_specs=[pl.BlockSpec((tm, d), lambda i, j: (i, 0)),
                  pl.BlockSpec((1, d), lambda i, j: (0, 0)),
                  pl.BlockSpec((None, d, tn), lambda i, j: (j // per, 0, j % per))],
        out_specs=pl.BlockSpec((tm, tn), lambda i, j: (i, j)),
        out_shape=jax.ShapeDtypeStruct((t, n), out_dtype),
        scratch_shapes=[pltpu.VMEM((tm, d), BF16)],
        compiler_params=_cp("parallel", "arbitrary"),
        name="norm_matmul",
    )(x, g.reshape(1, d), w3)


def _merge_kernel(x_ref, gate_ref, oh_ref, om_ref, on_ref, or_ref, wup_ref, wout_ref, o_ref):
    d = x_ref.shape[1]
    acc = None
    for m, r in enumerate((oh_ref, om_ref, on_ref, or_ref)):
        up = _dot(r[...].astype(BF16), wup_ref[m])
        term = gate_ref[:, m * d:(m + 1) * d].astype(F32) * up
        acc = term if acc is None else acc + term
    o_ref[...] = x_ref[...] + _dot(acc.astype(BF16), wout_ref[...])


def _merge(x, gates, outs, w_up, w_out, tm=512):
    t, d = x.shape
    tm = min(tm, t)
    mix = pl.BlockSpec((tm, MIX_WIDTH), lambda i: (i, 0))
    return pl.pallas_call(
        _merge_kernel,
        grid=(t // tm,),
        in_specs=[pl.BlockSpec((tm, d), lambda i: (i, 0)),
                  pl.BlockSpec((tm, 4 * d), lambda i: (i, 0)),
                  mix, mix, mix, mix,
                  pl.BlockSpec((4, MIX_WIDTH, d), lambda i: (0, 0, 0)),
                  pl.BlockSpec((d, d), lambda i: (0, 0))],
        out_specs=pl.BlockSpec((tm, d), lambda i: (i, 0)),
        out_shape=jax.ShapeDtypeStruct((t, d), F32),
        compiler_params=_cp("parallel"),
        name="merge",
    )(x, gates, *outs, w_up, w_out)


REC_BLOCK = 256


def _chunk_consts():
    t = lax.broadcasted_iota(jnp.int32, (CHUNK, CHUNK), 0)
    s = lax.broadcasted_iota(jnp.int32, (CHUNK, CHUNK), 1)
    return t, s


def _hgrn_levels():
    t = np.arange(CHUNK)
    rows = []
    masks = []
    h = CHUNK // 2
    while h >= 1:
        ref = (t // (2 * h)) * (2 * h) + h
        p = np.zeros((CHUNK, CHUNK), np.float32)
        p[t, np.minimum(ref, CHUNK - 1)] = 1.0
        rows.append(p)
        same = (t[:, None] // (2 * h)) == (t[None, :] // (2 * h))
        m = same & ((t[:, None] // h) % 2 == 1) & ((t[None, :] // h) % 2 == 0)
        masks.append(m.astype(np.float32))
        h //= 2
    masks.append(np.eye(CHUNK, dtype=np.float32))
    return np.concatenate(rows, 0), np.stack(masks, 0)


def _hgrn_kernel(p_ref, lb_ref, g_ref, psel_ref, lmask_ref, o_ref, st_ref):
    @pl.when(pl.program_id(1) == 0)
    def _():
        st_ref[...] = jnp.zeros_like(st_ref)

    c = CHUNK
    w = MIX_WIDTH
    ones_bd = _block_ones(w)
    bd_mask = _block_ones(w, F32)
    tri = (lax.broadcasted_iota(jnp.int32, (c, c), 0) >= lax.broadcasted_iota(jnp.int32, (c, c), 1)).astype(BF16)
    psel = psel_ref[...]
    n_lv = lmask_ref.shape[0]
    log_lb, log_1mlb, one_mlb = lb_ref[0:1, :], lb_ref[1:2, :], lb_ref[2:3, :]
    gain = g_ref[...]

    def chunk(ci, carry):
        r0 = pl.multiple_of(ci * c, c)
        q = _silu(p_ref[pl.ds(r0, c), 0:w])
        fl = p_ref[pl.ds(r0, c), w:2 * w]
        v = p_ref[pl.ds(r0, c), 2 * w:3 * w]
        gp = p_ref[pl.ds(r0, c), 3 * w:4 * w]
        a1 = jnp.broadcast_to(log_lb, fl.shape)
        a2 = log_1mlb + _log_sigmoid(fl)
        mx = jnp.maximum(a1, a2)
        log_f = mx + jnp.log(jnp.exp(a1 - mx) + jnp.exp(a2 - mx))
        k = one_mlb * _sigmoid(-fl)
        b = _dot01_l(tri, log_f)
        bref = _dot01_l(psel, b)
        vb = v.astype(BF16)
        a = jnp.zeros((N_HEADS * c, c), F32)
        for lv in range(n_lv):
            if lv < n_lv - 1:
                br = bref[lv * c:(lv + 1) * c, :]
                qs = q * jnp.exp(jnp.minimum(b - br, 0.0))
                ks = k * jnp.exp(jnp.minimum(br - b, 0.0))
            else:
                qs, ks = q, k
            s_lv = _dot_nt(_stack_heads(qs).astype(BF16), ks.astype(BF16))
            a = a + jnp.concatenate([lmask_ref[lv]] * N_HEADS, axis=0) * s_lv
        o = _unstack_heads(_dot(a.astype(BF16), vb), c)
        st = st_ref[...]
        o = o + _dot_nt((q * jnp.exp(b)).astype(BF16), st.astype(BF16))
        b_last = b[c - 1:c, :]
        kb = k * jnp.exp(b_last - b)
        st_ref[...] = st * jnp.exp(b_last) + bd_mask * _dot_tn(vb, kb.astype(BF16))
        y = _head_rms(o, gain, ones_bd) * _silu(gp)
        o_ref[pl.ds(r0, c), :] = y
        return carry

    lax.fori_loop(0, p_ref.shape[0] // c, chunk, 0)


def _hgrn(proj, lb_rows, gain, bsz, seq):
    psel, lmask = _hgrn_levels()
    tb = min(REC_BLOCK, seq)
    nb = seq // tb
    return pl.pallas_call(
        _hgrn_kernel,
        grid=(bsz, nb),
        in_specs=[pl.BlockSpec((tb, 4 * MIX_WIDTH), lambda b, i: (b * nb + i, OFF_H // (4 * MIX_WIDTH))),
                  pl.BlockSpec((8, MIX_WIDTH), lambda b, i: (0, 0)),
                  pl.BlockSpec((1, MIX_WIDTH), lambda b, i: (0, 0)),
                  pl.BlockSpec(psel.shape, lambda b, i: (0, 0)),
                  pl.BlockSpec(lmask.shape, lambda b, i: (0, 0, 0))],
        out_specs=pl.BlockSpec((tb, MIX_WIDTH), lambda b, i: (b * nb + i, 0)),
        out_shape=jax.ShapeDtypeStruct((bsz * seq, MIX_WIDTH), F32),
        scratch_shapes=[pltpu.VMEM((MIX_WIDTH, MIX_WIDTH), F32)],
        compiler_params=_cp("parallel", "arbitrary"),
        name="hgrn2",
    )(proj, lb_rows, gain.reshape(1, MIX_WIDTH), jnp.asarray(psel, BF16), jnp.asarray(lmask, F32))


def _ret_kernel(p_ref, cos_ref, sin_ref, dec_ref, decin_ref, g_ref, o_ref, st_ref):
    @pl.when(pl.program_id(1) == 0)
    def _():
        st_ref[...] = jnp.zeros_like(st_ref)

    c = CHUNK
    w = MIX_WIDTH
    ones_bd = _block_ones(w)
    bd_mask = _block_ones(w, F32)
    gain = g_ref[...]
    dec_q = dec_ref[0:c, :]
    dec_k = dec_ref[c:2 * c, :]
    dec_state = dec_ref[2 * c:2 * c + 1, :]
    dec_in = decin_ref[...]

    def chunk(ci, carry):
        r0 = pl.multiple_of(ci * c, c)
        cos_t = cos_ref[pl.ds(r0, c), :]
        sin_t = sin_ref[pl.ds(r0, c), :]
        q = _rope(p_ref[pl.ds(r0, c), 0:w], cos_t, sin_t)
        k = _rope(p_ref[pl.ds(r0, c), w:2 * w], cos_t, sin_t) * (HEAD_DIM ** -0.5)
        v = p_ref[pl.ds(r0, c), 2 * w:3 * w]
        gp = p_ref[pl.ds(r0, c), 3 * w:4 * w]
        vb = v.astype(BF16)
        a = _dot_nt(_stack_heads(q).astype(BF16), k.astype(BF16)) * dec_in
        o = _unstack_heads(_dot(a.astype(BF16), vb), c)
        st = st_ref[...]
        o = o + _dot_nt(q.astype(BF16), st.astype(BF16)) * dec_q
        st_ref[...] = st * dec_state + bd_mask * _dot_tn(vb, (k * dec_k).astype(BF16))
        o_ref[pl.ds(r0, c), :] = _head_rms(o, gain, ones_bd) * _silu(gp)
        return carry

    lax.fori_loop(0, p_ref.shape[0] // c, chunk, 0)


def _ret_consts():
    log_gamma = np.log1p(-np.exp2(-5.0 - np.arange(N_HEADS, dtype=np.float64)))
    t = np.arange(CHUNK, dtype=np.float64)
    lane_h = np.arange(MIX_WIDTH) // HEAD_DIM
    dec_q = np.exp(log_gamma[lane_h][None, :] * (t[:, None] + 1.0))
    dec_k = np.exp(log_gamma[lane_h][None, :] * (CHUNK - 1.0 - t[:, None]))
    dec_state = np.exp(log_gamma[lane_h] * CHUNK)[None, :]
    dec = np.concatenate([dec_q, dec_k, np.broadcast_to(dec_state, (8, MIX_WIDTH))], 0)
    diff = t[:, None] - t[None, :]
    dec_in = np.concatenate([np.where(diff >= 0, np.exp(log_gamma[h] * diff), 0.0) for h in range(N_HEADS)], 0)
    return dec.astype(np.float32), dec_in.astype(np.float32)


def _ret(proj, cos4, sin4, gain, bsz, seq):
    dec, dec_in = _ret_consts()
    tb = min(REC_BLOCK, seq)
    nb = seq // tb
    return pl.pallas_call(
        _ret_kernel,
        grid=(bsz, nb),
        in_specs=[pl.BlockSpec((tb, 4 * MIX_WIDTH), lambda b, i: (b * nb + i, OFF_R // (4 * MIX_WIDTH))),
                  pl.BlockSpec((tb, MIX_WIDTH), lambda b, i: (i, 0)),
                  pl.BlockSpec((tb, MIX_WIDTH), lambda b, i: (i, 0)),
                  pl.BlockSpec(dec.shape, lambda b, i: (0, 0)),
                  pl.BlockSpec(dec_in.shape, lambda b, i: (0, 0)),
                  pl.BlockSpec((1, MIX_WIDTH), lambda b, i: (0, 0))],
        out_specs=pl.BlockSpec((tb, MIX_WIDTH), lambda b, i: (b * nb + i, 0)),
        out_shape=jax.ShapeDtypeStruct((bsz * seq, MIX_WIDTH), F32),
        scratch_shapes=[pltpu.VMEM((MIX_WIDTH, MIX_WIDTH), F32)],
        compiler_params=_cp("parallel", "arbitrary"),
        name="retention",
    )(proj, cos4, sin4, jnp.asarray(dec), jnp.asarray(dec_in), gain.reshape(1, MIX_WIDTH))


def _hgrn_lb_rows(lb):
    lb = lb.astype(F32)
    rows = jnp.stack([jnp.log(lb), jnp.log1p(-lb), 1.0 - lb], 0)
    return jnp.concatenate([rows, jnp.zeros((5, lb.shape[0]), F32)], 0)


def _rope_lane_tables(seq):
    inv = 1.0 / (ROPE_THETA ** (jnp.arange(0, HEAD_DIM, 2, dtype=F32) / HEAD_DIM))
    ang = jnp.arange(seq, dtype=F32)[:, None] * inv[None, :]
    cos, sin = jnp.cos(ang), jnp.sin(ang)
    cos_t = jnp.tile(cos, (1, LANES // (HEAD_DIM // 2)))
    sin_t = jnp.tile(jnp.concatenate([-sin, sin], axis=1), (1, LANES // HEAD_DIM))
    return cos_t, sin_t


def _expand_heads(cols, shape):
    hl = _head_of_lane(shape, 1)
    out = jnp.broadcast_to(cols[-1], shape)
    for h in range(len(cols) - 2, -1, -1):
        out = jnp.where(hl == h, jnp.broadcast_to(cols[h], shape), out)
    return out


def _mlstm_kernel(p_ref, gcol_ref, grow_ref, cw_ref, cb_ref, gbr_ref, gbc_ref, g_ref, o_ref,
                  ct_ref, n_ref, m_ref, hist_ref, cbuf_ref, qk_ref):
    c = CHUNK
    w = MIX_WIDTH
    tb = p_ref.shape[0]

    @pl.when(pl.program_id(1) == 0)
    def _():
        ct_ref[...] = jnp.zeros_like(ct_ref)
        n_ref[...] = jnp.zeros_like(n_ref)
        m_ref[...] = jnp.zeros_like(m_ref)
        hist_ref[...] = jnp.zeros_like(hist_ref)

    cbuf_ref[0:8, :] = hist_ref[...]
    cbuf_ref[8:, :] = p_ref[:, 0:2 * w]
    hist_ref[...] = p_ref[tb - 8:tb, 0:2 * w]
    acc = jnp.broadcast_to(cb_ref[...], (tb, 2 * w))
    for j in range(CONV_W):
        acc = acc + cw_ref[j:j + 1, :] * cbuf_ref[pl.ds(8 - (CONV_W - 1) + j, tb), :]
    qk_ref[...] = _silu(acc)

    ones_bd = _block_ones(w)
    bd_mask = _block_ones(w, F32)
    ti = lax.broadcasted_iota(jnp.int32, (c, c), 0)
    si = lax.broadcasted_iota(jnp.int32, (c, c), 1)
    causal = ti >= si
    tri = causal.astype(BF16)
    tri_t = (ti <= si).astype(BF16)
    gain = g_ref[...]
    ones_ext = jnp.ones((c, LANES), BF16)

    def chunk(ci, carry):
        r0 = pl.multiple_of(ci * c, c)
        q = qk_ref[pl.ds(r0, c), 0:w]
        k = qk_ref[pl.ds(r0, c), w:2 * w] * (HEAD_DIM ** -0.5)
        v = p_ref[pl.ds(r0, c), 2 * w:3 * w]
        op = p_ref[pl.ds(r0, c), 3 * w:4 * w]
        gc = gcol_ref[pl.ds(r0, c), :] + gbr_ref[...]
        gr = grow_ref[ci] + gbc_ref[...]
        b_c = _dot01_l(tri, _log_sigmoid(gc))
        b_r = _dot01_r(_log_sigmoid(gr), tri_t)
        wd, s_inter, em, wk, decay = [], [], [], [], []
        for h in range(N_HEADS):
            bc = b_c[:, N_HEADS + h:N_HEADS + h + 1]
            lic = gc[:, h:h + 1]
            br = b_r[N_HEADS + h:N_HEADS + h + 1, :]
            lir = gr[h:h + 1, :]
            dmat = jnp.where(causal, bc - br + lir, -jnp.inf)
            m_prev = m_ref[h:h + 1, 0:1]
            inter = bc + m_prev
            mrow = jnp.maximum(inter, jnp.max(dmat, axis=1, keepdims=True))
            wd.append(jnp.exp(dmat - mrow))
            s_inter.append(jnp.exp(inter - mrow))
            em.append(jnp.exp(-mrow))
            b_last = br[:, c - 1:c]
            m_new = jnp.maximum(b_last + m_prev, jnp.max(b_last - br + lir, axis=1, keepdims=True))
            wk.append(jnp.exp(b_last - bc + lic - m_new))
            decay.append(jnp.exp(b_last + m_prev - m_new))
            m_ref[h:h + 1, :] = jnp.broadcast_to(m_new, (1, LANES))
        s_inter_l = _expand_heads(s_inter, (c, w))
        em_l = _expand_heads(em, (c, w))
        wk_l = _expand_heads(wk, (c, w))
        decay_l = _expand_heads(decay, (1, w))
        qk = _dot_nt(_stack_heads(q).astype(BF16), k.astype(BF16))
        wmat = jnp.concatenate(wd, axis=0) * qk
        vb = v.astype(BF16)
        r = _dot(wmat.astype(BF16), jnp.concatenate([vb, ones_ext], axis=1))
        num_intra = _unstack_heads(r[:, 0:w], c)
        rs_l = _expand_heads([r[h * c:(h + 1) * c, w:w + 1] for h in range(N_HEADS)], (c, w))
        ct = ct_ref[...]
        nrow = n_ref[0:1, :]
        num = s_inter_l * _dot_nt(q.astype(BF16), ct.astype(BF16)) + num_intra
        den = s_inter_l * _group_sum(q * nrow, ones_bd) + rs_l
        hval = num / jnp.maximum(jnp.abs(den), em_l)
        kw = wk_l * k
        ct_ref[...] = ct * decay_l + bd_mask * _dot_tn(vb, kw.astype(BF16))
        n_ref[0:1, :] = nrow * decay_l + jnp.sum(kw, axis=0, keepdims=True)
        o_ref[pl.ds(r0, c), :] = _head_rms(hval, gain, ones_bd) * _sigmoid(op)
        return carry

    lax.fori_loop(0, tb // c, chunk, 0)


def _mlstm(proj, conv_w, conv_b, gate_b, gain, bsz, seq):
    t = bsz * seq
    w = MIX_WIDTH
    tb = min(REC_BLOCK, seq)
    nb = seq // tb
    ncb = tb // CHUNK
    grow = proj[:, OFF_MG:OFF_MG + 8].reshape(t // CHUNK, CHUNK, 8).transpose(0, 2, 1)
    gb_row = jnp.zeros((1, LANES), F32).at[0, 0:8].set(gate_b.astype(F32))
    gb_col = gate_b.astype(F32).reshape(8, 1)
    return pl.pallas_call(
        _mlstm_kernel,
        grid=(bsz, nb),
        in_specs=[pl.BlockSpec((tb, 4 * w), lambda b, i: (b * nb + i, OFF_M // (4 * w))),
                  pl.BlockSpec((tb, LANES), lambda b, i: (b * nb + i, OFF_MG // LANES)),
                  pl.BlockSpec((ncb, 8, CHUNK), lambda b, i: (b * nb + i, 0, 0)),
                  pl.BlockSpec((CONV_W, 2 * w), lambda b, i: (0, 0)),
                  pl.BlockSpec((1, 2 * w), lambda b, i: (0, 0)),
                  pl.BlockSpec((1, LANES), lambda b, i: (0, 0)),
                  pl.BlockSpec((8, 1), lambda b, i: (0, 0)),
                  pl.BlockSpec((1, w), lambda b, i: (0, 0))],
        out_specs=pl.BlockSpec((tb, w), lambda b, i: (b * nb + i, 0)),
        out_shape=jax.ShapeDtypeStruct((t, w), F32),
        scratch_shapes=[pltpu.VMEM((w, w), F32), pltpu.VMEM((8, w), F32), pltpu.VMEM((8, LANES), F32),
                        pltpu.VMEM((8, 2 * w), F32), pltpu.VMEM((tb + 8, 2 * w), F32),
                        pltpu.VMEM((tb, 2 * w), F32)],
        compiler_params=_cp("parallel", "arbitrary"),
        name="mlstm",
    )(proj, proj, grow, conv_w.astype(F32), conv_b.astype(F32).reshape(1, 2 * w), gb_row, gb_col,
      gain.reshape(1, w))


NSA_TQ = 128
NSA_KC = 512
GW = 2 * HEAD_DIM


def _nsa_prep_kernel(pq_ref, pk_ref, pv_ref, cos_ref, sin_ref, qg_ref, kg_ref,
                     qn_ref, qr_ref, ks_ref, kw_ref, vst_ref, vwt_ref):
    w = MIX_WIDTH
    for src, dst in ((pv_ref[:, 0:GW], vst_ref), (pv_ref[:, GW:2 * GW], vwt_ref)):
        vt = src.T
        tk = dst.shape[4]
        for g in range(NSA_GROUPS):
            rows = vt[g * HEAD_DIM:(g + 1) * HEAD_DIM, :]
            dup = jnp.concatenate([rows, rows], axis=0).astype(BF16)
            for j in range(dst.shape[2]):
                dst[0, g, j] = dup[:, j * tk:(j + 1) * tk]
    ones_bd = _block_ones(w)
    cos_t, sin_t = cos_ref[...], sin_ref[...]
    scale = HEAD_DIM ** -0.5
    qh = _head_rms(pq_ref[...], qg_ref[...], ones_bd)
    qn_ref[...] = (qh * scale).astype(BF16)
    qr_ref[...] = (_rope(qh, cos_t, sin_t) * scale).astype(BF16)
    ks_ref[...] = _rope(_head_rms(pk_ref[:, 0:w], kg_ref[1:2, :], ones_bd), cos_t, sin_t).astype(BF16)
    kw_ref[...] = _rope(_head_rms(pk_ref[:, w:2 * w], kg_ref[2:3, :], ones_bd), cos_t, sin_t).astype(BF16)


def _nsa_prep(proj, cos4, sin4, qnorm_g, knorm_g, bsz, seq):
    t = bsz * seq
    w = MIX_WIDTH
    tm = min(NSA_KC, seq)
    tq = min(NSA_TQ, seq)
    ns = seq // tm
    qg = jnp.tile(qnorm_g.astype(F32), w // HEAD_DIM).reshape(1, w)
    kg = jnp.concatenate([jnp.tile(knorm_g.astype(F32), (1, w // HEAD_DIM)), jnp.zeros((5, w), F32)], axis=0)
    out = jax.ShapeDtypeStruct((t, w), BF16)
    row = pl.BlockSpec((tm, w), lambda i: (i, 0))
    return pl.pallas_call(
        _nsa_prep_kernel,
        grid=(t // tm,),
        in_specs=[pl.BlockSpec((tm, w), lambda i: (i, OFF_NQ // w)),
                  pl.BlockSpec((tm, 2 * w), lambda i: (i, OFF_KD // (2 * w))),
                  pl.BlockSpec((tm, 2 * GW), lambda i: (i, OFF_V // (2 * GW))),
                  pl.BlockSpec((tm, w), lambda i: (i % ns, 0)),
                  pl.BlockSpec((tm, w), lambda i: (i % ns, 0)),
                  pl.BlockSpec((1, w), lambda i: (0, 0)),
                  pl.BlockSpec((8, w), lambda i: (0, 0))],
        out_specs=[row, row, row, row,
                   pl.BlockSpec((1, NSA_GROUPS, 1, GW, tm), lambda i: (i // ns, 0, i % ns, 0, 0)),
                   pl.BlockSpec((1, NSA_GROUPS, tm // tq, GW, tq), lambda i: (i // ns, 0, i % ns, 0, 0))],
        out_shape=[out, out, out, out,
                   jax.ShapeDtypeStruct((bsz, NSA_GROUPS, seq // tm, GW, tm), BF16),
                   jax.ShapeDtypeStruct((bsz, NSA_GROUPS, seq // tq, GW, tq), BF16)],
        compiler_params=_cp("parallel"),
        name="nsa_prep",
    )(proj, proj, proj, cos4, sin4, qg, kg)


def _nsa_cmp_kernel(xr_ref, pe_ref, w0_ref, w1_ref, kg_ref, ovt_ref, qn_ref, ocmp_ref, sel_ref,
                    kc_ref, vc_ref, v_ref, *, n_top):
    tq = qn_ref.shape[0]
    nr = xr_ref.shape[0]
    nsel = sel_ref.shape[2]
    w = MIX_WIDTH

    @pl.when(pl.program_id(1) == 0)
    def _():
        xr = xr_ref[...]
        y0 = _dot((xr + pe_ref[0]).astype(BF16), w0_ref[...])
        y1 = _dot((xr + pe_ref[1]).astype(BF16), w1_ref[...])
        kv = y0 + pltpu.roll(y1, nr - 1, 0)
        kc_ref[...] = _head_rms(kv[:, 0:w], kg_ref[...], _block_ones(w)).astype(BF16)
        vc_ref[...] = kv[:, w:2 * w].astype(BF16)

    pos0 = pl.program_id(1) * tq
    hl = _head_of_lane((tq, GW), 1)
    pos_r = pos0 + lax.broadcasted_iota(jnp.int32, (tq, nr), 0)
    valid = lax.broadcasted_iota(jnp.int32, (tq, nr), 1) * CMP_STRIDE + (CMP_LEN - 1) <= pos_r
    pos_c = pos0 + lax.broadcasted_iota(jnp.int32, (nr, tq), 1)
    valid_t = lax.broadcasted_iota(jnp.int32, (nr, tq), 0) * CMP_STRIDE + (CMP_LEN - 1) <= pos_c
    jrow = lax.broadcasted_iota(jnp.int32, (nsel, tq), 0)
    cur = (pos0 + lax.broadcasted_iota(jnp.int32, (nsel, tq), 1)) // SEL_BLOCK
    forced = (jrow == 0) | (jrow == cur) | (jrow == cur - 1)
    ovt = ovt_ref[...]

    for g in range(NSA_GROUPS):
        qg = qn_ref[:, g * GW:(g + 1) * GW]
        kg = kc_ref[:, g * GW:(g + 1) * GW]
        vg = vc_ref[:, g * GW:(g + 1) * GW]
        o_g = jnp.zeros((tq, GW), F32)
        pt_sum = jnp.zeros((nr, tq), F32)
        for hh in range(2):
            qm = jnp.where(hl == hh, qg, jnp.zeros_like(qg))
            s = jnp.where(valid, _dot_nt(qm, kg), NEG_INF)
            e = jnp.exp(s - jnp.max(s, axis=1, keepdims=True))
            p = jnp.where(valid, e / jnp.sum(e, axis=1, keepdims=True), 0.0)
            o_g = jnp.where(hl == hh, _dot(p.astype(BF16), vg), o_g)
            st = jnp.where(valid_t, _dot_nt(kg, qm), NEG_INF)
            et = jnp.exp(st - jnp.max(st, axis=0, keepdims=True))
            pt_sum = pt_sum + jnp.where(valid_t, et / jnp.sum(et, axis=0, keepdims=True), 0.0)
        ocmp_ref[:, g * GW:(g + 1) * GW] = o_g
        p_hi = pt_sum.astype(BF16)
        p_lo = (pt_sum - p_hi.astype(F32)).astype(BF16)
        imp = _dot(ovt, p_hi) + _dot(ovt, p_lo)
        val = jnp.where(jrow <= cur, imp + FORCE_BONUS * forced.astype(F32), NEG_INF)
        v_ref[...] = val

        def rank(jp, cnt):
            row = v_ref[pl.ds(jp, 1), :]
            tie = jnp.where(jrow > jp, 1.0, 0.0)
            return cnt + jnp.where(row > val, 1.0, jnp.where(row == val, tie, 0.0))

        cnt = lax.fori_loop(0, nsel, rank, jnp.zeros((nsel, tq), F32))
        sel_ref[0, g] = ((cnt < n_top) & (jrow <= cur)).astype(F32)


def _nsa_cmp_weights(cmp_pe, cmp_w):
    half = CMP_LEN // 2
    wl = cmp_w.astype(F32).reshape(2, 2, half, HEAD_DIM, HEAD_DIM)
    eye2 = jnp.eye(2, dtype=F32)
    w2 = jnp.einsum('kardz,kK,gG,h->arkgdKGhz', wl, eye2, eye2, jnp.ones((2,), F32))
    w2 = w2.reshape(2, half * 4 * HEAD_DIM, 8 * HEAD_DIM)
    pl_ = cmp_pe.astype(F32).reshape(2, 2, half, HEAD_DIM)
    pe2 = jnp.broadcast_to(pl_.transpose(1, 2, 0, 3)[:, :, :, None, :], (2, half, 2, 2, HEAD_DIM))
    return w2.astype(BF16), pe2.reshape(2, 1, half * 4 * HEAD_DIM)


def _nsa_cmp(kcvc, qn, cmp_pe, cmp_w, knorm0, bsz, seq, tq=512):
    t = bsz * seq
    w = MIX_WIDTH
    tq = min(tq, seq)
    nq = seq // tq
    nr = seq // CMP_STRIDE
    nsel = seq // SEL_BLOCK
    n_top = min(SEL_TOPK, nsel)
    w2, pe2 = _nsa_cmp_weights(cmp_pe, cmp_w)
    xr = kcvc.reshape(t // CMP_STRIDE, CMP_STRIDE * w)
    kg = jnp.tile(knorm0.astype(F32), w // HEAD_DIM).reshape(1, w)
    n_i = np.arange(nr)[:, None] * CMP_STRIDE
    j_i = np.arange(nsel)[None, :] * SEL_BLOCK
    ov = ((n_i < j_i + SEL_BLOCK) & (n_i + CMP_LEN > j_i)).astype(np.float32)
    ov[nr - 1, :] = 0.0
    kin = CMP_STRIDE * w
    return pl.pallas_call(
        functools.partial(_nsa_cmp_kernel, n_top=n_top),
        grid=(bsz, nq),
        in_specs=[pl.BlockSpec((nr, kin), lambda b, i: (b, 0)),
                  pl.BlockSpec((2, 1, kin), lambda b, i: (0, 0, 0)),
                  pl.BlockSpec((None, kin, 2 * w), lambda b, i: (0, 0, 0)),
                  pl.BlockSpec((None, kin, 2 * w), lambda b, i: (1, 0, 0)),
                  pl.BlockSpec((1, w), lambda b, i: (0, 0)),
                  pl.BlockSpec((nsel, nr), lambda b, i: (0, 0)),
                  pl.BlockSpec((tq, w), lambda b, i: (b * nq + i, 0))],
        out_specs=[pl.BlockSpec((tq, w), lambda b, i: (b * nq + i, 0)),
                   pl.BlockSpec((1, NSA_GROUPS, nsel, tq), lambda b, i: (b, 0, 0, i))],
        out_shape=[jax.ShapeDtypeStruct((t, w), F32),
                   jax.ShapeDtypeStruct((bsz, NSA_GROUPS, nsel, seq), F32)],
        scratch_shapes=[pltpu.VMEM((nr, w), BF16), pltpu.VMEM((nr, w), BF16), pltpu.VMEM((nsel, tq), F32)],
        compiler_params=_cp("parallel", "arbitrary"),
        name="nsa_cmp",
    )(xr, pe2, w2, w2, kg, jnp.asarray(ov.T, BF16), qn)


def _nsa_attn_kernel(qr_ref, ks_ref, kw_ref, vs_ref, vw_ref, sel_ref, ocmp_ref, gate_ref, o_ref, *, kc, wt):
    tq = qr_ref.shape[0]
    i = pl.program_id(2)
    g = pl.program_id(1)
    hl = _head_of_lane((tq, GW), 1)
    q = qr_ref[...]
    qs = jnp.concatenate([jnp.where(hl == 0, q, jnp.zeros_like(q)), jnp.where(hl == 1, q, jnp.zeros_like(q))], axis=0)
    nbk = kc // SEL_BLOCK

    def lane_qpos(rows):
        return i * tq + lax.broadcasted_iota(jnp.int32, (rows, 2 * tq), 1) % tq

    def finish(acc, l):
        ot = (acc / l).T
        return jnp.where(hl == 0, ot[0:tq, :], ot[tq:2 * tq, :])

    qpos_s = lane_qpos(kc)
    krow_s = lax.broadcasted_iota(jnp.int32, (kc, 2 * tq), 0)

    def sel_body(c, carry):
        m, l, acc = carry
        k0 = pl.multiple_of(c * kc, kc)
        st = _dot_nt(ks_ref[pl.ds(k0, kc), :], qs)
        srows = sel_ref[0, 0, pl.ds(pl.multiple_of(c * nbk, nbk), nbk), :]
        srows = jnp.concatenate([srows, srows], axis=1)
        smask = jnp.concatenate([jnp.broadcast_to(srows[r:r + 1, :], (SEL_BLOCK, 2 * tq)) for r in range(nbk)],
                                axis=0)
        msk = (smask > 0.5) & (k0 + krow_s <= qpos_s)
        st = jnp.where(msk, st, NEG_INF)
        m_new = jnp.maximum(m, jnp.max(st, axis=0, keepdims=True))
        p = jnp.where(msk, jnp.exp(st - m_new), 0.0)
        alpha = jnp.exp(m - m_new)
        l = l * alpha + jnp.sum(p, axis=0, keepdims=True)
        acc = acc * alpha + _dot(vs_ref[0, 0, c], p.astype(BF16))
        return m_new, l, acc

    init = (jnp.full((1, 2 * tq), NEG_INF, F32), jnp.zeros((1, 2 * tq), F32), jnp.zeros((GW, 2 * tq), F32))
    _, l_s, acc_s = lax.fori_loop(0, ((i + 1) * tq + kc - 1) // kc, sel_body, init)
    o_sel = finish(acc_s, l_s)

    j0 = jnp.maximum(i - (wt - 1), 0)
    k0 = pl.multiple_of(j0 * tq, tq)
    span = wt * tq
    st = _dot_nt(kw_ref[pl.ds(k0, span), :], qs)
    kpos = k0 + lax.broadcasted_iota(jnp.int32, (span, 2 * tq), 0)
    qpos_w = lane_qpos(span)
    msk = (kpos <= qpos_w) & (kpos > qpos_w - WINDOW)
    st = jnp.where(msk, st, NEG_INF)
    p = jnp.where(msk, jnp.exp(st - jnp.max(st, axis=0, keepdims=True)), 0.0)
    vt = jnp.concatenate([vw_ref[0, 0, j0 + r] for r in range(wt)], axis=1)
    o_win = finish(_dot(vt, p.astype(BF16)), jnp.sum(p, axis=0, keepdims=True))

    gb = _sigmoid(gate_ref[...])

    def gate(branch):
        cols = []
        for hh in range(2):
            c0 = gb[:, hh * 3 + branch:hh * 3 + branch + 1]
            c1 = gb[:, (2 + hh) * 3 + branch:(2 + hh) * 3 + branch + 1]
            cols.append(jnp.where(g == 0, c0, c1))
        return _expand_heads(cols, (tq, GW))

    o_ref[...] = gate(0) * ocmp_ref[...] + gate(1) * o_sel + gate(2) * o_win


def _nsa_attn(proj, qr, ks, kw, vst, vwt, sel, o_cmp, bsz, seq):
    t = bsz * seq
    w = MIX_WIDTH
    tq = min(NSA_TQ, seq)
    nq = seq // tq
    nsel = seq // SEL_BLOCK
    kc = min(NSA_KC, seq)
    wt = min(WINDOW // tq + 1, nq)
    kspec = pl.BlockSpec((seq, GW), lambda b, g, i: (b, g))
    return pl.pallas_call(
        functools.partial(_nsa_attn_kernel, kc=kc, wt=wt),
        grid=(bsz, NSA_GROUPS, nq),
        in_specs=[pl.BlockSpec((tq, GW), lambda b, g, i: (b * nq + i, g)),
                  kspec, kspec,
                  pl.BlockSpec((1, 1, seq // kc, GW, kc), lambda b, g, i: (b, g, 0, 0, 0)),
                  pl.BlockSpec((1, 1, nq, GW, tq), lambda b, g, i: (b, g, 0, 0, 0)),
                  pl.BlockSpec((1, 1, nsel, tq), lambda b, g, i: (b, g, 0, i)),
                  pl.BlockSpec((tq, GW), lambda b, g, i: (b * nq + i, g)),
                  pl.BlockSpec((tq, LANES), lambda b, g, i: (b * nq + i, OFF_NG // LANES))],
        out_specs=pl.BlockSpec((tq, GW), lambda b, g, i: (b * nq + i, g)),
        out_shape=jax.ShapeDtypeStruct((t, w), F32),
        compiler_params=_cp("parallel", "parallel", "arbitrary"),
        name="nsa_attn",
    )(qr, ks, kw, vst, vwt, sel, o_cmp, proj)


def _nsa(proj, kcvc, cos4, sin4, qnorm_g, knorm_g, cmp_pe, cmp_w, bsz, seq):
    qn, qr, ks, kw, vst, vwt = _nsa_prep(proj, cos4, sin4, qnorm_g, knorm_g, bsz, seq)
    o_cmp, sel = _nsa_cmp(kcvc, qn, cmp_pe, cmp_w, knorm_g[0], bsz, seq)
    return _nsa_attn(proj, qr, ks, kw, vst, vwt, sel, o_cmp, bsz, seq)


PEER_TT = 128
PEER_CT = 8
HALF_D = 512


SUBLANES = 8
CODE_BITS = 127
FAR_BELOW = -3.0e38


def _with_code(x, code):
    bits = lax.bitcast_convert_type(x, jnp.int32)
    return lax.bitcast_convert_type((bits & ~CODE_BITS) | code, F32)


def _split_code(x):
    bits = lax.bitcast_convert_type(x, jnp.int32)
    return lax.bitcast_convert_type(bits & ~CODE_BITS, F32), bits & CODE_BITS


def _sort16_desc(xs):
    xs = list(xs)
    n = len(xs)
    k = 2
    while k <= n:
        j = k // 2
        while j >= 1:
            for i in range(n):
                l = i ^ j
                if l > i:
                    hi, lo = jnp.maximum(xs[i], xs[l]), jnp.minimum(xs[i], xs[l])
                    xs[i], xs[l] = (hi, lo) if (i & k) == 0 else (lo, hi)
            j //= 2
        k *= 2
    return xs


def _merge16_desc(xs):
    xs = list(xs)
    j = len(xs) // 2
    while j >= 1:
        for i in range(len(xs)):
            l = i ^ j
            if l > i:
                xs[i], xs[l] = jnp.maximum(xs[i], xs[l]), jnp.minimum(xs[i], xs[l])
        j //= 2
    return xs


def _top16_columns(x):
    n = PEER_TOPK
    xs = _sort16_desc([x[SUBLANES * j:SUBLANES * (j + 1), :] for j in range(n)])
    shift = SUBLANES // 2
    while shift >= 1:
        rolled = [pltpu.roll(a, shift, 0) for a in xs]
        xs = _merge16_desc([jnp.maximum(xs[i], rolled[n - 1 - i]) for i in range(n)])
        shift //= 2
    return xs


_PEER_CAND_TILES = ((0, 0, 8), (0, 1, 8), (1, 0, 8), (2, 0, 5), (3, 0, 4), (4, 0, 3), (5, 0, 2), (6, 0, 2), (7, 0, 2))


def _peer_route_kernel(q_ref, key_ref, e_ref, g_ref):
    tt = q_ref.shape[0]
    nk = PEER_NKEYS
    n = PEER_TOPK
    row = lax.broadcasted_iota(jnp.int32, (nk, tt), 0)
    sub = lax.broadcasted_iota(jnp.int32, (SUBLANES, tt), 0)
    vals, ids = [], []
    for p in range(2):
        st = _dot_nt(key_ref[0, p], q_ref[:, p * PEER_KDIM:(p + 1) * PEER_KDIM])
        top = [_split_code(a) for a in _top16_columns(_with_code(st, (nk - 1) - row))]
        vals.append([v for v, _ in top])
        ids.append([(nk - 1) - c for _, c in top])
    (v1, v2), (i1, i2) = vals, ids

    def stack(xs, lo):
        out = xs[lo]
        for s in range(1, SUBLANES):
            out = jnp.where(sub == s, xs[lo + s], out)
        return out

    v2t, i2t = (stack(v2, 0), stack(v2, SUBLANES)), (stack(i2, 0), stack(i2, SUBLANES))
    cand, cexp = [], []
    for a, tile, nvalid in _PEER_CAND_TILES:
        v = v1[a] + v2t[tile]
        cand.append(v if nvalid == SUBLANES else jnp.where(sub < nvalid, v, FAR_BELOW))
        cexp.append(i1[a] * nk + i2t[tile])
    cand.append(stack(v1, SUBLANES) + v2[0])
    cexp.append(stack(i1, SUBLANES) * nk + i2[0])
    n_tiles = len(cand)
    slot_code = [(nk - 1) - (c * SUBLANES + sub) for c in range(n_tiles)]
    coded = [_with_code(v, sc) for v, sc in zip(cand, slot_code)]
    coded += [jnp.full((SUBLANES, tt), FAR_BELOW, F32)] * (n - n_tiles)
    top = [_split_code(a) for a in _top16_columns(jnp.concatenate(coded, axis=0))]
    call = jnp.concatenate(cexp, axis=0)
    slot = (nk - 1) - lax.broadcasted_iota(jnp.int32, call.shape, 0)
    ex = [jnp.exp(v - top[0][0]) for v, _ in top]
    tot = ex[0]
    for k in range(1, n):
        tot = tot + ex[k]
    for k in range(n):
        hit = slot == jnp.concatenate([top[k][1]] * n_tiles, axis=0)
        e_ref[0, k:k + 1, :] = jnp.sum(jnp.where(hit, call, 0), axis=0, keepdims=True)
        g_ref[0, k:k + 1, :] = (ex[k] / tot)[0:1, :]


def _peer_route(qp, keys):
    t = qp.shape[0]
    tt = min(PEER_TT, t)
    return pl.pallas_call(
        _peer_route_kernel,
        grid=(t // tt, PEER_HEADS),
        in_specs=[pl.BlockSpec((tt, 2 * PEER_KDIM), lambda i, h: (i, h)),
                  pl.BlockSpec((1, 2, PEER_NKEYS, PEER_KDIM), lambda i, h: (h, 0, 0, 0))],
        out_specs=[pl.BlockSpec((1, PEER_TOPK, tt), lambda i, h: (h, 0, i)),
                   pl.BlockSpec((1, PEER_TOPK, tt), lambda i, h: (h, 0, i))],
        out_shape=[jax.ShapeDtypeStruct((PEER_HEADS, PEER_TOPK, t), jnp.int32),
                   jax.ShapeDtypeStruct((PEER_HEADS, PEER_TOPK, t), F32)],
        compiler_params=_cp("parallel", "arbitrary"),
        name="peer_route",
    )(qp, keys)


def _pack_tables_kernel(u_ref, v_ref, o_ref):
    def pack(x):
        lo = lax.bitcast_convert_type(x[:, 0:HALF_D].astype(BF16).astype(F32), jnp.int32)
        hi = lax.bitcast_convert_type(x[:, HALF_D:2 * HALF_D].astype(BF16).astype(F32), jnp.int32)
        return lax.shift_right_logical(lo, 16) | (hi & jnp.int32(-65536))

    o_ref[:, 0:HALF_D] = pack(u_ref[...])
    o_ref[:, HALF_D:2 * HALF_D] = pack(v_ref[...])


def _pack_tables(u_tabs, v_tabs, layer, tr=512):
    _, e, d = u_tabs.shape
    assert d == 2 * HALF_D
    spec_in = pl.BlockSpec((None, tr, d), lambda i: (layer, i, 0))
    spec = pl.BlockSpec((tr, d), lambda i: (i, 0))
    return pl.pallas_call(
        _pack_tables_kernel,
        grid=(e // tr,),
        in_specs=[spec_in, spec_in],
        out_specs=spec,
        out_shape=jax.ShapeDtypeStruct((e, d), jnp.int32),
        compiler_params=_cp("parallel"),
        name="peer_pack",
    )(u_tabs, v_tabs)


def _unpack_rows(wd):
    lo = lax.bitcast_convert_type(lax.shift_left(wd, 16), F32)
    hi = lax.bitcast_convert_type(lax.bitwise_and(wd, jnp.int32(-65536)), F32)
    return lo, hi


SC_WINDOW = 32


def _sc_gather(table, idx):
    from jax.experimental.pallas import tpu_sc as plsc
    n = idx.shape[0]
    width = table.shape[1]
    mesh = plsc.VectorSubcoreMesh(core_axis_name="core", subcore_axis_name="subcore")

    @functools.partial(pl.kernel, out_type=jax.ShapeDtypeStruct((n, width), table.dtype), mesh=mesh)
    def gather(tab_hbm, idx_hbm, out_hbm):
        def body(idx_vmem, out_vmem):
            pltpu.sync_copy(tab_hbm.at[idx_vmem.at[0, pl.ds(0, SC_WINDOW)]], out_vmem)

        pltpu.emit_pipeline(
            body,
            grid=(n // SC_WINDOW,),
            in_specs=[pl.BlockSpec((1, LANES), lambda i: (0, i))],
            out_specs=[pl.BlockSpec((SC_WINDOW, width), lambda i: (i, 0))],
            core_axis_name=("core", "subcore"),
            dimension_semantics=(pltpu.PARALLEL,),
            trace_scopes=False,
        )(idx_hbm, out_hbm)

    idx_pad = jnp.pad(idx.reshape(n // SC_WINDOW, SC_WINDOW), ((0, 0), (0, LANES - SC_WINDOW)))
    return gather(table, idx_pad.reshape(1, (n // SC_WINDOW) * LANES))


def _peer_combine_kernel(x_ref, g2_ref, rows_ref, gate_ref, o_ref):
    ne = PEER_HEADS * PEER_TOPK
    x = x_ref[...]
    ct = x.shape[0]
    xn = x * lax.rsqrt(jnp.mean(x * x, axis=-1, keepdims=True) + NORM_EPS) * g2_ref[...]
    gate_t = jnp.concatenate([gate_ref[...]] * (ne // ct), axis=0).T
    for j in range(ct):
        u_lo, u_hi = _unpack_rows(rows_ref[j * ne:(j + 1) * ne, 0:HALF_D])
        xr = xn[j:j + 1, :]
        h = jnp.sum(u_lo * xr[:, 0:HALF_D] + u_hi * xr[:, HALF_D:2 * HALF_D], axis=1, keepdims=True)
        act = 0.5 * h * (1.0 + lax.erf(h * (2.0 ** -0.5)))
        wgt = gate_t[:, j:j + 1] * act
        v_lo, v_hi = _unpack_rows(rows_ref[j * ne:(j + 1) * ne, HALF_D:2 * HALF_D])
        o_ref[j:j + 1, 0:HALF_D] = x[j:j + 1, 0:HALF_D] + jnp.sum(wgt * v_lo, axis=0, keepdims=True)
        o_ref[j:j + 1, HALF_D:2 * HALF_D] = x[j:j + 1, HALF_D:2 * HALF_D] + jnp.sum(wgt * v_hi, axis=0, keepdims=True)


def _peer_combine(x, g2, rows, gates, first_token):
    t, d = x.shape
    ne = PEER_HEADS * PEER_TOPK
    ct = PEER_CT
    steps = rows.shape[0] // (ct * ne)
    off = first_token // ct
    return pl.pallas_call(
        _peer_combine_kernel,
        grid=(steps,),
        in_specs=[pl.BlockSpec((ct, d), lambda i: (off + i, 0)),
                  pl.BlockSpec((1, d), lambda i: (0, 0)),
                  pl.BlockSpec((ct * ne, d), lambda i: (i, 0)),
                  pl.BlockSpec((ct, ne), lambda i: (off + i, 0))],
        out_specs=pl.BlockSpec((ct, d), lambda i: (off + i, 0)),
        out_shape=jax.ShapeDtypeStruct((t, d), F32),
        input_output_aliases={0: 0},
        compiler_params=_cp("parallel"),
        name="peer_combine",
    )(x, g2.reshape(1, d), rows, gates)


PEER_TOKENS_PER_GATHER = 2048


def _peer_route_stage(x, g2, wq_b, keys_b):
    t = x.shape[0]
    ne = PEER_HEADS * PEER_TOPK
    qp = _norm_matmul(x, g2, wq_b, out_dtype=BF16)
    e_t, g_t = _peer_route(qp, keys_b)
    return e_t.reshape(ne, t).T.reshape(t * ne), g_t.reshape(ne, t).T


def _peer_gather_stage(table, idx, t, gather_fn):
    ne = PEER_HEADS * PEER_TOPK
    tc = min(PEER_TOKENS_PER_GATHER, t)
    return [gather_fn(table, idx[c * tc * ne:(c + 1) * tc * ne]) for c in range(t // tc)]


def _peer_combine_stage(x, g2, rows_list, gates):
    tc = x.shape[0] // len(rows_list)
    for c, rows in enumerate(rows_list):
        x = _peer_combine(x, g2, rows, gates, c * tc)
    return x


def _peer(x, g2, wq, keys, u_tabs, v_tabs, layer, gather_fn):
    idx, gates = _peer_route_stage(x, g2, wq.astype(BF16), keys.astype(BF16))
    table = _pack_tables(u_tabs, v_tabs, layer)
    rows_list = _peer_gather_stage(table, idx, x.shape[0], gather_fn)
    return _peer_combine_stage(x, g2, rows_list, gates)


_IN_WIDTHS = (256, 256, 256, 256, 256, 256, 256, 4, 4, 256, 256, 128, 128, 128, 128, 128, 128, 12,
              256, 256, 256, 256)


def _dup_groups(wcols):
    g0, g1 = wcols[:, :HEAD_DIM], wcols[:, HEAD_DIM:]
    return jnp.concatenate([g0, g0, g1, g1], axis=1)


def _layout_w_in(w_in):
    offs = np.cumsum((0,) + _IN_WIDTHS)
    cols = [w_in[:, offs[i]:offs[i + 1]] for i in range(len(_IN_WIDTHS))]
    (hq, hf, hi, hg, mq, mk, mv, mi, mf, mo, nq, nkc, nvc, nks, nvs, nkw, nvw, ng, rq, rk, rv, rg) = cols
    d = w_in.shape[0]
    pad = lambda c, n: jnp.concatenate([c, jnp.zeros((d, n - c.shape[1]), w_in.dtype)], axis=1)
    main = jnp.concatenate([hq, hf, hi, hg, mq, mk, mv, mo, rq, rk, rv, rg,
                            _dup_groups(nks), _dup_groups(nkw), nq, nvs, nvw,
                            pad(jnp.concatenate([mi, mf], axis=1), LANES), pad(ng, LANES)], axis=1)
    assert main.shape[1] == N_MAIN
    kcvc = jnp.concatenate([nkc, nvc], axis=1)
    return main.astype(BF16), kcvc.astype(BF16)


def kernel(x, norm1_g, w_in, hgrn_lb, hgrn_onorm_g, mlstm_conv_w, mlstm_conv_b, mlstm_gate_b, mlstm_onorm_g, nsa_qnorm_g, nsa_knorm_g, nsa_cmp_pe, nsa_cmp_w, ret_onorm_g, w_up, w_gate, w_out, norm2_g, peer_wq, peer_keys, peer_u, peer_v):
    bsz, seq, d = x.shape
    t = bsz * seq
    depth = w_in.shape[0]
    cos_t, sin_t = _rope_lane_tables(seq)
    cos4, sin4 = jnp.tile(cos_t, (1, 2)), jnp.tile(sin_t, (1, 2))
    lb_cum = jnp.cumsum(jax.nn.softmax(hgrn_lb.astype(F32), axis=0), axis=0)
    lb_all = lb_cum - lb_cum[0:1]
    weights = []
    for l in range(depth):
        w_main, w_kcvc = _layout_w_in(w_in[l])
        weights.append(dict(
            main=w_main, kcvc=w_kcvc, gate=w_gate[l].astype(BF16), up=w_up[l].astype(BF16),
            out=w_out[l].astype(BF16), lb=_hgrn_lb_rows(lb_all[l]), wq=peer_wq[l].astype(BF16),
            keys=peer_keys[l].astype(BF16), table=_pack_tables(peer_u, peer_v, l)))

    def mix_and_route(xh, l, nb):
        wl = weights[l]
        proj = _norm_matmul(xh, norm1_g[l], wl["main"])
        gates = _norm_matmul(xh, norm1_g[l], wl["gate"], act="sigmoid", out_dtype=BF16)
        o_h = _hgrn(proj, wl["lb"], hgrn_onorm_g[l], nb, seq)
        o_m = _mlstm(proj, mlstm_conv_w[l], mlstm_conv_b[l], mlstm_gate_b[l], mlstm_onorm_g[l], nb, seq)
        o_r = _ret(proj, cos4, sin4, ret_onorm_g[l], nb, seq)
        kcvc = _norm_matmul(xh, norm1_g[l], wl["kcvc"])
        o_n = _nsa(proj, kcvc, cos4, sin4, nsa_qnorm_g[l], nsa_knorm_g[l], nsa_cmp_pe[l], nsa_cmp_w[l], nb, seq)
        xm = _merge(xh, gates, (o_h, o_m, o_n, o_r), wl["up"], wl["out"])
        idx, pgates = _peer_route_stage(xm, norm2_g[l], wl["wq"], wl["keys"])
        return xm, idx, pgates

    n_groups = 2 if bsz % 2 == 0 else 1
    nb = bsz // n_groups
    xs = [x[g * nb:(g + 1) * nb].reshape(nb * seq, d) for g in range(n_groups)]
    pending = None
    for l in range(depth):
        for g in range(n_groups):
            xm, idx, pgates = mix_and_route(xs[g], l, nb)
            rows_list = _peer_gather_stage(weights[l]["table"], idx, nb * seq, _sc_gather)
            if pending is not None:
                pg, pl_, pxm, prows, pgt = pending
                xs[pg] = _peer_combine_stage(pxm, norm2_g[pl_], prows, pgt)
            pending = (g, l, xm, rows_list, pgates)
            if n_groups == 1:
                xs[g] = _peer_combine_stage(xm, norm2_g[l], rows_list, pgates)
                pending = None
    if pending is not None:
        pg, pl_, pxm, prows, pgt = pending
        xs[pg] = _peer_combine_stage(pxm, norm2_g[pl_], prows, pgt)
    return jnp.concatenate(xs, axis=0).reshape(bsz, seq, d)
```

```python
import functools
import math

import numpy as np
import jax
import jax.numpy as jnp
from jax import lax
from jax.experimental import pallas as pl
from jax.experimental.pallas import tpu as pltpu

F32 = jnp.float32
BF16 = jnp.bfloat16

HEAD_DIM = 64
N_HEADS = 4
MIX_WIDTH = N_HEADS * HEAD_DIM
CHUNK = 64
NORM_EPS = 1e-6
NEG_INF = -1e30
ROPE_THETA = 10000.0
CONV_W = 4
NSA_GROUPS = 2
CMP_LEN = 32
CMP_STRIDE = 16
SEL_BLOCK = 64
SEL_TOPK = 16
WINDOW = 512
FORCE_BONUS = 1e3
PEER_HEADS = 8
PEER_NKEYS = 128
PEER_TOPK = 16
PEER_KDIM = 128

LANES = 128
VMEM_LIMIT = 48 * 1024 * 1024

OFF_H, OFF_M, OFF_R, OFF_KD, OFF_NQ, OFF_V, OFF_MG, OFF_NG = 0, 1024, 2048, 3072, 3584, 3840, 4096, 4224
N_MAIN = 4352


def _cp(*sem):
    return pltpu.CompilerParams(dimension_semantics=sem, vmem_limit_bytes=VMEM_LIMIT)


def _dot(a, b):
    return jnp.dot(a, b, preferred_element_type=F32)


def _dot_nt(a, b):
    return lax.dot_general(a, b, (((1,), (1,)), ((), ())), preferred_element_type=F32)


def _dot_tn(a, b):
    return lax.dot_general(a, b, (((0,), (0,)), ((), ())), preferred_element_type=F32)


def _split3(x):
    hi = x.astype(BF16)
    r1 = x - hi.astype(F32)
    mid = r1.astype(BF16)
    lo = (r1 - mid.astype(F32)).astype(BF16)
    return hi, mid, lo


def _dot01_l(m01, x):
    hi, mid, lo = _split3(x)
    return _dot(m01, hi) + _dot(m01, mid) + _dot(m01, lo)


def _dot01_r(x, m01):
    hi, mid, lo = _split3(x)
    return _dot(hi, m01) + _dot(mid, m01) + _dot(lo, m01)


def _head_of_lane(shape, axis):
    return lax.broadcasted_iota(jnp.int32, shape, axis) // HEAD_DIM


def _block_ones(n, dtype=BF16):
    r = lax.broadcasted_iota(jnp.int32, (n, n), 0) // HEAD_DIM
    c = lax.broadcasted_iota(jnp.int32, (n, n), 1) // HEAD_DIM
    return (r == c).astype(dtype)


def _group_sum(x, ones_bd):
    hi = x.astype(BF16)
    lo = (x - hi.astype(F32)).astype(BF16)
    return _dot(hi, ones_bd) + _dot(lo, ones_bd)


def _head_rms(x, gain, ones_bd):
    ms = _group_sum(x * x, ones_bd) * (1.0 / HEAD_DIM)
    return x * lax.rsqrt(ms + NORM_EPS) * gain


def _sigmoid(x):
    return 1.0 / (1.0 + jnp.exp(-x))


def _silu(x):
    return x * _sigmoid(x)


def _log_sigmoid(x):
    return jnp.minimum(x, 0.0) - jnp.log(1.0 + jnp.exp(-jnp.abs(x)))


def _stack_heads(x, n_heads=N_HEADS):
    hl = _head_of_lane(x.shape, 1)
    return jnp.concatenate([jnp.where(hl == h, x, jnp.zeros_like(x)) for h in range(n_heads)], axis=0)


def _unstack_heads(r, c, n_heads=N_HEADS):
    hl = _head_of_lane((c, r.shape[1]), 1)
    out = jnp.zeros((c, r.shape[1]), F32)
    for h in range(n_heads):
        out = jnp.where(hl == h, r[h * c:(h + 1) * c, :], out)
    return out


def _rope(x, cos_t, sin_t):
    n = x.shape[1]
    first = (lax.broadcasted_iota(jnp.int32, x.shape, 1) % HEAD_DIM) < (HEAD_DIM // 2)
    partner = jnp.where(first, pltpu.roll(x, n - HEAD_DIM // 2, 1), pltpu.roll(x, HEAD_DIM // 2, 1))
    return x * cos_t + partner * sin_t


def _after_kernel(a_ref, dep_ref, o_ref):
    del dep_ref
    o_ref[...] = a_ref[...]


def _after(a, dep):
    if dep is None:
        return a
    a2 = a.reshape(1, a.size)
    out = pl.pallas_call(
        _after_kernel,
        in_specs=[pl.BlockSpec(a2.shape, lambda: (0, 0)), pl.BlockSpec(memory_space=pl.ANY)],
        out_specs=pl.BlockSpec(a2.shape, lambda: (0, 0)),
        out_shape=jax.ShapeDtypeStruct(a2.shape, a2.dtype),
        name="order_after",
    )(a2, dep)
    return out.reshape(a.shape)


def _norm_matmul_kernel(x_ref, g_ref, w_ref, o_ref, xn_ref, *, act):
    @pl.when(pl.program_id(1) == 0)
    def _():
        x = x_ref[...]
        ms = jnp.mean(x * x, axis=-1, keepdims=True)
        xn_ref[...] = (x * lax.rsqrt(ms + NORM_EPS) * g_ref[...]).astype(BF16)

    y = _dot(xn_ref[...], w_ref[...])
    if act == "sigmoid":
        y = _sigmoid(y)
    o_ref[...] = y.astype(o_ref.dtype)


def _norm_matmul(x, g, w, *, act=None, out_dtype=F32, tm=1024, tn=512):
    t, d = x.shape
    w3 = w if w.ndim == 3 else w[None]
    n_per = w3.shape[2]
    tm = min(tm, t)
    tn = next(c for c in (tn, 256, 128) if n_per % c == 0)
    per = n_per // tn
    n = w3.shape[0] * n_per
    assert t % tm == 0
    return pl.pallas_call(
        functools.partial(_norm_matmul_kernel, act=act),
        grid=(t // tm, n // tn),
        in_specs=[pl.BlockSpec((tm, d), lambda i, j: (i, 0)),
                  pl.BlockSpec((1, d), lambda i, j: (0, 0)),
                  pl.BlockSpec((None, d, tn), lambda i, j: (j // per, 0, j % per))],
        out_specs=pl.BlockSpec((tm, tn), lambda i, j: (i, j)),
        out_shape=jax.ShapeDtypeStruct((t, n), out_dtype),
        scratch_shapes=[pltpu.VMEM((tm, d), BF16)],
        compiler_params=_cp("parallel", "arbitrary"),
        name="norm_matmul",
    )(x, g.reshape(1, d), w3)


def _merge_kernel(x_ref, gate_ref, oh_ref, om_ref, on_ref, or_ref, wup_ref, wout_ref, o_ref):
    d = x_ref.shape[1]
    acc = None
    for m, r in enumerate((oh_ref, om_ref, on_ref, or_ref)):
        up = _dot(r[...].astype(BF16), wup_ref[m])
        term = gate_ref[:, m * d:(m + 1) * d].astype(F32) * up
        acc = term if acc is None else acc + term
    o_ref[...] = x_ref[...] + _dot(acc.astype(BF16), wout_ref[...])


def _merge(x, gates, outs, w_up, w_out, tm=512):
    t, d = x.shape
    tm = min(tm, t)
    mix = pl.BlockSpec((tm, MIX_WIDTH), lambda i: (i, 0))
    return pl.pallas_call(
        _merge_kernel,
        grid=(t // tm,),
        in_specs=[pl.BlockSpec((tm, d), lambda i: (i, 0)),
                  pl.BlockSpec((tm, 4 * d), lambda i: (i, 0)),
                  mix, mix, mix, mix,
                  pl.BlockSpec((4, MIX_WIDTH, d), lambda i: (0, 0, 0)),
                  pl.BlockSpec((d, d), lambda i: (0, 0))],
        out_specs=pl.BlockSpec((tm, d), lambda i: (i, 0)),
        out_shape=jax.ShapeDtypeStruct((t, d), F32),
        compiler_params=_cp("parallel"),
        name="merge",
    )(x, gates, *outs, w_up, w_out)


REC_BLOCK = 256


def _chunk_consts():
    t = lax.broadcasted_iota(jnp.int32, (CHUNK, CHUNK), 0)
    s = lax.broadcasted_iota(jnp.int32, (CHUNK, CHUNK), 1)
    return t, s


def _hgrn_levels():
    t = np.arange(CHUNK)
    rows = []
    masks = []
    h = CHUNK // 2
    while h >= 1:
        ref = (t // (2 * h)) * (2 * h) + h
        p = np.zeros((CHUNK, CHUNK), np.float32)
        p[t, np.minimum(ref, CHUNK - 1)] = 1.0
        rows.append(p)
        same = (t[:, None] // (2 * h)) == (t[None, :] // (2 * h))
        m = same & ((t[:, None] // h) % 2 == 1) & ((t[None, :] // h) % 2 == 0)
        masks.append(m.astype(np.float32))
        h //= 2
    masks.append(np.eye(CHUNK, dtype=np.float32))
    return np.concatenate(rows, 0), np.stack(masks, 0)


def _hgrn_kernel(p_ref, lb_ref, g_ref, psel_ref, lmask_ref, o_ref, st_ref):
    @pl.when(pl.program_id(1) == 0)
    def _():
        st_ref[...] = jnp.zeros_like(st_ref)

    c = CHUNK
    w = MIX_WIDTH
    ones_bd = _block_ones(w)
    bd_mask = _block_ones(w, F32)
    tri = (lax.broadcasted_iota(jnp.int32, (c, c), 0) >= lax.broadcasted_iota(jnp.int32, (c, c), 1)).astype(BF16)
    psel = psel_ref[...]
    n_lv = lmask_ref.shape[0]
    log_lb, log_1mlb, one_mlb = lb_ref[0:1, :], lb_ref[1:2, :], lb_ref[2:3, :]
    gain = g_ref[...]

    def chunk(ci, carry):
        r0 = pl.multiple_of(ci * c, c)
        q = _silu(p_ref[pl.ds(r0, c), 0:w])
        fl = p_ref[pl.ds(r0, c), w:2 * w]
        v = p_ref[pl.ds(r0, c), 2 * w:3 * w]
        gp = p_ref[pl.ds(r0, c), 3 * w:4 * w]
        a1 = jnp.broadcast_to(log_lb, fl.shape)
        a2 = log_1mlb + _log_sigmoid(fl)
        mx = jnp.maximum(a1, a2)
        log_f = mx + jnp.log(jnp.exp(a1 - mx) + jnp.exp(a2 - mx))
        k = one_mlb * _sigmoid(-fl)
        b = _dot01_l(tri, log_f)
        bref = _dot01_l(psel, b)
        vb = v.astype(BF16)
        a = jnp.zeros((N_HEADS * c, c), F32)
        for lv in range(n_lv):
            if lv < n_lv - 1:
                br = bref[lv * c:(lv + 1) * c, :]
                qs = q * jnp.exp(jnp.minimum(b - br, 0.0))
                ks = k * jnp.exp(jnp.minimum(br - b, 0.0))
            else:
                qs, ks = q, k
            s_lv = _dot_nt(_stack_heads(qs).astype(BF16), ks.astype(BF16))
            a = a + jnp.concatenate([lmask_ref[lv]] * N_HEADS, axis=0) * s_lv
        o = _unstack_heads(_dot(a.astype(BF16), vb), c)
        st = st_ref[...]
        o = o + _dot_nt((q * jnp.exp(b)).astype(BF16), st.astype(BF16))
        b_last = b[c - 1:c, :]
        kb = k * jnp.exp(b_last - b)
        st_ref[...] = st * jnp.exp(b_last) + bd_mask * _dot_tn(vb, kb.astype(BF16))
        y = _head_rms(o, gain, ones_bd) * _silu(gp)
        o_ref[pl.ds(r0, c), :] = y
        return carry

    lax.fori_loop(0, p_ref.shape[0] // c, chunk, 0)


def _hgrn(proj, lb_rows, gain, bsz, seq):
    psel, lmask = _hgrn_levels()
    tb = min(REC_BLOCK, seq)
    nb = seq // tb
    return pl.pallas_call(
        _hgrn_kernel,
        grid=(bsz, nb),
        in_specs=[pl.BlockSpec((tb, 4 * MIX_WIDTH), lambda b, i: (b * nb + i, OFF_H // (4 * MIX_WIDTH))),
                  pl.BlockSpec((8, MIX_WIDTH), lambda b, i: (0, 0)),
                  pl.BlockSpec((1, MIX_WIDTH), lambda b, i: (0, 0)),
                  pl.BlockSpec(psel.shape, lambda b, i: (0, 0)),
                  pl.BlockSpec(lmask.shape, lambda b, i: (0, 0, 0))],
        out_specs=pl.BlockSpec((tb, MIX_WIDTH), lambda b, i: (b * nb + i, 0)),
        out_shape=jax.ShapeDtypeStruct((bsz * seq, MIX_WIDTH), F32),
        scratch_shapes=[pltpu.VMEM((MIX_WIDTH, MIX_WIDTH), F32)],
        compiler_params=_cp("parallel", "arbitrary"),
        name="hgrn2",
    )(proj, lb_rows, gain.reshape(1, MIX_WIDTH), jnp.asarray(psel, BF16), jnp.asarray(lmask, F32))


def _ret_kernel(p_ref, cos_ref, sin_ref, dec_ref, decin_ref, g_ref, o_ref, st_ref):
    @pl.when(pl.program_id(1) == 0)
    def _():
        st_ref[...] = jnp.zeros_like(st_ref)

    c = CHUNK
    w = MIX_WIDTH
    ones_bd = _block_ones(w)
    bd_mask = _block_ones(w, F32)
    gain = g_ref[...]
    dec_q = dec_ref[0:c, :]
    dec_k = dec_ref[c:2 * c, :]
    dec_state = dec_ref[2 * c:2 * c + 1, :]
    dec_in = decin_ref[...]

    def chunk(ci, carry):
        r0 = pl.multiple_of(ci * c, c)
        cos_t = cos_ref[pl.ds(r0, c), :]
        sin_t = sin_ref[pl.ds(r0, c), :]
        q = _rope(p_ref[pl.ds(r0, c), 0:w], cos_t, sin_t)
        k = _rope(p_ref[pl.ds(r0, c), w:2 * w], cos_t, sin_t) * (HEAD_DIM ** -0.5)
        v = p_ref[pl.ds(r0, c), 2 * w:3 * w]
        gp = p_ref[pl.ds(r0, c), 3 * w:4 * w]
        vb = v.astype(BF16)
        a = _dot_nt(_stack_heads(q).astype(BF16), k.astype(BF16)) * dec_in
        o = _unstack_heads(_dot(a.astype(BF16), vb), c)
        st = st_ref[...]
        o = o + _dot_nt(q.astype(BF16), st.astype(BF16)) * dec_q
        st_ref[...] = st * dec_state + bd_mask * _dot_tn(vb, (k * dec_k).astype(BF16))
        o_ref[pl.ds(r0, c), :] = _head_rms(o, gain, ones_bd) * _silu(gp)
        return carry

    lax.fori_loop(0, p_ref.shape[0] // c, chunk, 0)


def _ret_consts():
    log_gamma = np.log1p(-np.exp2(-5.0 - np.arange(N_HEADS, dtype=np.float64)))
    t = np.arange(CHUNK, dtype=np.float64)
    lane_h = np.arange(MIX_WIDTH) // HEAD_DIM
    dec_q = np.exp(log_gamma[lane_h][None, :] * (t[:, None] + 1.0))
    dec_k = np.exp(log_gamma[lane_h][None, :] * (CHUNK - 1.0 - t[:, None]))
    dec_state = np.exp(log_gamma[lane_h] * CHUNK)[None, :]
    dec = np.concatenate([dec_q, dec_k, np.broadcast_to(dec_state, (8, MIX_WIDTH))], 0)
    diff = t[:, None] - t[None, :]
    dec_in = np.concatenate([np.where(diff >= 0, np.exp(log_gamma[h] * diff), 0.0) for h in range(N_HEADS)], 0)
    return dec.astype(np.float32), dec_in.astype(np.float32)


def _ret(proj, cos4, sin4, gain, bsz, seq):
    dec, dec_in = _ret_consts()
    tb = min(REC_BLOCK, seq)
    nb = seq // tb
    return pl.pallas_call(
        _ret_kernel,
        grid=(bsz, nb),
        in_specs=[pl.BlockSpec((tb, 4 * MIX_WIDTH), lambda b, i: (b * nb + i, OFF_R // (4 * MIX_WIDTH))),
                  pl.BlockSpec((tb, MIX_WIDTH), lambda b, i: (i, 0)),
                  pl.BlockSpec((tb, MIX_WIDTH), lambda b, i: (i, 0)),
                  pl.BlockSpec(dec.shape, lambda b, i: (0, 0)),
                  pl.BlockSpec(dec_in.shape, lambda b, i: (0, 0)),
                  pl.BlockSpec((1, MIX_WIDTH), lambda b, i: (0, 0))],
        out_specs=pl.BlockSpec((tb, MIX_WIDTH), lambda b, i: (b * nb + i, 0)),
        out_shape=jax.ShapeDtypeStruct((bsz * seq, MIX_WIDTH), F32),
        scratch_shapes=[pltpu.VMEM((MIX_WIDTH, MIX_WIDTH), F32)],
        compiler_params=_cp("parallel", "arbitrary"),
        name="retention",
    )(proj, cos4, sin4, jnp.asarray(dec), jnp.asarray(dec_in), gain.reshape(1, MIX_WIDTH))


def _hgrn_lb_rows(lb):
    lb = lb.astype(F32)
    rows = jnp.stack([jnp.log(lb), jnp.log1p(-lb), 1.0 - lb], 0)
    return jnp.concatenate([rows, jnp.zeros((5, lb.shape[0]), F32)], 0)


def _rope_lane_tables(seq):
    inv = 1.0 / (ROPE_THETA ** (jnp.arange(0, HEAD_DIM, 2, dtype=F32) / HEAD_DIM))
    ang = jnp.arange(seq, dtype=F32)[:, None] * inv[None, :]
    cos, sin = jnp.cos(ang), jnp.sin(ang)
    cos_t = jnp.tile(cos, (1, LANES // (HEAD_DIM // 2)))
    sin_t = jnp.tile(jnp.concatenate([-sin, sin], axis=1), (1, LANES // HEAD_DIM))
    return cos_t, sin_t


def _expand_heads(cols, shape):
    hl = _head_of_lane(shape, 1)
    out = jnp.broadcast_to(cols[-1], shape)
    for h in range(len(cols) - 2, -1, -1):
        out = jnp.where(hl == h, jnp.broadcast_to(cols[h], shape), out)
    return out


def _mlstm_kernel(p_ref, gcol_ref, grow_ref, cw_ref, cb_ref, gbr_ref, gbc_ref, g_ref, o_ref,
                  ct_ref, n_ref, m_ref, hist_ref, cbuf_ref, qk_ref):
    c = CHUNK
    w = MIX_WIDTH
    tb = p_ref.shape[0]

    @pl.when(pl.program_id(1) == 0)
    def _():
        ct_ref[...] = jnp.zeros_like(ct_ref)
        n_ref[...] = jnp.zeros_like(n_ref)
        m_ref[...] = jnp.zeros_like(m_ref)
        hist_ref[...] = jnp.zeros_like(hist_ref)

    cbuf_ref[0:8, :] = hist_ref[...]
    cbuf_ref[8:, :] = p_ref[:, 0:2 * w]
    hist_ref[...] = p_ref[tb - 8:tb, 0:2 * w]
    acc = jnp.broadcast_to(cb_ref[...], (tb, 2 * w))
    for j in range(CONV_W):
        acc = acc + cw_ref[j:j + 1, :] * cbuf_ref[pl.ds(8 - (CONV_W - 1) + j, tb), :]
    qk_ref[...] = _silu(acc)

    ones_bd = _block_ones(w)
    bd_mask = _block_ones(w, F32)
    ti = lax.broadcasted_iota(jnp.int32, (c, c), 0)
    si = lax.broadcasted_iota(jnp.int32, (c, c), 1)
    causal = ti >= si
    tri = causal.astype(BF16)
    tri_t = (ti <= si).astype(BF16)
    gain = g_ref[...]
    ones_ext = jnp.ones((c, LANES), BF16)

    def chunk(ci, carry):
        r0 = pl.multiple_of(ci * c, c)
        q = qk_ref[pl.ds(r0, c), 0:w]
        k = qk_ref[pl.ds(r0, c), w:2 * w] * (HEAD_DIM ** -0.5)
        v = p_ref[pl.ds(r0, c), 2 * w:3 * w]
        op = p_ref[pl.ds(r0, c), 3 * w:4 * w]
        gc = gcol_ref[pl.ds(r0, c), :] + gbr_ref[...]
        gr = grow_ref[ci] + gbc_ref[...]
        b_c = _dot01_l(tri, _log_sigmoid(gc))
        b_r = _dot01_r(_log_sigmoid(gr), tri_t)
        wd, s_inter, em, wk, decay = [], [], [], [], []
        for h in range(N_HEADS):
            bc = b_c[:, N_HEADS + h:N_HEADS + h + 1]
            lic = gc[:, h:h + 1]
            br = b_r[N_HEADS + h:N_HEADS + h + 1, :]
            lir = gr[h:h + 1, :]
            dmat = jnp.where(causal, bc - br + lir, -jnp.inf)
            m_prev = m_ref[h:h + 1, 0:1]
            inter = bc + m_prev
            mrow = jnp.maximum(inter, jnp.max(dmat, axis=1, keepdims=True))
            wd.append(jnp.exp(dmat - mrow))
            s_inter.append(jnp.exp(inter - mrow))
            em.append(jnp.exp(-mrow))
            b_last = br[:, c - 1:c]
            m_new = jnp.maximum(b_last + m_prev, jnp.max(b_last - br + lir, axis=1, keepdims=True))
            wk.append(jnp.exp(b_last - bc + lic - m_new))
            decay.append(jnp.exp(b_last + m_prev - m_new))
            m_ref[h:h + 1, :] = jnp.broadcast_to(m_new, (1, LANES))
        s_inter_l = _expand_heads(s_inter, (c, w))
        em_l = _expand_heads(em, (c, w))
        wk_l = _expand_heads(wk, (c, w))
        decay_l = _expand_heads(decay, (1, w))
        qk = _dot_nt(_stack_heads(q).astype(BF16), k.astype(BF16))
        wmat = jnp.concatenate(wd, axis=0) * qk
        vb = v.astype(BF16)
        r = _dot(wmat.astype(BF16), jnp.concatenate([vb, ones_ext], axis=1))
        num_intra = _unstack_heads(r[:, 0:w], c)
        rs_l = _expand_heads([r[h * c:(h + 1) * c, w:w + 1] for h in range(N_HEADS)], (c, w))
        ct = ct_ref[...]
        nrow = n_ref[0:1, :]
        num = s_inter_l * _dot_nt(q.astype(BF16), ct.astype(BF16)) + num_intra
        den = s_inter_l * _group_sum(q * nrow, ones_bd) + rs_l
        hval = num / jnp.maximum(jnp.abs(den), em_l)
        kw = wk_l * k
        ct_ref[...] = ct * decay_l + bd_mask * _dot_tn(vb, kw.astype(BF16))
        n_ref[0:1, :] = nrow * decay_l + jnp.sum(kw, axis=0, keepdims=True)
        o_ref[pl.ds(r0, c), :] = _head_rms(hval, gain, ones_bd) * _sigmoid(op)
        return carry

    lax.fori_loop(0, tb // c, chunk, 0)


def _mlstm(proj, conv_w, conv_b, gate_b, gain, bsz, seq):
    t = bsz * seq
    w = MIX_WIDTH
    tb = min(REC_BLOCK, seq)
    nb = seq // tb
    ncb = tb // CHUNK
    grow = proj[:, OFF_MG:OFF_MG + 8].reshape(t // CHUNK, CHUNK, 8).transpose(0, 2, 1)
    gb_row = jnp.zeros((1, LANES), F32).at[0, 0:8].set(gate_b.astype(F32))
    gb_col = gate_b.astype(F32).reshape(8, 1)
    return pl.pallas_call(
        _mlstm_kernel,
        grid=(bsz, nb),
        in_specs=[pl.BlockSpec((tb, 4 * w), lambda b, i: (b * nb + i, OFF_M // (4 * w))),
                  pl.BlockSpec((tb, LANES), lambda b, i: (b * nb + i, OFF_MG // LANES)),
                  pl.BlockSpec((ncb, 8, CHUNK), lambda b, i: (b * nb + i, 0, 0)),
                  pl.BlockSpec((CONV_W, 2 * w), lambda b, i: (0, 0)),
                  pl.BlockSpec((1, 2 * w), lambda b, i: (0, 0)),
                  pl.BlockSpec((1, LANES), lambda b, i: (0, 0)),
                  pl.BlockSpec((8, 1), lambda b, i: (0, 0)),
                  pl.BlockSpec((1, w), lambda b, i: (0, 0))],
        out_specs=pl.BlockSpec((tb, w), lambda b, i: (b * nb + i, 0)),
        out_shape=jax.ShapeDtypeStruct((t, w), F32),
        scratch_shapes=[pltpu.VMEM((w, w), F32), pltpu.VMEM((8, w), F32), pltpu.VMEM((8, LANES), F32),
                        pltpu.VMEM((8, 2 * w), F32), pltpu.VMEM((tb + 8, 2 * w), F32),
                        pltpu.VMEM((tb, 2 * w), F32)],
        compiler_params=_cp("parallel", "arbitrary"),
        name="mlstm",
    )(proj, proj, grow, conv_w.astype(F32), conv_b.astype(F32).reshape(1, 2 * w), gb_row, gb_col,
      gain.reshape(1, w))


NSA_TQ = 128
NSA_KC = 512
GW = 2 * HEAD_DIM


def _nsa_prep_kernel(pq_ref, pk_ref, pv_ref, cos_ref, sin_ref, qg_ref, kg_ref,
                     qn_ref, qr_ref, ks_ref, kw_ref, vst_ref, vwt_ref):
    w = MIX_WIDTH
    for src, dst in ((pv_ref[:, 0:GW], vst_ref), (pv_ref[:, GW:2 * GW], vwt_ref)):
        vt = src.T
        tk = dst.shape[4]
        for g in range(NSA_GROUPS):
            rows = vt[g * HEAD_DIM:(g + 1) * HEAD_DIM, :]
            dup = jnp.concatenate([rows, rows], axis=0).astype(BF16)
            for j in range(dst.shape[2]):
                dst[0, g, j] = dup[:, j * tk:(j + 1) * tk]
    ones_bd = _block_ones(w)
    cos_t, sin_t = cos_ref[...], sin_ref[...]
    scale = HEAD_DIM ** -0.5
    qh = _head_rms(pq_ref[...], qg_ref[...], ones_bd)
    qn_ref[...] = (qh * scale).astype(BF16)
    qr_ref[...] = (_rope(qh, cos_t, sin_t) * scale).astype(BF16)
    ks_ref[...] = _rope(_head_rms(pk_ref[:, 0:w], kg_ref[1:2, :], ones_bd), cos_t, sin_t).astype(BF16)
    kw_ref[...] = _rope(_head_rms(pk_ref[:, w:2 * w], kg_ref[2:3, :], ones_bd), cos_t, sin_t).astype(BF16)


def _nsa_prep(proj, cos4, sin4, qnorm_g, knorm_g, bsz, seq):
    t = bsz * seq
    w = MIX_WIDTH
    tm = min(NSA_KC, seq)
    tq = min(NSA_TQ, seq)
    ns = seq // tm
    qg = jnp.tile(qnorm_g.astype(F32), w // HEAD_DIM).reshape(1, w)
    kg = jnp.concatenate([jnp.tile(knorm_g.astype(F32), (1, w // HEAD_DIM)), jnp.zeros((5, w), F32)], axis=0)
    out = jax.ShapeDtypeStruct((t, w), BF16)
    row = pl.BlockSpec((tm, w), lambda i: (i, 0))
    return pl.pallas_call(
        _nsa_prep_kernel,
        grid=(t // tm,),
        in_specs=[pl.BlockSpec((tm, w), lambda i: (i, OFF_NQ // w)),
                  pl.BlockSpec((tm, 2 * w), lambda i: (i, OFF_KD // (2 * w))),
                  pl.BlockSpec((tm, 2 * GW), lambda i: (i, OFF_V // (2 * GW))),
                  pl.BlockSpec((tm, w), lambda i: (i % ns, 0)),
                  pl.BlockSpec((tm, w), lambda i: (i % ns, 0)),
                  pl.BlockSpec((1, w), lambda i: (0, 0)),
                  pl.BlockSpec((8, w), lambda i: (0, 0))],
        out_specs=[row, row, row, row,
                   pl.BlockSpec((1, NSA_GROUPS, 1, GW, tm), lambda i: (i // ns, 0, i % ns, 0, 0)),
                   pl.BlockSpec((1, NSA_GROUPS, tm // tq, GW, tq), lambda i: (i // ns, 0, i % ns, 0, 0))],
        out_shape=[out, out, out, out,
                   jax.ShapeDtypeStruct((bsz, NSA_GROUPS, seq // tm, GW, tm), BF16),
                   jax.ShapeDtypeStruct((bsz, NSA_GROUPS, seq // tq, GW, tq), BF16)],
        compiler_params=_cp("parallel"),
        name="nsa_prep",
    )(proj, proj, proj, cos4, sin4, qg, kg)


def _nsa_cmp_kernel(xr_ref, pe_ref, w0_ref, w1_ref, kg_ref, ovt_ref, qn_ref, ocmp_ref, sel_ref,
                    kc_ref, vc_ref, v_ref, *, n_top):
    tq = qn_ref.shape[0]
    nr = xr_ref.shape[0]
    nsel = sel_ref.shape[2]
    w = MIX_WIDTH

    @pl.when(pl.program_id(1) == 0)
    def _():
        xr = xr_ref[...]
        y0 = _dot((xr + pe_ref[0]).astype(BF16), w0_ref[...])
        y1 = _dot((xr + pe_ref[1]).astype(BF16), w1_ref[...])
        kv = y0 + pltpu.roll(y1, nr - 1, 0)
        kc_ref[...] = _head_rms(kv[:, 0:w], kg_ref[...], _block_ones(w)).astype(BF16)
        vc_ref[...] = kv[:, w:2 * w].astype(BF16)

    pos0 = pl.program_id(1) * tq
    hl = _head_of_lane((tq, GW), 1)
    pos_r = pos0 + lax.broadcasted_iota(jnp.int32, (tq, nr), 0)
    valid = lax.broadcasted_iota(jnp.int32, (tq, nr), 1) * CMP_STRIDE + (CMP_LEN - 1) <= pos_r
    pos_c = pos0 + lax.broadcasted_iota(jnp.int32, (nr, tq), 1)
    valid_t = lax.broadcasted_iota(jnp.int32, (nr, tq), 0) * CMP_STRIDE + (CMP_LEN - 1) <= pos_c
    jrow = lax.broadcasted_iota(jnp.int32, (nsel, tq), 0)
    cur = (pos0 + lax.broadcasted_iota(jnp.int32, (nsel, tq), 1)) // SEL_BLOCK
    forced = (jrow == 0) | (jrow == cur) | (jrow == cur - 1)
    ovt = ovt_ref[...]

    for g in range(NSA_GROUPS):
        qg = qn_ref[:, g * GW:(g + 1) * GW]
        kg = kc_ref[:, g * GW:(g + 1) * GW]
        vg = vc_ref[:, g * GW:(g + 1) * GW]
        o_g = jnp.zeros((tq, GW), F32)
        pt_sum = jnp.zeros((nr, tq), F32)
        for hh in range(2):
            qm = jnp.where(hl == hh, qg, jnp.zeros_like(qg))
            s = jnp.where(valid, _dot_nt(qm, kg), NEG_INF)
            e = jnp.exp(s - jnp.max(s, axis=1, keepdims=True))
            p = jnp.where(valid, e / jnp.sum(e, axis=1, keepdims=True), 0.0)
            o_g = jnp.where(hl == hh, _dot(p.astype(BF16), vg), o_g)
            st = jnp.where(valid_t, _dot_nt(kg, qm), NEG_INF)
            et = jnp.exp(st - jnp.max(st, axis=0, keepdims=True))
            pt_sum = pt_sum + jnp.where(valid_t, et / jnp.sum(et, axis=0, keepdims=True), 0.0)
        ocmp_ref[:, g * GW:(g + 1) * GW] = o_g
        p_hi = pt_sum.astype(BF16)
        p_lo = (pt_sum - p_hi.astype(F32)).astype(BF16)
        imp = _dot(ovt, p_hi) + _dot(ovt, p_lo)
        val = jnp.where(jrow <= cur, imp + FORCE_BONUS * forced.astype(F32), NEG_INF)
        v_ref[...] = val

        def rank(jp, cnt):
            row = v_ref[pl.ds(jp, 1), :]
            tie = jnp.where(jrow > jp, 1.0, 0.0)
            return cnt + jnp.where(row > val, 1.0, jnp.where(row == val, tie, 0.0))

        cnt = lax.fori_loop(0, nsel, rank, jnp.zeros((nsel, tq), F32))
        sel_ref[0, g] = ((cnt < n_top) & (jrow <= cur)).astype(F32)


def _nsa_cmp_weights(cmp_pe, cmp_w):
    half = CMP_LEN // 2
    wl = cmp_w.astype(F32).reshape(2, 2, half, HEAD_DIM, HEAD_DIM)
    eye2 = jnp.eye(2, dtype=F32)
    w2 = jnp.einsum('kardz,kK,gG,h->arkgdKGhz', wl, eye2, eye2, jnp.ones((2,), F32))
    w2 = w2.reshape(2, half * 4 * HEAD_DIM, 8 * HEAD_DIM)
    pl_ = cmp_pe.astype(F32).reshape(2, 2, half, HEAD_DIM)
    pe2 = jnp.broadcast_to(pl_.transpose(1, 2, 0, 3)[:, :, :, None, :], (2, half, 2, 2, HEAD_DIM))
    return w2.astype(BF16), pe2.reshape(2, 1, half * 4 * HEAD_DIM)


def _nsa_cmp(kcvc, qn, cmp_pe, cmp_w, knorm0, bsz, seq, tq=512):
    t = bsz * seq
    w = MIX_WIDTH
    tq = min(tq, seq)
    nq = seq // tq
    nr = seq // CMP_STRIDE
    nsel = seq // SEL_BLOCK
    n_top = min(SEL_TOPK, nsel)
    w2, pe2 = _nsa_cmp_weights(cmp_pe, cmp_w)
    xr = kcvc.reshape(t // CMP_STRIDE, CMP_STRIDE * w)
    kg = jnp.tile(knorm0.astype(F32), w // HEAD_DIM).reshape(1, w)
    n_i = np.arange(nr)[:, None] * CMP_STRIDE
    j_i = np.arange(nsel)[None, :] * SEL_BLOCK
    ov = ((n_i < j_i + SEL_BLOCK) & (n_i + CMP_LEN > j_i)).astype(np.float32)
    ov[nr - 1, :] = 0.0
    kin = CMP_STRIDE * w
    return pl.pallas_call(
        functools.partial(_nsa_cmp_kernel, n_top=n_top),
        grid=(bsz, nq),
        in_specs=[pl.BlockSpec((nr, kin), lambda b, i: (b, 0)),
                  pl.BlockSpec((2, 1, kin), lambda b, i: (0, 0, 0)),
                  pl.BlockSpec((None, kin, 2 * w), lambda b, i: (0, 0, 0)),
                  pl.BlockSpec((None, kin, 2 * w), lambda b, i: (1, 0, 0)),
                  pl.BlockSpec((1, w), lambda b, i: (0, 0)),
                  pl.BlockSpec((nsel, nr), lambda b, i: (0, 0)),
                  pl.BlockSpec((tq, w), lambda b, i: (b * nq + i, 0))],
        out_specs=[pl.BlockSpec((tq, w), lambda b, i: (b * nq + i, 0)),
                   pl.BlockSpec((1, NSA_GROUPS, nsel, tq), lambda b, i: (b, 0, 0, i))],
        out_shape=[jax.ShapeDtypeStruct((t, w), F32),
                   jax.ShapeDtypeStruct((bsz, NSA_GROUPS, nsel, seq), F32)],
        scratch_shapes=[pltpu.VMEM((nr, w), BF16), pltpu.VMEM((nr, w), BF16), pltpu.VMEM((nsel, tq), F32)],
        compiler_params=_cp("parallel", "arbitrary"),
        name="nsa_cmp",
    )(xr, pe2, w2, w2, kg, jnp.asarray(ov.T, BF16), qn)


def _nsa_attn_kernel(qr_ref, ks_ref, kw_ref, vs_ref, vw_ref, sel_ref, ocmp_ref, gate_ref, o_ref, *, kc, wt):
    tq = qr_ref.shape[0]
    i = pl.program_id(2)
    g = pl.program_id(1)
    hl = _head_of_lane((tq, GW), 1)
    q = qr_ref[...]
    qs = jnp.concatenate([jnp.where(hl == 0, q, jnp.zeros_like(q)), jnp.where(hl == 1, q, jnp.zeros_like(q))], axis=0)
    nbk = kc // SEL_BLOCK

    def lane_qpos(rows):
        return i * tq + lax.broadcasted_iota(jnp.int32, (rows, 2 * tq), 1) % tq

    def finish(acc, l):
        ot = (acc / l).T
        return jnp.where(hl == 0, ot[0:tq, :], ot[tq:2 * tq, :])

    qpos_s = lane_qpos(kc)
    krow_s = lax.broadcasted_iota(jnp.int32, (kc, 2 * tq), 0)

    def sel_body(c, carry):
        m, l, acc = carry
        k0 = pl.multiple_of(c * kc, kc)
        st = _dot_nt(ks_ref[pl.ds(k0, kc), :], qs)
        srows = sel_ref[0, 0, pl.ds(pl.multiple_of(c * nbk, nbk), nbk), :]
        srows = jnp.concatenate([srows, srows], axis=1)
        smask = jnp.concatenate([jnp.broadcast_to(srows[r:r + 1, :], (SEL_BLOCK, 2 * tq)) for r in range(nbk)],
                                axis=0)
        msk = (smask > 0.5) & (k0 + krow_s <= qpos_s)
        st = jnp.where(msk, st, NEG_INF)
        m_new = jnp.maximum(m, jnp.max(st, axis=0, keepdims=True))
        p = jnp.where(msk, jnp.exp(st - m_new), 0.0)
        alpha = jnp.exp(m - m_new)
        l = l * alpha + jnp.sum(p, axis=0, keepdims=True)
        acc = acc * alpha + _dot(vs_ref[0, 0, c], p.astype(BF16))
        return m_new, l, acc

    init = (jnp.full((1, 2 * tq), NEG_INF, F32), jnp.zeros((1, 2 * tq), F32), jnp.zeros((GW, 2 * tq), F32))
    _, l_s, acc_s = lax.fori_loop(0, ((i + 1) * tq + kc - 1) // kc, sel_body, init)
    o_sel = finish(acc_s, l_s)

    j0 = jnp.maximum(i - (wt - 1), 0)
    k0 = pl.multiple_of(j0 * tq, tq)
    span = wt * tq
    st = _dot_nt(kw_ref[pl.ds(k0, span), :], qs)
    kpos = k0 + lax.broadcasted_iota(jnp.int32, (span, 2 * tq), 0)
    qpos_w = lane_qpos(span)
    msk = (kpos <= qpos_w) & (kpos > qpos_w - WINDOW)
    st = jnp.where(msk, st, NEG_INF)
    p = jnp.where(msk, jnp.exp(st - jnp.max(st, axis=0, keepdims=True)), 0.0)
    vt = jnp.concatenate([vw_ref[0, 0, j0 + r] for r in range(wt)], axis=1)
    o_win = finish(_dot(vt, p.astype(BF16)), jnp.sum(p, axis=0, keepdims=True))

    gb = _sigmoid(gate_ref[...])

    def gate(branch):
        cols = []
        for hh in range(2):
            c0 = gb[:, hh * 3 + branch:hh * 3 + branch + 1]
            c1 = gb[:, (2 + hh) * 3 + branch:(2 + hh) * 3 + branch + 1]
            cols.append(jnp.where(g == 0, c0, c1))
        return _expand_heads(cols, (tq, GW))

    o_ref[...] = gate(0) * ocmp_ref[...] + gate(1) * o_sel + gate(2) * o_win


def _nsa_attn(proj, qr, ks, kw, vst, vwt, sel, o_cmp, bsz, seq):
    t = bsz * seq
    w = MIX_WIDTH
    tq = min(NSA_TQ, seq)
    nq = seq // tq
    nsel = seq // SEL_BLOCK
    kc = min(NSA_KC, seq)
    wt = min(WINDOW // tq + 1, nq)
    kspec = pl.BlockSpec((seq, GW), lambda b, g, i: (b, g))
    return pl.pallas_call(
        functools.partial(_nsa_attn_kernel, kc=kc, wt=wt),
        grid=(bsz, NSA_GROUPS, nq),
        in_specs=[pl.BlockSpec((tq, GW), lambda b, g, i: (b * nq + i, g)),
                  kspec, kspec,
                  pl.BlockSpec((1, 1, seq // kc, GW, kc), lambda b, g, i: (b, g, 0, 0, 0)),
                  pl.BlockSpec((1, 1, nq, GW, tq), lambda b, g, i: (b, g, 0, 0, 0)),
                  pl.BlockSpec((1, 1, nsel, tq), lambda b, g, i: (b, g, 0, i)),
                  pl.BlockSpec((tq, GW), lambda b, g, i: (b * nq + i, g)),
                  pl.BlockSpec((tq, LANES), lambda b, g, i: (b * nq + i, OFF_NG // LANES))],
        out_specs=pl.BlockSpec((tq, GW), lambda b, g, i: (b * nq + i, g)),
        out_shape=jax.ShapeDtypeStruct((t, w), F32),
        compiler_params=_cp("parallel", "parallel", "arbitrary"),
        name="nsa_attn",
    )(qr, ks, kw, vst, vwt, sel, o_cmp, proj)


def _nsa(proj, kcvc, cos4, sin4, qnorm_g, knorm_g, cmp_pe, cmp_w, bsz, seq):
    qn, qr, ks, kw, vst, vwt = _nsa_prep(proj, cos4, sin4, qnorm_g, knorm_g, bsz, seq)
    o_cmp, sel = _nsa_cmp(kcvc, qn, cmp_pe, cmp_w, knorm_g[0], bsz, seq)
    return _nsa_attn(proj, qr, ks, kw, vst, vwt, sel, o_cmp, bsz, seq)


PEER_TT = 128
PEER_CT = 8
HALF_D = 512


SUBLANES = 8
CODE_BITS = 127
FAR_BELOW = -3.0e38


def _with_code(x, code):
    bits = lax.bitcast_convert_type(x, jnp.int32)
    return lax.bitcast_convert_type((bits & ~CODE_BITS) | code, F32)


def _split_code(x):
    bits = lax.bitcast_convert_type(x, jnp.int32)
    return lax.bitcast_convert_type(bits & ~CODE_BITS, F32), bits & CODE_BITS


def _sort16_desc(xs):
    xs = list(xs)
    n = len(xs)
    k = 2
    while k <= n:
        j = k // 2
        while j >= 1:
            for i in range(n):
                l = i ^ j
                if l > i:
                    hi, lo = jnp.maximum(xs[i], xs[l]), jnp.minimum(xs[i], xs[l])
                    xs[i], xs[l] = (hi, lo) if (i & k) == 0 else (lo, hi)
            j //= 2
        k *= 2
    return xs


def _merge16_desc(xs):
    xs = list(xs)
    j = len(xs) // 2
    while j >= 1:
        for i in range(len(xs)):
            l = i ^ j
            if l > i:
                xs[i], xs[l] = jnp.maximum(xs[i], xs[l]), jnp.minimum(xs[i], xs[l])
        j //= 2
    return xs


def _top16_columns(x):
    n = PEER_TOPK
    xs = _sort16_desc([x[SUBLANES * j:SUBLANES * (j + 1), :] for j in range(n)])
    shift = SUBLANES // 2
    while shift >= 1:
        rolled = [pltpu.roll(a, shift, 0) for a in xs]
        xs = _merge16_desc([jnp.maximum(xs[i], rolled[n - 1 - i]) for i in range(n)])
        shift //= 2
    return xs


_PEER_CAND_TILES = ((0, 0, 8), (0, 1, 8), (1, 0, 8), (2, 0, 5), (3, 0, 4), (4, 0, 3), (5, 0, 2), (6, 0, 2), (7, 0, 2))


def _peer_route_kernel(q_ref, key_ref, e_ref, g_ref):
    tt = q_ref.shape[0]
    nk = PEER_NKEYS
    n = PEER_TOPK
    row = lax.broadcasted_iota(jnp.int32, (nk, tt), 0)
    sub = lax.broadcasted_iota(jnp.int32, (SUBLANES, tt), 0)
    vals, ids = [], []
    for p in range(2):
        st = _dot_nt(key_ref[0, p], q_ref[:, p * PEER_KDIM:(p + 1) * PEER_KDIM])
        top = [_split_code(a) for a in _top16_columns(_with_code(st, (nk - 1) - row))]
        vals.append([v for v, _ in top])
        ids.append([(nk - 1) - c for _, c in top])
    (v1, v2), (i1, i2) = vals, ids

    def stack(xs, lo):
        out = xs[lo]
        for s in range(1, SUBLANES):
            out = jnp.where(sub == s, xs[lo + s], out)
        return out

    v2t, i2t = (stack(v2, 0), stack(v2, SUBLANES)), (stack(i2, 0), stack(i2, SUBLANES))
    cand, cexp = [], []
    for a, tile, nvalid in _PEER_CAND_TILES:
        v = v1[a] + v2t[tile]
        cand.append(v if nvalid == SUBLANES else jnp.where(sub < nvalid, v, FAR_BELOW))
        cexp.append(i1[a] * nk + i2t[tile])
    cand.append(stack(v1, SUBLANES) + v2[0])
    cexp.append(stack(i1, SUBLANES) * nk + i2[0])
    n_tiles = len(cand)
    slot_code = [(nk - 1) - (c * SUBLANES + sub) for c in range(n_tiles)]
    coded = [_with_code(v, sc) for v, sc in zip(cand, slot_code)]
    coded += [jnp.full((SUBLANES, tt), FAR_BELOW, F32)] * (n - n_tiles)
    top = [_split_code(a) for a in _top16_columns(jnp.concatenate(coded, axis=0))]
    call = jnp.concatenate(cexp, axis=0)
    slot = (nk - 1) - lax.broadcasted_iota(jnp.int32, call.shape, 0)
    ex = [jnp.exp(v - top[0][0]) for v, _ in top]
    tot = ex[0]
    for k in range(1, n):
        tot = tot + ex[k]
    for k in range(n):
        hit = slot == jnp.concatenate([top[k][1]] * n_tiles, axis=0)
        e_ref[0, k:k + 1, :] = jnp.sum(jnp.where(hit, call, 0), axis=0, keepdims=True)
        g_ref[0, k:k + 1, :] = (ex[k] / tot)[0:1, :]


def _peer_route(qp, keys):
    t = qp.shape[0]
    tt = min(PEER_TT, t)
    return pl.pallas_call(
        _peer_route_kernel,
        grid=(t // tt, PEER_HEADS),
        in_specs=[pl.BlockSpec((tt, 2 * PEER_KDIM), lambda i, h: (i, h)),
                  pl.BlockSpec((1, 2, PEER_NKEYS, PEER_KDIM), lambda i, h: (h, 0, 0, 0))],
        out_specs=[pl.BlockSpec((1, PEER_TOPK, tt), lambda i, h: (h, 0, i)),
                   pl.BlockSpec((1, PEER_TOPK, tt), lambda i, h: (h, 0, i))],
        out_shape=[jax.ShapeDtypeStruct((PEER_HEADS, PEER_TOPK, t), jnp.int32),
                   jax.ShapeDtypeStruct((PEER_HEADS, PEER_TOPK, t), F32)],
        compiler_params=_cp("parallel", "arbitrary"),
        name="peer_route",
    )(qp, keys)


def _pack_tables_kernel(u_ref, v_ref, o_ref):
    def pack(x):
        lo = lax.bitcast_convert_type(x[:, 0:HALF_D].astype(BF16).astype(F32), jnp.int32)
        hi = lax.bitcast_convert_type(x[:, HALF_D:2 * HALF_D].astype(BF16).astype(F32), jnp.int32)
        return lax.shift_right_logical(lo, 16) | (hi & jnp.int32(-65536))

    o_ref[:, 0:HALF_D] = pack(u_ref[...])
    o_ref[:, HALF_D:2 * HALF_D] = pack(v_ref[...])


def _pack_tables(u_tabs, v_tabs, layer, tr=512):
    _, e, d = u_tabs.shape
    assert d == 2 * HALF_D
    spec_in = pl.BlockSpec((None, tr, d), lambda i: (layer, i, 0))
    spec = pl.BlockSpec((tr, d), lambda i: (i, 0))
    return pl.pallas_call(
        _pack_tables_kernel,
        grid=(e // tr,),
        in_specs=[spec_in, spec_in],
        out_specs=spec,
        out_shape=jax.ShapeDtypeStruct((e, d), jnp.int32),
        compiler_params=_cp("parallel"),
        name="peer_pack",
    )(u_tabs, v_tabs)


def _unpack_rows(wd):
    lo = lax.bitcast_convert_type(lax.shift_left(wd, 16), F32)
    hi = lax.bitcast_convert_type(lax.bitwise_and(wd, jnp.int32(-65536)), F32)
    return lo, hi


SC_WINDOW = 32


def _sc_gather(table, idx):
    from jax.experimental.pallas import tpu_sc as plsc
    n = idx.shape[0]
    width = table.shape[1]
    mesh = plsc.VectorSubcoreMesh(core_axis_name="core", subcore_axis_name="subcore")

    @functools.partial(pl.kernel, out_type=jax.ShapeDtypeStruct((n, width), table.dtype), mesh=mesh)
    def gather(tab_hbm, idx_hbm, out_hbm):
        def body(idx_vmem, out_vmem):
            pltpu.sync_copy(tab_hbm.at[idx_vmem.at[0, pl.ds(0, SC_WINDOW)]], out_vmem)

        pltpu.emit_pipeline(
            body,
            grid=(n // SC_WINDOW,),
            in_specs=[pl.BlockSpec((1, LANES), lambda i: (0, i))],
            out_specs=[pl.BlockSpec((SC_WINDOW, width), lambda i: (i, 0))],
            core_axis_name=("core", "subcore"),
            dimension_semantics=(pltpu.PARALLEL,),
            trace_scopes=False,
        )(idx_hbm, out_hbm)

    idx_pad = jnp.pad(idx.reshape(n // SC_WINDOW, SC_WINDOW), ((0, 0), (0, LANES - SC_WINDOW)))
    return gather(table, idx_pad.reshape(1, (n // SC_WINDOW) * LANES))


def _peer_combine_kernel(x_ref, g2_ref, rows_ref, gate_ref, o_ref):
    ne = PEER_HEADS * PEER_TOPK
    x = x_ref[...]
    ct = x.shape[0]
    xn = x * lax.rsqrt(jnp.mean(x * x, axis=-1, keepdims=True) + NORM_EPS) * g2_ref[...]
    gate_t = jnp.concatenate([gate_ref[...]] * (ne // ct), axis=0).T
    for j in range(ct):
        u_lo, u_hi = _unpack_rows(rows_ref[j * ne:(j + 1) * ne, 0:HALF_D])
        xr = xn[j:j + 1, :]
        h = jnp.sum(u_lo * xr[:, 0:HALF_D] + u_hi * xr[:, HALF_D:2 * HALF_D], axis=1, keepdims=True)
        act = 0.5 * h * (1.0 + lax.erf(h * (2.0 ** -0.5)))
        wgt = gate_t[:, j:j + 1] * act
        v_lo, v_hi = _unpack_rows(rows_ref[j * ne:(j + 1) * ne, HALF_D:2 * HALF_D])
        o_ref[j:j + 1, 0:HALF_D] = x[j:j + 1, 0:HALF_D] + jnp.sum(wgt * v_lo, axis=0, keepdims=True)
        o_ref[j:j + 1, HALF_D:2 * HALF_D] = x[j:j + 1, HALF_D:2 * HALF_D] + jnp.sum(wgt * v_hi, axis=0, keepdims=True)


def _peer_combine(x, g2, rows, gates, first_token):
    t, d = x.shape
    ne = PEER_HEADS * PEER_TOPK
    ct = PEER_CT
    steps = rows.shape[0] // (ct * ne)
    off = first_token // ct
    return pl.pallas_call(
        _peer_combine_kernel,
        grid=(steps,),
        in_specs=[pl.BlockSpec((ct, d), lambda i: (off + i, 0)),
                  pl.BlockSpec((1, d), lambda i: (0, 0)),
                  pl.BlockSpec((ct * ne, d), lambda i: (i, 0)),
                  pl.BlockSpec((ct, ne), lambda i: (off + i, 0))],
        out_specs=pl.BlockSpec((ct, d), lambda i: (off + i, 0)),
        out_shape=jax.ShapeDtypeStruct((t, d), F32),
        input_output_aliases={0: 0},
        compiler_params=_cp("parallel"),
        name="peer_combine",
    )(x, g2.reshape(1, d), rows, gates)


PEER_TOKENS_PER_GATHER = 2048


def _peer_route_stage(x, g2, wq_b, keys_b):
    t = x.shape[0]
    ne = PEER_HEADS * PEER_TOPK
    qp = _norm_matmul(x, g2, wq_b, out_dtype=BF16)
    e_t, g_t = _peer_route(qp, keys_b)
    return e_t.reshape(ne, t).T.reshape(t * ne), g_t.reshape(ne, t).T


def _peer_gather_stage(table, idx, t, gather_fn):
    ne = PEER_HEADS * PEER_TOPK
    tc = min(PEER_TOKENS_PER_GATHER, t)
    return [gather_fn(table, idx[c * tc * ne:(c + 1) * tc * ne]) for c in range(t // tc)]


def _peer_combine_stage(x, g2, rows_list, gates):
    tc = x.shape[0] // len(rows_list)
    for c, rows in enumerate(rows_list):
        x = _peer_combine(x, g2, rows, gates, c * tc)
    return x


def _peer(x, g2, wq, keys, u_tabs, v_tabs, layer, gather_fn):
    idx, gates = _peer_route_stage(x, g2, wq.astype(BF16), keys.astype(BF16))
    table = _pack_tables(u_tabs, v_tabs, layer)
    rows_list = _peer_gather_stage(table, idx, x.shape[0], gather_fn)
    return _peer_combine_stage(x, g2, rows_list, gates)


_IN_WIDTHS = (256, 256, 256, 256, 256, 256, 256, 4, 4, 256, 256, 128, 128, 128, 128, 128, 128, 12,
              256, 256, 256, 256)


def _dup_groups(wcols):
    g0, g1 = wcols[:, :HEAD_DIM], wcols[:, HEAD_DIM:]
    return jnp.concatenate([g0, g0, g1, g1], axis=1)


def _layout_w_in(w_in):
    offs = np.cumsum((0,) + _IN_WIDTHS)
    cols = [w_in[:, offs[i]:offs[i + 1]] for i in range(len(_IN_WIDTHS))]
    (hq, hf, hi, hg, mq, mk, mv, mi, mf, mo, nq, nkc, nvc, nks, nvs, nkw, nvw, ng, rq, rk, rv, rg) = cols
    d = w_in.shape[0]
    pad = lambda c, n: jnp.concatenate([c, jnp.zeros((d, n - c.shape[1]), w_in.dtype)], axis=1)
    main = jnp.concatenate([hq, hf, hi, hg, mq, mk, mv, mo, rq, rk, rv, rg,
                            _dup_groups(nks), _dup_groups(nkw), nq, nvs, nvw,
                            pad(jnp.concatenate([mi, mf], axis=1), LANES), pad(ng, LANES)], axis=1)
    assert main.shape[1] == N_MAIN
    kcvc = jnp.concatenate([nkc, nvc], axis=1)
    return main.astype(BF16), kcvc.astype(BF16)


def kernel(x, norm1_g, w_in, hgrn_lb, hgrn_onorm_g, mlstm_conv_w, mlstm_conv_b, mlstm_gate_b, mlstm_onorm_g, nsa_qnorm_g, nsa_knorm_g, nsa_cmp_pe, nsa_cmp_w, ret_onorm_g, w_up, w_gate, w_out, norm2_g, peer_wq, peer_keys, peer_u, peer_v):
    bsz, seq, d = x.shape
    t = bsz * seq
    depth = w_in.shape[0]
    cos_t, sin_t = _rope_lane_tables(seq)
    cos4, sin4 = jnp.tile(cos_t, (1, 2)), jnp.tile(sin_t, (1, 2))
    lb_cum = jnp.cumsum(jax.nn.softmax(hgrn_lb.astype(F32), axis=0), axis=0)
    lb_all = lb_cum - lb_cum[0:1]
    weights = []
    for l in range(depth):
        w_main, w_kcvc = _layout_w_in(w_in[l])
        weights.append(dict(
            main=w_main, kcvc=w_kcvc, gate=w_gate[l].astype(BF16), up=w_up[l].astype(BF16),
            out=w_out[l].astype(BF16), lb=_hgrn_lb_rows(lb_all[l]), wq=peer_wq[l].astype(BF16),
            keys=peer_keys[l].astype(BF16), table=_pack_tables(peer_u, peer_v, l)))

    def mixer_steps(xh, l, nb):
        wl = weights[l]
        st = {}

        def proj(dep):
            st["proj"] = _norm_matmul(xh, _after(norm1_g[l], dep), wl["main"])
            return st["proj"]

        def gates(dep):
            st["gates"] = _norm_matmul(xh, _after(norm1_g[l], dep), wl["gate"], act="sigmoid", out_dtype=BF16)
            return st["gates"]

        def hgrn(dep):
            st["oh"] = _hgrn(st["proj"], wl["lb"], _after(hgrn_onorm_g[l], dep), nb, seq)
            return st["oh"]

        def mlstm(dep):
            st["om"] = _mlstm(st["proj"], mlstm_conv_w[l], mlstm_conv_b[l], mlstm_gate_b[l],
                              _after(mlstm_onorm_g[l], dep), nb, seq)
            return st["om"]

        def ret(dep):
            st["or"] = _ret(st["proj"], cos4, sin4, _after(ret_onorm_g[l], dep), nb, seq)
            return st["or"]

        def nsa_front(dep):
            kcvc = _norm_matmul(xh, _after(norm1_g[l], dep), wl["kcvc"])
            qn, qr, ks, kw, vst, vwt = _nsa_prep(st["proj"], cos4, sin4, nsa_qnorm_g[l], nsa_knorm_g[l], nb, seq)
            o_cmp, sel = _nsa_cmp(kcvc, qn, nsa_cmp_pe[l], nsa_cmp_w[l], nsa_knorm_g[l][0], nb, seq)
            st["nsa"] = (qr, ks, kw, vst, vwt, sel, o_cmp)
            return o_cmp

        def nsa_attn(dep):
            del dep
            st["on"] = _nsa_attn(st["proj"], *st["nsa"], nb, seq)
            return st["on"]

        def merge(dep):
            del dep
            st["xm"] = _merge(xh, st["gates"], (st["oh"], st["om"], st["on"], st["or"]), wl["up"], wl["out"])
            return st["xm"]

        def route(dep):
            st["idx"], st["pgates"] = _peer_route_stage(st["xm"], _after(norm2_g[l], dep), wl["wq"], wl["keys"])
            return st["pgates"]

        return [proj, gates, hgrn, mlstm, ret, nsa_front, nsa_attn, merge, route], st

    combine_slots = (0, 2, 4, 5, 6, 6, 7, 8)

    def combine_steps(l, xm, rows_list, pgates):
        box = {"x": xm}
        tc = xm.shape[0] // len(rows_list)

        def make(c):
            def step(dep):
                box["x"] = _peer_combine(box["x"], _after(norm2_g[l], dep), rows_list[c], pgates, c * tc)
                return box["x"]
            return step

        return [make(c) for c in range(len(rows_list))], box

    n_groups = 2 if bsz % 2 == 0 else 1
    nb = bsz // n_groups
    xs = [x[g * nb:(g + 1) * nb].reshape(nb * seq, d) for g in range(n_groups)]
    dep = None
    pending = None
    for l in range(depth):
        for g in range(n_groups):
            if pending is not None and pending[0] == g:
                for cstep in pending[1]:
                    dep = cstep(dep)
                xs[g] = pending[2]["x"]
                pending = None
            msteps, st = mixer_steps(xs[g], l, nb)
            csteps = pending[1] if pending is not None else []
            ci = 0
            for si, mstep in enumerate(msteps):
                dep = mstep(dep)
                while ci < len(csteps) and (ci >= len(combine_slots) or combine_slots[ci] <= si):
                    dep = csteps[ci](dep)
                    ci += 1
            for cstep in csteps[ci:]:
                dep = cstep(dep)
            if pending is not None:
                xs[pending[0]] = pending[2]["x"]
            rows_list = _peer_gather_stage(weights[l]["table"], st["idx"], nb * seq, _sc_gather)
            csteps, box = combine_steps(l, st["xm"], rows_list, st["pgates"])
            pending = (g, csteps, box)
    for cstep in pending[1]:
        dep = cstep(dep)
    xs[pending[0]] = pending[2]["x"]
    return jnp.concatenate(xs, axis=0).reshape(bsz, seq, d)
```

```python
import functools
import math

import numpy as np
import jax
import jax.numpy as jnp
from jax import lax
from jax.experimental import pallas as pl
from jax.experimental.pallas import tpu as pltpu

F32 = jnp.float32
BF16 = jnp.bfloat16

HEAD_DIM = 64
N_HEADS = 4
MIX_WIDTH = N_HEADS * HEAD_DIM
CHUNK = 64
NORM_EPS = 1e-6
NEG_INF = -1e30
ROPE_THETA = 10000.0
CONV_W = 4
NSA_GROUPS = 2
CMP_LEN = 32
CMP_STRIDE = 16
SEL_BLOCK = 64
SEL_TOPK = 16
WINDOW = 512
FORCE_BONUS = 1e3
PEER_HEADS = 8
PEER_NKEYS = 128
PEER_TOPK = 16
PEER_KDIM = 128

LANES = 128
VMEM_LIMIT = 48 * 1024 * 1024

OFF_H, OFF_M, OFF_R, OFF_KD, OFF_NQ, OFF_V, OFF_MG, OFF_NG = 0, 1024, 2048, 3072, 3584, 3840, 4096, 4224
N_MAIN = 4352


def _cp(*sem):
    return pltpu.CompilerParams(dimension_semantics=sem, vmem_limit_bytes=VMEM_LIMIT)


def _dot(a, b):
    return jnp.dot(a, b, preferred_element_type=F32)


def _dot_nt(a, b):
    return lax.dot_general(a, b, (((1,), (1,)), ((), ())), preferred_element_type=F32)


def _dot_tn(a, b):
    return lax.dot_general(a, b, (((0,), (0,)), ((), ())), preferred_element_type=F32)


def _split3(x):
    hi = x.astype(BF16)
    r1 = x - hi.astype(F32)
    mid = r1.astype(BF16)
    lo = (r1 - mid.astype(F32)).astype(BF16)
    return hi, mid, lo


def _dot01_l(m01, x):
    hi, mid, lo = _split3(x)
    return _dot(m01, hi) + _dot(m01, mid) + _dot(m01, lo)


def _dot01_r(x, m01):
    hi, mid, lo = _split3(x)
    return _dot(hi, m01) + _dot(mid, m01) + _dot(lo, m01)


def _head_of_lane(shape, axis):
    return lax.broadcasted_iota(jnp.int32, shape, axis) // HEAD_DIM


def _block_ones(n, dtype=BF16):
    r = lax.broadcasted_iota(jnp.int32, (n, n), 0) // HEAD_DIM
    c = lax.broadcasted_iota(jnp.int32, (n, n), 1) // HEAD_DIM
    return (r == c).astype(dtype)


def _group_sum(x, ones_bd):
    hi = x.astype(BF16)
    lo = (x - hi.astype(F32)).astype(BF16)
    return _dot(hi, ones_bd) + _dot(lo, ones_bd)


def _head_rms(x, gain, ones_bd):
    ms = _group_sum(x * x, ones_bd) * (1.0 / HEAD_DIM)
    return x * lax.rsqrt(ms + NORM_EPS) * gain


def _sigmoid(x):
    return 1.0 / (1.0 + jnp.exp(-x))


def _silu(x):
    return x * _sigmoid(x)


def _log_sigmoid(x):
    return jnp.minimum(x, 0.0) - jnp.log(1.0 + jnp.exp(-jnp.abs(x)))


def _stack_heads(x, n_heads=N_HEADS):
    hl = _head_of_lane(x.shape, 1)
    return jnp.concatenate([jnp.where(hl == h, x, jnp.zeros_like(x)) for h in range(n_heads)], axis=0)


def _unstack_heads(r, c, n_heads=N_HEADS):
    hl = _head_of_lane((c, r.shape[1]), 1)
    out = jnp.zeros((c, r.shape[1]), F32)
    for h in range(n_heads):
        out = jnp.where(hl == h, r[h * c:(h + 1) * c, :], out)
    return out


def _rope(x, cos_t, sin_t):
    n = x.shape[1]
    first = (lax.broadcasted_iota(jnp.int32, x.shape, 1) % HEAD_DIM) < (HEAD_DIM // 2)
    partner = jnp.where(first, pltpu.roll(x, n - HEAD_DIM // 2, 1), pltpu.roll(x, HEAD_DIM // 2, 1))
    return x * cos_t + partner * sin_t


def _after_kernel(a_ref, dep_ref, o_ref):
    del dep_ref
    o_ref[...] = a_ref[...]


def _after(a, dep):
    if dep is None:
        return a
    a2 = a.reshape(1, a.size)
    out = pl.pallas_call(
        _after_kernel,
        in_specs=[pl.BlockSpec(a2.shape, lambda: (0, 0)), pl.BlockSpec(memory_space=pl.ANY)],
        out_specs=pl.BlockSpec(a2.shape, lambda: (0, 0)),
        out_shape=jax.ShapeDtypeStruct(a2.shape, a2.dtype),
        name="order_after",
    )(a2, dep)
    return out.reshape(a.shape)


def _norm_matmul_kernel(x_ref, g_ref, w_ref, o_ref, xn_ref, *, act):
    @pl.when(pl.program_id(1) == 0)
    def _():
        x = x_ref[...]
        ms = jnp.mean(x * x, axis=-1, keepdims=True)
        xn_ref[...] = (x * lax.rsqrt(ms + NORM_EPS) * g_ref[...]).astype(BF16)

    y = _dot(xn_ref[...], w_ref[...])
    if act == "sigmoid":
        y = _sigmoid(y)
    o_ref[...] = y.astype(o_ref.dtype)


def _norm_matmul(x, g, w, *, act=None, out_dtype=F32, tm=1024, tn=512):
    t, d = x.shape
    w3 = w if w.ndim == 3 else w[None]
    n_per = w3.shape[2]
    tm = min(tm, t)
    tn = next(c for c in (tn, 256, 128) if n_per % c == 0)
    per = n_per // tn
    n = w3.shape[0] * n_per
    assert t % tm == 0
    return pl.pallas_call(
        functools.partial(_norm_matmul_kernel, act=act),
        grid=(t // tm, n // tn),
        in_specs=[pl.BlockSpec((tm, d), lambda i, j: (i, 0)),
                  pl.BlockSpec((1, d), lambda i, j: (0, 0)),
                  pl.BlockSpec((None, d, tn), lambda i, j: (j // per, 0, j % per))],
        out_specs=pl.BlockSpec((tm, tn), lambda i, j: (i, j)),
        out_shape=jax.ShapeDtypeStruct((t, n), out_dtype),
        scratch_shapes=[pltpu.VMEM((tm, d), BF16)],
        compiler_params=_cp("parallel", "arbitrary"),
        name="norm_matmul",
    )(x, g.reshape(1, d), w3)


def _merge_kernel(x_ref, gate_ref, oh_ref, om_ref, on_ref, or_ref, wup_ref, wout_ref, o_ref):
    d = x_ref.shape[1]
    acc = None
    for m, r in enumerate((oh_ref, om_ref, on_ref, or_ref)):
        up = _dot(r[...].astype(BF16), wup_ref[m])
        term = gate_ref[:, m * d:(m + 1) * d].astype(F32) * up
        acc = term if acc is None else acc + term
    o_ref[...] = x_ref[...] + _dot(acc.astype(BF16), wout_ref[...])


def _merge(x, gates, outs, w_up, w_out, tm=512):
    t, d = x.shape
    tm = min(tm, t)
    mix = pl.BlockSpec((tm, MIX_WIDTH), lambda i: (i, 0))
    return pl.pallas_call(
        _merge_kernel,
        grid=(t // tm,),
        in_specs=[pl.BlockSpec((tm, d), lambda i: (i, 0)),
                  pl.BlockSpec((tm, 4 * d), lambda i: (i, 0)),
                  mix, mix, mix, mix,
                  pl.BlockSpec((4, MIX_WIDTH, d), lambda i: (0, 0, 0)),
                  pl.BlockSpec((d, d), lambda i: (0, 0))],
        out_specs=pl.BlockSpec((tm, d), lambda i: (i, 0)),
        out_shape=jax.ShapeDtypeStruct((t, d), F32),
        compiler_params=_cp("parallel"),
        name="merge",
    )(x, gates, *outs, w_up, w_out)


REC_BLOCK = 256


def _chunk_consts():
    t = lax.broadcasted_iota(jnp.int32, (CHUNK, CHUNK), 0)
    s = lax.broadcasted_iota(jnp.int32, (CHUNK, CHUNK), 1)
    return t, s


def _hgrn_levels():
    t = np.arange(CHUNK)
    rows = []
    masks = []
    h = CHUNK // 2
    while h >= 1:
        ref = (t // (2 * h)) * (2 * h) + h
        p = np.zeros((CHUNK, CHUNK), np.float32)
        p[t, np.minimum(ref, CHUNK - 1)] = 1.0
        rows.append(p)
        same = (t[:, None] // (2 * h)) == (t[None, :] // (2 * h))
        m = same & ((t[:, None] // h) % 2 == 1) & ((t[None, :] // h) % 2 == 0)
        masks.append(m.astype(np.float32))
        h //= 2
    masks.append(np.eye(CHUNK, dtype=np.float32))
    return np.concatenate(rows, 0), np.stack(masks, 0)


def _hgrn_kernel(p_ref, lb_ref, g_ref, psel_ref, lmask_ref, o_ref, st_ref):
    @pl.when(pl.program_id(1) == 0)
    def _():
        st_ref[...] = jnp.zeros_like(st_ref)

    c = CHUNK
    w = MIX_WIDTH
    ones_bd = _block_ones(w)
    bd_mask = _block_ones(w, F32)
    tri = (lax.broadcasted_iota(jnp.int32, (c, c), 0) >= lax.broadcasted_iota(jnp.int32, (c, c), 1)).astype(BF16)
    psel = psel_ref[...]
    n_lv = lmask_ref.shape[0]
    log_lb, log_1mlb, one_mlb = lb_ref[0:1, :], lb_ref[1:2, :], lb_ref[2:3, :]
    gain = g_ref[...]

    def chunk(ci, carry):
        r0 = pl.multiple_of(ci * c, c)
        q = _silu(p_ref[pl.ds(r0, c), 0:w])
        fl = p_ref[pl.ds(r0, c), w:2 * w]
        v = p_ref[pl.ds(r0, c), 2 * w:3 * w]
        gp = p_ref[pl.ds(r0, c), 3 * w:4 * w]
        a1 = jnp.broadcast_to(log_lb, fl.shape)
        a2 = log_1mlb + _log_sigmoid(fl)
        mx = jnp.maximum(a1, a2)
        log_f = mx + jnp.log(jnp.exp(a1 - mx) + jnp.exp(a2 - mx))
        k = one_mlb * _sigmoid(-fl)
        b = _dot01_l(tri, log_f)
        bref = _dot01_l(psel, b)
        vb = v.astype(BF16)
        a = jnp.zeros((N_HEADS * c, c), F32)
        for lv in range(n_lv):
            if lv < n_lv - 1:
                br = bref[lv * c:(lv + 1) * c, :]
                qs = q * jnp.exp(jnp.minimum(b - br, 0.0))
                ks = k * jnp.exp(jnp.minimum(br - b, 0.0))
            else:
                qs, ks = q, k
            s_lv = _dot_nt(_stack_heads(qs).astype(BF16), ks.astype(BF16))
            a = a + jnp.concatenate([lmask_ref[lv]] * N_HEADS, axis=0) * s_lv
        o = _unstack_heads(_dot(a.astype(BF16), vb), c)
        st = st_ref[...]
        o = o + _dot_nt((q * jnp.exp(b)).astype(BF16), st.astype(BF16))
        b_last = b[c - 1:c, :]
        kb = k * jnp.exp(b_last - b)
        st_ref[...] = st * jnp.exp(b_last) + bd_mask * _dot_tn(vb, kb.astype(BF16))
        y = _head_rms(o, gain, ones_bd) * _silu(gp)
        o_ref[pl.ds(r0, c), :] = y
        return carry

    lax.fori_loop(0, p_ref.shape[0] // c, chunk, 0)


def _hgrn(proj, lb_rows, gain, bsz, seq):
    psel, lmask = _hgrn_levels()
    tb = min(REC_BLOCK, seq)
    nb = seq // tb
    return pl.pallas_call(
        _hgrn_kernel,
        grid=(bsz, nb),
        in_specs=[pl.BlockSpec((tb, 4 * MIX_WIDTH), lambda b, i: (b * nb + i, OFF_H // (4 * MIX_WIDTH))),
                  pl.BlockSpec((8, MIX_WIDTH), lambda b, i: (0, 0)),
                  pl.BlockSpec((1, MIX_WIDTH), lambda b, i: (0, 0)),
                  pl.BlockSpec(psel.shape, lambda b, i: (0, 0)),
                  pl.BlockSpec(lmask.shape, lambda b, i: (0, 0, 0))],
        out_specs=pl.BlockSpec((tb, MIX_WIDTH), lambda b, i: (b * nb + i, 0)),
        out_shape=jax.ShapeDtypeStruct((bsz * seq, MIX_WIDTH), F32),
        scratch_shapes=[pltpu.VMEM((MIX_WIDTH, MIX_WIDTH), F32)],
        compiler_params=_cp("parallel", "arbitrary"),
        name="hgrn2",
    )(proj, lb_rows, gain.reshape(1, MIX_WIDTH), jnp.asarray(psel, BF16), jnp.asarray(lmask, F32))


def _ret_kernel(p_ref, cos_ref, sin_ref, dec_ref, decin_ref, g_ref, o_ref, st_ref):
    @pl.when(pl.program_id(1) == 0)
    def _():
        st_ref[...] = jnp.zeros_like(st_ref)

    c = CHUNK
    w = MIX_WIDTH
    ones_bd = _block_ones(w)
    bd_mask = _block_ones(w, F32)
    gain = g_ref[...]
    dec_q = dec_ref[0:c, :]
    dec_k = dec_ref[c:2 * c, :]
    dec_state = dec_ref[2 * c:2 * c + 1, :]
    dec_in = decin_ref[...]

    def chunk(ci, carry):
        r0 = pl.multiple_of(ci * c, c)
        cos_t = cos_ref[pl.ds(r0, c), :]
        sin_t = sin_ref[pl.ds(r0, c), :]
        q = _rope(p_ref[pl.ds(r0, c), 0:w], cos_t, sin_t)
        k = _rope(p_ref[pl.ds(r0, c), w:2 * w], cos_t, sin_t) * (HEAD_DIM ** -0.5)
        v = p_ref[pl.ds(r0, c), 2 * w:3 * w]
        gp = p_ref[pl.ds(r0, c), 3 * w:4 * w]
        vb = v.astype(BF16)
        a = _dot_nt(_stack_heads(q).astype(BF16), k.astype(BF16)) * dec_in
        o = _unstack_heads(_dot(a.astype(BF16), vb), c)
        st = st_ref[...]
        o = o + _dot_nt(q.astype(BF16), st.astype(BF16)) * dec_q
        st_ref[...] = st * dec_state + bd_mask * _dot_tn(vb, (k * dec_k).astype(BF16))
        o_ref[pl.ds(r0, c), :] = _head_rms(o, gain, ones_bd) * _silu(gp)
        return carry

    lax.fori_loop(0, p_ref.shape[0] // c, chunk, 0)


def _ret_consts():
    log_gamma = np.log1p(-np.exp2(-5.0 - np.arange(N_HEADS, dtype=np.float64)))
    t = np.arange(CHUNK, dtype=np.float64)
    lane_h = np.arange(MIX_WIDTH) // HEAD_DIM
    dec_q = np.exp(log_gamma[lane_h][None, :] * (t[:, None] + 1.0))
    dec_k = np.exp(log_gamma[lane_h][None, :] * (CHUNK - 1.0 - t[:, None]))
    dec_state = np.exp(log_gamma[lane_h] * CHUNK)[None, :]
    dec = np.concatenate([dec_q, dec_k, np.broadcast_to(dec_state, (8, MIX_WIDTH))], 0)
    diff = t[:, None] - t[None, :]
    dec_in = np.concatenate([np.where(diff >= 0, np.exp(log_gamma[h] * diff), 0.0) for h in range(N_HEADS)], 0)
    return dec.astype(np.float32), dec_in.astype(np.float32)


def _ret(proj, cos4, sin4, gain, bsz, seq):
    dec, dec_in = _ret_consts()
    tb = min(REC_BLOCK, seq)
    nb = seq // tb
    return pl.pallas_call(
        _ret_kernel,
        grid=(bsz, nb),
        in_specs=[pl.BlockSpec((tb, 4 * MIX_WIDTH), lambda b, i: (b * nb + i, OFF_R // (4 * MIX_WIDTH))),
                  pl.BlockSpec((tb, MIX_WIDTH), lambda b, i: (i, 0)),
                  pl.BlockSpec((tb, MIX_WIDTH), lambda b, i: (i, 0)),
                  pl.BlockSpec(dec.shape, lambda b, i: (0, 0)),
                  pl.BlockSpec(dec_in.shape, lambda b, i: (0, 0)),
                  pl.BlockSpec((1, MIX_WIDTH), lambda b, i: (0, 0))],
        out_specs=pl.BlockSpec((tb, MIX_WIDTH), lambda b, i: (b * nb + i, 0)),
        out_shape=jax.ShapeDtypeStruct((bsz * seq, MIX_WIDTH), F32),
        scratch_shapes=[pltpu.VMEM((MIX_WIDTH, MIX_WIDTH), F32)],
        compiler_params=_cp("parallel", "arbitrary"),
        name="retention",
    )(proj, cos4, sin4, jnp.asarray(dec), jnp.asarray(dec_in), gain.reshape(1, MIX_WIDTH))


def _hgrn_lb_rows(lb):
    lb = lb.astype(F32)
    rows = jnp.stack([jnp.log(lb), jnp.log1p(-lb), 1.0 - lb], 0)
    return jnp.concatenate([rows, jnp.zeros((5, lb.shape[0]), F32)], 0)


def _rope_lane_tables(seq):
    inv = 1.0 / (ROPE_THETA ** (jnp.arange(0, HEAD_DIM, 2, dtype=F32) / HEAD_DIM))
    ang = jnp.arange(seq, dtype=F32)[:, None] * inv[None, :]
    cos, sin = jnp.cos(ang), jnp.sin(ang)
    cos_t = jnp.tile(cos, (1, LANES // (HEAD_DIM // 2)))
    sin_t = jnp.tile(jnp.concatenate([-sin, sin], axis=1), (1, LANES // HEAD_DIM))
    return cos_t, sin_t


def _expand_heads(cols, shape):
    hl = _head_of_lane(shape, 1)
    out = jnp.broadcast_to(cols[-1], shape)
    for h in range(len(cols) - 2, -1, -1):
        out = jnp.where(hl == h, jnp.broadcast_to(cols[h], shape), out)
    return out


def _mlstm_kernel(p_ref, gcol_ref, grow_ref, cw_ref, cb_ref, gbr_ref, gbc_ref, g_ref, o_ref,
                  ct_ref, n_ref, m_ref, hist_ref, cbuf_ref, qk_ref):
    c = CHUNK
    w = MIX_WIDTH
    tb = p_ref.shape[0]

    @pl.when(pl.program_id(1) == 0)
    def _():
        ct_ref[...] = jnp.zeros_like(ct_ref)
        n_ref[...] = jnp.zeros_like(n_ref)
        m_ref[...] = jnp.zeros_like(m_ref)
        hist_ref[...] = jnp.zeros_like(hist_ref)

    cbuf_ref[0:8, :] = hist_ref[...]
    cbuf_ref[8:, :] = p_ref[:, 0:2 * w]
    hist_ref[...] = p_ref[tb - 8:tb, 0:2 * w]
    acc = jnp.broadcast_to(cb_ref[...], (tb, 2 * w))
    for j in range(CONV_W):
        acc = acc + cw_ref[j:j + 1, :] * cbuf_ref[pl.ds(8 - (CONV_W - 1) + j, tb), :]
    qk_ref[...] = _silu(acc)

    ones_bd = _block_ones(w)
    bd_mask = _block_ones(w, F32)
    ti = lax.broadcasted_iota(jnp.int32, (c, c), 0)
    si = lax.broadcasted_iota(jnp.int32, (c, c), 1)
    causal = ti >= si
    tri = causal.astype(BF16)
    tri_t = (ti <= si).astype(BF16)
    gain = g_ref[...]
    ones_ext = jnp.ones((c, LANES), BF16)

    def chunk(ci, carry):
        r0 = pl.multiple_of(ci * c, c)
        q = qk_ref[pl.ds(r0, c), 0:w]
        k = qk_ref[pl.ds(r0, c), w:2 * w] * (HEAD_DIM ** -0.5)
        v = p_ref[pl.ds(r0, c), 2 * w:3 * w]
        op = p_ref[pl.ds(r0, c), 3 * w:4 * w]
        gc = gcol_ref[pl.ds(r0, c), :] + gbr_ref[...]
        gr = grow_ref[ci] + gbc_ref[...]
        b_c = _dot01_l(tri, _log_sigmoid(gc))
        b_r = _dot01_r(_log_sigmoid(gr), tri_t)
        wd, s_inter, em, wk, decay = [], [], [], [], []
        for h in range(N_HEADS):
            bc = b_c[:, N_HEADS + h:N_HEADS + h + 1]
            lic = gc[:, h:h + 1]
            br = b_r[N_HEADS + h:N_HEADS + h + 1, :]
            lir = gr[h:h + 1, :]
            dmat = jnp.where(causal, bc - br + lir, -jnp.inf)
            m_prev = m_ref[h:h + 1, 0:1]
            inter = bc + m_prev
            mrow = jnp.maximum(inter, jnp.max(dmat, axis=1, keepdims=True))
            wd.append(jnp.exp(dmat - mrow))
            s_inter.append(jnp.exp(inter - mrow))
            em.append(jnp.exp(-mrow))
            b_last = br[:, c - 1:c]
            m_new = jnp.maximum(b_last + m_prev, jnp.max(b_last - br + lir, axis=1, keepdims=True))
            wk.append(jnp.exp(b_last - bc + lic - m_new))
            decay.append(jnp.exp(b_last + m_prev - m_new))
            m_ref[h:h + 1, :] = jnp.broadcast_to(m_new, (1, LANES))
        s_inter_l = _expand_heads(s_inter, (c, w))
        em_l = _expand_heads(em, (c, w))
        wk_l = _expand_heads(wk, (c, w))
        decay_l = _expand_heads(decay, (1, w))
        qk = _dot_nt(_stack_heads(q).astype(BF16), k.astype(BF16))
        wmat = jnp.concatenate(wd, axis=0) * qk
        vb = v.astype(BF16)
        r = _dot(wmat.astype(BF16), jnp.concatenate([vb, ones_ext], axis=1))
        num_intra = _unstack_heads(r[:, 0:w], c)
        rs_l = _expand_heads([r[h * c:(h + 1) * c, w:w + 1] for h in range(N_HEADS)], (c, w))
        ct = ct_ref[...]
        nrow = n_ref[0:1, :]
        num = s_inter_l * _dot_nt(q.astype(BF16), ct.astype(BF16)) + num_intra
        den = s_inter_l * _group_sum(q * nrow, ones_bd) + rs_l
        hval = num / jnp.maximum(jnp.abs(den), em_l)
        kw = wk_l * k
        ct_ref[...] = ct * decay_l + bd_mask * _dot_tn(vb, kw.astype(BF16))
        n_ref[0:1, :] = nrow * decay_l + jnp.sum(kw, axis=0, keepdims=True)
        o_ref[pl.ds(r0, c), :] = _head_rms(hval, gain, ones_bd) * _sigmoid(op)
        return carry

    lax.fori_loop(0, tb // c, chunk, 0)


def _mlstm(proj, conv_w, conv_b, gate_b, gain, bsz, seq):
    t = bsz * seq
    w = MIX_WIDTH
    tb = min(REC_BLOCK, seq)
    nb = seq // tb
    ncb = tb // CHUNK
    grow = proj[:, OFF_MG:OFF_MG + 8].reshape(t // CHUNK, CHUNK, 8).transpose(0, 2, 1)
    gb_row = jnp.zeros((1, LANES), F32).at[0, 0:8].set(gate_b.astype(F32))
    gb_col = gate_b.astype(F32).reshape(8, 1)
    return pl.pallas_call(
        _mlstm_kernel,
        grid=(bsz, nb),
        in_specs=[pl.BlockSpec((tb, 4 * w), lambda b, i: (b * nb + i, OFF_M // (4 * w))),
                  pl.BlockSpec((tb, LANES), lambda b, i: (b * nb + i, OFF_MG // LANES)),
                  pl.BlockSpec((ncb, 8, CHUNK), lambda b, i: (b * nb + i, 0, 0)),
                  pl.BlockSpec((CONV_W, 2 * w), lambda b, i: (0, 0)),
                  pl.BlockSpec((1, 2 * w), lambda b, i: (0, 0)),
                  pl.BlockSpec((1, LANES), lambda b, i: (0, 0)),
                  pl.BlockSpec((8, 1), lambda b, i: (0, 0)),
                  pl.BlockSpec((1, w), lambda b, i: (0, 0))],
        out_specs=pl.BlockSpec((tb, w), lambda b, i: (b * nb + i, 0)),
        out_shape=jax.ShapeDtypeStruct((t, w), F32),
        scratch_shapes=[pltpu.VMEM((w, w), F32), pltpu.VMEM((8, w), F32), pltpu.VMEM((8, LANES), F32),
                        pltpu.VMEM((8, 2 * w), F32), pltpu.VMEM((tb + 8, 2 * w), F32),
                        pltpu.VMEM((tb, 2 * w), F32)],
        compiler_params=_cp("parallel", "arbitrary"),
        name="mlstm",
    )(proj, proj, grow, conv_w.astype(F32), conv_b.astype(F32).reshape(1, 2 * w), gb_row, gb_col,
      gain.reshape(1, w))


NSA_TQ = 128
NSA_KC = 512
GW = 2 * HEAD_DIM


def _nsa_prep_kernel(pq_ref, pk_ref, pv_ref, cos_ref, sin_ref, qg_ref, kg_ref,
                     qn_ref, qr_ref, ks_ref, kw_ref, vst_ref, vwt_ref):
    w = MIX_WIDTH
    for src, dst in ((pv_ref[:, 0:GW], vst_ref), (pv_ref[:, GW:2 * GW], vwt_ref)):
        vt = src.T
        tk = dst.shape[4]
        for g in range(NSA_GROUPS):
            rows = vt[g * HEAD_DIM:(g + 1) * HEAD_DIM, :]
            dup = jnp.concatenate([rows, rows], axis=0).astype(BF16)
            for j in range(dst.shape[2]):
                dst[0, g, j] = dup[:, j * tk:(j + 1) * tk]
    ones_bd = _block_ones(w)
    cos_t, sin_t = cos_ref[...], sin_ref[...]
    scale = HEAD_DIM ** -0.5
    qh = _head_rms(pq_ref[...], qg_ref[...], ones_bd)
    qn_ref[...] = (qh * scale).astype(BF16)
    qr_ref[...] = (_rope(qh, cos_t, sin_t) * scale).astype(BF16)
    ks_ref[...] = _rope(_head_rms(pk_ref[:, 0:w], kg_ref[1:2, :], ones_bd), cos_t, sin_t).astype(BF16)
    kw_ref[...] = _rope(_head_rms(pk_ref[:, w:2 * w], kg_ref[2:3, :], ones_bd), cos_t, sin_t).astype(BF16)


def _nsa_prep(proj, cos4, sin4, qnorm_g, knorm_g, bsz, seq):
    t = bsz * seq
    w = MIX_WIDTH
    tm = min(NSA_KC, seq)
    tq = min(NSA_TQ, seq)
    ns = seq // tm
    qg = jnp.tile(qnorm_g.astype(F32), w // HEAD_DIM).reshape(1, w)
    kg = jnp.concatenate([jnp.tile(knorm_g.astype(F32), (1, w // HEAD_DIM)), jnp.zeros((5, w), F32)], axis=0)
    out = jax.ShapeDtypeStruct((t, w), BF16)
    row = pl.BlockSpec((tm, w), lambda i: (i, 0))
    return pl.pallas_call(
        _nsa_prep_kernel,
        grid=(t // tm,),
        in_specs=[pl.BlockSpec((tm, w), lambda i: (i, OFF_NQ // w)),
                  pl.BlockSpec((tm, 2 * w), lambda i: (i, OFF_KD // (2 * w))),
                  pl.BlockSpec((tm, 2 * GW), lambda i: (i, OFF_V // (2 * GW))),
                  pl.BlockSpec((tm, w), lambda i: (i % ns, 0)),
                  pl.BlockSpec((tm, w), lambda i: (i % ns, 0)),
                  pl.BlockSpec((1, w), lambda i: (0, 0)),
                  pl.BlockSpec((8, w), lambda i: (0, 0))],
        out_specs=[row, row, row, row,
                   pl.BlockSpec((1, NSA_GROUPS, 1, GW, tm), lambda i: (i // ns, 0, i % ns, 0, 0)),
                   pl.BlockSpec((1, NSA_GROUPS, tm // tq, GW, tq), lambda i: (i // ns, 0, i % ns, 0, 0))],
        out_shape=[out, out, out, out,
                   jax.ShapeDtypeStruct((bsz, NSA_GROUPS, seq // tm, GW, tm), BF16),
                   jax.ShapeDtypeStruct((bsz, NSA_GROUPS, seq // tq, GW, tq), BF16)],
        compiler_params=_cp("parallel"),
        name="nsa_prep",
    )(proj, proj, proj, cos4, sin4, qg, kg)


def _nsa_cmp_kernel(xr_ref, pe_ref, w0_ref, w1_ref, kg_ref, ovt_ref, qn_ref, ocmp_ref, sel_ref,
                    kc_ref, vc_ref, v_ref, *, n_top):
    tq = qn_ref.shape[0]
    nr = xr_ref.shape[0]
    nsel = sel_ref.shape[2]
    w = MIX_WIDTH

    @pl.when(pl.program_id(1) == 0)
    def _():
        xr = xr_ref[...]
        y0 = _dot((xr + pe_ref[0]).astype(BF16), w0_ref[...])
        y1 = _dot((xr + pe_ref[1]).astype(BF16), w1_ref[...])
        kv = y0 + pltpu.roll(y1, nr - 1, 0)
        kc_ref[...] = _head_rms(kv[:, 0:w], kg_ref[...], _block_ones(w)).astype(BF16)
        vc_ref[...] = kv[:, w:2 * w].astype(BF16)

    pos0 = pl.program_id(1) * tq
    hl = _head_of_lane((tq, GW), 1)
    pos_r = pos0 + lax.broadcasted_iota(jnp.int32, (tq, nr), 0)
    valid = lax.broadcasted_iota(jnp.int32, (tq, nr), 1) * CMP_STRIDE + (CMP_LEN - 1) <= pos_r
    pos_c = pos0 + lax.broadcasted_iota(jnp.int32, (nr, tq), 1)
    valid_t = lax.broadcasted_iota(jnp.int32, (nr, tq), 0) * CMP_STRIDE + (CMP_LEN - 1) <= pos_c
    jrow = lax.broadcasted_iota(jnp.int32, (nsel, tq), 0)
    cur = (pos0 + lax.broadcasted_iota(jnp.int32, (nsel, tq), 1)) // SEL_BLOCK
    forced = (jrow == 0) | (jrow == cur) | (jrow == cur - 1)
    ovt = ovt_ref[...]

    for g in range(NSA_GROUPS):
        qg = qn_ref[:, g * GW:(g + 1) * GW]
        kg = kc_ref[:, g * GW:(g + 1) * GW]
        vg = vc_ref[:, g * GW:(g + 1) * GW]
        o_g = jnp.zeros((tq, GW), F32)
        pt_sum = jnp.zeros((nr, tq), F32)
        for hh in range(2):
            qm = jnp.where(hl == hh, qg, jnp.zeros_like(qg))
            s = jnp.where(valid, _dot_nt(qm, kg), NEG_INF)
            e = jnp.exp(s - jnp.max(s, axis=1, keepdims=True))
            p = jnp.where(valid, e / jnp.sum(e, axis=1, keepdims=True), 0.0)
            o_g = jnp.where(hl == hh, _dot(p.astype(BF16), vg), o_g)
            st = jnp.where(valid_t, _dot_nt(kg, qm), NEG_INF)
            et = jnp.exp(st - jnp.max(st, axis=0, keepdims=True))
            pt_sum = pt_sum + jnp.where(valid_t, et / jnp.sum(et, axis=0, keepdims=True), 0.0)
        ocmp_ref[:, g * GW:(g + 1) * GW] = o_g
        p_hi = pt_sum.astype(BF16)
        p_lo = (pt_sum - p_hi.astype(F32)).astype(BF16)
        imp = _dot(ovt, p_hi) + _dot(ovt, p_lo)
        val = jnp.where(jrow <= cur, imp + FORCE_BONUS * forced.astype(F32), NEG_INF)
        v_ref[...] = val

        def rank(jp, cnt):
            row = v_ref[pl.ds(jp, 1), :]
            tie = jnp.where(jrow > jp, 1.0, 0.0)
            return cnt + jnp.where(row > val, 1.0, jnp.where(row == val, tie, 0.0))

        cnt = lax.fori_loop(0, nsel, rank, jnp.zeros((nsel, tq), F32))
        sel_ref[0, g] = ((cnt < n_top) & (jrow <= cur)).astype(F32)


def _nsa_cmp_weights(cmp_pe, cmp_w):
    half = CMP_LEN // 2
    wl = cmp_w.astype(F32).reshape(2, 2, half, HEAD_DIM, HEAD_DIM)
    eye2 = jnp.eye(2, dtype=F32)
    w2 = jnp.einsum('kardz,kK,gG,h->arkgdKGhz', wl, eye2, eye2, jnp.ones((2,), F32))
    w2 = w2.reshape(2, half * 4 * HEAD_DIM, 8 * HEAD_DIM)
    pl_ = cmp_pe.astype(F32).reshape(2, 2, half, HEAD_DIM)
    pe2 = jnp.broadcast_to(pl_.transpose(1, 2, 0, 3)[:, :, :, None, :], (2, half, 2, 2, HEAD_DIM))
    return w2.astype(BF16), pe2.reshape(2, 1, half * 4 * HEAD_DIM)


def _nsa_cmp(kcvc, qn, cmp_pe, cmp_w, knorm0, bsz, seq, tq=512):
    t = bsz * seq
    w = MIX_WIDTH
    tq = min(tq, seq)
    nq = seq // tq
    nr = seq // CMP_STRIDE
    nsel = seq // SEL_BLOCK
    n_top = min(SEL_TOPK, nsel)
    w2, pe2 = _nsa_cmp_weights(cmp_pe, cmp_w)
    xr = kcvc.reshape(t // CMP_STRIDE, CMP_STRIDE * w)
    kg = jnp.tile(knorm0.astype(F32), w // HEAD_DIM).reshape(1, w)
    n_i = np.arange(nr)[:, None] * CMP_STRIDE
    j_i = np.arange(nsel)[None, :] * SEL_BLOCK
    ov = ((n_i < j_i + SEL_BLOCK) & (n_i + CMP_LEN > j_i)).astype(np.float32)
    ov[nr - 1, :] = 0.0
    kin = CMP_STRIDE * w
    return pl.pallas_call(
        functools.partial(_nsa_cmp_kernel, n_top=n_top),
        grid=(bsz, nq),
        in_specs=[pl.BlockSpec((nr, kin), lambda b, i: (b, 0)),
                  pl.BlockSpec((2, 1, kin), lambda b, i: (0, 0, 0)),
                  pl.BlockSpec((None, kin, 2 * w), lambda b, i: (0, 0, 0)),
                  pl.BlockSpec((None, kin, 2 * w), lambda b, i: (1, 0, 0)),
                  pl.BlockSpec((1, w), lambda b, i: (0, 0)),
                  pl.BlockSpec((nsel, nr), lambda b, i: (0, 0)),
                  pl.BlockSpec((tq, w), lambda b, i: (b * nq + i, 0))],
        out_specs=[pl.BlockSpec((tq, w), lambda b, i: (b * nq + i, 0)),
                   pl.BlockSpec((1, NSA_GROUPS, nsel, tq), lambda b, i: (b, 0, 0, i))],
        out_shape=[jax.ShapeDtypeStruct((t, w), F32),
                   jax.ShapeDtypeStruct((bsz, NSA_GROUPS, nsel, seq), F32)],
        scratch_shapes=[pltpu.VMEM((nr, w), BF16), pltpu.VMEM((nr, w), BF16), pltpu.VMEM((nsel, tq), F32)],
        compiler_params=_cp("parallel", "arbitrary"),
        name="nsa_cmp",
    )(xr, pe2, w2, w2, kg, jnp.asarray(ov.T, BF16), qn)


def _nsa_attn_kernel(qr_ref, ks_ref, kw_ref, vs_ref, vw_ref, sel_ref, ocmp_ref, gate_ref, o_ref, *, kc, wt):
    tq = qr_ref.shape[0]
    i = pl.program_id(2)
    g = pl.program_id(1)
    hl = _head_of_lane((tq, GW), 1)
    q = qr_ref[...]
    qs = jnp.concatenate([jnp.where(hl == 0, q, jnp.zeros_like(q)), jnp.where(hl == 1, q, jnp.zeros_like(q))], axis=0)
    nbk = kc // SEL_BLOCK

    def lane_qpos(rows):
        return i * tq + lax.broadcasted_iota(jnp.int32, (rows, 2 * tq), 1) % tq

    def finish(acc, l):
        ot = (acc / l).T
        return jnp.where(hl == 0, ot[0:tq, :], ot[tq:2 * tq, :])

    qpos_s = lane_qpos(kc)
    krow_s = lax.broadcasted_iota(jnp.int32, (kc, 2 * tq), 0)

    def sel_body(c, carry):
        m, l, acc = carry
        k0 = pl.multiple_of(c * kc, kc)
        st = _dot_nt(ks_ref[pl.ds(k0, kc), :], qs)
        srows = sel_ref[0, 0, pl.ds(pl.multiple_of(c * nbk, nbk), nbk), :]
        srows = jnp.concatenate([srows, srows], axis=1)
        smask = jnp.concatenate([jnp.broadcast_to(srows[r:r + 1, :], (SEL_BLOCK, 2 * tq)) for r in range(nbk)],
                                axis=0)
        msk = (smask > 0.5) & (k0 + krow_s <= qpos_s)
        st = jnp.where(msk, st, NEG_INF)
        m_new = jnp.maximum(m, jnp.max(st, axis=0, keepdims=True))
        p = jnp.where(msk, jnp.exp(st - m_new), 0.0)
        alpha = jnp.exp(m - m_new)
        l = l * alpha + jnp.sum(p, axis=0, keepdims=True)
        acc = acc * alpha + _dot(vs_ref[0, 0, c], p.astype(BF16))
        return m_new, l, acc

    init = (jnp.full((1, 2 * tq), NEG_INF, F32), jnp.zeros((1, 2 * tq), F32), jnp.zeros((GW, 2 * tq), F32))
    _, l_s, acc_s = lax.fori_loop(0, ((i + 1) * tq + kc - 1) // kc, sel_body, init)
    o_sel = finish(acc_s, l_s)

    j0 = jnp.maximum(i - (wt - 1), 0)
    k0 = pl.multiple_of(j0 * tq, tq)
    span = wt * tq
    st = _dot_nt(kw_ref[pl.ds(k0, span), :], qs)
    kpos = k0 + lax.broadcasted_iota(jnp.int32, (span, 2 * tq), 0)
    qpos_w = lane_qpos(span)
    msk = (kpos <= qpos_w) & (kpos > qpos_w - WINDOW)
    st = jnp.where(msk, st, NEG_INF)
    p = jnp.where(msk, jnp.exp(st - jnp.max(st, axis=0, keepdims=True)), 0.0)
    vt = jnp.concatenate([vw_ref[0, 0, j0 + r] for r in range(wt)], axis=1)
    o_win = finish(_dot(vt, p.astype(BF16)), jnp.sum(p, axis=0, keepdims=True))

    gb = _sigmoid(gate_ref[...])

    def gate(branch):
        cols = []
        for hh in range(2):
            c0 = gb[:, hh * 3 + branch:hh * 3 + branch + 1]
            c1 = gb[:, (2 + hh) * 3 + branch:(2 + hh) * 3 + branch + 1]
            cols.append(jnp.where(g == 0, c0, c1))
        return _expand_heads(cols, (tq, GW))

    o_ref[...] = gate(0) * ocmp_ref[...] + gate(1) * o_sel + gate(2) * o_win


def _nsa_attn(proj, qr, ks, kw, vst, vwt, sel, o_cmp, bsz, seq):
    t = bsz * seq
    w = MIX_WIDTH
    tq = min(NSA_TQ, seq)
    nq = seq // tq
    nsel = seq // SEL_BLOCK
    kc = min(NSA_KC, seq)
    wt = min(WINDOW // tq + 1, nq)
    kspec = pl.BlockSpec((seq, GW), lambda b, g, i: (b, g))
    return pl.pallas_call(
        functools.partial(_nsa_attn_kernel, kc=kc, wt=wt),
        grid=(bsz, NSA_GROUPS, nq),
        in_specs=[pl.BlockSpec((tq, GW), lambda b, g, i: (b * nq + i, g)),
                  kspec, kspec,
                  pl.BlockSpec((1, 1, seq // kc, GW, kc), lambda b, g, i: (b, g, 0, 0, 0)),
                  pl.BlockSpec((1, 1, nq, GW, tq), lambda b, g, i: (b, g, 0, 0, 0)),
                  pl.BlockSpec((1, 1, nsel, tq), lambda b, g, i: (b, g, 0, i)),
                  pl.BlockSpec((tq, GW), lambda b, g, i: (b * nq + i, g)),
                  pl.BlockSpec((tq, LANES), lambda b, g, i: (b * nq + i, OFF_NG // LANES))],
        out_specs=pl.BlockSpec((tq, GW), lambda b, g, i: (b * nq + i, g)),
        out_shape=jax.ShapeDtypeStruct((t, w), F32),
        compiler_params=_cp("parallel", "parallel", "arbitrary"),
        name="nsa_attn",
    )(qr, ks, kw, vst, vwt, sel, o_cmp, proj)


def _nsa(proj, kcvc, cos4, sin4, qnorm_g, knorm_g, cmp_pe, cmp_w, bsz, seq):
    qn, qr, ks, kw, vst, vwt = _nsa_prep(proj, cos4, sin4, qnorm_g, knorm_g, bsz, seq)
    o_cmp, sel = _nsa_cmp(kcvc, qn, cmp_pe, cmp_w, knorm_g[0], bsz, seq)
    return _nsa_attn(proj, qr, ks, kw, vst, vwt, sel, o_cmp, bsz, seq)


PEER_TT = 128
PEER_CT = 8
HALF_D = 512


SUBLANES = 8
CODE_BITS = 127
FAR_BELOW = -3.0e38


def _with_code(x, code):
    bits = lax.bitcast_convert_type(x, jnp.int32)
    return lax.bitcast_convert_type((bits & ~CODE_BITS) | code, F32)


def _split_code(x):
    bits = lax.bitcast_convert_type(x, jnp.int32)
    return lax.bitcast_convert_type(bits & ~CODE_BITS, F32), bits & CODE_BITS


def _sort16_desc(xs):
    xs = list(xs)
    n = len(xs)
    k = 2
    while k <= n:
        j = k // 2
        while j >= 1:
            for i in range(n):
                l = i ^ j
                if l > i:
                    hi, lo = jnp.maximum(xs[i], xs[l]), jnp.minimum(xs[i], xs[l])
                    xs[i], xs[l] = (hi, lo) if (i & k) == 0 else (lo, hi)
            j //= 2
        k *= 2
    return xs


def _merge16_desc(xs):
    xs = list(xs)
    j = len(xs) // 2
    while j >= 1:
        for i in range(len(xs)):
            l = i ^ j
            if l > i:
                xs[i], xs[l] = jnp.maximum(xs[i], xs[l]), jnp.minimum(xs[i], xs[l])
        j //= 2
    return xs


def _top16_columns(x):
    n = PEER_TOPK
    xs = _sort16_desc([x[SUBLANES * j:SUBLANES * (j + 1), :] for j in range(n)])
    shift = SUBLANES // 2
    while shift >= 1:
        rolled = [pltpu.roll(a, shift, 0) for a in xs]
        xs = _merge16_desc([jnp.maximum(xs[i], rolled[n - 1 - i]) for i in range(n)])
        shift //= 2
    return xs


_PEER_CAND_TILES = ((0, 0, 8), (0, 1, 8), (1, 0, 8), (2, 0, 5), (3, 0, 4), (4, 0, 3), (5, 0, 2), (6, 0, 2), (7, 0, 2))


def _peer_route_kernel(q_ref, key_ref, e_ref, g_ref, e_scr, g_scr):
    tt = q_ref.shape[0]
    nk = PEER_NKEYS
    n = PEER_TOPK
    row = lax.broadcasted_iota(jnp.int32, (nk, tt), 0)
    sub = lax.broadcasted_iota(jnp.int32, (SUBLANES, tt), 0)
    vals, ids = [], []
    for p in range(2):
        st = _dot_nt(key_ref[0, p], q_ref[:, p * PEER_KDIM:(p + 1) * PEER_KDIM])
        top = [_split_code(a) for a in _top16_columns(_with_code(st, (nk - 1) - row))]
        vals.append([v for v, _ in top])
        ids.append([(nk - 1) - c for _, c in top])
    (v1, v2), (i1, i2) = vals, ids

    def stack(xs, lo):
        out = xs[lo]
        for s in range(1, SUBLANES):
            out = jnp.where(sub == s, xs[lo + s], out)
        return out

    v2t, i2t = (stack(v2, 0), stack(v2, SUBLANES)), (stack(i2, 0), stack(i2, SUBLANES))
    cand, cexp = [], []
    for a, tile, nvalid in _PEER_CAND_TILES:
        v = v1[a] + v2t[tile]
        cand.append(v if nvalid == SUBLANES else jnp.where(sub < nvalid, v, FAR_BELOW))
        cexp.append(i1[a] * nk + i2t[tile])
    cand.append(stack(v1, SUBLANES) + v2[0])
    cexp.append(stack(i1, SUBLANES) * nk + i2[0])
    n_tiles = len(cand)
    slot_code = [(nk - 1) - (c * SUBLANES + sub) for c in range(n_tiles)]
    coded = [_with_code(v, sc) for v, sc in zip(cand, slot_code)]
    coded += [jnp.full((SUBLANES, tt), FAR_BELOW, F32)] * (n - n_tiles)
    top = [_split_code(a) for a in _top16_columns(jnp.concatenate(coded, axis=0))]
    call = jnp.concatenate(cexp, axis=0)
    slot = (nk - 1) - lax.broadcasted_iota(jnp.int32, call.shape, 0)
    ex = [jnp.exp(v - top[0][0]) for v, _ in top]
    tot = ex[0]
    for k in range(1, n):
        tot = tot + ex[k]
    krow = lax.broadcasted_iota(jnp.int32, (n, tt), 0)
    e_tile = jnp.zeros((n, tt), F32)
    g_tile = jnp.zeros((n, tt), F32)
    for k in range(n):
        hit = slot == jnp.concatenate([top[k][1]] * n_tiles, axis=0)
        e_k = jnp.sum(jnp.where(hit, call, 0), axis=0, keepdims=True)
        e_tile = jnp.where(krow == k, e_k.astype(F32), e_tile)
        g_tile = jnp.where(krow == k, (ex[k] / tot)[0:1, :], g_tile)
    h = pl.program_id(1)
    r0 = pl.multiple_of(h * n, n)
    e_scr[pl.ds(r0, n), :] = e_tile
    g_scr[pl.ds(r0, n), :] = g_tile

    @pl.when(h == pl.num_programs(1) - 1)
    def _():
        e_ref[...] = e_scr[...].T.astype(jnp.int32)
        g_ref[...] = g_scr[...].T


def _peer_route(qp, keys):
    t = qp.shape[0]
    tt = min(PEER_TT, t)
    ne = PEER_HEADS * PEER_TOPK
    return pl.pallas_call(
        _peer_route_kernel,
        grid=(t // tt, PEER_HEADS),
        in_specs=[pl.BlockSpec((tt, 2 * PEER_KDIM), lambda i, h: (i, h)),
                  pl.BlockSpec((1, 2, PEER_NKEYS, PEER_KDIM), lambda i, h: (h, 0, 0, 0))],
        out_specs=[pl.BlockSpec((tt, ne), lambda i, h: (i, 0)),
                   pl.BlockSpec((tt, ne), lambda i, h: (i, 0))],
        out_shape=[jax.ShapeDtypeStruct((t, ne), jnp.int32),
                   jax.ShapeDtypeStruct((t, ne), F32)],
        scratch_shapes=[pltpu.VMEM((ne, tt), F32), pltpu.VMEM((ne, tt), F32)],
        compiler_params=_cp("parallel", "arbitrary"),
        name="peer_route",
    )(qp, keys)


def _pack_tables_kernel(u_ref, v_ref, o_ref):
    def pack(x):
        lo = lax.bitcast_convert_type(x[:, 0:HALF_D].astype(BF16).astype(F32), jnp.int32)
        hi = lax.bitcast_convert_type(x[:, HALF_D:2 * HALF_D].astype(BF16).astype(F32), jnp.int32)
        return lax.shift_right_logical(lo, 16) | (hi & jnp.int32(-65536))

    o_ref[:, 0:HALF_D] = pack(u_ref[...])
    o_ref[:, HALF_D:2 * HALF_D] = pack(v_ref[...])


def _pack_tables(u_tabs, v_tabs, layer, tr=512):
    _, e, d = u_tabs.shape
    assert d == 2 * HALF_D
    spec_in = pl.BlockSpec((None, tr, d), lambda i: (layer, i, 0))
    spec = pl.BlockSpec((tr, d), lambda i: (i, 0))
    return pl.pallas_call(
        _pack_tables_kernel,
        grid=(e // tr,),
        in_specs=[spec_in, spec_in],
        out_specs=spec,
        out_shape=jax.ShapeDtypeStruct((e, d), jnp.int32),
        compiler_params=_cp("parallel"),
        name="peer_pack",
    )(u_tabs, v_tabs)


def _unpack_rows(wd):
    lo = lax.bitcast_convert_type(lax.shift_left(wd, 16), F32)
    hi = lax.bitcast_convert_type(lax.bitwise_and(wd, jnp.int32(-65536)), F32)
    return lo, hi


SC_WINDOW = 32


def _sc_gather(table, idx):
    from jax.experimental.pallas import tpu_sc as plsc
    n = idx.shape[0]
    width = table.shape[1]
    mesh = plsc.VectorSubcoreMesh(core_axis_name="core", subcore_axis_name="subcore")

    @functools.partial(pl.kernel, out_type=jax.ShapeDtypeStruct((n, width), table.dtype), mesh=mesh)
    def gather(tab_hbm, idx_hbm, out_hbm):
        def body(idx_vmem, out_vmem):
            pltpu.sync_copy(tab_hbm.at[idx_vmem.at[0, pl.ds(0, SC_WINDOW)]], out_vmem)

        pltpu.emit_pipeline(
            body,
            grid=(n // SC_WINDOW,),
            in_specs=[pl.BlockSpec((1, LANES), lambda i: (0, i))],
            out_specs=[pl.BlockSpec((SC_WINDOW, width), lambda i: (i, 0))],
            core_axis_name=("core", "subcore"),
            dimension_semantics=(pltpu.PARALLEL,),
            trace_scopes=False,
        )(idx_hbm, out_hbm)

    idx_pad = jnp.pad(idx.reshape(n // SC_WINDOW, SC_WINDOW), ((0, 0), (0, LANES - SC_WINDOW)))
    return gather(table, idx_pad.reshape(1, (n // SC_WINDOW) * LANES))


def _peer_combine_kernel(x_ref, g2_ref, rows_ref, gate_ref, o_ref):
    ne = PEER_HEADS * PEER_TOPK
    x = x_ref[...]
    ct = x.shape[0]
    xn = x * lax.rsqrt(jnp.mean(x * x, axis=-1, keepdims=True) + NORM_EPS) * g2_ref[...]
    gate_t = jnp.concatenate([gate_ref[...]] * (ne // ct), axis=0).T
    for j in range(ct):
        u_lo, u_hi = _unpack_rows(rows_ref[j * ne:(j + 1) * ne, 0:HALF_D])
        xr = xn[j:j + 1, :]
        h = jnp.sum(u_lo * xr[:, 0:HALF_D] + u_hi * xr[:, HALF_D:2 * HALF_D], axis=1, keepdims=True)
        act = 0.5 * h * (1.0 + lax.erf(h * (2.0 ** -0.5)))
        wgt = gate_t[:, j:j + 1] * act
        v_lo, v_hi = _unpack_rows(rows_ref[j * ne:(j + 1) * ne, HALF_D:2 * HALF_D])
        o_ref[j:j + 1, 0:HALF_D] = x[j:j + 1, 0:HALF_D] + jnp.sum(wgt * v_lo, axis=0, keepdims=True)
        o_ref[j:j + 1, HALF_D:2 * HALF_D] = x[j:j + 1, HALF_D:2 * HALF_D] + jnp.sum(wgt * v_hi, axis=0, keepdims=True)


def _peer_combine(x, g2, rows, gates, first_token):
    t, d = x.shape
    ne = PEER_HEADS * PEER_TOPK
    ct = PEER_CT
    steps = rows.shape[0] // (ct * ne)
    off = first_token // ct
    return pl.pallas_call(
        _peer_combine_kernel,
        grid=(steps,),
        in_specs=[pl.BlockSpec((ct, d), lambda i: (off + i, 0)),
                  pl.BlockSpec((1, d), lambda i: (0, 0)),
                  pl.BlockSpec((ct * ne, d), lambda i: (i, 0)),
                  pl.BlockSpec((ct, ne), lambda i: (off + i, 0))],
        out_specs=pl.BlockSpec((ct, d), lambda i: (off + i, 0)),
        out_shape=jax.ShapeDtypeStruct((t, d), F32),
        input_output_aliases={0: 0},
        compiler_params=_cp("parallel"),
        name="peer_combine",
    )(x, g2.reshape(1, d), rows, gates)


PEER_TOKENS_PER_GATHER = 2048


def _peer_route_stage(x, g2, wq_b, keys_b):
    t = x.shape[0]
    ne = PEER_HEADS * PEER_TOPK
    qp = _norm_matmul(x, g2, wq_b, out_dtype=BF16)
    e_tok, g_tok = _peer_route(qp, keys_b)
    return e_tok.reshape(t * ne), g_tok


def _peer_gather_stage(table, idx, t, gather_fn):
    ne = PEER_HEADS * PEER_TOPK
    tc = min(PEER_TOKENS_PER_GATHER, t)
    return [gather_fn(table, idx[c * tc * ne:(c + 1) * tc * ne]) for c in range(t // tc)]


def _peer_combine_stage(x, g2, rows_list, gates):
    tc = x.shape[0] // len(rows_list)
    for c, rows in enumerate(rows_list):
        x = _peer_combine(x, g2, rows, gates, c * tc)
    return x


def _peer(x, g2, wq, keys, u_tabs, v_tabs, layer, gather_fn):
    idx, gates = _peer_route_stage(x, g2, wq.astype(BF16), keys.astype(BF16))
    table = _pack_tables(u_tabs, v_tabs, layer)
    rows_list = _peer_gather_stage(table, idx, x.shape[0], gather_fn)
    return _peer_combine_stage(x, g2, rows_list, gates)


_IN_WIDTHS = (256, 256, 256, 256, 256, 256, 256, 4, 4, 256, 256, 128, 128, 128, 128, 128, 128, 12,
              256, 256, 256, 256)


def _dup_groups(wcols):
    g0, g1 = wcols[:, :HEAD_DIM], wcols[:, HEAD_DIM:]
    return jnp.concatenate([g0, g0, g1, g1], axis=1)


def _layout_w_in(w_in):
    offs = np.cumsum((0,) + _IN_WIDTHS)
    cols = [w_in[:, offs[i]:offs[i + 1]] for i in range(len(_IN_WIDTHS))]
    (hq, hf, hi, hg, mq, mk, mv, mi, mf, mo, nq, nkc, nvc, nks, nvs, nkw, nvw, ng, rq, rk, rv, rg) = cols
    d = w_in.shape[0]
    pad = lambda c, n: jnp.concatenate([c, jnp.zeros((d, n - c.shape[1]), w_in.dtype)], axis=1)
    main = jnp.concatenate([hq, hf, hi, hg, mq, mk, mv, mo, rq, rk, rv, rg,
                            _dup_groups(nks), _dup_groups(nkw), nq, nvs, nvw,
                            pad(jnp.concatenate([mi, mf], axis=1), LANES), pad(ng, LANES)], axis=1)
    assert main.shape[1] == N_MAIN
    kcvc = jnp.concatenate([nkc, nvc], axis=1)
    return main.astype(BF16), kcvc.astype(BF16)


def kernel(x, norm1_g, w_in, hgrn_lb, hgrn_onorm_g, mlstm_conv_w, mlstm_conv_b, mlstm_gate_b, mlstm_onorm_g, nsa_qnorm_g, nsa_knorm_g, nsa_cmp_pe, nsa_cmp_w, ret_onorm_g, w_up, w_gate, w_out, norm2_g, peer_wq, peer_keys, peer_u, peer_v):
    bsz, seq, d = x.shape
    t = bsz * seq
    depth = w_in.shape[0]
    cos_t, sin_t = _rope_lane_tables(seq)
    cos4, sin4 = jnp.tile(cos_t, (1, 2)), jnp.tile(sin_t, (1, 2))
    lb_cum = jnp.cumsum(jax.nn.softmax(hgrn_lb.astype(F32), axis=0), axis=0)
    lb_all = lb_cum - lb_cum[0:1]
    weights = []
    for l in range(depth):
        w_main, w_kcvc = _layout_w_in(w_in[l])
        weights.append(dict(
            main=w_main, kcvc=w_kcvc, gate=w_gate[l].astype(BF16), up=w_up[l].astype(BF16),
            out=w_out[l].astype(BF16), lb=_hgrn_lb_rows(lb_all[l]), wq=peer_wq[l].astype(BF16),
            keys=peer_keys[l].astype(BF16), table=_pack_tables(peer_u, peer_v, l)))

    def mixer_steps(xh, l, nb):
        wl = weights[l]
        st = {}

        def proj(dep):
            st["proj"] = _norm_matmul(xh, _after(norm1_g[l], dep), wl["main"])
            return st["proj"]

        def gates(dep):
            st["gates"] = _norm_matmul(xh, _after(norm1_g[l], dep), wl["gate"], act="sigmoid", out_dtype=BF16)
            return st["gates"]

        def hgrn(dep):
            st["oh"] = _hgrn(st["proj"], wl["lb"], _after(hgrn_onorm_g[l], dep), nb, seq)
            return st["oh"]

        def mlstm(dep):
            st["om"] = _mlstm(st["proj"], mlstm_conv_w[l], mlstm_conv_b[l], mlstm_gate_b[l],
                              _after(mlstm_onorm_g[l], dep), nb, seq)
            return st["om"]

        def ret(dep):
            st["or"] = _ret(st["proj"], cos4, sin4, _after(ret_onorm_g[l], dep), nb, seq)
            return st["or"]

        def nsa_front(dep):
            kcvc = _norm_matmul(xh, _after(norm1_g[l], dep), wl["kcvc"])
            qn, qr, ks, kw, vst, vwt = _nsa_prep(st["proj"], cos4, sin4, nsa_qnorm_g[l], nsa_knorm_g[l], nb, seq)
            o_cmp, sel = _nsa_cmp(kcvc, qn, nsa_cmp_pe[l], nsa_cmp_w[l], nsa_knorm_g[l][0], nb, seq)
            st["nsa"] = (qr, ks, kw, vst, vwt, sel, o_cmp)
            return o_cmp

        def nsa_attn(dep):
            del dep
            st["on"] = _nsa_attn(st["proj"], *st["nsa"], nb, seq)
            return st["on"]

        def merge(dep):
            del dep
            st["xm"] = _merge(xh, st["gates"], (st["oh"], st["om"], st["on"], st["or"]), wl["up"], wl["out"])
            return st["xm"]

        def route(dep):
            st["idx"], st["pgates"] = _peer_route_stage(st["xm"], _after(norm2_g[l], dep), wl["wq"], wl["keys"])
            return st["pgates"]

        return [proj, gates, hgrn, mlstm, ret, nsa_front, nsa_attn, merge, route], st

    combine_slots = (0, 8, 8, 8, 8, 8, 8, 8)

    def combine_steps(l, xm, rows_list, pgates):
        box = {"x": xm}
        tc = xm.shape[0] // len(rows_list)

        def make(c):
            def step(dep):
                box["x"] = _peer_combine(box["x"], _after(norm2_g[l], dep), rows_list[c], pgates, c * tc)
                return box["x"]
            return step

        return [make(c) for c in range(len(rows_list))], box

    n_groups = 2 if bsz % 2 == 0 else 1
    nb = bsz // n_groups
    xs = [x[g * nb:(g + 1) * nb].reshape(nb * seq, d) for g in range(n_groups)]
    dep = None
    pending = None
    for l in range(depth):
        for g in range(n_groups):
            if pending is not None and pending[0] == g:
                for cstep in pending[1]:
                    dep = cstep(dep)
                xs[g] = pending[2]["x"]
                pending = None
            msteps, st = mixer_steps(xs[g], l, nb)
            csteps = pending[1] if pending is not None else []
            ci = 0
            for si, mstep in enumerate(msteps):
                dep = mstep(dep)
                while ci < len(csteps) and (ci >= len(combine_slots) or combine_slots[ci] <= si):
                    dep = csteps[ci](dep)
                    ci += 1
            for cstep in csteps[ci:]:
                dep = cstep(dep)
            if pending is not None:
                xs[pending[0]] = pending[2]["x"]
            rows_list = _peer_gather_stage(weights[l]["table"], st["idx"], nb * seq, _sc_gather)
            csteps, box = combine_steps(l, st["xm"], rows_list, st["pgates"])
            pending = (g, csteps, box)
    for cstep in pending[1]:
        dep = cstep(dep)
    xs[pending[0]] = pending[2]["x"]
    return jnp.concatenate(xs, axis=0).reshape(bsz, seq, d)
```

```python
import functools
import math

import numpy as np
import jax
import jax.numpy as jnp
from jax import lax
from jax.experimental import pallas as pl
from jax.experimental.pallas import tpu as pltpu

F32 = jnp.float32
BF16 = jnp.bfloat16

HEAD_DIM = 64
N_HEADS = 4
MIX_WIDTH = N_HEADS * HEAD_DIM
CHUNK = 64
NORM_EPS = 1e-6
NEG_INF = -1e30
ROPE_THETA = 10000.0
CONV_W = 4
NSA_GROUPS = 2
CMP_LEN = 32
CMP_STRIDE = 16
SEL_BLOCK = 64
SEL_TOPK = 16
WINDOW = 512
FORCE_BONUS = 1e3
PEER_HEADS = 8
PEER_NKEYS = 128
PEER_TOPK = 16
PEER_KDIM = 128

LANES = 128
VMEM_LIMIT = 48 * 1024 * 1024

OFF_H, OFF_M, OFF_R, OFF_KD, OFF_NQ, OFF_V, OFF_MG, OFF_NG = 0, 1024, 2048, 3072, 3584, 3840, 4096, 4224
N_MAIN = 4352


def _cp(*sem):
    return pltpu.CompilerParams(dimension_semantics=sem, vmem_limit_bytes=VMEM_LIMIT)


def _dot(a, b):
    return jnp.dot(a, b, preferred_element_type=F32)


def _dot_nt(a, b):
    return lax.dot_general(a, b, (((1,), (1,)), ((), ())), preferred_element_type=F32)


def _dot_tn(a, b):
    return lax.dot_general(a, b, (((0,), (0,)), ((), ())), preferred_element_type=F32)


def _split3(x):
    hi = x.astype(BF16)
    r1 = x - hi.astype(F32)
    mid = r1.astype(BF16)
    lo = (r1 - mid.astype(F32)).astype(BF16)
    return hi, mid, lo


def _dot01_l(m01, x):
    hi, mid, lo = _split3(x)
    return _dot(m01, hi) + _dot(m01, mid) + _dot(m01, lo)


def _dot01_r(x, m01):
    hi, mid, lo = _split3(x)
    return _dot(hi, m01) + _dot(mid, m01) + _dot(lo, m01)


def _head_of_lane(shape, axis):
    return lax.broadcasted_iota(jnp.int32, shape, axis) // HEAD_DIM


def _block_ones(n, dtype=BF16):
    r = lax.broadcasted_iota(jnp.int32, (n, n), 0) // HEAD_DIM
    c = lax.broadcasted_iota(jnp.int32, (n, n), 1) // HEAD_DIM
    return (r == c).astype(dtype)


def _group_sum(x, ones_bd):
    hi = x.astype(BF16)
    lo = (x - hi.astype(F32)).astype(BF16)
    return _dot(hi, ones_bd) + _dot(lo, ones_bd)


def _head_rms(x, gain, ones_bd):
    ms = _group_sum(x * x, ones_bd) * (1.0 / HEAD_DIM)
    return x * lax.rsqrt(ms + NORM_EPS) * gain


def _sigmoid(x):
    return 1.0 / (1.0 + jnp.exp(-x))


def _silu(x):
    return x * _sigmoid(x)


def _log_sigmoid(x):
    return jnp.minimum(x, 0.0) - jnp.log(1.0 + jnp.exp(-jnp.abs(x)))


def _stack_heads(x, n_heads=N_HEADS):
    hl = _head_of_lane(x.shape, 1)
    return jnp.concatenate([jnp.where(hl == h, x, jnp.zeros_like(x)) for h in range(n_heads)], axis=0)


def _unstack_heads(r, c, n_heads=N_HEADS):
    hl = _head_of_lane((c, r.shape[1]), 1)
    out = jnp.zeros((c, r.shape[1]), F32)
    for h in range(n_heads):
        out = jnp.where(hl == h, r[h * c:(h + 1) * c, :], out)
    return out


def _rope(x, cos_t, sin_t):
    n = x.shape[1]
    first = (lax.broadcasted_iota(jnp.int32, x.shape, 1) % HEAD_DIM) < (HEAD_DIM // 2)
    partner = jnp.where(first, pltpu.roll(x, n - HEAD_DIM // 2, 1), pltpu.roll(x, HEAD_DIM // 2, 1))
    return x * cos_t + partner * sin_t


def _after_kernel(a_ref, dep_ref, o_ref):
    del dep_ref
    o_ref[...] = a_ref[...]


def _after(a, dep):
    if dep is None:
        return a
    a2 = a.reshape(1, a.size)
    out = pl.pallas_call(
        _after_kernel,
        in_specs=[pl.BlockSpec(a2.shape, lambda: (0, 0)), pl.BlockSpec(memory_space=pl.ANY)],
        out_specs=pl.BlockSpec(a2.shape, lambda: (0, 0)),
        out_shape=jax.ShapeDtypeStruct(a2.shape, a2.dtype),
        name="order_after",
    )(a2, dep)
    return out.reshape(a.shape)


def _norm_matmul_kernel(x_ref, g_ref, w_ref, o_ref, xn_ref, *, act):
    @pl.when(pl.program_id(1) == 0)
    def _():
        x = x_ref[...]
        ms = jnp.mean(x * x, axis=-1, keepdims=True)
        xn_ref[...] = (x * lax.rsqrt(ms + NORM_EPS) * g_ref[...]).astype(BF16)

    y = _dot(xn_ref[...], w_ref[...])
    if act == "sigmoid":
        y = _sigmoid(y)
    o_ref[...] = y.astype(o_ref.dtype)


def _norm_matmul(x, g, w, *, act=None, out_dtype=F32, tm=1024, tn=2176):
    t, d = x.shape
    w3 = w if w.ndim == 3 else w[None]
    n_per = w3.shape[2]
    tm = min(tm, t)
    tn = next(c for c in (tn, 2048, 1024, 512, 256, 128) if n_per % c == 0)
    per = n_per // tn
    n = w3.shape[0] * n_per
    assert t % tm == 0
    return pl.pallas_call(
        functools.partial(_norm_matmul_kernel, act=act),
        grid=(t // tm, n // tn),
        in_specs=[pl.BlockSpec((tm, d), lambda i, j: (i, 0)),
                  pl.BlockSpec((1, d), lambda i, j: (0, 0)),
                  pl.BlockSpec((None, d, tn), lambda i, j: (j // per, 0, j % per))],
        out_specs=pl.BlockSpec((tm, tn), lambda i, j: (i, j)),
        out_shape=jax.ShapeDtypeStruct((t, n), out_dtype),
        scratch_shapes=[pltpu.VMEM((tm, d), BF16)],
        compiler_params=_cp("parallel", "arbitrary"),
        name="norm_matmul",
    )(x, g.reshape(1, d), w3)


def _merge_kernel(x_ref, gate_ref, oh_ref, om_ref, on_ref, or_ref, wup_ref, wout_ref, o_ref):
    d = x_ref.shape[1]
    acc = None
    for m, r in enumerate((oh_ref, om_ref, on_ref, or_ref)):
        up = _dot(r[...].astype(BF16), wup_ref[m])
        term = gate_ref[:, m * d:(m + 1) * d].astype(F32) * up
        acc = term if acc is None else acc + term
    o_ref[...] = x_ref[...] + _dot(acc.astype(BF16), wout_ref[...])


def _merge(x, gates, outs, w_up, w_out, tm=512):
    t, d = x.shape
    tm = min(tm, t)
    mix = pl.BlockSpec((tm, MIX_WIDTH), lambda i: (i, 0))
    return pl.pallas_call(
        _merge_kernel,
        grid=(t // tm,),
        in_specs=[pl.BlockSpec((tm, d), lambda i: (i, 0)),
                  pl.BlockSpec((tm, 4 * d), lambda i: (i, 0)),
                  mix, mix, mix, mix,
                  pl.BlockSpec((4, MIX_WIDTH, d), lambda i: (0, 0, 0)),
                  pl.BlockSpec((d, d), lambda i: (0, 0))],
        out_specs=pl.BlockSpec((tm, d), lambda i: (i, 0)),
        out_shape=jax.ShapeDtypeStruct((t, d), F32),
        compiler_params=_cp("parallel"),
        name="merge",
    )(x, gates, *outs, w_up, w_out)


REC_BLOCK = 256


def _chunk_consts():
    t = lax.broadcasted_iota(jnp.int32, (CHUNK, CHUNK), 0)
    s = lax.broadcasted_iota(jnp.int32, (CHUNK, CHUNK), 1)
    return t, s


def _hgrn_levels():
    t = np.arange(CHUNK)
    rows = []
    masks = []
    h = CHUNK // 2
    while h >= 1:
        ref = (t // (2 * h)) * (2 * h) + h
        p = np.zeros((CHUNK, CHUNK), np.float32)
        p[t, np.minimum(ref, CHUNK - 1)] = 1.0
        rows.append(p)
        same = (t[:, None] // (2 * h)) == (t[None, :] // (2 * h))
        m = same & ((t[:, None] // h) % 2 == 1) & ((t[None, :] // h) % 2 == 0)
        masks.append(m.astype(np.float32))
        h //= 2
    masks.append(np.eye(CHUNK, dtype=np.float32))
    return np.concatenate(rows, 0), np.stack(masks, 0)


def _hgrn_kernel(p_ref, lb_ref, g_ref, psel_ref, lmask_ref, o_ref, st_ref):
    @pl.when(pl.program_id(1) == 0)
    def _():
        st_ref[...] = jnp.zeros_like(st_ref)

    c = CHUNK
    w = MIX_WIDTH
    ones_bd = _block_ones(w)
    bd_mask = _block_ones(w, F32)
    tri = (lax.broadcasted_iota(jnp.int32, (c, c), 0) >= lax.broadcasted_iota(jnp.int32, (c, c), 1)).astype(BF16)
    psel = psel_ref[...]
    n_lv = lmask_ref.shape[0]
    log_lb, log_1mlb, one_mlb = lb_ref[0:1, :], lb_ref[1:2, :], lb_ref[2:3, :]
    gain = g_ref[...]

    def chunk(ci, carry):
        r0 = pl.multiple_of(ci * c, c)
        q = _silu(p_ref[pl.ds(r0, c), 0:w])
        fl = p_ref[pl.ds(r0, c), w:2 * w]
        v = p_ref[pl.ds(r0, c), 2 * w:3 * w]
        gp = p_ref[pl.ds(r0, c), 3 * w:4 * w]
        a1 = jnp.broadcast_to(log_lb, fl.shape)
        a2 = log_1mlb + _log_sigmoid(fl)
        mx = jnp.maximum(a1, a2)
        log_f = mx + jnp.log(jnp.exp(a1 - mx) + jnp.exp(a2 - mx))
        k = one_mlb * _sigmoid(-fl)
        b = _dot01_l(tri, log_f)
        bref = _dot01_l(psel, b)
        vb = v.astype(BF16)
        a = jnp.zeros((N_HEADS * c, c), F32)
        for lv in range(n_lv):
            if lv < n_lv - 1:
                br = bref[lv * c:(lv + 1) * c, :]
                qs = q * jnp.exp(jnp.minimum(b - br, 0.0))
                ks = k * jnp.exp(jnp.minimum(br - b, 0.0))
            else:
                qs, ks = q, k
            s_lv = _dot_nt(_stack_heads(qs).astype(BF16), ks.astype(BF16))
            a = a + jnp.concatenate([lmask_ref[lv]] * N_HEADS, axis=0) * s_lv
        o = _unstack_heads(_dot(a.astype(BF16), vb), c)
        st = st_ref[...]
        o = o + _dot_nt((q * jnp.exp(b)).astype(BF16), st.astype(BF16))
        b_last = b[c - 1:c, :]
        kb = k * jnp.exp(b_last - b)
        st_ref[...] = st * jnp.exp(b_last) + bd_mask * _dot_tn(vb, kb.astype(BF16))
        y = _head_rms(o, gain, ones_bd) * _silu(gp)
        o_ref[pl.ds(r0, c), :] = y
        return carry

    lax.fori_loop(0, p_ref.shape[0] // c, chunk, 0)


def _hgrn(proj, lb_rows, gain, bsz, seq):
    psel, lmask = _hgrn_levels()
    tb = min(REC_BLOCK, seq)
    nb = seq // tb
    return pl.pallas_call(
        _hgrn_kernel,
        grid=(bsz, nb),
        in_specs=[pl.BlockSpec((tb, 4 * MIX_WIDTH), lambda b, i: (b * nb + i, OFF_H // (4 * MIX_WIDTH))),
                  pl.BlockSpec((8, MIX_WIDTH), lambda b, i: (0, 0)),
                  pl.BlockSpec((1, MIX_WIDTH), lambda b, i: (0, 0)),
                  pl.BlockSpec(psel.shape, lambda b, i: (0, 0)),
                  pl.BlockSpec(lmask.shape, lambda b, i: (0, 0, 0))],
        out_specs=pl.BlockSpec((tb, MIX_WIDTH), lambda b, i: (b * nb + i, 0)),
        out_shape=jax.ShapeDtypeStruct((bsz * seq, MIX_WIDTH), F32),
        scratch_shapes=[pltpu.VMEM((MIX_WIDTH, MIX_WIDTH), F32)],
        compiler_params=_cp("parallel", "arbitrary"),
        name="hgrn2",
    )(proj, lb_rows, gain.reshape(1, MIX_WIDTH), jnp.asarray(psel, BF16), jnp.asarray(lmask, F32))


def _ret_kernel(p_ref, cos_ref, sin_ref, dec_ref, decin_ref, g_ref, o_ref, st_ref):
    @pl.when(pl.program_id(1) == 0)
    def _():
        st_ref[...] = jnp.zeros_like(st_ref)

    c = CHUNK
    w = MIX_WIDTH
    ones_bd = _block_ones(w)
    bd_mask = _block_ones(w, F32)
    gain = g_ref[...]
    dec_q = dec_ref[0:c, :]
    dec_k = dec_ref[c:2 * c, :]
    dec_state = dec_ref[2 * c:2 * c + 1, :]
    dec_in = decin_ref[...]

    def chunk(ci, carry):
        r0 = pl.multiple_of(ci * c, c)
        cos_t = cos_ref[pl.ds(r0, c), :]
        sin_t = sin_ref[pl.ds(r0, c), :]
        q = _rope(p_ref[pl.ds(r0, c), 0:w], cos_t, sin_t)
        k = _rope(p_ref[pl.ds(r0, c), w:2 * w], cos_t, sin_t) * (HEAD_DIM ** -0.5)
        v = p_ref[pl.ds(r0, c), 2 * w:3 * w]
        gp = p_ref[pl.ds(r0, c), 3 * w:4 * w]
        vb = v.astype(BF16)
        a = _dot_nt(_stack_heads(q).astype(BF16), k.astype(BF16)) * dec_in
        o = _unstack_heads(_dot(a.astype(BF16), vb), c)
        st = st_ref[...]
        o = o + _dot_nt(q.astype(BF16), st.astype(BF16)) * dec_q
        st_ref[...] = st * dec_state + bd_mask * _dot_tn(vb, (k * dec_k).astype(BF16))
        o_ref[pl.ds(r0, c), :] = _head_rms(o, gain, ones_bd) * _silu(gp)
        return carry

    lax.fori_loop(0, p_ref.shape[0] // c, chunk, 0)


def _ret_consts():
    log_gamma = np.log1p(-np.exp2(-5.0 - np.arange(N_HEADS, dtype=np.float64)))
    t = np.arange(CHUNK, dtype=np.float64)
    lane_h = np.arange(MIX_WIDTH) // HEAD_DIM
    dec_q = np.exp(log_gamma[lane_h][None, :] * (t[:, None] + 1.0))
    dec_k = np.exp(log_gamma[lane_h][None, :] * (CHUNK - 1.0 - t[:, None]))
    dec_state = np.exp(log_gamma[lane_h] * CHUNK)[None, :]
    dec = np.concatenate([dec_q, dec_k, np.broadcast_to(dec_state, (8, MIX_WIDTH))], 0)
    diff = t[:, None] - t[None, :]
    dec_in = np.concatenate([np.where(diff >= 0, np.exp(log_gamma[h] * diff), 0.0) for h in range(N_HEADS)], 0)
    return dec.astype(np.float32), dec_in.astype(np.float32)


def _ret(proj, cos4, sin4, gain, bsz, seq):
    dec, dec_in = _ret_consts()
    tb = min(REC_BLOCK, seq)
    nb = seq // tb
    return pl.pallas_call(
        _ret_kernel,
        grid=(bsz, nb),
        in_specs=[pl.BlockSpec((tb, 4 * MIX_WIDTH), lambda b, i: (b * nb + i, OFF_R // (4 * MIX_WIDTH))),
                  pl.BlockSpec((tb, MIX_WIDTH), lambda b, i: (i, 0)),
                  pl.BlockSpec((tb, MIX_WIDTH), lambda b, i: (i, 0)),
                  pl.BlockSpec(dec.shape, lambda b, i: (0, 0)),
                  pl.BlockSpec(dec_in.shape, lambda b, i: (0, 0)),
                  pl.BlockSpec((1, MIX_WIDTH), lambda b, i: (0, 0))],
        out_specs=pl.BlockSpec((tb, MIX_WIDTH), lambda b, i: (b * nb + i, 0)),
        out_shape=jax.ShapeDtypeStruct((bsz * seq, MIX_WIDTH), F32),
        scratch_shapes=[pltpu.VMEM((MIX_WIDTH, MIX_WIDTH), F32)],
        compiler_params=_cp("parallel", "arbitrary"),
        name="retention",
    )(proj, cos4, sin4, jnp.asarray(dec), jnp.asarray(dec_in), gain.reshape(1, MIX_WIDTH))


def _hgrn_lb_rows(lb):
    lb = lb.astype(F32)
    rows = jnp.stack([jnp.log(lb), jnp.log1p(-lb), 1.0 - lb], 0)
    return jnp.concatenate([rows, jnp.zeros((5, lb.shape[0]), F32)], 0)


def _rope_lane_tables(seq):
    inv = 1.0 / (ROPE_THETA ** (jnp.arange(0, HEAD_DIM, 2, dtype=F32) / HEAD_DIM))
    ang = jnp.arange(seq, dtype=F32)[:, None] * inv[None, :]
    cos, sin = jnp.cos(ang), jnp.sin(ang)
    cos_t = jnp.tile(cos, (1, LANES // (HEAD_DIM // 2)))
    sin_t = jnp.tile(jnp.concatenate([-sin, sin], axis=1), (1, LANES // HEAD_DIM))
    return cos_t, sin_t


def _expand_heads(cols, shape):
    hl = _head_of_lane(shape, 1)
    out = jnp.broadcast_to(cols[-1], shape)
    for h in range(len(cols) - 2, -1, -1):
        out = jnp.where(hl == h, jnp.broadcast_to(cols[h], shape), out)
    return out


def _mlstm_kernel(p_ref, gcol_ref, grow_ref, cw_ref, cb_ref, gbr_ref, gbc_ref, g_ref, o_ref,
                  ct_ref, n_ref, m_ref, hist_ref, cbuf_ref, qk_ref):
    c = CHUNK
    w = MIX_WIDTH
    tb = p_ref.shape[0]

    @pl.when(pl.program_id(1) == 0)
    def _():
        ct_ref[...] = jnp.zeros_like(ct_ref)
        n_ref[...] = jnp.zeros_like(n_ref)
        m_ref[...] = jnp.zeros_like(m_ref)
        hist_ref[...] = jnp.zeros_like(hist_ref)

    cbuf_ref[0:8, :] = hist_ref[...]
    cbuf_ref[8:, :] = p_ref[:, 0:2 * w]
    hist_ref[...] = p_ref[tb - 8:tb, 0:2 * w]
    acc = jnp.broadcast_to(cb_ref[...], (tb, 2 * w))
    for j in range(CONV_W):
        acc = acc + cw_ref[j:j + 1, :] * cbuf_ref[pl.ds(8 - (CONV_W - 1) + j, tb), :]
    qk_ref[...] = _silu(acc)

    ones_bd = _block_ones(w)
    bd_mask = _block_ones(w, F32)
    ti = lax.broadcasted_iota(jnp.int32, (c, c), 0)
    si = lax.broadcasted_iota(jnp.int32, (c, c), 1)
    causal = ti >= si
    tri = causal.astype(BF16)
    tri_t = (ti <= si).astype(BF16)
    gain = g_ref[...]
    ones_ext = jnp.ones((c, LANES), BF16)

    def chunk(ci, carry):
        r0 = pl.multiple_of(ci * c, c)
        q = qk_ref[pl.ds(r0, c), 0:w]
        k = qk_ref[pl.ds(r0, c), w:2 * w] * (HEAD_DIM ** -0.5)
        v = p_ref[pl.ds(r0, c), 2 * w:3 * w]
        op = p_ref[pl.ds(r0, c), 3 * w:4 * w]
        gc = gcol_ref[pl.ds(r0, c), :] + gbr_ref[...]
        gr = grow_ref[ci] + gbc_ref[...]
        b_c = _dot01_l(tri, _log_sigmoid(gc))
        b_r = _dot01_r(_log_sigmoid(gr), tri_t)
        wd, s_inter, em, wk, decay = [], [], [], [], []
        for h in range(N_HEADS):
            bc = b_c[:, N_HEADS + h:N_HEADS + h + 1]
            lic = gc[:, h:h + 1]
            br = b_r[N_HEADS + h:N_HEADS + h + 1, :]
            lir = gr[h:h + 1, :]
            dmat = jnp.where(causal, bc - br + lir, -jnp.inf)
            m_prev = m_ref[h:h + 1, 0:1]
            inter = bc + m_prev
            mrow = jnp.maximum(inter, jnp.max(dmat, axis=1, keepdims=True))
            wd.append(jnp.exp(dmat - mrow))
            s_inter.append(jnp.exp(inter - mrow))
            em.append(jnp.exp(-mrow))
            b_last = br[:, c - 1:c]
            m_new = jnp.maximum(b_last + m_prev, jnp.max(b_last - br + lir, axis=1, keepdims=True))
            wk.append(jnp.exp(b_last - bc + lic - m_new))
            decay.append(jnp.exp(b_last + m_prev - m_new))
            m_ref[h:h + 1, :] = jnp.broadcast_to(m_new, (1, LANES))
        s_inter_l = _expand_heads(s_inter, (c, w))
        em_l = _expand_heads(em, (c, w))
        wk_l = _expand_heads(wk, (c, w))
        decay_l = _expand_heads(decay, (1, w))
        qk = _dot_nt(_stack_heads(q).astype(BF16), k.astype(BF16))
        wmat = jnp.concatenate(wd, axis=0) * qk
        vb = v.astype(BF16)
        r = _dot(wmat.astype(BF16), jnp.concatenate([vb, ones_ext], axis=1))
        num_intra = _unstack_heads(r[:, 0:w], c)
        rs_l = _expand_heads([r[h * c:(h + 1) * c, w:w + 1] for h in range(N_HEADS)], (c, w))
        ct = ct_ref[...]
        nrow = n_ref[0:1, :]
        num = s_inter_l * _dot_nt(q.astype(BF16), ct.astype(BF16)) + num_intra
        den = s_inter_l * _group_sum(q * nrow, ones_bd) + rs_l
        hval = num / jnp.maximum(jnp.abs(den), em_l)
        kw = wk_l * k
        ct_ref[...] = ct * decay_l + bd_mask * _dot_tn(vb, kw.astype(BF16))
        n_ref[0:1, :] = nrow * decay_l + jnp.sum(kw, axis=0, keepdims=True)
        o_ref[pl.ds(r0, c), :] = _head_rms(hval, gain, ones_bd) * _sigmoid(op)
        return carry

    lax.fori_loop(0, tb // c, chunk, 0)


def _mlstm(proj, conv_w, conv_b, gate_b, gain, bsz, seq):
    t = bsz * seq
    w = MIX_WIDTH
    tb = min(REC_BLOCK, seq)
    nb = seq // tb
    ncb = tb // CHUNK
    grow = proj[:, OFF_MG:OFF_MG + 8].reshape(t // CHUNK, CHUNK, 8).transpose(0, 2, 1)
    gb_row = jnp.zeros((1, LANES), F32).at[0, 0:8].set(gate_b.astype(F32))
    gb_col = gate_b.astype(F32).reshape(8, 1)
    return pl.pallas_call(
        _mlstm_kernel,
        grid=(bsz, nb),
        in_specs=[pl.BlockSpec((tb, 4 * w), lambda b, i: (b * nb + i, OFF_M // (4 * w))),
                  pl.BlockSpec((tb, LANES), lambda b, i: (b * nb + i, OFF_MG // LANES)),
                  pl.BlockSpec((ncb, 8, CHUNK), lambda b, i: (b * nb + i, 0, 0)),
                  pl.BlockSpec((CONV_W, 2 * w), lambda b, i: (0, 0)),
                  pl.BlockSpec((1, 2 * w), lambda b, i: (0, 0)),
                  pl.BlockSpec((1, LANES), lambda b, i: (0, 0)),
                  pl.BlockSpec((8, 1), lambda b, i: (0, 0)),
                  pl.BlockSpec((1, w), lambda b, i: (0, 0))],
        out_specs=pl.BlockSpec((tb, w), lambda b, i: (b * nb + i, 0)),
        out_shape=jax.ShapeDtypeStruct((t, w), F32),
        scratch_shapes=[pltpu.VMEM((w, w), F32), pltpu.VMEM((8, w), F32), pltpu.VMEM((8, LANES), F32),
                        pltpu.VMEM((8, 2 * w), F32), pltpu.VMEM((tb + 8, 2 * w), F32),
                        pltpu.VMEM((tb, 2 * w), F32)],
        compiler_params=_cp("parallel", "arbitrary"),
        name="mlstm",
    )(proj, proj, grow, conv_w.astype(F32), conv_b.astype(F32).reshape(1, 2 * w), gb_row, gb_col,
      gain.reshape(1, w))


NSA_TQ = 128
NSA_KC = 512
GW = 2 * HEAD_DIM


def _nsa_prep_kernel(pq_ref, pk_ref, pv_ref, cos_ref, sin_ref, qg_ref, kg_ref,
                     qn_ref, qr_ref, ks_ref, kw_ref, vst_ref, vwt_ref):
    w = MIX_WIDTH
    for src, dst in ((pv_ref[:, 0:GW], vst_ref), (pv_ref[:, GW:2 * GW], vwt_ref)):
        vt = src.T
        tk = dst.shape[4]
        for g in range(NSA_GROUPS):
            rows = vt[g * HEAD_DIM:(g + 1) * HEAD_DIM, :]
            dup = jnp.concatenate([rows, rows], axis=0).astype(BF16)
            for j in range(dst.shape[2]):
                dst[0, g, j] = dup[:, j * tk:(j + 1) * tk]
    ones_bd = _block_ones(w)
    cos_t, sin_t = cos_ref[...], sin_ref[...]
    scale = HEAD_DIM ** -0.5
    qh = _head_rms(pq_ref[...], qg_ref[...], ones_bd)
    qn_ref[...] = (qh * scale).astype(BF16)
    qr_ref[...] = (_rope(qh, cos_t, sin_t) * scale).astype(BF16)
    ks_ref[...] = _rope(_head_rms(pk_ref[:, 0:w], kg_ref[1:2, :], ones_bd), cos_t, sin_t).astype(BF16)
    kw_ref[...] = _rope(_head_rms(pk_ref[:, w:2 * w], kg_ref[2:3, :], ones_bd), cos_t, sin_t).astype(BF16)


def _nsa_prep(proj, cos4, sin4, qnorm_g, knorm_g, bsz, seq):
    t = bsz * seq
    w = MIX_WIDTH
    tm = min(NSA_KC, seq)
    tq = min(NSA_TQ, seq)
    ns = seq // tm
    qg = jnp.tile(qnorm_g.astype(F32), w // HEAD_DIM).reshape(1, w)
    kg = jnp.concatenate([jnp.tile(knorm_g.astype(F32), (1, w // HEAD_DIM)), jnp.zeros((5, w), F32)], axis=0)
    out = jax.ShapeDtypeStruct((t, w), BF16)
    row = pl.BlockSpec((tm, w), lambda i: (i, 0))
    return pl.pallas_call(
        _nsa_prep_kernel,
        grid=(t // tm,),
        in_specs=[pl.BlockSpec((tm, w), lambda i: (i, OFF_NQ // w)),
                  pl.BlockSpec((tm, 2 * w), lambda i: (i, OFF_KD // (2 * w))),
                  pl.BlockSpec((tm, 2 * GW), lambda i: (i, OFF_V // (2 * GW))),
                  pl.BlockSpec((tm, w), lambda i: (i % ns, 0)),
                  pl.BlockSpec((tm, w), lambda i: (i % ns, 0)),
                  pl.BlockSpec((1, w), lambda i: (0, 0)),
                  pl.BlockSpec((8, w), lambda i: (0, 0))],
        out_specs=[row, row, row, row,
                   pl.BlockSpec((1, NSA_GROUPS, 1, GW, tm), lambda i: (i // ns, 0, i % ns, 0, 0)),
                   pl.BlockSpec((1, NSA_GROUPS, tm // tq, GW, tq), lambda i: (i // ns, 0, i % ns, 0, 0))],
        out_shape=[out, out, out, out,
                   jax.ShapeDtypeStruct((bsz, NSA_GROUPS, seq // tm, GW, tm), BF16),
                   jax.ShapeDtypeStruct((bsz, NSA_GROUPS, seq // tq, GW, tq), BF16)],
        compiler_params=_cp("parallel"),
        name="nsa_prep",
    )(proj, proj, proj, cos4, sin4, qg, kg)


def _nsa_cmp_kernel(xr_ref, pe_ref, w0_ref, w1_ref, kg_ref, ovt_ref, qn_ref, ocmp_ref, sel_ref,
                    kc_ref, vc_ref, v_ref, *, n_top):
    tq = qn_ref.shape[0]
    nr = xr_ref.shape[0]
    nsel = sel_ref.shape[2]
    w = MIX_WIDTH

    @pl.when(pl.program_id(1) == 0)
    def _():
        xr = xr_ref[...]
        y0 = _dot((xr + pe_ref[0]).astype(BF16), w0_ref[...])
        y1 = _dot((xr + pe_ref[1]).astype(BF16), w1_ref[...])
        kv = y0 + pltpu.roll(y1, nr - 1, 0)
        kc_ref[...] = _head_rms(kv[:, 0:w], kg_ref[...], _block_ones(w)).astype(BF16)
        vc_ref[...] = kv[:, w:2 * w].astype(BF16)

    pos0 = pl.program_id(1) * tq
    hl = _head_of_lane((tq, GW), 1)
    pos_r = pos0 + lax.broadcasted_iota(jnp.int32, (tq, nr), 0)
    valid = lax.broadcasted_iota(jnp.int32, (tq, nr), 1) * CMP_STRIDE + (CMP_LEN - 1) <= pos_r
    pos_c = pos0 + lax.broadcasted_iota(jnp.int32, (nr, tq), 1)
    valid_t = lax.broadcasted_iota(jnp.int32, (nr, tq), 0) * CMP_STRIDE + (CMP_LEN - 1) <= pos_c
    jrow = lax.broadcasted_iota(jnp.int32, (nsel, tq), 0)
    cur = (pos0 + lax.broadcasted_iota(jnp.int32, (nsel, tq), 1)) // SEL_BLOCK
    forced = (jrow == 0) | (jrow == cur) | (jrow == cur - 1)
    ovt = ovt_ref[...]

    for g in range(NSA_GROUPS):
        qg = qn_ref[:, g * GW:(g + 1) * GW]
        kg = kc_ref[:, g * GW:(g + 1) * GW]
        vg = vc_ref[:, g * GW:(g + 1) * GW]
        o_g = jnp.zeros((tq, GW), F32)
        pt_sum = jnp.zeros((nr, tq), F32)
        for hh in range(2):
            qm = jnp.where(hl == hh, qg, jnp.zeros_like(qg))
            s = jnp.where(valid, _dot_nt(qm, kg), NEG_INF)
            e = jnp.exp(s - jnp.max(s, axis=1, keepdims=True))
            p = jnp.where(valid, e / jnp.sum(e, axis=1, keepdims=True), 0.0)
            o_g = jnp.where(hl == hh, _dot(p.astype(BF16), vg), o_g)
            st = jnp.where(valid_t, _dot_nt(kg, qm), NEG_INF)
            et = jnp.exp(st - jnp.max(st, axis=0, keepdims=True))
            pt_sum = pt_sum + jnp.where(valid_t, et / jnp.sum(et, axis=0, keepdims=True), 0.0)
        ocmp_ref[:, g * GW:(g + 1) * GW] = o_g
        p_hi = pt_sum.astype(BF16)
        p_lo = (pt_sum - p_hi.astype(F32)).astype(BF16)
        imp = _dot(ovt, p_hi) + _dot(ovt, p_lo)
        val = jnp.where(jrow <= cur, imp + FORCE_BONUS * forced.astype(F32), NEG_INF)
        v_ref[...] = val

        def rank(jp, cnt):
            row = v_ref[pl.ds(jp, 1), :]
            tie = jnp.where(jrow > jp, 1.0, 0.0)
            return cnt + jnp.where(row > val, 1.0, jnp.where(row == val, tie, 0.0))

        cnt = lax.fori_loop(0, nsel, rank, jnp.zeros((nsel, tq), F32))
        sel_ref[0, g] = ((cnt < n_top) & (jrow <= cur)).astype(F32)


def _nsa_cmp_weights(cmp_pe, cmp_w):
    half = CMP_LEN // 2
    wl = cmp_w.astype(F32).reshape(2, 2, half, HEAD_DIM, HEAD_DIM)
    eye2 = jnp.eye(2, dtype=F32)
    w2 = jnp.einsum('kardz,kK,gG,h->arkgdKGhz', wl, eye2, eye2, jnp.ones((2,), F32))
    w2 = w2.reshape(2, half * 4 * HEAD_DIM, 8 * HEAD_DIM)
    pl_ = cmp_pe.astype(F32).reshape(2, 2, half, HEAD_DIM)
    pe2 = jnp.broadcast_to(pl_.transpose(1, 2, 0, 3)[:, :, :, None, :], (2, half, 2, 2, HEAD_DIM))
    return w2.astype(BF16), pe2.reshape(2, 1, half * 4 * HEAD_DIM)


def _nsa_cmp(kcvc, qn, cmp_pe, cmp_w, knorm0, bsz, seq, tq=512):
    t = bsz * seq
    w = MIX_WIDTH
    tq = min(tq, seq)
    nq = seq // tq
    nr = seq // CMP_STRIDE
    nsel = seq // SEL_BLOCK
    n_top = min(SEL_TOPK, nsel)
    w2, pe2 = _nsa_cmp_weights(cmp_pe, cmp_w)
    xr = kcvc.reshape(t // CMP_STRIDE, CMP_STRIDE * w)
    kg = jnp.tile(knorm0.astype(F32), w // HEAD_DIM).reshape(1, w)
    n_i = np.arange(nr)[:, None] * CMP_STRIDE
    j_i = np.arange(nsel)[None, :] * SEL_BLOCK
    ov = ((n_i < j_i + SEL_BLOCK) & (n_i + CMP_LEN > j_i)).astype(np.float32)
    ov[nr - 1, :] = 0.0
    kin = CMP_STRIDE * w
    return pl.pallas_call(
        functools.partial(_nsa_cmp_kernel, n_top=n_top),
        grid=(bsz, nq),
        in_specs=[pl.BlockSpec((nr, kin), lambda b, i: (b, 0)),
                  pl.BlockSpec((2, 1, kin), lambda b, i: (0, 0, 0)),
                  pl.BlockSpec((None, kin, 2 * w), lambda b, i: (0, 0, 0)),
                  pl.BlockSpec((None, kin, 2 * w), lambda b, i: (1, 0, 0)),
                  pl.BlockSpec((1, w), lambda b, i: (0, 0)),
                  pl.BlockSpec((nsel, nr), lambda b, i: (0, 0)),
                  pl.BlockSpec((tq, w), lambda b, i: (b * nq + i, 0))],
        out_specs=[pl.BlockSpec((tq, w), lambda b, i: (b * nq + i, 0)),
                   pl.BlockSpec((1, NSA_GROUPS, nsel, tq), lambda b, i: (b, 0, 0, i))],
        out_shape=[jax.ShapeDtypeStruct((t, w), F32),
                   jax.ShapeDtypeStruct((bsz, NSA_GROUPS, nsel, seq), F32)],
        scratch_shapes=[pltpu.VMEM((nr, w), BF16), pltpu.VMEM((nr, w), BF16), pltpu.VMEM((nsel, tq), F32)],
        compiler_params=_cp("parallel", "arbitrary"),
        name="nsa_cmp",
    )(xr, pe2, w2, w2, kg, jnp.asarray(ov.T, BF16), qn)


def _nsa_attn_kernel(qr_ref, ks_ref, kw_ref, vs_ref, vw_ref, sel_ref, ocmp_ref, gate_ref, o_ref, *, kc, wt):
    tq = qr_ref.shape[0]
    i = pl.program_id(2)
    g = pl.program_id(1)
    hl = _head_of_lane((tq, GW), 1)
    q = qr_ref[...]
    qs = jnp.concatenate([jnp.where(hl == 0, q, jnp.zeros_like(q)), jnp.where(hl == 1, q, jnp.zeros_like(q))], axis=0)
    nbk = kc // SEL_BLOCK

    def lane_qpos(rows):
        return i * tq + lax.broadcasted_iota(jnp.int32, (rows, 2 * tq), 1) % tq

    def finish(acc, l):
        ot = (acc / l).T
        return jnp.where(hl == 0, ot[0:tq, :], ot[tq:2 * tq, :])

    qpos_s = lane_qpos(kc)
    krow_s = lax.broadcasted_iota(jnp.int32, (kc, 2 * tq), 0)

    def sel_body(c, carry, diagonal):
        m, l, acc = carry
        k0 = pl.multiple_of(c * kc, kc)
        st = _dot_nt(ks_ref[pl.ds(k0, kc), :], qs)
        srows = sel_ref[0, 0, pl.ds(pl.multiple_of(c * nbk, nbk), nbk), :]
        srows = jnp.concatenate([srows, srows], axis=1)
        smask = jnp.concatenate([jnp.broadcast_to(srows[r:r + 1, :], (SEL_BLOCK, 2 * tq)) for r in range(nbk)],
                                axis=0)
        msk = smask > 0.5
        if diagonal:
            msk = msk & (k0 + krow_s <= qpos_s)
        st = jnp.where(msk, st, NEG_INF)
        m_new = jnp.maximum(m, jnp.max(st, axis=0, keepdims=True))
        p = jnp.exp(st - m_new)
        alpha = jnp.exp(m - m_new)
        l = l * alpha + jnp.sum(p, axis=0, keepdims=True)
        acc = acc * alpha + _dot(vs_ref[0, 0, c], p.astype(BF16))
        return m_new, l, acc

    init = (jnp.full((1, 2 * tq), NEG_INF, F32), jnp.zeros((1, 2 * tq), F32), jnp.zeros((GW, 2 * tq), F32))
    n_before = (i * tq) // kc
    carry = lax.fori_loop(0, n_before, functools.partial(sel_body, diagonal=False), init)
    _, l_s, acc_s = sel_body(n_before, carry, True)
    o_sel = finish(acc_s, l_s)

    j0 = jnp.maximum(i - (wt - 1), 0)
    k0 = pl.multiple_of(j0 * tq, tq)
    span = wt * tq
    st = _dot_nt(kw_ref[pl.ds(k0, span), :], qs)
    kpos = k0 + lax.broadcasted_iota(jnp.int32, (span, 2 * tq), 0)
    qpos_w = lane_qpos(span)
    msk = (kpos <= qpos_w) & (kpos > qpos_w - WINDOW)
    st = jnp.where(msk, st, NEG_INF)
    p = jnp.exp(st - jnp.max(st, axis=0, keepdims=True))
    vt =jnp.concatenate([vw_ref[0, 0, j0 + r] for r in range(wt)], axis=1)
    o_win = finish(_dot(vt, p.astype(BF16)), jnp.sum(p, axis=0, keepdims=True))

    gb = _sigmoid(gate_ref[...])

    def gate(branch):
        cols = []
        for hh in range(2):
            c0 = gb[:, hh * 3 + branch:hh * 3 + branch + 1]
            c1 = gb[:, (2 + hh) * 3 + branch:(2 + hh) * 3 + branch + 1]
            cols.append(jnp.where(g == 0, c0, c1))
        return _expand_heads(cols, (tq, GW))

    o_ref[...] = gate(0) * ocmp_ref[...] + gate(1) * o_sel + gate(2) * o_win


def _nsa_attn(proj, qr, ks, kw, vst, vwt, sel, o_cmp, bsz, seq):
    t = bsz * seq
    w = MIX_WIDTH
    tq = min(NSA_TQ, seq)
    nq = seq // tq
    nsel = seq // SEL_BLOCK
    kc = min(NSA_KC, seq)
    wt = min(WINDOW // tq + 1, nq)
    kspec = pl.BlockSpec((seq, GW), lambda b, g, i: (b, g))
    return pl.pallas_call(
        functools.partial(_nsa_attn_kernel, kc=kc, wt=wt),
        grid=(bsz, NSA_GROUPS, nq),
        in_specs=[pl.BlockSpec((tq, GW), lambda b, g, i: (b * nq + i, g)),
                  kspec, kspec,
                  pl.BlockSpec((1, 1, seq // kc, GW, kc), lambda b, g, i: (b, g, 0, 0, 0)),
                  pl.BlockSpec((1, 1, nq, GW, tq), lambda b, g, i: (b, g, 0, 0, 0)),
                  pl.BlockSpec((1, 1, nsel, tq), lambda b, g, i: (b, g, 0, i)),
                  pl.BlockSpec((tq, GW), lambda b, g, i: (b * nq + i, g)),
                  pl.BlockSpec((tq, LANES), lambda b, g, i: (b * nq + i, OFF_NG // LANES))],
        out_specs=pl.BlockSpec((tq, GW), lambda b, g, i: (b * nq + i, g)),
        out_shape=jax.ShapeDtypeStruct((t, w), F32),
        compiler_params=_cp("parallel", "parallel", "arbitrary"),
        name="nsa_attn",
    )(qr, ks, kw, vst, vwt, sel, o_cmp, proj)


def _nsa(proj, kcvc, cos4, sin4, qnorm_g, knorm_g, cmp_pe, cmp_w, bsz, seq):
    qn, qr, ks, kw, vst, vwt = _nsa_prep(proj, cos4, sin4, qnorm_g, knorm_g, bsz, seq)
    o_cmp, sel = _nsa_cmp(kcvc, qn, cmp_pe, cmp_w, knorm_g[0], bsz, seq)
    return _nsa_attn(proj, qr, ks, kw, vst, vwt, sel, o_cmp, bsz, seq)


PEER_TT = 128
PEER_CT = 8
HALF_D = 512


SUBLANES = 8
CODE_BITS = 127
FAR_BELOW = -3.0e38


def _with_code(x, code):
    bits = lax.bitcast_convert_type(x, jnp.int32)
    return lax.bitcast_convert_type((bits & ~CODE_BITS) | code, F32)


def _split_code(x):
    bits = lax.bitcast_convert_type(x, jnp.int32)
    return lax.bitcast_convert_type(bits & ~CODE_BITS, F32), bits & CODE_BITS


def _sort16_desc(xs):
    xs = list(xs)
    n = len(xs)
    k = 2
    while k <= n:
        j = k // 2
        while j >= 1:
            for i in range(n):
                l = i ^ j
                if l > i:
                    hi, lo = jnp.maximum(xs[i], xs[l]), jnp.minimum(xs[i], xs[l])
                    xs[i], xs[l] = (hi, lo) if (i & k) == 0 else (lo, hi)
            j //= 2
        k *= 2
    return xs


def _merge16_desc(xs):
    xs = list(xs)
    j = len(xs) // 2
    while j >= 1:
        for i in range(len(xs)):
            l = i ^ j
            if l > i:
                xs[i], xs[l] = jnp.maximum(xs[i], xs[l]), jnp.minimum(xs[i], xs[l])
        j //= 2
    return xs


def _top16_columns(x):
    n = PEER_TOPK
    xs = _sort16_desc([x[SUBLANES * j:SUBLANES * (j + 1), :] for j in range(n)])
    shift = SUBLANES // 2
    while shift >= 1:
        rolled = [pltpu.roll(a, shift, 0) for a in xs]
        xs = _merge16_desc([jnp.maximum(xs[i], rolled[n - 1 - i]) for i in range(n)])
        shift //= 2
    return xs


_PEER_CAND_TILES = ((0, 0, 8), (0, 1, 8), (1, 0, 8), (2, 0, 5), (3, 0, 4), (4, 0, 3), (5, 0, 2), (6, 0, 2), (7, 0, 2))


def _peer_route_kernel(q_ref, key_ref, e_ref, g_ref, e_scr, g_scr):
    tt = q_ref.shape[0]
    nk = PEER_NKEYS
    n = PEER_TOPK
    row = lax.broadcasted_iota(jnp.int32, (nk, tt), 0)
    sub = lax.broadcasted_iota(jnp.int32, (SUBLANES, tt), 0)
    vals, ids = [], []
    for p in range(2):
        st = _dot_nt(key_ref[0, p], q_ref[:, p * PEER_KDIM:(p + 1) * PEER_KDIM])
        top = [_split_code(a) for a in _top16_columns(_with_code(st, (nk - 1) - row))]
        vals.append([v for v, _ in top])
        ids.append([(nk - 1) - c for _, c in top])
    (v1, v2), (i1, i2) = vals, ids

    def stack(xs, lo):
        out = xs[lo]
        for s in range(1, SUBLANES):
            out = jnp.where(sub == s, xs[lo + s], out)
        return out

    v2t, i2t = (stack(v2, 0), stack(v2, SUBLANES)), (stack(i2, 0), stack(i2, SUBLANES))
    cand, cexp = [], []
    for a, tile, nvalid in _PEER_CAND_TILES:
        v = v1[a] + v2t[tile]
        cand.append(v if nvalid == SUBLANES else jnp.where(sub < nvalid, v, FAR_BELOW))
        cexp.append(i1[a] * nk + i2t[tile])
    cand.append(stack(v1, SUBLANES) + v2[0])
    cexp.append(stack(i1, SUBLANES) * nk + i2[0])
    n_tiles = len(cand)
    slot_code = [(nk - 1) - (c * SUBLANES + sub) for c in range(n_tiles)]
    coded = [_with_code(v, sc) for v, sc in zip(cand, slot_code)]
    coded += [jnp.full((SUBLANES, tt), FAR_BELOW, F32)] * (n - n_tiles)
    top = [_split_code(a) for a in _top16_columns(jnp.concatenate(coded, axis=0))]
    call = jnp.concatenate(cexp, axis=0)
    slot = (nk - 1) - lax.broadcasted_iota(jnp.int32, call.shape, 0)
    ex = [jnp.exp(v - top[0][0]) for v, _ in top]
    tot = ex[0]
    for k in range(1, n):
        tot = tot + ex[k]
    krow = lax.broadcasted_iota(jnp.int32, (n, tt), 0)
    e_tile = jnp.zeros((n, tt), F32)
    g_tile = jnp.zeros((n, tt), F32)
    for k in range(n):
        hit = slot == jnp.concatenate([top[k][1]] * n_tiles, axis=0)
        e_k = jnp.sum(jnp.where(hit, call, 0), axis=0, keepdims=True)
        e_tile = jnp.where(krow == k, e_k.astype(F32), e_tile)
        g_tile = jnp.where(krow == k, (ex[k] / tot)[0:1, :], g_tile)
    h = pl.program_id(1)
    r0 = pl.multiple_of(h * n, n)
    e_scr[pl.ds(r0, n), :] = e_tile
    g_scr[pl.ds(r0, n), :] = g_tile

    @pl.when(h == pl.num_programs(1) - 1)
    def _():
        e_ref[...] = e_scr[...].T.astype(jnp.int32)
        g_ref[...] = g_scr[...].T


def _peer_route(qp, keys):
    t = qp.shape[0]
    tt = min(PEER_TT, t)
    ne = PEER_HEADS * PEER_TOPK
    return pl.pallas_call(
        _peer_route_kernel,
        grid=(t // tt, PEER_HEADS),
        in_specs=[pl.BlockSpec((tt, 2 * PEER_KDIM), lambda i, h: (i, h)),
                  pl.BlockSpec((1, 2, PEER_NKEYS, PEER_KDIM), lambda i, h: (h, 0, 0, 0))],
        out_specs=[pl.BlockSpec((tt, ne), lambda i, h: (i, 0)),
                   pl.BlockSpec((tt, ne), lambda i, h: (i, 0))],
        out_shape=[jax.ShapeDtypeStruct((t, ne), jnp.int32),
                   jax.ShapeDtypeStruct((t, ne), F32)],
        scratch_shapes=[pltpu.VMEM((ne, tt), F32), pltpu.VMEM((ne, tt), F32)],
        compiler_params=_cp("parallel", "arbitrary"),
        name="peer_route",
    )(qp, keys)


def _pack_tables_kernel(u_ref, v_ref, o_ref):
    def pack(x):
        lo = lax.bitcast_convert_type(x[:, 0:HALF_D].astype(BF16).astype(F32), jnp.int32)
        hi = lax.bitcast_convert_type(x[:, HALF_D:2 * HALF_D].astype(BF16).astype(F32), jnp.int32)
        return lax.shift_right_logical(lo, 16) | (hi & jnp.int32(-65536))

    o_ref[:, 0:HALF_D] = pack(u_ref[...])
    o_ref[:, HALF_D:2 * HALF_D] = pack(v_ref[...])


def _pack_tables(u_tabs, v_tabs, layer, tr=512):
    _, e, d = u_tabs.shape
    assert d == 2 * HALF_D
    spec_in = pl.BlockSpec((None, tr, d), lambda i: (layer, i, 0))
    spec = pl.BlockSpec((tr, d), lambda i: (i, 0))
    return pl.pallas_call(
        _pack_tables_kernel,
        grid=(e // tr,),
        in_specs=[spec_in, spec_in],
        out_specs=spec,
        out_shape=jax.ShapeDtypeStruct((e, d), jnp.int32),
        compiler_params=_cp("parallel"),
        name="peer_pack",
    )(u_tabs, v_tabs)


def _unpack_rows(wd):
    lo = lax.bitcast_convert_type(lax.shift_left(wd, 16), F32)
    hi = lax.bitcast_convert_type(lax.bitwise_and(wd, jnp.int32(-65536)), F32)
    return lo, hi


SC_WINDOW = 32


def _sc_gather(table, idx):
    from jax.experimental.pallas import tpu_sc as plsc
    n = idx.shape[0]
    width = table.shape[1]
    mesh = plsc.VectorSubcoreMesh(core_axis_name="core", subcore_axis_name="subcore")

    @functools.partial(pl.kernel, out_type=jax.ShapeDtypeStruct((n, width), table.dtype), mesh=mesh)
    def gather(tab_hbm, idx_hbm, out_hbm):
        def body(idx_vmem, out_vmem):
            pltpu.sync_copy(tab_hbm.at[idx_vmem.at[0, pl.ds(0, SC_WINDOW)]], out_vmem)

        pltpu.emit_pipeline(
            body,
            grid=(n // SC_WINDOW,),
            in_specs=[pl.BlockSpec((1, LANES), lambda i: (0, i))],
            out_specs=[pl.BlockSpec((SC_WINDOW, width), lambda i: (i, 0))],
            core_axis_name=("core", "subcore"),
            dimension_semantics=(pltpu.PARALLEL,),
            trace_scopes=False,
        )(idx_hbm, out_hbm)

    idx_pad = jnp.pad(idx.reshape(n // SC_WINDOW, SC_WINDOW), ((0, 0), (0, LANES - SC_WINDOW)))
    return gather(table, idx_pad.reshape(1, (n // SC_WINDOW) * LANES))


def _peer_combine_kernel(x_ref, g2_ref, rows_ref, gate_ref, o_ref):
    ne = PEER_HEADS * PEER_TOPK
    x = x_ref[...]
    ct = x.shape[0]
    xn = x * lax.rsqrt(jnp.mean(x * x, axis=-1, keepdims=True) + NORM_EPS) * g2_ref[...]
    gate_t = jnp.concatenate([gate_ref[...]] * (ne // ct), axis=0).T
    for j in range(ct):
        u_lo, u_hi = _unpack_rows(rows_ref[j * ne:(j + 1) * ne, 0:HALF_D])
        xr = xn[j:j + 1, :]
        h = jnp.sum(u_lo * xr[:, 0:HALF_D] + u_hi * xr[:, HALF_D:2 * HALF_D], axis=1, keepdims=True)
        act = 0.5 * h * (1.0 + lax.erf(h * (2.0 ** -0.5)))
        wgt = gate_t[:, j:j + 1] * act
        v_lo, v_hi = _unpack_rows(rows_ref[j * ne:(j + 1) * ne, HALF_D:2 * HALF_D])
        o_ref[j:j + 1, 0:HALF_D] = x[j:j + 1, 0:HALF_D] + jnp.sum(wgt * v_lo, axis=0, keepdims=True)
        o_ref[j:j + 1, HALF_D:2 * HALF_D] = x[j:j + 1, HALF_D:2 * HALF_D] + jnp.sum(wgt * v_hi, axis=0, keepdims=True)


def _peer_combine(x, g2, rows, gates, first_token):
    t, d = x.shape
    ne = PEER_HEADS * PEER_TOPK
    ct = PEER_CT
    steps = rows.shape[0] // (ct * ne)
    off = first_token // ct
    return pl.pallas_call(
        _peer_combine_kernel,
        grid=(steps,),
        in_specs=[pl.BlockSpec((ct, d), lambda i: (off + i, 0)),
                  pl.BlockSpec((1, d), lambda i: (0, 0)),
                  pl.BlockSpec((ct * ne, d), lambda i: (i, 0)),
                  pl.BlockSpec((ct, ne), lambda i: (off + i, 0))],
        out_specs=pl.BlockSpec((ct, d), lambda i: (off + i, 0)),
        out_shape=jax.ShapeDtypeStruct((t, d), F32),
        input_output_aliases={0: 0},
        compiler_params=_cp("parallel"),
        name="peer_combine",
    )(x, g2.reshape(1, d), rows, gates)


PEER_TOKENS_PER_GATHER = 2048


def _peer_route_stage(x, g2, wq_b, keys_b):
    t = x.shape[0]
    ne = PEER_HEADS * PEER_TOPK
    qp = _norm_matmul(x, g2, wq_b, out_dtype=BF16)
    e_tok, g_tok = _peer_route(qp, keys_b)
    return e_tok.reshape(t * ne), g_tok


def _peer_gather_stage(table, idx, t, gather_fn):
    ne = PEER_HEADS * PEER_TOPK
    tc = min(PEER_TOKENS_PER_GATHER, t)
    return [gather_fn(table, idx[c * tc * ne:(c + 1) * tc * ne]) for c in range(t // tc)]


def _peer_combine_stage(x, g2, rows_list, gates):
    tc = x.shape[0] // len(rows_list)
    for c, rows in enumerate(rows_list):
        x = _peer_combine(x, g2, rows, gates, c * tc)
    return x


def _peer(x, g2, wq, keys, u_tabs, v_tabs, layer, gather_fn):
    idx, gates = _peer_route_stage(x, g2, wq.astype(BF16), keys.astype(BF16))
    table = _pack_tables(u_tabs, v_tabs, layer)
    rows_list = _peer_gather_stage(table, idx, x.shape[0], gather_fn)
    return _peer_combine_stage(x, g2, rows_list, gates)


_IN_WIDTHS = (256, 256, 256, 256, 256, 256, 256, 4, 4, 256, 256, 128, 128, 128, 128, 128, 128, 12,
              256, 256, 256, 256)


def _dup_groups(wcols):
    g0, g1 = wcols[:, :HEAD_DIM], wcols[:, HEAD_DIM:]
    return jnp.concatenate([g0, g0, g1, g1], axis=1)


def _layout_w_in(w_in):
    offs = np.cumsum((0,) + _IN_WIDTHS)
    cols = [w_in[:, offs[i]:offs[i + 1]] for i in range(len(_IN_WIDTHS))]
    (hq, hf, hi, hg, mq, mk, mv, mi, mf, mo, nq, nkc, nvc, nks, nvs, nkw, nvw, ng, rq, rk, rv, rg) = cols
    d = w_in.shape[0]
    pad = lambda c, n: jnp.concatenate([c, jnp.zeros((d, n - c.shape[1]), w_in.dtype)], axis=1)
    main = jnp.concatenate([hq, hf, hi, hg, mq, mk, mv, mo, rq, rk, rv, rg,
                            _dup_groups(nks), _dup_groups(nkw), nq, nvs, nvw,
                            pad(jnp.concatenate([mi, mf], axis=1), LANES), pad(ng, LANES)], axis=1)
    assert main.shape[1] == N_MAIN
    kcvc = jnp.concatenate([nkc, nvc], axis=1)
    return main.astype(BF16), kcvc.astype(BF16)


def kernel(x, norm1_g, w_in, hgrn_lb, hgrn_onorm_g, mlstm_conv_w, mlstm_conv_b, mlstm_gate_b, mlstm_onorm_g, nsa_qnorm_g, nsa_knorm_g, nsa_cmp_pe, nsa_cmp_w, ret_onorm_g, w_up, w_gate, w_out, norm2_g, peer_wq, peer_keys, peer_u, peer_v):
    bsz, seq, d = x.shape
    t = bsz * seq
    depth = w_in.shape[0]
    cos_t, sin_t = _rope_lane_tables(seq)
    cos4, sin4 = jnp.tile(cos_t, (1, 2)), jnp.tile(sin_t, (1, 2))
    lb_cum = jnp.cumsum(jax.nn.softmax(hgrn_lb.astype(F32), axis=0), axis=0)
    lb_all = lb_cum - lb_cum[0:1]
    weights = []
    for l in range(depth):
        w_main, w_kcvc = _layout_w_in(w_in[l])
        weights.append(dict(
            main=w_main, kcvc=w_kcvc, gate=w_gate[l].astype(BF16), up=w_up[l].astype(BF16),
            out=w_out[l].astype(BF16), lb=_hgrn_lb_rows(lb_all[l]), wq=peer_wq[l].astype(BF16),
            keys=peer_keys[l].astype(BF16), table=_pack_tables(peer_u, peer_v, l)))

    def mixer_steps(xh, l, nb):
        wl = weights[l]
        st = {}

        def proj(dep):
            st["proj"] = _norm_matmul(xh, _after(norm1_g[l], dep), wl["main"])
            return st["proj"]

        def gates(dep):
            st["gates"] = _norm_matmul(xh, _after(norm1_g[l], dep), wl["gate"], act="sigmoid", out_dtype=BF16)
            return st["gates"]

        def hgrn(dep):
            st["oh"] = _hgrn(st["proj"], wl["lb"], _after(hgrn_onorm_g[l], dep), nb, seq)
            return st["oh"]

        def mlstm(dep):
            st["om"] = _mlstm(st["proj"], mlstm_conv_w[l], mlstm_conv_b[l], mlstm_gate_b[l],
                              _after(mlstm_onorm_g[l], dep), nb, seq)
            return st["om"]

        def ret(dep):
            st["or"] = _ret(st["proj"], cos4, sin4, _after(ret_onorm_g[l], dep), nb, seq)
            return st["or"]

        def nsa_front(dep):
            kcvc = _norm_matmul(xh, _after(norm1_g[l], dep), wl["kcvc"])
            qn, qr, ks, kw, vst, vwt = _nsa_prep(st["proj"], cos4, sin4, nsa_qnorm_g[l], nsa_knorm_g[l], nb, seq)
            o_cmp, sel = _nsa_cmp(kcvc, qn, nsa_cmp_pe[l], nsa_cmp_w[l], nsa_knorm_g[l][0], nb, seq)
            st["nsa"] = (qr, ks, kw, vst, vwt, sel, o_cmp)
            return o_cmp

        def nsa_attn(dep):
            del dep
            st["on"] = _nsa_attn(st["proj"], *st["nsa"], nb, seq)
            return st["on"]

        def merge(dep):
            del dep
            st["xm"] = _merge(xh, st["gates"], (st["oh"], st["om"], st["on"], st["or"]), wl["up"], wl["out"])
            return st["xm"]

        def route(dep):
            st["idx"], st["pgates"] = _peer_route_stage(st["xm"], _after(norm2_g[l], dep), wl["wq"], wl["keys"])
            return st["pgates"]

        return [proj, gates, hgrn, mlstm, ret, nsa_front, nsa_attn, merge, route], st

    combine_slots = (0, 8, 8, 8, 8, 8, 8, 8)

    def combine_steps(l, xm, rows_list, pgates):
        box = {"x": xm}
        tc = xm.shape[0] // len(rows_list)

        def make(c):
            def step(dep):
                box["x"] = _peer_combine(box["x"], _after(norm2_g[l], dep), rows_list[c], pgates, c * tc)
                return box["x"]
            return step

        return [make(c) for c in range(len(rows_list))], box

    n_groups = next(n for n in (4, 2, 1) if bsz % n == 0)
    nb = bsz // n_groups
    xs = [x[g * nb:(g + 1) * nb].reshape(nb * seq, d) for g in range(n_groups)]
    dep = None
    pending = None
    for l in range(depth):
        for g in range(n_groups):
            if pending is not None and pending[0] == g:
                for cstep in pending[1]:
                    dep = cstep(dep)
                xs[g] = pending[2]["x"]
                pending = None
            msteps, st = mixer_steps(xs[g], l, nb)
            csteps = pending[1] if pending is not None else []
            ci = 0
            for si, mstep in enumerate(msteps):
                dep = mstep(dep)
                while ci < len(csteps) and (ci >= len(combine_slots) or combine_slots[ci] <= si):
                    dep = csteps[ci](dep)
                    ci += 1
            for cstep in csteps[ci:]:
                dep = cstep(dep)
            if pending is not None:
                xs[pending[0]] = pending[2]["x"]
            rows_list = _peer_gather_stage(weights[l]["table"], st["idx"], nb * seq, _sc_gather)
            csteps, box = combine_steps(l, st["xm"], rows_list, st["pgates"])
            pending = (g, csteps, box)
    for cstep in pending[1]:
        dep = cstep(dep)
    xs[pending[0]] = pending[2]["x"]
    return jnp.concatenate(xs, axis=0).reshape(bsz, seq, d)
```

```python
import functools
import math

import numpy as np
import jax
import jax.numpy as jnp
from jax import lax
from jax.experimental import pallas as pl
from jax.experimental.pallas import tpu as pltpu

F32 = jnp.float32
BF16 = jnp.bfloat16

HEAD_DIM = 64
N_HEADS = 4
MIX_WIDTH = N_HEADS * HEAD_DIM
CHUNK = 64
NORM_EPS = 1e-6
NEG_INF = -1e30
ROPE_THETA = 10000.0
CONV_W = 4
NSA_GROUPS = 2
CMP_LEN = 32
CMP_STRIDE = 16
SEL_BLOCK = 64
SEL_TOPK = 16
WINDOW = 512
FORCE_BONUS = 1e3
PEER_HEADS = 8
PEER_NKEYS = 128
PEER_TOPK = 16
PEER_KDIM = 128

LANES = 128
VMEM_LIMIT = 48 * 1024 * 1024

OFF_H, OFF_M, OFF_R, OFF_KD, OFF_NQ, OFF_V, OFF_MG, OFF_NG = 0, 1024, 2048, 3072, 3584, 3840, 4096, 4224
N_MAIN = 4352


def _cp(*sem):
    return pltpu.CompilerParams(dimension_semantics=sem, vmem_limit_bytes=VMEM_LIMIT)


def _dot(a, b):
    return jnp.dot(a, b, preferred_element_type=F32)


def _dot_nt(a, b):
    return lax.dot_general(a, b, (((1,), (1,)), ((), ())), preferred_element_type=F32)


def _dot_tn(a, b):
    return lax.dot_general(a, b, (((0,), (0,)), ((), ())), preferred_element_type=F32)


def _split3(x):
    hi = x.astype(BF16)
    r1 = x - hi.astype(F32)
    mid = r1.astype(BF16)
    lo = (r1 - mid.astype(F32)).astype(BF16)
    return hi, mid, lo


def _dot01_l(m01, x):
    hi, mid, lo = _split3(x)
    return _dot(m01, hi) + _dot(m01, mid) + _dot(m01, lo)


def _dot01_r(x, m01):
    hi, mid, lo = _split3(x)
    return _dot(hi, m01) + _dot(mid, m01) + _dot(lo, m01)


def _head_of_lane(shape, axis):
    return lax.broadcasted_iota(jnp.int32, shape, axis) // HEAD_DIM


def _block_ones(n, dtype=BF16):
    r = lax.broadcasted_iota(jnp.int32, (n, n), 0) // HEAD_DIM
    c = lax.broadcasted_iota(jnp.int32, (n, n), 1) // HEAD_DIM
    return (r == c).astype(dtype)


def _group_sum(x, ones_bd):
    hi = x.astype(BF16)
    lo = (x - hi.astype(F32)).astype(BF16)
    return _dot(hi, ones_bd) + _dot(lo, ones_bd)


def _head_rms(x, gain, ones_bd):
    ms = _group_sum(x * x, ones_bd) * (1.0 / HEAD_DIM)
    return x * lax.rsqrt(ms + NORM_EPS) * gain


def _sigmoid(x):
    return 1.0 / (1.0 + jnp.exp(-x))


def _silu(x):
    return x * _sigmoid(x)


def _log_sigmoid(x):
    return jnp.minimum(x, 0.0) - jnp.log(1.0 + jnp.exp(-jnp.abs(x)))


def _stack_heads(x, n_heads=N_HEADS):
    hl = _head_of_lane(x.shape, 1)
    return jnp.concatenate([jnp.where(hl == h, x, jnp.zeros_like(x)) for h in range(n_heads)], axis=0)


def _unstack_heads(r, c, n_heads=N_HEADS):
    hl = _head_of_lane((c, r.shape[1]), 1)
    out = jnp.zeros((c, r.shape[1]), F32)
    for h in range(n_heads):
        out = jnp.where(hl == h, r[h * c:(h + 1) * c, :], out)
    return out


def _rope(x, cos_t, sin_t):
    n = x.shape[1]
    first = (lax.broadcasted_iota(jnp.int32, x.shape, 1) % HEAD_DIM) < (HEAD_DIM // 2)
    partner = jnp.where(first, pltpu.roll(x, n - HEAD_DIM // 2, 1), pltpu.roll(x, HEAD_DIM // 2, 1))
    return x * cos_t + partner * sin_t


def _after_kernel(a_ref, dep_ref, o_ref):
    del dep_ref
    o_ref[...] = a_ref[...]


def _after(a, dep):
    if dep is None:
        return a
    a2 = a.reshape(1, a.size)
    out = pl.pallas_call(
        _after_kernel,
        in_specs=[pl.BlockSpec(a2.shape, lambda: (0, 0)), pl.BlockSpec(memory_space=pl.ANY)],
        out_specs=pl.BlockSpec(a2.shape, lambda: (0, 0)),
        out_shape=jax.ShapeDtypeStruct(a2.shape, a2.dtype),
        name="order_after",
    )(a2, dep)
    return out.reshape(a.shape)


def _norm_matmul_kernel(x_ref, g_ref, w_ref, o_ref, xn_ref, *, act):
    @pl.when(pl.program_id(1) == 0)
    def _():
        x = x_ref[...]
        ms = jnp.mean(x * x, axis=-1, keepdims=True)
        xn_ref[...] = (x * lax.rsqrt(ms + NORM_EPS) * g_ref[...]).astype(BF16)

    y = _dot(xn_ref[...], w_ref[...])
    if act == "sigmoid":
        y = _sigmoid(y)
    o_ref[...] = y.astype(o_ref.dtype)


def _norm_matmul(x, g, w, *, act=None, out_dtype=F32, tm=1024, tn=2176):
    t, d = x.shape
    w3 = w if w.ndim == 3 else w[None]
    n_per = w3.shape[2]
    tm = min(tm, t)
    tn = next(c for c in (tn, 2048, 1024, 512, 256, 128) if n_per % c == 0)
    per = n_per // tn
    n = w3.shape[0] * n_per
    assert t % tm == 0
    return pl.pallas_call(
        functools.partial(_norm_matmul_kernel, act=act),
        grid=(t // tm, n // tn),
        in_specs=[pl.BlockSpec((tm, d), lambda i, j: (i, 0)),
                  pl.BlockSpec((1, d), lambda i, j: (0, 0)),
                  pl.BlockSpec((None, d, tn), lambda i, j: (j // per, 0, j % per))],
        out_specs=pl.BlockSpec((tm, tn), lambda i, j: (i, j)),
        out_shape=jax.ShapeDtypeStruct((t, n), out_dtype),
        scratch_shapes=[pltpu.VMEM((tm, d), BF16)],
        compiler_params=_cp("parallel", "arbitrary"),
        name="norm_matmul",
    )(x, g.reshape(1, d), w3)


def _merge_kernel(x_ref, gate_ref, oh_ref, om_ref, on_ref, or_ref, wup_ref, wout_ref, o_ref):
    d = x_ref.shape[1]
    acc = None
    for m, r in enumerate((oh_ref, om_ref, on_ref, or_ref)):
        up = _dot(r[...].astype(BF16), wup_ref[m])
        term = gate_ref[:, m * d:(m + 1) * d].astype(F32) * up
        acc = term if acc is None else acc + term
    o_ref[...] = x_ref[...] + _dot(acc.astype(BF16), wout_ref[...])


def _merge(x, gates, outs, w_up, w_out, tm=512):
    t, d = x.shape
    tm = min(tm, t)
    mix = pl.BlockSpec((tm, MIX_WIDTH), lambda i: (i, 0))
    return pl.pallas_call(
        _merge_kernel,
        grid=(t // tm,),
        in_specs=[pl.BlockSpec((tm, d), lambda i: (i, 0)),
                  pl.BlockSpec((tm, 4 * d), lambda i: (i, 0)),
                  mix, mix, mix, mix,
                  pl.BlockSpec((4, MIX_WIDTH, d), lambda i: (0, 0, 0)),
                  pl.BlockSpec((d, d), lambda i: (0, 0))],
        out_specs=pl.BlockSpec((tm, d), lambda i: (i, 0)),
        out_shape=jax.ShapeDtypeStruct((t, d), F32),
        compiler_params=_cp("parallel"),
        name="merge",
    )(x, gates, *outs, w_up, w_out)


REC_BLOCK = 256


def _chunk_consts():
    t = lax.broadcasted_iota(jnp.int32, (CHUNK, CHUNK), 0)
    s = lax.broadcasted_iota(jnp.int32, (CHUNK, CHUNK), 1)
    return t, s


def _hgrn_levels():
    t = np.arange(CHUNK)
    rows = []
    masks = []
    h = CHUNK // 2
    while h >= 1:
        ref = (t // (2 * h)) * (2 * h) + h
        p = np.zeros((CHUNK, CHUNK), np.float32)
        p[t, np.minimum(ref, CHUNK - 1)] = 1.0
        rows.append(p)
        same = (t[:, None] // (2 * h)) == (t[None, :] // (2 * h))
        m = same & ((t[:, None] // h) % 2 == 1) & ((t[None, :] // h) % 2 == 0)
        masks.append(m.astype(np.float32))
        h //= 2
    masks.append(np.eye(CHUNK, dtype=np.float32))
    return np.concatenate(rows, 0), np.stack(masks, 0)


def _hgrn_kernel(p_ref, lb_ref, g_ref, psel_ref, lmask_ref, o_ref, st_ref):
    @pl.when(pl.program_id(1) == 0)
    def _():
        st_ref[...] = jnp.zeros_like(st_ref)

    c = CHUNK
    w = MIX_WIDTH
    ones_bd = _block_ones(w)
    bd_mask = _block_ones(w, F32)
    tri = (lax.broadcasted_iota(jnp.int32, (c, c), 0) >= lax.broadcasted_iota(jnp.int32, (c, c), 1)).astype(BF16)
    psel = psel_ref[...]
    n_lv = lmask_ref.shape[0]
    log_lb, log_1mlb, one_mlb = lb_ref[0:1, :], lb_ref[1:2, :], lb_ref[2:3, :]
    gain = g_ref[...]

    def chunk(ci, carry):
        r0 = pl.multiple_of(ci * c, c)
        q = _silu(p_ref[pl.ds(r0, c), 0:w])
        fl = p_ref[pl.ds(r0, c), w:2 * w]
        v = p_ref[pl.ds(r0, c), 2 * w:3 * w]
        gp = p_ref[pl.ds(r0, c), 3 * w:4 * w]
        a1 = jnp.broadcast_to(log_lb, fl.shape)
        a2 = log_1mlb + _log_sigmoid(fl)
        mx = jnp.maximum(a1, a2)
        log_f = mx + jnp.log(jnp.exp(a1 - mx) + jnp.exp(a2 - mx))
        k = one_mlb * _sigmoid(-fl)
        b = _dot01_l(tri, log_f)
        bref = _dot01_l(psel, b)
        vb = v.astype(BF16)
        a = jnp.zeros((N_HEADS * c, c), F32)
        for lv in range(n_lv):
            if lv < n_lv - 1:
                br = bref[lv * c:(lv + 1) * c, :]
                qs = q * jnp.exp(jnp.minimum(b - br, 0.0))
                ks = k * jnp.exp(jnp.minimum(br - b, 0.0))
            else:
                qs, ks = q, k
            s_lv = _dot_nt(_stack_heads(qs).astype(BF16), ks.astype(BF16))
            a = a + jnp.concatenate([lmask_ref[lv]] * N_HEADS, axis=0) * s_lv
        o = _unstack_heads(_dot(a.astype(BF16), vb), c)
        st = st_ref[...]
        o = o + _dot_nt((q * jnp.exp(b)).astype(BF16), st.astype(BF16))
        b_last = b[c - 1:c, :]
        kb = k * jnp.exp(b_last - b)
        st_ref[...] = st * jnp.exp(b_last) + bd_mask * _dot_tn(vb, kb.astype(BF16))
        y = _head_rms(o, gain, ones_bd) * _silu(gp)
        o_ref[pl.ds(r0, c), :] = y
        return carry

    lax.fori_loop(0, p_ref.shape[0] // c, chunk, 0)


def _hgrn(proj, lb_rows, gain, bsz, seq):
    psel, lmask = _hgrn_levels()
    tb = min(REC_BLOCK, seq)
    nb = seq // tb
    return pl.pallas_call(
        _hgrn_kernel,
        grid=(bsz, nb),
        in_specs=[pl.BlockSpec((tb, 4 * MIX_WIDTH), lambda b, i: (b * nb + i, OFF_H // (4 * MIX_WIDTH))),
                  pl.BlockSpec((8, MIX_WIDTH), lambda b, i: (0, 0)),
                  pl.BlockSpec((1, MIX_WIDTH), lambda b, i: (0, 0)),
                  pl.BlockSpec(psel.shape, lambda b, i: (0, 0)),
                  pl.BlockSpec(lmask.shape, lambda b, i: (0, 0, 0))],
        out_specs=pl.BlockSpec((tb, MIX_WIDTH), lambda b, i: (b * nb + i, 0)),
        out_shape=jax.ShapeDtypeStruct((bsz * seq, MIX_WIDTH), F32),
        scratch_shapes=[pltpu.VMEM((MIX_WIDTH, MIX_WIDTH), F32)],
        compiler_params=_cp("parallel", "arbitrary"),
        name="hgrn2",
    )(proj, lb_rows, gain.reshape(1, MIX_WIDTH), jnp.asarray(psel, BF16), jnp.asarray(lmask, F32))


def _ret_kernel(p_ref, cos_ref, sin_ref, dec_ref, decin_ref, g_ref, o_ref, st_ref):
    @pl.when(pl.program_id(1) == 0)
    def _():
        st_ref[...] = jnp.zeros_like(st_ref)

    c = CHUNK
    w = MIX_WIDTH
    ones_bd = _block_ones(w)
    bd_mask = _block_ones(w, F32)
    gain = g_ref[...]
    dec_q = dec_ref[0:c, :]
    dec_k = dec_ref[c:2 * c, :]
    dec_state = dec_ref[2 * c:2 * c + 1, :]
    dec_in = decin_ref[...]

    def chunk(ci, carry):
        r0 = pl.multiple_of(ci * c, c)
        cos_t = cos_ref[pl.ds(r0, c), :]
        sin_t = sin_ref[pl.ds(r0, c), :]
        q = _rope(p_ref[pl.ds(r0, c), 0:w], cos_t, sin_t)
        k = _rope(p_ref[pl.ds(r0, c), w:2 * w], cos_t, sin_t) * (HEAD_DIM ** -0.5)
        v = p_ref[pl.ds(r0, c), 2 * w:3 * w]
        gp = p_ref[pl.ds(r0, c), 3 * w:4 * w]
        vb = v.astype(BF16)
        a = _dot_nt(_stack_heads(q).astype(BF16), k.astype(BF16)) * dec_in
        o = _unstack_heads(_dot(a.astype(BF16), vb), c)
        st = st_ref[...]
        o = o + _dot_nt(q.astype(BF16), st.astype(BF16)) * dec_q
        st_ref[...] = st * dec_state + bd_mask * _dot_tn(vb, (k * dec_k).astype(BF16))
        o_ref[pl.ds(r0, c), :] = _head_rms(o, gain, ones_bd) * _silu(gp)
        return carry

    lax.fori_loop(0, p_ref.shape[0] // c, chunk, 0)


def _ret_consts():
    log_gamma = np.log1p(-np.exp2(-5.0 - np.arange(N_HEADS, dtype=np.float64)))
    t = np.arange(CHUNK, dtype=np.float64)
    lane_h = np.arange(MIX_WIDTH) // HEAD_DIM
    dec_q = np.exp(log_gamma[lane_h][None, :] * (t[:, None] + 1.0))
    dec_k = np.exp(log_gamma[lane_h][None, :] * (CHUNK - 1.0 - t[:, None]))
    dec_state = np.exp(log_gamma[lane_h] * CHUNK)[None, :]
    dec = np.concatenate([dec_q, dec_k, np.broadcast_to(dec_state, (8, MIX_WIDTH))], 0)
    diff = t[:, None] - t[None, :]
    dec_in = np.concatenate([np.where(diff >= 0, np.exp(log_gamma[h] * diff), 0.0) for h in range(N_HEADS)], 0)
    return dec.astype(np.float32), dec_in.astype(np.float32)


def _ret(proj, cos4, sin4, gain, bsz, seq):
    dec, dec_in = _ret_consts()
    tb = min(REC_BLOCK, seq)
    nb = seq // tb
    return pl.pallas_call(
        _ret_kernel,
        grid=(bsz, nb),
        in_specs=[pl.BlockSpec((tb, 4 * MIX_WIDTH), lambda b, i: (b * nb + i, OFF_R // (4 * MIX_WIDTH))),
                  pl.BlockSpec((tb, MIX_WIDTH), lambda b, i: (i, 0)),
                  pl.BlockSpec((tb, MIX_WIDTH), lambda b, i: (i, 0)),
                  pl.BlockSpec(dec.shape, lambda b, i: (0, 0)),
                  pl.BlockSpec(dec_in.shape, lambda b, i: (0, 0)),
                  pl.BlockSpec((1, MIX_WIDTH), lambda b, i: (0, 0))],
        out_specs=pl.BlockSpec((tb, MIX_WIDTH), lambda b, i: (b * nb + i, 0)),
        out_shape=jax.ShapeDtypeStruct((bsz * seq, MIX_WIDTH), F32),
        scratch_shapes=[pltpu.VMEM((MIX_WIDTH, MIX_WIDTH), F32)],
        compiler_params=_cp("parallel", "arbitrary"),
        name="retention",
    )(proj, cos4, sin4, jnp.asarray(dec), jnp.asarray(dec_in), gain.reshape(1, MIX_WIDTH))


def _hgrn_lb_rows(lb):
    lb = lb.astype(F32)
    rows = jnp.stack([jnp.log(lb), jnp.log1p(-lb), 1.0 - lb], 0)
    return jnp.concatenate([rows, jnp.zeros((5, lb.shape[0]), F32)], 0)


def _rope_lane_tables(seq):
    inv = 1.0 / (ROPE_THETA ** (jnp.arange(0, HEAD_DIM, 2, dtype=F32) / HEAD_DIM))
    ang = jnp.arange(seq, dtype=F32)[:, None] * inv[None, :]
    cos, sin = jnp.cos(ang), jnp.sin(ang)
    cos_t = jnp.tile(cos, (1, LANES // (HEAD_DIM // 2)))
    sin_t = jnp.tile(jnp.concatenate([-sin, sin], axis=1), (1, LANES // HEAD_DIM))
    return cos_t, sin_t


def _expand_heads(cols, shape):
    hl = _head_of_lane(shape, 1)
    out = jnp.broadcast_to(cols[-1], shape)
    for h in range(len(cols) - 2, -1, -1):
        out = jnp.where(hl == h, jnp.broadcast_to(cols[h], shape), out)
    return out


def _mlstm_kernel(p_ref, gcol_ref, grow_ref, cw_ref, cb_ref, gbr_ref, gbc_ref, g_ref, o_ref,
                  ct_ref, n_ref, m_ref, hist_ref, cbuf_ref, qk_ref):
    c = CHUNK
    w = MIX_WIDTH
    tb = p_ref.shape[0]

    @pl.when(pl.program_id(1) == 0)
    def _():
        ct_ref[...] = jnp.zeros_like(ct_ref)
        n_ref[...] = jnp.zeros_like(n_ref)
        m_ref[...] = jnp.zeros_like(m_ref)
        hist_ref[...] = jnp.zeros_like(hist_ref)

    cbuf_ref[0:8, :] = hist_ref[...]
    cbuf_ref[8:, :] = p_ref[:, 0:2 * w]
    hist_ref[...] = p_ref[tb - 8:tb, 0:2 * w]
    acc = jnp.broadcast_to(cb_ref[...], (tb, 2 * w))
    for j in range(CONV_W):
        acc = acc + cw_ref[j:j + 1, :] * cbuf_ref[pl.ds(8 - (CONV_W - 1) + j, tb), :]
    qk_ref[...] = _silu(acc)

    ones_bd = _block_ones(w)
    bd_mask = _block_ones(w, F32)
    ti = lax.broadcasted_iota(jnp.int32, (c, c), 0)
    si = lax.broadcasted_iota(jnp.int32, (c, c), 1)
    causal = ti >= si
    tri = causal.astype(BF16)
    tri_t = (ti <= si).astype(BF16)
    gain = g_ref[...]
    ones_ext = jnp.ones((c, LANES), BF16)

    def chunk(ci, carry):
        r0 = pl.multiple_of(ci * c, c)
        q = qk_ref[pl.ds(r0, c), 0:w]
        k = qk_ref[pl.ds(r0, c), w:2 * w] * (HEAD_DIM ** -0.5)
        v = p_ref[pl.ds(r0, c), 2 * w:3 * w]
        op = p_ref[pl.ds(r0, c), 3 * w:4 * w]
        gc = gcol_ref[pl.ds(r0, c), :] + gbr_ref[...]
        gr = grow_ref[ci] + gbc_ref[...]
        b_c = _dot01_l(tri, _log_sigmoid(gc))
        b_r = _dot01_r(_log_sigmoid(gr), tri_t)
        wd, s_inter, em, wk, decay = [], [], [], [], []
        for h in range(N_HEADS):
            bc = b_c[:, N_HEADS + h:N_HEADS + h + 1]
            lic = gc[:, h:h + 1]
            br = b_r[N_HEADS + h:N_HEADS + h + 1, :]
            lir = gr[h:h + 1, :]
            dmat = jnp.where(causal, bc - br + lir, -jnp.inf)
            m_prev = m_ref[h:h + 1, 0:1]
            inter = bc + m_prev
            mrow = jnp.maximum(inter, jnp.max(dmat, axis=1, keepdims=True))
            wd.append(jnp.exp(dmat - mrow))
            s_inter.append(jnp.exp(inter - mrow))
            em.append(jnp.exp(-mrow))
            b_last = br[:, c - 1:c]
            m_new = jnp.maximum(b_last + m_prev, jnp.max(b_last - br + lir, axis=1, keepdims=True))
            wk.append(jnp.exp(b_last - bc + lic - m_new))
            decay.append(jnp.exp(b_last + m_prev - m_new))
            m_ref[h:h + 1, :] = jnp.broadcast_to(m_new, (1, LANES))
        s_inter_l = _expand_heads(s_inter, (c, w))
        em_l = _expand_heads(em, (c, w))
        wk_l = _expand_heads(wk, (c, w))
        decay_l = _expand_heads(decay, (1, w))
        qk = _dot_nt(_stack_heads(q).astype(BF16), k.astype(BF16))
        wmat = jnp.concatenate(wd, axis=0) * qk
        vb = v.astype(BF16)
        r = _dot(wmat.astype(BF16), jnp.concatenate([vb, ones_ext], axis=1))
        num_intra = _unstack_heads(r[:, 0:w], c)
        rs_l = _expand_heads([r[h * c:(h + 1) * c, w:w + 1] for h in range(N_HEADS)], (c, w))
        ct = ct_ref[...]
        nrow = n_ref[0:1, :]
        num = s_inter_l * _dot_nt(q.astype(BF16), ct.astype(BF16)) + num_intra
        den = s_inter_l * _group_sum(q * nrow, ones_bd) + rs_l
        hval = num / jnp.maximum(jnp.abs(den), em_l)
        kw = wk_l * k
        ct_ref[...] = ct * decay_l + bd_mask * _dot_tn(vb, kw.astype(BF16))
        n_ref[0:1, :] = nrow * decay_l + jnp.sum(kw, axis=0, keepdims=True)
        o_ref[pl.ds(r0, c), :] = _head_rms(hval, gain, ones_bd) * _sigmoid(op)
        return carry

    lax.fori_loop(0, tb // c, chunk, 0)


def _mlstm(proj, conv_w, conv_b, gate_b, gain, bsz, seq):
    t = bsz * seq
    w = MIX_WIDTH
    tb = min(REC_BLOCK, seq)
    nb = seq // tb
    ncb = tb // CHUNK
    grow = proj[:, OFF_MG:OFF_MG + 8].reshape(t // CHUNK, CHUNK, 8).transpose(0, 2, 1)
    gb_row = jnp.zeros((1, LANES), F32).at[0, 0:8].set(gate_b.astype(F32))
    gb_col = gate_b.astype(F32).reshape(8, 1)
    return pl.pallas_call(
        _mlstm_kernel,
        grid=(bsz, nb),
        in_specs=[pl.BlockSpec((tb, 4 * w), lambda b, i: (b * nb + i, OFF_M // (4 * w))),
                  pl.BlockSpec((tb, LANES), lambda b, i: (b * nb + i, OFF_MG // LANES)),
                  pl.BlockSpec((ncb, 8, CHUNK), lambda b, i: (b * nb + i, 0, 0)),
                  pl.BlockSpec((CONV_W, 2 * w), lambda b, i: (0, 0)),
                  pl.BlockSpec((1, 2 * w), lambda b, i: (0, 0)),
                  pl.BlockSpec((1, LANES), lambda b, i: (0, 0)),
                  pl.BlockSpec((8, 1), lambda b, i: (0, 0)),
                  pl.BlockSpec((1, w), lambda b, i: (0, 0))],
        out_specs=pl.BlockSpec((tb, w), lambda b, i: (b * nb + i, 0)),
        out_shape=jax.ShapeDtypeStruct((t, w), F32),
        scratch_shapes=[pltpu.VMEM((w, w), F32), pltpu.VMEM((8, w), F32), pltpu.VMEM((8, LANES), F32),
                        pltpu.VMEM((8, 2 * w), F32), pltpu.VMEM((tb + 8, 2 * w), F32),
                        pltpu.VMEM((tb, 2 * w), F32)],
        compiler_params=_cp("parallel", "arbitrary"),
        name="mlstm",
    )(proj, proj, grow, conv_w.astype(F32), conv_b.astype(F32).reshape(1, 2 * w), gb_row, gb_col,
      gain.reshape(1, w))


NSA_TQ = 128
NSA_KC = 512
GW = 2 * HEAD_DIM


def _nsa_prep_kernel(pq_ref, pk_ref, pv_ref, cos_ref, sin_ref, qg_ref, kg_ref,
                     qn_ref, qr_ref, ks_ref, kw_ref, vst_ref, vwt_ref):
    w = MIX_WIDTH
    for src, dst in ((pv_ref[:, 0:GW], vst_ref), (pv_ref[:, GW:2 * GW], vwt_ref)):
        vt = src.T
        tk = dst.shape[4]
        for g in range(NSA_GROUPS):
            rows = vt[g * HEAD_DIM:(g + 1) * HEAD_DIM, :]
            dup = jnp.concatenate([rows, rows], axis=0).astype(BF16)
            for j in range(dst.shape[2]):
                dst[0, g, j] = dup[:, j * tk:(j + 1) * tk]
    ones_bd = _block_ones(w)
    cos_t, sin_t = cos_ref[...], sin_ref[...]
    scale = HEAD_DIM ** -0.5
    qh = _head_rms(pq_ref[...], qg_ref[...], ones_bd)
    qn_ref[...] = (qh * scale).astype(BF16)
    qr_ref[...] = (_rope(qh, cos_t, sin_t) * scale).astype(BF16)
    ks_ref[...] = _rope(_head_rms(pk_ref[:, 0:w], kg_ref[1:2, :], ones_bd), cos_t, sin_t).astype(BF16)
    kw_ref[...] = _rope(_head_rms(pk_ref[:, w:2 * w], kg_ref[2:3, :], ones_bd), cos_t, sin_t).astype(BF16)


def _nsa_prep(proj, cos4, sin4, qnorm_g, knorm_g, bsz, seq):
    t = bsz * seq
    w = MIX_WIDTH
    tm = min(NSA_KC, seq)
    tq = min(NSA_TQ, seq)
    ns = seq // tm
    qg = jnp.tile(qnorm_g.astype(F32), w // HEAD_DIM).reshape(1, w)
    kg = jnp.concatenate([jnp.tile(knorm_g.astype(F32), (1, w // HEAD_DIM)), jnp.zeros((5, w), F32)], axis=0)
    out = jax.ShapeDtypeStruct((t, w), BF16)
    row = pl.BlockSpec((tm, w), lambda i: (i, 0))
    return pl.pallas_call(
        _nsa_prep_kernel,
        grid=(t // tm,),
        in_specs=[pl.BlockSpec((tm, w), lambda i: (i, OFF_NQ // w)),
                  pl.BlockSpec((tm, 2 * w), lambda i: (i, OFF_KD // (2 * w))),
                  pl.BlockSpec((tm, 2 * GW), lambda i: (i, OFF_V // (2 * GW))),
                  pl.BlockSpec((tm, w), lambda i: (i % ns, 0)),
                  pl.BlockSpec((tm, w), lambda i: (i % ns, 0)),
                  pl.BlockSpec((1, w), lambda i: (0, 0)),
                  pl.BlockSpec((8, w), lambda i: (0, 0))],
        out_specs=[row, row, row, row,
                   pl.BlockSpec((1, NSA_GROUPS, 1, GW, tm), lambda i: (i // ns, 0, i % ns, 0, 0)),
                   pl.BlockSpec((1, NSA_GROUPS, tm // tq, GW, tq), lambda i: (i // ns, 0, i % ns, 0, 0))],
        out_shape=[out, out, out, out,
                   jax.ShapeDtypeStruct((bsz, NSA_GROUPS, seq // tm, GW, tm), BF16),
                   jax.ShapeDtypeStruct((bsz, NSA_GROUPS, seq // tq, GW, tq), BF16)],
        compiler_params=_cp("parallel"),
        name="nsa_prep",
    )(proj, proj, proj, cos4, sin4, qg, kg)


def _nsa_cmp_kernel(xr_ref, pe_ref, w0_ref, w1_ref, kg_ref, ovt_ref, qn_ref, ocmp_ref, sel_ref,
                    kc_ref, vc_ref, v_ref, *, n_top):
    tq = qn_ref.shape[0]
    nr = xr_ref.shape[0]
    nsel = sel_ref.shape[2]
    w = MIX_WIDTH

    @pl.when(pl.program_id(1) == 0)
    def _():
        xr = xr_ref[...]
        y0 = _dot((xr + pe_ref[0]).astype(BF16), w0_ref[...])
        y1 = _dot((xr + pe_ref[1]).astype(BF16), w1_ref[...])
        kv = y0 + pltpu.roll(y1, nr - 1, 0)
        kc_ref[...] = _head_rms(kv[:, 0:w], kg_ref[...], _block_ones(w)).astype(BF16)
        vc_ref[...] = kv[:, w:2 * w].astype(BF16)

    pos0 = pl.program_id(1) * tq
    hl = _head_of_lane((tq, GW), 1)
    pos_r = pos0 + lax.broadcasted_iota(jnp.int32, (tq, nr), 0)
    valid = lax.broadcasted_iota(jnp.int32, (tq, nr), 1) * CMP_STRIDE + (CMP_LEN - 1) <= pos_r
    pos_c = pos0 + lax.broadcasted_iota(jnp.int32, (nr, tq), 1)
    valid_t = lax.broadcasted_iota(jnp.int32, (nr, tq), 0) * CMP_STRIDE + (CMP_LEN - 1) <= pos_c
    jrow = lax.broadcasted_iota(jnp.int32, (nsel, tq), 0)
    cur = (pos0 + lax.broadcasted_iota(jnp.int32, (nsel, tq), 1)) // SEL_BLOCK
    forced = (jrow == 0) | (jrow == cur) | (jrow == cur - 1)
    ovt = ovt_ref[...]

    for g in range(NSA_GROUPS):
        qg = qn_ref[:, g * GW:(g + 1) * GW]
        kg = kc_ref[:, g * GW:(g + 1) * GW]
        vg = vc_ref[:, g * GW:(g + 1) * GW]
        o_g = jnp.zeros((tq, GW), F32)
        pt_sum = jnp.zeros((nr, tq), F32)
        for hh in range(2):
            qm = jnp.where(hl == hh, qg, jnp.zeros_like(qg))
            s = jnp.where(valid, _dot_nt(qm, kg), NEG_INF)
            e = jnp.exp(s - jnp.max(s, axis=1, keepdims=True))
            p = jnp.where(valid, e / jnp.sum(e, axis=1, keepdims=True), 0.0)
            o_g = jnp.where(hl == hh, _dot(p.astype(BF16), vg), o_g)
            st = jnp.where(valid_t, _dot_nt(kg, qm), NEG_INF)
            et = jnp.exp(st - jnp.max(st, axis=0, keepdims=True))
            pt_sum = pt_sum + jnp.where(valid_t, et / jnp.sum(et, axis=0, keepdims=True), 0.0)
        ocmp_ref[:, g * GW:(g + 1) * GW] = o_g
        p_hi = pt_sum.astype(BF16)
        p_lo = (pt_sum - p_hi.astype(F32)).astype(BF16)
        imp = _dot(ovt, p_hi) + _dot(ovt, p_lo)
        val = jnp.where(jrow <= cur, imp + FORCE_BONUS * forced.astype(F32), NEG_INF)
        v_ref[...] = val

        def rank(jp, cnt):
            row = v_ref[pl.ds(jp, 1), :]
            tie = jnp.where(jrow > jp, 1.0, 0.0)
            return cnt + jnp.where(row > val, 1.0, jnp.where(row == val, tie, 0.0))

        cnt = lax.fori_loop(0, nsel, rank, jnp.zeros((nsel, tq), F32))
        sel_ref[0, g] = ((cnt < n_top) & (jrow <= cur)).astype(F32)


def _nsa_cmp_weights(cmp_pe, cmp_w):
    half = CMP_LEN // 2
    wl = cmp_w.astype(F32).reshape(2, 2, half, HEAD_DIM, HEAD_DIM)
    eye2 = jnp.eye(2, dtype=F32)
    w2 = jnp.einsum('kardz,kK,gG,h->arkgdKGhz', wl, eye2, eye2, jnp.ones((2,), F32))
    w2 = w2.reshape(2, half * 4 * HEAD_DIM, 8 * HEAD_DIM)
    pl_ = cmp_pe.astype(F32).reshape(2, 2, half, HEAD_DIM)
    pe2 = jnp.broadcast_to(pl_.transpose(1, 2, 0, 3)[:, :, :, None, :], (2, half, 2, 2, HEAD_DIM))
    return w2.astype(BF16), pe2.reshape(2, 1, half * 4 * HEAD_DIM)


def _nsa_cmp(kcvc, qn, cmp_pe, cmp_w, knorm0, bsz, seq, tq=512):
    t = bsz * seq
    w = MIX_WIDTH
    tq = min(tq, seq)
    nq = seq // tq
    nr = seq // CMP_STRIDE
    nsel = seq // SEL_BLOCK
    n_top = min(SEL_TOPK, nsel)
    w2, pe2 = _nsa_cmp_weights(cmp_pe, cmp_w)
    xr = kcvc.reshape(t // CMP_STRIDE, CMP_STRIDE * w)
    kg = jnp.tile(knorm0.astype(F32), w // HEAD_DIM).reshape(1, w)
    n_i = np.arange(nr)[:, None] * CMP_STRIDE
    j_i = np.arange(nsel)[None, :] * SEL_BLOCK
    ov = ((n_i < j_i + SEL_BLOCK) & (n_i + CMP_LEN > j_i)).astype(np.float32)
    ov[nr - 1, :] = 0.0
    kin = CMP_STRIDE * w
    return pl.pallas_call(
        functools.partial(_nsa_cmp_kernel, n_top=n_top),
        grid=(bsz, nq),
        in_specs=[pl.BlockSpec((nr, kin), lambda b, i: (b, 0)),
                  pl.BlockSpec((2, 1, kin), lambda b, i: (0, 0, 0)),
                  pl.BlockSpec((None, kin, 2 * w), lambda b, i: (0, 0, 0)),
                  pl.BlockSpec((None, kin, 2 * w), lambda b, i: (1, 0, 0)),
                  pl.BlockSpec((1, w), lambda b, i: (0, 0)),
                  pl.BlockSpec((nsel, nr), lambda b, i: (0, 0)),
                  pl.BlockSpec((tq, w), lambda b, i: (b * nq + i, 0))],
        out_specs=[pl.BlockSpec((tq, w), lambda b, i: (b * nq + i, 0)),
                   pl.BlockSpec((1, NSA_GROUPS, nsel, tq), lambda b, i: (b, 0, 0, i))],
        out_shape=[jax.ShapeDtypeStruct((t, w), F32),
                   jax.ShapeDtypeStruct((bsz, NSA_GROUPS, nsel, seq), F32)],
        scratch_shapes=[pltpu.VMEM((nr, w), BF16), pltpu.VMEM((nr, w), BF16), pltpu.VMEM((nsel, tq), F32)],
        compiler_params=_cp("parallel", "arbitrary"),
        name="nsa_cmp",
    )(xr, pe2, w2, w2, kg, jnp.asarray(ov.T, BF16), qn)


def _nsa_attn_kernel(qr_ref, ks_ref, kw_ref, vs_ref, vw_ref, sel_ref, ocmp_ref, gate_ref, o_ref, *, kc, wt):
    tq = qr_ref.shape[0]
    i = pl.program_id(2)
    g = pl.program_id(1)
    hl = _head_of_lane((tq, GW), 1)
    q = qr_ref[...]
    qs = jnp.concatenate([jnp.where(hl == 0, q, jnp.zeros_like(q)), jnp.where(hl == 1, q, jnp.zeros_like(q))], axis=0)
    nbk = kc // SEL_BLOCK

    def lane_qpos(rows):
        return i * tq + lax.broadcasted_iota(jnp.int32, (rows, 2 * tq), 1) % tq

    def finish(acc, l):
        ot = (acc / l).T
        return jnp.where(hl == 0, ot[0:tq, :], ot[tq:2 * tq, :])

    qpos_s = lane_qpos(kc)
    krow_s = lax.broadcasted_iota(jnp.int32, (kc, 2 * tq), 0)

    def sel_body(c, carry, diagonal):
        m, l, acc = carry
        k0 = pl.multiple_of(c * kc, kc)
        st = _dot_nt(ks_ref[pl.ds(k0, kc), :], qs)
        srows = sel_ref[0, 0, pl.ds(pl.multiple_of(c * nbk, nbk), nbk), :]
        srows = jnp.concatenate([srows, srows], axis=1)
        smask = jnp.concatenate([jnp.broadcast_to(srows[r:r + 1, :], (SEL_BLOCK, 2 * tq)) for r in range(nbk)],
                                axis=0)
        msk = smask > 0.5
        if diagonal:
            msk = msk & (k0 + krow_s <= qpos_s)
        st = jnp.where(msk, st, NEG_INF)
        m_new = jnp.maximum(m, jnp.max(st, axis=0, keepdims=True))
        p = jnp.exp(st - m_new)
        alpha = jnp.exp(m - m_new)
        l = l * alpha + jnp.sum(p, axis=0, keepdims=True)
        acc = acc * alpha + _dot(vs_ref[0, 0, c], p.astype(BF16))
        return m_new, l, acc

    init = (jnp.full((1, 2 * tq), NEG_INF, F32), jnp.zeros((1, 2 * tq), F32), jnp.zeros((GW, 2 * tq), F32))
    n_before = (i * tq) // kc
    carry = lax.fori_loop(0, n_before, functools.partial(sel_body, diagonal=False), init)
    _, l_s, acc_s = sel_body(n_before, carry, True)
    o_sel = finish(acc_s, l_s)

    j0 = jnp.maximum(i - (wt - 1), 0)
    k0 = pl.multiple_of(j0 * tq, tq)
    span = wt * tq
    st = _dot_nt(kw_ref[pl.ds(k0, span), :], qs)
    kpos = k0 + lax.broadcasted_iota(jnp.int32, (span, 2 * tq), 0)
    qpos_w = lane_qpos(span)
    msk = (kpos <= qpos_w) & (kpos > qpos_w - WINDOW)
    st = jnp.where(msk, st, NEG_INF)
    p = jnp.exp(st - jnp.max(st, axis=0, keepdims=True))
    vt =jnp.concatenate([vw_ref[0, 0, j0 + r] for r in range(wt)], axis=1)
    o_win = finish(_dot(vt, p.astype(BF16)), jnp.sum(p, axis=0, keepdims=True))

    gb = _sigmoid(gate_ref[...])

    def gate(branch):
        cols = []
        for hh in range(2):
            c0 = gb[:, hh * 3 + branch:hh * 3 + branch + 1]
            c1 = gb[:, (2 + hh) * 3 + branch:(2 + hh) * 3 + branch + 1]
            cols.append(jnp.where(g == 0, c0, c1))
        return _expand_heads(cols, (tq, GW))

    o_ref[...] = gate(0) * ocmp_ref[...] + gate(1) * o_sel + gate(2) * o_win


def _nsa_attn(proj, qr, ks, kw, vst, vwt, sel, o_cmp, bsz, seq):
    t = bsz * seq
    w = MIX_WIDTH
    tq = min(NSA_TQ, seq)
    nq = seq // tq
    nsel = seq // SEL_BLOCK
    kc = min(NSA_KC, seq)
    wt = min(WINDOW // tq + 1, nq)
    kspec = pl.BlockSpec((seq, GW), lambda b, g, i: (b, g))
    return pl.pallas_call(
        functools.partial(_nsa_attn_kernel, kc=kc, wt=wt),
        grid=(bsz, NSA_GROUPS, nq),
        in_specs=[pl.BlockSpec((tq, GW), lambda b, g, i: (b * nq + i, g)),
                  kspec, kspec,
                  pl.BlockSpec((1, 1, seq // kc, GW, kc), lambda b, g, i: (b, g, 0, 0, 0)),
                  pl.BlockSpec((1, 1, nq, GW, tq), lambda b, g, i: (b, g, 0, 0, 0)),
                  pl.BlockSpec((1, 1, nsel, tq), lambda b, g, i: (b, g, 0, i)),
                  pl.BlockSpec((tq, GW), lambda b, g, i: (b * nq + i, g)),
                  pl.BlockSpec((tq, LANES), lambda b, g, i: (b * nq + i, OFF_NG // LANES))],
        out_specs=pl.BlockSpec((tq, GW), lambda b, g, i: (b * nq + i, g)),
        out_shape=jax.ShapeDtypeStruct((t, w), F32),
        compiler_params=_cp("parallel", "parallel", "arbitrary"),
        name="nsa_attn",
    )(qr, ks, kw, vst, vwt, sel, o_cmp, proj)


def _nsa(proj, kcvc, cos4, sin4, qnorm_g, knorm_g, cmp_pe, cmp_w, bsz, seq):
    qn, qr, ks, kw, vst, vwt = _nsa_prep(proj, cos4, sin4, qnorm_g, knorm_g, bsz, seq)
    o_cmp, sel = _nsa_cmp(kcvc, qn, cmp_pe, cmp_w, knorm_g[0], bsz, seq)
    return _nsa_attn(proj, qr, ks, kw, vst, vwt, sel, o_cmp, bsz, seq)


PEER_TT = 128
PEER_CT = 8
HALF_D = 512


SUBLANES = 8
CODE_BITS = 127
FAR_BELOW = -3.0e38


def _with_code(x, code):
    bits = lax.bitcast_convert_type(x, jnp.int32)
    return lax.bitcast_convert_type((bits & ~CODE_BITS) | code, F32)


def _split_code(x):
    bits = lax.bitcast_convert_type(x, jnp.int32)
    return lax.bitcast_convert_type(bits & ~CODE_BITS, F32), bits & CODE_BITS


def _sort16_desc(xs):
    xs = list(xs)
    n = len(xs)
    k = 2
    while k <= n:
        j = k // 2
        while j >= 1:
            for i in range(n):
                l = i ^ j
                if l > i:
                    hi, lo = jnp.maximum(xs[i], xs[l]), jnp.minimum(xs[i], xs[l])
                    xs[i], xs[l] = (hi, lo) if (i & k) == 0 else (lo, hi)
            j //= 2
        k *= 2
    return xs


def _merge16_desc(xs):
    xs = list(xs)
    j = len(xs) // 2
    while j >= 1:
        for i in range(len(xs)):
            l = i ^ j
            if l > i:
                xs[i], xs[l] = jnp.maximum(xs[i], xs[l]), jnp.minimum(xs[i], xs[l])
        j //= 2
    return xs


def _top16_columns(x):
    n = PEER_TOPK
    xs = _sort16_desc([x[SUBLANES * j:SUBLANES * (j + 1), :] for j in range(n)])
    shift = SUBLANES // 2
    while shift >= 1:
        rolled = [pltpu.roll(a, shift, 0) for a in xs]
        xs = _merge16_desc([jnp.maximum(xs[i], rolled[n - 1 - i]) for i in range(n)])
        shift //= 2
    return xs


_PEER_CAND_TILES = ((0, 0, 8), (0, 1, 8), (1, 0, 8), (2, 0, 5), (3, 0, 4), (4, 0, 3), (5, 0, 2), (6, 0, 2), (7, 0, 2))


def _peer_route_kernel(q_ref, key_ref, e_ref, g_ref, e_scr, g_scr):
    tt = q_ref.shape[0]
    nk = PEER_NKEYS
    n = PEER_TOPK
    row = lax.broadcasted_iota(jnp.int32, (nk, tt), 0)
    sub = lax.broadcasted_iota(jnp.int32, (SUBLANES, tt), 0)
    vals, ids = [], []
    for p in range(2):
        st = _dot_nt(key_ref[0, p], q_ref[:, p * PEER_KDIM:(p + 1) * PEER_KDIM])
        top = [_split_code(a) for a in _top16_columns(_with_code(st, (nk - 1) - row))]
        vals.append([v for v, _ in top])
        ids.append([(nk - 1) - c for _, c in top])
    (v1, v2), (i1, i2) = vals, ids

    def stack(xs, lo):
        out = xs[lo]
        for s in range(1, SUBLANES):
            out = jnp.where(sub == s, xs[lo + s], out)
        return out

    v2t, i2t = (stack(v2, 0), stack(v2, SUBLANES)), (stack(i2, 0), stack(i2, SUBLANES))
    cand, cexp = [], []
    for a, tile, nvalid in _PEER_CAND_TILES:
        v = v1[a] + v2t[tile]
        cand.append(v if nvalid == SUBLANES else jnp.where(sub < nvalid, v, FAR_BELOW))
        cexp.append(i1[a] * nk + i2t[tile])
    cand.append(stack(v1, SUBLANES) + v2[0])
    cexp.append(stack(i1, SUBLANES) * nk + i2[0])
    n_tiles = len(cand)
    slot_code = [(nk - 1) - (c * SUBLANES + sub) for c in range(n_tiles)]
    coded = [_with_code(v, sc) for v, sc in zip(cand, slot_code)]
    coded += [jnp.full((SUBLANES, tt), FAR_BELOW, F32)] * (n - n_tiles)
    top = [_split_code(a) for a in _top16_columns(jnp.concatenate(coded, axis=0))]
    call = jnp.concatenate(cexp, axis=0)
    slot = (nk - 1) - lax.broadcasted_iota(jnp.int32, call.shape, 0)
    ex = [jnp.exp(v - top[0][0]) for v, _ in top]
    tot = ex[0]
    for k in range(1, n):
        tot = tot + ex[k]
    krow = lax.broadcasted_iota(jnp.int32, (n, tt), 0)
    e_tile = jnp.zeros((n, tt), F32)
    g_tile = jnp.zeros((n, tt), F32)
    for k in range(n):
        hit = slot == jnp.concatenate([top[k][1]] * n_tiles, axis=0)
        e_k = jnp.sum(jnp.where(hit, call, 0), axis=0, keepdims=True)
        e_tile = jnp.where(krow == k, e_k.astype(F32), e_tile)
        g_tile = jnp.where(krow == k, (ex[k] / tot)[0:1, :], g_tile)
    h = pl.program_id(1)
    r0 = pl.multiple_of(h * n, n)
    e_scr[pl.ds(r0, n), :] = e_tile
    g_scr[pl.ds(r0, n), :] = g_tile

    @pl.when(h == pl.num_programs(1) - 1)
    def _():
        e_ref[...] = e_scr[...].T.astype(jnp.int32)
        g_ref[...] = g_scr[...].T


def _peer_route(qp, keys):
    t = qp.shape[0]
    tt = min(PEER_TT, t)
    ne = PEER_HEADS * PEER_TOPK
    return pl.pallas_call(
        _peer_route_kernel,
        grid=(t // tt, PEER_HEADS),
        in_specs=[pl.BlockSpec((tt, 2 * PEER_KDIM), lambda i, h: (i, h)),
                  pl.BlockSpec((1, 2, PEER_NKEYS, PEER_KDIM), lambda i, h: (h, 0, 0, 0))],
        out_specs=[pl.BlockSpec((tt, ne), lambda i, h: (i, 0)),
                   pl.BlockSpec((tt, ne), lambda i, h: (i, 0))],
        out_shape=[jax.ShapeDtypeStruct((t, ne), jnp.int32),
                   jax.ShapeDtypeStruct((t, ne), F32)],
        scratch_shapes=[pltpu.VMEM((ne, tt), F32), pltpu.VMEM((ne, tt), F32)],
        compiler_params=_cp("parallel", "arbitrary"),
        name="peer_route",
    )(qp, keys)


def _pack_tables_kernel(u_ref, v_ref, o_ref):
    def pack(x):
        lo = lax.bitcast_convert_type(x[:, 0:HALF_D].astype(BF16).astype(F32), jnp.int32)
        hi = lax.bitcast_convert_type(x[:, HALF_D:2 * HALF_D].astype(BF16).astype(F32), jnp.int32)
        return lax.shift_right_logical(lo, 16) | (hi & jnp.int32(-65536))

    o_ref[:, 0:HALF_D] = pack(u_ref[...])
    o_ref[:, HALF_D:2 * HALF_D] = pack(v_ref[...])


def _pack_tables(u_tabs, v_tabs, layer, tr=512):
    _, e, d = u_tabs.shape
    assert d == 2 * HALF_D
    spec_in = pl.BlockSpec((None, tr, d), lambda i: (layer, i, 0))
    spec = pl.BlockSpec((tr, d), lambda i: (i, 0))
    return pl.pallas_call(
        _pack_tables_kernel,
        grid=(e // tr,),
        in_specs=[spec_in, spec_in],
        out_specs=spec,
        out_shape=jax.ShapeDtypeStruct((e, d), jnp.int32),
        compiler_params=_cp("parallel"),
        name="peer_pack",
    )(u_tabs, v_tabs)


def _unpack_rows(wd):
    lo = lax.bitcast_convert_type(lax.shift_left(wd, 16), F32)
    hi = lax.bitcast_convert_type(lax.bitwise_and(wd, jnp.int32(-65536)), F32)
    return lo, hi


SC_WINDOW = 32


def _sc_gather(table, idx):
    from jax.experimental.pallas import tpu_sc as plsc
    n = idx.shape[0]
    width = table.shape[1]
    mesh = plsc.VectorSubcoreMesh(core_axis_name="core", subcore_axis_name="subcore")

    @functools.partial(pl.kernel, out_type=jax.ShapeDtypeStruct((n, width), table.dtype), mesh=mesh)
    def gather(tab_hbm, idx_hbm, out_hbm):
        def body(idx_vmem, out_vmem):
            pltpu.sync_copy(tab_hbm.at[idx_vmem.at[0, pl.ds(0, SC_WINDOW)]], out_vmem)

        pltpu.emit_pipeline(
            body,
            grid=(n // SC_WINDOW,),
            in_specs=[pl.BlockSpec((1, LANES), lambda i: (0, i))],
            out_specs=[pl.BlockSpec((SC_WINDOW, width), lambda i: (i, 0))],
            core_axis_name=("core", "subcore"),
            dimension_semantics=(pltpu.PARALLEL,),
            trace_scopes=False,
        )(idx_hbm, out_hbm)

    idx_pad = jnp.pad(idx.reshape(n // SC_WINDOW, SC_WINDOW), ((0, 0), (0, LANES - SC_WINDOW)))
    return gather(table, idx_pad.reshape(1, (n // SC_WINDOW) * LANES))


def _peer_combine_kernel(x_ref, g2_ref, rows_ref, gate_ref, o_ref):
    ne = PEER_HEADS * PEER_TOPK
    x = x_ref[...]
    ct = x.shape[0]
    xn = x * lax.rsqrt(jnp.mean(x * x, axis=-1, keepdims=True) + NORM_EPS) * g2_ref[...]
    gate_t = jnp.concatenate([gate_ref[...]] * (ne // ct), axis=0).T
    for j in range(ct):
        u_lo, u_hi = _unpack_rows(rows_ref[j * ne:(j + 1) * ne, 0:HALF_D])
        xr = xn[j:j + 1, :]
        h = jnp.sum(u_lo * xr[:, 0:HALF_D] + u_hi * xr[:, HALF_D:2 * HALF_D], axis=1, keepdims=True)
        act = 0.5 * h * (1.0 + lax.erf(h * (2.0 ** -0.5)))
        wgt = gate_t[:, j:j + 1] * act
        v_lo, v_hi = _unpack_rows(rows_ref[j * ne:(j + 1) * ne, HALF_D:2 * HALF_D])
        o_ref[j:j + 1, 0:HALF_D] = x[j:j + 1, 0:HALF_D] + jnp.sum(wgt * v_lo, axis=0, keepdims=True)
        o_ref[j:j + 1, HALF_D:2 * HALF_D] = x[j:j + 1, HALF_D:2 * HALF_D] + jnp.sum(wgt * v_hi, axis=0, keepdims=True)


def _peer_combine(x, g2, rows, gates, first_token):
    t, d = x.shape
    ne = PEER_HEADS * PEER_TOPK
    ct = PEER_CT
    steps = rows.shape[0] // (ct * ne)
    off = first_token // ct
    return pl.pallas_call(
        _peer_combine_kernel,
        grid=(steps,),
        in_specs=[pl.BlockSpec((ct, d), lambda i: (off + i, 0)),
                  pl.BlockSpec((1, d), lambda i: (0, 0)),
                  pl.BlockSpec((ct * ne, d), lambda i: (i, 0)),
                  pl.BlockSpec((ct, ne), lambda i: (off + i, 0))],
        out_specs=pl.BlockSpec((ct, d), lambda i: (off + i, 0)),
        out_shape=jax.ShapeDtypeStruct((t, d), F32),
        input_output_aliases={0: 0},
        compiler_params=_cp("parallel"),
        name="peer_combine",
    )(x, g2.reshape(1, d), rows, gates)


PEER_TOKENS_PER_GATHER = 2048


def _peer_route_stage(x, g2, wq_b, keys_b):
    t = x.shape[0]
    ne = PEER_HEADS * PEER_TOPK
    qp = _norm_matmul(x, g2, wq_b, out_dtype=BF16)
    e_tok, g_tok = _peer_route(qp, keys_b)
    return e_tok.reshape(t * ne), g_tok


def _peer_gather_stage(table, idx, t, gather_fn):
    ne = PEER_HEADS * PEER_TOPK
    tc = min(PEER_TOKENS_PER_GATHER, t)
    return [gather_fn(table, idx[c * tc * ne:(c + 1) * tc * ne]) for c in range(t // tc)]


def _peer_combine_stage(x, g2, rows_list, gates):
    tc = x.shape[0] // len(rows_list)
    for c, rows in enumerate(rows_list):
        x = _peer_combine(x, g2, rows, gates, c * tc)
    return x


def _peer(x, g2, wq, keys, u_tabs, v_tabs, layer, gather_fn):
    idx, gates = _peer_route_stage(x, g2, wq.astype(BF16), keys.astype(BF16))
    table = _pack_tables(u_tabs, v_tabs, layer)
    rows_list = _peer_gather_stage(table, idx, x.shape[0], gather_fn)
    return _peer_combine_stage(x, g2, rows_list, gates)


_IN_WIDTHS = (256, 256, 256, 256, 256, 256, 256, 4, 4, 256, 256, 128, 128, 128, 128, 128, 128, 12,
              256, 256, 256, 256)


def _dup_groups(wcols):
    g0, g1 = wcols[:, :HEAD_DIM], wcols[:, HEAD_DIM:]
    return jnp.concatenate([g0, g0, g1, g1], axis=1)


def _layout_w_in(w_in):
    offs = np.cumsum((0,) + _IN_WIDTHS)
    cols = [w_in[:, offs[i]:offs[i + 1]] for i in range(len(_IN_WIDTHS))]
    (hq, hf, hi, hg, mq, mk, mv, mi, mf, mo, nq, nkc, nvc, nks, nvs, nkw, nvw, ng, rq, rk, rv, rg) = cols
    d = w_in.shape[0]
    pad = lambda c, n: jnp.concatenate([c, jnp.zeros((d, n - c.shape[1]), w_in.dtype)], axis=1)
    main = jnp.concatenate([hq, hf, hi, hg, mq, mk, mv, mo, rq, rk, rv, rg,
                            _dup_groups(nks), _dup_groups(nkw), nq, nvs, nvw,
                            pad(jnp.concatenate([mi, mf], axis=1), LANES), pad(ng, LANES)], axis=1)
    assert main.shape[1] == N_MAIN
    kcvc = jnp.concatenate([nkc, nvc], axis=1)
    return main.astype(BF16), kcvc.astype(BF16)


def kernel(x, norm1_g, w_in, hgrn_lb, hgrn_onorm_g, mlstm_conv_w, mlstm_conv_b, mlstm_gate_b, mlstm_onorm_g, nsa_qnorm_g, nsa_knorm_g, nsa_cmp_pe, nsa_cmp_w, ret_onorm_g, w_up, w_gate, w_out, norm2_g, peer_wq, peer_keys, peer_u, peer_v):
    bsz, seq, d = x.shape
    t = bsz * seq
    depth = w_in.shape[0]
    cos_t, sin_t = _rope_lane_tables(seq)
    cos4, sin4 = jnp.tile(cos_t, (1, 2)), jnp.tile(sin_t, (1, 2))
    lb_cum = jnp.cumsum(jax.nn.softmax(hgrn_lb.astype(F32), axis=0), axis=0)
    lb_all = lb_cum - lb_cum[0:1]
    weights = []
    for l in range(depth):
        w_main, w_kcvc = _layout_w_in(w_in[l])
        weights.append(dict(
            main=w_main, kcvc=w_kcvc, gate=w_gate[l].astype(BF16), up=w_up[l].astype(BF16),
            out=w_out[l].astype(BF16), lb=_hgrn_lb_rows(lb_all[l]), wq=peer_wq[l].astype(BF16),
            keys=peer_keys[l].astype(BF16), table=_pack_tables(peer_u, peer_v, l)))

    def mixer_steps(xh, l, nb):
        wl = weights[l]
        st = {}

        def proj(dep):
            st["proj"] = _norm_matmul(xh, _after(norm1_g[l], dep), wl["main"])
            return st["proj"]

        def gates(dep):
            st["gates"] = _norm_matmul(xh, _after(norm1_g[l], dep), wl["gate"], act="sigmoid", out_dtype=BF16)
            return st["gates"]

        def hgrn(dep):
            st["oh"] = _hgrn(st["proj"], wl["lb"], _after(hgrn_onorm_g[l], dep), nb, seq)
            return st["oh"]

        def mlstm(dep):
            st["om"] = _mlstm(st["proj"], mlstm_conv_w[l], mlstm_conv_b[l], mlstm_gate_b[l],
                              _after(mlstm_onorm_g[l], dep), nb, seq)
            return st["om"]

        def ret(dep):
            st["or"] = _ret(st["proj"], cos4, sin4, _after(ret_onorm_g[l], dep), nb, seq)
            return st["or"]

        def nsa_front(dep):
            kcvc = _norm_matmul(xh, _after(norm1_g[l], dep), wl["kcvc"])
            qn, qr, ks, kw, vst, vwt = _nsa_prep(st["proj"], cos4, sin4, nsa_qnorm_g[l], nsa_knorm_g[l], nb, seq)
            o_cmp, sel = _nsa_cmp(kcvc, qn, nsa_cmp_pe[l], nsa_cmp_w[l], nsa_knorm_g[l][0], nb, seq)
            st["nsa"] = (qr, ks, kw, vst, vwt, sel, o_cmp)
            return o_cmp

        def nsa_attn(dep):
            del dep
            st["on"] = _nsa_attn(st["proj"], *st["nsa"], nb, seq)
            return st["on"]

        def merge(dep):
            del dep
            st["xm"] = _merge(xh, st["gates"], (st["oh"], st["om"], st["on"], st["or"]), wl["up"], wl["out"])
            return st["xm"]

        def route(dep):
            st["idx"], st["pgates"] = _peer_route_stage(st["xm"], _after(norm2_g[l], dep), wl["wq"], wl["keys"])
            return st["pgates"]

        return [proj, gates, hgrn, mlstm, ret, nsa_front, nsa_attn, merge, route], st

    combine_slots = (0, 8, 8, 8, 8, 8, 8, 8)

    def combine_steps(l, xm, rows_list, pgates):
        box = {"x": xm}
        tc = xm.shape[0] // len(rows_list)

        def make(c):
            def step(dep):
                box["x"] = _peer_combine(box["x"], _after(norm2_g[l], dep), rows_list[c], pgates, c * tc)
                return box["x"]
            return step

        return [make(c) for c in range(len(rows_list))], box

    n_groups = next(n for n in (4, 2, 1) if bsz % n == 0)
    nb = bsz // n_groups
    xs = [x[g * nb:(g + 1) * nb].reshape(nb * seq, d) for g in range(n_groups)]
    dep = None
    lag = min(2, n_groups - 1)
    pending = []
    for l in range(depth):
        for g in range(n_groups):
            msteps, st = mixer_steps(xs[g], l, nb)
            due = pending.pop(0) if len(pending) == lag and lag > 0 else None
            csteps = due[1] if due is not None else []
            ci = 0
            for si, mstep in enumerate(msteps):
                dep = mstep(dep)
                while ci < len(csteps) and (ci >= len(combine_slots) or combine_slots[ci] <= si):
                    dep = csteps[ci](dep)
                    ci += 1
            for cstep in csteps[ci:]:
                dep = cstep(dep)
            if due is not None:
                xs[due[0]] = due[2]["x"]
            rows_list = _peer_gather_stage(weights[l]["table"], st["idx"], nb * seq, _sc_gather)
            csteps, box = combine_steps(l, st["xm"], rows_list, st["pgates"])
            pending.append((g, csteps, box))
            if lag == 0:
                for cstep in pending.pop(0)[1]:
                    dep = cstep(dep)
                xs[g] = box["x"]
    for pg, csteps, box in pending:
        for cstep in csteps:
            dep = cstep(dep)
        xs[pg] = box["x"]
    return jnp.concatenate(xs, axis=0).reshape(bsz, seq, d)
```

```python
import functools
import math

import numpy as np
import jax
import jax.numpy as jnp
from jax import lax
from jax.experimental import pallas as pl
from jax.experimental.pallas import tpu as pltpu

F32 = jnp.float32
BF16 = jnp.bfloat16

HEAD_DIM = 64
N_HEADS = 4
MIX_WIDTH = N_HEADS * HEAD_DIM
CHUNK = 64
NORM_EPS = 1e-6
NEG_INF = -1e30
ROPE_THETA = 10000.0
CONV_W = 4
NSA_GROUPS = 2
CMP_LEN = 32
CMP_STRIDE = 16
SEL_BLOCK = 64
SEL_TOPK = 16
WINDOW = 512
FORCE_BONUS = 1e3
PEER_HEADS = 8
PEER_NKEYS = 128
PEER_TOPK = 16
PEER_KDIM = 128

LANES = 128
VMEM_LIMIT = 48 * 1024 * 1024

OFF_H, OFF_M, OFF_R, OFF_KD, OFF_NQ, OFF_V, OFF_MG, OFF_NG = 0, 1024, 2048, 3072, 3584, 3840, 4096, 4224
N_MAIN = 4352


def _cp(*sem):
    return pltpu.CompilerParams(dimension_semantics=sem, vmem_limit_bytes=VMEM_LIMIT)


def _dot(a, b):
    return jnp.dot(a, b, preferred_element_type=F32)


def _dot_nt(a, b):
    return lax.dot_general(a, b, (((1,), (1,)), ((), ())), preferred_element_type=F32)


def _dot_tn(a, b):
    return lax.dot_general(a, b, (((0,), (0,)), ((), ())), preferred_element_type=F32)


def _split3(x):
    hi = x.astype(BF16)
    r1 = x - hi.astype(F32)
    mid = r1.astype(BF16)
    lo = (r1 - mid.astype(F32)).astype(BF16)
    return hi, mid, lo


def _dot01_l(m01, x):
    hi, mid, lo = _split3(x)
    return _dot(m01, hi) + _dot(m01, mid) + _dot(m01, lo)


def _dot01_r(x, m01):
    hi, mid, lo = _split3(x)
    return _dot(hi, m01) + _dot(mid, m01) + _dot(lo, m01)


def _head_of_lane(shape, axis):
    return lax.broadcasted_iota(jnp.int32, shape, axis) // HEAD_DIM


def _block_ones(n, dtype=BF16):
    r = lax.broadcasted_iota(jnp.int32, (n, n), 0) // HEAD_DIM
    c = lax.broadcasted_iota(jnp.int32, (n, n), 1) // HEAD_DIM
    return (r == c).astype(dtype)


def _group_sum(x, ones_bd):
    hi = x.astype(BF16)
    lo = (x - hi.astype(F32)).astype(BF16)
    return _dot(hi, ones_bd) + _dot(lo, ones_bd)


def _head_rms(x, gain, ones_bd):
    ms = _group_sum(x * x, ones_bd) * (1.0 / HEAD_DIM)
    return x * lax.rsqrt(ms + NORM_EPS) * gain


def _sigmoid(x):
    return 1.0 / (1.0 + jnp.exp(-x))


def _silu(x):
    return x * _sigmoid(x)


def _log_sigmoid(x):
    return jnp.minimum(x, 0.0) - jnp.log(1.0 + jnp.exp(-jnp.abs(x)))


def _stack_heads(x, n_heads=N_HEADS):
    hl = _head_of_lane(x.shape, 1)
    return jnp.concatenate([jnp.where(hl == h, x, jnp.zeros_like(x)) for h in range(n_heads)], axis=0)


def _unstack_heads(r, c, n_heads=N_HEADS):
    hl = _head_of_lane((c, r.shape[1]), 1)
    out = jnp.zeros((c, r.shape[1]), F32)
    for h in range(n_heads):
        out = jnp.where(hl == h, r[h * c:(h + 1) * c, :], out)
    return out


def _rope(x, cos_t, sin_t):
    n = x.shape[1]
    first = (lax.broadcasted_iota(jnp.int32, x.shape, 1) % HEAD_DIM) < (HEAD_DIM // 2)
    partner = jnp.where(first, pltpu.roll(x, n - HEAD_DIM // 2, 1), pltpu.roll(x, HEAD_DIM // 2, 1))
    return x * cos_t + partner * sin_t


def _after_kernel(a_ref, dep_ref, o_ref):
    del dep_ref
    o_ref[...] = a_ref[...]


def _after(a, dep):
    if dep is None:
        return a
    a2 = a.reshape(1, a.size)
    out = pl.pallas_call(
        _after_kernel,
        in_specs=[pl.BlockSpec(a2.shape, lambda: (0, 0)), pl.BlockSpec(memory_space=pl.ANY)],
        out_specs=pl.BlockSpec(a2.shape, lambda: (0, 0)),
        out_shape=jax.ShapeDtypeStruct(a2.shape, a2.dtype),
        name="order_after",
    )(a2, dep)
    return out.reshape(a.shape)


def _norm_matmul_kernel(x_ref, g_ref, w_ref, o_ref, xn_ref, *, act):
    @pl.when(pl.program_id(1) == 0)
    def _():
        x = x_ref[...]
        ms = jnp.mean(x * x, axis=-1, keepdims=True)
        xn_ref[...] = (x * lax.rsqrt(ms + NORM_EPS) * g_ref[...]).astype(BF16)

    y = _dot(xn_ref[...], w_ref[...])
    if act == "sigmoid":
        y = _sigmoid(y)
    o_ref[...] = y.astype(o_ref.dtype)


def _norm_matmul(x, g, w, *, act=None, out_dtype=F32, tm=1024, tn=2176):
    t, d = x.shape
    w3 = w if w.ndim == 3 else w[None]
    n_per = w3.shape[2]
    tm = min(tm, t)
    tn = next(c for c in (tn, 2048, 1024, 512, 256, 128) if n_per % c == 0)
    per = n_per // tn
    n = w3.shape[0] * n_per
    assert t % tm == 0
    return pl.pallas_call(
        functools.partial(_norm_matmul_kernel, act=act),
        grid=(t // tm, n // tn),
        in_specs=[pl.BlockSpec((tm, d), lambda i, j: (i, 0)),
                  pl.BlockSpec((1, d), lambda i, j: (0, 0)),
                  pl.BlockSpec((None, d, tn), lambda i, j: (j // per, 0, j % per))],
        out_specs=pl.BlockSpec((tm, tn), lambda i, j: (i, j)),
        out_shape=jax.ShapeDtypeStruct((t, n), out_dtype),
        scratch_shapes=[pltpu.VMEM((tm, d), BF16)],
        compiler_params=_cp("parallel", "arbitrary"),
        name="norm_matmul",
    )(x, g.reshape(1, d), w3)


def _merge_kernel(x_ref, gate_ref, oh_ref, om_ref, on_ref, or_ref, wup_ref, wout_ref, o_ref):
    d = x_ref.shape[1]
    acc = None
    for m, r in enumerate((oh_ref, om_ref, on_ref, or_ref)):
        up = _dot(r[...].astype(BF16), wup_ref[m])
        term = gate_ref[:, m * d:(m + 1) * d].astype(F32) * up
        acc = term if acc is None else acc + term
    o_ref[...] = x_ref[...] + _dot(acc.astype(BF16), wout_ref[...])


def _merge(x, gates, outs, w_up, w_out, tm=512):
    t, d = x.shape
    tm = min(tm, t)
    mix = pl.BlockSpec((tm, MIX_WIDTH), lambda i: (i, 0))
    return pl.pallas_call(
        _merge_kernel,
        grid=(t // tm,),
        in_specs=[pl.BlockSpec((tm, d), lambda i: (i, 0)),
                  pl.BlockSpec((tm, 4 * d), lambda i: (i, 0)),
                  mix, mix, mix, mix,
                  pl.BlockSpec((4, MIX_WIDTH, d), lambda i: (0, 0, 0)),
                  pl.BlockSpec((d, d), lambda i: (0, 0))],
        out_specs=pl.BlockSpec((tm, d), lambda i: (i, 0)),
        out_shape=jax.ShapeDtypeStruct((t, d), F32),
        compiler_params=_cp("parallel"),
        name="merge",
    )(x, gates, *outs, w_up, w_out)


REC_BLOCK = 256


def _chunk_consts():
    t = lax.broadcasted_iota(jnp.int32, (CHUNK, CHUNK), 0)
    s = lax.broadcasted_iota(jnp.int32, (CHUNK, CHUNK), 1)
    return t, s


def _hgrn_levels():
    t = np.arange(CHUNK)
    rows = []
    masks = []
    h = CHUNK // 2
    while h >= 1:
        ref = (t // (2 * h)) * (2 * h) + h
        p = np.zeros((CHUNK, CHUNK), np.float32)
        p[t, np.minimum(ref, CHUNK - 1)] = 1.0
        rows.append(p)
        same = (t[:, None] // (2 * h)) == (t[None, :] // (2 * h))
        m = same & ((t[:, None] // h) % 2 == 1) & ((t[None, :] // h) % 2 == 0)
        masks.append(m.astype(np.float32))
        h //= 2
    masks.append(np.eye(CHUNK, dtype=np.float32))
    return np.concatenate(rows, 0), np.stack(masks, 0)


def _hgrn_kernel(p_ref, lb_ref, g_ref, psel_ref, lmask_ref, o_ref, st_ref):
    @pl.when(pl.program_id(1) == 0)
    def _():
        st_ref[...] = jnp.zeros_like(st_ref)

    c = CHUNK
    w = MIX_WIDTH
    ones_bd = _block_ones(w)
    bd_mask = _block_ones(w, F32)
    tri = (lax.broadcasted_iota(jnp.int32, (c, c), 0) >= lax.broadcasted_iota(jnp.int32, (c, c), 1)).astype(BF16)
    psel = psel_ref[...]
    n_lv = lmask_ref.shape[0]
    log_lb, log_1mlb, one_mlb = lb_ref[0:1, :], lb_ref[1:2, :], lb_ref[2:3, :]
    gain = g_ref[...]

    def chunk(ci, carry):
        r0 = pl.multiple_of(ci * c, c)
        q = _silu(p_ref[pl.ds(r0, c), 0:w])
        fl = p_ref[pl.ds(r0, c), w:2 * w]
        v = p_ref[pl.ds(r0, c), 2 * w:3 * w]
        gp = p_ref[pl.ds(r0, c), 3 * w:4 * w]
        a1 = jnp.broadcast_to(log_lb, fl.shape)
        a2 = log_1mlb + _log_sigmoid(fl)
        mx = jnp.maximum(a1, a2)
        log_f = mx + jnp.log(jnp.exp(a1 - mx) + jnp.exp(a2 - mx))
        k = one_mlb * _sigmoid(-fl)
        b = _dot01_l(tri, log_f)
        bref = _dot01_l(psel, b)
        vb = v.astype(BF16)
        a = jnp.zeros((N_HEADS * c, c), F32)
        for lv in range(n_lv):
            if lv < n_lv - 1:
                br = bref[lv * c:(lv + 1) * c, :]
                qs = q * jnp.exp(jnp.minimum(b - br, 0.0))
                ks = k * jnp.exp(jnp.minimum(br - b, 0.0))
            else:
                qs, ks = q, k
            s_lv = _dot_nt(_stack_heads(qs).astype(BF16), ks.astype(BF16))
            a = a + jnp.concatenate([lmask_ref[lv]] * N_HEADS, axis=0) * s_lv
        o = _unstack_heads(_dot(a.astype(BF16), vb), c)
        st = st_ref[...]
        o = o + _dot_nt((q * jnp.exp(b)).astype(BF16), st.astype(BF16))
        b_last = b[c - 1:c, :]
        kb = k * jnp.exp(b_last - b)
        st_ref[...] = st * jnp.exp(b_last) + bd_mask * _dot_tn(vb, kb.astype(BF16))
        y = _head_rms(o, gain, ones_bd) * _silu(gp)
        o_ref[pl.ds(r0, c), :] = y
        return carry

    lax.fori_loop(0, p_ref.shape[0] // c, chunk, 0)


def _hgrn(proj, lb_rows, gain, bsz, seq):
    psel, lmask = _hgrn_levels()
    tb = min(REC_BLOCK, seq)
    nb = seq // tb
    return pl.pallas_call(
        _hgrn_kernel,
        grid=(bsz, nb),
        in_specs=[pl.BlockSpec((tb, 4 * MIX_WIDTH), lambda b, i: (b * nb + i, OFF_H // (4 * MIX_WIDTH))),
                  pl.BlockSpec((8, MIX_WIDTH), lambda b, i: (0, 0)),
                  pl.BlockSpec((1, MIX_WIDTH), lambda b, i: (0, 0)),
                  pl.BlockSpec(psel.shape, lambda b, i: (0, 0)),
                  pl.BlockSpec(lmask.shape, lambda b, i: (0, 0, 0))],
        out_specs=pl.BlockSpec((tb, MIX_WIDTH), lambda b, i: (b * nb + i, 0)),
        out_shape=jax.ShapeDtypeStruct((bsz * seq, MIX_WIDTH), F32),
        scratch_shapes=[pltpu.VMEM((MIX_WIDTH, MIX_WIDTH), F32)],
        compiler_params=_cp("parallel", "arbitrary"),
        name="hgrn2",
    )(proj, lb_rows, gain.reshape(1, MIX_WIDTH), jnp.asarray(psel, BF16), jnp.asarray(lmask, F32))


def _ret_kernel(p_ref, cos_ref, sin_ref, dec_ref, decin_ref, g_ref, o_ref, st_ref):
    @pl.when(pl.program_id(1) == 0)
    def _():
        st_ref[...] = jnp.zeros_like(st_ref)

    c = CHUNK
    w = MIX_WIDTH
    ones_bd = _block_ones(w)
    bd_mask = _block_ones(w, F32)
    gain = g_ref[...]
    dec_q = dec_ref[0:c, :]
    dec_k = dec_ref[c:2 * c, :]
    dec_state = dec_ref[2 * c:2 * c + 1, :]
    dec_in = decin_ref[...]

    def chunk(ci, carry):
        r0 = pl.multiple_of(ci * c, c)
        cos_t = cos_ref[pl.ds(r0, c), :]
        sin_t = sin_ref[pl.ds(r0, c), :]
        q = _rope(p_ref[pl.ds(r0, c), 0:w], cos_t, sin_t)
        k = _rope(p_ref[pl.ds(r0, c), w:2 * w], cos_t, sin_t) * (HEAD_DIM ** -0.5)
        v = p_ref[pl.ds(r0, c), 2 * w:3 * w]
        gp = p_ref[pl.ds(r0, c), 3 * w:4 * w]
        vb = v.astype(BF16)
        a = _dot_nt(_stack_heads(q).astype(BF16), k.astype(BF16)) * dec_in
        o = _unstack_heads(_dot(a.astype(BF16), vb), c)
        st = st_ref[...]
        o = o + _dot_nt(q.astype(BF16), st.astype(BF16)) * dec_q
        st_ref[...] = st * dec_state + bd_mask * _dot_tn(vb, (k * dec_k).astype(BF16))
        o_ref[pl.ds(r0, c), :] = _head_rms(o, gain, ones_bd) * _silu(gp)
        return carry

    lax.fori_loop(0, p_ref.shape[0] // c, chunk, 0)


def _ret_consts():
    log_gamma = np.log1p(-np.exp2(-5.0 - np.arange(N_HEADS, dtype=np.float64)))
    t = np.arange(CHUNK, dtype=np.float64)
    lane_h = np.arange(MIX_WIDTH) // HEAD_DIM
    dec_q = np.exp(log_gamma[lane_h][None, :] * (t[:, None] + 1.0))
    dec_k = np.exp(log_gamma[lane_h][None, :] * (CHUNK - 1.0 - t[:, None]))
    dec_state = np.exp(log_gamma[lane_h] * CHUNK)[None, :]
    dec = np.concatenate([dec_q, dec_k, np.broadcast_to(dec_state, (8, MIX_WIDTH))], 0)
    diff = t[:, None] - t[None, :]
    dec_in = np.concatenate([np.where(diff >= 0, np.exp(log_gamma[h] * diff), 0.0) for h in range(N_HEADS)], 0)
    return dec.astype(np.float32), dec_in.astype(np.float32)


def _ret(proj, cos4, sin4, gain, bsz, seq):
    dec, dec_in = _ret_consts()
    tb = min(REC_BLOCK, seq)
    nb = seq // tb
    return pl.pallas_call(
        _ret_kernel,
        grid=(bsz, nb),
        in_specs=[pl.BlockSpec((tb, 4 * MIX_WIDTH), lambda b, i: (b * nb + i, OFF_R // (4 * MIX_WIDTH))),
                  pl.BlockSpec((tb, MIX_WIDTH), lambda b, i: (i, 0)),
                  pl.BlockSpec((tb, MIX_WIDTH), lambda b, i: (i, 0)),
                  pl.BlockSpec(dec.shape, lambda b, i: (0, 0)),
                  pl.BlockSpec(dec_in.shape, lambda b, i: (0, 0)),
                  pl.BlockSpec((1, MIX_WIDTH), lambda b, i: (0, 0))],
        out_specs=pl.BlockSpec((tb, MIX_WIDTH), lambda b, i: (b * nb + i, 0)),
        out_shape=jax.ShapeDtypeStruct((bsz * seq, MIX_WIDTH), F32),
        scratch_shapes=[pltpu.VMEM((MIX_WIDTH, MIX_WIDTH), F32)],
        compiler_params=_cp("parallel", "arbitrary"),
        name="retention",
    )(proj, cos4, sin4, jnp.asarray(dec), jnp.asarray(dec_in), gain.reshape(1, MIX_WIDTH))


def _hgrn_lb_rows(lb):
    lb = lb.astype(F32)
    rows = jnp.stack([jnp.log(lb), jnp.log1p(-lb), 1.0 - lb], 0)
    return jnp.concatenate([rows, jnp.zeros((5, lb.shape[0]), F32)], 0)


def _rope_lane_tables(seq):
    inv = 1.0 / (ROPE_THETA ** (jnp.arange(0, HEAD_DIM, 2, dtype=F32) / HEAD_DIM))
    ang = jnp.arange(seq, dtype=F32)[:, None] * inv[None, :]
    cos, sin = jnp.cos(ang), jnp.sin(ang)
    cos_t = jnp.tile(cos, (1, LANES // (HEAD_DIM // 2)))
    sin_t = jnp.tile(jnp.concatenate([-sin, sin], axis=1), (1, LANES // HEAD_DIM))
    return cos_t, sin_t


def _expand_heads(cols, shape):
    hl = _head_of_lane(shape, 1)
    out = jnp.broadcast_to(cols[-1], shape)
    for h in range(len(cols) - 2, -1, -1):
        out = jnp.where(hl == h, jnp.broadcast_to(cols[h], shape), out)
    return out


def _mlstm_kernel(p_ref, gcol_ref, grow_ref, cw_ref, cb_ref, gbr_ref, gbc_ref, g_ref, o_ref,
                  ct_ref, n_ref, m_ref, hist_ref, cbuf_ref, qk_ref):
    c = CHUNK
    w = MIX_WIDTH
    tb = p_ref.shape[0]

    @pl.when(pl.program_id(1) == 0)
    def _():
        ct_ref[...] = jnp.zeros_like(ct_ref)
        n_ref[...] = jnp.zeros_like(n_ref)
        m_ref[...] = jnp.zeros_like(m_ref)
        hist_ref[...] = jnp.zeros_like(hist_ref)

    cbuf_ref[0:8, :] = hist_ref[...]
    cbuf_ref[8:, :] = p_ref[:, 0:2 * w]
    hist_ref[...] = p_ref[tb - 8:tb, 0:2 * w]
    acc = jnp.broadcast_to(cb_ref[...], (tb, 2 * w))
    for j in range(CONV_W):
        acc = acc + cw_ref[j:j + 1, :] * cbuf_ref[pl.ds(8 - (CONV_W - 1) + j, tb), :]
    qk_ref[...] = _silu(acc)

    ones_bd = _block_ones(w)
    bd_mask = _block_ones(w, F32)
    ti = lax.broadcasted_iota(jnp.int32, (c, c), 0)
    si = lax.broadcasted_iota(jnp.int32, (c, c), 1)
    causal = ti >= si
    tri = causal.astype(BF16)
    tri_t = (ti <= si).astype(BF16)
    gain = g_ref[...]
    ones_ext = jnp.ones((c, LANES), BF16)

    def chunk(ci, carry):
        r0 = pl.multiple_of(ci * c, c)
        q = qk_ref[pl.ds(r0, c), 0:w]
        k = qk_ref[pl.ds(r0, c), w:2 * w] * (HEAD_DIM ** -0.5)
        v = p_ref[pl.ds(r0, c), 2 * w:3 * w]
        op = p_ref[pl.ds(r0, c), 3 * w:4 * w]
        gc = gcol_ref[pl.ds(r0, c), :] + gbr_ref[...]
        gr = grow_ref[ci] + gbc_ref[...]
        b_c = _dot01_l(tri, _log_sigmoid(gc))
        b_r = _dot01_r(_log_sigmoid(gr), tri_t)
        wd, s_inter, em, wk, decay = [], [], [], [], []
        for h in range(N_HEADS):
            bc = b_c[:, N_HEADS + h:N_HEADS + h + 1]
            lic = gc[:, h:h + 1]
            br = b_r[N_HEADS + h:N_HEADS + h + 1, :]
            lir = gr[h:h + 1, :]
            dmat = jnp.where(causal, bc - br + lir, -jnp.inf)
            m_prev = m_ref[h:h + 1, 0:1]
            inter = bc + m_prev
            mrow = jnp.maximum(inter, jnp.max(dmat, axis=1, keepdims=True))
            wd.append(jnp.exp(dmat - mrow))
            s_inter.append(jnp.exp(inter - mrow))
            em.append(jnp.exp(-mrow))
            b_last = br[:, c - 1:c]
            m_new = jnp.maximum(b_last + m_prev, jnp.max(b_last - br + lir, axis=1, keepdims=True))
            wk.append(jnp.exp(b_last - bc + lic - m_new))
            decay.append(jnp.exp(b_last + m_prev - m_new))
            m_ref[h:h + 1, :] = jnp.broadcast_to(m_new, (1, LANES))
        s_inter_l = _expand_heads(s_inter, (c, w))
        em_l = _expand_heads(em, (c, w))
        wk_l = _expand_heads(wk, (c, w))
        decay_l = _expand_heads(decay, (1, w))
        qk = _dot_nt(_stack_heads(q).astype(BF16), k.astype(BF16))
        wmat = jnp.concatenate(wd, axis=0) * qk
        vb = v.astype(BF16)
        r = _dot(wmat.astype(BF16), jnp.concatenate([vb, ones_ext], axis=1))
        num_intra = _unstack_heads(r[:, 0:w], c)
        rs_l = _expand_heads([r[h * c:(h + 1) * c, w:w + 1] for h in range(N_HEADS)], (c, w))
        ct = ct_ref[...]
        nrow = n_ref[0:1, :]
        num = s_inter_l * _dot_nt(q.astype(BF16), ct.astype(BF16)) + num_intra
        den = s_inter_l * _group_sum(q * nrow, ones_bd) + rs_l
        hval = num / jnp.maximum(jnp.abs(den), em_l)
        kw = wk_l * k
        ct_ref[...] = ct * decay_l + bd_mask * _dot_tn(vb, kw.astype(BF16))
        n_ref[0:1, :] = nrow * decay_l + jnp.sum(kw, axis=0, keepdims=True)
        o_ref[pl.ds(r0, c), :] = _head_rms(hval, gain, ones_bd) * _sigmoid(op)
        return carry

    lax.fori_loop(0, tb // c, chunk, 0)


def _mlstm(proj, conv_w, conv_b, gate_b, gain, bsz, seq):
    t = bsz * seq
    w = MIX_WIDTH
    tb = min(REC_BLOCK, seq)
    nb = seq // tb
    ncb = tb // CHUNK
    grow = proj[:, OFF_MG:OFF_MG + 8].reshape(t // CHUNK, CHUNK, 8).transpose(0, 2, 1)
    gb_row = jnp.zeros((1, LANES), F32).at[0, 0:8].set(gate_b.astype(F32))
    gb_col = gate_b.astype(F32).reshape(8, 1)
    return pl.pallas_call(
        _mlstm_kernel,
        grid=(bsz, nb),
        in_specs=[pl.BlockSpec((tb, 4 * w), lambda b, i: (b * nb + i, OFF_M // (4 * w))),
                  pl.BlockSpec((tb, LANES), lambda b, i: (b * nb + i, OFF_MG // LANES)),
                  pl.BlockSpec((ncb, 8, CHUNK), lambda b, i: (b * nb + i, 0, 0)),
                  pl.BlockSpec((CONV_W, 2 * w), lambda b, i: (0, 0)),
                  pl.BlockSpec((1, 2 * w), lambda b, i: (0, 0)),
                  pl.BlockSpec((1, LANES), lambda b, i: (0, 0)),
                  pl.BlockSpec((8, 1), lambda b, i: (0, 0)),
                  pl.BlockSpec((1, w), lambda b, i: (0, 0))],
        out_specs=pl.BlockSpec((tb, w), lambda b, i: (b * nb + i, 0)),
        out_shape=jax.ShapeDtypeStruct((t, w), F32),
        scratch_shapes=[pltpu.VMEM((w, w), F32), pltpu.VMEM((8, w), F32), pltpu.VMEM((8, LANES), F32),
                        pltpu.VMEM((8, 2 * w), F32), pltpu.VMEM((tb + 8, 2 * w), F32),
                        pltpu.VMEM((tb, 2 * w), F32)],
        compiler_params=_cp("parallel", "arbitrary"),
        name="mlstm",
    )(proj, proj, grow, conv_w.astype(F32), conv_b.astype(F32).reshape(1, 2 * w), gb_row, gb_col,
      gain.reshape(1, w))


NSA_TQ = 128
NSA_KC = 512
GW = 2 * HEAD_DIM


def _nsa_prep_kernel(pq_ref, pk_ref, pv_ref, cos_ref, sin_ref, qg_ref, kg_ref,
                     qn_ref, qr_ref, ks_ref, kw_ref, vst_ref, vwt_ref):
    w = MIX_WIDTH
    for src, dst in ((pv_ref[:, 0:GW], vst_ref), (pv_ref[:, GW:2 * GW], vwt_ref)):
        vt = src.T
        tk = dst.shape[4]
        for g in range(NSA_GROUPS):
            rows = vt[g * HEAD_DIM:(g + 1) * HEAD_DIM, :]
            dup = jnp.concatenate([rows, rows], axis=0).astype(BF16)
            for j in range(dst.shape[2]):
                dst[0, g, j] = dup[:, j * tk:(j + 1) * tk]
    ones_bd = _block_ones(w)
    cos_t, sin_t = cos_ref[...], sin_ref[...]
    scale = HEAD_DIM ** -0.5
    qh = _head_rms(pq_ref[...], qg_ref[...], ones_bd)
    qn_ref[...] = (qh * scale).astype(BF16)
    qr_ref[...] = (_rope(qh, cos_t, sin_t) * scale).astype(BF16)
    ks_ref[...] = _rope(_head_rms(pk_ref[:, 0:w], kg_ref[1:2, :], ones_bd), cos_t, sin_t).astype(BF16)
    kw_ref[...] = _rope(_head_rms(pk_ref[:, w:2 * w], kg_ref[2:3, :], ones_bd), cos_t, sin_t).astype(BF16)


def _nsa_prep(proj, cos4, sin4, qnorm_g, knorm_g, bsz, seq):
    t = bsz * seq
    w = MIX_WIDTH
    tm = min(NSA_KC, seq)
    tq = min(NSA_TQ, seq)
    ns = seq // tm
    qg = jnp.tile(qnorm_g.astype(F32), w // HEAD_DIM).reshape(1, w)
    kg = jnp.concatenate([jnp.tile(knorm_g.astype(F32), (1, w // HEAD_DIM)), jnp.zeros((5, w), F32)], axis=0)
    out = jax.ShapeDtypeStruct((t, w), BF16)
    row = pl.BlockSpec((tm, w), lambda i: (i, 0))
    return pl.pallas_call(
        _nsa_prep_kernel,
        grid=(t // tm,),
        in_specs=[pl.BlockSpec((tm, w), lambda i: (i, OFF_NQ // w)),
                  pl.BlockSpec((tm, 2 * w), lambda i: (i, OFF_KD // (2 * w))),
                  pl.BlockSpec((tm, 2 * GW), lambda i: (i, OFF_V // (2 * GW))),
                  pl.BlockSpec((tm, w), lambda i: (i % ns, 0)),
                  pl.BlockSpec((tm, w), lambda i: (i % ns, 0)),
                  pl.BlockSpec((1, w), lambda i: (0, 0)),
                  pl.BlockSpec((8, w), lambda i: (0, 0))],
        out_specs=[row, row, row, row,
                   pl.BlockSpec((1, NSA_GROUPS, 1, GW, tm), lambda i: (i // ns, 0, i % ns, 0, 0)),
                   pl.BlockSpec((1, NSA_GROUPS, tm // tq, GW, tq), lambda i: (i // ns, 0, i % ns, 0, 0))],
        out_shape=[out, out, out, out,
                   jax.ShapeDtypeStruct((bsz, NSA_GROUPS, seq // tm, GW, tm), BF16),
                   jax.ShapeDtypeStruct((bsz, NSA_GROUPS, seq // tq, GW, tq), BF16)],
        compiler_params=_cp("parallel"),
        name="nsa_prep",
    )(proj, proj, proj, cos4, sin4, qg, kg)


def _nsa_cmp_kernel(xr_ref, pe_ref, w0_ref, w1_ref, kg_ref, ovt_ref, qn_ref, ocmp_ref, sel_ref,
                    kc_ref, vc_ref, v_ref, *, n_top):
    tq = qn_ref.shape[0]
    nr = xr_ref.shape[0]
    nsel = sel_ref.shape[2]
    w = MIX_WIDTH

    @pl.when(pl.program_id(1) == 0)
    def _():
        xr = xr_ref[...]
        y0 = _dot((xr + pe_ref[0]).astype(BF16), w0_ref[...])
        y1 = _dot((xr + pe_ref[1]).astype(BF16), w1_ref[...])
        kv = y0 + pltpu.roll(y1, nr - 1, 0)
        kc_ref[...] = _head_rms(kv[:, 0:w], kg_ref[...], _block_ones(w)).astype(BF16)
        vc_ref[...] = kv[:, w:2 * w].astype(BF16)

    pos0 = pl.program_id(1) * tq
    hl = _head_of_lane((tq, GW), 1)
    pos_r = pos0 + lax.broadcasted_iota(jnp.int32, (tq, nr), 0)
    valid = lax.broadcasted_iota(jnp.int32, (tq, nr), 1) * CMP_STRIDE + (CMP_LEN - 1) <= pos_r
    pos_c = pos0 + lax.broadcasted_iota(jnp.int32, (nr, tq), 1)
    valid_t = lax.broadcasted_iota(jnp.int32, (nr, tq), 0) * CMP_STRIDE + (CMP_LEN - 1) <= pos_c
    jrow = lax.broadcasted_iota(jnp.int32, (nsel, tq), 0)
    cur = (pos0 + lax.broadcasted_iota(jnp.int32, (nsel, tq), 1)) // SEL_BLOCK
    forced = (jrow == 0) | (jrow == cur) | (jrow == cur - 1)
    ovt = ovt_ref[...]

    for g in range(NSA_GROUPS):
        qg = qn_ref[:, g * GW:(g + 1) * GW]
        kg = kc_ref[:, g * GW:(g + 1) * GW]
        vg = vc_ref[:, g * GW:(g + 1) * GW]
        o_g = jnp.zeros((tq, GW), F32)
        pt_sum = jnp.zeros((nr, tq), F32)
        for hh in range(2):
            qm = jnp.where(hl == hh, qg, jnp.zeros_like(qg))
            s = jnp.where(valid, _dot_nt(qm, kg), NEG_INF)
            e = jnp.exp(s - jnp.max(s, axis=1, keepdims=True))
            p = jnp.where(valid, e / jnp.sum(e, axis=1, keepdims=True), 0.0)
            o_g = jnp.where(hl == hh, _dot(p.astype(BF16), vg), o_g)
            st = jnp.where(valid_t, _dot_nt(kg, qm), NEG_INF)
            et = jnp.exp(st - jnp.max(st, axis=0, keepdims=True))
            pt_sum = pt_sum + jnp.where(valid_t, et / jnp.sum(et, axis=0, keepdims=True), 0.0)
        ocmp_ref[:, g * GW:(g + 1) * GW] = o_g
        p_hi = pt_sum.astype(BF16)
        p_lo = (pt_sum - p_hi.astype(F32)).astype(BF16)
        imp = _dot(ovt, p_hi) + _dot(ovt, p_lo)
        val = jnp.where(jrow <= cur, imp + FORCE_BONUS * forced.astype(F32), NEG_INF)
        v_ref[...] = val

        def rank(jp, cnt):
            row = v_ref[pl.ds(jp, 1), :]
            tie = jnp.where(jrow > jp, 1.0, 0.0)
            return cnt + jnp.where(row > val, 1.0, jnp.where(row == val, tie, 0.0))

        cnt = lax.fori_loop(0, nsel, rank, jnp.zeros((nsel, tq), F32))
        sel_ref[0, g] = ((cnt < n_top) & (jrow <= cur)).astype(F32)


def _nsa_cmp_weights(cmp_pe, cmp_w):
    half = CMP_LEN // 2
    wl = cmp_w.astype(F32).reshape(2, 2, half, HEAD_DIM, HEAD_DIM)
    eye2 = jnp.eye(2, dtype=F32)
    w2 = jnp.einsum('kardz,kK,gG,h->arkgdKGhz', wl, eye2, eye2, jnp.ones((2,), F32))
    w2 = w2.reshape(2, half * 4 * HEAD_DIM, 8 * HEAD_DIM)
    pl_ = cmp_pe.astype(F32).reshape(2, 2, half, HEAD_DIM)
    pe2 = jnp.broadcast_to(pl_.transpose(1, 2, 0, 3)[:, :, :, None, :], (2, half, 2, 2, HEAD_DIM))
    return w2.astype(BF16), pe2.reshape(2, 1, half * 4 * HEAD_DIM)


def _nsa_cmp(kcvc, qn, cmp_pe, cmp_w, knorm0, bsz, seq, tq=512):
    t = bsz * seq
    w = MIX_WIDTH
    tq = min(tq, seq)
    nq = seq // tq
    nr = seq // CMP_STRIDE
    nsel = seq // SEL_BLOCK
    n_top = min(SEL_TOPK, nsel)
    w2, pe2 = _nsa_cmp_weights(cmp_pe, cmp_w)
    xr = kcvc.reshape(t // CMP_STRIDE, CMP_STRIDE * w)
    kg = jnp.tile(knorm0.astype(F32), w // HEAD_DIM).reshape(1, w)
    n_i = np.arange(nr)[:, None] * CMP_STRIDE
    j_i = np.arange(nsel)[None, :] * SEL_BLOCK
    ov = ((n_i < j_i + SEL_BLOCK) & (n_i + CMP_LEN > j_i)).astype(np.float32)
    ov[nr - 1, :] = 0.0
    kin = CMP_STRIDE * w
    return pl.pallas_call(
        functools.partial(_nsa_cmp_kernel, n_top=n_top),
        grid=(bsz, nq),
        in_specs=[pl.BlockSpec((nr, kin), lambda b, i: (b, 0)),
                  pl.BlockSpec((2, 1, kin), lambda b, i: (0, 0, 0)),
                  pl.BlockSpec((None, kin, 2 * w), lambda b, i: (0, 0, 0)),
                  pl.BlockSpec((None, kin, 2 * w), lambda b, i: (1, 0, 0)),
                  pl.BlockSpec((1, w), lambda b, i: (0, 0)),
                  pl.BlockSpec((nsel, nr), lambda b, i: (0, 0)),
                  pl.BlockSpec((tq, w), lambda b, i: (b * nq + i, 0))],
        out_specs=[pl.BlockSpec((tq, w), lambda b, i: (b * nq + i, 0)),
                   pl.BlockSpec((1, NSA_GROUPS, nsel, tq), lambda b, i: (b, 0, 0, i))],
        out_shape=[jax.ShapeDtypeStruct((t, w), F32),
                   jax.ShapeDtypeStruct((bsz, NSA_GROUPS, nsel, seq), F32)],
        scratch_shapes=[pltpu.VMEM((nr, w), BF16), pltpu.VMEM((nr, w), BF16), pltpu.VMEM((nsel, tq), F32)],
        compiler_params=_cp("parallel", "arbitrary"),
        name="nsa_cmp",
    )(xr, pe2, w2, w2, kg, jnp.asarray(ov.T, BF16), qn)


def _nsa_attn_kernel(qr_ref, ks_ref, kw_ref, vs_ref, vw_ref, sel_ref, ocmp_ref, gate_ref, o_ref, *, kc, wt):
    tq = qr_ref.shape[0]
    i = pl.program_id(2)
    g = pl.program_id(1)
    hl = _head_of_lane((tq, GW), 1)
    q = qr_ref[...]
    qs = jnp.concatenate([jnp.where(hl == 0, q, jnp.zeros_like(q)), jnp.where(hl == 1, q, jnp.zeros_like(q))], axis=0)
    nbk = kc // SEL_BLOCK

    def lane_qpos(rows):
        return i * tq + lax.broadcasted_iota(jnp.int32, (rows, 2 * tq), 1) % tq

    def finish(acc, l):
        ot = (acc / l).T
        return jnp.where(hl == 0, ot[0:tq, :], ot[tq:2 * tq, :])

    qpos_s = lane_qpos(kc)
    krow_s = lax.broadcasted_iota(jnp.int32, (kc, 2 * tq), 0)

    def sel_body(c, carry, diagonal):
        m, l, acc = carry
        k0 = pl.multiple_of(c * kc, kc)
        st = _dot_nt(ks_ref[pl.ds(k0, kc), :], qs)
        srows = sel_ref[0, 0, pl.ds(pl.multiple_of(c * nbk, nbk), nbk), :]
        srows = jnp.concatenate([srows, srows], axis=1)
        smask = jnp.concatenate([jnp.broadcast_to(srows[r:r + 1, :], (SEL_BLOCK, 2 * tq)) for r in range(nbk)],
                                axis=0)
        msk = smask > 0.5
        if diagonal:
            msk = msk & (k0 + krow_s <= qpos_s)
        st = jnp.where(msk, st, NEG_INF)
        m_new = jnp.maximum(m, jnp.max(st, axis=0, keepdims=True))
        p = jnp.exp(st - m_new)
        alpha = jnp.exp(m - m_new)
        l = l * alpha + jnp.sum(p, axis=0, keepdims=True)
        acc = acc * alpha + _dot(vs_ref[0, 0, c], p.astype(BF16))
        return m_new, l, acc

    init = (jnp.full((1, 2 * tq), NEG_INF, F32), jnp.zeros((1, 2 * tq), F32), jnp.zeros((GW, 2 * tq), F32))
    n_before = (i * tq) // kc
    carry = lax.fori_loop(0, n_before, functools.partial(sel_body, diagonal=False), init)
    _, l_s, acc_s = sel_body(n_before, carry, True)
    o_sel = finish(acc_s, l_s)

    j0 = jnp.maximum(i - (wt - 1), 0)
    k0 = pl.multiple_of(j0 * tq, tq)
    span = wt * tq
    st = _dot_nt(kw_ref[pl.ds(k0, span), :], qs)
    kpos = k0 + lax.broadcasted_iota(jnp.int32, (span, 2 * tq), 0)
    qpos_w = lane_qpos(span)
    msk = (kpos <= qpos_w) & (kpos > qpos_w - WINDOW)
    st = jnp.where(msk, st, NEG_INF)
    p = jnp.exp(st - jnp.max(st, axis=0, keepdims=True))
    vt =jnp.concatenate([vw_ref[0, 0, j0 + r] for r in range(wt)], axis=1)
    o_win = finish(_dot(vt, p.astype(BF16)), jnp.sum(p, axis=0, keepdims=True))

    gb = _sigmoid(gate_ref[...])

    def gate(branch):
        cols = []
        for hh in range(2):
            c0 = gb[:, hh * 3 + branch:hh * 3 + branch + 1]
            c1 = gb[:, (2 + hh) * 3 + branch:(2 + hh) * 3 + branch + 1]
            cols.append(jnp.where(g == 0, c0, c1))
        return _expand_heads(cols, (tq, GW))

    o_ref[...] = gate(0) * ocmp_ref[...] + gate(1) * o_sel + gate(2) * o_win


def _nsa_attn(proj, qr, ks, kw, vst, vwt, sel, o_cmp, bsz, seq):
    t = bsz * seq
    w = MIX_WIDTH
    tq = min(NSA_TQ, seq)
    nq = seq // tq
    nsel = seq // SEL_BLOCK
    kc = min(NSA_KC, seq)
    wt = min(WINDOW // tq + 1, nq)
    kspec = pl.BlockSpec((seq, GW), lambda b, g, i: (b, g))
    return pl.pallas_call(
        functools.partial(_nsa_attn_kernel, kc=kc, wt=wt),
        grid=(bsz, NSA_GROUPS, nq),
        in_specs=[pl.BlockSpec((tq, GW), lambda b, g, i: (b * nq + i, g)),
                  kspec, kspec,
                  pl.BlockSpec((1, 1, seq // kc, GW, kc), lambda b, g, i: (b, g, 0, 0, 0)),
                  pl.BlockSpec((1, 1, nq, GW, tq), lambda b, g, i: (b, g, 0, 0, 0)),
                  pl.BlockSpec((1, 1, nsel, tq), lambda b, g, i: (b, g, 0, i)),
                  pl.BlockSpec((tq, GW), lambda b, g, i: (b * nq + i, g)),
                  pl.BlockSpec((tq, LANES), lambda b, g, i: (b * nq + i, OFF_NG // LANES))],
        out_specs=pl.BlockSpec((tq, GW), lambda b, g, i: (b * nq + i, g)),
        out_shape=jax.ShapeDtypeStruct((t, w), F32),
        compiler_params=_cp("parallel", "parallel", "arbitrary"),
        name="nsa_attn",
    )(qr, ks, kw, vst, vwt, sel, o_cmp, proj)


def _nsa(proj, kcvc, cos4, sin4, qnorm_g, knorm_g, cmp_pe, cmp_w, bsz, seq):
    qn, qr, ks, kw, vst, vwt = _nsa_prep(proj, cos4, sin4, qnorm_g, knorm_g, bsz, seq)
    o_cmp, sel = _nsa_cmp(kcvc, qn, cmp_pe, cmp_w, knorm_g[0], bsz, seq)
    return _nsa_attn(proj, qr, ks, kw, vst, vwt, sel, o_cmp, bsz, seq)


PEER_TT = 128
PEER_CT = 8
HALF_D = 512


SUBLANES = 8
CODE_BITS = 127
FAR_BELOW = -3.0e38


def _with_code(x, code):
    bits = lax.bitcast_convert_type(x, jnp.int32)
    return lax.bitcast_convert_type((bits & ~CODE_BITS) | code, F32)


def _split_code(x):
    bits = lax.bitcast_convert_type(x, jnp.int32)
    return lax.bitcast_convert_type(bits & ~CODE_BITS, F32), bits & CODE_BITS


def _sort16_desc(xs):
    xs = list(xs)
    n = len(xs)
    k = 2
    while k <= n:
        j = k // 2
        while j >= 1:
            for i in range(n):
                l = i ^ j
                if l > i:
                    hi, lo = jnp.maximum(xs[i], xs[l]), jnp.minimum(xs[i], xs[l])
                    xs[i], xs[l] = (hi, lo) if (i & k) == 0 else (lo, hi)
            j //= 2
        k *= 2
    return xs


def _merge16_desc(xs):
    xs = list(xs)
    j = len(xs) // 2
    while j >= 1:
        for i in range(len(xs)):
            l = i ^ j
            if l > i:
                xs[i], xs[l] = jnp.maximum(xs[i], xs[l]), jnp.minimum(xs[i], xs[l])
        j //= 2
    return xs


def _top16_columns(x):
    n = PEER_TOPK
    xs = _sort16_desc([x[SUBLANES * j:SUBLANES * (j + 1), :] for j in range(n)])
    shift = SUBLANES // 2
    while shift >= 1:
        rolled = [pltpu.roll(a, shift, 0) for a in xs]
        xs = _merge16_desc([jnp.maximum(xs[i], rolled[n - 1 - i]) for i in range(n)])
        shift //= 2
    return xs


_PEER_CAND_TILES = ((0, 0, 8), (0, 1, 8), (1, 0, 8), (2, 0, 5), (3, 0, 4), (4, 0, 3), (5, 0, 2), (6, 0, 2), (7, 0, 2))


ROUTE_HEADS_PER_STEP = 4


def _route_head(q_ref, key_ref, hh):
    tt = q_ref.shape[0]
    nk = PEER_NKEYS
    n = PEER_TOPK
    row = lax.broadcasted_iota(jnp.int32, (nk, tt), 0)
    sub = lax.broadcasted_iota(jnp.int32, (SUBLANES, tt), 0)
    vals, ids = [], []
    for p in range(2):
        c0 = (2 * hh + p) * PEER_KDIM
        st = _dot_nt(key_ref[hh, p], q_ref[:, c0:c0 + PEER_KDIM])
        top = [_split_code(a) for a in _top16_columns(_with_code(st, (nk - 1) - row))]
        vals.append([v for v, _ in top])
        ids.append([(nk - 1) - c for _, c in top])
    (v1, v2), (i1, i2) = vals, ids

    def stack(xs, lo):
        out = xs[lo]
        for s in range(1, SUBLANES):
            out = jnp.where(sub == s, xs[lo + s], out)
        return out

    v2t, i2t = (stack(v2, 0), stack(v2, SUBLANES)), (stack(i2, 0), stack(i2, SUBLANES))
    cand, cexp = [], []
    for a, tile, nvalid in _PEER_CAND_TILES:
        v = v1[a] + v2t[tile]
        cand.append(v if nvalid == SUBLANES else jnp.where(sub < nvalid, v, FAR_BELOW))
        cexp.append(i1[a] * nk + i2t[tile])
    cand.append(stack(v1, SUBLANES) + v2[0])
    cexp.append(stack(i1, SUBLANES) * nk + i2[0])
    n_tiles = len(cand)
    slot_code = [(nk - 1) - (c * SUBLANES + sub) for c in range(n_tiles)]
    coded = [_with_code(v, sc) for v, sc in zip(cand, slot_code)]
    coded += [jnp.full((SUBLANES, tt), FAR_BELOW, F32)] * (n - n_tiles)
    top = [_split_code(a) for a in _top16_columns(jnp.concatenate(coded, axis=0))]
    call = jnp.concatenate(cexp, axis=0)
    slot = (nk - 1) - lax.broadcasted_iota(jnp.int32, call.shape, 0)
    ex = [jnp.exp(v - top[0][0]) for v, _ in top]
    tot = ex[0]
    for k in range(1, n):
        tot = tot + ex[k]
    krow = lax.broadcasted_iota(jnp.int32, (n, tt), 0)
    e_tile = jnp.zeros((n, tt), F32)
    g_tile = jnp.zeros((n, tt), F32)
    for k in range(n):
        hit = slot == jnp.concatenate([top[k][1]] * n_tiles, axis=0)
        e_k = jnp.sum(jnp.where(hit, call, 0), axis=0, keepdims=True)
        e_tile = jnp.where(krow == k, e_k.astype(F32), e_tile)
        g_tile = jnp.where(krow == k, (ex[k] / tot)[0:1, :], g_tile)
    return e_tile, g_tile


def _peer_route_kernel(q_ref, key_ref, e_ref, g_ref, e_scr, g_scr):
    hps = key_ref.shape[0]
    tiles = [_route_head(q_ref, key_ref, hh) for hh in range(hps)]
    rows = hps * PEER_TOPK
    r0 = pl.multiple_of(pl.program_id(1) * rows, rows)
    e_scr[pl.ds(r0, rows), :] = jnp.concatenate([e for e, _ in tiles], axis=0)
    g_scr[pl.ds(r0, rows), :] = jnp.concatenate([g for _, g in tiles], axis=0)

    @pl.when(pl.program_id(1) == pl.num_programs(1) - 1)
    def _():
        e_ref[...] = e_scr[...].T.astype(jnp.int32)
        g_ref[...] = g_scr[...].T


def _peer_route(qp, keys):
    t = qp.shape[0]
    tt = min(PEER_TT, t)
    ne = PEER_HEADS * PEER_TOPK
    hps = ROUTE_HEADS_PER_STEP
    return pl.pallas_call(
        _peer_route_kernel,
        grid=(t // tt, PEER_HEADS // hps),
        in_specs=[pl.BlockSpec((tt, hps * 2 * PEER_KDIM), lambda i, h: (i, h)),
                  pl.BlockSpec((hps, 2, PEER_NKEYS, PEER_KDIM), lambda i, h: (h, 0, 0, 0))],
        out_specs=[pl.BlockSpec((tt, ne), lambda i, h: (i, 0)),
                   pl.BlockSpec((tt, ne), lambda i, h: (i, 0))],
        out_shape=[jax.ShapeDtypeStruct((t, ne), jnp.int32),
                   jax.ShapeDtypeStruct((t, ne), F32)],
        scratch_shapes=[pltpu.VMEM((ne, tt), F32), pltpu.VMEM((ne, tt), F32)],
        compiler_params=_cp("parallel", "arbitrary"),
        name="peer_route",
    )(qp, keys)


def _pack_tables_kernel(u_ref, v_ref, o_ref):
    def pack(x):
        lo = lax.bitcast_convert_type(x[:, 0:HALF_D].astype(BF16).astype(F32), jnp.int32)
        hi = lax.bitcast_convert_type(x[:, HALF_D:2 * HALF_D].astype(BF16).astype(F32), jnp.int32)
        return lax.shift_right_logical(lo, 16) | (hi & jnp.int32(-65536))

    o_ref[:, 0:HALF_D] = pack(u_ref[...])
    o_ref[:, HALF_D:2 * HALF_D] = pack(v_ref[...])


def _pack_tables(u_tabs, v_tabs, layer, tr=512):
    _, e, d = u_tabs.shape
    assert d == 2 * HALF_D
    spec_in = pl.BlockSpec((None, tr, d), lambda i: (layer, i, 0))
    spec = pl.BlockSpec((tr, d), lambda i: (i, 0))
    return pl.pallas_call(
        _pack_tables_kernel,
        grid=(e // tr,),
        in_specs=[spec_in, spec_in],
        out_specs=spec,
        out_shape=jax.ShapeDtypeStruct((e, d), jnp.int32),
        compiler_params=_cp("parallel"),
        name="peer_pack",
    )(u_tabs, v_tabs)


def _unpack_rows(wd):
    lo = lax.bitcast_convert_type(lax.shift_left(wd, 16), F32)
    hi = lax.bitcast_convert_type(lax.bitwise_and(wd, jnp.int32(-65536)), F32)
    return lo, hi


SC_WINDOW = 32


def _sc_gather(table, idx):
    from jax.experimental.pallas import tpu_sc as plsc
    n = idx.shape[0]
    width = table.shape[1]
    mesh = plsc.VectorSubcoreMesh(core_axis_name="core", subcore_axis_name="subcore")

    @functools.partial(pl.kernel, out_type=jax.ShapeDtypeStruct((n, width), table.dtype), mesh=mesh)
    def gather(tab_hbm, idx_hbm, out_hbm):
        def body(idx_vmem, out_vmem):
            pltpu.sync_copy(tab_hbm.at[idx_vmem.at[0, pl.ds(0, SC_WINDOW)]], out_vmem)

        pltpu.emit_pipeline(
            body,
            grid=(n // SC_WINDOW,),
            in_specs=[pl.BlockSpec((1, LANES), lambda i: (0, i))],
            out_specs=[pl.BlockSpec((SC_WINDOW, width), lambda i: (i, 0))],
            core_axis_name=("core", "subcore"),
            dimension_semantics=(pltpu.PARALLEL,),
            trace_scopes=False,
        )(idx_hbm, out_hbm)

    idx_pad = jnp.pad(idx.reshape(n // SC_WINDOW, SC_WINDOW), ((0, 0), (0, LANES - SC_WINDOW)))
    return gather(table, idx_pad.reshape(1, (n // SC_WINDOW) * LANES))


def _peer_combine_kernel(x_ref, g2_ref, rows_a_ref, rows_b_ref, gate_ref, o_ref):
    ne = PEER_HEADS * PEER_TOPK
    x = x_ref[...]
    ct = x.shape[0]
    xn = x * lax.rsqrt(jnp.mean(x * x, axis=-1, keepdims=True) + NORM_EPS) * g2_ref[...]
    gate_t = jnp.concatenate([gate_ref[...]] * (ne // ct), axis=0).T
    for jj in range(ct):
        rows_ref, j = (rows_a_ref, jj) if jj < ct // 2 else (rows_b_ref, jj - ct // 2)
        u_lo, u_hi = _unpack_rows(rows_ref[j * ne:(j + 1) * ne, 0:HALF_D])
        xr = xn[jj:jj + 1, :]
        h = jnp.sum(u_lo * xr[:, 0:HALF_D] + u_hi * xr[:, HALF_D:2 * HALF_D], axis=1, keepdims=True)
        act = 0.5 * h * (1.0 + lax.erf(h * (2.0 ** -0.5)))
        wgt = gate_t[:, jj:jj + 1] * act
        v_lo, v_hi = _unpack_rows(rows_ref[j * ne:(j + 1) * ne, HALF_D:2 * HALF_D])
        o_ref[jj:jj + 1, 0:HALF_D] = x[jj:jj + 1, 0:HALF_D] + jnp.sum(wgt * v_lo, axis=0, keepdims=True)
        o_ref[jj:jj + 1, HALF_D:2 * HALF_D] = (x[jj:jj + 1, HALF_D:2 * HALF_D]
                                               + jnp.sum(wgt * v_hi, axis=0, keepdims=True))


def _peer_combine(x, g2, rows, gates, first_token):
    t, d = x.shape
    ne = PEER_HEADS * PEER_TOPK
    ct = PEER_CT
    steps = rows.shape[0] // (ct * ne)
    off = first_token // ct
    return pl.pallas_call(
        _peer_combine_kernel,
        grid=(steps,),
        in_specs=[pl.BlockSpec((ct, d), lambda i: (off + i, 0)),
                  pl.BlockSpec((1, d), lambda i: (0, 0)),
                  pl.BlockSpec((ct * ne // 2, d), lambda i: (2 * i, 0)),
                  pl.BlockSpec((ct * ne // 2, d), lambda i: (2 * i + 1, 0)),
                  pl.BlockSpec((ct, ne), lambda i: (off + i, 0))],
        out_specs=pl.BlockSpec((ct, d), lambda i: (off + i, 0)),
        out_shape=jax.ShapeDtypeStruct((t, d), F32),
        input_output_aliases={0: 0},
        compiler_params=_cp("parallel"),
        name="peer_combine",
    )(x, g2.reshape(1, d), rows, rows, gates)


PEER_TOKENS_PER_GATHER = 2048


def _peer_route_stage(x, g2, wq_b, keys_b):
    t = x.shape[0]
    ne = PEER_HEADS * PEER_TOPK
    qp = _norm_matmul(x, g2, wq_b, out_dtype=BF16)
    e_tok, g_tok = _peer_route(qp, keys_b)
    return e_tok.reshape(t * ne), g_tok


def _peer_gather_stage(table, idx, t, gather_fn):
    ne = PEER_HEADS * PEER_TOPK
    tc = min(PEER_TOKENS_PER_GATHER, t)
    return [gather_fn(table, idx[c * tc * ne:(c + 1) * tc * ne]) for c in range(t // tc)]


def _peer_combine_stage(x, g2, rows_list, gates):
    tc = x.shape[0] // len(rows_list)
    for c, rows in enumerate(rows_list):
        x = _peer_combine(x, g2, rows, gates, c * tc)
    return x


def _peer(x, g2, wq, keys, u_tabs, v_tabs, layer, gather_fn):
    idx, gates = _peer_route_stage(x, g2, wq.astype(BF16), keys.astype(BF16))
    table = _pack_tables(u_tabs, v_tabs, layer)
    rows_list = _peer_gather_stage(table, idx, x.shape[0], gather_fn)
    return _peer_combine_stage(x, g2, rows_list, gates)


_IN_WIDTHS = (256, 256, 256, 256, 256, 256, 256, 4, 4, 256, 256, 128, 128, 128, 128, 128, 128, 12,
              256, 256, 256, 256)


def _dup_groups(wcols):
    g0, g1 = wcols[:, :HEAD_DIM], wcols[:, HEAD_DIM:]
    return jnp.concatenate([g0, g0, g1, g1], axis=1)


def _layout_w_in(w_in):
    offs = np.cumsum((0,) + _IN_WIDTHS)
    cols = [w_in[:, offs[i]:offs[i + 1]] for i in range(len(_IN_WIDTHS))]
    (hq, hf, hi, hg, mq, mk, mv, mi, mf, mo, nq, nkc, nvc, nks, nvs, nkw, nvw, ng, rq, rk, rv, rg) = cols
    d = w_in.shape[0]
    pad = lambda c, n: jnp.concatenate([c, jnp.zeros((d, n - c.shape[1]), w_in.dtype)], axis=1)
    main = jnp.concatenate([hq, hf, hi, hg, mq, mk, mv, mo, rq, rk, rv, rg,
                            _dup_groups(nks), _dup_groups(nkw), nq, nvs, nvw,
                            pad(jnp.concatenate([mi, mf], axis=1), LANES), pad(ng, LANES)], axis=1)
    assert main.shape[1] == N_MAIN
    kcvc = jnp.concatenate([nkc, nvc], axis=1)
    return main.astype(BF16), kcvc.astype(BF16)


def kernel(x, norm1_g, w_in, hgrn_lb, hgrn_onorm_g, mlstm_conv_w, mlstm_conv_b, mlstm_gate_b, mlstm_onorm_g, nsa_qnorm_g, nsa_knorm_g, nsa_cmp_pe, nsa_cmp_w, ret_onorm_g, w_up, w_gate, w_out, norm2_g, peer_wq, peer_keys, peer_u, peer_v):
    bsz, seq, d = x.shape
    t = bsz * seq
    depth = w_in.shape[0]
    cos_t, sin_t = _rope_lane_tables(seq)
    cos4, sin4 = jnp.tile(cos_t, (1, 2)), jnp.tile(sin_t, (1, 2))
    lb_cum = jnp.cumsum(jax.nn.softmax(hgrn_lb.astype(F32), axis=0), axis=0)
    lb_all = lb_cum - lb_cum[0:1]
    weights = []
    for l in range(depth):
        w_main, w_kcvc = _layout_w_in(w_in[l])
        weights.append(dict(
            main=w_main, kcvc=w_kcvc, gate=w_gate[l].astype(BF16), up=w_up[l].astype(BF16),
            out=w_out[l].astype(BF16), lb=_hgrn_lb_rows(lb_all[l]), wq=peer_wq[l].astype(BF16),
            keys=peer_keys[l].astype(BF16), table=_pack_tables(peer_u, peer_v, l)))

    def mixer_steps(xh, l, nb):
        wl = weights[l]
        st = {}

        def proj(dep):
            st["proj"] = _norm_matmul(xh, _after(norm1_g[l], dep), wl["main"])
            return st["proj"]

        def gates(dep):
            st["gates"] = _norm_matmul(xh, _after(norm1_g[l], dep), wl["gate"], act="sigmoid", out_dtype=BF16)
            return st["gates"]

        def hgrn(dep):
            st["oh"] = _hgrn(st["proj"], wl["lb"], _after(hgrn_onorm_g[l], dep), nb, seq)
            return st["oh"]

        def mlstm(dep):
            st["om"] = _mlstm(st["proj"], mlstm_conv_w[l], mlstm_conv_b[l], mlstm_gate_b[l],
                              _after(mlstm_onorm_g[l], dep), nb, seq)
            return st["om"]

        def ret(dep):
            st["or"] = _ret(st["proj"], cos4, sin4, _after(ret_onorm_g[l], dep), nb, seq)
            return st["or"]

        def nsa_front(dep):
            kcvc = _norm_matmul(xh, _after(norm1_g[l], dep), wl["kcvc"])
            qn, qr, ks, kw, vst, vwt = _nsa_prep(st["proj"], cos4, sin4, nsa_qnorm_g[l], nsa_knorm_g[l], nb, seq)
            o_cmp, sel = _nsa_cmp(kcvc, qn, nsa_cmp_pe[l], nsa_cmp_w[l], nsa_knorm_g[l][0], nb, seq)
            st["nsa"] = (qr, ks, kw, vst, vwt, sel, o_cmp)
            return o_cmp

        def nsa_attn(dep):
            del dep
            st["on"] = _nsa_attn(st["proj"], *st["nsa"], nb, seq)
            return st["on"]

        def merge(dep):
            del dep
            st["xm"] = _merge(xh, st["gates"], (st["oh"], st["om"], st["on"], st["or"]), wl["up"], wl["out"])
            return st["xm"]

        def route(dep):
            st["idx"], st["pgates"] = _peer_route_stage(st["xm"], _after(norm2_g[l], dep), wl["wq"], wl["keys"])
            return st["pgates"]

        return [proj, gates, hgrn, mlstm, ret, nsa_front, nsa_attn, merge, route], st

    combine_slots = (0, 8, 8, 8, 8, 8, 8, 8)

    def combine_steps(l, xm, rows_list, pgates):
        box = {"x": xm}
        tc = xm.shape[0] // len(rows_list)

        def make(c):
            def step(dep):
                box["x"] = _peer_combine(box["x"], _after(norm2_g[l], dep), rows_list[c], pgates, c * tc)
                return box["x"]
            return step

        return [make(c) for c in range(len(rows_list))], box

    n_groups = next(n for n in (4, 2, 1) if bsz % n == 0)
    nb = bsz // n_groups
    xs = [x[g * nb:(g + 1) * nb].reshape(nb * seq, d) for g in range(n_groups)]
    dep = None
    lag = min(2, n_groups - 1)
    pending = []
    for l in range(depth):
        for g in range(n_groups):
            msteps, st = mixer_steps(xs[g], l, nb)
            due = pending.pop(0) if len(pending) == lag and lag > 0 else None
            csteps = due[1] if due is not None else []
            ci = 0
            for si, mstep in enumerate(msteps):
                dep = mstep(dep)
                while ci < len(csteps) and (ci >= len(combine_slots) or combine_slots[ci] <= si):
                    dep = csteps[ci](dep)
                    ci += 1
            for cstep in csteps[ci:]:
                dep = cstep(dep)
            if due is not None:
                xs[due[0]] = due[2]["x"]
            rows_list = _peer_gather_stage(weights[l]["table"], st["idx"], nb * seq, _sc_gather)
            csteps, box = combine_steps(l, st["xm"], rows_list, st["pgates"])
            pending.append((g, csteps, box))
            if lag == 0:
                for cstep in pending.pop(0)[1]:
                    dep = cstep(dep)
                xs[g] = box["x"]
    for pg, csteps, box in pending:
        for cstep in csteps:
            dep = cstep(dep)
        xs[pg] = box["x"]
    return jnp.concatenate(xs, axis=0).reshape(bsz, seq, d)
```

```python
import functools
import math

import numpy as np
import jax
import jax.numpy as jnp
from jax import lax
from jax.experimental import pallas as pl
from jax.experimental.pallas import tpu as pltpu

F32 = jnp.float32
BF16 = jnp.bfloat16

HEAD_DIM = 64
N_HEADS = 4
MIX_WIDTH = N_HEADS * HEAD_DIM
CHUNK = 64
NORM_EPS = 1e-6
NEG_INF = -1e30
ROPE_THETA = 10000.0
CONV_W = 4
NSA_GROUPS = 2
CMP_LEN = 32
CMP_STRIDE = 16
SEL_BLOCK = 64
SEL_TOPK = 16
WINDOW = 512
FORCE_BONUS = 1e3
PEER_HEADS = 8
PEER_NKEYS = 128
PEER_TOPK = 16
PEER_KDIM = 128

LANES = 128
VMEM_LIMIT = 48 * 1024 * 1024

OFF_H, OFF_M, OFF_R, OFF_KD, OFF_NQ, OFF_V, OFF_MG, OFF_NG = 0, 1024, 2048, 3072, 3584, 3840, 4096, 4224
N_MAIN = 4352


def _cp(*sem):
    return pltpu.CompilerParams(dimension_semantics=sem, vmem_limit_bytes=VMEM_LIMIT)


def _dot(a, b):
    return jnp.dot(a, b, preferred_element_type=F32)


def _dot_nt(a, b):
    return lax.dot_general(a, b, (((1,), (1,)), ((), ())), preferred_element_type=F32)


def _dot_tn(a, b):
    return lax.dot_general(a, b, (((0,), (0,)), ((), ())), preferred_element_type=F32)


def _split3(x):
    hi = x.astype(BF16)
    r1 = x - hi.astype(F32)
    mid = r1.astype(BF16)
    lo = (r1 - mid.astype(F32)).astype(BF16)
    return hi, mid, lo


def _dot01_l(m01, x):
    hi, mid, lo = _split3(x)
    return _dot(m01, hi) + _dot(m01, mid) + _dot(m01, lo)


def _dot01_r(x, m01):
    hi, mid, lo = _split3(x)
    return _dot(hi, m01) + _dot(mid, m01) + _dot(lo, m01)


def _head_of_lane(shape, axis):
    return lax.broadcasted_iota(jnp.int32, shape, axis) // HEAD_DIM


def _block_ones(n, dtype=BF16):
    r = lax.broadcasted_iota(jnp.int32, (n, n), 0) // HEAD_DIM
    c = lax.broadcasted_iota(jnp.int32, (n, n), 1) // HEAD_DIM
    return (r == c).astype(dtype)


def _group_sum(x, ones_bd):
    hi = x.astype(BF16)
    lo = (x - hi.astype(F32)).astype(BF16)
    return _dot(hi, ones_bd) + _dot(lo, ones_bd)


def _head_rms(x, gain, ones_bd):
    ms = _group_sum(x * x, ones_bd) * (1.0 / HEAD_DIM)
    return x * lax.rsqrt(ms + NORM_EPS) * gain


def _sigmoid(x):
    return 1.0 / (1.0 + jnp.exp(-x))


def _silu(x):
    return x * _sigmoid(x)


def _log_sigmoid(x):
    return jnp.minimum(x, 0.0) - jnp.log(1.0 + jnp.exp(-jnp.abs(x)))


def _stack_heads(x, n_heads=N_HEADS):
    hl = _head_of_lane(x.shape, 1)
    return jnp.concatenate([jnp.where(hl == h, x, jnp.zeros_like(x)) for h in range(n_heads)], axis=0)


def _unstack_heads(r, c, n_heads=N_HEADS):
    hl = _head_of_lane((c, r.shape[1]), 1)
    out = jnp.zeros((c, r.shape[1]), F32)
    for h in range(n_heads):
        out = jnp.where(hl == h, r[h * c:(h + 1) * c, :], out)
    return out


def _rope(x, cos_t, sin_t):
    n = x.shape[1]
    first = (lax.broadcasted_iota(jnp.int32, x.shape, 1) % HEAD_DIM) < (HEAD_DIM // 2)
    partner = jnp.where(first, pltpu.roll(x, n - HEAD_DIM // 2, 1), pltpu.roll(x, HEAD_DIM // 2, 1))
    return x * cos_t + partner * sin_t


def _after_kernel(a_ref, dep_ref, o_ref):
    del dep_ref
    o_ref[...] = a_ref[...]


def _after(a, dep):
    if dep is None:
        return a
    a2 = a.reshape(1, a.size)
    out = pl.pallas_call(
        _after_kernel,
        in_specs=[pl.BlockSpec(a2.shape, lambda: (0, 0)), pl.BlockSpec(memory_space=pl.ANY)],
        out_specs=pl.BlockSpec(a2.shape, lambda: (0, 0)),
        out_shape=jax.ShapeDtypeStruct(a2.shape, a2.dtype),
        name="order_after",
    )(a2, dep)
    return out.reshape(a.shape)


def _norm_matmul_kernel(x_ref, g_ref, w_ref, o_ref, xn_ref, *, act):
    @pl.when(pl.program_id(1) == 0)
    def _():
        x = x_ref[...]
        ms = jnp.mean(x * x, axis=-1, keepdims=True)
        xn_ref[...] = (x * lax.rsqrt(ms + NORM_EPS) * g_ref[...]).astype(BF16)

    y = _dot(xn_ref[...], w_ref[...])
    if act == "sigmoid":
        y = _sigmoid(y)
    o_ref[...] = y.astype(o_ref.dtype)


def _norm_matmul(x, g, w, *, act=None, out_dtype=F32, tm=1024, tn=2176):
    t, d = x.shape
    w3 = w if w.ndim == 3 else w[None]
    n_per = w3.shape[2]
    tm = min(tm, t)
    tn = next(c for c in (tn, 2048, 1024, 512, 256, 128) if n_per % c == 0)
    per = n_per // tn
    n = w3.shape[0] * n_per
    assert t % tm == 0
    return pl.pallas_call(
        functools.partial(_norm_matmul_kernel, act=act),
        grid=(t // tm, n // tn),
        in_specs=[pl.BlockSpec((tm, d), lambda i, j: (i, 0)),
                  pl.BlockSpec((1, d), lambda i, j: (0, 0)),
                  pl.BlockSpec((None, d, tn), lambda i, j: (j // per, 0, j % per))],
        out_specs=pl.BlockSpec((tm, tn), lambda i, j: (i, j)),
        out_shape=jax.ShapeDtypeStruct((t, n), out_dtype),
        scratch_shapes=[pltpu.VMEM((tm, d), BF16)],
        compiler_params=_cp("parallel", "arbitrary"),
        name="norm_matmul",
    )(x, g.reshape(1, d), w3)


def _merge_kernel(x_ref, gate_ref, oh_ref, om_ref, on_ref, or_ref, wup_ref, wout_ref, o_ref):
    d = x_ref.shape[1]
    acc = None
    for m, r in enumerate((oh_ref, om_ref, on_ref, or_ref)):
        up = _dot(r[...].astype(BF16), wup_ref[m])
        term = gate_ref[:, m * d:(m + 1) * d].astype(F32) * up
        acc = term if acc is None else acc + term
    o_ref[...] = x_ref[...] + _dot(acc.astype(BF16), wout_ref[...])


def _merge(x, gates, outs, w_up, w_out, tm=512):
    t, d = x.shape
    tm = min(tm, t)
    mix = pl.BlockSpec((tm, MIX_WIDTH), lambda i: (i, 0))
    return pl.pallas_call(
        _merge_kernel,
        grid=(t // tm,),
        in_specs=[pl.BlockSpec((tm, d), lambda i: (i, 0)),
                  pl.BlockSpec((tm, 4 * d), lambda i: (i, 0)),
                  mix, mix, mix, mix,
                  pl.BlockSpec((4, MIX_WIDTH, d), lambda i: (0, 0, 0)),
                  pl.BlockSpec((d, d), lambda i: (0, 0))],
        out_specs=pl.BlockSpec((tm, d), lambda i: (i, 0)),
        out_shape=jax.ShapeDtypeStruct((t, d), F32),
        compiler_params=_cp("parallel"),
        name="merge",
    )(x, gates, *outs, w_up, w_out)


REC_BLOCK = 256


def _chunk_consts():
    t = lax.broadcasted_iota(jnp.int32, (CHUNK, CHUNK), 0)
    s = lax.broadcasted_iota(jnp.int32, (CHUNK, CHUNK), 1)
    return t, s


def _hgrn_levels():
    t = np.arange(CHUNK)
    rows = []
    masks = []
    h = CHUNK // 2
    while h >= 1:
        ref = (t // (2 * h)) * (2 * h) + h
        p = np.zeros((CHUNK, CHUNK), np.float32)
        p[t, np.minimum(ref, CHUNK - 1)] = 1.0
        rows.append(p)
        same = (t[:, None] // (2 * h)) == (t[None, :] // (2 * h))
        m = same & ((t[:, None] // h) % 2 == 1) & ((t[None, :] // h) % 2 == 0)
        masks.append(m.astype(np.float32))
        h //= 2
    masks.append(np.eye(CHUNK, dtype=np.float32))
    return np.concatenate(rows, 0), np.stack(masks, 0)


def _hgrn_kernel(p_ref, lb_ref, g_ref, psel_ref, lmask_ref, o_ref, st_ref):
    @pl.when(pl.program_id(1) == 0)
    def _():
        st_ref[...] = jnp.zeros_like(st_ref)

    c = CHUNK
    w = MIX_WIDTH
    ones_bd = _block_ones(w)
    bd_mask = _block_ones(w, F32)
    tri = (lax.broadcasted_iota(jnp.int32, (c, c), 0) >= lax.broadcasted_iota(jnp.int32, (c, c), 1)).astype(BF16)
    psel = psel_ref[...]
    n_lv = lmask_ref.shape[0]
    log_lb, log_1mlb, one_mlb = lb_ref[0:1, :], lb_ref[1:2, :], lb_ref[2:3, :]
    gain = g_ref[...]

    def chunk(ci, carry):
        r0 = pl.multiple_of(ci * c, c)
        q = _silu(p_ref[pl.ds(r0, c), 0:w])
        fl = p_ref[pl.ds(r0, c), w:2 * w]
        v = p_ref[pl.ds(r0, c), 2 * w:3 * w]
        gp = p_ref[pl.ds(r0, c), 3 * w:4 * w]
        a1 = jnp.broadcast_to(log_lb, fl.shape)
        a2 = log_1mlb + _log_sigmoid(fl)
        mx = jnp.maximum(a1, a2)
        log_f = mx + jnp.log(jnp.exp(a1 - mx) + jnp.exp(a2 - mx))
        k = one_mlb * _sigmoid(-fl)
        b = _dot01_l(tri, log_f)
        bref = _dot01_l(psel, b)
        vb = v.astype(BF16)
        a = jnp.zeros((N_HEADS * c, c), F32)
        for lv in range(n_lv):
            if lv < n_lv - 1:
                br = bref[lv * c:(lv + 1) * c, :]
                qs = q * jnp.exp(jnp.minimum(b - br, 0.0))
                ks = k * jnp.exp(jnp.minimum(br - b, 0.0))
            else:
                qs, ks = q, k
            s_lv = _dot_nt(_stack_heads(qs).astype(BF16), ks.astype(BF16))
            a = a + jnp.concatenate([lmask_ref[lv]] * N_HEADS, axis=0) * s_lv
        o = _unstack_heads(_dot(a.astype(BF16), vb), c)
        st = st_ref[...]
        o = o + _dot_nt((q * jnp.exp(b)).astype(BF16), st.astype(BF16))
        b_last = b[c - 1:c, :]
        kb = k * jnp.exp(b_last - b)
        st_ref[...] = st * jnp.exp(b_last) + bd_mask * _dot_tn(vb, kb.astype(BF16))
        y = _head_rms(o, gain, ones_bd) * _silu(gp)
        o_ref[pl.ds(r0, c), :] = y
        return carry

    lax.fori_loop(0, p_ref.shape[0] // c, chunk, 0)


def _hgrn(proj, lb_rows, gain, bsz, seq):
    psel, lmask = _hgrn_levels()
    tb = min(REC_BLOCK, seq)
    nb = seq // tb
    return pl.pallas_call(
        _hgrn_kernel,
        grid=(bsz, nb),
        in_specs=[pl.BlockSpec((tb, 4 * MIX_WIDTH), lambda b, i: (b * nb + i, OFF_H // (4 * MIX_WIDTH))),
                  pl.BlockSpec((8, MIX_WIDTH), lambda b, i: (0, 0)),
                  pl.BlockSpec((1, MIX_WIDTH), lambda b, i: (0, 0)),
                  pl.BlockSpec(psel.shape, lambda b, i: (0, 0)),
                  pl.BlockSpec(lmask.shape, lambda b, i: (0, 0, 0))],
        out_specs=pl.BlockSpec((tb, MIX_WIDTH), lambda b, i: (b * nb + i, 0)),
        out_shape=jax.ShapeDtypeStruct((bsz * seq, MIX_WIDTH), F32),
        scratch_shapes=[pltpu.VMEM((MIX_WIDTH, MIX_WIDTH), F32)],
        compiler_params=_cp("parallel", "arbitrary"),
        name="hgrn2",
    )(proj, lb_rows, gain.reshape(1, MIX_WIDTH), jnp.asarray(psel, BF16), jnp.asarray(lmask, F32))


def _ret_kernel(p_ref, cos_ref, sin_ref, dec_ref, decin_ref, g_ref, o_ref, st_ref):
    @pl.when(pl.program_id(1) == 0)
    def _():
        st_ref[...] = jnp.zeros_like(st_ref)

    c = CHUNK
    w = MIX_WIDTH
    ones_bd = _block_ones(w)
    bd_mask = _block_ones(w, F32)
    gain = g_ref[...]
    dec_q = dec_ref[0:c, :]
    dec_k = dec_ref[c:2 * c, :]
    dec_state = dec_ref[2 * c:2 * c + 1, :]
    dec_in = decin_ref[...]

    def chunk(ci, carry):
        r0 = pl.multiple_of(ci * c, c)
        cos_t = cos_ref[pl.ds(r0, c), :]
        sin_t = sin_ref[pl.ds(r0, c), :]
        q = _rope(p_ref[pl.ds(r0, c), 0:w], cos_t, sin_t)
        k = _rope(p_ref[pl.ds(r0, c), w:2 * w], cos_t, sin_t) * (HEAD_DIM ** -0.5)
        v = p_ref[pl.ds(r0, c), 2 * w:3 * w]
        gp = p_ref[pl.ds(r0, c), 3 * w:4 * w]
        vb = v.astype(BF16)
        a = _dot_nt(_stack_heads(q).astype(BF16), k.astype(BF16)) * dec_in
        o = _unstack_heads(_dot(a.astype(BF16), vb), c)
        st = st_ref[...]
        o = o + _dot_nt(q.astype(BF16), st.astype(BF16)) * dec_q
        st_ref[...] = st * dec_state + bd_mask * _dot_tn(vb, (k * dec_k).astype(BF16))
        o_ref[pl.ds(r0, c), :] = _head_rms(o, gain, ones_bd) * _silu(gp)
        return carry

    lax.fori_loop(0, p_ref.shape[0] // c, chunk, 0)


def _ret_consts():
    log_gamma = np.log1p(-np.exp2(-5.0 - np.arange(N_HEADS, dtype=np.float64)))
    t = np.arange(CHUNK, dtype=np.float64)
    lane_h = np.arange(MIX_WIDTH) // HEAD_DIM
    dec_q = np.exp(log_gamma[lane_h][None, :] * (t[:, None] + 1.0))
    dec_k = np.exp(log_gamma[lane_h][None, :] * (CHUNK - 1.0 - t[:, None]))
    dec_state = np.exp(log_gamma[lane_h] * CHUNK)[None, :]
    dec = np.concatenate([dec_q, dec_k, np.broadcast_to(dec_state, (8, MIX_WIDTH))], 0)
    diff = t[:, None] - t[None, :]
    dec_in = np.concatenate([np.where(diff >= 0, np.exp(log_gamma[h] * diff), 0.0) for h in range(N_HEADS)], 0)
    return dec.astype(np.float32), dec_in.astype(np.float32)


def _ret(proj, cos4, sin4, gain, bsz, seq):
    dec, dec_in = _ret_consts()
    tb = min(REC_BLOCK, seq)
    nb = seq // tb
    return pl.pallas_call(
        _ret_kernel,
        grid=(bsz, nb),
        in_specs=[pl.BlockSpec((tb, 4 * MIX_WIDTH), lambda b, i: (b * nb + i, OFF_R // (4 * MIX_WIDTH))),
                  pl.BlockSpec((tb, MIX_WIDTH), lambda b, i: (i, 0)),
                  pl.BlockSpec((tb, MIX_WIDTH), lambda b, i: (i, 0)),
                  pl.BlockSpec(dec.shape, lambda b, i: (0, 0)),
                  pl.BlockSpec(dec_in.shape, lambda b, i: (0, 0)),
                  pl.BlockSpec((1, MIX_WIDTH), lambda b, i: (0, 0))],
        out_specs=pl.BlockSpec((tb, MIX_WIDTH), lambda b, i: (b * nb + i, 0)),
        out_shape=jax.ShapeDtypeStruct((bsz * seq, MIX_WIDTH), F32),
        scratch_shapes=[pltpu.VMEM((MIX_WIDTH, MIX_WIDTH), F32)],
        compiler_params=_cp("parallel", "arbitrary"),
        name="retention",
    )(proj, cos4, sin4, jnp.asarray(dec), jnp.asarray(dec_in), gain.reshape(1, MIX_WIDTH))


def _hgrn_lb_rows(lb):
    lb = lb.astype(F32)
    rows = jnp.stack([jnp.log(lb), jnp.log1p(-lb), 1.0 - lb], 0)
    return jnp.concatenate([rows, jnp.zeros((5, lb.shape[0]), F32)], 0)


def _rope_lane_tables(seq):
    inv = 1.0 / (ROPE_THETA ** (jnp.arange(0, HEAD_DIM, 2, dtype=F32) / HEAD_DIM))
    ang = jnp.arange(seq, dtype=F32)[:, None] * inv[None, :]
    cos, sin = jnp.cos(ang), jnp.sin(ang)
    cos_t = jnp.tile(cos, (1, LANES // (HEAD_DIM // 2)))
    sin_t = jnp.tile(jnp.concatenate([-sin, sin], axis=1), (1, LANES // HEAD_DIM))
    return cos_t, sin_t


def _expand_heads(cols, shape):
    hl = _head_of_lane(shape, 1)
    out = jnp.broadcast_to(cols[-1], shape)
    for h in range(len(cols) - 2, -1, -1):
        out = jnp.where(hl == h, jnp.broadcast_to(cols[h], shape), out)
    return out


def _mlstm_kernel(p_ref, gcol_ref, grow_ref, cw_ref, cb_ref, gbr_ref, gbc_ref, g_ref, o_ref,
                  ct_ref, n_ref, m_ref, hist_ref, cbuf_ref, qk_ref):
    c = CHUNK
    w = MIX_WIDTH
    tb = p_ref.shape[0]

    @pl.when(pl.program_id(1) == 0)
    def _():
        ct_ref[...] = jnp.zeros_like(ct_ref)
        n_ref[...] = jnp.zeros_like(n_ref)
        m_ref[...] = jnp.zeros_like(m_ref)
        hist_ref[...] = jnp.zeros_like(hist_ref)

    cbuf_ref[0:8, :] = hist_ref[...]
    cbuf_ref[8:, :] = p_ref[:, 0:2 * w]
    hist_ref[...] = p_ref[tb - 8:tb, 0:2 * w]
    acc = jnp.broadcast_to(cb_ref[...], (tb, 2 * w))
    for j in range(CONV_W):
        acc = acc + cw_ref[j:j + 1, :] * cbuf_ref[pl.ds(8 - (CONV_W - 1) + j, tb), :]
    qk_ref[...] = _silu(acc)

    ones_bd = _block_ones(w)
    bd_mask = _block_ones(w, F32)
    ti = lax.broadcasted_iota(jnp.int32, (c, c), 0)
    si = lax.broadcasted_iota(jnp.int32, (c, c), 1)
    causal = ti >= si
    tri = causal.astype(BF16)
    tri_t = (ti <= si).astype(BF16)
    gain = g_ref[...]
    ones_ext = jnp.ones((c, LANES), BF16)

    def chunk(ci, carry):
        r0 = pl.multiple_of(ci * c, c)
        q = qk_ref[pl.ds(r0, c), 0:w]
        k = qk_ref[pl.ds(r0, c), w:2 * w] * (HEAD_DIM ** -0.5)
        v = p_ref[pl.ds(r0, c), 2 * w:3 * w]
        op = p_ref[pl.ds(r0, c), 3 * w:4 * w]
        gc = gcol_ref[pl.ds(r0, c), :] + gbr_ref[...]
        gr = grow_ref[ci] + gbc_ref[...]
        b_c = _dot01_l(tri, _log_sigmoid(gc))
        b_r = _dot01_r(_log_sigmoid(gr), tri_t)
        wd, s_inter, em, wk, decay = [], [], [], [], []
        for h in range(N_HEADS):
            bc = b_c[:, N_HEADS + h:N_HEADS + h + 1]
            lic = gc[:, h:h + 1]
            br = b_r[N_HEADS + h:N_HEADS + h + 1, :]
            lir = gr[h:h + 1, :]
            dmat = jnp.where(causal, bc - br + lir, -jnp.inf)
            m_prev = m_ref[h:h + 1, 0:1]
            inter = bc + m_prev
            mrow = jnp.maximum(inter, jnp.max(dmat, axis=1, keepdims=True))
            wd.append(jnp.exp(dmat - mrow))
            s_inter.append(jnp.exp(inter - mrow))
            em.append(jnp.exp(-mrow))
            b_last = br[:, c - 1:c]
            m_new = jnp.maximum(b_last + m_prev, jnp.max(b_last - br + lir, axis=1, keepdims=True))
            wk.append(jnp.exp(b_last - bc + lic - m_new))
            decay.append(jnp.exp(b_last + m_prev - m_new))
            m_ref[h:h + 1, :] = jnp.broadcast_to(m_new, (1, LANES))
        s_inter_l = _expand_heads(s_inter, (c, w))
        em_l = _expand_heads(em, (c, w))
        wk_l = _expand_heads(wk, (c, w))
        decay_l = _expand_heads(decay, (1, w))
        qk = _dot_nt(_stack_heads(q).astype(BF16), k.astype(BF16))
        wmat = jnp.concatenate(wd, axis=0) * qk
        vb = v.astype(BF16)
        r = _dot(wmat.astype(BF16), jnp.concatenate([vb, ones_ext], axis=1))
        num_intra = _unstack_heads(r[:, 0:w], c)
        rs_l = _expand_heads([r[h * c:(h + 1) * c, w:w + 1] for h in range(N_HEADS)], (c, w))
        ct = ct_ref[...]
        nrow = n_ref[0:1, :]
        num = s_inter_l * _dot_nt(q.astype(BF16), ct.astype(BF16)) + num_intra
        den = s_inter_l * _group_sum(q * nrow, ones_bd) + rs_l
        hval = num / jnp.maximum(jnp.abs(den), em_l)
        kw = wk_l * k
        ct_ref[...] = ct * decay_l + bd_mask * _dot_tn(vb, kw.astype(BF16))
        n_ref[0:1, :] = nrow * decay_l + jnp.sum(kw, axis=0, keepdims=True)
        o_ref[pl.ds(r0, c), :] = _head_rms(hval, gain, ones_bd) * _sigmoid(op)
        return carry

    lax.fori_loop(0, tb // c, chunk, 0)


def _mlstm(proj, conv_w, conv_b, gate_b, gain, bsz, seq):
    t = bsz * seq
    w = MIX_WIDTH
    tb = min(REC_BLOCK, seq)
    nb = seq // tb
    ncb = tb // CHUNK
    grow = proj[:, OFF_MG:OFF_MG + 8].reshape(t // CHUNK, CHUNK, 8).transpose(0, 2, 1)
    gb_row = jnp.zeros((1, LANES), F32).at[0, 0:8].set(gate_b.astype(F32))
    gb_col = gate_b.astype(F32).reshape(8, 1)
    return pl.pallas_call(
        _mlstm_kernel,
        grid=(bsz, nb),
        in_specs=[pl.BlockSpec((tb, 4 * w), lambda b, i: (b * nb + i, OFF_M // (4 * w))),
                  pl.BlockSpec((tb, LANES), lambda b, i: (b * nb + i, OFF_MG // LANES)),
                  pl.BlockSpec((ncb, 8, CHUNK), lambda b, i: (b * nb + i, 0, 0)),
                  pl.BlockSpec((CONV_W, 2 * w), lambda b, i: (0, 0)),
                  pl.BlockSpec((1, 2 * w), lambda b, i: (0, 0)),
                  pl.BlockSpec((1, LANES), lambda b, i: (0, 0)),
                  pl.BlockSpec((8, 1), lambda b, i: (0, 0)),
                  pl.BlockSpec((1, w), lambda b, i: (0, 0))],
        out_specs=pl.BlockSpec((tb, w), lambda b, i: (b * nb + i, 0)),
        out_shape=jax.ShapeDtypeStruct((t, w), F32),
        scratch_shapes=[pltpu.VMEM((w, w), F32), pltpu.VMEM((8, w), F32), pltpu.VMEM((8, LANES), F32),
                        pltpu.VMEM((8, 2 * w), F32), pltpu.VMEM((tb + 8, 2 * w), F32),
                        pltpu.VMEM((tb, 2 * w), F32)],
        compiler_params=_cp("parallel", "arbitrary"),
        name="mlstm",
    )(proj, proj, grow, conv_w.astype(F32), conv_b.astype(F32).reshape(1, 2 * w), gb_row, gb_col,
      gain.reshape(1, w))


NSA_TQ = 128
NSA_KC = 512
GW = 2 * HEAD_DIM


def _nsa_prep_kernel(pq_ref, pk_ref, pv_ref, cos_ref, sin_ref, qg_ref, kg_ref,
                     qn_ref, qr_ref, ks_ref, kw_ref, vst_ref, vwt_ref):
    w = MIX_WIDTH
    for src, dst in ((pv_ref[:, 0:GW], vst_ref), (pv_ref[:, GW:2 * GW], vwt_ref)):
        vt = src.T
        tk = dst.shape[4]
        for g in range(NSA_GROUPS):
            rows = vt[g * HEAD_DIM:(g + 1) * HEAD_DIM, :]
            dup = jnp.concatenate([rows, rows], axis=0).astype(BF16)
            for j in range(dst.shape[2]):
                dst[0, g, j] = dup[:, j * tk:(j + 1) * tk]
    ones_bd = _block_ones(w)
    cos_t, sin_t = cos_ref[...], sin_ref[...]
    scale = HEAD_DIM ** -0.5
    qh = _head_rms(pq_ref[...], qg_ref[...], ones_bd)
    qn_ref[...] = (qh * scale).astype(BF16)
    qr_ref[...] = (_rope(qh, cos_t, sin_t) * scale).astype(BF16)
    ks_ref[...] = _rope(_head_rms(pk_ref[:, 0:w], kg_ref[1:2, :], ones_bd), cos_t, sin_t).astype(BF16)
    kw_ref[...] = _rope(_head_rms(pk_ref[:, w:2 * w], kg_ref[2:3, :], ones_bd), cos_t, sin_t).astype(BF16)


def _nsa_prep(proj, cos4, sin4, qnorm_g, knorm_g, bsz, seq):
    t = bsz * seq
    w = MIX_WIDTH
    tm = min(NSA_KC, seq)
    tq = min(NSA_TQ, seq)
    ns = seq // tm
    qg = jnp.tile(qnorm_g.astype(F32), w // HEAD_DIM).reshape(1, w)
    kg = jnp.concatenate([jnp.tile(knorm_g.astype(F32), (1, w // HEAD_DIM)), jnp.zeros((5, w), F32)], axis=0)
    out = jax.ShapeDtypeStruct((t, w), BF16)
    row = pl.BlockSpec((tm, w), lambda i: (i, 0))
    return pl.pallas_call(
        _nsa_prep_kernel,
        grid=(t // tm,),
        in_specs=[pl.BlockSpec((tm, w), lambda i: (i, OFF_NQ // w)),
                  pl.BlockSpec((tm, 2 * w), lambda i: (i, OFF_KD // (2 * w))),
                  pl.BlockSpec((tm, 2 * GW), lambda i: (i, OFF_V // (2 * GW))),
                  pl.BlockSpec((tm, w), lambda i: (i % ns, 0)),
                  pl.BlockSpec((tm, w), lambda i: (i % ns, 0)),
                  pl.BlockSpec((1, w), lambda i: (0, 0)),
                  pl.BlockSpec((8, w), lambda i: (0, 0))],
        out_specs=[row, row, row, row,
                   pl.BlockSpec((1, NSA_GROUPS, 1, GW, tm), lambda i: (i // ns, 0, i % ns, 0, 0)),
                   pl.BlockSpec((1, NSA_GROUPS, tm // tq, GW, tq), lambda i: (i // ns, 0, i % ns, 0, 0))],
        out_shape=[out, out, out, out,
                   jax.ShapeDtypeStruct((bsz, NSA_GROUPS, seq // tm, GW, tm), BF16),
                   jax.ShapeDtypeStruct((bsz, NSA_GROUPS, seq // tq, GW, tq), BF16)],
        compiler_params=_cp("parallel"),
        name="nsa_prep",
    )(proj, proj, proj, cos4, sin4, qg, kg)


def _nsa_cmp_kernel(xr_ref, pe_ref, w0_ref, w1_ref, kg_ref, ovt_ref, qn_ref, ocmp_ref, sel_ref,
                    kc_ref, vc_ref, v_ref, *, n_top):
    tq = qn_ref.shape[0]
    nr = xr_ref.shape[0]
    nsel = sel_ref.shape[2]
    w = MIX_WIDTH

    @pl.when(pl.program_id(1) == 0)
    def _():
        xr = xr_ref[...]
        y0 = _dot((xr + pe_ref[0]).astype(BF16), w0_ref[...])
        y1 = _dot((xr + pe_ref[1]).astype(BF16), w1_ref[...])
        kv = y0 + pltpu.roll(y1, nr - 1, 0)
        kc_ref[...] = _head_rms(kv[:, 0:w], kg_ref[...], _block_ones(w)).astype(BF16)
        vc_ref[...] = kv[:, w:2 * w].astype(BF16)

    pos0 = pl.program_id(1) * tq
    hl = _head_of_lane((tq, GW), 1)
    pos_r = pos0 + lax.broadcasted_iota(jnp.int32, (tq, nr), 0)
    valid = lax.broadcasted_iota(jnp.int32, (tq, nr), 1) * CMP_STRIDE + (CMP_LEN - 1) <= pos_r
    pos_c = pos0 + lax.broadcasted_iota(jnp.int32, (nr, tq), 1)
    valid_t = lax.broadcasted_iota(jnp.int32, (nr, tq), 0) * CMP_STRIDE + (CMP_LEN - 1) <= pos_c
    jrow = lax.broadcasted_iota(jnp.int32, (nsel, tq), 0)
    cur = (pos0 + lax.broadcasted_iota(jnp.int32, (nsel, tq), 1)) // SEL_BLOCK
    forced = (jrow == 0) | (jrow == cur) | (jrow == cur - 1)
    ovt = ovt_ref[...]

    for g in range(NSA_GROUPS):
        qg = qn_ref[:, g * GW:(g + 1) * GW]
        kg = kc_ref[:, g * GW:(g + 1) * GW]
        vg = vc_ref[:, g * GW:(g + 1) * GW]
        o_g = jnp.zeros((tq, GW), F32)
        pt_sum = jnp.zeros((nr, tq), F32)
        for hh in range(2):
            qm = jnp.where(hl == hh, qg, jnp.zeros_like(qg))
            s = jnp.where(valid, _dot_nt(qm, kg), NEG_INF)
            e = jnp.exp(s - jnp.max(s, axis=1, keepdims=True))
            p = jnp.where(valid, e / jnp.sum(e, axis=1, keepdims=True), 0.0)
            o_g = jnp.where(hl == hh, _dot(p.astype(BF16), vg), o_g)
            st = jnp.where(valid_t, _dot_nt(kg, qm), NEG_INF)
            et = jnp.exp(st - jnp.max(st, axis=0, keepdims=True))
            pt_sum = pt_sum + jnp.where(valid_t, et / jnp.sum(et, axis=0, keepdims=True), 0.0)
        ocmp_ref[:, g * GW:(g + 1) * GW] = o_g
        p_hi = pt_sum.astype(BF16)
        p_lo = (pt_sum - p_hi.astype(F32)).astype(BF16)
        imp = _dot(ovt, p_hi) + _dot(ovt, p_lo)
        val = jnp.where(jrow <= cur, imp + FORCE_BONUS * forced.astype(F32), NEG_INF)
        v_ref[...] = val

        def rank(jp, cnt):
            row = v_ref[pl.ds(jp, 1), :]
            tie = jnp.where(jrow > jp, 1.0, 0.0)
            return cnt + jnp.where(row > val, 1.0, jnp.where(row == val, tie, 0.0))

        cnt = lax.fori_loop(0, nsel, rank, jnp.zeros((nsel, tq), F32))
        sel_ref[0, g] = ((cnt < n_top) & (jrow <= cur)).astype(F32)


def _nsa_cmp_weights(cmp_pe, cmp_w):
    half = CMP_LEN // 2
    wl = cmp_w.astype(F32).reshape(2, 2, half, HEAD_DIM, HEAD_DIM)
    eye2 = jnp.eye(2, dtype=F32)
    w2 = jnp.einsum('kardz,kK,gG,h->arkgdKGhz', wl, eye2, eye2, jnp.ones((2,), F32))
    w2 = w2.reshape(2, half * 4 * HEAD_DIM, 8 * HEAD_DIM)
    pl_ = cmp_pe.astype(F32).reshape(2, 2, half, HEAD_DIM)
    pe2 = jnp.broadcast_to(pl_.transpose(1, 2, 0, 3)[:, :, :, None, :], (2, half, 2, 2, HEAD_DIM))
    return w2.astype(BF16), pe2.reshape(2, 1, half * 4 * HEAD_DIM)


def _nsa_cmp(kcvc, qn, cmp_pe, cmp_w, knorm0, bsz, seq, tq=512):
    t = bsz * seq
    w = MIX_WIDTH
    tq = min(tq, seq)
    nq = seq // tq
    nr = seq // CMP_STRIDE
    nsel = seq // SEL_BLOCK
    n_top = min(SEL_TOPK, nsel)
    w2, pe2 = _nsa_cmp_weights(cmp_pe, cmp_w)
    xr = kcvc.reshape(t // CMP_STRIDE, CMP_STRIDE * w)
    kg = jnp.tile(knorm0.astype(F32), w // HEAD_DIM).reshape(1, w)
    n_i = np.arange(nr)[:, None] * CMP_STRIDE
    j_i = np.arange(nsel)[None, :] * SEL_BLOCK
    ov = ((n_i < j_i + SEL_BLOCK) & (n_i + CMP_LEN > j_i)).astype(np.float32)
    ov[nr - 1, :] = 0.0
    kin = CMP_STRIDE * w
    return pl.pallas_call(
        functools.partial(_nsa_cmp_kernel, n_top=n_top),
        grid=(bsz, nq),
        in_specs=[pl.BlockSpec((nr, kin), lambda b, i: (b, 0)),
                  pl.BlockSpec((2, 1, kin), lambda b, i: (0, 0, 0)),
                  pl.BlockSpec((None, kin, 2 * w), lambda b, i: (0, 0, 0)),
                  pl.BlockSpec((None, kin, 2 * w), lambda b, i: (1, 0, 0)),
                  pl.BlockSpec((1, w), lambda b, i: (0, 0)),
                  pl.BlockSpec((nsel, nr), lambda b, i: (0, 0)),
                  pl.BlockSpec((tq, w), lambda b, i: (b * nq + i, 0))],
        out_specs=[pl.BlockSpec((tq, w), lambda b, i: (b * nq + i, 0)),
                   pl.BlockSpec((1, NSA_GROUPS, nsel, tq), lambda b, i: (b, 0, 0, i))],
        out_shape=[jax.ShapeDtypeStruct((t, w), F32),
                   jax.ShapeDtypeStruct((bsz, NSA_GROUPS, nsel, seq), F32)],
        scratch_shapes=[pltpu.VMEM((nr, w), BF16), pltpu.VMEM((nr, w), BF16), pltpu.VMEM((nsel, tq), F32)],
        compiler_params=_cp("parallel", "arbitrary"),
        name="nsa_cmp",
    )(xr, pe2, w2, w2, kg, jnp.asarray(ov.T, BF16), qn)


def _nsa_attn_kernel(qr_ref, ks_ref, kw_ref, vs_ref, vw_ref, sel_ref, ocmp_ref, gate_ref, o_ref, *, kc, wt):
    tq = qr_ref.shape[0]
    i = pl.program_id(1)
    hl = _head_of_lane((tq, GW), 1)
    nbk = kc // SEL_BLOCK
    groups = range(NSA_GROUPS)

    def stacked_q(g):
        q = qr_ref[:, g * GW:(g + 1) * GW]
        return jnp.concatenate([jnp.where(hl == 0, q, jnp.zeros_like(q)), jnp.where(hl == 1, q, jnp.zeros_like(q))],
                               axis=0)

    qs_all = [stacked_q(g) for g in groups]

    def lane_qpos(rows):
        return i * tq + lax.broadcasted_iota(jnp.int32, (rows, 2 * tq), 1) % tq

    def finish(acc, l):
        ot = (acc / l).T
        return jnp.where(hl == 0, ot[0:tq, :], ot[tq:2 * tq, :])

    qpos_s = lane_qpos(kc)
    krow_s = lax.broadcasted_iota(jnp.int32, (kc, 2 * tq), 0)

    def sel_step(g, c, carry, diagonal):
        m, l, acc = carry
        k0 = pl.multiple_of(c * kc, kc)
        st = _dot_nt(ks_ref[pl.ds(k0, kc), g * GW:(g + 1) * GW], qs_all[g])
        srows = sel_ref[0, g, pl.ds(pl.multiple_of(c * nbk, nbk), nbk), :]
        srows = jnp.concatenate([srows, srows], axis=1)
        smask = jnp.concatenate([jnp.broadcast_to(srows[r:r + 1, :], (SEL_BLOCK, 2 * tq)) for r in range(nbk)],
                                axis=0)
        msk = smask > 0.5
        if diagonal:
            msk = msk & (k0 + krow_s <= qpos_s)
        st = jnp.where(msk, st, NEG_INF)
        m_new = jnp.maximum(m, jnp.max(st, axis=0, keepdims=True))
        p = jnp.exp(st - m_new)
        alpha = jnp.exp(m - m_new)
        l = l * alpha + jnp.sum(p, axis=0, keepdims=True)
        acc = acc * alpha + _dot(vs_ref[0, g, c], p.astype(BF16))
        return m_new, l, acc

    def sel_body(c, carries, diagonal):
        return tuple(sel_step(g, c, carries[g], diagonal) for g in groups)

    init = (jnp.full((1, 2 * tq), NEG_INF, F32), jnp.zeros((1, 2 * tq), F32), jnp.zeros((GW, 2 * tq), F32))
    n_before = (i * tq) // kc
    carries = lax.fori_loop(0, n_before, functools.partial(sel_body, diagonal=False), (init,) * NSA_GROUPS)
    carries = sel_body(n_before, carries, True)

    j0 = jnp.maximum(i - (wt - 1), 0)
    k0 = pl.multiple_of(j0 * tq, tq)
    span = wt * tq
    kpos = k0 + lax.broadcasted_iota(jnp.int32, (span, 2 * tq), 0)
    qpos_w = lane_qpos(span)
    wmask = (kpos <= qpos_w) & (kpos > qpos_w - WINDOW)
    gb = _sigmoid(gate_ref[...])
    for g in groups:
        _, l_s, acc_s = carries[g]
        o_sel = finish(acc_s, l_s)
        st = jnp.where(wmask, _dot_nt(kw_ref[pl.ds(k0, span), g * GW:(g + 1) * GW], qs_all[g]), NEG_INF)
        p = jnp.exp(st - jnp.max(st, axis=0, keepdims=True))
        vt = jnp.concatenate([vw_ref[0, g, j0 + r] for r in range(wt)], axis=1)
        o_win = finish(_dot(vt, p.astype(BF16)), jnp.sum(p, axis=0, keepdims=True))

        def gate(branch):
            cols = [gb[:, (2 * g + hh) * 3 + branch:(2 * g + hh) * 3 + branch + 1] for hh in range(2)]
            return _expand_heads(cols, (tq, GW))

        o_ref[:, g * GW:(g + 1) * GW] = (gate(0) * ocmp_ref[:, g * GW:(g + 1) * GW]
                                         + gate(1) * o_sel + gate(2) * o_win)


def _nsa_attn(proj, qr, ks, kw, vst, vwt, sel, o_cmp, bsz, seq):
    t = bsz * seq
    w = MIX_WIDTH
    tq = min(NSA_TQ, seq)
    nq = seq // tq
    nsel = seq // SEL_BLOCK
    kc = min(NSA_KC, seq)
    wt = min(WINDOW // tq + 1, nq)
    kspec = pl.BlockSpec((seq, w), lambda b, i: (b, 0))
    return pl.pallas_call(
        functools.partial(_nsa_attn_kernel, kc=kc, wt=wt),
        grid=(bsz, nq),
        in_specs=[pl.BlockSpec((tq, w), lambda b, i: (b * nq + i, 0)),
                  kspec, kspec,
                  pl.BlockSpec((1, NSA_GROUPS, seq // kc, GW, kc), lambda b, i: (b, 0, 0, 0, 0)),
                  pl.BlockSpec((1, NSA_GROUPS, nq, GW, tq), lambda b, i: (b, 0, 0, 0, 0)),
                  pl.BlockSpec((1, NSA_GROUPS, nsel, tq), lambda b, i: (b, 0, 0, i)),
                  pl.BlockSpec((tq, w), lambda b, i: (b * nq + i, 0)),
                  pl.BlockSpec((tq, LANES), lambda b, i: (b * nq + i, OFF_NG // LANES))],
        out_specs=pl.BlockSpec((tq, w), lambda b, i: (b * nq + i, 0)),
        out_shape=jax.ShapeDtypeStruct((t, w), F32),
        compiler_params=_cp("parallel", "arbitrary"),
        name="nsa_attn",
    )(qr, ks, kw, vst, vwt, sel, o_cmp, proj)


def _nsa(proj, kcvc, cos4, sin4, qnorm_g, knorm_g, cmp_pe, cmp_w, bsz, seq):
    qn, qr, ks, kw, vst, vwt = _nsa_prep(proj, cos4, sin4, qnorm_g, knorm_g, bsz, seq)
    o_cmp, sel = _nsa_cmp(kcvc, qn, cmp_pe, cmp_w, knorm_g[0], bsz, seq)
    return _nsa_attn(proj, qr, ks, kw, vst, vwt, sel, o_cmp, bsz, seq)


PEER_TT = 128
PEER_CT = 8
HALF_D = 512


SUBLANES = 8
CODE_BITS = 127
FAR_BELOW = -3.0e38


def _with_code(x, code):
    bits = lax.bitcast_convert_type(x, jnp.int32)
    return lax.bitcast_convert_type((bits & ~CODE_BITS) | code, F32)


def _split_code(x):
    bits = lax.bitcast_convert_type(x, jnp.int32)
    return lax.bitcast_convert_type(bits & ~CODE_BITS, F32), bits & CODE_BITS


def _sort16_desc(xs):
    xs = list(xs)
    n = len(xs)
    k = 2
    while k <= n:
        j = k // 2
        while j >= 1:
            for i in range(n):
                l = i ^ j
                if l > i:
                    hi, lo = jnp.maximum(xs[i], xs[l]), jnp.minimum(xs[i], xs[l])
                    xs[i], xs[l] = (hi, lo) if (i & k) == 0 else (lo, hi)
            j //= 2
        k *= 2
    return xs


def _merge16_desc(xs):
    xs = list(xs)
    j = len(xs) // 2
    while j >= 1:
        for i in range(len(xs)):
            l = i ^ j
            if l > i:
                xs[i], xs[l] = jnp.maximum(xs[i], xs[l]), jnp.minimum(xs[i], xs[l])
        j //= 2
    return xs


def _top16_columns(x):
    n = PEER_TOPK
    xs = _sort16_desc([x[SUBLANES * j:SUBLANES * (j + 1), :] for j in range(n)])
    shift = SUBLANES // 2
    while shift >= 1:
        rolled = [pltpu.roll(a, shift, 0) for a in xs]
        xs = _merge16_desc([jnp.maximum(xs[i], rolled[n - 1 - i]) for i in range(n)])
        shift //= 2
    return xs


_PEER_CAND_TILES = ((0, 0, 8), (0, 1, 8), (1, 0, 8), (2, 0, 5), (3, 0, 4), (4, 0, 3), (5, 0, 2), (6, 0, 2), (7, 0, 2))


ROUTE_HEADS_PER_STEP = 4


def _route_head(q_ref, key_ref, hh):
    tt = q_ref.shape[0]
    nk = PEER_NKEYS
    n = PEER_TOPK
    row = lax.broadcasted_iota(jnp.int32, (nk, tt), 0)
    sub = lax.broadcasted_iota(jnp.int32, (SUBLANES, tt), 0)
    vals, ids = [], []
    for p in range(2):
        c0 = (2 * hh + p) * PEER_KDIM
        st = _dot_nt(key_ref[hh, p], q_ref[:, c0:c0 + PEER_KDIM])
        top = [_split_code(a) for a in _top16_columns(_with_code(st, (nk - 1) - row))]
        vals.append([v for v, _ in top])
        ids.append([(nk - 1) - c for _, c in top])
    (v1, v2), (i1, i2) = vals, ids

    def stack(xs, lo):
        out = xs[lo]
        for s in range(1, SUBLANES):
            out = jnp.where(sub == s, xs[lo + s], out)
        return out

    v2t, i2t = (stack(v2, 0), stack(v2, SUBLANES)), (stack(i2, 0), stack(i2, SUBLANES))
    cand, cexp = [], []
    for a, tile, nvalid in _PEER_CAND_TILES:
        v = v1[a] + v2t[tile]
        cand.append(v if nvalid == SUBLANES else jnp.where(sub < nvalid, v, FAR_BELOW))
        cexp.append(i1[a] * nk + i2t[tile])
    cand.append(stack(v1, SUBLANES) + v2[0])
    cexp.append(stack(i1, SUBLANES) * nk + i2[0])
    n_tiles = len(cand)
    slot_code = [(nk - 1) - (c * SUBLANES + sub) for c in range(n_tiles)]
    coded = [_with_code(v, sc) for v, sc in zip(cand, slot_code)]
    coded += [jnp.full((SUBLANES, tt), FAR_BELOW, F32)] * (n - n_tiles)
    top = [_split_code(a) for a in _top16_columns(jnp.concatenate(coded, axis=0))]
    call = jnp.concatenate(cexp, axis=0)
    slot = (nk - 1) - lax.broadcasted_iota(jnp.int32, call.shape, 0)
    ex = [jnp.exp(v - top[0][0]) for v, _ in top]
    tot = ex[0]
    for k in range(1, n):
        tot = tot + ex[k]
    krow = lax.broadcasted_iota(jnp.int32, (n, tt), 0)
    e_tile = jnp.zeros((n, tt), F32)
    g_tile = jnp.zeros((n, tt), F32)
    for k in range(n):
        hit = slot == jnp.concatenate([top[k][1]] * n_tiles, axis=0)
        e_k = jnp.sum(jnp.where(hit, call, 0), axis=0, keepdims=True)
        e_tile = jnp.where(krow == k, e_k.astype(F32), e_tile)
        g_tile = jnp.where(krow == k, (ex[k] / tot)[0:1, :], g_tile)
    return e_tile, g_tile


def _peer_route_kernel(q_ref, key_ref, e_ref, g_ref, e_scr, g_scr):
    hps = key_ref.shape[0]
    tiles = [_route_head(q_ref, key_ref, hh) for hh in range(hps)]
    rows = hps * PEER_TOPK
    r0 = pl.multiple_of(pl.program_id(1) * rows, rows)
    e_scr[pl.ds(r0, rows), :] = jnp.concatenate([e for e, _ in tiles], axis=0)
    g_scr[pl.ds(r0, rows), :] = jnp.concatenate([g for _, g in tiles], axis=0)

    @pl.when(pl.program_id(1) == pl.num_programs(1) - 1)
    def _():
        e_ref[...] = e_scr[...].T.astype(jnp.int32)
        g_ref[...] = g_scr[...].T


def _peer_route(qp, keys):
    t = qp.shape[0]
    tt = min(PEER_TT, t)
    ne = PEER_HEADS * PEER_TOPK
    hps = ROUTE_HEADS_PER_STEP
    return pl.pallas_call(
        _peer_route_kernel,
        grid=(t // tt, PEER_HEADS // hps),
        in_specs=[pl.BlockSpec((tt, hps * 2 * PEER_KDIM), lambda i, h: (i, h)),
                  pl.BlockSpec((hps, 2, PEER_NKEYS, PEER_KDIM), lambda i, h: (h, 0, 0, 0))],
        out_specs=[pl.BlockSpec((tt, ne), lambda i, h: (i, 0)),
                   pl.BlockSpec((tt, ne), lambda i, h: (i, 0))],
        out_shape=[jax.ShapeDtypeStruct((t, ne), jnp.int32),
                   jax.ShapeDtypeStruct((t, ne), F32)],
        scratch_shapes=[pltpu.VMEM((ne, tt), F32), pltpu.VMEM((ne, tt), F32)],
        compiler_params=_cp("parallel", "arbitrary"),
        name="peer_route",
    )(qp, keys)


def _pack_tables_kernel(u_ref, v_ref, o_ref):
    def pack(x):
        lo = lax.bitcast_convert_type(x[:, 0:HALF_D].astype(BF16).astype(F32), jnp.int32)
        hi = lax.bitcast_convert_type(x[:, HALF_D:2 * HALF_D].astype(BF16).astype(F32), jnp.int32)
        return lax.shift_right_logical(lo, 16) | (hi & jnp.int32(-65536))

    o_ref[:, 0:HALF_D] = pack(u_ref[...])
    o_ref[:, HALF_D:2 * HALF_D] = pack(v_ref[...])


def _pack_tables(u_tabs, v_tabs, layer, tr=512):
    _, e, d = u_tabs.shape
    assert d == 2 * HALF_D
    spec_in = pl.BlockSpec((None, tr, d), lambda i: (layer, i, 0))
    spec = pl.BlockSpec((tr, d), lambda i: (i, 0))
    return pl.pallas_call(
        _pack_tables_kernel,
        grid=(e // tr,),
        in_specs=[spec_in, spec_in],
        out_specs=spec,
        out_shape=jax.ShapeDtypeStruct((e, d), jnp.int32),
        compiler_params=_cp("parallel"),
        name="peer_pack",
    )(u_tabs, v_tabs)


def _unpack_rows(wd):
    lo = lax.bitcast_convert_type(lax.shift_left(wd, 16), F32)
    hi = lax.bitcast_convert_type(lax.bitwise_and(wd, jnp.int32(-65536)), F32)
    return lo, hi


SC_WINDOW = 32


def _sc_gather(table, idx):
    from jax.experimental.pallas import tpu_sc as plsc
    n = idx.shape[0]
    width = table.shape[1]
    mesh = plsc.VectorSubcoreMesh(core_axis_name="core", subcore_axis_name="subcore")

    @functools.partial(pl.kernel, out_type=jax.ShapeDtypeStruct((n, width), table.dtype), mesh=mesh)
    def gather(tab_hbm, idx_hbm, out_hbm):
        def body(idx_vmem, out_vmem):
            pltpu.sync_copy(tab_hbm.at[idx_vmem.at[0, pl.ds(0, SC_WINDOW)]], out_vmem)

        pltpu.emit_pipeline(
            body,
            grid=(n // SC_WINDOW,),
            in_specs=[pl.BlockSpec((1, LANES), lambda i: (0, i))],
            out_specs=[pl.BlockSpec((SC_WINDOW, width), lambda i: (i, 0))],
            core_axis_name=("core", "subcore"),
            dimension_semantics=(pltpu.PARALLEL,),
            trace_scopes=False,
        )(idx_hbm, out_hbm)

    idx_pad = jnp.pad(idx.reshape(n // SC_WINDOW, SC_WINDOW), ((0, 0), (0, LANES - SC_WINDOW)))
    return gather(table, idx_pad.reshape(1, (n // SC_WINDOW) * LANES))


def _peer_combine_kernel(x_ref, g2_ref, rows_a_ref, rows_b_ref, gate_ref, o_ref):
    ne = PEER_HEADS * PEER_TOPK
    x = x_ref[...]
    ct = x.shape[0]
    xn = x * lax.rsqrt(jnp.mean(x * x, axis=-1, keepdims=True) + NORM_EPS) * g2_ref[...]
    gate_t = jnp.concatenate([gate_ref[...]] * (ne // ct), axis=0).T
    for jj in range(ct):
        rows_ref, j = (rows_a_ref, jj) if jj < ct // 2 else (rows_b_ref, jj - ct // 2)
        u_lo, u_hi = _unpack_rows(rows_ref[j * ne:(j + 1) * ne, 0:HALF_D])
        xr = xn[jj:jj + 1, :]
        h = jnp.sum(u_lo * xr[:, 0:HALF_D] + u_hi * xr[:, HALF_D:2 * HALF_D], axis=1, keepdims=True)
        act = 0.5 * h * (1.0 + lax.erf(h * (2.0 ** -0.5)))
        wgt = gate_t[:, jj:jj + 1] * act
        v_lo, v_hi = _unpack_rows(rows_ref[j * ne:(j + 1) * ne, HALF_D:2 * HALF_D])
        o_ref[jj:jj + 1, 0:HALF_D] = x[jj:jj + 1, 0:HALF_D] + jnp.sum(wgt * v_lo, axis=0, keepdims=True)
        o_ref[jj:jj + 1, HALF_D:2 * HALF_D] = (x[jj:jj + 1, HALF_D:2 * HALF_D]
                                               + jnp.sum(wgt * v_hi, axis=0, keepdims=True))


def _peer_combine(x, g2, rows, gates, first_token):
    t, d = x.shape
    ne = PEER_HEADS * PEER_TOPK
    ct = PEER_CT
    steps = rows.shape[0] // (ct * ne)
    off = first_token // ct
    return pl.pallas_call(
        _peer_combine_kernel,
        grid=(steps,),
        in_specs=[pl.BlockSpec((ct, d), lambda i: (off + i, 0)),
                  pl.BlockSpec((1, d), lambda i: (0, 0)),
                  pl.BlockSpec((ct * ne // 2, d), lambda i: (2 * i, 0)),
                  pl.BlockSpec((ct * ne // 2, d), lambda i: (2 * i + 1, 0)),
                  pl.BlockSpec((ct, ne), lambda i: (off + i, 0))],
        out_specs=pl.BlockSpec((ct, d), lambda i: (off + i, 0)),
        out_shape=jax.ShapeDtypeStruct((t, d), F32),
        input_output_aliases={0: 0},
        compiler_params=_cp("parallel"),
        name="peer_combine",
    )(x, g2.reshape(1, d), rows, rows, gates)


PEER_TOKENS_PER_GATHER = 2048


def _peer_route_stage(x, g2, wq_b, keys_b):
    t = x.shape[0]
    ne = PEER_HEADS * PEER_TOPK
    qp = _norm_matmul(x, g2, wq_b, out_dtype=BF16)
    e_tok, g_tok = _peer_route(qp, keys_b)
    return e_tok.reshape(t * ne), g_tok


def _peer_gather_stage(table, idx, t, gather_fn):
    ne = PEER_HEADS * PEER_TOPK
    tc = min(PEER_TOKENS_PER_GATHER, t)
    return [gather_fn(table, idx[c * tc * ne:(c + 1) * tc * ne]) for c in range(t // tc)]


def _peer_combine_stage(x, g2, rows_list, gates):
    tc = x.shape[0] // len(rows_list)
    for c, rows in enumerate(rows_list):
        x = _peer_combine(x, g2, rows, gates, c * tc)
    return x


def _peer(x, g2, wq, keys, u_tabs, v_tabs, layer, gather_fn):
    idx, gates = _peer_route_stage(x, g2, wq.astype(BF16), keys.astype(BF16))
    table = _pack_tables(u_tabs, v_tabs, layer)
    rows_list = _peer_gather_stage(table, idx, x.shape[0], gather_fn)
    return _peer_combine_stage(x, g2, rows_list, gates)


_IN_WIDTHS = (256, 256, 256, 256, 256, 256, 256, 4, 4, 256, 256, 128, 128, 128, 128, 128, 128, 12,
              256, 256, 256, 256)


def _dup_groups(wcols):
    g0, g1 = wcols[:, :HEAD_DIM], wcols[:, HEAD_DIM:]
    return jnp.concatenate([g0, g0, g1, g1], axis=1)


def _layout_w_in(w_in):
    offs = np.cumsum((0,) + _IN_WIDTHS)
    cols = [w_in[:, offs[i]:offs[i + 1]] for i in range(len(_IN_WIDTHS))]
    (hq, hf, hi, hg, mq, mk, mv, mi, mf, mo, nq, nkc, nvc, nks, nvs, nkw, nvw, ng, rq, rk, rv, rg) = cols
    d = w_in.shape[0]
    pad = lambda c, n: jnp.concatenate([c, jnp.zeros((d, n - c.shape[1]), w_in.dtype)], axis=1)
    main = jnp.concatenate([hq, hf, hi, hg, mq, mk, mv, mo, rq, rk, rv, rg,
                            _dup_groups(nks), _dup_groups(nkw), nq, nvs, nvw,
                            pad(jnp.concatenate([mi, mf], axis=1), LANES), pad(ng, LANES)], axis=1)
    assert main.shape[1] == N_MAIN
    kcvc = jnp.concatenate([nkc, nvc], axis=1)
    return main.astype(BF16), kcvc.astype(BF16)


def kernel(x, norm1_g, w_in, hgrn_lb, hgrn_onorm_g, mlstm_conv_w, mlstm_conv_b, mlstm_gate_b, mlstm_onorm_g, nsa_qnorm_g, nsa_knorm_g, nsa_cmp_pe, nsa_cmp_w, ret_onorm_g, w_up, w_gate, w_out, norm2_g, peer_wq, peer_keys, peer_u, peer_v):
    bsz, seq, d = x.shape
    t = bsz * seq
    depth = w_in.shape[0]
    cos_t, sin_t = _rope_lane_tables(seq)
    cos4, sin4 = jnp.tile(cos_t, (1, 2)), jnp.tile(sin_t, (1, 2))
    lb_cum = jnp.cumsum(jax.nn.softmax(hgrn_lb.astype(F32), axis=0), axis=0)
    lb_all = lb_cum - lb_cum[0:1]
    weights = []
    for l in range(depth):
        w_main, w_kcvc = _layout_w_in(w_in[l])
        weights.append(dict(
            main=w_main, kcvc=w_kcvc, gate=w_gate[l].astype(BF16), up=w_up[l].astype(BF16),
            out=w_out[l].astype(BF16), lb=_hgrn_lb_rows(lb_all[l]), wq=peer_wq[l].astype(BF16),
            keys=peer_keys[l].astype(BF16), table=_pack_tables(peer_u, peer_v, l)))

    def mixer_steps(xh, l, nb):
        wl = weights[l]
        st = {}

        def proj(dep):
            st["proj"] = _norm_matmul(xh, _after(norm1_g[l], dep), wl["main"])
            return st["proj"]

        def gates(dep):
            st["gates"] = _norm_matmul(xh, _after(norm1_g[l], dep), wl["gate"], act="sigmoid", out_dtype=BF16)
            return st["gates"]

        def hgrn(dep):
            st["oh"] = _hgrn(st["proj"], wl["lb"], _after(hgrn_onorm_g[l], dep), nb, seq)
            return st["oh"]

        def mlstm(dep):
            st["om"] = _mlstm(st["proj"], mlstm_conv_w[l], mlstm_conv_b[l], mlstm_gate_b[l],
                              _after(mlstm_onorm_g[l], dep), nb, seq)
            return st["om"]

        def ret(dep):
            st["or"] = _ret(st["proj"], cos4, sin4, _after(ret_onorm_g[l], dep), nb, seq)
            return st["or"]

        def nsa_front(dep):
            kcvc = _norm_matmul(xh, _after(norm1_g[l], dep), wl["kcvc"])
            qn, qr, ks, kw, vst, vwt = _nsa_prep(st["proj"], cos4, sin4, nsa_qnorm_g[l], nsa_knorm_g[l], nb, seq)
            o_cmp, sel = _nsa_cmp(kcvc, qn, nsa_cmp_pe[l], nsa_cmp_w[l], nsa_knorm_g[l][0], nb, seq)
            st["nsa"] = (qr, ks, kw, vst, vwt, sel, o_cmp)
            return o_cmp

        def nsa_attn(dep):
            del dep
            st["on"] = _nsa_attn(st["proj"], *st["nsa"], nb, seq)
            return st["on"]

        def merge(dep):
            del dep
            st["xm"] = _merge(xh, st["gates"], (st["oh"], st["om"], st["on"], st["or"]), wl["up"], wl["out"])
            return st["xm"]

        def route(dep):
            st["idx"], st["pgates"] = _peer_route_stage(st["xm"], _after(norm2_g[l], dep), wl["wq"], wl["keys"])
            return st["pgates"]

        return [proj, gates, hgrn, mlstm, ret, nsa_front, nsa_attn, merge, route], st

    combine_slots = (0, 8, 8, 8, 8, 8, 8, 8)

    def combine_steps(l, xm, rows_list, pgates):
        box = {"x": xm}
        tc = xm.shape[0] // len(rows_list)

        def make(c):
            def step(dep):
                box["x"] = _peer_combine(box["x"], _after(norm2_g[l], dep), rows_list[c], pgates, c * tc)
                return box["x"]
            return step

        return [make(c) for c in range(len(rows_list))], box

    n_groups = next(n for n in (8, 4, 2, 1) if bsz % n == 0)
    nb = bsz // n_groups
    xs = [x[g * nb:(g + 1) * nb].reshape(nb * seq, d) for g in range(n_groups)]
    dep = None
    lag = min(2, n_groups - 1)
    pending = []
    for l in range(depth):
        for g in range(n_groups):
            msteps, st = mixer_steps(xs[g], l, nb)
            due = pending.pop(0) if len(pending) == lag and lag > 0 else None
            csteps = due[1] if due is not None else []
            ci = 0
            for si, mstep in enumerate(msteps):
                dep = mstep(dep)
                while ci < len(csteps) and (ci >= len(combine_slots) or combine_slots[ci] <= si):
                    dep = csteps[ci](dep)
                    ci += 1
            for cstep in csteps[ci:]:
                dep = cstep(dep)
            if due is not None:
                xs[due[0]] = due[2]["x"]
            rows_list = _peer_gather_stage(weights[l]["table"], st["idx"], nb * seq, _sc_gather)
            csteps, box = combine_steps(l, st["xm"], rows_list, st["pgates"])
            pending.append((g, csteps, box))
            if lag == 0:
                for cstep in pending.pop(0)[1]:
                    dep = cstep(dep)
                xs[g] = box["x"]
    for pg, csteps, box in pending:
        for cstep in csteps:
            dep = cstep(dep)
        xs[pg] = box["x"]
    return jnp.concatenate(xs, axis=0).reshape(bsz, seq, d)
```

```python
import functools
import math

import numpy as np
import jax
import jax.numpy as jnp
from jax import lax
from jax.experimental import pallas as pl
from jax.experimental.pallas import tpu as pltpu

F32 = jnp.float32
BF16 = jnp.bfloat16

HEAD_DIM = 64
N_HEADS = 4
MIX_WIDTH = N_HEADS * HEAD_DIM
CHUNK = 64
NORM_EPS = 1e-6
NEG_INF = -1e30
ROPE_THETA = 10000.0
CONV_W = 4
NSA_GROUPS = 2
CMP_LEN = 32
CMP_STRIDE = 16
SEL_BLOCK = 64
SEL_TOPK = 16
WINDOW = 512
FORCE_BONUS = 1e3
PEER_HEADS = 8
PEER_NKEYS = 128
PEER_TOPK = 16
PEER_KDIM = 128

LANES = 128
VMEM_LIMIT = 48 * 1024 * 1024

OFF_H, OFF_M, OFF_R, OFF_KD, OFF_NQ, OFF_V, OFF_MG, OFF_NG = 0, 1024, 2048, 3072, 3584, 3840, 4096, 4224
N_MAIN = 4352


def _cp(*sem):
    return pltpu.CompilerParams(dimension_semantics=sem, vmem_limit_bytes=VMEM_LIMIT)


def _dot(a, b):
    return jnp.dot(a, b, preferred_element_type=F32)


def _dot_nt(a, b):
    return lax.dot_general(a, b, (((1,), (1,)), ((), ())), preferred_element_type=F32)


def _dot_tn(a, b):
    return lax.dot_general(a, b, (((0,), (0,)), ((), ())), preferred_element_type=F32)


def _split3(x):
    hi = x.astype(BF16)
    r1 = x - hi.astype(F32)
    mid = r1.astype(BF16)
    lo = (r1 - mid.astype(F32)).astype(BF16)
    return hi, mid, lo


def _dot01_l(m01, x):
    hi, mid, lo = _split3(x)
    return _dot(m01, hi) + _dot(m01, mid) + _dot(m01, lo)


def _dot01_r(x, m01):
    hi, mid, lo = _split3(x)
    return _dot(hi, m01) + _dot(mid, m01) + _dot(lo, m01)


def _head_of_lane(shape, axis):
    return lax.broadcasted_iota(jnp.int32, shape, axis) // HEAD_DIM


def _block_ones(n, dtype=BF16):
    r = lax.broadcasted_iota(jnp.int32, (n, n), 0) // HEAD_DIM
    c = lax.broadcasted_iota(jnp.int32, (n, n), 1) // HEAD_DIM
    return (r == c).astype(dtype)


def _group_sum(x, ones_bd):
    hi = x.astype(BF16)
    lo = (x - hi.astype(F32)).astype(BF16)
    return _dot(hi, ones_bd) + _dot(lo, ones_bd)


def _head_rms(x, gain, ones_bd):
    ms = _group_sum(x * x, ones_bd) * (1.0 / HEAD_DIM)
    return x * lax.rsqrt(ms + NORM_EPS) * gain


def _sigmoid(x):
    return 1.0 / (1.0 + jnp.exp(-x))


def _silu(x):
    return x * _sigmoid(x)


def _log_sigmoid(x):
    return jnp.minimum(x, 0.0) - jnp.log(1.0 + jnp.exp(-jnp.abs(x)))


def _stack_heads(x, n_heads=N_HEADS):
    hl = _head_of_lane(x.shape, 1)
    return jnp.concatenate([jnp.where(hl == h, x, jnp.zeros_like(x)) for h in range(n_heads)], axis=0)


def _unstack_heads(r, c, n_heads=N_HEADS):
    hl = _head_of_lane((c, r.shape[1]), 1)
    out = jnp.zeros((c, r.shape[1]), F32)
    for h in range(n_heads):
        out = jnp.where(hl == h, r[h * c:(h + 1) * c, :], out)
    return out


def _rope(x, cos_t, sin_t):
    n = x.shape[1]
    first = (lax.broadcasted_iota(jnp.int32, x.shape, 1) % HEAD_DIM) < (HEAD_DIM // 2)
    partner = jnp.where(first, pltpu.roll(x, n - HEAD_DIM // 2, 1), pltpu.roll(x, HEAD_DIM // 2, 1))
    return x * cos_t + partner * sin_t


def _after_kernel(a_ref, dep_ref, o_ref):
    del dep_ref
    o_ref[...] = a_ref[...]


def _after(a, dep):
    if dep is None:
        return a
    a2 = a.reshape(1, a.size)
    out = pl.pallas_call(
        _after_kernel,
        in_specs=[pl.BlockSpec(a2.shape, lambda: (0, 0)), pl.BlockSpec(memory_space=pl.ANY)],
        out_specs=pl.BlockSpec(a2.shape, lambda: (0, 0)),
        out_shape=jax.ShapeDtypeStruct(a2.shape, a2.dtype),
        name="order_after",
    )(a2, dep)
    return out.reshape(a.shape)


def _norm_matmul_kernel(x_ref, g_ref, w_ref, o_ref, xn_ref, *, act):
    @pl.when(pl.program_id(1) == 0)
    def _():
        x = x_ref[...]
        ms = jnp.mean(x * x, axis=-1, keepdims=True)
        xn_ref[...] = (x * lax.rsqrt(ms + NORM_EPS) * g_ref[...]).astype(BF16)

    y = _dot(xn_ref[...], w_ref[...])
    if act == "sigmoid":
        y = _sigmoid(y)
    o_ref[...] = y.astype(o_ref.dtype)


def _norm_matmul(x, g, w, *, act=None, out_dtype=F32, tm=1024, tn=2176):
    t, d = x.shape
    w3 = w if w.ndim == 3 else w[None]
    n_per = w3.shape[2]
    tm = min(tm, t)
    tn = next(c for c in (tn, 2048, 1024, 512, 256, 128) if n_per % c == 0)
    per = n_per // tn
    n = w3.shape[0] * n_per
    assert t % tm == 0
    return pl.pallas_call(
        functools.partial(_norm_matmul_kernel, act=act),
        grid=(t // tm, n // tn),
        in_specs=[pl.BlockSpec((tm, d), lambda i, j: (i, 0)),
                  pl.BlockSpec((1, d), lambda i, j: (0, 0)),
                  pl.BlockSpec((None, d, tn), lambda i, j: (j // per, 0, j % per))],
        out_specs=pl.BlockSpec((tm, tn), lambda i, j: (i, j)),
        out_shape=jax.ShapeDtypeStruct((t, n), out_dtype),
        scratch_shapes=[pltpu.VMEM((tm, d), BF16)],
        compiler_params=_cp("parallel", "arbitrary"),
        name="norm_matmul",
    )(x, g.reshape(1, d), w3)


def _merge_kernel(x_ref, gate_ref, oh_ref, om_ref, on_ref, or_ref, wup_ref, wout_ref, o_ref):
    d = x_ref.shape[1]
    acc = None
    for m, r in enumerate((oh_ref, om_ref, on_ref, or_ref)):
        up = _dot(r[...].astype(BF16), wup_ref[m])
        term = gate_ref[:, m * d:(m + 1) * d].astype(F32) * up
        acc = term if acc is None else acc + term
    o_ref[...] = x_ref[...] + _dot(acc.astype(BF16), wout_ref[...])


def _merge(x, gates, outs, w_up, w_out, tm=512):
    t, d = x.shape
    tm = min(tm, t)
    mix = pl.BlockSpec((tm, MIX_WIDTH), lambda i: (i, 0))
    return pl.pallas_call(
        _merge_kernel,
        grid=(t // tm,),
        in_specs=[pl.BlockSpec((tm, d), lambda i: (i, 0)),
                  pl.BlockSpec((tm, 4 * d), lambda i: (i, 0)),
                  mix, mix, mix, mix,
                  pl.BlockSpec((4, MIX_WIDTH, d), lambda i: (0, 0, 0)),
                  pl.BlockSpec((d, d), lambda i: (0, 0))],
        out_specs=pl.BlockSpec((tm, d), lambda i: (i, 0)),
        out_shape=jax.ShapeDtypeStruct((t, d), F32),
        compiler_params=_cp("parallel"),
        name="merge",
    )(x, gates, *outs, w_up, w_out)


REC_BLOCK = 256


def _chunk_consts():
    t = lax.broadcasted_iota(jnp.int32, (CHUNK, CHUNK), 0)
    s = lax.broadcasted_iota(jnp.int32, (CHUNK, CHUNK), 1)
    return t, s


def _hgrn_levels():
    t = np.arange(CHUNK)
    rows = []
    masks = []
    h = CHUNK // 2
    while h >= 1:
        ref = (t // (2 * h)) * (2 * h) + h
        p = np.zeros((CHUNK, CHUNK), np.float32)
        p[t, np.minimum(ref, CHUNK - 1)] = 1.0
        rows.append(p)
        same = (t[:, None] // (2 * h)) == (t[None, :] // (2 * h))
        m = same & ((t[:, None] // h) % 2 == 1) & ((t[None, :] // h) % 2 == 0)
        masks.append(m.astype(np.float32))
        h //= 2
    masks.append(np.eye(CHUNK, dtype=np.float32))
    return np.concatenate(rows, 0), np.stack(masks, 0)


def _hgrn_kernel(p_ref, lb_ref, g_ref, psel_ref, lmask_ref, o_ref, st_ref):
    @pl.when(pl.program_id(1) == 0)
    def _():
        st_ref[...] = jnp.zeros_like(st_ref)

    c = CHUNK
    w = MIX_WIDTH
    ones_bd = _block_ones(w)
    bd_mask = _block_ones(w, F32)
    tri = (lax.broadcasted_iota(jnp.int32, (c, c), 0) >= lax.broadcasted_iota(jnp.int32, (c, c), 1)).astype(BF16)
    psel = psel_ref[...]
    n_lv = lmask_ref.shape[0]
    log_lb, log_1mlb, one_mlb = lb_ref[0:1, :], lb_ref[1:2, :], lb_ref[2:3, :]
    gain = g_ref[...]

    def chunk(ci, carry):
        r0 = pl.multiple_of(ci * c, c)
        q = _silu(p_ref[pl.ds(r0, c), 0:w])
        fl = p_ref[pl.ds(r0, c), w:2 * w]
        v = p_ref[pl.ds(r0, c), 2 * w:3 * w]
        gp = p_ref[pl.ds(r0, c), 3 * w:4 * w]
        a1 = jnp.broadcast_to(log_lb, fl.shape)
        a2 = log_1mlb + _log_sigmoid(fl)
        mx = jnp.maximum(a1, a2)
        log_f = mx + jnp.log(jnp.exp(a1 - mx) + jnp.exp(a2 - mx))
        k = one_mlb * _sigmoid(-fl)
        b = _dot01_l(tri, log_f)
        bref = _dot01_l(psel, b)
        vb = v.astype(BF16)
        a = jnp.zeros((N_HEADS * c, c), F32)
        for lv in range(n_lv):
            if lv < n_lv - 1:
                br = bref[lv * c:(lv + 1) * c, :]
                qs = q * jnp.exp(jnp.minimum(b - br, 0.0))
                ks = k * jnp.exp(jnp.minimum(br - b, 0.0))
            else:
                qs, ks = q, k
            s_lv = _dot_nt(_stack_heads(qs).astype(BF16), ks.astype(BF16))
            a = a + jnp.concatenate([lmask_ref[lv]] * N_HEADS, axis=0) * s_lv
        o = _unstack_heads(_dot(a.astype(BF16), vb), c)
        st = st_ref[...]
        o = o + _dot_nt((q * jnp.exp(b)).astype(BF16), st.astype(BF16))
        b_last = b[c - 1:c, :]
        kb = k * jnp.exp(b_last - b)
        st_ref[...] = st * jnp.exp(b_last) + bd_mask * _dot_tn(vb, kb.astype(BF16))
        y = _head_rms(o, gain, ones_bd) * _silu(gp)
        o_ref[pl.ds(r0, c), :] = y
        return carry

    lax.fori_loop(0, p_ref.shape[0] // c, chunk, 0)


def _hgrn(proj, lb_rows, gain, bsz, seq):
    psel, lmask = _hgrn_levels()
    tb = min(REC_BLOCK, seq)
    nb = seq // tb
    return pl.pallas_call(
        _hgrn_kernel,
        grid=(bsz, nb),
        in_specs=[pl.BlockSpec((tb, 4 * MIX_WIDTH), lambda b, i: (b * nb + i, OFF_H // (4 * MIX_WIDTH))),
                  pl.BlockSpec((8, MIX_WIDTH), lambda b, i: (0, 0)),
                  pl.BlockSpec((1, MIX_WIDTH), lambda b, i: (0, 0)),
                  pl.BlockSpec(psel.shape, lambda b, i: (0, 0)),
                  pl.BlockSpec(lmask.shape, lambda b, i: (0, 0, 0))],
        out_specs=pl.BlockSpec((tb, MIX_WIDTH), lambda b, i: (b * nb + i, 0)),
        out_shape=jax.ShapeDtypeStruct((bsz * seq, MIX_WIDTH), F32),
        scratch_shapes=[pltpu.VMEM((MIX_WIDTH, MIX_WIDTH), F32)],
        compiler_params=_cp("parallel", "arbitrary"),
        name="hgrn2",
    )(proj, lb_rows, gain.reshape(1, MIX_WIDTH), jnp.asarray(psel, BF16), jnp.asarray(lmask, F32))


def _ret_kernel(p_ref, cos_ref, sin_ref, dec_ref, decin_ref, g_ref, o_ref, st_ref):
    @pl.when(pl.program_id(1) == 0)
    def _():
        st_ref[...] = jnp.zeros_like(st_ref)

    c = CHUNK
    w = MIX_WIDTH
    ones_bd = _block_ones(w)
    bd_mask = _block_ones(w, F32)
    gain = g_ref[...]
    dec_q = dec_ref[0:c, :]
    dec_k = dec_ref[c:2 * c, :]
    dec_state = dec_ref[2 * c:2 * c + 1, :]
    dec_in = decin_ref[...]

    def chunk(ci, carry):
        r0 = pl.multiple_of(ci * c, c)
        cos_t = cos_ref[pl.ds(r0, c), :]
        sin_t = sin_ref[pl.ds(r0, c), :]
        q = _rope(p_ref[pl.ds(r0, c), 0:w], cos_t, sin_t)
        k = _rope(p_ref[pl.ds(r0, c), w:2 * w], cos_t, sin_t) * (HEAD_DIM ** -0.5)
        v = p_ref[pl.ds(r0, c), 2 * w:3 * w]
        gp = p_ref[pl.ds(r0, c), 3 * w:4 * w]
        vb = v.astype(BF16)
        a = _dot_nt(_stack_heads(q).astype(BF16), k.astype(BF16)) * dec_in
        o = _unstack_heads(_dot(a.astype(BF16), vb), c)
        st = st_ref[...]
        o = o + _dot_nt(q.astype(BF16), st.astype(BF16)) * dec_q
        st_ref[...] = st * dec_state + bd_mask * _dot_tn(vb, (k * dec_k).astype(BF16))
        o_ref[pl.ds(r0, c), :] = _head_rms(o, gain, ones_bd) * _silu(gp)
        return carry

    lax.fori_loop(0, p_ref.shape[0] // c, chunk, 0)


def _ret_consts():
    log_gamma = np.log1p(-np.exp2(-5.0 - np.arange(N_HEADS, dtype=np.float64)))
    t = np.arange(CHUNK, dtype=np.float64)
    lane_h = np.arange(MIX_WIDTH) // HEAD_DIM
    dec_q = np.exp(log_gamma[lane_h][None, :] * (t[:, None] + 1.0))
    dec_k = np.exp(log_gamma[lane_h][None, :] * (CHUNK - 1.0 - t[:, None]))
    dec_state = np.exp(log_gamma[lane_h] * CHUNK)[None, :]
    dec = np.concatenate([dec_q, dec_k, np.broadcast_to(dec_state, (8, MIX_WIDTH))], 0)
    diff = t[:, None] - t[None, :]
    dec_in = np.concatenate([np.where(diff >= 0, np.exp(log_gamma[h] * diff), 0.0) for h in range(N_HEADS)], 0)
    return dec.astype(np.float32), dec_in.astype(np.float32)


def _ret(proj, cos4, sin4, gain, bsz, seq):
    dec, dec_in = _ret_consts()
    tb = min(REC_BLOCK, seq)
    nb = seq // tb
    return pl.pallas_call(
        _ret_kernel,
        grid=(bsz, nb),
        in_specs=[pl.BlockSpec((tb, 4 * MIX_WIDTH), lambda b, i: (b * nb + i, OFF_R // (4 * MIX_WIDTH))),
                  pl.BlockSpec((tb, MIX_WIDTH), lambda b, i: (i, 0)),
                  pl.BlockSpec((tb, MIX_WIDTH), lambda b, i: (i, 0)),
                  pl.BlockSpec(dec.shape, lambda b, i: (0, 0)),
                  pl.BlockSpec(dec_in.shape, lambda b, i: (0, 0)),
                  pl.BlockSpec((1, MIX_WIDTH), lambda b, i: (0, 0))],
        out_specs=pl.BlockSpec((tb, MIX_WIDTH), lambda b, i: (b * nb + i, 0)),
        out_shape=jax.ShapeDtypeStruct((bsz * seq, MIX_WIDTH), F32),
        scratch_shapes=[pltpu.VMEM((MIX_WIDTH, MIX_WIDTH), F32)],
        compiler_params=_cp("parallel", "arbitrary"),
        name="retention",
    )(proj, cos4, sin4, jnp.asarray(dec), jnp.asarray(dec_in), gain.reshape(1, MIX_WIDTH))


def _hgrn_lb_rows(lb):
    lb = lb.astype(F32)
    rows = jnp.stack([jnp.log(lb), jnp.log1p(-lb), 1.0 - lb], 0)
    return jnp.concatenate([rows, jnp.zeros((5, lb.shape[0]), F32)], 0)


def _rope_lane_tables(seq):
    inv = 1.0 / (ROPE_THETA ** (jnp.arange(0, HEAD_DIM, 2, dtype=F32) / HEAD_DIM))
    ang = jnp.arange(seq, dtype=F32)[:, None] * inv[None, :]
    cos, sin = jnp.cos(ang), jnp.sin(ang)
    cos_t = jnp.tile(cos, (1, LANES // (HEAD_DIM // 2)))
    sin_t = jnp.tile(jnp.concatenate([-sin, sin], axis=1), (1, LANES // HEAD_DIM))
    return cos_t, sin_t


def _expand_heads(cols, shape):
    hl = _head_of_lane(shape, 1)
    out = jnp.broadcast_to(cols[-1], shape)
    for h in range(len(cols) - 2, -1, -1):
        out = jnp.where(hl == h, jnp.broadcast_to(cols[h], shape), out)
    return out


def _mlstm_kernel(p_ref, gcol_ref, grow_ref, cw_ref, cb_ref, gbr_ref, gbc_ref, g_ref, o_ref,
                  ct_ref, n_ref, m_ref, hist_ref, cbuf_ref, qk_ref):
    c = CHUNK
    w = MIX_WIDTH
    tb = p_ref.shape[0]

    @pl.when(pl.program_id(1) == 0)
    def _():
        ct_ref[...] = jnp.zeros_like(ct_ref)
        n_ref[...] = jnp.zeros_like(n_ref)
        m_ref[...] = jnp.zeros_like(m_ref)
        hist_ref[...] = jnp.zeros_like(hist_ref)

    cbuf_ref[0:8, :] = hist_ref[...]
    cbuf_ref[8:, :] = p_ref[:, 0:2 * w]
    hist_ref[...] = p_ref[tb - 8:tb, 0:2 * w]
    acc = jnp.broadcast_to(cb_ref[...], (tb, 2 * w))
    for j in range(CONV_W):
        acc = acc + cw_ref[j:j + 1, :] * cbuf_ref[pl.ds(8 - (CONV_W - 1) + j, tb), :]
    qk_ref[...] = _silu(acc)

    ones_bd = _block_ones(w)
    bd_mask = _block_ones(w, F32)
    ti = lax.broadcasted_iota(jnp.int32, (c, c), 0)
    si = lax.broadcasted_iota(jnp.int32, (c, c), 1)
    causal = ti >= si
    tri = causal.astype(BF16)
    tri_t = (ti <= si).astype(BF16)
    gain = g_ref[...]
    ones_ext = jnp.ones((c, LANES), BF16)

    def chunk(ci, carry):
        r0 = pl.multiple_of(ci * c, c)
        q = qk_ref[pl.ds(r0, c), 0:w]
        k = qk_ref[pl.ds(r0, c), w:2 * w] * (HEAD_DIM ** -0.5)
        v = p_ref[pl.ds(r0, c), 2 * w:3 * w]
        op = p_ref[pl.ds(r0, c), 3 * w:4 * w]
        gc = gcol_ref[pl.ds(r0, c), :] + gbr_ref[...]
        gr = grow_ref[ci] + gbc_ref[...]
        b_c = _dot01_l(tri, _log_sigmoid(gc))
        b_r = _dot01_r(_log_sigmoid(gr), tri_t)
        wd, s_inter, em, wk, decay = [], [], [], [], []
        for h in range(N_HEADS):
            bc = b_c[:, N_HEADS + h:N_HEADS + h + 1]
            lic = gc[:, h:h + 1]
            br = b_r[N_HEADS + h:N_HEADS + h + 1, :]
            lir = gr[h:h + 1, :]
            dmat = jnp.where(causal, bc - br + lir, -jnp.inf)
            m_prev = m_ref[h:h + 1, 0:1]
            inter = bc + m_prev
            mrow = jnp.maximum(inter, jnp.max(dmat, axis=1, keepdims=True))
            wd.append(jnp.exp(dmat - mrow))
            s_inter.append(jnp.exp(inter - mrow))
            em.append(jnp.exp(-mrow))
            b_last = br[:, c - 1:c]
            m_new = jnp.maximum(b_last + m_prev, jnp.max(b_last - br + lir, axis=1, keepdims=True))
            wk.append(jnp.exp(b_last - bc + lic - m_new))
            decay.append(jnp.exp(b_last + m_prev - m_new))
            m_ref[h:h + 1, :] = jnp.broadcast_to(m_new, (1, LANES))
        s_inter_l = _expand_heads(s_inter, (c, w))
        em_l = _expand_heads(em, (c, w))
        wk_l = _expand_heads(wk, (c, w))
        decay_l = _expand_heads(decay, (1, w))
        qk = _dot_nt(_stack_heads(q).astype(BF16), k.astype(BF16))
        wmat = jnp.concatenate(wd, axis=0) * qk
        vb = v.astype(BF16)
        r = _dot(wmat.astype(BF16), jnp.concatenate([vb, ones_ext], axis=1))
        num_intra = _unstack_heads(r[:, 0:w], c)
        rs_l = _expand_heads([r[h * c:(h + 1) * c, w:w + 1] for h in range(N_HEADS)], (c, w))
        ct = ct_ref[...]
        nrow = n_ref[0:1, :]
        num = s_inter_l * _dot_nt(q.astype(BF16), ct.astype(BF16)) + num_intra
        den = s_inter_l * _group_sum(q * nrow, ones_bd) + rs_l
        hval = num / jnp.maximum(jnp.abs(den), em_l)
        kw = wk_l * k
        ct_ref[...] = ct * decay_l + bd_mask * _dot_tn(vb, kw.astype(BF16))
        n_ref[0:1, :] = nrow * decay_l + jnp.sum(kw, axis=0, keepdims=True)
        o_ref[pl.ds(r0, c), :] = _head_rms(hval, gain, ones_bd) * _sigmoid(op)
        return carry

    lax.fori_loop(0, tb // c, chunk, 0)


def _mlstm(proj, conv_w, conv_b, gate_b, gain, bsz, seq):
    t = bsz * seq
    w = MIX_WIDTH
    tb = min(REC_BLOCK, seq)
    nb = seq // tb
    ncb = tb // CHUNK
    grow = proj[:, OFF_MG:OFF_MG + 8].reshape(t // CHUNK, CHUNK, 8).transpose(0, 2, 1)
    gb_row = jnp.zeros((1, LANES), F32).at[0, 0:8].set(gate_b.astype(F32))
    gb_col = gate_b.astype(F32).reshape(8, 1)
    return pl.pallas_call(
        _mlstm_kernel,
        grid=(bsz, nb),
        in_specs=[pl.BlockSpec((tb, 4 * w), lambda b, i: (b * nb + i, OFF_M // (4 * w))),
                  pl.BlockSpec((tb, LANES), lambda b, i: (b * nb + i, OFF_MG // LANES)),
                  pl.BlockSpec((ncb, 8, CHUNK), lambda b, i: (b * nb + i, 0, 0)),
                  pl.BlockSpec((CONV_W, 2 * w), lambda b, i: (0, 0)),
                  pl.BlockSpec((1, 2 * w), lambda b, i: (0, 0)),
                  pl.BlockSpec((1, LANES), lambda b, i: (0, 0)),
                  pl.BlockSpec((8, 1), lambda b, i: (0, 0)),
                  pl.BlockSpec((1, w), lambda b, i: (0, 0))],
        out_specs=pl.BlockSpec((tb, w), lambda b, i: (b * nb + i, 0)),
        out_shape=jax.ShapeDtypeStruct((t, w), F32),
        scratch_shapes=[pltpu.VMEM((w, w), F32), pltpu.VMEM((8, w), F32), pltpu.VMEM((8, LANES), F32),
                        pltpu.VMEM((8, 2 * w), F32), pltpu.VMEM((tb + 8, 2 * w), F32),
                        pltpu.VMEM((tb, 2 * w), F32)],
        compiler_params=_cp("parallel", "arbitrary"),
        name="mlstm",
    )(proj, proj, grow, conv_w.astype(F32), conv_b.astype(F32).reshape(1, 2 * w), gb_row, gb_col,
      gain.reshape(1, w))


NSA_TQ = 128
NSA_KC = 512
GW = 2 * HEAD_DIM


def _nsa_prep_kernel(pq_ref, pk_ref, pv_ref, cos_ref, sin_ref, qg_ref, kg_ref,
                     qn_ref, qr_ref, ks_ref, kw_ref, vst_ref, vwt_ref):
    w = MIX_WIDTH
    for src, dst in ((pv_ref[:, 0:GW], vst_ref), (pv_ref[:, GW:2 * GW], vwt_ref)):
        vt = src.T
        tk = dst.shape[4]
        for g in range(NSA_GROUPS):
            rows = vt[g * HEAD_DIM:(g + 1) * HEAD_DIM, :]
            dup = jnp.concatenate([rows, rows], axis=0).astype(BF16)
            for j in range(dst.shape[2]):
                dst[0, g, j] = dup[:, j * tk:(j + 1) * tk]
    ones_bd = _block_ones(w)
    cos_t, sin_t = cos_ref[...], sin_ref[...]
    scale = HEAD_DIM ** -0.5
    qh = _head_rms(pq_ref[...], qg_ref[...], ones_bd)
    qn_ref[...] = (qh * scale).astype(BF16)
    qr_ref[...] = (_rope(qh, cos_t, sin_t) * scale).astype(BF16)
    ks_ref[...] = _rope(_head_rms(pk_ref[:, 0:w], kg_ref[1:2, :], ones_bd), cos_t, sin_t).astype(BF16)
    kw_ref[...] = _rope(_head_rms(pk_ref[:, w:2 * w], kg_ref[2:3, :], ones_bd), cos_t, sin_t).astype(BF16)


def _nsa_prep(proj, cos4, sin4, qnorm_g, knorm_g, bsz, seq):
    t = bsz * seq
    w = MIX_WIDTH
    tm = min(NSA_KC, seq)
    tq = min(NSA_TQ, seq)
    ns = seq // tm
    qg = jnp.tile(qnorm_g.astype(F32), w // HEAD_DIM).reshape(1, w)
    kg = jnp.concatenate([jnp.tile(knorm_g.astype(F32), (1, w // HEAD_DIM)), jnp.zeros((5, w), F32)], axis=0)
    out = jax.ShapeDtypeStruct((t, w), BF16)
    row = pl.BlockSpec((tm, w), lambda i: (i, 0))
    return pl.pallas_call(
        _nsa_prep_kernel,
        grid=(t // tm,),
        in_specs=[pl.BlockSpec((tm, w), lambda i: (i, OFF_NQ // w)),
                  pl.BlockSpec((tm, 2 * w), lambda i: (i, OFF_KD // (2 * w))),
                  pl.BlockSpec((tm, 2 * GW), lambda i: (i, OFF_V // (2 * GW))),
                  pl.BlockSpec((tm, w), lambda i: (i % ns, 0)),
                  pl.BlockSpec((tm, w), lambda i: (i % ns, 0)),
                  pl.BlockSpec((1, w), lambda i: (0, 0)),
                  pl.BlockSpec((8, w), lambda i: (0, 0))],
        out_specs=[row, row, row, row,
                   pl.BlockSpec((1, NSA_GROUPS, 1, GW, tm), lambda i: (i // ns, 0, i % ns, 0, 0)),
                   pl.BlockSpec((1, NSA_GROUPS, tm // tq, GW, tq), lambda i: (i // ns, 0, i % ns, 0, 0))],
        out_shape=[out, out, out, out,
                   jax.ShapeDtypeStruct((bsz, NSA_GROUPS, seq // tm, GW, tm), BF16),
                   jax.ShapeDtypeStruct((bsz, NSA_GROUPS, seq // tq, GW, tq), BF16)],
        compiler_params=_cp("parallel"),
        name="nsa_prep",
    )(proj, proj, proj, cos4, sin4, qg, kg)


def _nsa_cmp_kernel(xr_ref, pe_ref, w0_ref, w1_ref, kg_ref, ovt_ref, qn_ref, ocmp_ref, sel_ref,
                    kc_ref, vc_ref, v_ref, *, n_top):
    tq = qn_ref.shape[0]
    nr = xr_ref.shape[0]
    nsel = sel_ref.shape[2]
    w = MIX_WIDTH

    @pl.when(pl.program_id(1) == 0)
    def _():
        xr = xr_ref[...]
        y0 = _dot((xr + pe_ref[0]).astype(BF16), w0_ref[...])
        y1 = _dot((xr + pe_ref[1]).astype(BF16), w1_ref[...])
        kv = y0 + pltpu.roll(y1, nr - 1, 0)
        kc_ref[...] = _head_rms(kv[:, 0:w], kg_ref[...], _block_ones(w)).astype(BF16)
        vc_ref[...] = kv[:, w:2 * w].astype(BF16)

    pos0 = pl.program_id(1) * tq
    hl = _head_of_lane((tq, GW), 1)
    pos_r = pos0 + lax.broadcasted_iota(jnp.int32, (tq, nr), 0)
    valid = lax.broadcasted_iota(jnp.int32, (tq, nr), 1) * CMP_STRIDE + (CMP_LEN - 1) <= pos_r
    pos_c = pos0 + lax.broadcasted_iota(jnp.int32, (nr, tq), 1)
    valid_t = lax.broadcasted_iota(jnp.int32, (nr, tq), 0) * CMP_STRIDE + (CMP_LEN - 1) <= pos_c
    jrow = lax.broadcasted_iota(jnp.int32, (nsel, tq), 0)
    cur = (pos0 + lax.broadcasted_iota(jnp.int32, (nsel, tq), 1)) // SEL_BLOCK
    forced = (jrow == 0) | (jrow == cur) | (jrow == cur - 1)
    ovt = ovt_ref[...]

    for g in range(NSA_GROUPS):
        qg = qn_ref[:, g * GW:(g + 1) * GW]
        kg = kc_ref[:, g * GW:(g + 1) * GW]
        vg = vc_ref[:, g * GW:(g + 1) * GW]
        o_g = jnp.zeros((tq, GW), F32)
        pt_sum = jnp.zeros((nr, tq), F32)
        for hh in range(2):
            qm = jnp.where(hl == hh, qg, jnp.zeros_like(qg))
            s = jnp.where(valid, _dot_nt(qm, kg), NEG_INF)
            e = jnp.exp(s - jnp.max(s, axis=1, keepdims=True))
            p = jnp.where(valid, e / jnp.sum(e, axis=1, keepdims=True), 0.0)
            o_g = jnp.where(hl == hh, _dot(p.astype(BF16), vg), o_g)
            st = jnp.where(valid_t, _dot_nt(kg, qm), NEG_INF)
            et = jnp.exp(st - jnp.max(st, axis=0, keepdims=True))
            pt_sum = pt_sum + jnp.where(valid_t, et / jnp.sum(et, axis=0, keepdims=True), 0.0)
        ocmp_ref[:, g * GW:(g + 1) * GW] = o_g
        p_hi = pt_sum.astype(BF16)
        p_lo = (pt_sum - p_hi.astype(F32)).astype(BF16)
        imp = _dot(ovt, p_hi) + _dot(ovt, p_lo)
        val = jnp.where(jrow <= cur, imp + FORCE_BONUS * forced.astype(F32), NEG_INF)
        v_ref[...] = val

        def rank(jp, cnt):
            row = v_ref[pl.ds(jp, 1), :]
            tie = jnp.where(jrow > jp, 1.0, 0.0)
            return cnt + jnp.where(row > val, 1.0, jnp.where(row == val, tie, 0.0))

        cnt = lax.fori_loop(0, nsel, rank, jnp.zeros((nsel, tq), F32))
        sel_ref[0, g] = ((cnt < n_top) & (jrow <= cur)).astype(F32)


def _nsa_cmp_weights(cmp_pe, cmp_w):
    half = CMP_LEN // 2
    wl = cmp_w.astype(F32).reshape(2, 2, half, HEAD_DIM, HEAD_DIM)
    eye2 = jnp.eye(2, dtype=F32)
    w2 = jnp.einsum('kardz,kK,gG,h->arkgdKGhz', wl, eye2, eye2, jnp.ones((2,), F32))
    w2 = w2.reshape(2, half * 4 * HEAD_DIM, 8 * HEAD_DIM)
    pl_ = cmp_pe.astype(F32).reshape(2, 2, half, HEAD_DIM)
    pe2 = jnp.broadcast_to(pl_.transpose(1, 2, 0, 3)[:, :, :, None, :], (2, half, 2, 2, HEAD_DIM))
    return w2.astype(BF16), pe2.reshape(2, 1, half * 4 * HEAD_DIM)


def _nsa_cmp(kcvc, qn, cmp_pe, cmp_w, knorm0, bsz, seq, tq=512):
    t = bsz * seq
    w = MIX_WIDTH
    tq = min(tq, seq)
    nq = seq // tq
    nr = seq // CMP_STRIDE
    nsel = seq // SEL_BLOCK
    n_top = min(SEL_TOPK, nsel)
    w2, pe2 = _nsa_cmp_weights(cmp_pe, cmp_w)
    xr = kcvc.reshape(t // CMP_STRIDE, CMP_STRIDE * w)
    kg = jnp.tile(knorm0.astype(F32), w // HEAD_DIM).reshape(1, w)
    n_i = np.arange(nr)[:, None] * CMP_STRIDE
    j_i = np.arange(nsel)[None, :] * SEL_BLOCK
    ov = ((n_i < j_i + SEL_BLOCK) & (n_i + CMP_LEN > j_i)).astype(np.float32)
    ov[nr - 1, :] = 0.0
    kin = CMP_STRIDE * w
    return pl.pallas_call(
        functools.partial(_nsa_cmp_kernel, n_top=n_top),
        grid=(bsz, nq),
        in_specs=[pl.BlockSpec((nr, kin), lambda b, i: (b, 0)),
                  pl.BlockSpec((2, 1, kin), lambda b, i: (0, 0, 0)),
                  pl.BlockSpec((None, kin, 2 * w), lambda b, i: (0, 0, 0)),
                  pl.BlockSpec((None, kin, 2 * w), lambda b, i: (1, 0, 0)),
                  pl.BlockSpec((1, w), lambda b, i: (0, 0)),
                  pl.BlockSpec((nsel, nr), lambda b, i: (0, 0)),
                  pl.BlockSpec((tq, w), lambda b, i: (b * nq + i, 0))],
        out_specs=[pl.BlockSpec((tq, w), lambda b, i: (b * nq + i, 0)),
                   pl.BlockSpec((1, NSA_GROUPS, nsel, tq), lambda b, i: (b, 0, 0, i))],
        out_shape=[jax.ShapeDtypeStruct((t, w), F32),
                   jax.ShapeDtypeStruct((bsz, NSA_GROUPS, nsel, seq), F32)],
        scratch_shapes=[pltpu.VMEM((nr, w), BF16), pltpu.VMEM((nr, w), BF16), pltpu.VMEM((nsel, tq), F32)],
        compiler_params=_cp("parallel", "arbitrary"),
        name="nsa_cmp",
    )(xr, pe2, w2, w2, kg, jnp.asarray(ov.T, BF16), qn)


def _nsa_attn_kernel(qr_ref, ks_ref, kw_ref, vs_ref, vw_ref, sel_ref, ocmp_ref, gate_ref, o_ref, *, kc, wt):
    tq = qr_ref.shape[0]
    i = pl.program_id(1)
    hl = _head_of_lane((tq, GW), 1)
    nbk = kc // SEL_BLOCK
    groups = range(NSA_GROUPS)

    def stacked_q(g):
        q = qr_ref[:, g * GW:(g + 1) * GW]
        return jnp.concatenate([jnp.where(hl == 0, q, jnp.zeros_like(q)), jnp.where(hl == 1, q, jnp.zeros_like(q))],
                               axis=0)

    qs_all = [stacked_q(g) for g in groups]

    def lane_qpos(rows):
        return i * tq + lax.broadcasted_iota(jnp.int32, (rows, 2 * tq), 1) % tq

    def finish(acc, l):
        ot = (acc / l).T
        return jnp.where(hl == 0, ot[0:tq, :], ot[tq:2 * tq, :])

    qpos_s = lane_qpos(kc)
    krow_s = lax.broadcasted_iota(jnp.int32, (kc, 2 * tq), 0)

    def sel_step(g, c, carry, diagonal):
        m, l, acc = carry
        k0 = pl.multiple_of(c * kc, kc)
        st = _dot_nt(ks_ref[pl.ds(k0, kc), g * GW:(g + 1) * GW], qs_all[g])
        srows = sel_ref[0, g, pl.ds(pl.multiple_of(c * nbk, nbk), nbk), :]
        srows = jnp.concatenate([srows, srows], axis=1)
        smask = jnp.concatenate([jnp.broadcast_to(srows[r:r + 1, :], (SEL_BLOCK, 2 * tq)) for r in range(nbk)],
                                axis=0)
        msk = smask > 0.5
        if diagonal:
            msk = msk & (k0 + krow_s <= qpos_s)
        st = jnp.where(msk, st, NEG_INF)
        m_new = jnp.maximum(m, jnp.max(st, axis=0, keepdims=True))
        p = jnp.exp(st - m_new)
        alpha = jnp.exp(m - m_new)
        l = l * alpha + jnp.sum(p, axis=0, keepdims=True)
        acc = acc * alpha + _dot(vs_ref[0, g, c], p.astype(BF16))
        return m_new, l, acc

    def sel_body(c, carries, diagonal):
        return tuple(sel_step(g, c, carries[g], diagonal) for g in groups)

    init = (jnp.full((1, 2 * tq), NEG_INF, F32), jnp.zeros((1, 2 * tq), F32), jnp.zeros((GW, 2 * tq), F32))
    n_before = (i * tq) // kc
    carries = lax.fori_loop(0, n_before, functools.partial(sel_body, diagonal=False), (init,) * NSA_GROUPS)
    carries = sel_body(n_before, carries, True)

    j0 = jnp.maximum(i - (wt - 1), 0)
    k0 = pl.multiple_of(j0 * tq, tq)
    span = wt * tq
    kpos = k0 + lax.broadcasted_iota(jnp.int32, (span, 2 * tq), 0)
    qpos_w = lane_qpos(span)
    wmask = (kpos <= qpos_w) & (kpos > qpos_w - WINDOW)
    gb = _sigmoid(gate_ref[...])
    for g in groups:
        _, l_s, acc_s = carries[g]
        o_sel = finish(acc_s, l_s)
        st = jnp.where(wmask, _dot_nt(kw_ref[pl.ds(k0, span), g * GW:(g + 1) * GW], qs_all[g]), NEG_INF)
        p = jnp.exp(st - jnp.max(st, axis=0, keepdims=True))
        vt = jnp.concatenate([vw_ref[0, g, j0 + r] for r in range(wt)], axis=1)
        o_win = finish(_dot(vt, p.astype(BF16)), jnp.sum(p, axis=0, keepdims=True))

        def gate(branch):
            cols = [gb[:, (2 * g + hh) * 3 + branch:(2 * g + hh) * 3 + branch + 1] for hh in range(2)]
            return _expand_heads(cols, (tq, GW))

        o_ref[:, g * GW:(g + 1) * GW] = (gate(0) * ocmp_ref[:, g * GW:(g + 1) * GW]
                                         + gate(1) * o_sel + gate(2) * o_win)


def _nsa_attn(proj, qr, ks, kw, vst, vwt, sel, o_cmp, bsz, seq):
    t = bsz * seq
    w = MIX_WIDTH
    tq = min(NSA_TQ, seq)
    nq = seq // tq
    nsel = seq // SEL_BLOCK
    kc = min(NSA_KC, seq)
    wt = min(WINDOW // tq + 1, nq)
    kspec = pl.BlockSpec((seq, w), lambda b, i: (b, 0))
    return pl.pallas_call(
        functools.partial(_nsa_attn_kernel, kc=kc, wt=wt),
        grid=(bsz, nq),
        in_specs=[pl.BlockSpec((tq, w), lambda b, i: (b * nq + i, 0)),
                  kspec, kspec,
                  pl.BlockSpec((1, NSA_GROUPS, seq // kc, GW, kc), lambda b, i: (b, 0, 0, 0, 0)),
                  pl.BlockSpec((1, NSA_GROUPS, nq, GW, tq), lambda b, i: (b, 0, 0, 0, 0)),
                  pl.BlockSpec((1, NSA_GROUPS, nsel, tq), lambda b, i: (b, 0, 0, i)),
                  pl.BlockSpec((tq, w), lambda b, i: (b * nq + i, 0)),
                  pl.BlockSpec((tq, LANES), lambda b, i: (b * nq + i, OFF_NG // LANES))],
        out_specs=pl.BlockSpec((tq, w), lambda b, i: (b * nq + i, 0)),
        out_shape=jax.ShapeDtypeStruct((t, w), F32),
        compiler_params=_cp("parallel", "arbitrary"),
        name="nsa_attn",
    )(qr, ks, kw, vst, vwt, sel, o_cmp, proj)


def _nsa(proj, kcvc, cos4, sin4, qnorm_g, knorm_g, cmp_pe, cmp_w, bsz, seq):
    qn, qr, ks, kw, vst, vwt = _nsa_prep(proj, cos4, sin4, qnorm_g, knorm_g, bsz, seq)
    o_cmp, sel = _nsa_cmp(kcvc, qn, cmp_pe, cmp_w, knorm_g[0], bsz, seq)
    return _nsa_attn(proj, qr, ks, kw, vst, vwt, sel, o_cmp, bsz, seq)


PEER_TT = 128
PEER_CT = 8
HALF_D = 512


SUBLANES = 8
CODE_BITS = 127
FAR_BELOW = -3.0e38


def _with_code(x, code):
    bits = lax.bitcast_convert_type(x, jnp.int32)
    return lax.bitcast_convert_type((bits & ~CODE_BITS) | code, F32)


def _split_code(x):
    bits = lax.bitcast_convert_type(x, jnp.int32)
    return lax.bitcast_convert_type(bits & ~CODE_BITS, F32), bits & CODE_BITS


def _sort16_desc(xs):
    xs = list(xs)
    n = len(xs)
    k = 2
    while k <= n:
        j = k // 2
        while j >= 1:
            for i in range(n):
                l = i ^ j
                if l > i:
                    hi, lo = jnp.maximum(xs[i], xs[l]), jnp.minimum(xs[i], xs[l])
                    xs[i], xs[l] = (hi, lo) if (i & k) == 0 else (lo, hi)
            j //= 2
        k *= 2
    return xs


def _merge16_desc(xs):
    xs = list(xs)
    j = len(xs) // 2
    while j >= 1:
        for i in range(len(xs)):
            l = i ^ j
            if l > i:
                xs[i], xs[l] = jnp.maximum(xs[i], xs[l]), jnp.minimum(xs[i], xs[l])
        j //= 2
    return xs


def _top16_columns(x):
    n = PEER_TOPK
    xs = _sort16_desc([x[SUBLANES * j:SUBLANES * (j + 1), :] for j in range(n)])
    shift = SUBLANES // 2
    while shift >= 1:
        rolled = [pltpu.roll(a, shift, 0) for a in xs]
        xs = _merge16_desc([jnp.maximum(xs[i], rolled[n - 1 - i]) for i in range(n)])
        shift //= 2
    return xs


_PEER_CAND_TILES = ((0, 0, 8), (0, 1, 8), (1, 0, 8), (2, 0, 5), (3, 0, 4), (4, 0, 3), (5, 0, 2), (6, 0, 2), (7, 0, 2))


ROUTE_HEADS_PER_STEP = 4


def _route_head(q_ref, key_ref, hh):
    tt = q_ref.shape[0]
    nk = PEER_NKEYS
    n = PEER_TOPK
    row = lax.broadcasted_iota(jnp.int32, (nk, tt), 0)
    sub = lax.broadcasted_iota(jnp.int32, (SUBLANES, tt), 0)
    vals, ids = [], []
    for p in range(2):
        c0 = (2 * hh + p) * PEER_KDIM
        st = _dot_nt(key_ref[hh, p], q_ref[:, c0:c0 + PEER_KDIM])
        top = [_split_code(a) for a in _top16_columns(_with_code(st, (nk - 1) - row))]
        vals.append([v for v, _ in top])
        ids.append([(nk - 1) - c for _, c in top])
    (v1, v2), (i1, i2) = vals, ids

    def stack(xs, lo):
        out = xs[lo]
        for s in range(1, SUBLANES):
            out = jnp.where(sub == s, xs[lo + s], out)
        return out

    v2t, i2t = (stack(v2, 0), stack(v2, SUBLANES)), (stack(i2, 0), stack(i2, SUBLANES))
    cand, cexp = [], []
    for a, tile, nvalid in _PEER_CAND_TILES:
        v = v1[a] + v2t[tile]
        cand.append(v if nvalid == SUBLANES else jnp.where(sub < nvalid, v, FAR_BELOW))
        cexp.append(i1[a] * nk + i2t[tile])
    cand.append(stack(v1, SUBLANES) + v2[0])
    cexp.append(stack(i1, SUBLANES) * nk + i2[0])
    n_tiles = len(cand)
    slot_code = [(nk - 1) - (c * SUBLANES + sub) for c in range(n_tiles)]
    coded = [_with_code(v, sc) for v, sc in zip(cand, slot_code)]
    coded += [jnp.full((SUBLANES, tt), FAR_BELOW, F32)] * (n - n_tiles)
    top = [_split_code(a) for a in _top16_columns(jnp.concatenate(coded, axis=0))]
    call = jnp.concatenate(cexp, axis=0)
    slot = (nk - 1) - lax.broadcasted_iota(jnp.int32, call.shape, 0)
    ex = [jnp.exp(v - top[0][0]) for v, _ in top]
    tot = ex[0]
    for k in range(1, n):
        tot = tot + ex[k]
    krow = lax.broadcasted_iota(jnp.int32, (n, tt), 0)
    e_tile = jnp.zeros((n, tt), F32)
    g_tile = jnp.zeros((n, tt), F32)
    for k in range(n):
        hit = slot == jnp.concatenate([top[k][1]] * n_tiles, axis=0)
        e_k = jnp.sum(jnp.where(hit, call, 0), axis=0, keepdims=True)
        e_tile = jnp.where(krow == k, e_k.astype(F32), e_tile)
        g_tile = jnp.where(krow == k, (ex[k] / tot)[0:1, :], g_tile)
    return e_tile, g_tile


def _peer_route_kernel(q_ref, key_ref, e_ref, g_ref, e_scr, g_scr):
    hps = key_ref.shape[0]
    tiles = [_route_head(q_ref, key_ref, hh) for hh in range(hps)]
    rows = hps * PEER_TOPK
    r0 = pl.multiple_of(pl.program_id(1) * rows, rows)
    e_scr[pl.ds(r0, rows), :] = jnp.concatenate([e for e, _ in tiles], axis=0)
    g_scr[pl.ds(r0, rows), :] = jnp.concatenate([g for _, g in tiles], axis=0)

    @pl.when(pl.program_id(1) == pl.num_programs(1) - 1)
    def _():
        e_ref[...] = e_scr[...].T.astype(jnp.int32)
        g_ref[...] = g_scr[...].T


def _peer_route(qp, keys):
    t = qp.shape[0]
    tt = min(PEER_TT, t)
    ne = PEER_HEADS * PEER_TOPK
    hps = ROUTE_HEADS_PER_STEP
    return pl.pallas_call(
        _peer_route_kernel,
        grid=(t // tt, PEER_HEADS // hps),
        in_specs=[pl.BlockSpec((tt, hps * 2 * PEER_KDIM), lambda i, h: (i, h)),
                  pl.BlockSpec((hps, 2, PEER_NKEYS, PEER_KDIM), lambda i, h: (h, 0, 0, 0))],
        out_specs=[pl.BlockSpec((tt, ne), lambda i, h: (i, 0)),
                   pl.BlockSpec((tt, ne), lambda i, h: (i, 0))],
        out_shape=[jax.ShapeDtypeStruct((t, ne), jnp.int32),
                   jax.ShapeDtypeStruct((t, ne), F32)],
        scratch_shapes=[pltpu.VMEM((ne, tt), F32), pltpu.VMEM((ne, tt), F32)],
        compiler_params=_cp("parallel", "arbitrary"),
        name="peer_route",
    )(qp, keys)


def _pack_tables_kernel(u_ref, v_ref, o_ref):
    def pack(x):
        lo = lax.bitcast_convert_type(x[:, 0:HALF_D].astype(BF16).astype(F32), jnp.int32)
        hi = lax.bitcast_convert_type(x[:, HALF_D:2 * HALF_D].astype(BF16).astype(F32), jnp.int32)
        return lax.shift_right_logical(lo, 16) | (hi & jnp.int32(-65536))

    o_ref[:, 0:HALF_D] = pack(u_ref[...])
    o_ref[:, HALF_D:2 * HALF_D] = pack(v_ref[...])


def _pack_tables(u_tabs, v_tabs, layer, tr=512):
    _, e, d = u_tabs.shape
    assert d == 2 * HALF_D
    spec_in = pl.BlockSpec((None, tr, d), lambda i: (layer, i, 0))
    spec = pl.BlockSpec((tr, d), lambda i: (i, 0))
    return pl.pallas_call(
        _pack_tables_kernel,
        grid=(e // tr,),
        in_specs=[spec_in, spec_in],
        out_specs=spec,
        out_shape=jax.ShapeDtypeStruct((e, d), jnp.int32),
        compiler_params=_cp("parallel"),
        name="peer_pack",
    )(u_tabs, v_tabs)


def _unpack_rows(wd):
    lo = lax.bitcast_convert_type(lax.shift_left(wd, 16), F32)
    hi = lax.bitcast_convert_type(lax.bitwise_and(wd, jnp.int32(-65536)), F32)
    return lo, hi


SC_WINDOW = 32


def _sc_gather(table, idx):
    from jax.experimental.pallas import tpu_sc as plsc
    n = idx.shape[0]
    width = table.shape[1]
    mesh = plsc.VectorSubcoreMesh(core_axis_name="core", subcore_axis_name="subcore")

    @functools.partial(pl.kernel, out_type=jax.ShapeDtypeStruct((n, width), table.dtype), mesh=mesh)
    def gather(tab_hbm, idx_hbm, out_hbm):
        def body(idx_vmem, out_vmem):
            pltpu.sync_copy(tab_hbm.at[idx_vmem.at[0, pl.ds(0, SC_WINDOW)]], out_vmem)

        pltpu.emit_pipeline(
            body,
            grid=(n // SC_WINDOW,),
            in_specs=[pl.BlockSpec((1, LANES), lambda i: (0, i))],
            out_specs=[pl.BlockSpec((SC_WINDOW, width), lambda i: (i, 0))],
            core_axis_name=("core", "subcore"),
            dimension_semantics=(pltpu.PARALLEL,),
            trace_scopes=False,
        )(idx_hbm, out_hbm)

    idx_pad = jnp.pad(idx.reshape(n // SC_WINDOW, SC_WINDOW), ((0, 0), (0, LANES - SC_WINDOW)))
    return gather(table, idx_pad.reshape(1, (n // SC_WINDOW) * LANES))


def _peer_combine_kernel(x_ref, g2_ref, rows_a_ref, rows_b_ref, gate_ref, o_ref):
    ne = PEER_HEADS * PEER_TOPK
    x = x_ref[...]
    ct = x.shape[0]
    xn = x * lax.rsqrt(jnp.mean(x * x, axis=-1, keepdims=True) + NORM_EPS) * g2_ref[...]
    gate_t = jnp.concatenate([gate_ref[...]] * (ne // ct), axis=0).T
    for jj in range(ct):
        rows_ref, j = (rows_a_ref, jj) if jj < ct // 2 else (rows_b_ref, jj - ct // 2)
        u_lo, u_hi = _unpack_rows(rows_ref[j * ne:(j + 1) * ne, 0:HALF_D])
        xr = xn[jj:jj + 1, :]
        h = jnp.sum(u_lo * xr[:, 0:HALF_D] + u_hi * xr[:, HALF_D:2 * HALF_D], axis=1, keepdims=True)
        act = 0.5 * h * (1.0 + lax.erf(h * (2.0 ** -0.5)))
        wgt = gate_t[:, jj:jj + 1] * act
        v_lo, v_hi = _unpack_rows(rows_ref[j * ne:(j + 1) * ne, HALF_D:2 * HALF_D])
        o_ref[jj:jj + 1, 0:HALF_D] = x[jj:jj + 1, 0:HALF_D] + jnp.sum(wgt * v_lo, axis=0, keepdims=True)
        o_ref[jj:jj + 1, HALF_D:2 * HALF_D] = (x[jj:jj + 1, HALF_D:2 * HALF_D]
                                               + jnp.sum(wgt * v_hi, axis=0, keepdims=True))


def _peer_combine(x, g2, rows, gates, first_token):
    t, d = x.shape
    ne = PEER_HEADS * PEER_TOPK
    ct = PEER_CT
    steps = rows.shape[0] // (ct * ne)
    off = first_token // ct
    return pl.pallas_call(
        _peer_combine_kernel,
        grid=(steps,),
        in_specs=[pl.BlockSpec((ct, d), lambda i: (off + i, 0)),
                  pl.BlockSpec((1, d), lambda i: (0, 0)),
                  pl.BlockSpec((ct * ne // 2, d), lambda i: (2 * i, 0)),
                  pl.BlockSpec((ct * ne // 2, d), lambda i: (2 * i + 1, 0)),
                  pl.BlockSpec((ct, ne), lambda i: (off + i, 0))],
        out_specs=pl.BlockSpec((ct, d), lambda i: (off + i, 0)),
        out_shape=jax.ShapeDtypeStruct((t, d), F32),
        input_output_aliases={0: 0},
        compiler_params=_cp("parallel"),
        name="peer_combine",
    )(x, g2.reshape(1, d), rows, rows, gates)


PEER_TOKENS_PER_GATHER = 2048


def _peer_route_stage(x, g2, wq_b, keys_b):
    t = x.shape[0]
    ne = PEER_HEADS * PEER_TOPK
    qp = _norm_matmul(x, g2, wq_b, out_dtype=BF16)
    e_tok, g_tok = _peer_route(qp, keys_b)
    return e_tok.reshape(t * ne), g_tok


def _peer_gather_stage(table, idx, t, gather_fn):
    ne = PEER_HEADS * PEER_TOPK
    tc = min(PEER_TOKENS_PER_GATHER, t)
    return [gather_fn(table, idx[c * tc * ne:(c + 1) * tc * ne]) for c in range(t // tc)]


def _peer_combine_stage(x, g2, rows_list, gates):
    tc = x.shape[0] // len(rows_list)
    for c, rows in enumerate(rows_list):
        x = _peer_combine(x, g2, rows, gates, c * tc)
    return x


def _peer(x, g2, wq, keys, u_tabs, v_tabs, layer, gather_fn):
    idx, gates = _peer_route_stage(x, g2, wq.astype(BF16), keys.astype(BF16))
    table = _pack_tables(u_tabs, v_tabs, layer)
    rows_list = _peer_gather_stage(table, idx, x.shape[0], gather_fn)
    return _peer_combine_stage(x, g2, rows_list, gates)


_IN_WIDTHS = (256, 256, 256, 256, 256, 256, 256, 4, 4, 256, 256, 128, 128, 128, 128, 128, 128, 12,
              256, 256, 256, 256)


def _dup_groups(wcols):
    g0, g1 = wcols[:, :HEAD_DIM], wcols[:, HEAD_DIM:]
    return jnp.concatenate([g0, g0, g1, g1], axis=1)


def _layout_w_in(w_in):
    offs = np.cumsum((0,) + _IN_WIDTHS)
    cols = [w_in[:, offs[i]:offs[i + 1]] for i in range(len(_IN_WIDTHS))]
    (hq, hf, hi, hg, mq, mk, mv, mi, mf, mo, nq, nkc, nvc, nks, nvs, nkw, nvw, ng, rq, rk, rv, rg) = cols
    d = w_in.shape[0]
    pad = lambda c, n: jnp.concatenate([c, jnp.zeros((d, n - c.shape[1]), w_in.dtype)], axis=1)
    main = jnp.concatenate([hq, hf, hi, hg, mq, mk, mv, mo, rq, rk, rv, rg,
                            _dup_groups(nks), _dup_groups(nkw), nq, nvs, nvw,
                            pad(jnp.concatenate([mi, mf], axis=1), LANES), pad(ng, LANES)], axis=1)
    assert main.shape[1] == N_MAIN
    kcvc = jnp.concatenate([nkc, nvc], axis=1)
    return main.astype(BF16), kcvc.astype(BF16)


def kernel(x, norm1_g, w_in, hgrn_lb, hgrn_onorm_g, mlstm_conv_w, mlstm_conv_b, mlstm_gate_b, mlstm_onorm_g, nsa_qnorm_g, nsa_knorm_g, nsa_cmp_pe, nsa_cmp_w, ret_onorm_g, w_up, w_gate, w_out, norm2_g, peer_wq, peer_keys, peer_u, peer_v):
    bsz, seq, d = x.shape
    t = bsz * seq
    depth = w_in.shape[0]
    cos_t, sin_t = _rope_lane_tables(seq)
    cos4, sin4 = jnp.tile(cos_t, (1, 2)), jnp.tile(sin_t, (1, 2))
    lb_cum = jnp.cumsum(jax.nn.softmax(hgrn_lb.astype(F32), axis=0), axis=0)
    lb_all = lb_cum - lb_cum[0:1]
    weights = []
    for l in range(depth):
        w_main, w_kcvc = _layout_w_in(w_in[l])
        weights.append(dict(
            main=w_main, kcvc=w_kcvc, gate=w_gate[l].astype(BF16), up=w_up[l].astype(BF16),
            out=w_out[l].astype(BF16), lb=_hgrn_lb_rows(lb_all[l]), wq=peer_wq[l].astype(BF16),
            keys=peer_keys[l].astype(BF16), table=_pack_tables(peer_u, peer_v, l)))

    def mixer_steps(xh, l, nb):
        wl = weights[l]
        st = {}

        def proj(dep):
            st["proj"] = _norm_matmul(xh, _after(norm1_g[l], dep), wl["main"])
            return st["proj"]

        def gates(dep):
            st["gates"] = _norm_matmul(xh, _after(norm1_g[l], dep), wl["gate"], act="sigmoid", out_dtype=BF16)
            return st["gates"]

        def hgrn(dep):
            st["oh"] = _hgrn(st["proj"], wl["lb"], _after(hgrn_onorm_g[l], dep), nb, seq)
            return st["oh"]

        def mlstm(dep):
            st["om"] = _mlstm(st["proj"], mlstm_conv_w[l], mlstm_conv_b[l], mlstm_gate_b[l],
                              _after(mlstm_onorm_g[l], dep), nb, seq)
            return st["om"]

        def ret(dep):
            st["or"] = _ret(st["proj"], cos4, sin4, _after(ret_onorm_g[l], dep), nb, seq)
            return st["or"]

        def nsa_front(dep):
            kcvc = _norm_matmul(xh, _after(norm1_g[l], dep), wl["kcvc"])
            qn, qr, ks, kw, vst, vwt = _nsa_prep(st["proj"], cos4, sin4, nsa_qnorm_g[l], nsa_knorm_g[l], nb, seq)
            o_cmp, sel = _nsa_cmp(kcvc, qn, nsa_cmp_pe[l], nsa_cmp_w[l], nsa_knorm_g[l][0], nb, seq)
            st["nsa"] = (qr, ks, kw, vst, vwt, sel, o_cmp)
            return o_cmp

        def nsa_attn(dep):
            del dep
            st["on"] = _nsa_attn(st["proj"], *st["nsa"], nb, seq)
            return st["on"]

        def merge(dep):
            del dep
            st["xm"] = _merge(xh, st["gates"], (st["oh"], st["om"], st["on"], st["or"]), wl["up"], wl["out"])
            return st["xm"]

        def route(dep):
            st["idx"], st["pgates"] = _peer_route_stage(st["xm"], _after(norm2_g[l], dep), wl["wq"], wl["keys"])
            return st["pgates"]

        return [proj, gates, hgrn, mlstm, ret, nsa_front, nsa_attn, merge, route], st

    combine_slots = (0, 8, 8, 8, 8, 8, 8, 8)

    def combine_steps(l, xm, rows_list, pgates):
        box = {"x": xm}
        tc = xm.shape[0] // len(rows_list)

        def make(c):
            def step(dep):
                box["x"] = _peer_combine(box["x"], _after(norm2_g[l], dep), rows_list[c], pgates, c * tc)
                return box["x"]
            return step

        return [make(c) for c in range(len(rows_list))], box

    n_groups = next(n for n in (4, 2, 1) if bsz % n == 0)
    nb = bsz // n_groups
    xs = [x[g * nb:(g + 1) * nb].reshape(nb * seq, d) for g in range(n_groups)]
    dep = None
    lag = min(2, n_groups - 1)
    pending = []
    for l in range(depth):
        for g in range(n_groups):
            msteps, st = mixer_steps(xs[g], l, nb)
            due = pending.pop(0) if len(pending) == lag and lag > 0 else None
            csteps = due[1] if due is not None else []
            ci = 0
            for si, mstep in enumerate(msteps):
                dep = mstep(dep)
                while ci < len(csteps) and (ci >= len(combine_slots) or combine_slots[ci] <= si):
                    dep = csteps[ci](dep)
                    ci += 1
            for cstep in csteps[ci:]:
                dep = cstep(dep)
            if due is not None:
                xs[due[0]] = due[2]["x"]
            rows_list = _peer_gather_stage(weights[l]["table"], st["idx"], nb * seq, _sc_gather)
            csteps, box = combine_steps(l, st["xm"], rows_list, st["pgates"])
            pending.append((g, csteps, box))
            if lag == 0:
                for cstep in pending.pop(0)[1]:
                    dep = cstep(dep)
                xs[g] = box["x"]
    for pg, csteps, box in pending:
        for cstep in csteps:
            dep = cstep(dep)
        xs[pg] = box["x"]
    return jnp.concatenate(xs, axis=0).reshape(bsz, seq, d)
```

```python
import functools
import math

import numpy as np
import jax
import jax.numpy as jnp
from jax import lax
from jax.experimental import pallas as pl
from jax.experimental.pallas import tpu as pltpu

F32 = jnp.float32
BF16 = jnp.bfloat16

HEAD_DIM = 64
N_HEADS = 4
MIX_WIDTH = N_HEADS * HEAD_DIM
CHUNK = 64
NORM_EPS = 1e-6
NEG_INF = -1e30
ROPE_THETA = 10000.0
CONV_W = 4
NSA_GROUPS = 2
CMP_LEN = 32
CMP_STRIDE = 16
SEL_BLOCK = 64
SEL_TOPK = 16
WINDOW = 512
FORCE_BONUS = 1e3
PEER_HEADS = 8
PEER_NKEYS = 128
PEER_TOPK = 16
PEER_KDIM = 128

LANES = 128
VMEM_LIMIT = 48 * 1024 * 1024

OFF_H, OFF_M, OFF_R, OFF_KD, OFF_NQ, OFF_V, OFF_MG, OFF_NG = 0, 1024, 2048, 3072, 3584, 3840, 4096, 4224
N_MAIN = 4352


def _cp(*sem):
    return pltpu.CompilerParams(dimension_semantics=sem, vmem_limit_bytes=VMEM_LIMIT)


def _dot(a, b):
    return jnp.dot(a, b, preferred_element_type=F32)


def _dot_nt(a, b):
    return lax.dot_general(a, b, (((1,), (1,)), ((), ())), preferred_element_type=F32)


def _dot_tn(a, b):
    return lax.dot_general(a, b, (((0,), (0,)), ((), ())), preferred_element_type=F32)


def _split3(x):
    hi = x.astype(BF16)
    r1 = x - hi.astype(F32)
    mid = r1.astype(BF16)
    lo = (r1 - mid.astype(F32)).astype(BF16)
    return hi, mid, lo


def _dot01_l(m01, x):
    hi, mid, lo = _split3(x)
    return _dot(m01, hi) + _dot(m01, mid) + _dot(m01, lo)


def _dot01_r(x, m01):
    hi, mid, lo = _split3(x)
    return _dot(hi, m01) + _dot(mid, m01) + _dot(lo, m01)


def _head_of_lane(shape, axis):
    return lax.broadcasted_iota(jnp.int32, shape, axis) // HEAD_DIM


def _block_ones(n, dtype=BF16):
    r = lax.broadcasted_iota(jnp.int32, (n, n), 0) // HEAD_DIM
    c = lax.broadcasted_iota(jnp.int32, (n, n), 1) // HEAD_DIM
    return (r == c).astype(dtype)


def _group_sum(x, ones_bd):
    hi = x.astype(BF16)
    lo = (x - hi.astype(F32)).astype(BF16)
    return _dot(hi, ones_bd) + _dot(lo, ones_bd)


def _head_rms(x, gain, ones_bd):
    ms = _group_sum(x * x, ones_bd) * (1.0 / HEAD_DIM)
    return x * lax.rsqrt(ms + NORM_EPS) * gain


def _sigmoid(x):
    return 1.0 / (1.0 + jnp.exp(-x))


def _silu(x):
    return x * _sigmoid(x)


def _log_sigmoid(x):
    return jnp.minimum(x, 0.0) - jnp.log(1.0 + jnp.exp(-jnp.abs(x)))


def _stack_heads(x, n_heads=N_HEADS):
    hl = _head_of_lane(x.shape, 1)
    return jnp.concatenate([jnp.where(hl == h, x, jnp.zeros_like(x)) for h in range(n_heads)], axis=0)


def _unstack_heads(r, c, n_heads=N_HEADS):
    hl = _head_of_lane((c, r.shape[1]), 1)
    out = jnp.zeros((c, r.shape[1]), F32)
    for h in range(n_heads):
        out = jnp.where(hl == h, r[h * c:(h + 1) * c, :], out)
    return out


def _rope(x, cos_t, sin_t):
    n = x.shape[1]
    first = (lax.broadcasted_iota(jnp.int32, x.shape, 1) % HEAD_DIM) < (HEAD_DIM // 2)
    partner = jnp.where(first, pltpu.roll(x, n - HEAD_DIM // 2, 1), pltpu.roll(x, HEAD_DIM // 2, 1))
    return x * cos_t + partner * sin_t


def _after_kernel(a_ref, dep_ref, o_ref):
    del dep_ref
    o_ref[...] = a_ref[...]


def _after(a, dep):
    if dep is None:
        return a
    a2 = a.reshape(1, a.size)
    out = pl.pallas_call(
        _after_kernel,
        in_specs=[pl.BlockSpec(a2.shape, lambda: (0, 0)), pl.BlockSpec(memory_space=pl.ANY)],
        out_specs=pl.BlockSpec(a2.shape, lambda: (0, 0)),
        out_shape=jax.ShapeDtypeStruct(a2.shape, a2.dtype),
        name="order_after",
    )(a2, dep)
    return out.reshape(a.shape)


def _norm_matmul_kernel(x_ref, g_ref, w_ref, o_ref, xn_ref, *, act):
    @pl.when(pl.program_id(1) == 0)
    def _():
        x = x_ref[...]
        ms = jnp.mean(x * x, axis=-1, keepdims=True)
        xn_ref[...] = (x * lax.rsqrt(ms + NORM_EPS) * g_ref[...]).astype(BF16)

    y = _dot(xn_ref[...], w_ref[...])
    if act == "sigmoid":
        y = _sigmoid(y)
    o_ref[...] = y.astype(o_ref.dtype)


def _norm_matmul(x, g, w, *, act=None, out_dtype=F32, tm=1024, tn=2176):
    t, d = x.shape
    w3 = w if w.ndim == 3 else w[None]
    n_per = w3.shape[2]
    tm = min(tm, t)
    tn = next(c for c in (tn, 2048, 1024, 512, 256, 128) if n_per % c == 0)
    per = n_per // tn
    n = w3.shape[0] * n_per
    assert t % tm == 0
    return pl.pallas_call(
        functools.partial(_norm_matmul_kernel, act=act),
        grid=(t // tm, n // tn),
        in_specs=[pl.BlockSpec((tm, d), lambda i, j: (i, 0)),
                  pl.BlockSpec((1, d), lambda i, j: (0, 0)),
                  pl.BlockSpec((None, d, tn), lambda i, j: (j // per, 0, j % per))],
        out_specs=pl.BlockSpec((tm, tn), lambda i, j: (i, j)),
        out_shape=jax.ShapeDtypeStruct((t, n), out_dtype),
        scratch_shapes=[pltpu.VMEM((tm, d), BF16)],
        compiler_params=_cp("parallel", "arbitrary"),
        name="norm_matmul",
    )(x, g.reshape(1, d), w3)


def _merge_kernel(x_ref, gate_ref, oh_ref, om_ref, on_ref, or_ref, wup_ref, wout_ref, o_ref):
    d = x_ref.shape[1]
    acc = None
    for m, r in enumerate((oh_ref, om_ref, on_ref, or_ref)):
        up = _dot(r[...].astype(BF16), wup_ref[m])
        term = gate_ref[:, m * d:(m + 1) * d].astype(F32) * up
        acc = term if acc is None else acc + term
    o_ref[...] = x_ref[...] + _dot(acc.astype(BF16), wout_ref[...])


def _merge(x, gates, outs, w_up, w_out, tm=512):
    t, d = x.shape
    tm = min(tm, t)
    mix = pl.BlockSpec((tm, MIX_WIDTH), lambda i: (i, 0))
    return pl.pallas_call(
        _merge_kernel,
        grid=(t // tm,),
        in_specs=[pl.BlockSpec((tm, d), lambda i: (i, 0)),
                  pl.BlockSpec((tm, 4 * d), lambda i: (i, 0)),
                  mix, mix, mix, mix,
                  pl.BlockSpec((4, MIX_WIDTH, d), lambda i: (0, 0, 0)),
                  pl.BlockSpec((d, d), lambda i: (0, 0))],
        out_specs=pl.BlockSpec((tm, d), lambda i: (i, 0)),
        out_shape=jax.ShapeDtypeStruct((t, d), F32),
        compiler_params=_cp("parallel"),
        name="merge",
    )(x, gates, *outs, w_up, w_out)


REC_BLOCK = 256


def _chunk_consts():
    t = lax.broadcasted_iota(jnp.int32, (CHUNK, CHUNK), 0)
    s = lax.broadcasted_iota(jnp.int32, (CHUNK, CHUNK), 1)
    return t, s


def _hgrn_levels():
    t = np.arange(CHUNK)
    rows = []
    masks = []
    h = CHUNK // 2
    while h >= 1:
        ref = (t // (2 * h)) * (2 * h) + h
        p = np.zeros((CHUNK, CHUNK), np.float32)
        p[t, np.minimum(ref, CHUNK - 1)] = 1.0
        rows.append(p)
        same = (t[:, None] // (2 * h)) == (t[None, :] // (2 * h))
        m = same & ((t[:, None] // h) % 2 == 1) & ((t[None, :] // h) % 2 == 0)
        masks.append(m.astype(np.float32))
        h //= 2
    masks.append(np.eye(CHUNK, dtype=np.float32))
    return np.concatenate(rows, 0), np.stack(masks, 0)


def _hgrn_kernel(p_ref, lb_ref, g_ref, psel_ref, lmask_ref, o_ref, st_ref):
    @pl.when(pl.program_id(1) == 0)
    def _():
        st_ref[...] = jnp.zeros_like(st_ref)

    c = CHUNK
    w = MIX_WIDTH
    ones_bd = _block_ones(w)
    bd_mask = _block_ones(w, F32)
    tri = (lax.broadcasted_iota(jnp.int32, (c, c), 0) >= lax.broadcasted_iota(jnp.int32, (c, c), 1)).astype(BF16)
    psel = psel_ref[...]
    n_lv = lmask_ref.shape[0]
    log_lb, log_1mlb, one_mlb = lb_ref[0:1, :], lb_ref[1:2, :], lb_ref[2:3, :]
    gain = g_ref[...]

    def chunk(ci, carry):
        r0 = pl.multiple_of(ci * c, c)
        q = _silu(p_ref[pl.ds(r0, c), 0:w])
        fl = p_ref[pl.ds(r0, c), w:2 * w]
        v = p_ref[pl.ds(r0, c), 2 * w:3 * w]
        gp = p_ref[pl.ds(r0, c), 3 * w:4 * w]
        a1 = jnp.broadcast_to(log_lb, fl.shape)
        a2 = log_1mlb + _log_sigmoid(fl)
        mx = jnp.maximum(a1, a2)
        log_f = mx + jnp.log(jnp.exp(a1 - mx) + jnp.exp(a2 - mx))
        k = one_mlb * _sigmoid(-fl)
        b = _dot01_l(tri, log_f)
        bref = _dot01_l(psel, b)
        vb = v.astype(BF16)
        a = jnp.zeros((N_HEADS * c, c), F32)
        for lv in range(n_lv):
            if lv < n_lv - 1:
                br = bref[lv * c:(lv + 1) * c, :]
                qs = q * jnp.exp(jnp.minimum(b - br, 0.0))
                ks = k * jnp.exp(jnp.minimum(br - b, 0.0))
            else:
                qs, ks = q, k
            s_lv = _dot_nt(_stack_heads(qs).astype(BF16), ks.astype(BF16))
            a = a + jnp.concatenate([lmask_ref[lv]] * N_HEADS, axis=0) * s_lv
        o = _unstack_heads(_dot(a.astype(BF16), vb), c)
        st = st_ref[...]
        o = o + _dot_nt((q * jnp.exp(b)).astype(BF16), st.astype(BF16))
        b_last = b[c - 1:c, :]
        kb = k * jnp.exp(b_last - b)
        st_ref[...] = st * jnp.exp(b_last) + bd_mask * _dot_tn(vb, kb.astype(BF16))
        y = _head_rms(o, gain, ones_bd) * _silu(gp)
        o_ref[pl.ds(r0, c), :] = y
        return carry

    lax.fori_loop(0, p_ref.shape[0] // c, chunk, 0)


def _hgrn(proj, lb_rows, gain, bsz, seq):
    psel, lmask = _hgrn_levels()
    tb = min(REC_BLOCK, seq)
    nb = seq // tb
    return pl.pallas_call(
        _hgrn_kernel,
        grid=(bsz, nb),
        in_specs=[pl.BlockSpec((tb, 4 * MIX_WIDTH), lambda b, i: (b * nb + i, OFF_H // (4 * MIX_WIDTH))),
                  pl.BlockSpec((8, MIX_WIDTH), lambda b, i: (0, 0)),
                  pl.BlockSpec((1, MIX_WIDTH), lambda b, i: (0, 0)),
                  pl.BlockSpec(psel.shape, lambda b, i: (0, 0)),
                  pl.BlockSpec(lmask.shape, lambda b, i: (0, 0, 0))],
        out_specs=pl.BlockSpec((tb, MIX_WIDTH), lambda b, i: (b * nb + i, 0)),
        out_shape=jax.ShapeDtypeStruct((bsz * seq, MIX_WIDTH), F32),
        scratch_shapes=[pltpu.VMEM((MIX_WIDTH, MIX_WIDTH), F32)],
        compiler_params=_cp("parallel", "arbitrary"),
        name="hgrn2",
    )(proj, lb_rows, gain.reshape(1, MIX_WIDTH), jnp.asarray(psel, BF16), jnp.asarray(lmask, F32))


def _ret_kernel(p_ref, cos_ref, sin_ref, dec_ref, decin_ref, g_ref, o_ref, st_ref):
    @pl.when(pl.program_id(1) == 0)
    def _():
        st_ref[...] = jnp.zeros_like(st_ref)

    c = CHUNK
    w = MIX_WIDTH
    ones_bd = _block_ones(w)
    bd_mask = _block_ones(w, F32)
    gain = g_ref[...]
    dec_q = dec_ref[0:c, :]
    dec_k = dec_ref[c:2 * c, :]
    dec_state = dec_ref[2 * c:2 * c + 1, :]
    dec_in = decin_ref[...]

    def chunk(ci, carry):
        r0 = pl.multiple_of(ci * c, c)
        cos_t = cos_ref[pl.ds(r0, c), :]
        sin_t = sin_ref[pl.ds(r0, c), :]
        q = _rope(p_ref[pl.ds(r0, c), 0:w], cos_t, sin_t)
        k = _rope(p_ref[pl.ds(r0, c), w:2 * w], cos_t, sin_t) * (HEAD_DIM ** -0.5)
        v = p_ref[pl.ds(r0, c), 2 * w:3 * w]
        gp = p_ref[pl.ds(r0, c), 3 * w:4 * w]
        vb = v.astype(BF16)
        a = _dot_nt(_stack_heads(q).astype(BF16), k.astype(BF16)) * dec_in
        o = _unstack_heads(_dot(a.astype(BF16), vb), c)
        st = st_ref[...]
        o = o + _dot_nt(q.astype(BF16), st.astype(BF16)) * dec_q
        st_ref[...] = st * dec_state + bd_mask * _dot_tn(vb, (k * dec_k).astype(BF16))
        o_ref[pl.ds(r0, c), :] = _head_rms(o, gain, ones_bd) * _silu(gp)
        return carry

    lax.fori_loop(0, p_ref.shape[0] // c, chunk, 0)


def _ret_consts():
    log_gamma = np.log1p(-np.exp2(-5.0 - np.arange(N_HEADS, dtype=np.float64)))
    t = np.arange(CHUNK, dtype=np.float64)
    lane_h = np.arange(MIX_WIDTH) // HEAD_DIM
    dec_q = np.exp(log_gamma[lane_h][None, :] * (t[:, None] + 1.0))
    dec_k = np.exp(log_gamma[lane_h][None, :] * (CHUNK - 1.0 - t[:, None]))
    dec_state = np.exp(log_gamma[lane_h] * CHUNK)[None, :]
    dec = np.concatenate([dec_q, dec_k, np.broadcast_to(dec_state, (8, MIX_WIDTH))], 0)
    diff = t[:, None] - t[None, :]
    dec_in = np.concatenate([np.where(diff >= 0, np.exp(log_gamma[h] * diff), 0.0) for h in range(N_HEADS)], 0)
    return dec.astype(np.float32), dec_in.astype(np.float32)


def _ret(proj, cos4, sin4, gain, bsz, seq):
    dec, dec_in = _ret_consts()
    tb = min(REC_BLOCK, seq)
    nb = seq // tb
    return pl.pallas_call(
        _ret_kernel,
        grid=(bsz, nb),
        in_specs=[pl.BlockSpec((tb, 4 * MIX_WIDTH), lambda b, i: (b * nb + i, OFF_R // (4 * MIX_WIDTH))),
                  pl.BlockSpec((tb, MIX_WIDTH), lambda b, i: (i, 0)),
                  pl.BlockSpec((tb, MIX_WIDTH), lambda b, i: (i, 0)),
                  pl.BlockSpec(dec.shape, lambda b, i: (0, 0)),
                  pl.BlockSpec(dec_in.shape, lambda b, i: (0, 0)),
                  pl.BlockSpec((1, MIX_WIDTH), lambda b, i: (0, 0))],
        out_specs=pl.BlockSpec((tb, MIX_WIDTH), lambda b, i: (b * nb + i, 0)),
        out_shape=jax.ShapeDtypeStruct((bsz * seq, MIX_WIDTH), F32),
        scratch_shapes=[pltpu.VMEM((MIX_WIDTH, MIX_WIDTH), F32)],
        compiler_params=_cp("parallel", "arbitrary"),
        name="retention",
    )(proj, cos4, sin4, jnp.asarray(dec), jnp.asarray(dec_in), gain.reshape(1, MIX_WIDTH))


def _hgrn_lb_rows(lb):
    lb = lb.astype(F32)
    rows = jnp.stack([jnp.log(lb), jnp.log1p(-lb), 1.0 - lb], 0)
    return jnp.concatenate([rows, jnp.zeros((5, lb.shape[0]), F32)], 0)


def _rope_lane_tables(seq):
    inv = 1.0 / (ROPE_THETA ** (jnp.arange(0, HEAD_DIM, 2, dtype=F32) / HEAD_DIM))
    ang = jnp.arange(seq, dtype=F32)[:, None] * inv[None, :]
    cos, sin = jnp.cos(ang), jnp.sin(ang)
    cos_t = jnp.tile(cos, (1, LANES // (HEAD_DIM // 2)))
    sin_t = jnp.tile(jnp.concatenate([-sin, sin], axis=1), (1, LANES // HEAD_DIM))
    return cos_t, sin_t


def _expand_heads(cols, shape):
    hl = _head_of_lane(shape, 1)
    out = jnp.broadcast_to(cols[-1], shape)
    for h in range(len(cols) - 2, -1, -1):
        out = jnp.where(hl == h, jnp.broadcast_to(cols[h], shape), out)
    return out


def _mlstm_kernel(p_ref, gcol_ref, grow_ref, cw_ref, cb_ref, gbr_ref, gbc_ref, g_ref, o_ref,
                  ct_ref, n_ref, m_ref, hist_ref, cbuf_ref, qk_ref):
    c = CHUNK
    w = MIX_WIDTH
    tb = p_ref.shape[0]

    @pl.when(pl.program_id(1) == 0)
    def _():
        ct_ref[...] = jnp.zeros_like(ct_ref)
        n_ref[...] = jnp.zeros_like(n_ref)
        m_ref[...] = jnp.zeros_like(m_ref)
        hist_ref[...] = jnp.zeros_like(hist_ref)

    cbuf_ref[0:8, :] = hist_ref[...]
    cbuf_ref[8:, :] = p_ref[:, 0:2 * w]
    hist_ref[...] = p_ref[tb - 8:tb, 0:2 * w]
    acc = jnp.broadcast_to(cb_ref[...], (tb, 2 * w))
    for j in range(CONV_W):
        acc = acc + cw_ref[j:j + 1, :] * cbuf_ref[pl.ds(8 - (CONV_W - 1) + j, tb), :]
    qk_ref[...] = _silu(acc)

    ones_bd = _block_ones(w)
    bd_mask = _block_ones(w, F32)
    ti = lax.broadcasted_iota(jnp.int32, (c, c), 0)
    si = lax.broadcasted_iota(jnp.int32, (c, c), 1)
    causal = ti >= si
    tri = causal.astype(BF16)
    tri_t = (ti <= si).astype(BF16)
    gain = g_ref[...]
    ones_ext = jnp.ones((c, LANES), BF16)

    def chunk(ci, carry):
        r0 = pl.multiple_of(ci * c, c)
        q = qk_ref[pl.ds(r0, c), 0:w]
        k = qk_ref[pl.ds(r0, c), w:2 * w] * (HEAD_DIM ** -0.5)
        v = p_ref[pl.ds(r0, c), 2 * w:3 * w]
        op = p_ref[pl.ds(r0, c), 3 * w:4 * w]
        gc = gcol_ref[pl.ds(r0, c), :] + gbr_ref[...]
        gr = grow_ref[ci] + gbc_ref[...]
        b_c = _dot01_l(tri, _log_sigmoid(gc))
        b_r = _dot01_r(_log_sigmoid(gr), tri_t)
        wd, s_inter, em, wk, decay = [], [], [], [], []
        for h in range(N_HEADS):
            bc = b_c[:, N_HEADS + h:N_HEADS + h + 1]
            lic = gc[:, h:h + 1]
            br = b_r[N_HEADS + h:N_HEADS + h + 1, :]
            lir = gr[h:h + 1, :]
            dmat = jnp.where(causal, bc - br + lir, -jnp.inf)
            m_prev = m_ref[h:h + 1, 0:1]
            inter = bc + m_prev
            mrow = jnp.maximum(inter, jnp.max(dmat, axis=1, keepdims=True))
            wd.append(jnp.exp(dmat - mrow))
            s_inter.append(jnp.exp(inter - mrow))
            em.append(jnp.exp(-mrow))
            b_last = br[:, c - 1:c]
            m_new = jnp.maximum(b_last + m_prev, jnp.max(b_last - br + lir, axis=1, keepdims=True))
            wk.append(jnp.exp(b_last - bc + lic - m_new))
            decay.append(jnp.exp(b_last + m_prev - m_new))
            m_ref[h:h + 1, :] = jnp.broadcast_to(m_new, (1, LANES))
        s_inter_l = _expand_heads(s_inter, (c, w))
        em_l = _expand_heads(em, (c, w))
        wk_l = _expand_heads(wk, (c, w))
        decay_l = _expand_heads(decay, (1, w))
        qk = _dot_nt(_stack_heads(q).astype(BF16), k.astype(BF16))
        wmat = jnp.concatenate(wd, axis=0) * qk
        vb = v.astype(BF16)
        r = _dot(wmat.astype(BF16), jnp.concatenate([vb, ones_ext], axis=1))
        num_intra = _unstack_heads(r[:, 0:w], c)
        rs_l = _expand_heads([r[h * c:(h + 1) * c, w:w + 1] for h in range(N_HEADS)], (c, w))
        ct = ct_ref[...]
        nrow = n_ref[0:1, :]
        num = s_inter_l * _dot_nt(q.astype(BF16), ct.astype(BF16)) + num_intra
        den = s_inter_l * _group_sum(q * nrow, ones_bd) + rs_l
        hval = num / jnp.maximum(jnp.abs(den), em_l)
        kw = wk_l * k
        ct_ref[...] = ct * decay_l + bd_mask * _dot_tn(vb, kw.astype(BF16))
        n_ref[0:1, :] = nrow * decay_l + jnp.sum(kw, axis=0, keepdims=True)
        o_ref[pl.ds(r0, c), :] = _head_rms(hval, gain, ones_bd) * _sigmoid(op)
        return carry

    lax.fori_loop(0, tb // c, chunk, 0)


def _mlstm(proj, conv_w, conv_b, gate_b, gain, bsz, seq):
    t = bsz * seq
    w = MIX_WIDTH
    tb = min(REC_BLOCK, seq)
    nb = seq // tb
    ncb = tb // CHUNK
    grow = proj[:, OFF_MG:OFF_MG + 8].reshape(t // CHUNK, CHUNK, 8).transpose(0, 2, 1)
    gb_row = jnp.zeros((1, LANES), F32).at[0, 0:8].set(gate_b.astype(F32))
    gb_col = gate_b.astype(F32).reshape(8, 1)
    return pl.pallas_call(
        _mlstm_kernel,
        grid=(bsz, nb),
        in_specs=[pl.BlockSpec((tb, 4 * w), lambda b, i: (b * nb + i, OFF_M // (4 * w))),
                  pl.BlockSpec((tb, LANES), lambda b, i: (b * nb + i, OFF_MG // LANES)),
                  pl.BlockSpec((ncb, 8, CHUNK), lambda b, i: (b * nb + i, 0, 0)),
                  pl.BlockSpec((CONV_W, 2 * w), lambda b, i: (0, 0)),
                  pl.BlockSpec((1, 2 * w), lambda b, i: (0, 0)),
                  pl.BlockSpec((1, LANES), lambda b, i: (0, 0)),
                  pl.BlockSpec((8, 1), lambda b, i: (0, 0)),
                  pl.BlockSpec((1, w), lambda b, i: (0, 0))],
        out_specs=pl.BlockSpec((tb, w), lambda b, i: (b * nb + i, 0)),
        out_shape=jax.ShapeDtypeStruct((t, w), F32),
        scratch_shapes=[pltpu.VMEM((w, w), F32), pltpu.VMEM((8, w), F32), pltpu.VMEM((8, LANES), F32),
                        pltpu.VMEM((8, 2 * w), F32), pltpu.VMEM((tb + 8, 2 * w), F32),
                        pltpu.VMEM((tb, 2 * w), F32)],
        compiler_params=_cp("parallel", "arbitrary"),
        name="mlstm",
    )(proj, proj, grow, conv_w.astype(F32), conv_b.astype(F32).reshape(1, 2 * w), gb_row, gb_col,
      gain.reshape(1, w))


NSA_TQ = 128
NSA_KC = 512
GW = 2 * HEAD_DIM


def _nsa_prep_kernel(pq_ref, pk_ref, pv_ref, cos_ref, sin_ref, qg_ref, kg_ref,
                     qn_ref, qr_ref, ks_ref, kw_ref, vst_ref, vwt_ref):
    w = MIX_WIDTH
    for src, dst in ((pv_ref[:, 0:GW], vst_ref), (pv_ref[:, GW:2 * GW], vwt_ref)):
        vt = src.T
        tk = dst.shape[4]
        for g in range(NSA_GROUPS):
            rows = vt[g * HEAD_DIM:(g + 1) * HEAD_DIM, :]
            dup = jnp.concatenate([rows, rows], axis=0).astype(BF16)
            for j in range(dst.shape[2]):
                dst[0, g, j] = dup[:, j * tk:(j + 1) * tk]
    ones_bd = _block_ones(w)
    cos_t, sin_t = cos_ref[...], sin_ref[...]
    scale = HEAD_DIM ** -0.5
    qh = _head_rms(pq_ref[...], qg_ref[...], ones_bd)
    qn_ref[...] = (qh * scale).astype(BF16)
    qr_ref[...] = (_rope(qh, cos_t, sin_t) * scale).astype(BF16)
    ks_ref[...] = _rope(_head_rms(pk_ref[:, 0:w], kg_ref[1:2, :], ones_bd), cos_t, sin_t).astype(BF16)
    kw_ref[...] = _rope(_head_rms(pk_ref[:, w:2 * w], kg_ref[2:3, :], ones_bd), cos_t, sin_t).astype(BF16)


def _nsa_prep(proj, cos4, sin4, qnorm_g, knorm_g, bsz, seq):
    t = bsz * seq
    w = MIX_WIDTH
    tm = min(NSA_KC, seq)
    tq = min(NSA_TQ, seq)
    ns = seq // tm
    qg = jnp.tile(qnorm_g.astype(F32), w // HEAD_DIM).reshape(1, w)
    kg = jnp.concatenate([jnp.tile(knorm_g.astype(F32), (1, w // HEAD_DIM)), jnp.zeros((5, w), F32)], axis=0)
    out = jax.ShapeDtypeStruct((t, w), BF16)
    row = pl.BlockSpec((tm, w), lambda i: (i, 0))
    return pl.pallas_call(
        _nsa_prep_kernel,
        grid=(t // tm,),
        in_specs=[pl.BlockSpec((tm, w), lambda i: (i, OFF_NQ // w)),
                  pl.BlockSpec((tm, 2 * w), lambda i: (i, OFF_KD // (2 * w))),
                  pl.BlockSpec((tm, 2 * GW), lambda i: (i, OFF_V // (2 * GW))),
                  pl.BlockSpec((tm, w), lambda i: (i % ns, 0)),
                  pl.BlockSpec((tm, w), lambda i: (i % ns, 0)),
                  pl.BlockSpec((1, w), lambda i: (0, 0)),
                  pl.BlockSpec((8, w), lambda i: (0, 0))],
        out_specs=[row, row, row, row,
                   pl.BlockSpec((1, NSA_GROUPS, 1, GW, tm), lambda i: (i // ns, 0, i % ns, 0, 0)),
                   pl.BlockSpec((1, NSA_GROUPS, tm // tq, GW, tq), lambda i: (i // ns, 0, i % ns, 0, 0))],
        out_shape=[out, out, out, out,
                   jax.ShapeDtypeStruct((bsz, NSA_GROUPS, seq // tm, GW, tm), BF16),
                   jax.ShapeDtypeStruct((bsz, NSA_GROUPS, seq // tq, GW, tq), BF16)],
        compiler_params=_cp("parallel"),
        name="nsa_prep",
    )(proj, proj, proj, cos4, sin4, qg, kg)


def _nsa_cmp_kernel(xr_ref, pe_ref, w0_ref, w1_ref, kg_ref, ovt_ref, qn_ref, ocmp_ref, sel_ref,
                    kc_ref, vc_ref, v_ref, *, n_top):
    tq = qn_ref.shape[0]
    nr = xr_ref.shape[0]
    nsel = sel_ref.shape[2]
    w = MIX_WIDTH

    @pl.when(pl.program_id(1) == 0)
    def _():
        xr = xr_ref[...]
        y0 = _dot((xr + pe_ref[0]).astype(BF16), w0_ref[...])
        y1 = _dot((xr + pe_ref[1]).astype(BF16), w1_ref[...])
        kv = y0 + pltpu.roll(y1, nr - 1, 0)
        kc_ref[...] = _head_rms(kv[:, 0:w], kg_ref[...], _block_ones(w)).astype(BF16)
        vc_ref[...] = kv[:, w:2 * w].astype(BF16)

    pos0 = pl.program_id(1) * tq
    hl = _head_of_lane((tq, GW), 1)
    pos_r = pos0 + lax.broadcasted_iota(jnp.int32, (tq, nr), 0)
    valid = lax.broadcasted_iota(jnp.int32, (tq, nr), 1) * CMP_STRIDE + (CMP_LEN - 1) <= pos_r
    pos_c = pos0 + lax.broadcasted_iota(jnp.int32, (nr, tq), 1)
    valid_t = lax.broadcasted_iota(jnp.int32, (nr, tq), 0) * CMP_STRIDE + (CMP_LEN - 1) <= pos_c
    jrow = lax.broadcasted_iota(jnp.int32, (nsel, tq), 0)
    cur = (pos0 + lax.broadcasted_iota(jnp.int32, (nsel, tq), 1)) // SEL_BLOCK
    forced = (jrow == 0) | (jrow == cur) | (jrow == cur - 1)
    ovt = ovt_ref[...]

    for g in range(NSA_GROUPS):
        qg = qn_ref[:, g * GW:(g + 1) * GW]
        kg = kc_ref[:, g * GW:(g + 1) * GW]
        vg = vc_ref[:, g * GW:(g + 1) * GW]
        o_g = jnp.zeros((tq, GW), F32)
        pt_sum = jnp.zeros((nr, tq), F32)
        for hh in range(2):
            qm = jnp.where(hl == hh, qg, jnp.zeros_like(qg))
            s = jnp.where(valid, _dot_nt(qm, kg), NEG_INF)
            e = jnp.exp(s - jnp.max(s, axis=1, keepdims=True))
            p = jnp.where(valid, e / jnp.sum(e, axis=1, keepdims=True), 0.0)
            o_g = jnp.where(hl == hh, _dot(p.astype(BF16), vg), o_g)
            st = jnp.where(valid_t, _dot_nt(kg, qm), NEG_INF)
            et = jnp.exp(st - jnp.max(st, axis=0, keepdims=True))
            pt_sum = pt_sum + jnp.where(valid_t, et / jnp.sum(et, axis=0, keepdims=True), 0.0)
        ocmp_ref[:, g * GW:(g + 1) * GW] = o_g
        p_hi = pt_sum.astype(BF16)
        p_lo = (pt_sum - p_hi.astype(F32)).astype(BF16)
        imp = _dot(ovt, p_hi) + _dot(ovt, p_lo)
        val = jnp.where(jrow <= cur, imp + FORCE_BONUS * forced.astype(F32), NEG_INF)
        v_ref[...] = val

        def rank(jp, cnt):
            row = v_ref[pl.ds(jp, 1), :]
            tie = jnp.where(jrow > jp, 1.0, 0.0)
            return cnt + jnp.where(row > val, 1.0, jnp.where(row == val, tie, 0.0))

        cnt = lax.fori_loop(0, nsel, rank, jnp.zeros((nsel, tq), F32))
        sel_ref[0, g] = ((cnt < n_top) & (jrow <= cur)).astype(F32)


def _nsa_cmp_weights(cmp_pe, cmp_w):
    half = CMP_LEN // 2
    wl = cmp_w.astype(F32).reshape(2, 2, half, HEAD_DIM, HEAD_DIM)
    eye2 = jnp.eye(2, dtype=F32)
    w2 = jnp.einsum('kardz,kK,gG,h->arkgdKGhz', wl, eye2, eye2, jnp.ones((2,), F32))
    w2 = w2.reshape(2, half * 4 * HEAD_DIM, 8 * HEAD_DIM)
    pl_ = cmp_pe.astype(F32).reshape(2, 2, half, HEAD_DIM)
    pe2 = jnp.broadcast_to(pl_.transpose(1, 2, 0, 3)[:, :, :, None, :], (2, half, 2, 2, HEAD_DIM))
    return w2.astype(BF16), pe2.reshape(2, 1, half * 4 * HEAD_DIM)


def _nsa_cmp(kcvc, qn, cmp_pe, cmp_w, knorm0, bsz, seq, tq=512):
    t = bsz * seq
    w = MIX_WIDTH
    tq = min(tq, seq)
    nq = seq // tq
    nr = seq // CMP_STRIDE
    nsel = seq // SEL_BLOCK
    n_top = min(SEL_TOPK, nsel)
    w2, pe2 = _nsa_cmp_weights(cmp_pe, cmp_w)
    xr = kcvc.reshape(t // CMP_STRIDE, CMP_STRIDE * w)
    kg = jnp.tile(knorm0.astype(F32), w // HEAD_DIM).reshape(1, w)
    n_i = np.arange(nr)[:, None] * CMP_STRIDE
    j_i = np.arange(nsel)[None, :] * SEL_BLOCK
    ov = ((n_i < j_i + SEL_BLOCK) & (n_i + CMP_LEN > j_i)).astype(np.float32)
    ov[nr - 1, :] = 0.0
    kin = CMP_STRIDE * w
    return pl.pallas_call(
        functools.partial(_nsa_cmp_kernel, n_top=n_top),
        grid=(bsz, nq),
        in_specs=[pl.BlockSpec((nr, kin), lambda b, i: (b, 0)),
                  pl.BlockSpec((2, 1, kin), lambda b, i: (0, 0, 0)),
                  pl.BlockSpec((None, kin, 2 * w), lambda b, i: (0, 0, 0)),
                  pl.BlockSpec((None, kin, 2 * w), lambda b, i: (1, 0, 0)),
                  pl.BlockSpec((1, w), lambda b, i: (0, 0)),
                  pl.BlockSpec((nsel, nr), lambda b, i: (0, 0)),
                  pl.BlockSpec((tq, w), lambda b, i: (b * nq + i, 0))],
        out_specs=[pl.BlockSpec((tq, w), lambda b, i: (b * nq + i, 0)),
                   pl.BlockSpec((1, NSA_GROUPS, nsel, tq), lambda b, i: (b, 0, 0, i))],
        out_shape=[jax.ShapeDtypeStruct((t, w), F32),
                   jax.ShapeDtypeStruct((bsz, NSA_GROUPS, nsel, seq), F32)],
        scratch_shapes=[pltpu.VMEM((nr, w), BF16), pltpu.VMEM((nr, w), BF16), pltpu.VMEM((nsel, tq), F32)],
        compiler_params=_cp("parallel", "arbitrary"),
        name="nsa_cmp",
    )(xr, pe2, w2, w2, kg, jnp.asarray(ov.T, BF16), qn)


def _nsa_attn_kernel(qr_ref, ks_ref, kw_ref, vs_ref, vw_ref, sel_ref, ocmp_ref, gate_ref, o_ref, *, kc, wt):
    tq = qr_ref.shape[0]
    i = pl.program_id(1)
    hl = _head_of_lane((tq, GW), 1)
    nbk = kc // SEL_BLOCK
    groups = range(NSA_GROUPS)

    def stacked_q(g):
        q = qr_ref[:, g * GW:(g + 1) * GW]
        return jnp.concatenate([jnp.where(hl == 0, q, jnp.zeros_like(q)), jnp.where(hl == 1, q, jnp.zeros_like(q))],
                               axis=0)

    qs_all = [stacked_q(g) for g in groups]

    def lane_qpos(rows):
        return i * tq + lax.broadcasted_iota(jnp.int32, (rows, 2 * tq), 1) % tq

    def finish(acc, l):
        ot = (acc / l).T
        return jnp.where(hl == 0, ot[0:tq, :], ot[tq:2 * tq, :])

    qpos_s = lane_qpos(kc)
    krow_s = lax.broadcasted_iota(jnp.int32, (kc, 2 * tq), 0)

    def sel_step(g, c, carry, diagonal):
        m, l, acc = carry
        k0 = pl.multiple_of(c * kc, kc)
        st = _dot_nt(ks_ref[pl.ds(k0, kc), g * GW:(g + 1) * GW], qs_all[g])
        srows = sel_ref[0, g, pl.ds(pl.multiple_of(c * nbk, nbk), nbk), :]
        srows = jnp.concatenate([srows, srows], axis=1)
        smask = jnp.concatenate([jnp.broadcast_to(srows[r:r + 1, :], (SEL_BLOCK, 2 * tq)) for r in range(nbk)],
                                axis=0)
        msk = smask > 0.5
        if diagonal:
            msk = msk & (k0 + krow_s <= qpos_s)
        st = jnp.where(msk, st, NEG_INF)
        m_new = jnp.maximum(m, jnp.max(st, axis=0, keepdims=True))
        p = jnp.exp(st - m_new)
        alpha = jnp.exp(m - m_new)
        l = l * alpha + jnp.sum(p, axis=0, keepdims=True)
        acc = acc * alpha + _dot(vs_ref[0, g, c], p.astype(BF16))
        return m_new, l, acc

    def sel_body(c, carries, diagonal):
        return tuple(sel_step(g, c, carries[g], diagonal) for g in groups)

    init = (jnp.full((1, 2 * tq), NEG_INF, F32), jnp.zeros((1, 2 * tq), F32), jnp.zeros((GW, 2 * tq), F32))
    n_before = (i * tq) // kc
    carries = lax.fori_loop(0, n_before, functools.partial(sel_body, diagonal=False), (init,) * NSA_GROUPS)
    carries = sel_body(n_before, carries, True)

    j0 = jnp.maximum(i - (wt - 1), 0)
    k0 = pl.multiple_of(j0 * tq, tq)
    span = wt * tq
    kpos = k0 + lax.broadcasted_iota(jnp.int32, (span, 2 * tq), 0)
    qpos_w = lane_qpos(span)
    wmask = (kpos <= qpos_w) & (kpos > qpos_w - WINDOW)
    gb = _sigmoid(gate_ref[...])
    for g in groups:
        _, l_s, acc_s = carries[g]
        o_sel = finish(acc_s, l_s)
        st = jnp.where(wmask, _dot_nt(kw_ref[pl.ds(k0, span), g * GW:(g + 1) * GW], qs_all[g]), NEG_INF)
        p = jnp.exp(st - jnp.max(st, axis=0, keepdims=True))
        vt = jnp.concatenate([vw_ref[0, g, j0 + r] for r in range(wt)], axis=1)
        o_win = finish(_dot(vt, p.astype(BF16)), jnp.sum(p, axis=0, keepdims=True))

        def gate(branch):
            cols = [gb[:, (2 * g + hh) * 3 + branch:(2 * g + hh) * 3 + branch + 1] for hh in range(2)]
            return _expand_heads(cols, (tq, GW))

        o_ref[:, g * GW:(g + 1) * GW] = (gate(0) * ocmp_ref[:, g * GW:(g + 1) * GW]
                                         + gate(1) * o_sel + gate(2) * o_win)


def _nsa_attn(proj, qr, ks, kw, vst, vwt, sel, o_cmp, bsz, seq):
    t = bsz * seq
    w = MIX_WIDTH
    tq = min(NSA_TQ, seq)
    nq = seq // tq
    nsel = seq // SEL_BLOCK
    kc = min(NSA_KC, seq)
    wt = min(WINDOW // tq + 1, nq)
    kspec = pl.BlockSpec((seq, w), lambda b, i: (b, 0))
    return pl.pallas_call(
        functools.partial(_nsa_attn_kernel, kc=kc, wt=wt),
        grid=(bsz, nq),
        in_specs=[pl.BlockSpec((tq, w), lambda b, i: (b * nq + i, 0)),
                  kspec, kspec,
                  pl.BlockSpec((1, NSA_GROUPS, seq // kc, GW, kc), lambda b, i: (b, 0, 0, 0, 0)),
                  pl.BlockSpec((1, NSA_GROUPS, nq, GW, tq), lambda b, i: (b, 0, 0, 0, 0)),
                  pl.BlockSpec((1, NSA_GROUPS, nsel, tq), lambda b, i: (b, 0, 0, i)),
                  pl.BlockSpec((tq, w), lambda b, i: (b * nq + i, 0)),
                  pl.BlockSpec((tq, LANES), lambda b, i: (b * nq + i, OFF_NG // LANES))],
        out_specs=pl.BlockSpec((tq, w), lambda b, i: (b * nq + i, 0)),
        out_shape=jax.ShapeDtypeStruct((t, w), F32),
        compiler_params=_cp("parallel", "arbitrary"),
        name="nsa_attn",
    )(qr, ks, kw, vst, vwt, sel, o_cmp, proj)


def _nsa(proj, kcvc, cos4, sin4, qnorm_g, knorm_g, cmp_pe, cmp_w, bsz, seq):
    qn, qr, ks, kw, vst, vwt = _nsa_prep(proj, cos4, sin4, qnorm_g, knorm_g, bsz, seq)
    o_cmp, sel = _nsa_cmp(kcvc, qn, cmp_pe, cmp_w, knorm_g[0], bsz, seq)
    return _nsa_attn(proj, qr, ks, kw, vst, vwt, sel, o_cmp, bsz, seq)


PEER_TT = 128
PEER_CT = 8
HALF_D = 512


SUBLANES = 8
CODE_BITS = 127
FAR_BELOW = -3.0e38


def _with_code(x, code):
    bits = lax.bitcast_convert_type(x, jnp.int32)
    return lax.bitcast_convert_type((bits & ~CODE_BITS) | code, F32)


def _split_code(x):
    bits = lax.bitcast_convert_type(x, jnp.int32)
    return lax.bitcast_convert_type(bits & ~CODE_BITS, F32), bits & CODE_BITS


def _sort16_desc(xs):
    xs = list(xs)
    n = len(xs)
    k = 2
    while k <= n:
        j = k // 2
        while j >= 1:
            for i in range(n):
                l = i ^ j
                if l > i:
                    hi, lo = jnp.maximum(xs[i], xs[l]), jnp.minimum(xs[i], xs[l])
                    xs[i], xs[l] = (hi, lo) if (i & k) == 0 else (lo, hi)
            j //= 2
        k *= 2
    return xs


def _merge16_desc(xs):
    xs = list(xs)
    j = len(xs) // 2
    while j >= 1:
        for i in range(len(xs)):
            l = i ^ j
            if l > i:
                xs[i], xs[l] = jnp.maximum(xs[i], xs[l]), jnp.minimum(xs[i], xs[l])
        j //= 2
    return xs


def _top16_columns(x):
    n = PEER_TOPK
    xs = _sort16_desc([x[SUBLANES * j:SUBLANES * (j + 1), :] for j in range(n)])
    shift = SUBLANES // 2
    while shift >= 1:
        rolled = [pltpu.roll(a, shift, 0) for a in xs]
        xs = _merge16_desc([jnp.maximum(xs[i], rolled[n - 1 - i]) for i in range(n)])
        shift //= 2
    return xs


_PEER_CAND_TILES = ((0, 0, 8), (0, 1, 8), (1, 0, 8), (2, 0, 5), (3, 0, 4), (4, 0, 3), (5, 0, 2), (6, 0, 2), (7, 0, 2))


ROUTE_HEADS_PER_STEP = 4


def _route_head(q_ref, key_ref, hh):
    tt = q_ref.shape[0]
    nk = PEER_NKEYS
    n = PEER_TOPK
    row = lax.broadcasted_iota(jnp.int32, (nk, tt), 0)
    sub = lax.broadcasted_iota(jnp.int32, (SUBLANES, tt), 0)
    vals, ids = [], []
    for p in range(2):
        c0 = (2 * hh + p) * PEER_KDIM
        st = _dot_nt(key_ref[hh, p], q_ref[:, c0:c0 + PEER_KDIM])
        top = [_split_code(a) for a in _top16_columns(_with_code(st, (nk - 1) - row))]
        vals.append([v for v, _ in top])
        ids.append([(nk - 1) - c for _, c in top])
    (v1, v2), (i1, i2) = vals, ids

    def stack(xs, lo):
        out = xs[lo]
        for s in range(1, SUBLANES):
            out = jnp.where(sub == s, xs[lo + s], out)
        return out

    v2t, i2t = (stack(v2, 0), stack(v2, SUBLANES)), (stack(i2, 0), stack(i2, SUBLANES))
    cand, cexp = [], []
    for a, tile, nvalid in _PEER_CAND_TILES:
        v = v1[a] + v2t[tile]
        cand.append(v if nvalid == SUBLANES else jnp.where(sub < nvalid, v, FAR_BELOW))
        cexp.append(i1[a] * nk + i2t[tile])
    cand.append(stack(v1, SUBLANES) + v2[0])
    cexp.append(stack(i1, SUBLANES) * nk + i2[0])
    n_tiles = len(cand)
    slot_code = [(nk - 1) - (c * SUBLANES + sub) for c in range(n_tiles)]
    coded = [_with_code(v, sc) for v, sc in zip(cand, slot_code)]
    coded += [jnp.full((SUBLANES, tt), FAR_BELOW, F32)] * (n - n_tiles)
    top = [_split_code(a) for a in _top16_columns(jnp.concatenate(coded, axis=0))]
    call = jnp.concatenate(cexp, axis=0)
    slot = (nk - 1) - lax.broadcasted_iota(jnp.int32, call.shape, 0)
    ex = [jnp.exp(v - top[0][0]) for v, _ in top]
    tot = ex[0]
    for k in range(1, n):
        tot = tot + ex[k]
    krow = lax.broadcasted_iota(jnp.int32, (n, tt), 0)
    e_tile = jnp.zeros((n, tt), F32)
    g_tile = jnp.zeros((n, tt), F32)
    for k in range(n):
        hit = slot == jnp.concatenate([top[k][1]] * n_tiles, axis=0)
        e_k = jnp.sum(jnp.where(hit, call, 0), axis=0, keepdims=True)
        e_tile = jnp.where(krow == k, e_k.astype(F32), e_tile)
        g_tile = jnp.where(krow == k, (ex[k] / tot)[0:1, :], g_tile)
    return e_tile, g_tile


def _peer_route_kernel(q_ref, key_ref, e_ref, g_ref, e_scr, g_scr):
    hps = key_ref.shape[0]
    tiles = [_route_head(q_ref, key_ref, hh) for hh in range(hps)]
    rows = hps * PEER_TOPK
    r0 = pl.multiple_of(pl.program_id(1) * rows, rows)
    e_scr[pl.ds(r0, rows), :] = jnp.concatenate([e for e, _ in tiles], axis=0)
    g_scr[pl.ds(r0, rows), :] = jnp.concatenate([g for _, g in tiles], axis=0)

    @pl.when(pl.program_id(1) == pl.num_programs(1) - 1)
    def _():
        e_ref[...] = e_scr[...].T.astype(jnp.int32)
        g_ref[...] = g_scr[...].T


def _peer_route(qp, keys):
    t = qp.shape[0]
    tt = min(PEER_TT, t)
    ne = PEER_HEADS * PEER_TOPK
    hps = ROUTE_HEADS_PER_STEP
    return pl.pallas_call(
        _peer_route_kernel,
        grid=(t // tt, PEER_HEADS // hps),
        in_specs=[pl.BlockSpec((tt, hps * 2 * PEER_KDIM), lambda i, h: (i, h)),
                  pl.BlockSpec((hps, 2, PEER_NKEYS, PEER_KDIM), lambda i, h: (h, 0, 0, 0))],
        out_specs=[pl.BlockSpec((tt, ne), lambda i, h: (i, 0)),
                   pl.BlockSpec((tt, ne), lambda i, h: (i, 0))],
        out_shape=[jax.ShapeDtypeStruct((t, ne), jnp.int32),
                   jax.ShapeDtypeStruct((t, ne), F32)],
        scratch_shapes=[pltpu.VMEM((ne, tt), F32), pltpu.VMEM((ne, tt), F32)],
        compiler_params=_cp("parallel", "arbitrary"),
        name="peer_route",
    )(qp, keys)


def _pack_tables_kernel(u_ref, v_ref, o_ref):
    def pack(x):
        lo = lax.bitcast_convert_type(x[:, 0:HALF_D].astype(BF16).astype(F32), jnp.int32)
        hi = lax.bitcast_convert_type(x[:, HALF_D:2 * HALF_D].astype(BF16).astype(F32), jnp.int32)
        return lax.shift_right_logical(lo, 16) | (hi & jnp.int32(-65536))

    o_ref[:, 0:HALF_D] = pack(u_ref[...])
    o_ref[:, HALF_D:2 * HALF_D] = pack(v_ref[...])


def _pack_tables(u_tabs, v_tabs, layer, tr=512):
    _, e, d = u_tabs.shape
    assert d == 2 * HALF_D
    spec_in = pl.BlockSpec((None, tr, d), lambda i: (layer, i, 0))
    spec = pl.BlockSpec((tr, d), lambda i: (i, 0))
    return pl.pallas_call(
        _pack_tables_kernel,
        grid=(e // tr,),
        in_specs=[spec_in, spec_in],
        out_specs=spec,
        out_shape=jax.ShapeDtypeStruct((e, d), jnp.int32),
        compiler_params=_cp("parallel"),
        name="peer_pack",
    )(u_tabs, v_tabs)


def _unpack_rows(wd):
    lo = lax.bitcast_convert_type(lax.shift_left(wd, 16), F32)
    hi = lax.bitcast_convert_type(lax.bitwise_and(wd, jnp.int32(-65536)), F32)
    return lo, hi


SC_WINDOW = 16


def _sc_gather(table, idx):
    from jax.experimental.pallas import tpu_sc as plsc
    n = idx.shape[0]
    width = table.shape[1]
    mesh = plsc.VectorSubcoreMesh(core_axis_name="core", subcore_axis_name="subcore")

    @functools.partial(pl.kernel, out_type=jax.ShapeDtypeStruct((n, width), table.dtype), mesh=mesh)
    def gather(tab_hbm, idx_hbm, out_hbm):
        def body(idx_vmem, out_vmem):
            pltpu.sync_copy(tab_hbm.at[idx_vmem.at[0, pl.ds(0, SC_WINDOW)]], out_vmem)

        pltpu.emit_pipeline(
            body,
            grid=(n // SC_WINDOW,),
            in_specs=[pl.BlockSpec((1, LANES), lambda i: (0, i))],
            out_specs=[pl.BlockSpec((SC_WINDOW, width), lambda i: (i, 0))],
            core_axis_name=("core", "subcore"),
            dimension_semantics=(pltpu.PARALLEL,),
            trace_scopes=False,
        )(idx_hbm, out_hbm)

    idx_pad = jnp.pad(idx.reshape(n // SC_WINDOW, SC_WINDOW), ((0, 0), (0, LANES - SC_WINDOW)))
    return gather(table, idx_pad.reshape(1, (n // SC_WINDOW) * LANES))


def _peer_combine_kernel(x_ref, g2_ref, rows_a_ref, rows_b_ref, gate_ref, o_ref):
    ne = PEER_HEADS * PEER_TOPK
    x = x_ref[...]
    ct = x.shape[0]
    xn = x * lax.rsqrt(jnp.mean(x * x, axis=-1, keepdims=True) + NORM_EPS) * g2_ref[...]
    gate_t = jnp.concatenate([gate_ref[...]] * (ne // ct), axis=0).T
    for jj in range(ct):
        rows_ref, j = (rows_a_ref, jj) if jj < ct // 2 else (rows_b_ref, jj - ct // 2)
        u_lo, u_hi = _unpack_rows(rows_ref[j * ne:(j + 1) * ne, 0:HALF_D])
        xr = xn[jj:jj + 1, :]
        h = jnp.sum(u_lo * xr[:, 0:HALF_D] + u_hi * xr[:, HALF_D:2 * HALF_D], axis=1, keepdims=True)
        act = 0.5 * h * (1.0 + lax.erf(h * (2.0 ** -0.5)))
        wgt = gate_t[:, jj:jj + 1] * act
        v_lo, v_hi = _unpack_rows(rows_ref[j * ne:(j + 1) * ne, HALF_D:2 * HALF_D])
        o_ref[jj:jj + 1, 0:HALF_D] = x[jj:jj + 1, 0:HALF_D] + jnp.sum(wgt * v_lo, axis=0, keepdims=True)
        o_ref[jj:jj + 1, HALF_D:2 * HALF_D] = (x[jj:jj + 1, HALF_D:2 * HALF_D]
                                               + jnp.sum(wgt * v_hi, axis=0, keepdims=True))


def _peer_combine(x, g2, rows, gates, first_token):
    t, d = x.shape
    ne = PEER_HEADS * PEER_TOPK
    ct = PEER_CT
    steps = rows.shape[0] // (ct * ne)
    off = first_token // ct
    return pl.pallas_call(
        _peer_combine_kernel,
        grid=(steps,),
        in_specs=[pl.BlockSpec((ct, d), lambda i: (off + i, 0)),
                  pl.BlockSpec((1, d), lambda i: (0, 0)),
                  pl.BlockSpec((ct * ne // 2, d), lambda i: (2 * i, 0)),
                  pl.BlockSpec((ct * ne // 2, d), lambda i: (2 * i + 1, 0)),
                  pl.BlockSpec((ct, ne), lambda i: (off + i, 0))],
        out_specs=pl.BlockSpec((ct, d), lambda i: (off + i, 0)),
        out_shape=jax.ShapeDtypeStruct((t, d), F32),
        input_output_aliases={0: 0},
        compiler_params=_cp("parallel"),
        name="peer_combine",
    )(x, g2.reshape(1, d), rows, rows, gates)


PEER_TOKENS_PER_GATHER = 2048


def _peer_route_stage(x, g2, wq_b, keys_b):
    t = x.shape[0]
    ne = PEER_HEADS * PEER_TOPK
    qp = _norm_matmul(x, g2, wq_b, out_dtype=BF16)
    e_tok, g_tok = _peer_route(qp, keys_b)
    return e_tok.reshape(t * ne), g_tok


def _peer_gather_stage(table, idx, t, gather_fn):
    ne = PEER_HEADS * PEER_TOPK
    tc = min(PEER_TOKENS_PER_GATHER, t)
    return [gather_fn(table, idx[c * tc * ne:(c + 1) * tc * ne]) for c in range(t // tc)]


def _peer_combine_stage(x, g2, rows_list, gates):
    tc = x.shape[0] // len(rows_list)
    for c, rows in enumerate(rows_list):
        x = _peer_combine(x, g2, rows, gates, c * tc)
    return x


def _peer(x, g2, wq, keys, u_tabs, v_tabs, layer, gather_fn):
    idx, gates = _peer_route_stage(x, g2, wq.astype(BF16), keys.astype(BF16))
    table = _pack_tables(u_tabs, v_tabs, layer)
    rows_list = _peer_gather_stage(table, idx, x.shape[0], gather_fn)
    return _peer_combine_stage(x, g2, rows_list, gates)


_IN_WIDTHS = (256, 256, 256, 256, 256, 256, 256, 4, 4, 256, 256, 128, 128, 128, 128, 128, 128, 12,
              256, 256, 256, 256)


def _dup_groups(wcols):
    g0, g1 = wcols[:, :HEAD_DIM], wcols[:, HEAD_DIM:]
    return jnp.concatenate([g0, g0, g1, g1], axis=1)


def _layout_w_in(w_in):
    offs = np.cumsum((0,) + _IN_WIDTHS)
    cols = [w_in[:, offs[i]:offs[i + 1]] for i in range(len(_IN_WIDTHS))]
    (hq, hf, hi, hg, mq, mk, mv, mi, mf, mo, nq, nkc, nvc, nks, nvs, nkw, nvw, ng, rq, rk, rv, rg) = cols
    d = w_in.shape[0]
    pad = lambda c, n: jnp.concatenate([c, jnp.zeros((d, n - c.shape[1]), w_in.dtype)], axis=1)
    main = jnp.concatenate([hq, hf, hi, hg, mq, mk, mv, mo, rq, rk, rv, rg,
                            _dup_groups(nks), _dup_groups(nkw), nq, nvs, nvw,
                            pad(jnp.concatenate([mi, mf], axis=1), LANES), pad(ng, LANES)], axis=1)
    assert main.shape[1] == N_MAIN
    kcvc = jnp.concatenate([nkc, nvc], axis=1)
    return main.astype(BF16), kcvc.astype(BF16)


def kernel(x, norm1_g, w_in, hgrn_lb, hgrn_onorm_g, mlstm_conv_w, mlstm_conv_b, mlstm_gate_b, mlstm_onorm_g, nsa_qnorm_g, nsa_knorm_g, nsa_cmp_pe, nsa_cmp_w, ret_onorm_g, w_up, w_gate, w_out, norm2_g, peer_wq, peer_keys, peer_u, peer_v):
    bsz, seq, d = x.shape
    t = bsz * seq
    depth = w_in.shape[0]
    cos_t, sin_t = _rope_lane_tables(seq)
    cos4, sin4 = jnp.tile(cos_t, (1, 2)), jnp.tile(sin_t, (1, 2))
    lb_cum = jnp.cumsum(jax.nn.softmax(hgrn_lb.astype(F32), axis=0), axis=0)
    lb_all = lb_cum - lb_cum[0:1]
    weights = []
    for l in range(depth):
        w_main, w_kcvc = _layout_w_in(w_in[l])
        weights.append(dict(
            main=w_main, kcvc=w_kcvc, gate=w_gate[l].astype(BF16), up=w_up[l].astype(BF16),
            out=w_out[l].astype(BF16), lb=_hgrn_lb_rows(lb_all[l]), wq=peer_wq[l].astype(BF16),
            keys=peer_keys[l].astype(BF16), table=_pack_tables(peer_u, peer_v, l)))

    def mixer_steps(xh, l, nb):
        wl = weights[l]
        st = {}

        def proj(dep):
            st["proj"] = _norm_matmul(xh, _after(norm1_g[l], dep), wl["main"])
            return st["proj"]

        def gates(dep):
            st["gates"] = _norm_matmul(xh, _after(norm1_g[l], dep), wl["gate"], act="sigmoid", out_dtype=BF16)
            return st["gates"]

        def hgrn(dep):
            st["oh"] = _hgrn(st["proj"], wl["lb"], _after(hgrn_onorm_g[l], dep), nb, seq)
            return st["oh"]

        def mlstm(dep):
            st["om"] = _mlstm(st["proj"], mlstm_conv_w[l], mlstm_conv_b[l], mlstm_gate_b[l],
                              _after(mlstm_onorm_g[l], dep), nb, seq)
            return st["om"]

        def ret(dep):
            st["or"] = _ret(st["proj"], cos4, sin4, _after(ret_onorm_g[l], dep), nb, seq)
            return st["or"]

        def nsa_front(dep):
            kcvc = _norm_matmul(xh, _after(norm1_g[l], dep), wl["kcvc"])
            qn, qr, ks, kw, vst, vwt = _nsa_prep(st["proj"], cos4, sin4, nsa_qnorm_g[l], nsa_knorm_g[l], nb, seq)
            o_cmp, sel = _nsa_cmp(kcvc, qn, nsa_cmp_pe[l], nsa_cmp_w[l], nsa_knorm_g[l][0], nb, seq)
            st["nsa"] = (qr, ks, kw, vst, vwt, sel, o_cmp)
            return o_cmp

        def nsa_attn(dep):
            del dep
            st["on"] = _nsa_attn(st["proj"], *st["nsa"], nb, seq)
            return st["on"]

        def merge(dep):
            del dep
            st["xm"] = _merge(xh, st["gates"], (st["oh"], st["om"], st["on"], st["or"]), wl["up"], wl["out"])
            return st["xm"]

        def route(dep):
            st["idx"], st["pgates"] = _peer_route_stage(st["xm"], _after(norm2_g[l], dep), wl["wq"], wl["keys"])
            return st["pgates"]

        return [proj, gates, hgrn, mlstm, ret, nsa_front, nsa_attn, merge, route], st

    combine_slots = (0, 8, 8, 8, 8, 8, 8, 8)

    def combine_steps(l, xm, rows_list, pgates):
        box = {"x": xm}
        tc = xm.shape[0] // len(rows_list)

        def make(c):
            def step(dep):
                box["x"] = _peer_combine(box["x"], _after(norm2_g[l], dep), rows_list[c], pgates, c * tc)
                return box["x"]
            return step

        return [make(c) for c in range(len(rows_list))], box

    n_groups = next(n for n in (4, 2, 1) if bsz % n == 0)
    nb = bsz // n_groups
    xs = [x[g * nb:(g + 1) * nb].reshape(nb * seq, d) for g in range(n_groups)]
    dep = None
    lag = min(2, n_groups - 1)
    pending = []
    for l in range(depth):
        for g in range(n_groups):
            msteps, st = mixer_steps(xs[g], l, nb)
            due = pending.pop(0) if len(pending) == lag and lag > 0 else None
            csteps = due[1] if due is not None else []
            ci = 0
            for si, mstep in enumerate(msteps):
                dep = mstep(dep)
                while ci < len(csteps) and (ci >= len(combine_slots) or combine_slots[ci] <= si):
                    dep = csteps[ci](dep)
                    ci += 1
            for cstep in csteps[ci:]:
                dep = cstep(dep)
            if due is not None:
                xs[due[0]] = due[2]["x"]
            rows_list = _peer_gather_stage(weights[l]["table"], st["idx"], nb * seq, _sc_gather)
            csteps, box = combine_steps(l, st["xm"], rows_list, st["pgates"])
            pending.append((g, csteps, box))
            if lag == 0:
                for cstep in pending.pop(0)[1]:
                    dep = cstep(dep)
                xs[g] = box["x"]
    for pg, csteps, box in pending:
        for cstep in csteps:
            dep = cstep(dep)
        xs[pg] = box["x"]
    return jnp.concatenate(xs, axis=0).reshape(bsz, seq, d)
```

```python
import functools
import math

import numpy as np
import jax
import jax.numpy as jnp
from jax import lax
from jax.experimental import pallas as pl
from jax.experimental.pallas import tpu as pltpu

F32 = jnp.float32
BF16 = jnp.bfloat16

HEAD_DIM = 64
N_HEADS = 4
MIX_WIDTH = N_HEADS * HEAD_DIM
CHUNK = 64
NORM_EPS = 1e-6
NEG_INF = -1e30
ROPE_THETA = 10000.0
CONV_W = 4
NSA_GROUPS = 2
CMP_LEN = 32
CMP_STRIDE = 16
SEL_BLOCK = 64
SEL_TOPK = 16
WINDOW = 512
FORCE_BONUS = 1e3
PEER_HEADS = 8
PEER_NKEYS = 128
PEER_TOPK = 16
PEER_KDIM = 128

LANES = 128
VMEM_LIMIT = 48 * 1024 * 1024

OFF_H, OFF_M, OFF_R, OFF_KD, OFF_NQ, OFF_V, OFF_MG, OFF_NG = 0, 1024, 2048, 3072, 3584, 3840, 4096, 4224
N_MAIN = 4352


def _cp(*sem):
    return pltpu.CompilerParams(dimension_semantics=sem, vmem_limit_bytes=VMEM_LIMIT)


def _dot(a, b):
    return jnp.dot(a, b, preferred_element_type=F32)


def _dot_nt(a, b):
    return lax.dot_general(a, b, (((1,), (1,)), ((), ())), preferred_element_type=F32)


def _dot_tn(a, b):
    return lax.dot_general(a, b, (((0,), (0,)), ((), ())), preferred_element_type=F32)


def _split3(x):
    hi = x.astype(BF16)
    r1 = x - hi.astype(F32)
    mid = r1.astype(BF16)
    lo = (r1 - mid.astype(F32)).astype(BF16)
    return hi, mid, lo


def _dot01_l(m01, x):
    hi, mid, lo = _split3(x)
    return _dot(m01, hi) + _dot(m01, mid) + _dot(m01, lo)


def _dot01_r(x, m01):
    hi, mid, lo = _split3(x)
    return _dot(hi, m01) + _dot(mid, m01) + _dot(lo, m01)


def _head_of_lane(shape, axis):
    return lax.broadcasted_iota(jnp.int32, shape, axis) // HEAD_DIM


def _block_ones(n, dtype=BF16):
    r = lax.broadcasted_iota(jnp.int32, (n, n), 0) // HEAD_DIM
    c = lax.broadcasted_iota(jnp.int32, (n, n), 1) // HEAD_DIM
    return (r == c).astype(dtype)


def _group_sum(x, ones_bd):
    hi = x.astype(BF16)
    lo = (x - hi.astype(F32)).astype(BF16)
    return _dot(hi, ones_bd) + _dot(lo, ones_bd)


def _head_rms(x, gain, ones_bd):
    ms = _group_sum(x * x, ones_bd) * (1.0 / HEAD_DIM)
    return x * lax.rsqrt(ms + NORM_EPS) * gain


def _sigmoid(x):
    return 1.0 / (1.0 + jnp.exp(-x))


def _silu(x):
    return x * _sigmoid(x)


def _log_sigmoid(x):
    return jnp.minimum(x, 0.0) - jnp.log(1.0 + jnp.exp(-jnp.abs(x)))


def _stack_heads(x, n_heads=N_HEADS):
    hl = _head_of_lane(x.shape, 1)
    return jnp.concatenate([jnp.where(hl == h, x, jnp.zeros_like(x)) for h in range(n_heads)], axis=0)


def _unstack_heads(r, c, n_heads=N_HEADS):
    hl = _head_of_lane((c, r.shape[1]), 1)
    out = jnp.zeros((c, r.shape[1]), F32)
    for h in range(n_heads):
        out = jnp.where(hl == h, r[h * c:(h + 1) * c, :], out)
    return out


def _rope(x, cos_t, sin_t):
    n = x.shape[1]
    first = (lax.broadcasted_iota(jnp.int32, x.shape, 1) % HEAD_DIM) < (HEAD_DIM // 2)
    partner = jnp.where(first, pltpu.roll(x, n - HEAD_DIM // 2, 1), pltpu.roll(x, HEAD_DIM // 2, 1))
    return x * cos_t + partner * sin_t


def _after_kernel(a_ref, dep_ref, o_ref):
    del dep_ref
    o_ref[...] = a_ref[...]


def _after(a, dep):
    if dep is None:
        return a
    a2 = a.reshape(1, a.size)
    out = pl.pallas_call(
        _after_kernel,
        in_specs=[pl.BlockSpec(a2.shape, lambda: (0, 0)), pl.BlockSpec(memory_space=pl.ANY)],
        out_specs=pl.BlockSpec(a2.shape, lambda: (0, 0)),
        out_shape=jax.ShapeDtypeStruct(a2.shape, a2.dtype),
        name="order_after",
    )(a2, dep)
    return out.reshape(a.shape)


def _norm_matmul_kernel(x_ref, g_ref, w_ref, o_ref, xn_ref, *, act):
    @pl.when(pl.program_id(1) == 0)
    def _():
        x = x_ref[...]
        ms = jnp.mean(x * x, axis=-1, keepdims=True)
        xn_ref[...] = (x * lax.rsqrt(ms + NORM_EPS) * g_ref[...]).astype(BF16)

    y = _dot(xn_ref[...], w_ref[...])
    if act == "sigmoid":
        y = _sigmoid(y)
    o_ref[...] = y.astype(o_ref.dtype)


def _norm_matmul(x, g, w, *, act=None, out_dtype=F32, tm=1024, tn=2176):
    t, d = x.shape
    w3 = w if w.ndim == 3 else w[None]
    n_per = w3.shape[2]
    tm = min(tm, t)
    tn = next(c for c in (tn, 2048, 1024, 512, 256, 128) if n_per % c == 0)
    per = n_per // tn
    n = w3.shape[0] * n_per
    assert t % tm == 0
    return pl.pallas_call(
        functools.partial(_norm_matmul_kernel, act=act),
        grid=(t // tm, n // tn),
        in_specs=[pl.BlockSpec((tm, d), lambda i, j: (i, 0)),
                  pl.BlockSpec((1, d), lambda i, j: (0, 0)),
                  pl.BlockSpec((None, d, tn), lambda i, j: (j // per, 0, j % per))],
        out_specs=pl.BlockSpec((tm, tn), lambda i, j: (i, j)),
        out_shape=jax.ShapeDtypeStruct((t, n), out_dtype),
        scratch_shapes=[pltpu.VMEM((tm, d), BF16)],
        compiler_params=_cp("parallel", "arbitrary"),
        name="norm_matmul",
    )(x, g.reshape(1, d), w3)


def _merge_kernel(x_ref, gate_ref, oh_ref, om_ref, on_ref, or_ref, wup_ref, wout_ref, o_ref):
    d = x_ref.shape[1]
    acc = None
    for m, r in enumerate((oh_ref, om_ref, on_ref, or_ref)):
        up = _dot(r[...].astype(BF16), wup_ref[m])
        term = gate_ref[:, m * d:(m + 1) * d].astype(F32) * up
        acc = term if acc is None else acc + term
    o_ref[...] = x_ref[...] + _dot(acc.astype(BF16), wout_ref[...])


def _merge(x, gates, outs, w_up, w_out, tm=512):
    t, d = x.shape
    tm = min(tm, t)
    mix = pl.BlockSpec((tm, MIX_WIDTH), lambda i: (i, 0))
    return pl.pallas_call(
        _merge_kernel,
        grid=(t // tm,),
        in_specs=[pl.BlockSpec((tm, d), lambda i: (i, 0)),
                  pl.BlockSpec((tm, 4 * d), lambda i: (i, 0)),
                  mix, mix, mix, mix,
                  pl.BlockSpec((4, MIX_WIDTH, d), lambda i: (0, 0, 0)),
                  pl.BlockSpec((d, d), lambda i: (0, 0))],
        out_specs=pl.BlockSpec((tm, d), lambda i: (i, 0)),
        out_shape=jax.ShapeDtypeStruct((t, d), F32),
        compiler_params=_cp("parallel"),
        name="merge",
    )(x, gates, *outs, w_up, w_out)


REC_BLOCK = 256


def _chunk_consts():
    t = lax.broadcasted_iota(jnp.int32, (CHUNK, CHUNK), 0)
    s = lax.broadcasted_iota(jnp.int32, (CHUNK, CHUNK), 1)
    return t, s


def _hgrn_levels():
    t = np.arange(CHUNK)
    rows = []
    masks = []
    h = CHUNK // 2
    while h >= 1:
        ref = (t // (2 * h)) * (2 * h) + h
        p = np.zeros((CHUNK, CHUNK), np.float32)
        p[t, np.minimum(ref, CHUNK - 1)] = 1.0
        rows.append(p)
        same = (t[:, None] // (2 * h)) == (t[None, :] // (2 * h))
        m = same & ((t[:, None] // h) % 2 == 1) & ((t[None, :] // h) % 2 == 0)
        masks.append(m.astype(np.float32))
        h //= 2
    masks.append(np.eye(CHUNK, dtype=np.float32))
    return np.concatenate(rows, 0), np.stack(masks, 0)


def _hgrn_kernel(p_ref, lb_ref, g_ref, psel_ref, lmask_ref, o_ref, st_ref):
    @pl.when(pl.program_id(1) == 0)
    def _():
        st_ref[...] = jnp.zeros_like(st_ref)

    c = CHUNK
    w = MIX_WIDTH
    ones_bd = _block_ones(w)
    bd_mask = _block_ones(w, F32)
    tri = (lax.broadcasted_iota(jnp.int32, (c, c), 0) >= lax.broadcasted_iota(jnp.int32, (c, c), 1)).astype(BF16)
    psel = psel_ref[...]
    n_lv = lmask_ref.shape[0]
    log_lb, log_1mlb, one_mlb = lb_ref[0:1, :], lb_ref[1:2, :], lb_ref[2:3, :]
    gain = g_ref[...]

    def chunk(ci, carry):
        r0 = pl.multiple_of(ci * c, c)
        q = _silu(p_ref[pl.ds(r0, c), 0:w])
        fl = p_ref[pl.ds(r0, c), w:2 * w]
        v = p_ref[pl.ds(r0, c), 2 * w:3 * w]
        gp = p_ref[pl.ds(r0, c), 3 * w:4 * w]
        a1 = jnp.broadcast_to(log_lb, fl.shape)
        a2 = log_1mlb + _log_sigmoid(fl)
        mx = jnp.maximum(a1, a2)
        log_f = mx + jnp.log(jnp.exp(a1 - mx) + jnp.exp(a2 - mx))
        k = one_mlb * _sigmoid(-fl)
        b = _dot01_l(tri, log_f)
        bref = _dot01_l(psel, b)
        vb = v.astype(BF16)
        a = jnp.zeros((N_HEADS * c, c), F32)
        for lv in range(n_lv):
            if lv < n_lv - 1:
                br = bref[lv * c:(lv + 1) * c, :]
                qs = q * jnp.exp(jnp.minimum(b - br, 0.0))
                ks = k * jnp.exp(jnp.minimum(br - b, 0.0))
            else:
                qs, ks = q, k
            s_lv = _dot_nt(_stack_heads(qs).astype(BF16), ks.astype(BF16))
            a = a + jnp.concatenate([lmask_ref[lv]] * N_HEADS, axis=0) * s_lv
        o = _unstack_heads(_dot(a.astype(BF16), vb), c)
        st = st_ref[...]
        o = o + _dot_nt((q * jnp.exp(b)).astype(BF16), st.astype(BF16))
        b_last = b[c - 1:c, :]
        kb = k * jnp.exp(b_last - b)
        st_ref[...] = st * jnp.exp(b_last) + bd_mask * _dot_tn(vb, kb.astype(BF16))
        y = _head_rms(o, gain, ones_bd) * _silu(gp)
        o_ref[pl.ds(r0, c), :] = y
        return carry

    lax.fori_loop(0, p_ref.shape[0] // c, chunk, 0)


def _hgrn(proj, lb_rows, gain, bsz, seq):
    psel, lmask = _hgrn_levels()
    tb = min(REC_BLOCK, seq)
    nb = seq // tb
    return pl.pallas_call(
        _hgrn_kernel,
        grid=(bsz, nb),
        in_specs=[pl.BlockSpec((tb, 4 * MIX_WIDTH), lambda b, i: (b * nb + i, OFF_H // (4 * MIX_WIDTH))),
                  pl.BlockSpec((8, MIX_WIDTH), lambda b, i: (0, 0)),
                  pl.BlockSpec((1, MIX_WIDTH), lambda b, i: (0, 0)),
                  pl.BlockSpec(psel.shape, lambda b, i: (0, 0)),
                  pl.BlockSpec(lmask.shape, lambda b, i: (0, 0, 0))],
        out_specs=pl.BlockSpec((tb, MIX_WIDTH), lambda b, i: (b * nb + i, 0)),
        out_shape=jax.ShapeDtypeStruct((bsz * seq, MIX_WIDTH), F32),
        scratch_shapes=[pltpu.VMEM((MIX_WIDTH, MIX_WIDTH), F32)],
        compiler_params=_cp("parallel", "arbitrary"),
        name="hgrn2",
    )(proj, lb_rows, gain.reshape(1, MIX_WIDTH), jnp.asarray(psel, BF16), jnp.asarray(lmask, F32))


def _ret_kernel(p_ref, cos_ref, sin_ref, dec_ref, decin_ref, g_ref, o_ref, st_ref):
    @pl.when(pl.program_id(1) == 0)
    def _():
        st_ref[...] = jnp.zeros_like(st_ref)

    c = CHUNK
    w = MIX_WIDTH
    ones_bd = _block_ones(w)
    bd_mask = _block_ones(w, F32)
    gain = g_ref[...]
    dec_q = dec_ref[0:c, :]
    dec_k = dec_ref[c:2 * c, :]
    dec_state = dec_ref[2 * c:2 * c + 1, :]
    dec_in = decin_ref[...]

    def chunk(ci, carry):
        r0 = pl.multiple_of(ci * c, c)
        cos_t = cos_ref[pl.ds(r0, c), :]
        sin_t = sin_ref[pl.ds(r0, c), :]
        q = _rope(p_ref[pl.ds(r0, c), 0:w], cos_t, sin_t)
        k = _rope(p_ref[pl.ds(r0, c), w:2 * w], cos_t, sin_t) * (HEAD_DIM ** -0.5)
        v = p_ref[pl.ds(r0, c), 2 * w:3 * w]
        gp = p_ref[pl.ds(r0, c), 3 * w:4 * w]
        vb = v.astype(BF16)
        a = _dot_nt(_stack_heads(q).astype(BF16), k.astype(BF16)) * dec_in
        o = _unstack_heads(_dot(a.astype(BF16), vb), c)
        st = st_ref[...]
        o = o + _dot_nt(q.astype(BF16), st.astype(BF16)) * dec_q
        st_ref[...] = st * dec_state + bd_mask * _dot_tn(vb, (k * dec_k).astype(BF16))
        o_ref[pl.ds(r0, c), :] = _head_rms(o, gain, ones_bd) * _silu(gp)
        return carry

    lax.fori_loop(0, p_ref.shape[0] // c, chunk, 0)


def _ret_consts():
    log_gamma = np.log1p(-np.exp2(-5.0 - np.arange(N_HEADS, dtype=np.float64)))
    t = np.arange(CHUNK, dtype=np.float64)
    lane_h = np.arange(MIX_WIDTH) // HEAD_DIM
    dec_q = np.exp(log_gamma[lane_h][None, :] * (t[:, None] + 1.0))
    dec_k = np.exp(log_gamma[lane_h][None, :] * (CHUNK - 1.0 - t[:, None]))
    dec_state = np.exp(log_gamma[lane_h] * CHUNK)[None, :]
    dec = np.concatenate([dec_q, dec_k, np.broadcast_to(dec_state, (8, MIX_WIDTH))], 0)
    diff = t[:, None] - t[None, :]
    dec_in = np.concatenate([np.where(diff >= 0, np.exp(log_gamma[h] * diff), 0.0) for h in range(N_HEADS)], 0)
    return dec.astype(np.float32), dec_in.astype(np.float32)


def _ret(proj, cos4, sin4, gain, bsz, seq):
    dec, dec_in = _ret_consts()
    tb = min(REC_BLOCK, seq)
    nb = seq // tb
    return pl.pallas_call(
        _ret_kernel,
        grid=(bsz, nb),
        in_specs=[pl.BlockSpec((tb, 4 * MIX_WIDTH), lambda b, i: (b * nb + i, OFF_R // (4 * MIX_WIDTH))),
                  pl.BlockSpec((tb, MIX_WIDTH), lambda b, i: (i, 0)),
                  pl.BlockSpec((tb, MIX_WIDTH), lambda b, i: (i, 0)),
                  pl.BlockSpec(dec.shape, lambda b, i: (0, 0)),
                  pl.BlockSpec(dec_in.shape, lambda b, i: (0, 0)),
                  pl.BlockSpec((1, MIX_WIDTH), lambda b, i: (0, 0))],
        out_specs=pl.BlockSpec((tb, MIX_WIDTH), lambda b, i: (b * nb + i, 0)),
        out_shape=jax.ShapeDtypeStruct((bsz * seq, MIX_WIDTH), F32),
        scratch_shapes=[pltpu.VMEM((MIX_WIDTH, MIX_WIDTH), F32)],
        compiler_params=_cp("parallel", "arbitrary"),
        name="retention",
    )(proj, cos4, sin4, jnp.asarray(dec), jnp.asarray(dec_in), gain.reshape(1, MIX_WIDTH))


def _hgrn_lb_rows(lb):
    lb = lb.astype(F32)
    rows = jnp.stack([jnp.log(lb), jnp.log1p(-lb), 1.0 - lb], 0)
    return jnp.concatenate([rows, jnp.zeros((5, lb.shape[0]), F32)], 0)


def _rope_lane_tables(seq):
    inv = 1.0 / (ROPE_THETA ** (jnp.arange(0, HEAD_DIM, 2, dtype=F32) / HEAD_DIM))
    ang = jnp.arange(seq, dtype=F32)[:, None] * inv[None, :]
    cos, sin = jnp.cos(ang), jnp.sin(ang)
    cos_t = jnp.tile(cos, (1, LANES // (HEAD_DIM // 2)))
    sin_t = jnp.tile(jnp.concatenate([-sin, sin], axis=1), (1, LANES // HEAD_DIM))
    return cos_t, sin_t


def _expand_heads(cols, shape):
    hl = _head_of_lane(shape, 1)
    out = jnp.broadcast_to(cols[-1], shape)
    for h in range(len(cols) - 2, -1, -1):
        out = jnp.where(hl == h, jnp.broadcast_to(cols[h], shape), out)
    return out


def _mlstm_kernel(p_ref, gcol_ref, grow_ref, cw_ref, cb_ref, gbr_ref, gbc_ref, g_ref, o_ref,
                  ct_ref, n_ref, m_ref, hist_ref, cbuf_ref, qk_ref):
    c = CHUNK
    w = MIX_WIDTH
    tb = p_ref.shape[0]

    @pl.when(pl.program_id(1) == 0)
    def _():
        ct_ref[...] = jnp.zeros_like(ct_ref)
        n_ref[...] = jnp.zeros_like(n_ref)
        m_ref[...] = jnp.zeros_like(m_ref)
        hist_ref[...] = jnp.zeros_like(hist_ref)

    cbuf_ref[0:8, :] = hist_ref[...]
    cbuf_ref[8:, :] = p_ref[:, 0:2 * w]
    hist_ref[...] = p_ref[tb - 8:tb, 0:2 * w]
    acc = jnp.broadcast_to(cb_ref[...], (tb, 2 * w))
    for j in range(CONV_W):
        acc = acc + cw_ref[j:j + 1, :] * cbuf_ref[pl.ds(8 - (CONV_W - 1) + j, tb), :]
    qk_ref[...] = _silu(acc)

    ones_bd = _block_ones(w)
    bd_mask = _block_ones(w, F32)
    ti = lax.broadcasted_iota(jnp.int32, (c, c), 0)
    si = lax.broadcasted_iota(jnp.int32, (c, c), 1)
    causal = ti >= si
    tri = causal.astype(BF16)
    tri_t = (ti <= si).astype(BF16)
    gain = g_ref[...]
    ones_ext = jnp.ones((c, LANES), BF16)

    def chunk(ci, carry):
        r0 = pl.multiple_of(ci * c, c)
        q = qk_ref[pl.ds(r0, c), 0:w]
        k = qk_ref[pl.ds(r0, c), w:2 * w] * (HEAD_DIM ** -0.5)
        v = p_ref[pl.ds(r0, c), 2 * w:3 * w]
        op = p_ref[pl.ds(r0, c), 3 * w:4 * w]
        gc = gcol_ref[pl.ds(r0, c), :] + gbr_ref[...]
        gr = grow_ref[ci] + gbc_ref[...]
        b_c = _dot01_l(tri, _log_sigmoid(gc))
        b_r = _dot01_r(_log_sigmoid(gr), tri_t)
        wd, s_inter, em, wk, decay = [], [], [], [], []
        for h in range(N_HEADS):
            bc = b_c[:, N_HEADS + h:N_HEADS + h + 1]
            lic = gc[:, h:h + 1]
            br = b_r[N_HEADS + h:N_HEADS + h + 1, :]
            lir = gr[h:h + 1, :]
            dmat = jnp.where(causal, bc - br + lir, -jnp.inf)
            m_prev = m_ref[h:h + 1, 0:1]
            inter = bc + m_prev
            mrow = jnp.maximum(inter, jnp.max(dmat, axis=1, keepdims=True))
            wd.append(jnp.exp(dmat - mrow))
            s_inter.append(jnp.exp(inter - mrow))
            em.append(jnp.exp(-mrow))
            b_last = br[:, c - 1:c]
            m_new = jnp.maximum(b_last + m_prev, jnp.max(b_last - br + lir, axis=1, keepdims=True))
            wk.append(jnp.exp(b_last - bc + lic - m_new))
            decay.append(jnp.exp(b_last + m_prev - m_new))
            m_ref[h:h + 1, :] = jnp.broadcast_to(m_new, (1, LANES))
        s_inter_l = _expand_heads(s_inter, (c, w))
        em_l = _expand_heads(em, (c, w))
        wk_l = _expand_heads(wk, (c, w))
        decay_l = _expand_heads(decay, (1, w))
        qk = _dot_nt(_stack_heads(q).astype(BF16), k.astype(BF16))
        wmat = jnp.concatenate(wd, axis=0) * qk
        vb = v.astype(BF16)
        r = _dot(wmat.astype(BF16), jnp.concatenate([vb, ones_ext], axis=1))
        num_intra = _unstack_heads(r[:, 0:w], c)
        rs_l = _expand_heads([r[h * c:(h + 1) * c, w:w + 1] for h in range(N_HEADS)], (c, w))
        ct = ct_ref[...]
        nrow = n_ref[0:1, :]
        num = s_inter_l * _dot_nt(q.astype(BF16), ct.astype(BF16)) + num_intra
        den = s_inter_l * _group_sum(q * nrow, ones_bd) + rs_l
        hval = num / jnp.maximum(jnp.abs(den), em_l)
        kw = wk_l * k
        ct_ref[...] = ct * decay_l + bd_mask * _dot_tn(vb, kw.astype(BF16))
        n_ref[0:1, :] = nrow * decay_l + jnp.sum(kw, axis=0, keepdims=True)
        o_ref[pl.ds(r0, c), :] = _head_rms(hval, gain, ones_bd) * _sigmoid(op)
        return carry

    lax.fori_loop(0, tb // c, chunk, 0)


def _mlstm(proj, conv_w, conv_b, gate_b, gain, bsz, seq):
    t = bsz * seq
    w = MIX_WIDTH
    tb = min(REC_BLOCK, seq)
    nb = seq // tb
    ncb = tb // CHUNK
    grow = proj[:, OFF_MG:OFF_MG + 8].reshape(t // CHUNK, CHUNK, 8).transpose(0, 2, 1)
    gb_row = jnp.zeros((1, LANES), F32).at[0, 0:8].set(gate_b.astype(F32))
    gb_col = gate_b.astype(F32).reshape(8, 1)
    return pl.pallas_call(
        _mlstm_kernel,
        grid=(bsz, nb),
        in_specs=[pl.BlockSpec((tb, 4 * w), lambda b, i: (b * nb + i, OFF_M // (4 * w))),
                  pl.BlockSpec((tb, LANES), lambda b, i: (b * nb + i, OFF_MG // LANES)),
                  pl.BlockSpec((ncb, 8, CHUNK), lambda b, i: (b * nb + i, 0, 0)),
                  pl.BlockSpec((CONV_W, 2 * w), lambda b, i: (0, 0)),
                  pl.BlockSpec((1, 2 * w), lambda b, i: (0, 0)),
                  pl.BlockSpec((1, LANES), lambda b, i: (0, 0)),
                  pl.BlockSpec((8, 1), lambda b, i: (0, 0)),
                  pl.BlockSpec((1, w), lambda b, i: (0, 0))],
        out_specs=pl.BlockSpec((tb, w), lambda b, i: (b * nb + i, 0)),
        out_shape=jax.ShapeDtypeStruct((t, w), F32),
        scratch_shapes=[pltpu.VMEM((w, w), F32), pltpu.VMEM((8, w), F32), pltpu.VMEM((8, LANES), F32),
                        pltpu.VMEM((8, 2 * w), F32), pltpu.VMEM((tb + 8, 2 * w), F32),
                        pltpu.VMEM((tb, 2 * w), F32)],
        compiler_params=_cp("parallel", "arbitrary"),
        name="mlstm",
    )(proj, proj, grow, conv_w.astype(F32), conv_b.astype(F32).reshape(1, 2 * w), gb_row, gb_col,
      gain.reshape(1, w))


NSA_TQ = 128
NSA_KC = 512
GW = 2 * HEAD_DIM


def _nsa_prep_kernel(pq_ref, pk_ref, pv_ref, cos_ref, sin_ref, qg_ref, kg_ref,
                     qn_ref, qr_ref, ks_ref, kw_ref, vst_ref, vwt_ref):
    w = MIX_WIDTH
    for src, dst in ((pv_ref[:, 0:GW], vst_ref), (pv_ref[:, GW:2 * GW], vwt_ref)):
        vt = src.T
        tk = dst.shape[4]
        for g in range(NSA_GROUPS):
            rows = vt[g * HEAD_DIM:(g + 1) * HEAD_DIM, :]
            dup = jnp.concatenate([rows, rows], axis=0).astype(BF16)
            for j in range(dst.shape[2]):
                dst[0, g, j] = dup[:, j * tk:(j + 1) * tk]
    ones_bd = _block_ones(w)
    cos_t, sin_t = cos_ref[...], sin_ref[...]
    scale = HEAD_DIM ** -0.5
    qh = _head_rms(pq_ref[...], qg_ref[...], ones_bd)
    qn_ref[...] = (qh * scale).astype(BF16)
    qr_ref[...] = (_rope(qh, cos_t, sin_t) * scale).astype(BF16)
    ks_ref[...] = _rope(_head_rms(pk_ref[:, 0:w], kg_ref[1:2, :], ones_bd), cos_t, sin_t).astype(BF16)
    kw_ref[...] = _rope(_head_rms(pk_ref[:, w:2 * w], kg_ref[2:3, :], ones_bd), cos_t, sin_t).astype(BF16)


def _nsa_prep(proj, cos4, sin4, qnorm_g, knorm_g, bsz, seq):
    t = bsz * seq
    w = MIX_WIDTH
    tm = min(NSA_KC, seq)
    tq = min(NSA_TQ, seq)
    ns = seq // tm
    qg = jnp.tile(qnorm_g.astype(F32), w // HEAD_DIM).reshape(1, w)
    kg = jnp.concatenate([jnp.tile(knorm_g.astype(F32), (1, w // HEAD_DIM)), jnp.zeros((5, w), F32)], axis=0)
    out = jax.ShapeDtypeStruct((t, w), BF16)
    row = pl.BlockSpec((tm, w), lambda i: (i, 0))
    return pl.pallas_call(
        _nsa_prep_kernel,
        grid=(t // tm,),
        in_specs=[pl.BlockSpec((tm, w), lambda i: (i, OFF_NQ // w)),
                  pl.BlockSpec((tm, 2 * w), lambda i: (i, OFF_KD // (2 * w))),
                  pl.BlockSpec((tm, 2 * GW), lambda i: (i, OFF_V // (2 * GW))),
                  pl.BlockSpec((tm, w), lambda i: (i % ns, 0)),
                  pl.BlockSpec((tm, w), lambda i: (i % ns, 0)),
                  pl.BlockSpec((1, w), lambda i: (0, 0)),
                  pl.BlockSpec((8, w), lambda i: (0, 0))],
        out_specs=[row, row, row, row,
                   pl.BlockSpec((1, NSA_GROUPS, 1, GW, tm), lambda i: (i // ns, 0, i % ns, 0, 0)),
                   pl.BlockSpec((1, NSA_GROUPS, tm // tq, GW, tq), lambda i: (i // ns, 0, i % ns, 0, 0))],
        out_shape=[out, out, out, out,
                   jax.ShapeDtypeStruct((bsz, NSA_GROUPS, seq // tm, GW, tm), BF16),
                   jax.ShapeDtypeStruct((bsz, NSA_GROUPS, seq // tq, GW, tq), BF16)],
        compiler_params=_cp("parallel"),
        name="nsa_prep",
    )(proj, proj, proj, cos4, sin4, qg, kg)


def _nsa_cmp_kernel(xr_ref, pe_ref, w0_ref, w1_ref, kg_ref, ovt_ref, qn_ref, ocmp_ref, sel_ref,
                    kc_ref, vc_ref, v_ref, *, n_top):
    tq = qn_ref.shape[0]
    nr = xr_ref.shape[0]
    nsel = sel_ref.shape[2]
    w = MIX_WIDTH

    @pl.when(pl.program_id(1) == 0)
    def _():
        xr = xr_ref[...]
        y0 = _dot((xr + pe_ref[0]).astype(BF16), w0_ref[...])
        y1 = _dot((xr + pe_ref[1]).astype(BF16), w1_ref[...])
        kv = y0 + pltpu.roll(y1, nr - 1, 0)
        kc_ref[...] = _head_rms(kv[:, 0:w], kg_ref[...], _block_ones(w)).astype(BF16)
        vc_ref[...] = kv[:, w:2 * w].astype(BF16)

    pos0 = pl.program_id(1) * tq
    hl = _head_of_lane((tq, GW), 1)
    pos_r = pos0 + lax.broadcasted_iota(jnp.int32, (tq, nr), 0)
    valid = lax.broadcasted_iota(jnp.int32, (tq, nr), 1) * CMP_STRIDE + (CMP_LEN - 1) <= pos_r
    pos_c = pos0 + lax.broadcasted_iota(jnp.int32, (nr, tq), 1)
    valid_t = lax.broadcasted_iota(jnp.int32, (nr, tq), 0) * CMP_STRIDE + (CMP_LEN - 1) <= pos_c
    jrow = lax.broadcasted_iota(jnp.int32, (nsel, tq), 0)
    cur = (pos0 + lax.broadcasted_iota(jnp.int32, (nsel, tq), 1)) // SEL_BLOCK
    forced = (jrow == 0) | (jrow == cur) | (jrow == cur - 1)
    ovt = ovt_ref[...]

    for g in range(NSA_GROUPS):
        qg = qn_ref[:, g * GW:(g + 1) * GW]
        kg = kc_ref[:, g * GW:(g + 1) * GW]
        vg = vc_ref[:, g * GW:(g + 1) * GW]
        o_g = jnp.zeros((tq, GW), F32)
        pt_sum = jnp.zeros((nr, tq), F32)
        for hh in range(2):
            qm = jnp.where(hl == hh, qg, jnp.zeros_like(qg))
            s = jnp.where(valid, _dot_nt(qm, kg), NEG_INF)
            e = jnp.exp(s - jnp.max(s, axis=1, keepdims=True))
            p = jnp.where(valid, e / jnp.sum(e, axis=1, keepdims=True), 0.0)
            o_g = jnp.where(hl == hh, _dot(p.astype(BF16), vg), o_g)
            st = jnp.where(valid_t, _dot_nt(kg, qm), NEG_INF)
            et = jnp.exp(st - jnp.max(st, axis=0, keepdims=True))
            pt_sum = pt_sum + jnp.where(valid_t, et / jnp.sum(et, axis=0, keepdims=True), 0.0)
        ocmp_ref[:, g * GW:(g + 1) * GW] = o_g
        p_hi = pt_sum.astype(BF16)
        p_lo = (pt_sum - p_hi.astype(F32)).astype(BF16)
        imp = _dot(ovt, p_hi) + _dot(ovt, p_lo)
        val = jnp.where(jrow <= cur, imp + FORCE_BONUS * forced.astype(F32), NEG_INF)
        v_ref[...] = val

        def rank(jp, cnt):
            row = v_ref[pl.ds(jp, 1), :]
            tie = jnp.where(jrow > jp, 1.0, 0.0)
            return cnt + jnp.where(row > val, 1.0, jnp.where(row == val, tie, 0.0))

        cnt = lax.fori_loop(0, nsel, rank, jnp.zeros((nsel, tq), F32))
        sel_ref[0, g] = ((cnt < n_top) & (jrow <= cur)).astype(F32)


def _nsa_cmp_weights(cmp_pe, cmp_w):
    half = CMP_LEN // 2
    wl = cmp_w.astype(F32).reshape(2, 2, half, HEAD_DIM, HEAD_DIM)
    eye2 = jnp.eye(2, dtype=F32)
    w2 = jnp.einsum('kardz,kK,gG,h->arkgdKGhz', wl, eye2, eye2, jnp.ones((2,), F32))
    w2 = w2.reshape(2, half * 4 * HEAD_DIM, 8 * HEAD_DIM)
    pl_ = cmp_pe.astype(F32).reshape(2, 2, half, HEAD_DIM)
    pe2 = jnp.broadcast_to(pl_.transpose(1, 2, 0, 3)[:, :, :, None, :], (2, half, 2, 2, HEAD_DIM))
    return w2.astype(BF16), pe2.reshape(2, 1, half * 4 * HEAD_DIM)


def _nsa_cmp(kcvc, qn, cmp_pe, cmp_w, knorm0, bsz, seq, tq=512):
    t = bsz * seq
    w = MIX_WIDTH
    tq = min(tq, seq)
    nq = seq // tq
    nr = seq // CMP_STRIDE
    nsel = seq // SEL_BLOCK
    n_top = min(SEL_TOPK, nsel)
    w2, pe2 = _nsa_cmp_weights(cmp_pe, cmp_w)
    xr = kcvc.reshape(t // CMP_STRIDE, CMP_STRIDE * w)
    kg = jnp.tile(knorm0.astype(F32), w // HEAD_DIM).reshape(1, w)
    n_i = np.arange(nr)[:, None] * CMP_STRIDE
    j_i = np.arange(nsel)[None, :] * SEL_BLOCK
    ov = ((n_i < j_i + SEL_BLOCK) & (n_i + CMP_LEN > j_i)).astype(np.float32)
    ov[nr - 1, :] = 0.0
    kin = CMP_STRIDE * w
    return pl.pallas_call(
        functools.partial(_nsa_cmp_kernel, n_top=n_top),
        grid=(bsz, nq),
        in_specs=[pl.BlockSpec((nr, kin), lambda b, i: (b, 0)),
                  pl.BlockSpec((2, 1, kin), lambda b, i: (0, 0, 0)),
                  pl.BlockSpec((None, kin, 2 * w), lambda b, i: (0, 0, 0)),
                  pl.BlockSpec((None, kin, 2 * w), lambda b, i: (1, 0, 0)),
                  pl.BlockSpec((1, w), lambda b, i: (0, 0)),
                  pl.BlockSpec((nsel, nr), lambda b, i: (0, 0)),
                  pl.BlockSpec((tq, w), lambda b, i: (b * nq + i, 0))],
        out_specs=[pl.BlockSpec((tq, w), lambda b, i: (b * nq + i, 0)),
                   pl.BlockSpec((1, NSA_GROUPS, nsel, tq), lambda b, i: (b, 0, 0, i))],
        out_shape=[jax.ShapeDtypeStruct((t, w), F32),
                   jax.ShapeDtypeStruct((bsz, NSA_GROUPS, nsel, seq), F32)],
        scratch_shapes=[pltpu.VMEM((nr, w), BF16), pltpu.VMEM((nr, w), BF16), pltpu.VMEM((nsel, tq), F32)],
        compiler_params=_cp("parallel", "arbitrary"),
        name="nsa_cmp",
    )(xr, pe2, w2, w2, kg, jnp.asarray(ov.T, BF16), qn)


def _nsa_attn_kernel(qr_ref, ks_ref, kw_ref, vs_ref, vw_ref, sel_ref, ocmp_ref, gate_ref, o_ref, *, kc, wt):
    tq = qr_ref.shape[0]
    i = pl.program_id(1)
    hl = _head_of_lane((tq, GW), 1)
    nbk = kc // SEL_BLOCK
    groups = range(NSA_GROUPS)

    def stacked_q(g):
        q = qr_ref[:, g * GW:(g + 1) * GW]
        return jnp.concatenate([jnp.where(hl == 0, q, jnp.zeros_like(q)), jnp.where(hl == 1, q, jnp.zeros_like(q))],
                               axis=0)

    qs_all = [stacked_q(g) for g in groups]

    def lane_qpos(rows):
        return i * tq + lax.broadcasted_iota(jnp.int32, (rows, 2 * tq), 1) % tq

    def finish(acc, l):
        ot = (acc / l).T
        return jnp.where(hl == 0, ot[0:tq, :], ot[tq:2 * tq, :])

    qpos_s = lane_qpos(kc)
    krow_s = lax.broadcasted_iota(jnp.int32, (kc, 2 * tq), 0)

    def sel_step(g, c, carry, diagonal):
        m, l, acc = carry
        k0 = pl.multiple_of(c * kc, kc)
        st = _dot_nt(ks_ref[pl.ds(k0, kc), g * GW:(g + 1) * GW], qs_all[g])
        srows = sel_ref[0, g, pl.ds(pl.multiple_of(c * nbk, nbk), nbk), :]
        srows = jnp.concatenate([srows, srows], axis=1)
        smask = jnp.concatenate([jnp.broadcast_to(srows[r:r + 1, :], (SEL_BLOCK, 2 * tq)) for r in range(nbk)],
                                axis=0)
        msk = smask > 0.5
        if diagonal:
            msk = msk & (k0 + krow_s <= qpos_s)
        st = jnp.where(msk, st, NEG_INF)
        m_new = jnp.maximum(m, jnp.max(st, axis=0, keepdims=True))
        p = jnp.exp(st - m_new)
        alpha = jnp.exp(m - m_new)
        l = l * alpha + jnp.sum(p, axis=0, keepdims=True)
        acc = acc * alpha + _dot(vs_ref[0, g, c], p.astype(BF16))
        return m_new, l, acc

    def sel_body(c, carries, diagonal):
        return tuple(sel_step(g, c, carries[g], diagonal) for g in groups)

    init = (jnp.full((1, 2 * tq), NEG_INF, F32), jnp.zeros((1, 2 * tq), F32), jnp.zeros((GW, 2 * tq), F32))
    n_before = (i * tq) // kc
    carries = lax.fori_loop(0, n_before, functools.partial(sel_body, diagonal=False), (init,) * NSA_GROUPS)
    carries = sel_body(n_before, carries, True)

    j0 = jnp.maximum(i - (wt - 1), 0)
    k0 = pl.multiple_of(j0 * tq, tq)
    span = wt * tq
    kpos = k0 + lax.broadcasted_iota(jnp.int32, (span, 2 * tq), 0)
    qpos_w = lane_qpos(span)
    wmask = (kpos <= qpos_w) & (kpos > qpos_w - WINDOW)
    gb = _sigmoid(gate_ref[...])
    for g in groups:
        _, l_s, acc_s = carries[g]
        o_sel = finish(acc_s, l_s)
        st = jnp.where(wmask, _dot_nt(kw_ref[pl.ds(k0, span), g * GW:(g + 1) * GW], qs_all[g]), NEG_INF)
        p = jnp.exp(st - jnp.max(st, axis=0, keepdims=True))
        vt = jnp.concatenate([vw_ref[0, g, j0 + r] for r in range(wt)], axis=1)
        o_win = finish(_dot(vt, p.astype(BF16)), jnp.sum(p, axis=0, keepdims=True))

        def gate(branch):
            cols = [gb[:, (2 * g + hh) * 3 + branch:(2 * g + hh) * 3 + branch + 1] for hh in range(2)]
            return _expand_heads(cols, (tq, GW))

        o_ref[:, g * GW:(g + 1) * GW] = (gate(0) * ocmp_ref[:, g * GW:(g + 1) * GW]
                                         + gate(1) * o_sel + gate(2) * o_win)


def _nsa_attn(proj, qr, ks, kw, vst, vwt, sel, o_cmp, bsz, seq):
    t = bsz * seq
    w = MIX_WIDTH
    tq = min(NSA_TQ, seq)
    nq = seq // tq
    nsel = seq // SEL_BLOCK
    kc = min(NSA_KC, seq)
    wt = min(WINDOW // tq + 1, nq)
    kspec = pl.BlockSpec((seq, w), lambda b, i: (b, 0))
    return pl.pallas_call(
        functools.partial(_nsa_attn_kernel, kc=kc, wt=wt),
        grid=(bsz, nq),
        in_specs=[pl.BlockSpec((tq, w), lambda b, i: (b * nq + i, 0)),
                  kspec, kspec,
                  pl.BlockSpec((1, NSA_GROUPS, seq // kc, GW, kc), lambda b, i: (b, 0, 0, 0, 0)),
                  pl.BlockSpec((1, NSA_GROUPS, nq, GW, tq), lambda b, i: (b, 0, 0, 0, 0)),
                  pl.BlockSpec((1, NSA_GROUPS, nsel, tq), lambda b, i: (b, 0, 0, i)),
                  pl.BlockSpec((tq, w), lambda b, i: (b * nq + i, 0)),
                  pl.BlockSpec((tq, LANES), lambda b, i: (b * nq + i, OFF_NG // LANES))],
        out_specs=pl.BlockSpec((tq, w), lambda b, i: (b * nq + i, 0)),
        out_shape=jax.ShapeDtypeStruct((t, w), F32),
        compiler_params=_cp("parallel", "arbitrary"),
        name="nsa_attn",
    )(qr, ks, kw, vst, vwt, sel, o_cmp, proj)


def _nsa(proj, kcvc, cos4, sin4, qnorm_g, knorm_g, cmp_pe, cmp_w, bsz, seq):
    qn, qr, ks, kw, vst, vwt = _nsa_prep(proj, cos4, sin4, qnorm_g, knorm_g, bsz, seq)
    o_cmp, sel = _nsa_cmp(kcvc, qn, cmp_pe, cmp_w, knorm_g[0], bsz, seq)
    return _nsa_attn(proj, qr, ks, kw, vst, vwt, sel, o_cmp, bsz, seq)


PEER_TT = 128
PEER_CT = 8
HALF_D = 512


SUBLANES = 8
CODE_BITS = 127
FAR_BELOW = -3.0e38


def _with_code(x, code):
    bits = lax.bitcast_convert_type(x, jnp.int32)
    return lax.bitcast_convert_type((bits & ~CODE_BITS) | code, F32)


def _split_code(x):
    bits = lax.bitcast_convert_type(x, jnp.int32)
    return lax.bitcast_convert_type(bits & ~CODE_BITS, F32), bits & CODE_BITS


def _sort16_desc(xs):
    xs = list(xs)
    n = len(xs)
    k = 2
    while k <= n:
        j = k // 2
        while j >= 1:
            for i in range(n):
                l = i ^ j
                if l > i:
                    hi, lo = jnp.maximum(xs[i], xs[l]), jnp.minimum(xs[i], xs[l])
                    xs[i], xs[l] = (hi, lo) if (i & k) == 0 else (lo, hi)
            j //= 2
        k *= 2
    return xs


def _merge16_desc(xs):
    xs = list(xs)
    j = len(xs) // 2
    while j >= 1:
        for i in range(len(xs)):
            l = i ^ j
            if l > i:
                xs[i], xs[l] = jnp.maximum(xs[i], xs[l]), jnp.minimum(xs[i], xs[l])
        j //= 2
    return xs


def _top16_columns(x):
    n = PEER_TOPK
    xs = _sort16_desc([x[SUBLANES * j:SUBLANES * (j + 1), :] for j in range(n)])
    shift = SUBLANES // 2
    while shift >= 1:
        rolled = [pltpu.roll(a, shift, 0) for a in xs]
        xs = _merge16_desc([jnp.maximum(xs[i], rolled[n - 1 - i]) for i in range(n)])
        shift //= 2
    return xs


_PEER_CAND_TILES = ((0, 0, 8), (0, 1, 8), (1, 0, 8), (2, 0, 5), (3, 0, 4), (4, 0, 3), (5, 0, 2), (6, 0, 2), (7, 0, 2))


ROUTE_HEADS_PER_STEP = 4


def _route_head(q_ref, key_ref, hh):
    tt = q_ref.shape[0]
    nk = PEER_NKEYS
    n = PEER_TOPK
    row = lax.broadcasted_iota(jnp.int32, (nk, tt), 0)
    sub = lax.broadcasted_iota(jnp.int32, (SUBLANES, tt), 0)
    vals, ids = [], []
    for p in range(2):
        c0 = (2 * hh + p) * PEER_KDIM
        st = _dot_nt(key_ref[hh, p], q_ref[:, c0:c0 + PEER_KDIM])
        top = [_split_code(a) for a in _top16_columns(_with_code(st, (nk - 1) - row))]
        vals.append([v for v, _ in top])
        ids.append([(nk - 1) - c for _, c in top])
    (v1, v2), (i1, i2) = vals, ids

    def stack(xs, lo):
        out = xs[lo]
        for s in range(1, SUBLANES):
            out = jnp.where(sub == s, xs[lo + s], out)
        return out

    v2t, i2t = (stack(v2, 0), stack(v2, SUBLANES)), (stack(i2, 0), stack(i2, SUBLANES))
    cand, cexp = [], []
    for a, tile, nvalid in _PEER_CAND_TILES:
        v = v1[a] + v2t[tile]
        cand.append(v if nvalid == SUBLANES else jnp.where(sub < nvalid, v, FAR_BELOW))
        cexp.append(i1[a] * nk + i2t[tile])
    cand.append(stack(v1, SUBLANES) + v2[0])
    cexp.append(stack(i1, SUBLANES) * nk + i2[0])
    n_tiles = len(cand)
    slot_code = [(nk - 1) - (c * SUBLANES + sub) for c in range(n_tiles)]
    coded = [_with_code(v, sc) for v, sc in zip(cand, slot_code)]
    coded += [jnp.full((SUBLANES, tt), FAR_BELOW, F32)] * (n - n_tiles)
    top = [_split_code(a) for a in _top16_columns(jnp.concatenate(coded, axis=0))]
    call = jnp.concatenate(cexp, axis=0)
    slot = (nk - 1) - lax.broadcasted_iota(jnp.int32, call.shape, 0)
    ex = [jnp.exp(v - top[0][0]) for v, _ in top]
    tot = ex[0]
    for k in range(1, n):
        tot = tot + ex[k]
    krow = lax.broadcasted_iota(jnp.int32, (n, tt), 0)
    e_tile = jnp.zeros((n, tt), F32)
    g_tile = jnp.zeros((n, tt), F32)
    for k in range(n):
        hit = slot == jnp.concatenate([top[k][1]] * n_tiles, axis=0)
        e_k = jnp.sum(jnp.where(hit, call, 0), axis=0, keepdims=True)
        e_tile = jnp.where(krow == k, e_k.astype(F32), e_tile)
        g_tile = jnp.where(krow == k, (ex[k] / tot)[0:1, :], g_tile)
    return e_tile, g_tile


def _peer_route_kernel(q_ref, key_ref, e_ref, g_ref, e_scr, g_scr):
    hps = key_ref.shape[0]
    tiles = [_route_head(q_ref, key_ref, hh) for hh in range(hps)]
    rows = hps * PEER_TOPK
    r0 = pl.multiple_of(pl.program_id(1) * rows, rows)
    e_scr[pl.ds(r0, rows), :] = jnp.concatenate([e for e, _ in tiles], axis=0)
    g_scr[pl.ds(r0, rows), :] = jnp.concatenate([g for _, g in tiles], axis=0)

    @pl.when(pl.program_id(1) == pl.num_programs(1) - 1)
    def _():
        e_ref[...] = e_scr[...].T.astype(jnp.int32)
        g_ref[...] = g_scr[...].T


def _peer_route(qp, keys):
    t = qp.shape[0]
    tt = min(PEER_TT, t)
    ne = PEER_HEADS * PEER_TOPK
    hps = ROUTE_HEADS_PER_STEP
    return pl.pallas_call(
        _peer_route_kernel,
        grid=(t // tt, PEER_HEADS // hps),
        in_specs=[pl.BlockSpec((tt, hps * 2 * PEER_KDIM), lambda i, h: (i, h)),
                  pl.BlockSpec((hps, 2, PEER_NKEYS, PEER_KDIM), lambda i, h: (h, 0, 0, 0))],
        out_specs=[pl.BlockSpec((tt, ne), lambda i, h: (i, 0)),
                   pl.BlockSpec((tt, ne), lambda i, h: (i, 0))],
        out_shape=[jax.ShapeDtypeStruct((t, ne), jnp.int32),
                   jax.ShapeDtypeStruct((t, ne), F32)],
        scratch_shapes=[pltpu.VMEM((ne, tt), F32), pltpu.VMEM((ne, tt), F32)],
        compiler_params=_cp("parallel", "arbitrary"),
        name="peer_route",
    )(qp, keys)


def _pack_tables_kernel(u_ref, v_ref, o_ref):
    def pack(x):
        lo = lax.bitcast_convert_type(x[:, 0:HALF_D].astype(BF16).astype(F32), jnp.int32)
        hi = lax.bitcast_convert_type(x[:, HALF_D:2 * HALF_D].astype(BF16).astype(F32), jnp.int32)
        return lax.shift_right_logical(lo, 16) | (hi & jnp.int32(-65536))

    o_ref[:, 0:HALF_D] = pack(u_ref[...])
    o_ref[:, HALF_D:2 * HALF_D] = pack(v_ref[...])


def _pack_tables(u_tabs, v_tabs, layer, tr=512):
    _, e, d = u_tabs.shape
    assert d == 2 * HALF_D
    spec_in = pl.BlockSpec((None, tr, d), lambda i: (layer, i, 0))
    spec = pl.BlockSpec((tr, d), lambda i: (i, 0))
    return pl.pallas_call(
        _pack_tables_kernel,
        grid=(e // tr,),
        in_specs=[spec_in, spec_in],
        out_specs=spec,
        out_shape=jax.ShapeDtypeStruct((e, d), jnp.int32),
        compiler_params=_cp("parallel"),
        name="peer_pack",
    )(u_tabs, v_tabs)


def _unpack_rows(wd):
    lo = lax.bitcast_convert_type(lax.shift_left(wd, 16), F32)
    hi = lax.bitcast_convert_type(lax.bitwise_and(wd, jnp.int32(-65536)), F32)
    return lo, hi


SC_WINDOWS = (32, 16)


def _sc_gather(table, idx, window):
    from jax.experimental.pallas import tpu_sc as plsc
    n = idx.shape[0]
    width = table.shape[1]
    mesh = plsc.VectorSubcoreMesh(core_axis_name="core", subcore_axis_name="subcore")

    @functools.partial(pl.kernel, out_type=jax.ShapeDtypeStruct((n, width), table.dtype), mesh=mesh)
    def gather(tab_hbm, idx_hbm, out_hbm):
        def body(idx_vmem, out_vmem):
            pltpu.sync_copy(tab_hbm.at[idx_vmem.at[0, pl.ds(0, window)]], out_vmem)

        pltpu.emit_pipeline(
            body,
            grid=(n // window,),
            in_specs=[pl.BlockSpec((1, LANES), lambda i: (0, i))],
            out_specs=[pl.BlockSpec((window, width), lambda i: (i, 0))],
            core_axis_name=("core", "subcore"),
            dimension_semantics=(pltpu.PARALLEL,),
            trace_scopes=False,
        )(idx_hbm, out_hbm)

    idx_pad = jnp.pad(idx.reshape(n // window, window), ((0, 0), (0, LANES - window)))
    return gather(table, idx_pad.reshape(1, (n // window) * LANES))


def _peer_combine_kernel(x_ref, g2_ref, rows_a_ref, rows_b_ref, gate_ref, o_ref):
    ne = PEER_HEADS * PEER_TOPK
    x = x_ref[...]
    ct = x.shape[0]
    xn = x * lax.rsqrt(jnp.mean(x * x, axis=-1, keepdims=True) + NORM_EPS) * g2_ref[...]
    gate_t = jnp.concatenate([gate_ref[...]] * (ne // ct), axis=0).T
    for jj in range(ct):
        rows_ref, j = (rows_a_ref, jj) if jj < ct // 2 else (rows_b_ref, jj - ct // 2)
        u_lo, u_hi = _unpack_rows(rows_ref[j * ne:(j + 1) * ne, 0:HALF_D])
        xr = xn[jj:jj + 1, :]
        h = jnp.sum(u_lo * xr[:, 0:HALF_D] + u_hi * xr[:, HALF_D:2 * HALF_D], axis=1, keepdims=True)
        act = 0.5 * h * (1.0 + lax.erf(h * (2.0 ** -0.5)))
        wgt = gate_t[:, jj:jj + 1] * act
        v_lo, v_hi = _unpack_rows(rows_ref[j * ne:(j + 1) * ne, HALF_D:2 * HALF_D])
        o_ref[jj:jj + 1, 0:HALF_D] = x[jj:jj + 1, 0:HALF_D] + jnp.sum(wgt * v_lo, axis=0, keepdims=True)
        o_ref[jj:jj + 1, HALF_D:2 * HALF_D] = (x[jj:jj + 1, HALF_D:2 * HALF_D]
                                               + jnp.sum(wgt * v_hi, axis=0, keepdims=True))


def _peer_combine(x, g2, rows, gates, first_token):
    t, d = x.shape
    ne = PEER_HEADS * PEER_TOPK
    ct = PEER_CT
    steps = rows.shape[0] // (ct * ne)
    off = first_token // ct
    return pl.pallas_call(
        _peer_combine_kernel,
        grid=(steps,),
        in_specs=[pl.BlockSpec((ct, d), lambda i: (off + i, 0)),
                  pl.BlockSpec((1, d), lambda i: (0, 0)),
                  pl.BlockSpec((ct * ne // 2, d), lambda i: (2 * i, 0)),
                  pl.BlockSpec((ct * ne // 2, d), lambda i: (2 * i + 1, 0)),
                  pl.BlockSpec((ct, ne), lambda i: (off + i, 0))],
        out_specs=pl.BlockSpec((ct, d), lambda i: (off + i, 0)),
        out_shape=jax.ShapeDtypeStruct((t, d), F32),
        input_output_aliases={0: 0},
        compiler_params=_cp("parallel"),
        name="peer_combine",
    )(x, g2.reshape(1, d), rows, rows, gates)


PEER_TOKENS_PER_GATHER = 2048


def _peer_route_stage(x, g2, wq_b, keys_b):
    t = x.shape[0]
    ne = PEER_HEADS * PEER_TOPK
    qp = _norm_matmul(x, g2, wq_b, out_dtype=BF16)
    e_tok, g_tok = _peer_route(qp, keys_b)
    return e_tok.reshape(t * ne), g_tok


def _peer_gather_stage(table, idx, t, gather_fn):
    ne = PEER_HEADS * PEER_TOPK
    tc = min(PEER_TOKENS_PER_GATHER, t)
    return [gather_fn(table, idx[c * tc * ne:(c + 1) * tc * ne], SC_WINDOWS[c % len(SC_WINDOWS)])
            for c in range(t // tc)]


def _peer_combine_stage(x, g2, rows_list, gates):
    tc = x.shape[0] // len(rows_list)
    for c, rows in enumerate(rows_list):
        x = _peer_combine(x, g2, rows, gates, c * tc)
    return x


def _peer(x, g2, wq, keys, u_tabs, v_tabs, layer, gather_fn):
    idx, gates = _peer_route_stage(x, g2, wq.astype(BF16), keys.astype(BF16))
    table = _pack_tables(u_tabs, v_tabs, layer)
    rows_list = _peer_gather_stage(table, idx, x.shape[0], gather_fn)
    return _peer_combine_stage(x, g2, rows_list, gates)


_IN_WIDTHS = (256, 256, 256, 256, 256, 256, 256, 4, 4, 256, 256, 128, 128, 128, 128, 128, 128, 12,
              256, 256, 256, 256)


def _dup_groups(wcols):
    g0, g1 = wcols[:, :HEAD_DIM], wcols[:, HEAD_DIM:]
    return jnp.concatenate([g0, g0, g1, g1], axis=1)


def _layout_w_in(w_in):
    offs = np.cumsum((0,) + _IN_WIDTHS)
    cols = [w_in[:, offs[i]:offs[i + 1]] for i in range(len(_IN_WIDTHS))]
    (hq, hf, hi, hg, mq, mk, mv, mi, mf, mo, nq, nkc, nvc, nks, nvs, nkw, nvw, ng, rq, rk, rv, rg) = cols
    d = w_in.shape[0]
    pad = lambda c, n: jnp.concatenate([c, jnp.zeros((d, n - c.shape[1]), w_in.dtype)], axis=1)
    main = jnp.concatenate([hq, hf, hi, hg, mq, mk, mv, mo, rq, rk, rv, rg,
                            _dup_groups(nks), _dup_groups(nkw), nq, nvs, nvw,
                            pad(jnp.concatenate([mi, mf], axis=1), LANES), pad(ng, LANES)], axis=1)
    assert main.shape[1] == N_MAIN
    kcvc = jnp.concatenate([nkc, nvc], axis=1)
    return main.astype(BF16), kcvc.astype(BF16)


def kernel(x, norm1_g, w_in, hgrn_lb, hgrn_onorm_g, mlstm_conv_w, mlstm_conv_b, mlstm_gate_b, mlstm_onorm_g, nsa_qnorm_g, nsa_knorm_g, nsa_cmp_pe, nsa_cmp_w, ret_onorm_g, w_up, w_gate, w_out, norm2_g, peer_wq, peer_keys, peer_u, peer_v):
    bsz, seq, d = x.shape
    t = bsz * seq
    depth = w_in.shape[0]
    cos_t, sin_t = _rope_lane_tables(seq)
    cos4, sin4 = jnp.tile(cos_t, (1, 2)), jnp.tile(sin_t, (1, 2))
    lb_cum = jnp.cumsum(jax.nn.softmax(hgrn_lb.astype(F32), axis=0), axis=0)
    lb_all = lb_cum - lb_cum[0:1]
    weights = []
    for l in range(depth):
        w_main, w_kcvc = _layout_w_in(w_in[l])
        weights.append(dict(
            main=w_main, kcvc=w_kcvc, gate=w_gate[l].astype(BF16), up=w_up[l].astype(BF16),
            out=w_out[l].astype(BF16), lb=_hgrn_lb_rows(lb_all[l]), wq=peer_wq[l].astype(BF16),
            keys=peer_keys[l].astype(BF16), table=_pack_tables(peer_u, peer_v, l)))

    def mixer_steps(xh, l, nb):
        wl = weights[l]
        st = {}

        def proj(dep):
            st["proj"] = _norm_matmul(xh, _after(norm1_g[l], dep), wl["main"])
            return st["proj"]

        def gates(dep):
            st["gates"] = _norm_matmul(xh, _after(norm1_g[l], dep), wl["gate"], act="sigmoid", out_dtype=BF16)
            return st["gates"]

        def hgrn(dep):
            st["oh"] = _hgrn(st["proj"], wl["lb"], _after(hgrn_onorm_g[l], dep), nb, seq)
            return st["oh"]

        def mlstm(dep):
            st["om"] = _mlstm(st["proj"], mlstm_conv_w[l], mlstm_conv_b[l], mlstm_gate_b[l],
                              _after(mlstm_onorm_g[l], dep), nb, seq)
            return st["om"]

        def ret(dep):
            st["or"] = _ret(st["proj"], cos4, sin4, _after(ret_onorm_g[l], dep), nb, seq)
            return st["or"]

        def nsa_front(dep):
            kcvc = _norm_matmul(xh, _after(norm1_g[l], dep), wl["kcvc"])
            qn, qr, ks, kw, vst, vwt = _nsa_prep(st["proj"], cos4, sin4, nsa_qnorm_g[l], nsa_knorm_g[l], nb, seq)
            o_cmp, sel = _nsa_cmp(kcvc, qn, nsa_cmp_pe[l], nsa_cmp_w[l], nsa_knorm_g[l][0], nb, seq)
            st["nsa"] = (qr, ks, kw, vst, vwt, sel, o_cmp)
            return o_cmp

        def nsa_attn(dep):
            del dep
            st["on"] = _nsa_attn(st["proj"], *st["nsa"], nb, seq)
            return st["on"]

        def merge(dep):
            del dep
            st["xm"] = _merge(xh, st["gates"], (st["oh"], st["om"], st["on"], st["or"]), wl["up"], wl["out"])
            return st["xm"]

        def route(dep):
            st["idx"], st["pgates"] = _peer_route_stage(st["xm"], _after(norm2_g[l], dep), wl["wq"], wl["keys"])
            return st["pgates"]

        return [proj, gates, hgrn, mlstm, ret, nsa_front, nsa_attn, merge, route], st

    combine_slots = (0, 8, 8, 8, 8, 8, 8, 8)

    def combine_steps(l, xm, rows_list, pgates):
        box = {"x": xm}
        tc = xm.shape[0] // len(rows_list)

        def make(c):
            def step(dep):
                box["x"] = _peer_combine(box["x"], _after(norm2_g[l], dep), rows_list[c], pgates, c * tc)
                return box["x"]
            return step

        return [make(c) for c in range(len(rows_list))], box

    n_groups = next(n for n in (4, 2, 1) if bsz % n == 0)
    nb = bsz // n_groups
    xs = [x[g * nb:(g + 1) * nb].reshape(nb * seq, d) for g in range(n_groups)]
    dep = None
    lag = min(2, n_groups - 1)
    pending = []
    for l in range(depth):
        for g in range(n_groups):
            msteps, st = mixer_steps(xs[g], l, nb)
            due = pending.pop(0) if len(pending) == lag and lag > 0 else None
            csteps = due[1] if due is not None else []
            ci = 0
            for si, mstep in enumerate(msteps):
                dep = mstep(dep)
                while ci < len(csteps) and (ci >= len(combine_slots) or combine_slots[ci] <= si):
                    dep = csteps[ci](dep)
                    ci += 1
            for cstep in csteps[ci:]:
                dep = cstep(dep)
            if due is not None:
                xs[due[0]] = due[2]["x"]
            rows_list = _peer_gather_stage(weights[l]["table"], st["idx"], nb * seq, _sc_gather)
            csteps, box = combine_steps(l, st["xm"], rows_list, st["pgates"])
            pending.append((g, csteps, box))
            if lag == 0:
                for cstep in pending.pop(0)[1]:
                    dep = cstep(dep)
                xs[g] = box["x"]
    for pg, csteps, box in pending:
        for cstep in csteps:
            dep = cstep(dep)
        xs[pg] = box["x"]
    return jnp.concatenate(xs, axis=0).reshape(bsz, seq, d)
```

```python
import functools
import math

import numpy as np
import jax
import jax.numpy as jnp
from jax import lax
from jax.experimental import pallas as pl
from jax.experimental.pallas import tpu as pltpu

F32 = jnp.float32
BF16 = jnp.bfloat16

HEAD_DIM = 64
N_HEADS = 4
MIX_WIDTH = N_HEADS * HEAD_DIM
CHUNK = 64
NORM_EPS = 1e-6
NEG_INF = -1e30
ROPE_THETA = 10000.0
CONV_W = 4
NSA_GROUPS = 2
CMP_LEN = 32
CMP_STRIDE = 16
SEL_BLOCK = 64
SEL_TOPK = 16
WINDOW = 512
FORCE_BONUS = 1e3
PEER_HEADS = 8
PEER_NKEYS = 128
PEER_TOPK = 16
PEER_KDIM = 128

LANES = 128
VMEM_LIMIT = 48 * 1024 * 1024

OFF_H, OFF_M, OFF_R, OFF_KD, OFF_NQ, OFF_V, OFF_MG, OFF_NG = 0, 1024, 2048, 3072, 3584, 3840, 4096, 4224
N_MAIN = 4352


def _cp(*sem):
    return pltpu.CompilerParams(dimension_semantics=sem, vmem_limit_bytes=VMEM_LIMIT)


def _dot(a, b):
    return jnp.dot(a, b, preferred_element_type=F32)


def _dot_nt(a, b):
    return lax.dot_general(a, b, (((1,), (1,)), ((), ())), preferred_element_type=F32)


def _dot_tn(a, b):
    return lax.dot_general(a, b, (((0,), (0,)), ((), ())), preferred_element_type=F32)


def _split3(x):
    hi = x.astype(BF16)
    r1 = x - hi.astype(F32)
    mid = r1.astype(BF16)
    lo = (r1 - mid.astype(F32)).astype(BF16)
    return hi, mid, lo


def _dot01_l(m01, x):
    hi, mid, lo = _split3(x)
    return _dot(m01, hi) + _dot(m01, mid) + _dot(m01, lo)


def _dot01_r(x, m01):
    hi, mid, lo = _split3(x)
    return _dot(hi, m01) + _dot(mid, m01) + _dot(lo, m01)


def _head_of_lane(shape, axis):
    return lax.broadcasted_iota(jnp.int32, shape, axis) // HEAD_DIM


def _block_ones(n, dtype=BF16):
    r = lax.broadcasted_iota(jnp.int32, (n, n), 0) // HEAD_DIM
    c = lax.broadcasted_iota(jnp.int32, (n, n), 1) // HEAD_DIM
    return (r == c).astype(dtype)


def _group_sum(x, ones_bd):
    hi = x.astype(BF16)
    lo = (x - hi.astype(F32)).astype(BF16)
    return _dot(hi, ones_bd) + _dot(lo, ones_bd)


def _head_rms(x, gain, ones_bd):
    ms = _group_sum(x * x, ones_bd) * (1.0 / HEAD_DIM)
    return x * lax.rsqrt(ms + NORM_EPS) * gain


def _sigmoid(x):
    return 1.0 / (1.0 + jnp.exp(-x))


def _silu(x):
    return x * _sigmoid(x)


def _log_sigmoid(x):
    return jnp.minimum(x, 0.0) - jnp.log(1.0 + jnp.exp(-jnp.abs(x)))


def _stack_heads(x, n_heads=N_HEADS):
    hl = _head_of_lane(x.shape, 1)
    return jnp.concatenate([jnp.where(hl == h, x, jnp.zeros_like(x)) for h in range(n_heads)], axis=0)


def _unstack_heads(r, c, n_heads=N_HEADS):
    hl = _head_of_lane((c, r.shape[1]), 1)
    out = jnp.zeros((c, r.shape[1]), F32)
    for h in range(n_heads):
        out = jnp.where(hl == h, r[h * c:(h + 1) * c, :], out)
    return out


def _rope(x, cos_t, sin_t):
    n = x.shape[1]
    first = (lax.broadcasted_iota(jnp.int32, x.shape, 1) % HEAD_DIM) < (HEAD_DIM // 2)
    partner = jnp.where(first, pltpu.roll(x, n - HEAD_DIM // 2, 1), pltpu.roll(x, HEAD_DIM // 2, 1))
    return x * cos_t + partner * sin_t


def _after_kernel(a_ref, dep_ref, o_ref):
    del dep_ref
    o_ref[...] = a_ref[...]


def _after(a, dep):
    if dep is None:
        return a
    a2 = a.reshape(1, a.size)
    out = pl.pallas_call(
        _after_kernel,
        in_specs=[pl.BlockSpec(a2.shape, lambda: (0, 0)), pl.BlockSpec(memory_space=pl.ANY)],
        out_specs=pl.BlockSpec(a2.shape, lambda: (0, 0)),
        out_shape=jax.ShapeDtypeStruct(a2.shape, a2.dtype),
        name="order_after",
    )(a2, dep)
    return out.reshape(a.shape)


def _norm_matmul_kernel(x_ref, g_ref, w_ref, o_ref, xn_ref, *, act):
    @pl.when(pl.program_id(1) == 0)
    def _():
        x = x_ref[...]
        ms = jnp.mean(x * x, axis=-1, keepdims=True)
        xn_ref[...] = (x * lax.rsqrt(ms + NORM_EPS) * g_ref[...]).astype(BF16)

    y = _dot(xn_ref[...], w_ref[...])
    if act == "sigmoid":
        y = _sigmoid(y)
    o_ref[...] = y.astype(o_ref.dtype)


def _norm_matmul(x, g, w, *, act=None, out_dtype=F32, tm=1024, tn=2176):
    t, d = x.shape
    w3 = w if w.ndim == 3 else w[None]
    n_per = w3.shape[2]
    tm = min(tm, t)
    tn = next(c for c in (tn, 2048, 1024, 512, 256, 128) if n_per % c == 0)
    per = n_per // tn
    n = w3.shape[0] * n_per
    assert t % tm == 0
    return pl.pallas_call(
        functools.partial(_norm_matmul_kernel, act=act),
        grid=(t // tm, n // tn),
        in_specs=[pl.BlockSpec((tm, d), lambda i, j: (i, 0)),
                  pl.BlockSpec((1, d), lambda i, j: (0, 0)),
                  pl.BlockSpec((None, d, tn), lambda i, j: (j // per, 0, j % per))],
        out_specs=pl.BlockSpec((tm, tn), lambda i, j: (i, j)),
        out_shape=jax.ShapeDtypeStruct((t, n), out_dtype),
        scratch_shapes=[pltpu.VMEM((tm, d), BF16)],
        compiler_params=_cp("parallel", "arbitrary"),
        name="norm_matmul",
    )(x, g.reshape(1, d), w3)


def _merge_kernel(x_ref, gate_ref, oh_ref, om_ref, on_ref, or_ref, wup_ref, wout_ref, o_ref):
    d = x_ref.shape[1]
    acc = None
    for m, r in enumerate((oh_ref, om_ref, on_ref, or_ref)):
        up = _dot(r[...].astype(BF16), wup_ref[m])
        term = gate_ref[:, m * d:(m + 1) * d].astype(F32) * up
        acc = term if acc is None else acc + term
    o_ref[...] = x_ref[...] + _dot(acc.astype(BF16), wout_ref[...])


def _merge(x, gates, outs, w_up, w_out, tm=512):
    t, d = x.shape
    tm = min(tm, t)
    mix = pl.BlockSpec((tm, MIX_WIDTH), lambda i: (i, 0))
    return pl.pallas_call(
        _merge_kernel,
        grid=(t // tm,),
        in_specs=[pl.BlockSpec((tm, d), lambda i: (i, 0)),
                  pl.BlockSpec((tm, 4 * d), lambda i: (i, 0)),
                  mix, mix, mix, mix,
                  pl.BlockSpec((4, MIX_WIDTH, d), lambda i: (0, 0, 0)),
                  pl.BlockSpec((d, d), lambda i: (0, 0))],
        out_specs=pl.BlockSpec((tm, d), lambda i: (i, 0)),
        out_shape=jax.ShapeDtypeStruct((t, d), F32),
        compiler_params=_cp("parallel"),
        name="merge",
    )(x, gates, *outs, w_up, w_out)


REC_BLOCK = 256


def _chunk_consts():
    t = lax.broadcasted_iota(jnp.int32, (CHUNK, CHUNK), 0)
    s = lax.broadcasted_iota(jnp.int32, (CHUNK, CHUNK), 1)
    return t, s


def _hgrn_levels():
    t = np.arange(CHUNK)
    rows = []
    masks = []
    h = CHUNK // 2
    while h >= 1:
        ref = (t // (2 * h)) * (2 * h) + h
        p = np.zeros((CHUNK, CHUNK), np.float32)
        p[t, np.minimum(ref, CHUNK - 1)] = 1.0
        rows.append(p)
        same = (t[:, None] // (2 * h)) == (t[None, :] // (2 * h))
        m = same & ((t[:, None] // h) % 2 == 1) & ((t[None, :] // h) % 2 == 0)
        masks.append(m.astype(np.float32))
        h //= 2
    masks.append(np.eye(CHUNK, dtype=np.float32))
    return np.concatenate(rows, 0), np.stack(masks, 0)


def _hgrn_kernel(p_ref, lb_ref, g_ref, psel_ref, lmask_ref, o_ref, st_ref):
    @pl.when(pl.program_id(1) == 0)
    def _():
        st_ref[...] = jnp.zeros_like(st_ref)

    c = CHUNK
    w = MIX_WIDTH
    ones_bd = _block_ones(w)
    bd_mask = _block_ones(w, F32)
    tri = (lax.broadcasted_iota(jnp.int32, (c, c), 0) >= lax.broadcasted_iota(jnp.int32, (c, c), 1)).astype(BF16)
    psel = psel_ref[...]
    n_lv = lmask_ref.shape[0]
    log_lb, log_1mlb, one_mlb = lb_ref[0:1, :], lb_ref[1:2, :], lb_ref[2:3, :]
    gain = g_ref[...]

    def chunk(ci, carry):
        r0 = pl.multiple_of(ci * c, c)
        q = _silu(p_ref[pl.ds(r0, c), 0:w])
        fl = p_ref[pl.ds(r0, c), w:2 * w]
        v = p_ref[pl.ds(r0, c), 2 * w:3 * w]
        gp = p_ref[pl.ds(r0, c), 3 * w:4 * w]
        a1 = jnp.broadcast_to(log_lb, fl.shape)
        a2 = log_1mlb + _log_sigmoid(fl)
        mx = jnp.maximum(a1, a2)
        log_f = mx + jnp.log(jnp.exp(a1 - mx) + jnp.exp(a2 - mx))
        k = one_mlb * _sigmoid(-fl)
        b = _dot01_l(tri, log_f)
        bref = _dot01_l(psel, b)
        vb = v.astype(BF16)
        a = jnp.zeros((N_HEADS * c, c), F32)
        for lv in range(n_lv):
            if lv < n_lv - 1:
                br = bref[lv * c:(lv + 1) * c, :]
                qs = q * jnp.exp(jnp.minimum(b - br, 0.0))
                ks = k * jnp.exp(jnp.minimum(br - b, 0.0))
            else:
                qs, ks = q, k
            s_lv = _dot_nt(_stack_heads(qs).astype(BF16), ks.astype(BF16))
            a = a + jnp.concatenate([lmask_ref[lv]] * N_HEADS, axis=0) * s_lv
        o = _unstack_heads(_dot(a.astype(BF16), vb), c)
        st = st_ref[...]
        o = o + _dot_nt((q * jnp.exp(b)).astype(BF16), st.astype(BF16))
        b_last = b[c - 1:c, :]
        kb = k * jnp.exp(b_last - b)
        st_ref[...] = st * jnp.exp(b_last) + bd_mask * _dot_tn(vb, kb.astype(BF16))
        y = _head_rms(o, gain, ones_bd) * _silu(gp)
        o_ref[pl.ds(r0, c), :] = y
        return carry

    lax.fori_loop(0, p_ref.shape[0] // c, chunk, 0)


def _hgrn(proj, lb_rows, gain, bsz, seq):
    psel, lmask = _hgrn_levels()
    tb = min(REC_BLOCK, seq)
    nb = seq // tb
    return pl.pallas_call(
        _hgrn_kernel,
        grid=(bsz, nb),
        in_specs=[pl.BlockSpec((tb, 4 * MIX_WIDTH), lambda b, i: (b * nb + i, OFF_H // (4 * MIX_WIDTH))),
                  pl.BlockSpec((8, MIX_WIDTH), lambda b, i: (0, 0)),
                  pl.BlockSpec((1, MIX_WIDTH), lambda b, i: (0, 0)),
                  pl.BlockSpec(psel.shape, lambda b, i: (0, 0)),
                  pl.BlockSpec(lmask.shape, lambda b, i: (0, 0, 0))],
        out_specs=pl.BlockSpec((tb, MIX_WIDTH), lambda b, i: (b * nb + i, 0)),
        out_shape=jax.ShapeDtypeStruct((bsz * seq, MIX_WIDTH), F32),
        scratch_shapes=[pltpu.VMEM((MIX_WIDTH, MIX_WIDTH), F32)],
        compiler_params=_cp("parallel", "arbitrary"),
        name="hgrn2",
    )(proj, lb_rows, gain.reshape(1, MIX_WIDTH), jnp.asarray(psel, BF16), jnp.asarray(lmask, F32))


def _ret_kernel(p_ref, cos_ref, sin_ref, dec_ref, decin_ref, g_ref, o_ref, st_ref):
    @pl.when(pl.program_id(1) == 0)
    def _():
        st_ref[...] = jnp.zeros_like(st_ref)

    c = CHUNK
    w = MIX_WIDTH
    ones_bd = _block_ones(w)
    bd_mask = _block_ones(w, F32)
    gain = g_ref[...]
    dec_q = dec_ref[0:c, :]
    dec_k = dec_ref[c:2 * c, :]
    dec_state = dec_ref[2 * c:2 * c + 1, :]
    dec_in = decin_ref[...]

    def chunk(ci, carry):
        r0 = pl.multiple_of(ci * c, c)
        cos_t = cos_ref[pl.ds(r0, c), :]
        sin_t = sin_ref[pl.ds(r0, c), :]
        q = _rope(p_ref[pl.ds(r0, c), 0:w], cos_t, sin_t)
        k = _rope(p_ref[pl.ds(r0, c), w:2 * w], cos_t, sin_t) * (HEAD_DIM ** -0.5)
        v = p_ref[pl.ds(r0, c), 2 * w:3 * w]
        gp = p_ref[pl.ds(r0, c), 3 * w:4 * w]
        vb = v.astype(BF16)
        a = _dot_nt(_stack_heads(q).astype(BF16), k.astype(BF16)) * dec_in
        o = _unstack_heads(_dot(a.astype(BF16), vb), c)
        st = st_ref[...]
        o = o + _dot_nt(q.astype(BF16), st.astype(BF16)) * dec_q
        st_ref[...] = st * dec_state + bd_mask * _dot_tn(vb, (k * dec_k).astype(BF16))
        o_ref[pl.ds(r0, c), :] = _head_rms(o, gain, ones_bd) * _silu(gp)
        return carry

    lax.fori_loop(0, p_ref.shape[0] // c, chunk, 0)


def _ret_consts():
    log_gamma = np.log1p(-np.exp2(-5.0 - np.arange(N_HEADS, dtype=np.float64)))
    t = np.arange(CHUNK, dtype=np.float64)
    lane_h = np.arange(MIX_WIDTH) // HEAD_DIM
    dec_q = np.exp(log_gamma[lane_h][None, :] * (t[:, None] + 1.0))
    dec_k = np.exp(log_gamma[lane_h][None, :] * (CHUNK - 1.0 - t[:, None]))
    dec_state = np.exp(log_gamma[lane_h] * CHUNK)[None, :]
    dec = np.concatenate([dec_q, dec_k, np.broadcast_to(dec_state, (8, MIX_WIDTH))], 0)
    diff = t[:, None] - t[None, :]
    dec_in = np.concatenate([np.where(diff >= 0, np.exp(log_gamma[h] * diff), 0.0) for h in range(N_HEADS)], 0)
    return dec.astype(np.float32), dec_in.astype(np.float32)


def _ret(proj, cos4, sin4, gain, bsz, seq):
    dec, dec_in = _ret_consts()
    tb = min(REC_BLOCK, seq)
    nb = seq // tb
    return pl.pallas_call(
        _ret_kernel,
        grid=(bsz, nb),
        in_specs=[pl.BlockSpec((tb, 4 * MIX_WIDTH), lambda b, i: (b * nb + i, OFF_R // (4 * MIX_WIDTH))),
                  pl.BlockSpec((tb, MIX_WIDTH), lambda b, i: (i, 0)),
                  pl.BlockSpec((tb, MIX_WIDTH), lambda b, i: (i, 0)),
                  pl.BlockSpec(dec.shape, lambda b, i: (0, 0)),
                  pl.BlockSpec(dec_in.shape, lambda b, i: (0, 0)),
                  pl.BlockSpec((1, MIX_WIDTH), lambda b, i: (0, 0))],
        out_specs=pl.BlockSpec((tb, MIX_WIDTH), lambda b, i: (b * nb + i, 0)),
        out_shape=jax.ShapeDtypeStruct((bsz * seq, MIX_WIDTH), F32),
        scratch_shapes=[pltpu.VMEM((MIX_WIDTH, MIX_WIDTH), F32)],
        compiler_params=_cp("parallel", "arbitrary"),
        name="retention",
    )(proj, cos4, sin4, jnp.asarray(dec), jnp.asarray(dec_in), gain.reshape(1, MIX_WIDTH))


def _hgrn_lb_rows(lb):
    lb = lb.astype(F32)
    rows = jnp.stack([jnp.log(lb), jnp.log1p(-lb), 1.0 - lb], 0)
    return jnp.concatenate([rows, jnp.zeros((5, lb.shape[0]), F32)], 0)


def _rope_lane_tables(seq):
    inv = 1.0 / (ROPE_THETA ** (jnp.arange(0, HEAD_DIM, 2, dtype=F32) / HEAD_DIM))
    ang = jnp.arange(seq, dtype=F32)[:, None] * inv[None, :]
    cos, sin = jnp.cos(ang), jnp.sin(ang)
    cos_t = jnp.tile(cos, (1, LANES // (HEAD_DIM // 2)))
    sin_t = jnp.tile(jnp.concatenate([-sin, sin], axis=1), (1, LANES // HEAD_DIM))
    return cos_t, sin_t


def _expand_heads(cols, shape):
    hl = _head_of_lane(shape, 1)
    out = jnp.broadcast_to(cols[-1], shape)
    for h in range(len(cols) - 2, -1, -1):
        out = jnp.where(hl == h, jnp.broadcast_to(cols[h], shape), out)
    return out


def _mlstm_kernel(p_ref, gcol_ref, grow_ref, cw_ref, cb_ref, gbr_ref, gbc_ref, g_ref, o_ref,
                  ct_ref, n_ref, m_ref, hist_ref, cbuf_ref, qk_ref):
    c = CHUNK
    w = MIX_WIDTH
    tb = p_ref.shape[0]

    @pl.when(pl.program_id(1) == 0)
    def _():
        ct_ref[...] = jnp.zeros_like(ct_ref)
        n_ref[...] = jnp.zeros_like(n_ref)
        m_ref[...] = jnp.zeros_like(m_ref)
        hist_ref[...] = jnp.zeros_like(hist_ref)

    cbuf_ref[0:8, :] = hist_ref[...]
    cbuf_ref[8:, :] = p_ref[:, 0:2 * w]
    hist_ref[...] = p_ref[tb - 8:tb, 0:2 * w]
    acc = jnp.broadcast_to(cb_ref[...], (tb, 2 * w))
    for j in range(CONV_W):
        acc = acc + cw_ref[j:j + 1, :] * cbuf_ref[pl.ds(8 - (CONV_W - 1) + j, tb), :]
    qk_ref[...] = _silu(acc)

    ones_bd = _block_ones(w)
    bd_mask = _block_ones(w, F32)
    ti = lax.broadcasted_iota(jnp.int32, (c, c), 0)
    si = lax.broadcasted_iota(jnp.int32, (c, c), 1)
    causal = ti >= si
    tri = causal.astype(BF16)
    tri_t = (ti <= si).astype(BF16)
    gain = g_ref[...]
    ones_ext = jnp.ones((c, LANES), BF16)

    def chunk(ci, carry):
        r0 = pl.multiple_of(ci * c, c)
        q = qk_ref[pl.ds(r0, c), 0:w]
        k = qk_ref[pl.ds(r0, c), w:2 * w] * (HEAD_DIM ** -0.5)
        v = p_ref[pl.ds(r0, c), 2 * w:3 * w]
        op = p_ref[pl.ds(r0, c), 3 * w:4 * w]
        gc = gcol_ref[pl.ds(r0, c), :] + gbr_ref[...]
        gr = grow_ref[ci] + gbc_ref[...]
        b_c = _dot01_l(tri, _log_sigmoid(gc))
        b_r = _dot01_r(_log_sigmoid(gr), tri_t)
        wd, s_inter, em, wk, decay = [], [], [], [], []
        for h in range(N_HEADS):
            bc = b_c[:, N_HEADS + h:N_HEADS + h + 1]
            lic = gc[:, h:h + 1]
            br = b_r[N_HEADS + h:N_HEADS + h + 1, :]
            lir = gr[h:h + 1, :]
            dmat = jnp.where(causal, bc - br + lir, -jnp.inf)
            m_prev = m_ref[h:h + 1, 0:1]
            inter = bc + m_prev
            mrow = jnp.maximum(inter, jnp.max(dmat, axis=1, keepdims=True))
            wd.append(jnp.exp(dmat - mrow))
            s_inter.append(jnp.exp(inter - mrow))
            em.append(jnp.exp(-mrow))
            b_last = br[:, c - 1:c]
            m_new = jnp.maximum(b_last + m_prev, jnp.max(b_last - br + lir, axis=1, keepdims=True))
            wk.append(jnp.exp(b_last - bc + lic - m_new))
            decay.append(jnp.exp(b_last + m_prev - m_new))
            m_ref[h:h + 1, :] = jnp.broadcast_to(m_new, (1, LANES))
        s_inter_l = _expand_heads(s_inter, (c, w))
        em_l = _expand_heads(em, (c, w))
        wk_l = _expand_heads(wk, (c, w))
        decay_l = _expand_heads(decay, (1, w))
        qk = _dot_nt(_stack_heads(q).astype(BF16), k.astype(BF16))
        wmat = jnp.concatenate(wd, axis=0) * qk
        vb = v.astype(BF16)
        r = _dot(wmat.astype(BF16), jnp.concatenate([vb, ones_ext], axis=1))
        num_intra = _unstack_heads(r[:, 0:w], c)
        rs_l = _expand_heads([r[h * c:(h + 1) * c, w:w + 1] for h in range(N_HEADS)], (c, w))
        ct = ct_ref[...]
        nrow = n_ref[0:1, :]
        num = s_inter_l * _dot_nt(q.astype(BF16), ct.astype(BF16)) + num_intra
        den = s_inter_l * _group_sum(q * nrow, ones_bd) + rs_l
        hval = num / jnp.maximum(jnp.abs(den), em_l)
        kw = wk_l * k
        ct_ref[...] = ct * decay_l + bd_mask * _dot_tn(vb, kw.astype(BF16))
        n_ref[0:1, :] = nrow * decay_l + jnp.sum(kw, axis=0, keepdims=True)
        o_ref[pl.ds(r0, c), :] = _head_rms(hval, gain, ones_bd) * _sigmoid(op)
        return carry

    lax.fori_loop(0, tb // c, chunk, 0)


def _mlstm(proj, conv_w, conv_b, gate_b, gain, bsz, seq):
    t = bsz * seq
    w = MIX_WIDTH
    tb = min(REC_BLOCK, seq)
    nb = seq // tb
    ncb = tb // CHUNK
    grow = proj[:, OFF_MG:OFF_MG + 8].reshape(t // CHUNK, CHUNK, 8).transpose(0, 2, 1)
    gb_row = jnp.zeros((1, LANES), F32).at[0, 0:8].set(gate_b.astype(F32))
    gb_col = gate_b.astype(F32).reshape(8, 1)
    return pl.pallas_call(
        _mlstm_kernel,
        grid=(bsz, nb),
        in_specs=[pl.BlockSpec((tb, 4 * w), lambda b, i: (b * nb + i, OFF_M // (4 * w))),
                  pl.BlockSpec((tb, LANES), lambda b, i: (b * nb + i, OFF_MG // LANES)),
                  pl.BlockSpec((ncb, 8, CHUNK), lambda b, i: (b * nb + i, 0, 0)),
                  pl.BlockSpec((CONV_W, 2 * w), lambda b, i: (0, 0)),
                  pl.BlockSpec((1, 2 * w), lambda b, i: (0, 0)),
                  pl.BlockSpec((1, LANES), lambda b, i: (0, 0)),
                  pl.BlockSpec((8, 1), lambda b, i: (0, 0)),
                  pl.BlockSpec((1, w), lambda b, i: (0, 0))],
        out_specs=pl.BlockSpec((tb, w), lambda b, i: (b * nb + i, 0)),
        out_shape=jax.ShapeDtypeStruct((t, w), F32),
        scratch_shapes=[pltpu.VMEM((w, w), F32), pltpu.VMEM((8, w), F32), pltpu.VMEM((8, LANES), F32),
                        pltpu.VMEM((8, 2 * w), F32), pltpu.VMEM((tb + 8, 2 * w), F32),
                        pltpu.VMEM((tb, 2 * w), F32)],
        compiler_params=_cp("parallel", "arbitrary"),
        name="mlstm",
    )(proj, proj, grow, conv_w.astype(F32), conv_b.astype(F32).reshape(1, 2 * w), gb_row, gb_col,
      gain.reshape(1, w))


NSA_TQ = 128
NSA_KC = 512
GW = 2 * HEAD_DIM


def _nsa_prep_kernel(pq_ref, pk_ref, pv_ref, cos_ref, sin_ref, qg_ref, kg_ref,
                     qn_ref, qr_ref, ks_ref, kw_ref, vst_ref, vwt_ref):
    w = MIX_WIDTH
    for src, dst in ((pv_ref[:, 0:GW], vst_ref), (pv_ref[:, GW:2 * GW], vwt_ref)):
        vt = src.T
        tk = dst.shape[4]
        for g in range(NSA_GROUPS):
            rows = vt[g * HEAD_DIM:(g + 1) * HEAD_DIM, :]
            dup = jnp.concatenate([rows, rows], axis=0).astype(BF16)
            for j in range(dst.shape[2]):
                dst[0, g, j] = dup[:, j * tk:(j + 1) * tk]
    ones_bd = _block_ones(w)
    cos_t, sin_t = cos_ref[...], sin_ref[...]
    scale = HEAD_DIM ** -0.5
    qh = _head_rms(pq_ref[...], qg_ref[...], ones_bd)
    qn_ref[...] = (qh * scale).astype(BF16)
    qr_ref[...] = (_rope(qh, cos_t, sin_t) * scale).astype(BF16)
    ks_ref[...] = _rope(_head_rms(pk_ref[:, 0:w], kg_ref[1:2, :], ones_bd), cos_t, sin_t).astype(BF16)
    kw_ref[...] = _rope(_head_rms(pk_ref[:, w:2 * w], kg_ref[2:3, :], ones_bd), cos_t, sin_t).astype(BF16)


def _nsa_prep(proj, cos4, sin4, qnorm_g, knorm_g, bsz, seq):
    t = bsz * seq
    w = MIX_WIDTH
    tm = min(NSA_KC, seq)
    tq = min(NSA_TQ, seq)
    ns = seq // tm
    qg = jnp.tile(qnorm_g.astype(F32), w // HEAD_DIM).reshape(1, w)
    kg = jnp.concatenate([jnp.tile(knorm_g.astype(F32), (1, w // HEAD_DIM)), jnp.zeros((5, w), F32)], axis=0)
    out = jax.ShapeDtypeStruct((t, w), BF16)
    row = pl.BlockSpec((tm, w), lambda i: (i, 0))
    return pl.pallas_call(
        _nsa_prep_kernel,
        grid=(t // tm,),
        in_specs=[pl.BlockSpec((tm, w), lambda i: (i, OFF_NQ // w)),
                  pl.BlockSpec((tm, 2 * w), lambda i: (i, OFF_KD // (2 * w))),
                  pl.BlockSpec((tm, 2 * GW), lambda i: (i, OFF_V // (2 * GW))),
                  pl.BlockSpec((tm, w), lambda i: (i % ns, 0)),
                  pl.BlockSpec((tm, w), lambda i: (i % ns, 0)),
                  pl.BlockSpec((1, w), lambda i: (0, 0)),
                  pl.BlockSpec((8, w), lambda i: (0, 0))],
        out_specs=[row, row, row, row,
                   pl.BlockSpec((1, NSA_GROUPS, 1, GW, tm), lambda i: (i // ns, 0, i % ns, 0, 0)),
                   pl.BlockSpec((1, NSA_GROUPS, tm // tq, GW, tq), lambda i: (i // ns, 0, i % ns, 0, 0))],
        out_shape=[out, out, out, out,
                   jax.ShapeDtypeStruct((bsz, NSA_GROUPS, seq // tm, GW, tm), BF16),
                   jax.ShapeDtypeStruct((bsz, NSA_GROUPS, seq // tq, GW, tq), BF16)],
        compiler_params=_cp("parallel"),
        name="nsa_prep",
    )(proj, proj, proj, cos4, sin4, qg, kg)


def _nsa_cmp_kernel(xr_ref, pe_ref, w0_ref, w1_ref, kg_ref, ovt_ref, qn_ref, ocmp_ref, sel_ref,
                    kc_ref, vc_ref, v_ref, *, n_top):
    tq = qn_ref.shape[0]
    nr = xr_ref.shape[0]
    nsel = sel_ref.shape[2]
    w = MIX_WIDTH

    @pl.when(pl.program_id(1) == 0)
    def _():
        xr = xr_ref[...]
        y0 = _dot((xr + pe_ref[0]).astype(BF16), w0_ref[...])
        y1 = _dot((xr + pe_ref[1]).astype(BF16), w1_ref[...])
        kv = y0 + pltpu.roll(y1, nr - 1, 0)
        kc_ref[...] = _head_rms(kv[:, 0:w], kg_ref[...], _block_ones(w)).astype(BF16)
        vc_ref[...] = kv[:, w:2 * w].astype(BF16)

    pos0 = pl.program_id(1) * tq
    hl = _head_of_lane((tq, GW), 1)
    pos_r = pos0 + lax.broadcasted_iota(jnp.int32, (tq, nr), 0)
    valid = lax.broadcasted_iota(jnp.int32, (tq, nr), 1) * CMP_STRIDE + (CMP_LEN - 1) <= pos_r
    pos_c = pos0 + lax.broadcasted_iota(jnp.int32, (nr, tq), 1)
    valid_t = lax.broadcasted_iota(jnp.int32, (nr, tq), 0) * CMP_STRIDE + (CMP_LEN - 1) <= pos_c
    jrow = lax.broadcasted_iota(jnp.int32, (nsel, tq), 0)
    cur = (pos0 + lax.broadcasted_iota(jnp.int32, (nsel, tq), 1)) // SEL_BLOCK
    forced = (jrow == 0) | (jrow == cur) | (jrow == cur - 1)
    ovt = ovt_ref[...]

    for g in range(NSA_GROUPS):
        qg = qn_ref[:, g * GW:(g + 1) * GW]
        kg = kc_ref[:, g * GW:(g + 1) * GW]
        vg = vc_ref[:, g * GW:(g + 1) * GW]
        o_g = jnp.zeros((tq, GW), F32)
        pt_sum = jnp.zeros((nr, tq), F32)
        for hh in range(2):
            qm = jnp.where(hl == hh, qg, jnp.zeros_like(qg))
            s = jnp.where(valid, _dot_nt(qm, kg), NEG_INF)
            e = jnp.exp(s - jnp.max(s, axis=1, keepdims=True))
            p = jnp.where(valid, e / jnp.sum(e, axis=1, keepdims=True), 0.0)
            o_g = jnp.where(hl == hh, _dot(p.astype(BF16), vg), o_g)
            st = jnp.where(valid_t, _dot_nt(kg, qm), NEG_INF)
            et = jnp.exp(st - jnp.max(st, axis=0, keepdims=True))
            pt_sum = pt_sum + jnp.where(valid_t, et / jnp.sum(et, axis=0, keepdims=True), 0.0)
        ocmp_ref[:, g * GW:(g + 1) * GW] = o_g
        p_hi = pt_sum.astype(BF16)
        p_lo = (pt_sum - p_hi.astype(F32)).astype(BF16)
        imp = _dot(ovt, p_hi) + _dot(ovt, p_lo)
        val = jnp.where(jrow <= cur, imp + FORCE_BONUS * forced.astype(F32), NEG_INF)
        v_ref[...] = val

        def rank(jp, cnt):
            row = v_ref[pl.ds(jp, 1), :]
            tie = jnp.where(jrow > jp, 1.0, 0.0)
            return cnt + jnp.where(row > val, 1.0, jnp.where(row == val, tie, 0.0))

        cnt = lax.fori_loop(0, nsel, rank, jnp.zeros((nsel, tq), F32))
        sel_ref[0, g] = ((cnt < n_top) & (jrow <= cur)).astype(F32)


def _nsa_cmp_weights(cmp_pe, cmp_w):
    half = CMP_LEN // 2
    wl = cmp_w.astype(F32).reshape(2, 2, half, HEAD_DIM, HEAD_DIM)
    eye2 = jnp.eye(2, dtype=F32)
    w2 = jnp.einsum('kardz,kK,gG,h->arkgdKGhz', wl, eye2, eye2, jnp.ones((2,), F32))
    w2 = w2.reshape(2, half * 4 * HEAD_DIM, 8 * HEAD_DIM)
    pl_ = cmp_pe.astype(F32).reshape(2, 2, half, HEAD_DIM)
    pe2 = jnp.broadcast_to(pl_.transpose(1, 2, 0, 3)[:, :, :, None, :], (2, half, 2, 2, HEAD_DIM))
    return w2.astype(BF16), pe2.reshape(2, 1, half * 4 * HEAD_DIM)


def _nsa_cmp(kcvc, qn, cmp_pe, cmp_w, knorm0, bsz, seq, tq=512):
    t = bsz * seq
    w = MIX_WIDTH
    tq = min(tq, seq)
    nq = seq // tq
    nr = seq // CMP_STRIDE
    nsel = seq // SEL_BLOCK
    n_top = min(SEL_TOPK, nsel)
    w2, pe2 = _nsa_cmp_weights(cmp_pe, cmp_w)
    xr = kcvc.reshape(t // CMP_STRIDE, CMP_STRIDE * w)
    kg = jnp.tile(knorm0.astype(F32), w // HEAD_DIM).reshape(1, w)
    n_i = np.arange(nr)[:, None] * CMP_STRIDE
    j_i = np.arange(nsel)[None, :] * SEL_BLOCK
    ov = ((n_i < j_i + SEL_BLOCK) & (n_i + CMP_LEN > j_i)).astype(np.float32)
    ov[nr - 1, :] = 0.0
    kin = CMP_STRIDE * w
    return pl.pallas_call(
        functools.partial(_nsa_cmp_kernel, n_top=n_top),
        grid=(bsz, nq),
        in_specs=[pl.BlockSpec((nr, kin), lambda b, i: (b, 0)),
                  pl.BlockSpec((2, 1, kin), lambda b, i: (0, 0, 0)),
                  pl.BlockSpec((None, kin, 2 * w), lambda b, i: (0, 0, 0)),
                  pl.BlockSpec((None, kin, 2 * w), lambda b, i: (1, 0, 0)),
                  pl.BlockSpec((1, w), lambda b, i: (0, 0)),
                  pl.BlockSpec((nsel, nr), lambda b, i: (0, 0)),
                  pl.BlockSpec((tq, w), lambda b, i: (b * nq + i, 0))],
        out_specs=[pl.BlockSpec((tq, w), lambda b, i: (b * nq + i, 0)),
                   pl.BlockSpec((1, NSA_GROUPS, nsel, tq), lambda b, i: (b, 0, 0, i))],
        out_shape=[jax.ShapeDtypeStruct((t, w), F32),
                   jax.ShapeDtypeStruct((bsz, NSA_GROUPS, nsel, seq), F32)],
        scratch_shapes=[pltpu.VMEM((nr, w), BF16), pltpu.VMEM((nr, w), BF16), pltpu.VMEM((nsel, tq), F32)],
        compiler_params=_cp("parallel", "arbitrary"),
        name="nsa_cmp",
    )(xr, pe2, w2, w2, kg, jnp.asarray(ov.T, BF16), qn)


def _nsa_attn_kernel(qr_ref, ks_ref, kw_ref, vs_ref, vw_ref, sel_ref, ocmp_ref, gate_ref, o_ref, *, kc, wt):
    tq = qr_ref.shape[0]
    i = pl.program_id(1)
    hl = _head_of_lane((tq, GW), 1)
    nbk = kc // SEL_BLOCK
    groups = range(NSA_GROUPS)

    def stacked_q(g):
        q = qr_ref[:, g * GW:(g + 1) * GW]
        return jnp.concatenate([jnp.where(hl == 0, q, jnp.zeros_like(q)), jnp.where(hl == 1, q, jnp.zeros_like(q))],
                               axis=0)

    qs_all = [stacked_q(g) for g in groups]

    def lane_qpos(rows):
        return i * tq + lax.broadcasted_iota(jnp.int32, (rows, 2 * tq), 1) % tq

    def finish(acc, l):
        ot = (acc / l).T
        return jnp.where(hl == 0, ot[0:tq, :], ot[tq:2 * tq, :])

    qpos_s = lane_qpos(kc)
    krow_s = lax.broadcasted_iota(jnp.int32, (kc, 2 * tq), 0)

    def sel_step(g, c, carry, diagonal):
        m, l, acc = carry
        k0 = pl.multiple_of(c * kc, kc)
        st = _dot_nt(ks_ref[pl.ds(k0, kc), g * GW:(g + 1) * GW], qs_all[g])
        srows = sel_ref[0, g, pl.ds(pl.multiple_of(c * nbk, nbk), nbk), :]
        srows = jnp.concatenate([srows, srows], axis=1)
        smask = jnp.concatenate([jnp.broadcast_to(srows[r:r + 1, :], (SEL_BLOCK, 2 * tq)) for r in range(nbk)],
                                axis=0)
        msk = smask > 0.5
        if diagonal:
            msk = msk & (k0 + krow_s <= qpos_s)
        st = jnp.where(msk, st, NEG_INF)
        m_new = jnp.maximum(m, jnp.max(st, axis=0, keepdims=True))
        p = jnp.exp(st - m_new)
        alpha = jnp.exp(m - m_new)
        l = l * alpha + jnp.sum(p, axis=0, keepdims=True)
        acc = acc * alpha + _dot(vs_ref[0, g, c], p.astype(BF16))
        return m_new, l, acc

    def sel_body(c, carries, diagonal):
        return tuple(sel_step(g, c, carries[g], diagonal) for g in groups)

    init = (jnp.full((1, 2 * tq), NEG_INF, F32), jnp.zeros((1, 2 * tq), F32), jnp.zeros((GW, 2 * tq), F32))
    n_before = (i * tq) // kc
    carries = lax.fori_loop(0, n_before, functools.partial(sel_body, diagonal=False), (init,) * NSA_GROUPS)
    carries = sel_body(n_before, carries, True)

    j0 = jnp.maximum(i - (wt - 1), 0)
    k0 = pl.multiple_of(j0 * tq, tq)
    span = wt * tq
    kpos = k0 + lax.broadcasted_iota(jnp.int32, (span, 2 * tq), 0)
    qpos_w = lane_qpos(span)
    wmask = (kpos <= qpos_w) & (kpos > qpos_w - WINDOW)
    gb = _sigmoid(gate_ref[...])
    for g in groups:
        _, l_s, acc_s = carries[g]
        o_sel = finish(acc_s, l_s)
        st = jnp.where(wmask, _dot_nt(kw_ref[pl.ds(k0, span), g * GW:(g + 1) * GW], qs_all[g]), NEG_INF)
        p = jnp.exp(st - jnp.max(st, axis=0, keepdims=True))
        vt = jnp.concatenate([vw_ref[0, g, j0 + r] for r in range(wt)], axis=1)
        o_win = finish(_dot(vt, p.astype(BF16)), jnp.sum(p, axis=0, keepdims=True))

        def gate(branch):
            cols = [gb[:, (2 * g + hh) * 3 + branch:(2 * g + hh) * 3 + branch + 1] for hh in range(2)]
            return _expand_heads(cols, (tq, GW))

        o_ref[:, g * GW:(g + 1) * GW] = (gate(0) * ocmp_ref[:, g * GW:(g + 1) * GW]
                                         + gate(1) * o_sel + gate(2) * o_win)


def _nsa_attn(proj, qr, ks, kw, vst, vwt, sel, o_cmp, bsz, seq):
    t = bsz * seq
    w = MIX_WIDTH
    tq = min(NSA_TQ, seq)
    nq = seq // tq
    nsel = seq // SEL_BLOCK
    kc = min(NSA_KC, seq)
    wt = min(WINDOW // tq + 1, nq)
    kspec = pl.BlockSpec((seq, w), lambda b, i: (b, 0))
    return pl.pallas_call(
        functools.partial(_nsa_attn_kernel, kc=kc, wt=wt),
        grid=(bsz, nq),
        in_specs=[pl.BlockSpec((tq, w), lambda b, i: (b * nq + i, 0)),
                  kspec, kspec,
                  pl.BlockSpec((1, NSA_GROUPS, seq // kc, GW, kc), lambda b, i: (b, 0, 0, 0, 0)),
                  pl.BlockSpec((1, NSA_GROUPS, nq, GW, tq), lambda b, i: (b, 0, 0, 0, 0)),
                  pl.BlockSpec((1, NSA_GROUPS, nsel, tq), lambda b, i: (b, 0, 0, i)),
                  pl.BlockSpec((tq, w), lambda b, i: (b * nq + i, 0)),
                  pl.BlockSpec((tq, LANES), lambda b, i: (b * nq + i, OFF_NG // LANES))],
        out_specs=pl.BlockSpec((tq, w), lambda b, i: (b * nq + i, 0)),
        out_shape=jax.ShapeDtypeStruct((t, w), F32),
        compiler_params=_cp("parallel", "arbitrary"),
        name="nsa_attn",
    )(qr, ks, kw, vst, vwt, sel, o_cmp, proj)


def _nsa(proj, kcvc, cos4, sin4, qnorm_g, knorm_g, cmp_pe, cmp_w, bsz, seq):
    qn, qr, ks, kw, vst, vwt = _nsa_prep(proj, cos4, sin4, qnorm_g, knorm_g, bsz, seq)
    o_cmp, sel = _nsa_cmp(kcvc, qn, cmp_pe, cmp_w, knorm_g[0], bsz, seq)
    return _nsa_attn(proj, qr, ks, kw, vst, vwt, sel, o_cmp, bsz, seq)


PEER_TT = 128
PEER_CT = 8
HALF_D = 512


SUBLANES = 8
CODE_BITS = 127
FAR_BELOW = -3.0e38


def _with_code(x, code):
    bits = lax.bitcast_convert_type(x, jnp.int32)
    return lax.bitcast_convert_type((bits & ~CODE_BITS) | code, F32)


def _split_code(x):
    bits = lax.bitcast_convert_type(x, jnp.int32)
    return lax.bitcast_convert_type(bits & ~CODE_BITS, F32), bits & CODE_BITS


def _sort16_desc(xs):
    xs = list(xs)
    n = len(xs)
    k = 2
    while k <= n:
        j = k // 2
        while j >= 1:
            for i in range(n):
                l = i ^ j
                if l > i:
                    hi, lo = jnp.maximum(xs[i], xs[l]), jnp.minimum(xs[i], xs[l])
                    xs[i], xs[l] = (hi, lo) if (i & k) == 0 else (lo, hi)
            j //= 2
        k *= 2
    return xs


def _merge16_desc(xs):
    xs = list(xs)
    j = len(xs) // 2
    while j >= 1:
        for i in range(len(xs)):
            l = i ^ j
            if l > i:
                xs[i], xs[l] = jnp.maximum(xs[i], xs[l]), jnp.minimum(xs[i], xs[l])
        j //= 2
    return xs


def _top16_columns(x):
    n = PEER_TOPK
    xs = _sort16_desc([x[SUBLANES * j:SUBLANES * (j + 1), :] for j in range(n)])
    shift = SUBLANES // 2
    while shift >= 1:
        rolled = [pltpu.roll(a, shift, 0) for a in xs]
        xs = _merge16_desc([jnp.maximum(xs[i], rolled[n - 1 - i]) for i in range(n)])
        shift //= 2
    return xs


_PEER_CAND_TILES = ((0, 0, 8), (0, 1, 8), (1, 0, 8), (2, 0, 5), (3, 0, 4), (4, 0, 3), (5, 0, 2), (6, 0, 2), (7, 0, 2))


ROUTE_HEADS_PER_STEP = 4


def _route_head(q_ref, key_ref, hh):
    tt = q_ref.shape[0]
    nk = PEER_NKEYS
    n = PEER_TOPK
    row = lax.broadcasted_iota(jnp.int32, (nk, tt), 0)
    sub = lax.broadcasted_iota(jnp.int32, (SUBLANES, tt), 0)
    vals, ids = [], []
    for p in range(2):
        c0 = (2 * hh + p) * PEER_KDIM
        st = _dot_nt(key_ref[hh, p], q_ref[:, c0:c0 + PEER_KDIM])
        top = [_split_code(a) for a in _top16_columns(_with_code(st, (nk - 1) - row))]
        vals.append([v for v, _ in top])
        ids.append([(nk - 1) - c for _, c in top])
    (v1, v2), (i1, i2) = vals, ids

    def stack(xs, lo):
        out = xs[lo]
        for s in range(1, SUBLANES):
            out = jnp.where(sub == s, xs[lo + s], out)
        return out

    v2t, i2t = (stack(v2, 0), stack(v2, SUBLANES)), (stack(i2, 0), stack(i2, SUBLANES))
    cand, cexp = [], []
    for a, tile, nvalid in _PEER_CAND_TILES:
        v = v1[a] + v2t[tile]
        cand.append(v if nvalid == SUBLANES else jnp.where(sub < nvalid, v, FAR_BELOW))
        cexp.append(i1[a] * nk + i2t[tile])
    cand.append(stack(v1, SUBLANES) + v2[0])
    cexp.append(stack(i1, SUBLANES) * nk + i2[0])
    n_tiles = len(cand)
    slot_code = [(nk - 1) - (c * SUBLANES + sub) for c in range(n_tiles)]
    coded = [_with_code(v, sc) for v, sc in zip(cand, slot_code)]
    coded += [jnp.full((SUBLANES, tt), FAR_BELOW, F32)] * (n - n_tiles)
    top = [_split_code(a) for a in _top16_columns(jnp.concatenate(coded, axis=0))]
    call = jnp.concatenate(cexp, axis=0)
    slot = (nk - 1) - lax.broadcasted_iota(jnp.int32, call.shape, 0)
    ex = [jnp.exp(v - top[0][0]) for v, _ in top]
    tot = ex[0]
    for k in range(1, n):
        tot = tot + ex[k]
    krow = lax.broadcasted_iota(jnp.int32, (n, tt), 0)
    e_tile = jnp.zeros((n, tt), F32)
    g_tile = jnp.zeros((n, tt), F32)
    for k in range(n):
        hit = slot == jnp.concatenate([top[k][1]] * n_tiles, axis=0)
        e_k = jnp.sum(jnp.where(hit, call, 0), axis=0, keepdims=True)
        e_tile = jnp.where(krow == k, e_k.astype(F32), e_tile)
        g_tile = jnp.where(krow == k, (ex[k] / tot)[0:1, :], g_tile)
    return e_tile, g_tile


def _peer_route_kernel(q_ref, key_ref, e_ref, g_ref, e_scr, g_scr):
    hps = key_ref.shape[0]
    tiles = [_route_head(q_ref, key_ref, hh) for hh in range(hps)]
    rows = hps * PEER_TOPK
    r0 = pl.multiple_of(pl.program_id(1) * rows, rows)
    e_scr[pl.ds(r0, rows), :] = jnp.concatenate([e for e, _ in tiles], axis=0)
    g_scr[pl.ds(r0, rows), :] = jnp.concatenate([g for _, g in tiles], axis=0)

    @pl.when(pl.program_id(1) == pl.num_programs(1) - 1)
    def _():
        e_ref[...] = e_scr[...].T.astype(jnp.int32)
        g_ref[...] = g_scr[...].T


def _peer_route(qp, keys):
    t = qp.shape[0]
    tt = min(PEER_TT, t)
    ne = PEER_HEADS * PEER_TOPK
    hps = ROUTE_HEADS_PER_STEP
    return pl.pallas_call(
        _peer_route_kernel,
        grid=(t // tt, PEER_HEADS // hps),
        in_specs=[pl.BlockSpec((tt, hps * 2 * PEER_KDIM), lambda i, h: (i, h)),
                  pl.BlockSpec((hps, 2, PEER_NKEYS, PEER_KDIM), lambda i, h: (h, 0, 0, 0))],
        out_specs=[pl.BlockSpec((tt, ne), lambda i, h: (i, 0)),
                   pl.BlockSpec((tt, ne), lambda i, h: (i, 0))],
        out_shape=[jax.ShapeDtypeStruct((t, ne), jnp.int32),
                   jax.ShapeDtypeStruct((t, ne), F32)],
        scratch_shapes=[pltpu.VMEM((ne, tt), F32), pltpu.VMEM((ne, tt), F32)],
        compiler_params=_cp("parallel", "arbitrary"),
        name="peer_route",
    )(qp, keys)


def _pack_tables_kernel(u_ref, v_ref, o_ref):
    def pack(x):
        lo = lax.bitcast_convert_type(x[:, 0:HALF_D].astype(BF16).astype(F32), jnp.int32)
        hi = lax.bitcast_convert_type(x[:, HALF_D:2 * HALF_D].astype(BF16).astype(F32), jnp.int32)
        return lax.shift_right_logical(lo, 16) | (hi & jnp.int32(-65536))

    o_ref[:, 0:HALF_D] = pack(u_ref[...])
    o_ref[:, HALF_D:2 * HALF_D] = pack(v_ref[...])


def _pack_tables(u_tabs, v_tabs, layer, tr=512):
    _, e, d = u_tabs.shape
    assert d == 2 * HALF_D
    spec_in = pl.BlockSpec((None, tr, d), lambda i: (layer, i, 0))
    spec = pl.BlockSpec((tr, d), lambda i: (i, 0))
    return pl.pallas_call(
        _pack_tables_kernel,
        grid=(e // tr,),
        in_specs=[spec_in, spec_in],
        out_specs=spec,
        out_shape=jax.ShapeDtypeStruct((e, d), jnp.int32),
        compiler_params=_cp("parallel"),
        name="peer_pack",
    )(u_tabs, v_tabs)


def _unpack_rows(wd):
    lo = lax.bitcast_convert_type(lax.shift_left(wd, 16), F32)
    hi = lax.bitcast_convert_type(lax.bitwise_and(wd, jnp.int32(-65536)), F32)
    return lo, hi


SC_WINDOWS = (16,)


def _sc_gather(table, idx, window):
    from jax.experimental.pallas import tpu_sc as plsc
    n = idx.shape[0]
    width = table.shape[1]
    mesh = plsc.VectorSubcoreMesh(core_axis_name="core", subcore_axis_name="subcore")

    @functools.partial(pl.kernel, out_type=jax.ShapeDtypeStruct((n, width), table.dtype), mesh=mesh)
    def gather(tab_hbm, idx_hbm, out_hbm):
        def body(idx_vmem, out_vmem):
            pltpu.sync_copy(tab_hbm.at[idx_vmem.at[0, pl.ds(0, window)]], out_vmem)

        pltpu.emit_pipeline(
            body,
            grid=(n // window,),
            in_specs=[pl.BlockSpec((1, LANES), lambda i: (0, i))],
            out_specs=[pl.BlockSpec((window, width), lambda i: (i, 0))],
            core_axis_name=("core", "subcore"),
            dimension_semantics=(pltpu.PARALLEL,),
            trace_scopes=False,
        )(idx_hbm, out_hbm)

    idx_pad = jnp.pad(idx.reshape(n // window, window), ((0, 0), (0, LANES - window)))
    return gather(table, idx_pad.reshape(1, (n // window) * LANES))


def _peer_combine_kernel(x_ref, g2_ref, rows_a_ref, rows_b_ref, gate_ref, o_ref):
    ne = PEER_HEADS * PEER_TOPK
    x = x_ref[...]
    ct = x.shape[0]
    xn = x * lax.rsqrt(jnp.mean(x * x, axis=-1, keepdims=True) + NORM_EPS) * g2_ref[...]
    gate_t = jnp.concatenate([gate_ref[...]] * (ne // ct), axis=0).T
    for jj in range(ct):
        rows_ref, j = (rows_a_ref, jj) if jj < ct // 2 else (rows_b_ref, jj - ct // 2)
        u_lo, u_hi = _unpack_rows(rows_ref[j * ne:(j + 1) * ne, 0:HALF_D])
        xr = xn[jj:jj + 1, :]
        h = jnp.sum(u_lo * xr[:, 0:HALF_D] + u_hi * xr[:, HALF_D:2 * HALF_D], axis=1, keepdims=True)
        act = 0.5 * h * (1.0 + lax.erf(h * (2.0 ** -0.5)))
        wgt = gate_t[:, jj:jj + 1] * act
        v_lo, v_hi = _unpack_rows(rows_ref[j * ne:(j + 1) * ne, HALF_D:2 * HALF_D])
        o_ref[jj:jj + 1, 0:HALF_D] = x[jj:jj + 1, 0:HALF_D] + jnp.sum(wgt * v_lo, axis=0, keepdims=True)
        o_ref[jj:jj + 1, HALF_D:2 * HALF_D] = (x[jj:jj + 1, HALF_D:2 * HALF_D]
                                               + jnp.sum(wgt * v_hi, axis=0, keepdims=True))


def _peer_combine(x, g2, rows, gates, first_token):
    t, d = x.shape
    ne = PEER_HEADS * PEER_TOPK
    ct = PEER_CT
    steps = rows.shape[0] // (ct * ne)
    off = first_token // ct
    return pl.pallas_call(
        _peer_combine_kernel,
        grid=(steps,),
        in_specs=[pl.BlockSpec((ct, d), lambda i: (off + i, 0)),
                  pl.BlockSpec((1, d), lambda i: (0, 0)),
                  pl.BlockSpec((ct * ne // 2, d), lambda i: (2 * i, 0)),
                  pl.BlockSpec((ct * ne // 2, d), lambda i: (2 * i + 1, 0)),
                  pl.BlockSpec((ct, ne), lambda i: (off + i, 0))],
        out_specs=pl.BlockSpec((ct, d), lambda i: (off + i, 0)),
        out_shape=jax.ShapeDtypeStruct((t, d), F32),
        input_output_aliases={0: 0},
        compiler_params=_cp("parallel"),
        name="peer_combine",
    )(x, g2.reshape(1, d), rows, rows, gates)


PEER_TOKENS_PER_GATHER = 2048


def _peer_route_stage(x, g2, wq_b, keys_b):
    t = x.shape[0]
    ne = PEER_HEADS * PEER_TOPK
    qp = _norm_matmul(x, g2, wq_b, out_dtype=BF16)
    e_tok, g_tok = _peer_route(qp, keys_b)
    return e_tok.reshape(t * ne), g_tok


def _peer_gather_stage(table, idx, t, gather_fn):
    ne = PEER_HEADS * PEER_TOPK
    tc = min(PEER_TOKENS_PER_GATHER, t)
    return [gather_fn(table, idx[c * tc * ne:(c + 1) * tc * ne], SC_WINDOWS[c % len(SC_WINDOWS)])
            for c in range(t // tc)]


def _peer_combine_stage(x, g2, rows_list, gates):
    tc = x.shape[0] // len(rows_list)
    for c, rows in enumerate(rows_list):
        x = _peer_combine(x, g2, rows, gates, c * tc)
    return x


def _peer(x, g2, wq, keys, u_tabs, v_tabs, layer, gather_fn):
    idx, gates = _peer_route_stage(x, g2, wq.astype(BF16), keys.astype(BF16))
    table = _pack_tables(u_tabs, v_tabs, layer)
    rows_list = _peer_gather_stage(table, idx, x.shape[0], gather_fn)
    return _peer_combine_stage(x, g2, rows_list, gates)


_IN_WIDTHS = (256, 256, 256, 256, 256, 256, 256, 4, 4, 256, 256, 128, 128, 128, 128, 128, 128, 12,
              256, 256, 256, 256)


def _dup_groups(wcols):
    g0, g1 = wcols[:, :HEAD_DIM], wcols[:, HEAD_DIM:]
    return jnp.concatenate([g0, g0, g1, g1], axis=1)


def _layout_w_in(w_in):
    offs = np.cumsum((0,) + _IN_WIDTHS)
    cols = [w_in[:, offs[i]:offs[i + 1]] for i in range(len(_IN_WIDTHS))]
    (hq, hf, hi, hg, mq, mk, mv, mi, mf, mo, nq, nkc, nvc, nks, nvs, nkw, nvw, ng, rq, rk, rv, rg) = cols
    d = w_in.shape[0]
    pad = lambda c, n: jnp.concatenate([c, jnp.zeros((d, n - c.shape[1]), w_in.dtype)], axis=1)
    main = jnp.concatenate([hq, hf, hi, hg, mq, mk, mv, mo, rq, rk, rv, rg,
                            _dup_groups(nks), _dup_groups(nkw), nq, nvs, nvw,
                            pad(jnp.concatenate([mi, mf], axis=1), LANES), pad(ng, LANES)], axis=1)
    assert main.shape[1] == N_MAIN
    kcvc = jnp.concatenate([nkc, nvc], axis=1)
    return main.astype(BF16), kcvc.astype(BF16)


def kernel(x, norm1_g, w_in, hgrn_lb, hgrn_onorm_g, mlstm_conv_w, mlstm_conv_b, mlstm_gate_b, mlstm_onorm_g, nsa_qnorm_g, nsa_knorm_g, nsa_cmp_pe, nsa_cmp_w, ret_onorm_g, w_up, w_gate, w_out, norm2_g, peer_wq, peer_keys, peer_u, peer_v):
    bsz, seq, d = x.shape
    t = bsz * seq
    depth = w_in.shape[0]
    cos_t, sin_t = _rope_lane_tables(seq)
    cos4, sin4 = jnp.tile(cos_t, (1, 2)), jnp.tile(sin_t, (1, 2))
    lb_cum = jnp.cumsum(jax.nn.softmax(hgrn_lb.astype(F32), axis=0), axis=0)
    lb_all = lb_cum - lb_cum[0:1]
    weights = []
    for l in range(depth):
        w_main, w_kcvc = _layout_w_in(w_in[l])
        weights.append(dict(
            main=w_main, kcvc=w_kcvc, gate=w_gate[l].astype(BF16), up=w_up[l].astype(BF16),
            out=w_out[l].astype(BF16), lb=_hgrn_lb_rows(lb_all[l]), wq=peer_wq[l].astype(BF16),
            keys=peer_keys[l].astype(BF16), table=_pack_tables(peer_u, peer_v, l)))

    def mixer_steps(xh, l, nb):
        wl = weights[l]
        st = {}

        def proj(dep):
            st["proj"] = _norm_matmul(xh, _after(norm1_g[l], dep), wl["main"])
            return st["proj"]

        def gates(dep):
            st["gates"] = _norm_matmul(xh, _after(norm1_g[l], dep), wl["gate"], act="sigmoid", out_dtype=BF16)
            return st["gates"]

        def hgrn(dep):
            st["oh"] = _hgrn(st["proj"], wl["lb"], _after(hgrn_onorm_g[l], dep), nb, seq)
            return st["oh"]

        def mlstm(dep):
            st["om"] = _mlstm(st["proj"], mlstm_conv_w[l], mlstm_conv_b[l], mlstm_gate_b[l],
                              _after(mlstm_onorm_g[l], dep), nb, seq)
            return st["om"]

        def ret(dep):
            st["or"] = _ret(st["proj"], cos4, sin4, _after(ret_onorm_g[l], dep), nb, seq)
            return st["or"]

        def nsa_front(dep):
            kcvc = _norm_matmul(xh, _after(norm1_g[l], dep), wl["kcvc"])
            qn, qr, ks, kw, vst, vwt = _nsa_prep(st["proj"], cos4, sin4, nsa_qnorm_g[l], nsa_knorm_g[l], nb, seq)
            o_cmp, sel = _nsa_cmp(kcvc, qn, nsa_cmp_pe[l], nsa_cmp_w[l], nsa_knorm_g[l][0], nb, seq)
            st["nsa"] = (qr, ks, kw, vst, vwt, sel, o_cmp)
            return o_cmp

        def nsa_attn(dep):
            del dep
            st["on"] = _nsa_attn(st["proj"], *st["nsa"], nb, seq)
            return st["on"]

        def merge(dep):
            del dep
            st["xm"] = _merge(xh, st["gates"], (st["oh"], st["om"], st["on"], st["or"]), wl["up"], wl["out"])
            return st["xm"]

        def route(dep):
            st["idx"], st["pgates"] = _peer_route_stage(st["xm"], _after(norm2_g[l], dep), wl["wq"], wl["keys"])
            return st["pgates"]

        return [proj, gates, hgrn, mlstm, ret, nsa_front, nsa_attn, merge, route], st

    combine_slots = (0, 8, 8, 8, 8, 8, 8, 8)

    def combine_steps(l, xm, rows_list, pgates):
        box = {"x": xm}
        tc = xm.shape[0] // len(rows_list)

        def make(c):
            def step(dep):
                box["x"] = _peer_combine(box["x"], _after(norm2_g[l], dep), rows_list[c], pgates, c * tc)
                return box["x"]
            return step

        return [make(c) for c in range(len(rows_list))], box

    n_groups = next(n for n in (4, 2, 1) if bsz % n == 0)
    nb = bsz // n_groups
    xs = [x[g * nb:(g + 1) * nb].reshape(nb * seq, d) for g in range(n_groups)]
    dep = None
    lag = min(2, n_groups - 1)
    pending = []
    for l in range(depth):
        for g in range(n_groups):
            msteps, st = mixer_steps(xs[g], l, nb)
            due = pending.pop(0) if len(pending) == lag and lag > 0 else None
            csteps = due[1] if due is not None else []
            ci = 0
            for si, mstep in enumerate(msteps):
                dep = mstep(dep)
                while ci < len(csteps) and (ci >= len(combine_slots) or combine_slots[ci] <= si):
                    dep = csteps[ci](dep)
                    ci += 1
            for cstep in csteps[ci:]:
                dep = cstep(dep)
            if due is not None:
                xs[due[0]] = due[2]["x"]
            rows_list = _peer_gather_stage(weights[l]["table"], st["idx"], nb * seq, _sc_gather)
            csteps, box = combine_steps(l, st["xm"], rows_list, st["pgates"])
            pending.append((g, csteps, box))
            if lag == 0:
                for cstep in pending.pop(0)[1]:
                    dep = cstep(dep)
                xs[g] = box["x"]
    for pg, csteps, box in pending:
        for cstep in csteps:
            dep = cstep(dep)
        xs[pg] = box["x"]
    return jnp.concatenate(xs, axis=0).reshape(bsz, seq, d)
```

```python
import functools
import math

import numpy as np
import jax
import jax.numpy as jnp
from jax import lax
from jax.experimental import pallas as pl
from jax.experimental.pallas import tpu as pltpu

F32 = jnp.float32
BF16 = jnp.bfloat16

HEAD_DIM = 64
N_HEADS = 4
MIX_WIDTH = N_HEADS * HEAD_DIM
CHUNK = 64
NORM_EPS = 1e-6
NEG_INF = -1e30
ROPE_THETA = 10000.0
CONV_W = 4
NSA_GROUPS = 2
CMP_LEN = 32
CMP_STRIDE = 16
SEL_BLOCK = 64
SEL_TOPK = 16
WINDOW = 512
FORCE_BONUS = 1e3
PEER_HEADS = 8
PEER_NKEYS = 128
PEER_TOPK = 16
PEER_KDIM = 128

LANES = 128
VMEM_LIMIT = 48 * 1024 * 1024

OFF_H, OFF_M, OFF_R, OFF_KD, OFF_NQ, OFF_V, OFF_MG, OFF_NG = 0, 1024, 2048, 3072, 3584, 3840, 4096, 4224
N_MAIN = 4352


def _cp(*sem):
    return pltpu.CompilerParams(dimension_semantics=sem, vmem_limit_bytes=VMEM_LIMIT)


def _dot(a, b):
    return jnp.dot(a, b, preferred_element_type=F32)


def _dot_nt(a, b):
    return lax.dot_general(a, b, (((1,), (1,)), ((), ())), preferred_element_type=F32)


def _dot_tn(a, b):
    return lax.dot_general(a, b, (((0,), (0,)), ((), ())), preferred_element_type=F32)


def _split3(x):
    hi = x.astype(BF16)
    r1 = x - hi.astype(F32)
    mid = r1.astype(BF16)
    lo = (r1 - mid.astype(F32)).astype(BF16)
    return hi, mid, lo


def _dot01_l(m01, x):
    hi, mid, lo = _split3(x)
    return _dot(m01, hi) + _dot(m01, mid) + _dot(m01, lo)


def _dot01_r(x, m01):
    hi, mid, lo = _split3(x)
    return _dot(hi, m01) + _dot(mid, m01) + _dot(lo, m01)


def _head_of_lane(shape, axis):
    return lax.broadcasted_iota(jnp.int32, shape, axis) // HEAD_DIM


def _block_ones(n, dtype=BF16):
    r = lax.broadcasted_iota(jnp.int32, (n, n), 0) // HEAD_DIM
    c = lax.broadcasted_iota(jnp.int32, (n, n), 1) // HEAD_DIM
    return (r == c).astype(dtype)


def _group_sum(x, ones_bd):
    hi = x.astype(BF16)
    lo = (x - hi.astype(F32)).astype(BF16)
    return _dot(hi, ones_bd) + _dot(lo, ones_bd)


def _head_rms(x, gain, ones_bd):
    ms = _group_sum(x * x, ones_bd) * (1.0 / HEAD_DIM)
    return x * lax.rsqrt(ms + NORM_EPS) * gain


def _sigmoid(x):
    return 1.0 / (1.0 + jnp.exp(-x))


def _silu(x):
    return x * _sigmoid(x)


def _log_sigmoid(x):
    return jnp.minimum(x, 0.0) - jnp.log(1.0 + jnp.exp(-jnp.abs(x)))


def _stack_heads(x, n_heads=N_HEADS):
    hl = _head_of_lane(x.shape, 1)
    return jnp.concatenate([jnp.where(hl == h, x, jnp.zeros_like(x)) for h in range(n_heads)], axis=0)


def _unstack_heads(r, c, n_heads=N_HEADS):
    hl = _head_of_lane((c, r.shape[1]), 1)
    out = jnp.zeros((c, r.shape[1]), F32)
    for h in range(n_heads):
        out = jnp.where(hl == h, r[h * c:(h + 1) * c, :], out)
    return out


def _rope(x, cos_t, sin_t):
    n = x.shape[1]
    first = (lax.broadcasted_iota(jnp.int32, x.shape, 1) % HEAD_DIM) < (HEAD_DIM // 2)
    partner = jnp.where(first, pltpu.roll(x, n - HEAD_DIM // 2, 1), pltpu.roll(x, HEAD_DIM // 2, 1))
    return x * cos_t + partner * sin_t


def _after_kernel(a_ref, dep_ref, o_ref):
    del dep_ref
    o_ref[...] = a_ref[...]


def _after(a, dep):
    if dep is None:
        return a
    a2 = a.reshape(1, a.size)
    out = pl.pallas_call(
        _after_kernel,
        in_specs=[pl.BlockSpec(a2.shape, lambda: (0, 0)), pl.BlockSpec(memory_space=pl.ANY)],
        out_specs=pl.BlockSpec(a2.shape, lambda: (0, 0)),
        out_shape=jax.ShapeDtypeStruct(a2.shape, a2.dtype),
        name="order_after",
    )(a2, dep)
    return out.reshape(a.shape)


def _norm_matmul_kernel(x_ref, g_ref, w_ref, o_ref, xn_ref, *, act):
    @pl.when(pl.program_id(1) == 0)
    def _():
        x = x_ref[...]
        ms = jnp.mean(x * x, axis=-1, keepdims=True)
        xn_ref[...] = (x * lax.rsqrt(ms + NORM_EPS) * g_ref[...]).astype(BF16)

    y = _dot(xn_ref[...], w_ref[...])
    if act == "sigmoid":
        y = _sigmoid(y)
    o_ref[...] = y.astype(o_ref.dtype)


def _norm_matmul(x, g, w, *, act=None, out_dtype=F32, tm=1024, tn=2176):
    t, d = x.shape
    w3 = w if w.ndim == 3 else w[None]
    n_per = w3.shape[2]
    tm = min(tm, t)
    tn = next(c for c in (tn, 2048, 1024, 512, 256, 128) if n_per % c == 0)
    per = n_per // tn
    n = w3.shape[0] * n_per
    assert t % tm == 0
    return pl.pallas_call(
        functools.partial(_norm_matmul_kernel, act=act),
        grid=(t // tm, n // tn),
        in_specs=[pl.BlockSpec((tm, d), lambda i, j: (i, 0)),
                  pl.BlockSpec((1, d), lambda i, j: (0, 0)),
                  pl.BlockSpec((None, d, tn), lambda i, j: (j // per, 0, j % per))],
        out_specs=pl.BlockSpec((tm, tn), lambda i, j: (i, j)),
        out_shape=jax.ShapeDtypeStruct((t, n), out_dtype),
        scratch_shapes=[pltpu.VMEM((tm, d), BF16)],
        compiler_params=_cp("parallel", "arbitrary"),
        name="norm_matmul",
    )(x, g.reshape(1, d), w3)


def _merge_kernel(x_ref, gate_ref, oh_ref, om_ref, on_ref, or_ref, wup_ref, wout_ref, o_ref):
    d = x_ref.shape[1]
    acc = None
    for m, r in enumerate((oh_ref, om_ref, on_ref, or_ref)):
        up = _dot(r[...].astype(BF16), wup_ref[m])
        term = gate_ref[:, m * d:(m + 1) * d].astype(F32) * up
        acc = term if acc is None else acc + term
    o_ref[...] = x_ref[...] + _dot(acc.astype(BF16), wout_ref[...])


def _merge(x, gates, outs, w_up, w_out, tm=512):
    t, d = x.shape
    tm = min(tm, t)
    mix = pl.BlockSpec((tm, MIX_WIDTH), lambda i: (i, 0))
    return pl.pallas_call(
        _merge_kernel,
        grid=(t // tm,),
        in_specs=[pl.BlockSpec((tm, d), lambda i: (i, 0)),
                  pl.BlockSpec((tm, 4 * d), lambda i: (i, 0)),
                  mix, mix, mix, mix,
                  pl.BlockSpec((4, MIX_WIDTH, d), lambda i: (0, 0, 0)),
                  pl.BlockSpec((d, d), lambda i: (0, 0))],
        out_specs=pl.BlockSpec((tm, d), lambda i: (i, 0)),
        out_shape=jax.ShapeDtypeStruct((t, d), F32),
        compiler_params=_cp("parallel"),
        name="merge",
    )(x, gates, *outs, w_up, w_out)


REC_BLOCK = 256


def _chunk_consts():
    t = lax.broadcasted_iota(jnp.int32, (CHUNK, CHUNK), 0)
    s = lax.broadcasted_iota(jnp.int32, (CHUNK, CHUNK), 1)
    return t, s


def _hgrn_levels():
    t = np.arange(CHUNK)
    rows = []
    masks = []
    h = CHUNK // 2
    while h >= 1:
        ref = (t // (2 * h)) * (2 * h) + h
        p = np.zeros((CHUNK, CHUNK), np.float32)
        p[t, np.minimum(ref, CHUNK - 1)] = 1.0
        rows.append(p)
        same = (t[:, None] // (2 * h)) == (t[None, :] // (2 * h))
        m = same & ((t[:, None] // h) % 2 == 1) & ((t[None, :] // h) % 2 == 0)
        masks.append(m.astype(np.float32))
        h //= 2
    masks.append(np.eye(CHUNK, dtype=np.float32))
    return np.concatenate(rows, 0), np.stack(masks, 0)


def _hgrn_kernel(p_ref, lb_ref, g_ref, psel_ref, lmask_ref, o_ref, st_ref):
    @pl.when(pl.program_id(1) == 0)
    def _():
        st_ref[...] = jnp.zeros_like(st_ref)

    c = CHUNK
    w = MIX_WIDTH
    ones_bd = _block_ones(w)
    bd_mask = _block_ones(w, F32)
    tri = (lax.broadcasted_iota(jnp.int32, (c, c), 0) >= lax.broadcasted_iota(jnp.int32, (c, c), 1)).astype(BF16)
    psel = psel_ref[...]
    n_lv = lmask_ref.shape[0]
    log_lb, log_1mlb, one_mlb = lb_ref[0:1, :], lb_ref[1:2, :], lb_ref[2:3, :]
    gain = g_ref[...]

    def chunk(ci, carry):
        r0 = pl.multiple_of(ci * c, c)
        q = _silu(p_ref[pl.ds(r0, c), 0:w])
        fl = p_ref[pl.ds(r0, c), w:2 * w]
        v = p_ref[pl.ds(r0, c), 2 * w:3 * w]
        gp = p_ref[pl.ds(r0, c), 3 * w:4 * w]
        a1 = jnp.broadcast_to(log_lb, fl.shape)
        a2 = log_1mlb + _log_sigmoid(fl)
        mx = jnp.maximum(a1, a2)
        log_f = mx + jnp.log(jnp.exp(a1 - mx) + jnp.exp(a2 - mx))
        k = one_mlb * _sigmoid(-fl)
        b = _dot01_l(tri, log_f)
        bref = _dot01_l(psel, b)
        vb = v.astype(BF16)
        a = jnp.zeros((N_HEADS * c, c), F32)
        for lv in range(n_lv):
            if lv < n_lv - 1:
                br = bref[lv * c:(lv + 1) * c, :]
                qs = q * jnp.exp(jnp.minimum(b - br, 0.0))
                ks = k * jnp.exp(jnp.minimum(br - b, 0.0))
            else:
                qs, ks = q, k
            s_lv = _dot_nt(_stack_heads(qs).astype(BF16), ks.astype(BF16))
            a = a + jnp.concatenate([lmask_ref[lv]] * N_HEADS, axis=0) * s_lv
        o = _unstack_heads(_dot(a.astype(BF16), vb), c)
        st = st_ref[...]
        o = o + _dot_nt((q * jnp.exp(b)).astype(BF16), st.astype(BF16))
        b_last = b[c - 1:c, :]
        kb = k * jnp.exp(b_last - b)
        st_ref[...] = st * jnp.exp(b_last) + bd_mask * _dot_tn(vb, kb.astype(BF16))
        y = _head_rms(o, gain, ones_bd) * _silu(gp)
        o_ref[pl.ds(r0, c), :] = y
        return carry

    lax.fori_loop(0, p_ref.shape[0] // c, chunk, 0)


def _hgrn(proj, lb_rows, gain, bsz, seq):
    psel, lmask = _hgrn_levels()
    tb = min(REC_BLOCK, seq)
    nb = seq // tb
    return pl.pallas_call(
        _hgrn_kernel,
        grid=(bsz, nb),
        in_specs=[pl.BlockSpec((tb, 4 * MIX_WIDTH), lambda b, i: (b * nb + i, OFF_H // (4 * MIX_WIDTH))),
                  pl.BlockSpec((8, MIX_WIDTH), lambda b, i: (0, 0)),
                  pl.BlockSpec((1, MIX_WIDTH), lambda b, i: (0, 0)),
                  pl.BlockSpec(psel.shape, lambda b, i: (0, 0)),
                  pl.BlockSpec(lmask.shape, lambda b, i: (0, 0, 0))],
        out_specs=pl.BlockSpec((tb, MIX_WIDTH), lambda b, i: (b * nb + i, 0)),
        out_shape=jax.ShapeDtypeStruct((bsz * seq, MIX_WIDTH), F32),
        scratch_shapes=[pltpu.VMEM((MIX_WIDTH, MIX_WIDTH), F32)],
        compiler_params=_cp("parallel", "arbitrary"),
        name="hgrn2",
    )(proj, lb_rows, gain.reshape(1, MIX_WIDTH), jnp.asarray(psel, BF16), jnp.asarray(lmask, F32))


def _ret_kernel(p_ref, cos_ref, sin_ref, dec_ref, decin_ref, g_ref, o_ref, st_ref):
    @pl.when(pl.program_id(1) == 0)
    def _():
        st_ref[...] = jnp.zeros_like(st_ref)

    c = CHUNK
    w = MIX_WIDTH
    ones_bd = _block_ones(w)
    bd_mask = _block_ones(w, F32)
    gain = g_ref[...]
    dec_q = dec_ref[0:c, :]
    dec_k = dec_ref[c:2 * c, :]
    dec_state = dec_ref[2 * c:2 * c + 1, :]
    dec_in = decin_ref[...]

    def chunk(ci, carry):
        r0 = pl.multiple_of(ci * c, c)
        cos_t = cos_ref[pl.ds(r0, c), :]
        sin_t = sin_ref[pl.ds(r0, c), :]
        q = _rope(p_ref[pl.ds(r0, c), 0:w], cos_t, sin_t)
        k = _rope(p_ref[pl.ds(r0, c), w:2 * w], cos_t, sin_t) * (HEAD_DIM ** -0.5)
        v = p_ref[pl.ds(r0, c), 2 * w:3 * w]
        gp = p_ref[pl.ds(r0, c), 3 * w:4 * w]
        vb = v.astype(BF16)
        a = _dot_nt(_stack_heads(q).astype(BF16), k.astype(BF16)) * dec_in
        o = _unstack_heads(_dot(a.astype(BF16), vb), c)
        st = st_ref[...]
        o = o + _dot_nt(q.astype(BF16), st.astype(BF16)) * dec_q
        st_ref[...] = st * dec_state + bd_mask * _dot_tn(vb, (k * dec_k).astype(BF16))
        o_ref[pl.ds(r0, c), :] = _head_rms(o, gain, ones_bd) * _silu(gp)
        return carry

    lax.fori_loop(0, p_ref.shape[0] // c, chunk, 0)


def _ret_consts():
    log_gamma = np.log1p(-np.exp2(-5.0 - np.arange(N_HEADS, dtype=np.float64)))
    t = np.arange(CHUNK, dtype=np.float64)
    lane_h = np.arange(MIX_WIDTH) // HEAD_DIM
    dec_q = np.exp(log_gamma[lane_h][None, :] * (t[:, None] + 1.0))
    dec_k = np.exp(log_gamma[lane_h][None, :] * (CHUNK - 1.0 - t[:, None]))
    dec_state = np.exp(log_gamma[lane_h] * CHUNK)[None, :]
    dec = np.concatenate([dec_q, dec_k, np.broadcast_to(dec_state, (8, MIX_WIDTH))], 0)
    diff = t[:, None] - t[None, :]
    dec_in = np.concatenate([np.where(diff >= 0, np.exp(log_gamma[h] * diff), 0.0) for h in range(N_HEADS)], 0)
    return dec.astype(np.float32), dec_in.astype(np.float32)


def _ret(proj, cos4, sin4, gain, bsz, seq):
    dec, dec_in = _ret_consts()
    tb = min(REC_BLOCK, seq)
    nb = seq // tb
    return pl.pallas_call(
        _ret_kernel,
        grid=(bsz, nb),
        in_specs=[pl.BlockSpec((tb, 4 * MIX_WIDTH), lambda b, i: (b * nb + i, OFF_R // (4 * MIX_WIDTH))),
                  pl.BlockSpec((tb, MIX_WIDTH), lambda b, i: (i, 0)),
                  pl.BlockSpec((tb, MIX_WIDTH), lambda b, i: (i, 0)),
                  pl.BlockSpec(dec.shape, lambda b, i: (0, 0)),
                  pl.BlockSpec(dec_in.shape, lambda b, i: (0, 0)),
                  pl.BlockSpec((1, MIX_WIDTH), lambda b, i: (0, 0))],
        out_specs=pl.BlockSpec((tb, MIX_WIDTH), lambda b, i: (b * nb + i, 0)),
        out_shape=jax.ShapeDtypeStruct((bsz * seq, MIX_WIDTH), F32),
        scratch_shapes=[pltpu.VMEM((MIX_WIDTH, MIX_WIDTH), F32)],
        compiler_params=_cp("parallel", "arbitrary"),
        name="retention",
    )(proj, cos4, sin4, jnp.asarray(dec), jnp.asarray(dec_in), gain.reshape(1, MIX_WIDTH))


def _hgrn_lb_rows(lb):
    lb = lb.astype(F32)
    rows = jnp.stack([jnp.log(lb), jnp.log1p(-lb), 1.0 - lb], 0)
    return jnp.concatenate([rows, jnp.zeros((5, lb.shape[0]), F32)], 0)


def _rope_lane_tables(seq):
    inv = 1.0 / (ROPE_THETA ** (jnp.arange(0, HEAD_DIM, 2, dtype=F32) / HEAD_DIM))
    ang = jnp.arange(seq, dtype=F32)[:, None] * inv[None, :]
    cos, sin = jnp.cos(ang), jnp.sin(ang)
    cos_t = jnp.tile(cos, (1, LANES // (HEAD_DIM // 2)))
    sin_t = jnp.tile(jnp.concatenate([-sin, sin], axis=1), (1, LANES // HEAD_DIM))
    return cos_t, sin_t


def _expand_heads(cols, shape):
    hl = _head_of_lane(shape, 1)
    out = jnp.broadcast_to(cols[-1], shape)
    for h in range(len(cols) - 2, -1, -1):
        out = jnp.where(hl == h, jnp.broadcast_to(cols[h], shape), out)
    return out


def _mlstm_kernel(p_ref, gcol_ref, grow_ref, cw_ref, cb_ref, gbr_ref, gbc_ref, g_ref, o_ref,
                  ct_ref, n_ref, m_ref, hist_ref, cbuf_ref, qk_ref):
    c = CHUNK
    w = MIX_WIDTH
    tb = p_ref.shape[0]

    @pl.when(pl.program_id(1) == 0)
    def _():
        ct_ref[...] = jnp.zeros_like(ct_ref)
        n_ref[...] = jnp.zeros_like(n_ref)
        m_ref[...] = jnp.zeros_like(m_ref)
        hist_ref[...] = jnp.zeros_like(hist_ref)

    cbuf_ref[0:8, :] = hist_ref[...]
    cbuf_ref[8:, :] = p_ref[:, 0:2 * w]
    hist_ref[...] = p_ref[tb - 8:tb, 0:2 * w]
    acc = jnp.broadcast_to(cb_ref[...], (tb, 2 * w))
    for j in range(CONV_W):
        acc = acc + cw_ref[j:j + 1, :] * cbuf_ref[pl.ds(8 - (CONV_W - 1) + j, tb), :]
    qk_ref[...] = _silu(acc)

    ones_bd = _block_ones(w)
    bd_mask = _block_ones(w, F32)
    ti = lax.broadcasted_iota(jnp.int32, (c, c), 0)
    si = lax.broadcasted_iota(jnp.int32, (c, c), 1)
    causal = ti >= si
    tri = causal.astype(BF16)
    tri_t = (ti <= si).astype(BF16)
    gain = g_ref[...]
    ones_ext = jnp.ones((c, LANES), BF16)

    def chunk(ci, carry):
        r0 = pl.multiple_of(ci * c, c)
        q = qk_ref[pl.ds(r0, c), 0:w]
        k = qk_ref[pl.ds(r0, c), w:2 * w] * (HEAD_DIM ** -0.5)
        v = p_ref[pl.ds(r0, c), 2 * w:3 * w]
        op = p_ref[pl.ds(r0, c), 3 * w:4 * w]
        gc = gcol_ref[pl.ds(r0, c), :] + gbr_ref[...]
        gr = grow_ref[ci] + gbc_ref[...]
        b_c = _dot01_l(tri, _log_sigmoid(gc))
        b_r = _dot01_r(_log_sigmoid(gr), tri_t)
        wd, s_inter, em, wk, decay = [], [], [], [], []
        for h in range(N_HEADS):
            bc = b_c[:, N_HEADS + h:N_HEADS + h + 1]
            lic = gc[:, h:h + 1]
            br = b_r[N_HEADS + h:N_HEADS + h + 1, :]
            lir = gr[h:h + 1, :]
            dmat = jnp.where(causal, bc - br + lir, -jnp.inf)
            m_prev = m_ref[h:h + 1, 0:1]
            inter = bc + m_prev
            mrow = jnp.maximum(inter, jnp.max(dmat, axis=1, keepdims=True))
            wd.append(jnp.exp(dmat - mrow))
            s_inter.append(jnp.exp(inter - mrow))
            em.append(jnp.exp(-mrow))
            b_last = br[:, c - 1:c]
            m_new = jnp.maximum(b_last + m_prev, jnp.max(b_last - br + lir, axis=1, keepdims=True))
            wk.append(jnp.exp(b_last - bc + lic - m_new))
            decay.append(jnp.exp(b_last + m_prev - m_new))
            m_ref[h:h + 1, :] = jnp.broadcast_to(m_new, (1, LANES))
        s_inter_l = _expand_heads(s_inter, (c, w))
        em_l = _expand_heads(em, (c, w))
        wk_l = _expand_heads(wk, (c, w))
        decay_l = _expand_heads(decay, (1, w))
        qk = _dot_nt(_stack_heads(q).astype(BF16), k.astype(BF16))
        wmat = jnp.concatenate(wd, axis=0) * qk
        vb = v.astype(BF16)
        r = _dot(wmat.astype(BF16), jnp.concatenate([vb, ones_ext], axis=1))
        num_intra = _unstack_heads(r[:, 0:w], c)
        rs_l = _expand_heads([r[h * c:(h + 1) * c, w:w + 1] for h in range(N_HEADS)], (c, w))
        ct = ct_ref[...]
        nrow = n_ref[0:1, :]
        num = s_inter_l * _dot_nt(q.astype(BF16), ct.astype(BF16)) + num_intra
        den = s_inter_l * _group_sum(q * nrow, ones_bd) + rs_l
        hval = num / jnp.maximum(jnp.abs(den), em_l)
        kw = wk_l * k
        ct_ref[...] = ct * decay_l + bd_mask * _dot_tn(vb, kw.astype(BF16))
        n_ref[0:1, :] = nrow * decay_l + jnp.sum(kw, axis=0, keepdims=True)
        o_ref[pl.ds(r0, c), :] = _head_rms(hval, gain, ones_bd) * _sigmoid(op)
        return carry

    lax.fori_loop(0, tb // c, chunk, 0)


def _mlstm(proj, conv_w, conv_b, gate_b, gain, bsz, seq):
    t = bsz * seq
    w = MIX_WIDTH
    tb = min(REC_BLOCK, seq)
    nb = seq // tb
    ncb = tb // CHUNK
    grow = proj[:, OFF_MG:OFF_MG + 8].reshape(t // CHUNK, CHUNK, 8).transpose(0, 2, 1)
    gb_row = jnp.zeros((1, LANES), F32).at[0, 0:8].set(gate_b.astype(F32))
    gb_col = gate_b.astype(F32).reshape(8, 1)
    return pl.pallas_call(
        _mlstm_kernel,
        grid=(bsz, nb),
        in_specs=[pl.BlockSpec((tb, 4 * w), lambda b, i: (b * nb + i, OFF_M // (4 * w))),
                  pl.BlockSpec((tb, LANES), lambda b, i: (b * nb + i, OFF_MG // LANES)),
                  pl.BlockSpec((ncb, 8, CHUNK), lambda b, i: (b * nb + i, 0, 0)),
                  pl.BlockSpec((CONV_W, 2 * w), lambda b, i: (0, 0)),
                  pl.BlockSpec((1, 2 * w), lambda b, i: (0, 0)),
                  pl.BlockSpec((1, LANES), lambda b, i: (0, 0)),
                  pl.BlockSpec((8, 1), lambda b, i: (0, 0)),
                  pl.BlockSpec((1, w), lambda b, i: (0, 0))],
        out_specs=pl.BlockSpec((tb, w), lambda b, i: (b * nb + i, 0)),
        out_shape=jax.ShapeDtypeStruct((t, w), F32),
        scratch_shapes=[pltpu.VMEM((w, w), F32), pltpu.VMEM((8, w), F32), pltpu.VMEM((8, LANES), F32),
                        pltpu.VMEM((8, 2 * w), F32), pltpu.VMEM((tb + 8, 2 * w), F32),
                        pltpu.VMEM((tb, 2 * w), F32)],
        compiler_params=_cp("parallel", "arbitrary"),
        name="mlstm",
    )(proj, proj, grow, conv_w.astype(F32), conv_b.astype(F32).reshape(1, 2 * w), gb_row, gb_col,
      gain.reshape(1, w))


NSA_TQ = 128
NSA_KC = 512
GW = 2 * HEAD_DIM


def _nsa_prep_kernel(pq_ref, pk_ref, pv_ref, cos_ref, sin_ref, qg_ref, kg_ref,
                     qn_ref, qr_ref, ks_ref, kw_ref, vst_ref, vwt_ref):
    w = MIX_WIDTH
    for src, dst in ((pv_ref[:, 0:GW], vst_ref), (pv_ref[:, GW:2 * GW], vwt_ref)):
        vt = src.T
        tk = dst.shape[4]
        for g in range(NSA_GROUPS):
            rows = vt[g * HEAD_DIM:(g + 1) * HEAD_DIM, :]
            dup = jnp.concatenate([rows, rows], axis=0).astype(BF16)
            for j in range(dst.shape[2]):
                dst[0, g, j] = dup[:, j * tk:(j + 1) * tk]
    ones_bd = _block_ones(w)
    cos_t, sin_t = cos_ref[...], sin_ref[...]
    scale = HEAD_DIM ** -0.5
    qh = _head_rms(pq_ref[...], qg_ref[...], ones_bd)
    qn_ref[...] = (qh * scale).astype(BF16)
    qr_ref[...] = (_rope(qh, cos_t, sin_t) * scale).astype(BF16)
    ks_ref[...] = _rope(_head_rms(pk_ref[:, 0:w], kg_ref[1:2, :], ones_bd), cos_t, sin_t).astype(BF16)
    kw_ref[...] = _rope(_head_rms(pk_ref[:, w:2 * w], kg_ref[2:3, :], ones_bd), cos_t, sin_t).astype(BF16)


def _nsa_prep(proj, cos4, sin4, qnorm_g, knorm_g, bsz, seq):
    t = bsz * seq
    w = MIX_WIDTH
    tm = min(NSA_KC, seq)
    tq = min(NSA_TQ, seq)
    ns = seq // tm
    qg = jnp.tile(qnorm_g.astype(F32), w // HEAD_DIM).reshape(1, w)
    kg = jnp.concatenate([jnp.tile(knorm_g.astype(F32), (1, w // HEAD_DIM)), jnp.zeros((5, w), F32)], axis=0)
    out = jax.ShapeDtypeStruct((t, w), BF16)
    row = pl.BlockSpec((tm, w), lambda i: (i, 0))
    return pl.pallas_call(
        _nsa_prep_kernel,
        grid=(t // tm,),
        in_specs=[pl.BlockSpec((tm, w), lambda i: (i, OFF_NQ // w)),
                  pl.BlockSpec((tm, 2 * w), lambda i: (i, OFF_KD // (2 * w))),
                  pl.BlockSpec((tm, 2 * GW), lambda i: (i, OFF_V // (2 * GW))),
                  pl.BlockSpec((tm, w), lambda i: (i % ns, 0)),
                  pl.BlockSpec((tm, w), lambda i: (i % ns, 0)),
                  pl.BlockSpec((1, w), lambda i: (0, 0)),
                  pl.BlockSpec((8, w), lambda i: (0, 0))],
        out_specs=[row, row, row, row,
                   pl.BlockSpec((1, NSA_GROUPS, 1, GW, tm), lambda i: (i // ns, 0, i % ns, 0, 0)),
                   pl.BlockSpec((1, NSA_GROUPS, tm // tq, GW, tq), lambda i: (i // ns, 0, i % ns, 0, 0))],
        out_shape=[out, out, out, out,
                   jax.ShapeDtypeStruct((bsz, NSA_GROUPS, seq // tm, GW, tm), BF16),
                   jax.ShapeDtypeStruct((bsz, NSA_GROUPS, seq // tq, GW, tq), BF16)],
        compiler_params=_cp("parallel"),
        name="nsa_prep",
    )(proj, proj, proj, cos4, sin4, qg, kg)


def _nsa_cmp_kernel(xr_ref, pe_ref, w0_ref, w1_ref, kg_ref, ovt_ref, qn_ref, ocmp_ref, sel_ref,
                    kc_ref, vc_ref, v_ref, *, n_top):
    tq = qn_ref.shape[0]
    nr = xr_ref.shape[0]
    nsel = sel_ref.shape[2]
    w = MIX_WIDTH

    @pl.when(pl.program_id(1) == 0)
    def _():
        xr = xr_ref[...]
        y0 = _dot((xr + pe_ref[0]).astype(BF16), w0_ref[...])
        y1 = _dot((xr + pe_ref[1]).astype(BF16), w1_ref[...])
        kv = y0 + pltpu.roll(y1, nr - 1, 0)
        kc_ref[...] = _head_rms(kv[:, 0:w], kg_ref[...], _block_ones(w)).astype(BF16)
        vc_ref[...] = kv[:, w:2 * w].astype(BF16)

    pos0 = pl.program_id(1) * tq
    hl = _head_of_lane((tq, GW), 1)
    pos_r = pos0 + lax.broadcasted_iota(jnp.int32, (tq, nr), 0)
    valid = lax.broadcasted_iota(jnp.int32, (tq, nr), 1) * CMP_STRIDE + (CMP_LEN - 1) <= pos_r
    pos_c = pos0 + lax.broadcasted_iota(jnp.int32, (nr, tq), 1)
    valid_t = lax.broadcasted_iota(jnp.int32, (nr, tq), 0) * CMP_STRIDE + (CMP_LEN - 1) <= pos_c
    jrow = lax.broadcasted_iota(jnp.int32, (nsel, tq), 0)
    cur = (pos0 + lax.broadcasted_iota(jnp.int32, (nsel, tq), 1)) // SEL_BLOCK
    forced = (jrow == 0) | (jrow == cur) | (jrow == cur - 1)
    ovt = ovt_ref[...]

    for g in range(NSA_GROUPS):
        qg = qn_ref[:, g * GW:(g + 1) * GW]
        kg = kc_ref[:, g * GW:(g + 1) * GW]
        vg = vc_ref[:, g * GW:(g + 1) * GW]
        o_g = jnp.zeros((tq, GW), F32)
        pt_sum = jnp.zeros((nr, tq), F32)
        for hh in range(2):
            qm = jnp.where(hl == hh, qg, jnp.zeros_like(qg))
            s = jnp.where(valid, _dot_nt(qm, kg), NEG_INF)
            e = jnp.exp(s - jnp.max(s, axis=1, keepdims=True))
            p = jnp.where(valid, e / jnp.sum(e, axis=1, keepdims=True), 0.0)
            o_g = jnp.where(hl == hh, _dot(p.astype(BF16), vg), o_g)
            st = jnp.where(valid_t, _dot_nt(kg, qm), NEG_INF)
            et = jnp.exp(st - jnp.max(st, axis=0, keepdims=True))
            pt_sum = pt_sum + jnp.where(valid_t, et / jnp.sum(et, axis=0, keepdims=True), 0.0)
        ocmp_ref[:, g * GW:(g + 1) * GW] = o_g
        p_hi = pt_sum.astype(BF16)
        p_lo = (pt_sum - p_hi.astype(F32)).astype(BF16)
        imp = _dot(ovt, p_hi) + _dot(ovt, p_lo)
        val = jnp.where(jrow <= cur, imp + FORCE_BONUS * forced.astype(F32), NEG_INF)
        v_ref[...] = val

        def rank(jp, cnt):
            row = v_ref[pl.ds(jp, 1), :]
            tie = jnp.where(jrow > jp, 1.0, 0.0)
            return cnt + jnp.where(row > val, 1.0, jnp.where(row == val, tie, 0.0))

        cnt = lax.fori_loop(0, nsel, rank, jnp.zeros((nsel, tq), F32))
        sel_ref[0, g] = ((cnt < n_top) & (jrow <= cur)).astype(F32)


def _nsa_cmp_weights(cmp_pe, cmp_w):
    half = CMP_LEN // 2
    wl = cmp_w.astype(F32).reshape(2, 2, half, HEAD_DIM, HEAD_DIM)
    eye2 = jnp.eye(2, dtype=F32)
    w2 = jnp.einsum('kardz,kK,gG,h->arkgdKGhz', wl, eye2, eye2, jnp.ones((2,), F32))
    w2 = w2.reshape(2, half * 4 * HEAD_DIM, 8 * HEAD_DIM)
    pl_ = cmp_pe.astype(F32).reshape(2, 2, half, HEAD_DIM)
    pe2 = jnp.broadcast_to(pl_.transpose(1, 2, 0, 3)[:, :, :, None, :], (2, half, 2, 2, HEAD_DIM))
    return w2.astype(BF16), pe2.reshape(2, 1, half * 4 * HEAD_DIM)


def _nsa_cmp(kcvc, qn, cmp_pe, cmp_w, knorm0, bsz, seq, tq=512):
    t = bsz * seq
    w = MIX_WIDTH
    tq = min(tq, seq)
    nq = seq // tq
    nr = seq // CMP_STRIDE
    nsel = seq // SEL_BLOCK
    n_top = min(SEL_TOPK, nsel)
    w2, pe2 = _nsa_cmp_weights(cmp_pe, cmp_w)
    xr = kcvc.reshape(t // CMP_STRIDE, CMP_STRIDE * w)
    kg = jnp.tile(knorm0.astype(F32), w // HEAD_DIM).reshape(1, w)
    n_i = np.arange(nr)[:, None] * CMP_STRIDE
    j_i = np.arange(nsel)[None, :] * SEL_BLOCK
    ov = ((n_i < j_i + SEL_BLOCK) & (n_i + CMP_LEN > j_i)).astype(np.float32)
    ov[nr - 1, :] = 0.0
    kin = CMP_STRIDE * w
    return pl.pallas_call(
        functools.partial(_nsa_cmp_kernel, n_top=n_top),
        grid=(bsz, nq),
        in_specs=[pl.BlockSpec((nr, kin), lambda b, i: (b, 0)),
                  pl.BlockSpec((2, 1, kin), lambda b, i: (0, 0, 0)),
                  pl.BlockSpec((None, kin, 2 * w), lambda b, i: (0, 0, 0)),
                  pl.BlockSpec((None, kin, 2 * w), lambda b, i: (1, 0, 0)),
                  pl.BlockSpec((1, w), lambda b, i: (0, 0)),
                  pl.BlockSpec((nsel, nr), lambda b, i: (0, 0)),
                  pl.BlockSpec((tq, w), lambda b, i: (b * nq + i, 0))],
        out_specs=[pl.BlockSpec((tq, w), lambda b, i: (b * nq + i, 0)),
                   pl.BlockSpec((1, NSA_GROUPS, nsel, tq), lambda b, i: (b, 0, 0, i))],
        out_shape=[jax.ShapeDtypeStruct((t, w), F32),
                   jax.ShapeDtypeStruct((bsz, NSA_GROUPS, nsel, seq), F32)],
        scratch_shapes=[pltpu.VMEM((nr, w), BF16), pltpu.VMEM((nr, w), BF16), pltpu.VMEM((nsel, tq), F32)],
        compiler_params=_cp("parallel", "arbitrary"),
        name="nsa_cmp",
    )(xr, pe2, w2, w2, kg, jnp.asarray(ov.T, BF16), qn)


def _nsa_attn_kernel(qr_ref, ks_ref, kw_ref, vs_ref, vw_ref, sel_ref, ocmp_ref, gate_ref, o_ref, *, kc, wt):
    tq = qr_ref.shape[0]
    i = pl.program_id(1)
    hl = _head_of_lane((tq, GW), 1)
    nbk = kc // SEL_BLOCK
    groups = range(NSA_GROUPS)

    def stacked_q(g):
        q = qr_ref[:, g * GW:(g + 1) * GW]
        return jnp.concatenate([jnp.where(hl == 0, q, jnp.zeros_like(q)), jnp.where(hl == 1, q, jnp.zeros_like(q))],
                               axis=0)

    qs_all = [stacked_q(g) for g in groups]

    def lane_qpos(rows):
        return i * tq + lax.broadcasted_iota(jnp.int32, (rows, 2 * tq), 1) % tq

    def finish(acc, l):
        ot = (acc / l).T
        return jnp.where(hl == 0, ot[0:tq, :], ot[tq:2 * tq, :])

    qpos_s = lane_qpos(kc)
    krow_s = lax.broadcasted_iota(jnp.int32, (kc, 2 * tq), 0)

    def sel_step(g, c, carry, diagonal):
        m, l, acc = carry
        k0 = pl.multiple_of(c * kc, kc)
        st = _dot_nt(ks_ref[pl.ds(k0, kc), g * GW:(g + 1) * GW], qs_all[g])
        srows = sel_ref[0, g, pl.ds(pl.multiple_of(c * nbk, nbk), nbk), :]
        srows = jnp.concatenate([srows, srows], axis=1)
        smask = jnp.concatenate([jnp.broadcast_to(srows[r:r + 1, :], (SEL_BLOCK, 2 * tq)) for r in range(nbk)],
                                axis=0)
        msk = smask > 0.5
        if diagonal:
            msk = msk & (k0 + krow_s <= qpos_s)
        st = jnp.where(msk, st, NEG_INF)
        m_new = jnp.maximum(m, jnp.max(st, axis=0, keepdims=True))
        p = jnp.exp(st - m_new)
        alpha = jnp.exp(m - m_new)
        l = l * alpha + jnp.sum(p, axis=0, keepdims=True)
        acc = acc * alpha + _dot(vs_ref[0, g, c], p.astype(BF16))
        return m_new, l, acc

    def sel_body(c, carries, diagonal):
        return tuple(sel_step(g, c, carries[g], diagonal) for g in groups)

    init = (jnp.full((1, 2 * tq), NEG_INF, F32), jnp.zeros((1, 2 * tq), F32), jnp.zeros((GW, 2 * tq), F32))
    n_before = (i * tq) // kc
    carries = lax.fori_loop(0, n_before, functools.partial(sel_body, diagonal=False), (init,) * NSA_GROUPS)
    carries = sel_body(n_before, carries, True)

    j0 = jnp.maximum(i - (wt - 1), 0)
    k0 = pl.multiple_of(j0 * tq, tq)
    span = wt * tq
    kpos = k0 + lax.broadcasted_iota(jnp.int32, (span, 2 * tq), 0)
    qpos_w = lane_qpos(span)
    wmask = (kpos <= qpos_w) & (kpos > qpos_w - WINDOW)
    gb = _sigmoid(gate_ref[...])
    for g in groups:
        _, l_s, acc_s = carries[g]
        o_sel = finish(acc_s, l_s)
        st = jnp.where(wmask, _dot_nt(kw_ref[pl.ds(k0, span), g * GW:(g + 1) * GW], qs_all[g]), NEG_INF)
        p = jnp.exp(st - jnp.max(st, axis=0, keepdims=True))
        vt = jnp.concatenate([vw_ref[0, g, j0 + r] for r in range(wt)], axis=1)
        o_win = finish(_dot(vt, p.astype(BF16)), jnp.sum(p, axis=0, keepdims=True))

        def gate(branch):
            cols = [gb[:, (2 * g + hh) * 3 + branch:(2 * g + hh) * 3 + branch + 1] for hh in range(2)]
            return _expand_heads(cols, (tq, GW))

        o_ref[:, g * GW:(g + 1) * GW] = (gate(0) * ocmp_ref[:, g * GW:(g + 1) * GW]
                                         + gate(1) * o_sel + gate(2) * o_win)


def _nsa_attn(proj, qr, ks, kw, vst, vwt, sel, o_cmp, bsz, seq):
    t = bsz * seq
    w = MIX_WIDTH
    tq = min(NSA_TQ, seq)
    nq = seq // tq
    nsel = seq // SEL_BLOCK
    kc = min(NSA_KC, seq)
    wt = min(WINDOW // tq + 1, nq)
    kspec = pl.BlockSpec((seq, w), lambda b, i: (b, 0))
    return pl.pallas_call(
        functools.partial(_nsa_attn_kernel, kc=kc, wt=wt),
        grid=(bsz, nq),
        in_specs=[pl.BlockSpec((tq, w), lambda b, i: (b * nq + i, 0)),
                  kspec, kspec,
                  pl.BlockSpec((1, NSA_GROUPS, seq // kc, GW, kc), lambda b, i: (b, 0, 0, 0, 0)),
                  pl.BlockSpec((1, NSA_GROUPS, nq, GW, tq), lambda b, i: (b, 0, 0, 0, 0)),
                  pl.BlockSpec((1, NSA_GROUPS, nsel, tq), lambda b, i: (b, 0, 0, i)),
                  pl.BlockSpec((tq, w), lambda b, i: (b * nq + i, 0)),
                  pl.BlockSpec((tq, LANES), lambda b, i: (b * nq + i, OFF_NG // LANES))],
        out_specs=pl.BlockSpec((tq, w), lambda b, i: (b * nq + i, 0)),
        out_shape=jax.ShapeDtypeStruct((t, w), F32),
        compiler_params=_cp("parallel", "arbitrary"),
        name="nsa_attn",
    )(qr, ks, kw, vst, vwt, sel, o_cmp, proj)


def _nsa(proj, kcvc, cos4, sin4, qnorm_g, knorm_g, cmp_pe, cmp_w, bsz, seq):
    qn, qr, ks, kw, vst, vwt = _nsa_prep(proj, cos4, sin4, qnorm_g, knorm_g, bsz, seq)
    o_cmp, sel = _nsa_cmp(kcvc, qn, cmp_pe, cmp_w, knorm_g[0], bsz, seq)
    return _nsa_attn(proj, qr, ks, kw, vst, vwt, sel, o_cmp, bsz, seq)


PEER_TT = 128
PEER_CT = 8
HALF_D = 512


SUBLANES = 8
CODE_BITS = 127
FAR_BELOW = -3.0e38


def _with_code(x, code):
    bits = lax.bitcast_convert_type(x, jnp.int32)
    return lax.bitcast_convert_type((bits & ~CODE_BITS) | code, F32)


def _split_code(x):
    bits = lax.bitcast_convert_type(x, jnp.int32)
    return lax.bitcast_convert_type(bits & ~CODE_BITS, F32), bits & CODE_BITS


def _sort16_desc(xs):
    xs = list(xs)
    n = len(xs)
    k = 2
    while k <= n:
        j = k // 2
        while j >= 1:
            for i in range(n):
                l = i ^ j
                if l > i:
                    hi, lo = jnp.maximum(xs[i], xs[l]), jnp.minimum(xs[i], xs[l])
                    xs[i], xs[l] = (hi, lo) if (i & k) == 0 else (lo, hi)
            j //= 2
        k *= 2
    return xs


def _merge16_desc(xs):
    xs = list(xs)
    j = len(xs) // 2
    while j >= 1:
        for i in range(len(xs)):
            l = i ^ j
            if l > i:
                xs[i], xs[l] = jnp.maximum(xs[i], xs[l]), jnp.minimum(xs[i], xs[l])
        j //= 2
    return xs


def _top16_columns(x):
    n = PEER_TOPK
    xs = _sort16_desc([x[SUBLANES * j:SUBLANES * (j + 1), :] for j in range(n)])
    shift = SUBLANES // 2
    while shift >= 1:
        rolled = [pltpu.roll(a, shift, 0) for a in xs]
        xs = _merge16_desc([jnp.maximum(xs[i], rolled[n - 1 - i]) for i in range(n)])
        shift //= 2
    return xs


_PEER_CAND_TILES = ((0, 0, 8), (0, 1, 8), (1, 0, 8), (2, 0, 5), (3, 0, 4), (4, 0, 3), (5, 0, 2), (6, 0, 2), (7, 0, 2))


ROUTE_HEADS_PER_STEP = 4


def _route_head(q_ref, key_ref, hh):
    tt = q_ref.shape[0]
    nk = PEER_NKEYS
    n = PEER_TOPK
    row = lax.broadcasted_iota(jnp.int32, (nk, tt), 0)
    sub = lax.broadcasted_iota(jnp.int32, (SUBLANES, tt), 0)
    vals, ids = [], []
    for p in range(2):
        c0 = (2 * hh + p) * PEER_KDIM
        st = _dot_nt(key_ref[hh, p], q_ref[:, c0:c0 + PEER_KDIM])
        top = [_split_code(a) for a in _top16_columns(_with_code(st, (nk - 1) - row))]
        vals.append([v for v, _ in top])
        ids.append([(nk - 1) - c for _, c in top])
    (v1, v2), (i1, i2) = vals, ids

    def stack(xs, lo):
        out = xs[lo]
        for s in range(1, SUBLANES):
            out = jnp.where(sub == s, xs[lo + s], out)
        return out

    v2t, i2t = (stack(v2, 0), stack(v2, SUBLANES)), (stack(i2, 0), stack(i2, SUBLANES))
    cand, cexp = [], []
    for a, tile, nvalid in _PEER_CAND_TILES:
        v = v1[a] + v2t[tile]
        cand.append(v if nvalid == SUBLANES else jnp.where(sub < nvalid, v, FAR_BELOW))
        cexp.append(i1[a] * nk + i2t[tile])
    cand.append(stack(v1, SUBLANES) + v2[0])
    cexp.append(stack(i1, SUBLANES) * nk + i2[0])
    n_tiles = len(cand)
    slot_code = [(nk - 1) - (c * SUBLANES + sub) for c in range(n_tiles)]
    coded = [_with_code(v, sc) for v, sc in zip(cand, slot_code)]
    coded += [jnp.full((SUBLANES, tt), FAR_BELOW, F32)] * (n - n_tiles)
    top = [_split_code(a) for a in _top16_columns(jnp.concatenate(coded, axis=0))]
    call = jnp.concatenate(cexp, axis=0)
    slot = (nk - 1) - lax.broadcasted_iota(jnp.int32, call.shape, 0)
    ex = [jnp.exp(v - top[0][0]) for v, _ in top]
    tot = ex[0]
    for k in range(1, n):
        tot = tot + ex[k]
    krow = lax.broadcasted_iota(jnp.int32, (n, tt), 0)
    e_tile = jnp.zeros((n, tt), F32)
    g_tile = jnp.zeros((n, tt), F32)
    for k in range(n):
        hit = slot == jnp.concatenate([top[k][1]] * n_tiles, axis=0)
        e_k = jnp.sum(jnp.where(hit, call, 0), axis=0, keepdims=True)
        e_tile = jnp.where(krow == k, e_k.astype(F32), e_tile)
        g_tile = jnp.where(krow == k, (ex[k] / tot)[0:1, :], g_tile)
    return e_tile, g_tile


def _peer_route_kernel(q_ref, key_ref, e_ref, g_ref, e_scr, g_scr):
    hps = key_ref.shape[0]
    tiles = [_route_head(q_ref, key_ref, hh) for hh in range(hps)]
    rows = hps * PEER_TOPK
    r0 = pl.multiple_of(pl.program_id(1) * rows, rows)
    e_scr[pl.ds(r0, rows), :] = jnp.concatenate([e for e, _ in tiles], axis=0)
    g_scr[pl.ds(r0, rows), :] = jnp.concatenate([g for _, g in tiles], axis=0)

    @pl.when(pl.program_id(1) == pl.num_programs(1) - 1)
    def _():
        e_ref[...] = e_scr[...].T.astype(jnp.int32)
        g_ref[...] = g_scr[...].T


def _peer_route(qp, keys):
    t = qp.shape[0]
    tt = min(PEER_TT, t)
    ne = PEER_HEADS * PEER_TOPK
    hps = ROUTE_HEADS_PER_STEP
    return pl.pallas_call(
        _peer_route_kernel,
        grid=(t // tt, PEER_HEADS // hps),
        in_specs=[pl.BlockSpec((tt, hps * 2 * PEER_KDIM), lambda i, h: (i, h)),
                  pl.BlockSpec((hps, 2, PEER_NKEYS, PEER_KDIM), lambda i, h: (h, 0, 0, 0))],
        out_specs=[pl.BlockSpec((tt, ne), lambda i, h: (i, 0)),
                   pl.BlockSpec((tt, ne), lambda i, h: (i, 0))],
        out_shape=[jax.ShapeDtypeStruct((t, ne), jnp.int32),
                   jax.ShapeDtypeStruct((t, ne), F32)],
        scratch_shapes=[pltpu.VMEM((ne, tt), F32), pltpu.VMEM((ne, tt), F32)],
        compiler_params=_cp("parallel", "arbitrary"),
        name="peer_route",
    )(qp, keys)


def _pack_tables_kernel(u_ref, v_ref, o_ref):
    def pack(x):
        lo = lax.bitcast_convert_type(x[:, 0:HALF_D].astype(BF16).astype(F32), jnp.int32)
        hi = lax.bitcast_convert_type(x[:, HALF_D:2 * HALF_D].astype(BF16).astype(F32), jnp.int32)
        return lax.shift_right_logical(lo, 16) | (hi & jnp.int32(-65536))

    o_ref[:, 0:HALF_D] = pack(u_ref[...])
    o_ref[:, HALF_D:2 * HALF_D] = pack(v_ref[...])


def _pack_tables(u_tabs, v_tabs, layer, tr=512):
    _, e, d = u_tabs.shape
    assert d == 2 * HALF_D
    spec_in = pl.BlockSpec((None, tr, d), lambda i: (layer, i, 0))
    spec = pl.BlockSpec((tr, d), lambda i: (i, 0))
    return pl.pallas_call(
        _pack_tables_kernel,
        grid=(e // tr,),
        in_specs=[spec_in, spec_in],
        out_specs=spec,
        out_shape=jax.ShapeDtypeStruct((e, d), jnp.int32),
        compiler_params=_cp("parallel"),
        name="peer_pack",
    )(u_tabs, v_tabs)


def _unpack_rows(wd):
    lo = lax.bitcast_convert_type(lax.shift_left(wd, 16), F32)
    hi = lax.bitcast_convert_type(lax.bitwise_and(wd, jnp.int32(-65536)), F32)
    return lo, hi


SC_WINDOWS = (16, 32)


def _sc_gather(table, idx, window):
    from jax.experimental.pallas import tpu_sc as plsc
    n = idx.shape[0]
    width = table.shape[1]
    mesh = plsc.VectorSubcoreMesh(core_axis_name="core", subcore_axis_name="subcore")

    @functools.partial(pl.kernel, out_type=jax.ShapeDtypeStruct((n, width), table.dtype), mesh=mesh)
    def gather(tab_hbm, idx_hbm, out_hbm):
        def body(idx_vmem, out_vmem):
            pltpu.sync_copy(tab_hbm.at[idx_vmem.at[0, pl.ds(0, window)]], out_vmem)

        pltpu.emit_pipeline(
            body,
            grid=(n // window,),
            in_specs=[pl.BlockSpec((1, LANES), lambda i: (0, i))],
            out_specs=[pl.BlockSpec((window, width), lambda i: (i, 0))],
            core_axis_name=("core", "subcore"),
            dimension_semantics=(pltpu.PARALLEL,),
            trace_scopes=False,
        )(idx_hbm, out_hbm)

    idx_pad = jnp.pad(idx.reshape(n // window, window), ((0, 0), (0, LANES - window)))
    return gather(table, idx_pad.reshape(1, (n // window) * LANES))


def _peer_combine_kernel(x_ref, g2_ref, rows_a_ref, rows_b_ref, gate_ref, o_ref):
    ne = PEER_HEADS * PEER_TOPK
    x = x_ref[...]
    ct = x.shape[0]
    xn = x * lax.rsqrt(jnp.mean(x * x, axis=-1, keepdims=True) + NORM_EPS) * g2_ref[...]
    gate_t = jnp.concatenate([gate_ref[...]] * (ne // ct), axis=0).T
    for jj in range(ct):
        rows_ref, j = (rows_a_ref, jj) if jj < ct // 2 else (rows_b_ref, jj - ct // 2)
        u_lo, u_hi = _unpack_rows(rows_ref[j * ne:(j + 1) * ne, 0:HALF_D])
        xr = xn[jj:jj + 1, :]
        h = jnp.sum(u_lo * xr[:, 0:HALF_D] + u_hi * xr[:, HALF_D:2 * HALF_D], axis=1, keepdims=True)
        act = 0.5 * h * (1.0 + lax.erf(h * (2.0 ** -0.5)))
        wgt = gate_t[:, jj:jj + 1] * act
        v_lo, v_hi = _unpack_rows(rows_ref[j * ne:(j + 1) * ne, HALF_D:2 * HALF_D])
        o_ref[jj:jj + 1, 0:HALF_D] = x[jj:jj + 1, 0:HALF_D] + jnp.sum(wgt * v_lo, axis=0, keepdims=True)
        o_ref[jj:jj + 1, HALF_D:2 * HALF_D] = (x[jj:jj + 1, HALF_D:2 * HALF_D]
                                               + jnp.sum(wgt * v_hi, axis=0, keepdims=True))


def _peer_combine(x, g2, rows, gates, first_token):
    t, d = x.shape
    ne = PEER_HEADS * PEER_TOPK
    ct = PEER_CT
    steps = rows.shape[0] // (ct * ne)
    off = first_token // ct
    return pl.pallas_call(
        _peer_combine_kernel,
        grid=(steps,),
        in_specs=[pl.BlockSpec((ct, d), lambda i: (off + i, 0)),
                  pl.BlockSpec((1, d), lambda i: (0, 0)),
                  pl.BlockSpec((ct * ne // 2, d), lambda i: (2 * i, 0)),
                  pl.BlockSpec((ct * ne // 2, d), lambda i: (2 * i + 1, 0)),
                  pl.BlockSpec((ct, ne), lambda i: (off + i, 0))],
        out_specs=pl.BlockSpec((ct, d), lambda i: (off + i, 0)),
        out_shape=jax.ShapeDtypeStruct((t, d), F32),
        input_output_aliases={0: 0},
        compiler_params=_cp("parallel"),
        name="peer_combine",
    )(x, g2.reshape(1, d), rows, rows, gates)


PEER_TOKENS_PER_GATHER = 2048


def _peer_route_stage(x, g2, wq_b, keys_b):
    t = x.shape[0]
    ne = PEER_HEADS * PEER_TOPK
    qp = _norm_matmul(x, g2, wq_b, out_dtype=BF16)
    e_tok, g_tok = _peer_route(qp, keys_b)
    return e_tok.reshape(t * ne), g_tok


def _peer_gather_stage(table, idx, t, gather_fn):
    ne = PEER_HEADS * PEER_TOPK
    tc = min(PEER_TOKENS_PER_GATHER, t)
    return [gather_fn(table, idx[c * tc * ne:(c + 1) * tc * ne], SC_WINDOWS[c % len(SC_WINDOWS)])
            for c in range(t // tc)]


def _peer_combine_stage(x, g2, rows_list, gates):
    tc = x.shape[0] // len(rows_list)
    for c, rows in enumerate(rows_list):
        x = _peer_combine(x, g2, rows, gates, c * tc)
    return x


def _peer(x, g2, wq, keys, u_tabs, v_tabs, layer, gather_fn):
    idx, gates = _peer_route_stage(x, g2, wq.astype(BF16), keys.astype(BF16))
    table = _pack_tables(u_tabs, v_tabs, layer)
    rows_list = _peer_gather_stage(table, idx, x.shape[0], gather_fn)
    return _peer_combine_stage(x, g2, rows_list, gates)


_IN_WIDTHS = (256, 256, 256, 256, 256, 256, 256, 4, 4, 256, 256, 128, 128, 128, 128, 128, 128, 12,
              256, 256, 256, 256)


def _dup_groups(wcols):
    g0, g1 = wcols[:, :HEAD_DIM], wcols[:, HEAD_DIM:]
    return jnp.concatenate([g0, g0, g1, g1], axis=1)


def _layout_w_in(w_in):
    offs = np.cumsum((0,) + _IN_WIDTHS)
    cols = [w_in[:, offs[i]:offs[i + 1]] for i in range(len(_IN_WIDTHS))]
    (hq, hf, hi, hg, mq, mk, mv, mi, mf, mo, nq, nkc, nvc, nks, nvs, nkw, nvw, ng, rq, rk, rv, rg) = cols
    d = w_in.shape[0]
    pad = lambda c, n: jnp.concatenate([c, jnp.zeros((d, n - c.shape[1]), w_in.dtype)], axis=1)
    main = jnp.concatenate([hq, hf, hi, hg, mq, mk, mv, mo, rq, rk, rv, rg,
                            _dup_groups(nks), _dup_groups(nkw), nq, nvs, nvw,
                            pad(jnp.concatenate([mi, mf], axis=1), LANES), pad(ng, LANES)], axis=1)
    assert main.shape[1] == N_MAIN
    kcvc = jnp.concatenate([nkc, nvc], axis=1)
    return main.astype(BF16), kcvc.astype(BF16)


def kernel(x, norm1_g, w_in, hgrn_lb, hgrn_onorm_g, mlstm_conv_w, mlstm_conv_b, mlstm_gate_b, mlstm_onorm_g, nsa_qnorm_g, nsa_knorm_g, nsa_cmp_pe, nsa_cmp_w, ret_onorm_g, w_up, w_gate, w_out, norm2_g, peer_wq, peer_keys, peer_u, peer_v):
    bsz, seq, d = x.shape
    t = bsz * seq
    depth = w_in.shape[0]
    cos_t, sin_t = _rope_lane_tables(seq)
    cos4, sin4 = jnp.tile(cos_t, (1, 2)), jnp.tile(sin_t, (1, 2))
    lb_cum = jnp.cumsum(jax.nn.softmax(hgrn_lb.astype(F32), axis=0), axis=0)
    lb_all = lb_cum - lb_cum[0:1]
    weights = []
    for l in range(depth):
        w_main, w_kcvc = _layout_w_in(w_in[l])
        weights.append(dict(
            main=w_main, kcvc=w_kcvc, gate=w_gate[l].astype(BF16), up=w_up[l].astype(BF16),
            out=w_out[l].astype(BF16), lb=_hgrn_lb_rows(lb_all[l]), wq=peer_wq[l].astype(BF16),
            keys=peer_keys[l].astype(BF16), table=_pack_tables(peer_u, peer_v, l)))

    def mixer_steps(xh, l, nb):
        wl = weights[l]
        st = {}

        def proj(dep):
            st["proj"] = _norm_matmul(xh, _after(norm1_g[l], dep), wl["main"])
            return st["proj"]

        def gates(dep):
            st["gates"] = _norm_matmul(xh, _after(norm1_g[l], dep), wl["gate"], act="sigmoid", out_dtype=BF16)
            return st["gates"]

        def hgrn(dep):
            st["oh"] = _hgrn(st["proj"], wl["lb"], _after(hgrn_onorm_g[l], dep), nb, seq)
            return st["oh"]

        def mlstm(dep):
            st["om"] = _mlstm(st["proj"], mlstm_conv_w[l], mlstm_conv_b[l], mlstm_gate_b[l],
                              _after(mlstm_onorm_g[l], dep), nb, seq)
            return st["om"]

        def ret(dep):
            st["or"] = _ret(st["proj"], cos4, sin4, _after(ret_onorm_g[l], dep), nb, seq)
            return st["or"]

        def nsa_front(dep):
            kcvc = _norm_matmul(xh, _after(norm1_g[l], dep), wl["kcvc"])
            qn, qr, ks, kw, vst, vwt = _nsa_prep(st["proj"], cos4, sin4, nsa_qnorm_g[l], nsa_knorm_g[l], nb, seq)
            o_cmp, sel = _nsa_cmp(kcvc, qn, nsa_cmp_pe[l], nsa_cmp_w[l], nsa_knorm_g[l][0], nb, seq)
            st["nsa"] = (qr, ks, kw, vst, vwt, sel, o_cmp)
            return o_cmp

        def nsa_attn(dep):
            del dep
            st["on"] = _nsa_attn(st["proj"], *st["nsa"], nb, seq)
            return st["on"]

        def merge(dep):
            del dep
            st["xm"] = _merge(xh, st["gates"], (st["oh"], st["om"], st["on"], st["or"]), wl["up"], wl["out"])
            return st["xm"]

        def route(dep):
            st["idx"], st["pgates"] = _peer_route_stage(st["xm"], _after(norm2_g[l], dep), wl["wq"], wl["keys"])
            return st["pgates"]

        return [proj, gates, hgrn, mlstm, ret, nsa_front, nsa_attn, merge, route], st

    combine_slots = (0, 8, 8, 8, 8, 8, 8, 8)

    def combine_steps(l, xm, rows_list, pgates):
        box = {"x": xm}
        tc = xm.shape[0] // len(rows_list)

        def make(c):
            def step(dep):
                box["x"] = _peer_combine(box["x"], _after(norm2_g[l], dep), rows_list[c], pgates, c * tc)
                return box["x"]
            return step

        return [make(c) for c in range(len(rows_list))], box

    n_groups = next(n for n in (8, 4, 2, 1) if bsz % n == 0)
    nb = bsz // n_groups
    xs = [x[g * nb:(g + 1) * nb].reshape(nb * seq, d) for g in range(n_groups)]
    dep = None
    lag = min(2, n_groups - 1)
    pending = []
    for l in range(depth):
        for g in range(n_groups):
            msteps, st = mixer_steps(xs[g], l, nb)
            due = pending.pop(0) if len(pending) == lag and lag > 0 else None
            csteps = due[1] if due is not None else []
            ci = 0
            for si, mstep in enumerate(msteps):
                dep = mstep(dep)
                while ci < len(csteps) and (ci >= len(combine_slots) or combine_slots[ci] <= si):
                    dep = csteps[ci](dep)
                    ci += 1
            for cstep in csteps[ci:]:
                dep = cstep(dep)
            if due is not None:
                xs[due[0]] = due[2]["x"]
            rows_list = _peer_gather_stage(weights[l]["table"], st["idx"], nb * seq, _sc_gather)
            csteps, box = combine_steps(l, st["xm"], rows_list, st["pgates"])
            pending.append((g, csteps, box))
            if lag == 0:
                for cstep in pending.pop(0)[1]:
                    dep = cstep(dep)
                xs[g] = box["x"]
    for pg, csteps, box in pending:
        for cstep in csteps:
            dep = cstep(dep)
        xs[pg] = box["x"]
    return jnp.concatenate(xs, axis=0).reshape(bsz, seq, d)
```

```python
import functools
import math

import numpy as np
import jax
import jax.numpy as jnp
from jax import lax
from jax.experimental import pallas as pl
from jax.experimental.pallas import tpu as pltpu

F32 = jnp.float32
BF16 = jnp.bfloat16

HEAD_DIM = 64
N_HEADS = 4
MIX_WIDTH = N_HEADS * HEAD_DIM
CHUNK = 64
NORM_EPS = 1e-6
NEG_INF = -1e30
ROPE_THETA = 10000.0
CONV_W = 4
NSA_GROUPS = 2
CMP_LEN = 32
CMP_STRIDE = 16
SEL_BLOCK = 64
SEL_TOPK = 16
WINDOW = 512
FORCE_BONUS = 1e3
PEER_HEADS = 8
PEER_NKEYS = 128
PEER_TOPK = 16
PEER_KDIM = 128

LANES = 128
VMEM_LIMIT = 48 * 1024 * 1024

OFF_H, OFF_M, OFF_R, OFF_KD, OFF_NQ, OFF_V, OFF_MG, OFF_NG = 0, 1024, 2048, 3072, 3584, 3840, 4096, 4224
N_MAIN = 4352


def _cp(*sem):
    return pltpu.CompilerParams(dimension_semantics=sem, vmem_limit_bytes=VMEM_LIMIT)


def _dot(a, b):
    return jnp.dot(a, b, preferred_element_type=F32)


def _dot_nt(a, b):
    return lax.dot_general(a, b, (((1,), (1,)), ((), ())), preferred_element_type=F32)


def _dot_tn(a, b):
    return lax.dot_general(a, b, (((0,), (0,)), ((), ())), preferred_element_type=F32)


def _split3(x):
    hi = x.astype(BF16)
    r1 = x - hi.astype(F32)
    mid = r1.astype(BF16)
    lo = (r1 - mid.astype(F32)).astype(BF16)
    return hi, mid, lo


def _dot01_l(m01, x):
    hi, mid, lo = _split3(x)
    return _dot(m01, hi) + _dot(m01, mid) + _dot(m01, lo)


def _dot01_r(x, m01):
    hi, mid, lo = _split3(x)
    return _dot(hi, m01) + _dot(mid, m01) + _dot(lo, m01)


def _head_of_lane(shape, axis):
    return lax.broadcasted_iota(jnp.int32, shape, axis) // HEAD_DIM


def _block_ones(n, dtype=BF16):
    r = lax.broadcasted_iota(jnp.int32, (n, n), 0) // HEAD_DIM
    c = lax.broadcasted_iota(jnp.int32, (n, n), 1) // HEAD_DIM
    return (r == c).astype(dtype)


def _group_sum(x, ones_bd):
    hi = x.astype(BF16)
    lo = (x - hi.astype(F32)).astype(BF16)
    return _dot(hi, ones_bd) + _dot(lo, ones_bd)


def _head_rms(x, gain, ones_bd):
    ms = _group_sum(x * x, ones_bd) * (1.0 / HEAD_DIM)
    return x * lax.rsqrt(ms + NORM_EPS) * gain


def _sigmoid(x):
    return 1.0 / (1.0 + jnp.exp(-x))


def _silu(x):
    return x * _sigmoid(x)


def _log_sigmoid(x):
    return jnp.minimum(x, 0.0) - jnp.log(1.0 + jnp.exp(-jnp.abs(x)))


def _stack_heads(x, n_heads=N_HEADS):
    hl = _head_of_lane(x.shape, 1)
    return jnp.concatenate([jnp.where(hl == h, x, jnp.zeros_like(x)) for h in range(n_heads)], axis=0)


def _unstack_heads(r, c, n_heads=N_HEADS):
    hl = _head_of_lane((c, r.shape[1]), 1)
    out = jnp.zeros((c, r.shape[1]), F32)
    for h in range(n_heads):
        out = jnp.where(hl == h, r[h * c:(h + 1) * c, :], out)
    return out


def _rope(x, cos_t, sin_t):
    n = x.shape[1]
    first = (lax.broadcasted_iota(jnp.int32, x.shape, 1) % HEAD_DIM) < (HEAD_DIM // 2)
    partner = jnp.where(first, pltpu.roll(x, n - HEAD_DIM // 2, 1), pltpu.roll(x, HEAD_DIM // 2, 1))
    return x * cos_t + partner * sin_t


def _after_kernel(a_ref, dep_ref, o_ref):
    del dep_ref
    o_ref[...] = a_ref[...]


def _after(a, dep):
    if dep is None:
        return a
    a2 = a.reshape(1, a.size)
    out = pl.pallas_call(
        _after_kernel,
        in_specs=[pl.BlockSpec(a2.shape, lambda: (0, 0)), pl.BlockSpec(memory_space=pl.ANY)],
        out_specs=pl.BlockSpec(a2.shape, lambda: (0, 0)),
        out_shape=jax.ShapeDtypeStruct(a2.shape, a2.dtype),
        name="order_after",
    )(a2, dep)
    return out.reshape(a.shape)


def _norm_matmul_kernel(x_ref, g_ref, w_ref, o_ref, xn_ref, *, act):
    @pl.when(pl.program_id(1) == 0)
    def _():
        x = x_ref[...]
        ms = jnp.mean(x * x, axis=-1, keepdims=True)
        xn_ref[...] = (x * lax.rsqrt(ms + NORM_EPS) * g_ref[...]).astype(BF16)

    y = _dot(xn_ref[...], w_ref[...])
    if act == "sigmoid":
        y = _sigmoid(y)
    o_ref[...] = y.astype(o_ref.dtype)


def _norm_matmul(x, g, w, *, act=None, out_dtype=F32, tm=1024, tn=2176):
    t, d = x.shape
    w3 = w if w.ndim == 3 else w[None]
    n_per = w3.shape[2]
    tm = min(tm, t)
    tn = next(c for c in (tn, 2048, 1024, 512, 256, 128) if n_per % c == 0)
    per = n_per // tn
    n = w3.shape[0] * n_per
    assert t % tm == 0
    return pl.pallas_call(
        functools.partial(_norm_matmul_kernel, act=act),
        grid=(t // tm, n // tn),
        in_specs=[pl.BlockSpec((tm, d), lambda i, j: (i, 0)),
                  pl.BlockSpec((1, d), lambda i, j: (0, 0)),
                  pl.BlockSpec((None, d, tn), lambda i, j: (j // per, 0, j % per))],
        out_specs=pl.BlockSpec((tm, tn), lambda i, j: (i, j)),
        out_shape=jax.ShapeDtypeStruct((t, n), out_dtype),
        scratch_shapes=[pltpu.VMEM((tm, d), BF16)],
        compiler_params=_cp("parallel", "arbitrary"),
        name="norm_matmul",
    )(x, g.reshape(1, d), w3)


def _merge_kernel(x_ref, gate_ref, oh_ref, om_ref, on_ref, or_ref, wup_ref, wout_ref, o_ref):
    d = x_ref.shape[1]
    acc = None
    for m, r in enumerate((oh_ref, om_ref, on_ref, or_ref)):
        up = _dot(r[...].astype(BF16), wup_ref[m])
        term = gate_ref[:, m * d:(m + 1) * d].astype(F32) * up
        acc = term if acc is None else acc + term
    o_ref[...] = x_ref[...] + _dot(acc.astype(BF16), wout_ref[...])


def _merge(x, gates, outs, w_up, w_out, tm=512):
    t, d = x.shape
    tm = min(tm, t)
    mix = pl.BlockSpec((tm, MIX_WIDTH), lambda i: (i, 0))
    return pl.pallas_call(
        _merge_kernel,
        grid=(t // tm,),
        in_specs=[pl.BlockSpec((tm, d), lambda i: (i, 0)),
                  pl.BlockSpec((tm, 4 * d), lambda i: (i, 0)),
                  mix, mix, mix, mix,
                  pl.BlockSpec((4, MIX_WIDTH, d), lambda i: (0, 0, 0)),
                  pl.BlockSpec((d, d), lambda i: (0, 0))],
        out_specs=pl.BlockSpec((tm, d), lambda i: (i, 0)),
        out_shape=jax.ShapeDtypeStruct((t, d), F32),
        compiler_params=_cp("parallel"),
        name="merge",
    )(x, gates, *outs, w_up, w_out)


REC_BLOCK = 256


def _chunk_consts():
    t = lax.broadcasted_iota(jnp.int32, (CHUNK, CHUNK), 0)
    s = lax.broadcasted_iota(jnp.int32, (CHUNK, CHUNK), 1)
    return t, s


def _hgrn_levels():
    t = np.arange(CHUNK)
    rows = []
    masks = []
    h = CHUNK // 2
    while h >= 1:
        ref = (t // (2 * h)) * (2 * h) + h
        p = np.zeros((CHUNK, CHUNK), np.float32)
        p[t, np.minimum(ref, CHUNK - 1)] = 1.0
        rows.append(p)
        same = (t[:, None] // (2 * h)) == (t[None, :] // (2 * h))
        m = same & ((t[:, None] // h) % 2 == 1) & ((t[None, :] // h) % 2 == 0)
        masks.append(m.astype(np.float32))
        h //= 2
    masks.append(np.eye(CHUNK, dtype=np.float32))
    return np.concatenate(rows, 0), np.stack(masks, 0)


def _hgrn_kernel(p_ref, lb_ref, g_ref, psel_ref, lmask_ref, o_ref, st_ref):
    @pl.when(pl.program_id(1) == 0)
    def _():
        st_ref[...] = jnp.zeros_like(st_ref)

    c = CHUNK
    w = MIX_WIDTH
    ones_bd = _block_ones(w)
    bd_mask = _block_ones(w, F32)
    tri = (lax.broadcasted_iota(jnp.int32, (c, c), 0) >= lax.broadcasted_iota(jnp.int32, (c, c), 1)).astype(BF16)
    psel = psel_ref[...]
    n_lv = lmask_ref.shape[0]
    log_lb, log_1mlb, one_mlb = lb_ref[0:1, :], lb_ref[1:2, :], lb_ref[2:3, :]
    gain = g_ref[...]

    def chunk(ci, carry):
        r0 = pl.multiple_of(ci * c, c)
        q = _silu(p_ref[pl.ds(r0, c), 0:w])
        fl = p_ref[pl.ds(r0, c), w:2 * w]
        v = p_ref[pl.ds(r0, c), 2 * w:3 * w]
        gp = p_ref[pl.ds(r0, c), 3 * w:4 * w]
        a1 = jnp.broadcast_to(log_lb, fl.shape)
        a2 = log_1mlb + _log_sigmoid(fl)
        mx = jnp.maximum(a1, a2)
        log_f = mx + jnp.log(jnp.exp(a1 - mx) + jnp.exp(a2 - mx))
        k = one_mlb * _sigmoid(-fl)
        b = _dot01_l(tri, log_f)
        bref = _dot01_l(psel, b)
        vb = v.astype(BF16)
        a = jnp.zeros((N_HEADS * c, c), F32)
        for lv in range(n_lv):
            if lv < n_lv - 1:
                br = bref[lv * c:(lv + 1) * c, :]
                qs = q * jnp.exp(jnp.minimum(b - br, 0.0))
                ks = k * jnp.exp(jnp.minimum(br - b, 0.0))
            else:
                qs, ks = q, k
            s_lv = _dot_nt(_stack_heads(qs).astype(BF16), ks.astype(BF16))
            a = a + jnp.concatenate([lmask_ref[lv]] * N_HEADS, axis=0) * s_lv
        o = _unstack_heads(_dot(a.astype(BF16), vb), c)
        st = st_ref[...]
        o = o + _dot_nt((q * jnp.exp(b)).astype(BF16), st.astype(BF16))
        b_last = b[c - 1:c, :]
        kb = k * jnp.exp(b_last - b)
        st_ref[...] = st * jnp.exp(b_last) + bd_mask * _dot_tn(vb, kb.astype(BF16))
        y = _head_rms(o, gain, ones_bd) * _silu(gp)
        o_ref[pl.ds(r0, c), :] = y
        return carry

    lax.fori_loop(0, p_ref.shape[0] // c, chunk, 0)


def _hgrn(proj, lb_rows, gain, bsz, seq):
    psel, lmask = _hgrn_levels()
    tb = min(REC_BLOCK, seq)
    nb = seq // tb
    return pl.pallas_call(
        _hgrn_kernel,
        grid=(bsz, nb),
        in_specs=[pl.BlockSpec((tb, 4 * MIX_WIDTH), lambda b, i: (b * nb + i, OFF_H // (4 * MIX_WIDTH))),
                  pl.BlockSpec((8, MIX_WIDTH), lambda b, i: (0, 0)),
                  pl.BlockSpec((1, MIX_WIDTH), lambda b, i: (0, 0)),
                  pl.BlockSpec(psel.shape, lambda b, i: (0, 0)),
                  pl.BlockSpec(lmask.shape, lambda b, i: (0, 0, 0))],
        out_specs=pl.BlockSpec((tb, MIX_WIDTH), lambda b, i: (b * nb + i, 0)),
        out_shape=jax.ShapeDtypeStruct((bsz * seq, MIX_WIDTH), F32),
        scratch_shapes=[pltpu.VMEM((MIX_WIDTH, MIX_WIDTH), F32)],
        compiler_params=_cp("parallel", "arbitrary"),
        name="hgrn2",
    )(proj, lb_rows, gain.reshape(1, MIX_WIDTH), jnp.asarray(psel, BF16), jnp.asarray(lmask, F32))


def _ret_kernel(p_ref, cos_ref, sin_ref, dec_ref, decin_ref, g_ref, o_ref, st_ref):
    @pl.when(pl.program_id(1) == 0)
    def _():
        st_ref[...] = jnp.zeros_like(st_ref)

    c = CHUNK
    w = MIX_WIDTH
    ones_bd = _block_ones(w)
    bd_mask = _block_ones(w, F32)
    gain = g_ref[...]
    dec_q = dec_ref[0:c, :]
    dec_k = dec_ref[c:2 * c, :]
    dec_state = dec_ref[2 * c:2 * c + 1, :]
    dec_in = decin_ref[...]

    def chunk(ci, carry):
        r0 = pl.multiple_of(ci * c, c)
        cos_t = cos_ref[pl.ds(r0, c), :]
        sin_t = sin_ref[pl.ds(r0, c), :]
        q = _rope(p_ref[pl.ds(r0, c), 0:w], cos_t, sin_t)
        k = _rope(p_ref[pl.ds(r0, c), w:2 * w], cos_t, sin_t) * (HEAD_DIM ** -0.5)
        v = p_ref[pl.ds(r0, c), 2 * w:3 * w]
        gp = p_ref[pl.ds(r0, c), 3 * w:4 * w]
        vb = v.astype(BF16)
        a = _dot_nt(_stack_heads(q).astype(BF16), k.astype(BF16)) * dec_in
        o = _unstack_heads(_dot(a.astype(BF16), vb), c)
        st = st_ref[...]
        o = o + _dot_nt(q.astype(BF16), st.astype(BF16)) * dec_q
        st_ref[...] = st * dec_state + bd_mask * _dot_tn(vb, (k * dec_k).astype(BF16))
        o_ref[pl.ds(r0, c), :] = _head_rms(o, gain, ones_bd) * _silu(gp)
        return carry

    lax.fori_loop(0, p_ref.shape[0] // c, chunk, 0)


def _ret_consts():
    log_gamma = np.log1p(-np.exp2(-5.0 - np.arange(N_HEADS, dtype=np.float64)))
    t = np.arange(CHUNK, dtype=np.float64)
    lane_h = np.arange(MIX_WIDTH) // HEAD_DIM
    dec_q = np.exp(log_gamma[lane_h][None, :] * (t[:, None] + 1.0))
    dec_k = np.exp(log_gamma[lane_h][None, :] * (CHUNK - 1.0 - t[:, None]))
    dec_state = np.exp(log_gamma[lane_h] * CHUNK)[None, :]
    dec = np.concatenate([dec_q, dec_k, np.broadcast_to(dec_state, (8, MIX_WIDTH))], 0)
    diff = t[:, None] - t[None, :]
    dec_in = np.concatenate([np.where(diff >= 0, np.exp(log_gamma[h] * diff), 0.0) for h in range(N_HEADS)], 0)
    return dec.astype(np.float32), dec_in.astype(np.float32)


def _ret(proj, cos4, sin4, gain, bsz, seq):
    dec, dec_in = _ret_consts()
    tb = min(REC_BLOCK, seq)
    nb = seq // tb
    return pl.pallas_call(
        _ret_kernel,
        grid=(bsz, nb),
        in_specs=[pl.BlockSpec((tb, 4 * MIX_WIDTH), lambda b, i: (b * nb + i, OFF_R // (4 * MIX_WIDTH))),
                  pl.BlockSpec((tb, MIX_WIDTH), lambda b, i: (i, 0)),
                  pl.BlockSpec((tb, MIX_WIDTH), lambda b, i: (i, 0)),
                  pl.BlockSpec(dec.shape, lambda b, i: (0, 0)),
                  pl.BlockSpec(dec_in.shape, lambda b, i: (0, 0)),
                  pl.BlockSpec((1, MIX_WIDTH), lambda b, i: (0, 0))],
        out_specs=pl.BlockSpec((tb, MIX_WIDTH), lambda b, i: (b * nb + i, 0)),
        out_shape=jax.ShapeDtypeStruct((bsz * seq, MIX_WIDTH), F32),
        scratch_shapes=[pltpu.VMEM((MIX_WIDTH, MIX_WIDTH), F32)],
        compiler_params=_cp("parallel", "arbitrary"),
        name="retention",
    )(proj, cos4, sin4, jnp.asarray(dec), jnp.asarray(dec_in), gain.reshape(1, MIX_WIDTH))


def _hgrn_lb_rows(lb):
    lb = lb.astype(F32)
    rows = jnp.stack([jnp.log(lb), jnp.log1p(-lb), 1.0 - lb], 0)
    return jnp.concatenate([rows, jnp.zeros((5, lb.shape[0]), F32)], 0)


def _rope_lane_tables(seq):
    inv = 1.0 / (ROPE_THETA ** (jnp.arange(0, HEAD_DIM, 2, dtype=F32) / HEAD_DIM))
    ang = jnp.arange(seq, dtype=F32)[:, None] * inv[None, :]
    cos, sin = jnp.cos(ang), jnp.sin(ang)
    cos_t = jnp.tile(cos, (1, LANES // (HEAD_DIM // 2)))
    sin_t = jnp.tile(jnp.concatenate([-sin, sin], axis=1), (1, LANES // HEAD_DIM))
    return cos_t, sin_t


def _expand_heads(cols, shape):
    hl = _head_of_lane(shape, 1)
    out = jnp.broadcast_to(cols[-1], shape)
    for h in range(len(cols) - 2, -1, -1):
        out = jnp.where(hl == h, jnp.broadcast_to(cols[h], shape), out)
    return out


def _mlstm_kernel(p_ref, gcol_ref, grow_ref, cw_ref, cb_ref, gbr_ref, gbc_ref, g_ref, o_ref,
                  ct_ref, n_ref, m_ref, hist_ref, cbuf_ref, qk_ref):
    c = CHUNK
    w = MIX_WIDTH
    tb = p_ref.shape[0]

    @pl.when(pl.program_id(1) == 0)
    def _():
        ct_ref[...] = jnp.zeros_like(ct_ref)
        n_ref[...] = jnp.zeros_like(n_ref)
        m_ref[...] = jnp.zeros_like(m_ref)
        hist_ref[...] = jnp.zeros_like(hist_ref)

    cbuf_ref[0:8, :] = hist_ref[...]
    cbuf_ref[8:, :] = p_ref[:, 0:2 * w]
    hist_ref[...] = p_ref[tb - 8:tb, 0:2 * w]
    acc = jnp.broadcast_to(cb_ref[...], (tb, 2 * w))
    for j in range(CONV_W):
        acc = acc + cw_ref[j:j + 1, :] * cbuf_ref[pl.ds(8 - (CONV_W - 1) + j, tb), :]
    qk_ref[...] = _silu(acc)

    ones_bd = _block_ones(w)
    bd_mask = _block_ones(w, F32)
    ti = lax.broadcasted_iota(jnp.int32, (c, c), 0)
    si = lax.broadcasted_iota(jnp.int32, (c, c), 1)
    causal = ti >= si
    tri = causal.astype(BF16)
    tri_t = (ti <= si).astype(BF16)
    gain = g_ref[...]
    ones_ext = jnp.ones((c, LANES), BF16)

    def chunk(ci, carry):
        r0 = pl.multiple_of(ci * c, c)
        q = qk_ref[pl.ds(r0, c), 0:w]
        k = qk_ref[pl.ds(r0, c), w:2 * w] * (HEAD_DIM ** -0.5)
        v = p_ref[pl.ds(r0, c), 2 * w:3 * w]
        op = p_ref[pl.ds(r0, c), 3 * w:4 * w]
        gc = gcol_ref[pl.ds(r0, c), :] + gbr_ref[...]
        gr = grow_ref[ci] + gbc_ref[...]
        b_c = _dot01_l(tri, _log_sigmoid(gc))
        b_r = _dot01_r(_log_sigmoid(gr), tri_t)
        wd, s_inter, em, wk, decay = [], [], [], [], []
        for h in range(N_HEADS):
            bc = b_c[:, N_HEADS + h:N_HEADS + h + 1]
            lic = gc[:, h:h + 1]
            br = b_r[N_HEADS + h:N_HEADS + h + 1, :]
            lir = gr[h:h + 1, :]
            dmat = jnp.where(causal, bc - br + lir, -jnp.inf)
            m_prev = m_ref[h:h + 1, 0:1]
            inter = bc + m_prev
            mrow = jnp.maximum(inter, jnp.max(dmat, axis=1, keepdims=True))
            wd.append(jnp.exp(dmat - mrow))
            s_inter.append(jnp.exp(inter - mrow))
            em.append(jnp.exp(-mrow))
            b_last = br[:, c - 1:c]
            m_new = jnp.maximum(b_last + m_prev, jnp.max(b_last - br + lir, axis=1, keepdims=True))
            wk.append(jnp.exp(b_last - bc + lic - m_new))
            decay.append(jnp.exp(b_last + m_prev - m_new))
            m_ref[h:h + 1, :] = jnp.broadcast_to(m_new, (1, LANES))
        s_inter_l = _expand_heads(s_inter, (c, w))
        em_l = _expand_heads(em, (c, w))
        wk_l = _expand_heads(wk, (c, w))
        decay_l = _expand_heads(decay, (1, w))
        qk = _dot_nt(_stack_heads(q).astype(BF16), k.astype(BF16))
        wmat = jnp.concatenate(wd, axis=0) * qk
        vb = v.astype(BF16)
        r = _dot(wmat.astype(BF16), jnp.concatenate([vb, ones_ext], axis=1))
        num_intra = _unstack_heads(r[:, 0:w], c)
        rs_l = _expand_heads([r[h * c:(h + 1) * c, w:w + 1] for h in range(N_HEADS)], (c, w))
        ct = ct_ref[...]
        nrow = n_ref[0:1, :]
        num = s_inter_l * _dot_nt(q.astype(BF16), ct.astype(BF16)) + num_intra
        den = s_inter_l * _group_sum(q * nrow, ones_bd) + rs_l
        hval = num / jnp.maximum(jnp.abs(den), em_l)
        kw = wk_l * k
        ct_ref[...] = ct * decay_l + bd_mask * _dot_tn(vb, kw.astype(BF16))
        n_ref[0:1, :] = nrow * decay_l + jnp.sum(kw, axis=0, keepdims=True)
        o_ref[pl.ds(r0, c), :] = _head_rms(hval, gain, ones_bd) * _sigmoid(op)
        return carry

    lax.fori_loop(0, tb // c, chunk, 0)


def _mlstm(proj, conv_w, conv_b, gate_b, gain, bsz, seq):
    t = bsz * seq
    w = MIX_WIDTH
    tb = min(REC_BLOCK, seq)
    nb = seq // tb
    ncb = tb // CHUNK
    grow = proj[:, OFF_MG:OFF_MG + 8].reshape(t // CHUNK, CHUNK, 8).transpose(0, 2, 1)
    gb_row = jnp.zeros((1, LANES), F32).at[0, 0:8].set(gate_b.astype(F32))
    gb_col = gate_b.astype(F32).reshape(8, 1)
    return pl.pallas_call(
        _mlstm_kernel,
        grid=(bsz, nb),
        in_specs=[pl.BlockSpec((tb, 4 * w), lambda b, i: (b * nb + i, OFF_M // (4 * w))),
                  pl.BlockSpec((tb, LANES), lambda b, i: (b * nb + i, OFF_MG // LANES)),
                  pl.BlockSpec((ncb, 8, CHUNK), lambda b, i: (b * nb + i, 0, 0)),
                  pl.BlockSpec((CONV_W, 2 * w), lambda b, i: (0, 0)),
                  pl.BlockSpec((1, 2 * w), lambda b, i: (0, 0)),
                  pl.BlockSpec((1, LANES), lambda b, i: (0, 0)),
                  pl.BlockSpec((8, 1), lambda b, i: (0, 0)),
                  pl.BlockSpec((1, w), lambda b, i: (0, 0))],
        out_specs=pl.BlockSpec((tb, w), lambda b, i: (b * nb + i, 0)),
        out_shape=jax.ShapeDtypeStruct((t, w), F32),
        scratch_shapes=[pltpu.VMEM((w, w), F32), pltpu.VMEM((8, w), F32), pltpu.VMEM((8, LANES), F32),
                        pltpu.VMEM((8, 2 * w), F32), pltpu.VMEM((tb + 8, 2 * w), F32),
                        pltpu.VMEM((tb, 2 * w), F32)],
        compiler_params=_cp("parallel", "arbitrary"),
        name="mlstm",
    )(proj, proj, grow, conv_w.astype(F32), conv_b.astype(F32).reshape(1, 2 * w), gb_row, gb_col,
      gain.reshape(1, w))


NSA_TQ = 128
NSA_KC = 512
GW = 2 * HEAD_DIM


def _nsa_prep_kernel(pq_ref, pk_ref, pv_ref, cos_ref, sin_ref, qg_ref, kg_ref,
                     qn_ref, qr_ref, ks_ref, kw_ref, vst_ref, vwt_ref):
    w = MIX_WIDTH
    for src, dst in ((pv_ref[:, 0:GW], vst_ref), (pv_ref[:, GW:2 * GW], vwt_ref)):
        vt = src.T
        tk = dst.shape[4]
        for g in range(NSA_GROUPS):
            rows = vt[g * HEAD_DIM:(g + 1) * HEAD_DIM, :]
            dup = jnp.concatenate([rows, rows], axis=0).astype(BF16)
            for j in range(dst.shape[2]):
                dst[0, g, j] = dup[:, j * tk:(j + 1) * tk]
    ones_bd = _block_ones(w)
    cos_t, sin_t = cos_ref[...], sin_ref[...]
    scale = HEAD_DIM ** -0.5
    qh = _head_rms(pq_ref[...], qg_ref[...], ones_bd)
    qn_ref[...] = (qh * scale).astype(BF16)
    qr_ref[...] = (_rope(qh, cos_t, sin_t) * scale).astype(BF16)
    ks_ref[...] = _rope(_head_rms(pk_ref[:, 0:w], kg_ref[1:2, :], ones_bd), cos_t, sin_t).astype(BF16)
    kw_ref[...] = _rope(_head_rms(pk_ref[:, w:2 * w], kg_ref[2:3, :], ones_bd), cos_t, sin_t).astype(BF16)


def _nsa_prep(proj, cos4, sin4, qnorm_g, knorm_g, bsz, seq):
    t = bsz * seq
    w = MIX_WIDTH
    tm = min(NSA_KC, seq)
    tq = min(NSA_TQ, seq)
    ns = seq // tm
    qg = jnp.tile(qnorm_g.astype(F32), w // HEAD_DIM).reshape(1, w)
    kg = jnp.concatenate([jnp.tile(knorm_g.astype(F32), (1, w // HEAD_DIM)), jnp.zeros((5, w), F32)], axis=0)
    out = jax.ShapeDtypeStruct((t, w), BF16)
    row = pl.BlockSpec((tm, w), lambda i: (i, 0))
    return pl.pallas_call(
        _nsa_prep_kernel,
        grid=(t // tm,),
        in_specs=[pl.BlockSpec((tm, w), lambda i: (i, OFF_NQ // w)),
                  pl.BlockSpec((tm, 2 * w), lambda i: (i, OFF_KD // (2 * w))),
                  pl.BlockSpec((tm, 2 * GW), lambda i: (i, OFF_V // (2 * GW))),
                  pl.BlockSpec((tm, w), lambda i: (i % ns, 0)),
                  pl.BlockSpec((tm, w), lambda i: (i % ns, 0)),
                  pl.BlockSpec((1, w), lambda i: (0, 0)),
                  pl.BlockSpec((8, w), lambda i: (0, 0))],
        out_specs=[row, row, row, row,
                   pl.BlockSpec((1, NSA_GROUPS, 1, GW, tm), lambda i: (i // ns, 0, i % ns, 0, 0)),
                   pl.BlockSpec((1, NSA_GROUPS, tm // tq, GW, tq), lambda i: (i // ns, 0, i % ns, 0, 0))],
        out_shape=[out, out, out, out,
                   jax.ShapeDtypeStruct((bsz, NSA_GROUPS, seq // tm, GW, tm), BF16),
                   jax.ShapeDtypeStruct((bsz, NSA_GROUPS, seq // tq, GW, tq), BF16)],
        compiler_params=_cp("parallel"),
        name="nsa_prep",
    )(proj, proj, proj, cos4, sin4, qg, kg)


def _nsa_cmp_kernel(xr_ref, pe_ref, w0_ref, w1_ref, kg_ref, ovt_ref, qn_ref, ocmp_ref, sel_ref,
                    kc_ref, vc_ref, v_ref, *, n_top):
    tq = qn_ref.shape[0]
    nr = xr_ref.shape[0]
    nsel = sel_ref.shape[2]
    w = MIX_WIDTH

    @pl.when(pl.program_id(1) == 0)
    def _():
        xr = xr_ref[...]
        y0 = _dot((xr + pe_ref[0]).astype(BF16), w0_ref[...])
        y1 = _dot((xr + pe_ref[1]).astype(BF16), w1_ref[...])
        kv = y0 + pltpu.roll(y1, nr - 1, 0)
        kc_ref[...] = _head_rms(kv[:, 0:w], kg_ref[...], _block_ones(w)).astype(BF16)
        vc_ref[...] = kv[:, w:2 * w].astype(BF16)

    pos0 = pl.program_id(1) * tq
    hl = _head_of_lane((tq, GW), 1)
    pos_r = pos0 + lax.broadcasted_iota(jnp.int32, (tq, nr), 0)
    valid = lax.broadcasted_iota(jnp.int32, (tq, nr), 1) * CMP_STRIDE + (CMP_LEN - 1) <= pos_r
    pos_c = pos0 + lax.broadcasted_iota(jnp.int32, (nr, tq), 1)
    valid_t = lax.broadcasted_iota(jnp.int32, (nr, tq), 0) * CMP_STRIDE + (CMP_LEN - 1) <= pos_c
    jrow = lax.broadcasted_iota(jnp.int32, (nsel, tq), 0)
    cur = (pos0 + lax.broadcasted_iota(jnp.int32, (nsel, tq), 1)) // SEL_BLOCK
    forced = (jrow == 0) | (jrow == cur) | (jrow == cur - 1)
    ovt = ovt_ref[...]

    for g in range(NSA_GROUPS):
        qg = qn_ref[:, g * GW:(g + 1) * GW]
        kg = kc_ref[:, g * GW:(g + 1) * GW]
        vg = vc_ref[:, g * GW:(g + 1) * GW]
        o_g = jnp.zeros((tq, GW), F32)
        pt_sum = jnp.zeros((nr, tq), F32)
        for hh in range(2):
            qm = jnp.where(hl == hh, qg, jnp.zeros_like(qg))
            s = jnp.where(valid, _dot_nt(qm, kg), NEG_INF)
            e = jnp.exp(s - jnp.max(s, axis=1, keepdims=True))
            p = jnp.where(valid, e / jnp.sum(e, axis=1, keepdims=True), 0.0)
            o_g = jnp.where(hl == hh, _dot(p.astype(BF16), vg), o_g)
            st = jnp.where(valid_t, _dot_nt(kg, qm), NEG_INF)
            et = jnp.exp(st - jnp.max(st, axis=0, keepdims=True))
            pt_sum = pt_sum + jnp.where(valid_t, et / jnp.sum(et, axis=0, keepdims=True), 0.0)
        ocmp_ref[:, g * GW:(g + 1) * GW] = o_g
        p_hi = pt_sum.astype(BF16)
        p_lo = (pt_sum - p_hi.astype(F32)).astype(BF16)
        imp = _dot(ovt, p_hi) + _dot(ovt, p_lo)
        val = jnp.where(jrow <= cur, imp + FORCE_BONUS * forced.astype(F32), NEG_INF)
        v_ref[...] = val

        def rank(jp, cnt):
            row = v_ref[pl.ds(jp, 1), :]
            tie = jnp.where(jrow > jp, 1.0, 0.0)
            return cnt + jnp.where(row > val, 1.0, jnp.where(row == val, tie, 0.0))

        cnt = lax.fori_loop(0, nsel, rank, jnp.zeros((nsel, tq), F32))
        sel_ref[0, g] = ((cnt < n_top) & (jrow <= cur)).astype(F32)


def _nsa_cmp_weights(cmp_pe, cmp_w):
    half = CMP_LEN // 2
    wl = cmp_w.astype(F32).reshape(2, 2, half, HEAD_DIM, HEAD_DIM)
    eye2 = jnp.eye(2, dtype=F32)
    w2 = jnp.einsum('kardz,kK,gG,h->arkgdKGhz', wl, eye2, eye2, jnp.ones((2,), F32))
    w2 = w2.reshape(2, half * 4 * HEAD_DIM, 8 * HEAD_DIM)
    pl_ = cmp_pe.astype(F32).reshape(2, 2, half, HEAD_DIM)
    pe2 = jnp.broadcast_to(pl_.transpose(1, 2, 0, 3)[:, :, :, None, :], (2, half, 2, 2, HEAD_DIM))
    return w2.astype(BF16), pe2.reshape(2, 1, half * 4 * HEAD_DIM)


def _nsa_cmp(kcvc, qn, cmp_pe, cmp_w, knorm0, bsz, seq, tq=512):
    t = bsz * seq
    w = MIX_WIDTH
    tq = min(tq, seq)
    nq = seq // tq
    nr = seq // CMP_STRIDE
    nsel = seq // SEL_BLOCK
    n_top = min(SEL_TOPK, nsel)
    w2, pe2 = _nsa_cmp_weights(cmp_pe, cmp_w)
    xr = kcvc.reshape(t // CMP_STRIDE, CMP_STRIDE * w)
    kg = jnp.tile(knorm0.astype(F32), w // HEAD_DIM).reshape(1, w)
    n_i = np.arange(nr)[:, None] * CMP_STRIDE
    j_i = np.arange(nsel)[None, :] * SEL_BLOCK
    ov = ((n_i < j_i + SEL_BLOCK) & (n_i + CMP_LEN > j_i)).astype(np.float32)
    ov[nr - 1, :] = 0.0
    kin = CMP_STRIDE * w
    return pl.pallas_call(
        functools.partial(_nsa_cmp_kernel, n_top=n_top),
        grid=(bsz, nq),
        in_specs=[pl.BlockSpec((nr, kin), lambda b, i: (b, 0)),
                  pl.BlockSpec((2, 1, kin), lambda b, i: (0, 0, 0)),
                  pl.BlockSpec((None, kin, 2 * w), lambda b, i: (0, 0, 0)),
                  pl.BlockSpec((None, kin, 2 * w), lambda b, i: (1, 0, 0)),
                  pl.BlockSpec((1, w), lambda b, i: (0, 0)),
                  pl.BlockSpec((nsel, nr), lambda b, i: (0, 0)),
                  pl.BlockSpec((tq, w), lambda b, i: (b * nq + i, 0))],
        out_specs=[pl.BlockSpec((tq, w), lambda b, i: (b * nq + i, 0)),
                   pl.BlockSpec((1, NSA_GROUPS, nsel, tq), lambda b, i: (b, 0, 0, i))],
        out_shape=[jax.ShapeDtypeStruct((t, w), F32),
                   jax.ShapeDtypeStruct((bsz, NSA_GROUPS, nsel, seq), F32)],
        scratch_shapes=[pltpu.VMEM((nr, w), BF16), pltpu.VMEM((nr, w), BF16), pltpu.VMEM((nsel, tq), F32)],
        compiler_params=_cp("parallel", "arbitrary"),
        name="nsa_cmp",
    )(xr, pe2, w2, w2, kg, jnp.asarray(ov.T, BF16), qn)


def _nsa_attn_kernel(qr_ref, ks_ref, kw_ref, vs_ref, vw_ref, sel_ref, ocmp_ref, gate_ref, o_ref, *, kc, wt):
    tq = qr_ref.shape[0]
    i = pl.program_id(1)
    hl = _head_of_lane((tq, GW), 1)
    nbk = kc // SEL_BLOCK
    groups = range(NSA_GROUPS)

    def stacked_q(g):
        q = qr_ref[:, g * GW:(g + 1) * GW]
        return jnp.concatenate([jnp.where(hl == 0, q, jnp.zeros_like(q)), jnp.where(hl == 1, q, jnp.zeros_like(q))],
                               axis=0)

    qs_all = [stacked_q(g) for g in groups]

    def lane_qpos(rows):
        return i * tq + lax.broadcasted_iota(jnp.int32, (rows, 2 * tq), 1) % tq

    def finish(acc, l):
        ot = (acc / l).T
        return jnp.where(hl == 0, ot[0:tq, :], ot[tq:2 * tq, :])

    qpos_s = lane_qpos(kc)
    krow_s = lax.broadcasted_iota(jnp.int32, (kc, 2 * tq), 0)

    def sel_step(g, c, carry, diagonal):
        m, l, acc = carry
        k0 = pl.multiple_of(c * kc, kc)
        st = _dot_nt(ks_ref[pl.ds(k0, kc), g * GW:(g + 1) * GW], qs_all[g])
        srows = sel_ref[0, g, pl.ds(pl.multiple_of(c * nbk, nbk), nbk), :]
        srows = jnp.concatenate([srows, srows], axis=1)
        smask = jnp.concatenate([jnp.broadcast_to(srows[r:r + 1, :], (SEL_BLOCK, 2 * tq)) for r in range(nbk)],
                                axis=0)
        msk = smask > 0.5
        if diagonal:
            msk = msk & (k0 + krow_s <= qpos_s)
        st = jnp.where(msk, st, NEG_INF)
        m_new = jnp.maximum(m, jnp.max(st, axis=0, keepdims=True))
        p = jnp.exp(st - m_new)
        alpha = jnp.exp(m - m_new)
        l = l * alpha + jnp.sum(p, axis=0, keepdims=True)
        acc = acc * alpha + _dot(vs_ref[0, g, c], p.astype(BF16))
        return m_new, l, acc

    def sel_body(c, carries, diagonal):
        return tuple(sel_step(g, c, carries[g], diagonal) for g in groups)

    init = (jnp.full((1, 2 * tq), NEG_INF, F32), jnp.zeros((1, 2 * tq), F32), jnp.zeros((GW, 2 * tq), F32))
    n_before = (i * tq) // kc
    carries = lax.fori_loop(0, n_before, functools.partial(sel_body, diagonal=False), (init,) * NSA_GROUPS)
    carries = sel_body(n_before, carries, True)

    j0 = jnp.maximum(i - (wt - 1), 0)
    k0 = pl.multiple_of(j0 * tq, tq)
    span = wt * tq
    kpos = k0 + lax.broadcasted_iota(jnp.int32, (span, 2 * tq), 0)
    qpos_w = lane_qpos(span)
    wmask = (kpos <= qpos_w) & (kpos > qpos_w - WINDOW)
    gb = _sigmoid(gate_ref[...])
    for g in groups:
        _, l_s, acc_s = carries[g]
        o_sel = finish(acc_s, l_s)
        st = jnp.where(wmask, _dot_nt(kw_ref[pl.ds(k0, span), g * GW:(g + 1) * GW], qs_all[g]), NEG_INF)
        p = jnp.exp(st - jnp.max(st, axis=0, keepdims=True))
        vt = jnp.concatenate([vw_ref[0, g, j0 + r] for r in range(wt)], axis=1)
        o_win = finish(_dot(vt, p.astype(BF16)), jnp.sum(p, axis=0, keepdims=True))

        def gate(branch):
            cols = [gb[:, (2 * g + hh) * 3 + branch:(2 * g + hh) * 3 + branch + 1] for hh in range(2)]
            return _expand_heads(cols, (tq, GW))

        o_ref[:, g * GW:(g + 1) * GW] = (gate(0) * ocmp_ref[:, g * GW:(g + 1) * GW]
                                         + gate(1) * o_sel + gate(2) * o_win)


def _nsa_attn(proj, qr, ks, kw, vst, vwt, sel, o_cmp, bsz, seq):
    t = bsz * seq
    w = MIX_WIDTH
    tq = min(NSA_TQ, seq)
    nq = seq // tq
    nsel = seq // SEL_BLOCK
    kc = min(NSA_KC, seq)
    wt = min(WINDOW // tq + 1, nq)
    kspec = pl.BlockSpec((seq, w), lambda b, i: (b, 0))
    return pl.pallas_call(
        functools.partial(_nsa_attn_kernel, kc=kc, wt=wt),
        grid=(bsz, nq),
        in_specs=[pl.BlockSpec((tq, w), lambda b, i: (b * nq + i, 0)),
                  kspec, kspec,
                  pl.BlockSpec((1, NSA_GROUPS, seq // kc, GW, kc), lambda b, i: (b, 0, 0, 0, 0)),
                  pl.BlockSpec((1, NSA_GROUPS, nq, GW, tq), lambda b, i: (b, 0, 0, 0, 0)),
                  pl.BlockSpec((1, NSA_GROUPS, nsel, tq), lambda b, i: (b, 0, 0, i)),
                  pl.BlockSpec((tq, w), lambda b, i: (b * nq + i, 0)),
                  pl.BlockSpec((tq, LANES), lambda b, i: (b * nq + i, OFF_NG // LANES))],
        out_specs=pl.BlockSpec((tq, w), lambda b, i: (b * nq + i, 0)),
        out_shape=jax.ShapeDtypeStruct((t, w), F32),
        compiler_params=_cp("parallel", "arbitrary"),
        name="nsa_attn",
    )(qr, ks, kw, vst, vwt, sel, o_cmp, proj)


def _nsa(proj, kcvc, cos4, sin4, qnorm_g, knorm_g, cmp_pe, cmp_w, bsz, seq):
    qn, qr, ks, kw, vst, vwt = _nsa_prep(proj, cos4, sin4, qnorm_g, knorm_g, bsz, seq)
    o_cmp, sel = _nsa_cmp(kcvc, qn, cmp_pe, cmp_w, knorm_g[0], bsz, seq)
    return _nsa_attn(proj, qr, ks, kw, vst, vwt, sel, o_cmp, bsz, seq)


PEER_TT = 128
PEER_CT = 16
HALF_D = 512


SUBLANES = 8
CODE_BITS = 127
FAR_BELOW = -3.0e38


def _with_code(x, code):
    bits = lax.bitcast_convert_type(x, jnp.int32)
    return lax.bitcast_convert_type((bits & ~CODE_BITS) | code, F32)


def _split_code(x):
    bits = lax.bitcast_convert_type(x, jnp.int32)
    return lax.bitcast_convert_type(bits & ~CODE_BITS, F32), bits & CODE_BITS


def _sort16_desc(xs):
    xs = list(xs)
    n = len(xs)
    k = 2
    while k <= n:
        j = k // 2
        while j >= 1:
            for i in range(n):
                l = i ^ j
                if l > i:
                    hi, lo = jnp.maximum(xs[i], xs[l]), jnp.minimum(xs[i], xs[l])
                    xs[i], xs[l] = (hi, lo) if (i & k) == 0 else (lo, hi)
            j //= 2
        k *= 2
    return xs


def _merge16_desc(xs):
    xs = list(xs)
    j = len(xs) // 2
    while j >= 1:
        for i in range(len(xs)):
            l = i ^ j
            if l > i:
                xs[i], xs[l] = jnp.maximum(xs[i], xs[l]), jnp.minimum(xs[i], xs[l])
        j //= 2
    return xs


def _top16_columns(x):
    n = PEER_TOPK
    xs = _sort16_desc([x[SUBLANES * j:SUBLANES * (j + 1), :] for j in range(n)])
    shift = SUBLANES // 2
    while shift >= 1:
        rolled = [pltpu.roll(a, shift, 0) for a in xs]
        xs = _merge16_desc([jnp.maximum(xs[i], rolled[n - 1 - i]) for i in range(n)])
        shift //= 2
    return xs


_PEER_CAND_TILES = ((0, 0, 8), (0, 1, 8), (1, 0, 8), (2, 0, 5), (3, 0, 4), (4, 0, 3), (5, 0, 2), (6, 0, 2), (7, 0, 2))


ROUTE_HEADS_PER_STEP = 4


def _route_head(q_ref, key_ref, hh):
    tt = q_ref.shape[0]
    nk = PEER_NKEYS
    n = PEER_TOPK
    row = lax.broadcasted_iota(jnp.int32, (nk, tt), 0)
    sub = lax.broadcasted_iota(jnp.int32, (SUBLANES, tt), 0)
    vals, ids = [], []
    for p in range(2):
        c0 = (2 * hh + p) * PEER_KDIM
        st = _dot_nt(key_ref[hh, p], q_ref[:, c0:c0 + PEER_KDIM])
        top = [_split_code(a) for a in _top16_columns(_with_code(st, (nk - 1) - row))]
        vals.append([v for v, _ in top])
        ids.append([(nk - 1) - c for _, c in top])
    (v1, v2), (i1, i2) = vals, ids

    def stack(xs, lo):
        out = xs[lo]
        for s in range(1, SUBLANES):
            out = jnp.where(sub == s, xs[lo + s], out)
        return out

    v2t, i2t = (stack(v2, 0), stack(v2, SUBLANES)), (stack(i2, 0), stack(i2, SUBLANES))
    cand, cexp = [], []
    for a, tile, nvalid in _PEER_CAND_TILES:
        v = v1[a] + v2t[tile]
        cand.append(v if nvalid == SUBLANES else jnp.where(sub < nvalid, v, FAR_BELOW))
        cexp.append(i1[a] * nk + i2t[tile])
    cand.append(stack(v1, SUBLANES) + v2[0])
    cexp.append(stack(i1, SUBLANES) * nk + i2[0])
    n_tiles = len(cand)
    slot_code = [(nk - 1) - (c * SUBLANES + sub) for c in range(n_tiles)]
    coded = [_with_code(v, sc) for v, sc in zip(cand, slot_code)]
    coded += [jnp.full((SUBLANES, tt), FAR_BELOW, F32)] * (n - n_tiles)
    top = [_split_code(a) for a in _top16_columns(jnp.concatenate(coded, axis=0))]
    call = jnp.concatenate(cexp, axis=0)
    slot = (nk - 1) - lax.broadcasted_iota(jnp.int32, call.shape, 0)
    ex = [jnp.exp(v - top[0][0]) for v, _ in top]
    tot = ex[0]
    for k in range(1, n):
        tot = tot + ex[k]
    krow = lax.broadcasted_iota(jnp.int32, (n, tt), 0)
    e_tile = jnp.zeros((n, tt), F32)
    g_tile = jnp.zeros((n, tt), F32)
    for k in range(n):
        hit = slot == jnp.concatenate([top[k][1]] * n_tiles, axis=0)
        e_k = jnp.sum(jnp.where(hit, call, 0), axis=0, keepdims=True)
        e_tile = jnp.where(krow == k, e_k.astype(F32), e_tile)
        g_tile = jnp.where(krow == k, (ex[k] / tot)[0:1, :], g_tile)
    return e_tile, g_tile


def _peer_route_kernel(q_ref, key_ref, e_ref, g_ref, e_scr, g_scr):
    hps = key_ref.shape[0]
    tiles = [_route_head(q_ref, key_ref, hh) for hh in range(hps)]
    rows = hps * PEER_TOPK
    r0 = pl.multiple_of(pl.program_id(1) * rows, rows)
    e_scr[pl.ds(r0, rows), :] = jnp.concatenate([e for e, _ in tiles], axis=0)
    g_scr[pl.ds(r0, rows), :] = jnp.concatenate([g for _, g in tiles], axis=0)

    @pl.when(pl.program_id(1) == pl.num_programs(1) - 1)
    def _():
        e_ref[...] = e_scr[...].T.astype(jnp.int32)
        g_ref[...] = g_scr[...].T


def _peer_route(qp, keys):
    t = qp.shape[0]
    tt = min(PEER_TT, t)
    ne = PEER_HEADS * PEER_TOPK
    hps = ROUTE_HEADS_PER_STEP
    return pl.pallas_call(
        _peer_route_kernel,
        grid=(t // tt, PEER_HEADS // hps),
        in_specs=[pl.BlockSpec((tt, hps * 2 * PEER_KDIM), lambda i, h: (i, h)),
                  pl.BlockSpec((hps, 2, PEER_NKEYS, PEER_KDIM), lambda i, h: (h, 0, 0, 0))],
        out_specs=[pl.BlockSpec((tt, ne), lambda i, h: (i, 0)),
                   pl.BlockSpec((tt, ne), lambda i, h: (i, 0))],
        out_shape=[jax.ShapeDtypeStruct((t, ne), jnp.int32),
                   jax.ShapeDtypeStruct((t, ne), F32)],
        scratch_shapes=[pltpu.VMEM((ne, tt), F32), pltpu.VMEM((ne, tt), F32)],
        compiler_params=_cp("parallel", "arbitrary"),
        name="peer_route",
    )(qp, keys)


def _pack_tables_kernel(u_ref, v_ref, o_ref):
    def pack(x):
        lo = lax.bitcast_convert_type(x[:, 0:HALF_D].astype(BF16).astype(F32), jnp.int32)
        hi = lax.bitcast_convert_type(x[:, HALF_D:2 * HALF_D].astype(BF16).astype(F32), jnp.int32)
        return lax.shift_right_logical(lo, 16) | (hi & jnp.int32(-65536))

    o_ref[:, 0:HALF_D] = pack(u_ref[...])
    o_ref[:, HALF_D:2 * HALF_D] = pack(v_ref[...])


def _pack_tables(u_tabs, v_tabs, layer, tr=512):
    _, e, d = u_tabs.shape
    assert d == 2 * HALF_D
    spec_in = pl.BlockSpec((None, tr, d), lambda i: (layer, i, 0))
    spec = pl.BlockSpec((tr, d), lambda i: (i, 0))
    return pl.pallas_call(
        _pack_tables_kernel,
        grid=(e // tr,),
        in_specs=[spec_in, spec_in],
        out_specs=spec,
        out_shape=jax.ShapeDtypeStruct((e, d), jnp.int32),
        compiler_params=_cp("parallel"),
        name="peer_pack",
    )(u_tabs, v_tabs)


def _unpack_rows(wd):
    lo = lax.bitcast_convert_type(lax.shift_left(wd, 16), F32)
    hi = lax.bitcast_convert_type(lax.bitwise_and(wd, jnp.int32(-65536)), F32)
    return lo, hi


SC_WINDOWS = (16,)


def _sc_gather(table, idx, window):
    from jax.experimental.pallas import tpu_sc as plsc
    n = idx.shape[0]
    width = table.shape[1]
    mesh = plsc.VectorSubcoreMesh(core_axis_name="core", subcore_axis_name="subcore")

    @functools.partial(pl.kernel, out_type=jax.ShapeDtypeStruct((n, width), table.dtype), mesh=mesh)
    def gather(tab_hbm, idx_hbm, out_hbm):
        def body(idx_vmem, out_vmem):
            pltpu.sync_copy(tab_hbm.at[idx_vmem.at[0, pl.ds(0, window)]], out_vmem)

        pltpu.emit_pipeline(
            body,
            grid=(n // window,),
            in_specs=[pl.BlockSpec((1, LANES), lambda i: (0, i))],
            out_specs=[pl.BlockSpec((window, width), lambda i: (i, 0))],
            core_axis_name=("core", "subcore"),
            dimension_semantics=(pltpu.PARALLEL,),
            trace_scopes=False,
        )(idx_hbm, out_hbm)

    idx_pad = jnp.pad(idx.reshape(n // window, window), ((0, 0), (0, LANES - window)))
    return gather(table, idx_pad.reshape(1, (n // window) * LANES))


def _peer_combine_kernel(x_ref, g2_ref, rows_a_ref, rows_b_ref, gate_ref, o_ref):
    ne = PEER_HEADS * PEER_TOPK
    x = x_ref[...]
    ct = x.shape[0]
    xn = x * lax.rsqrt(jnp.mean(x * x, axis=-1, keepdims=True) + NORM_EPS) * g2_ref[...]
    gate_t = jnp.concatenate([gate_ref[...]] * (ne // ct), axis=0).T
    for jj in range(ct):
        rows_ref, j = (rows_a_ref, jj) if jj < ct // 2 else (rows_b_ref, jj - ct // 2)
        u_lo, u_hi = _unpack_rows(rows_ref[j * ne:(j + 1) * ne, 0:HALF_D])
        xr = xn[jj:jj + 1, :]
        h = jnp.sum(u_lo * xr[:, 0:HALF_D] + u_hi * xr[:, HALF_D:2 * HALF_D], axis=1, keepdims=True)
        act = 0.5 * h * (1.0 + lax.erf(h * (2.0 ** -0.5)))
        wgt = gate_t[:, jj:jj + 1] * act
        v_lo, v_hi = _unpack_rows(rows_ref[j * ne:(j + 1) * ne, HALF_D:2 * HALF_D])
        o_ref[jj:jj + 1, 0:HALF_D] = x[jj:jj + 1, 0:HALF_D] + jnp.sum(wgt * v_lo, axis=0, keepdims=True)
        o_ref[jj:jj + 1, HALF_D:2 * HALF_D] = (x[jj:jj + 1, HALF_D:2 * HALF_D]
                                               + jnp.sum(wgt * v_hi, axis=0, keepdims=True))


def _peer_combine(x, g2, rows, gates, first_token):
    t, d = x.shape
    ne = PEER_HEADS * PEER_TOPK
    ct = PEER_CT
    steps = rows.shape[0] // (ct * ne)
    off = first_token // ct
    return pl.pallas_call(
        _peer_combine_kernel,
        grid=(steps,),
        in_specs=[pl.BlockSpec((ct, d), lambda i: (off + i, 0)),
                  pl.BlockSpec((1, d), lambda i: (0, 0)),
                  pl.BlockSpec((ct * ne // 2, d), lambda i: (2 * i, 0)),
                  pl.BlockSpec((ct * ne // 2, d), lambda i: (2 * i + 1, 0)),
                  pl.BlockSpec((ct, ne), lambda i: (off + i, 0))],
        out_specs=pl.BlockSpec((ct, d), lambda i: (off + i, 0)),
        out_shape=jax.ShapeDtypeStruct((t, d), F32),
        input_output_aliases={0: 0},
        compiler_params=_cp("parallel"),
        name="peer_combine",
    )(x, g2.reshape(1, d), rows, rows, gates)


PEER_TOKENS_PER_GATHER = 2048


def _peer_route_stage(x, g2, wq_b, keys_b):
    t = x.shape[0]
    ne = PEER_HEADS * PEER_TOPK
    qp = _norm_matmul(x, g2, wq_b, out_dtype=BF16)
    e_tok, g_tok = _peer_route(qp, keys_b)
    return e_tok.reshape(t * ne), g_tok


def _peer_gather_stage(table, idx, t, gather_fn):
    ne = PEER_HEADS * PEER_TOPK
    tc = min(PEER_TOKENS_PER_GATHER, t)
    return [gather_fn(table, idx[c * tc * ne:(c + 1) * tc * ne], SC_WINDOWS[c % len(SC_WINDOWS)])
            for c in range(t // tc)]


def _peer_combine_stage(x, g2, rows_list, gates):
    tc = x.shape[0] // len(rows_list)
    for c, rows in enumerate(rows_list):
        x = _peer_combine(x, g2, rows, gates, c * tc)
    return x


def _peer(x, g2, wq, keys, u_tabs, v_tabs, layer, gather_fn):
    idx, gates = _peer_route_stage(x, g2, wq.astype(BF16), keys.astype(BF16))
    table = _pack_tables(u_tabs, v_tabs, layer)
    rows_list = _peer_gather_stage(table, idx, x.shape[0], gather_fn)
    return _peer_combine_stage(x, g2, rows_list, gates)


_IN_WIDTHS = (256, 256, 256, 256, 256, 256, 256, 4, 4, 256, 256, 128, 128, 128, 128, 128, 128, 12,
              256, 256, 256, 256)


def _dup_groups(wcols):
    g0, g1 = wcols[:, :HEAD_DIM], wcols[:, HEAD_DIM:]
    return jnp.concatenate([g0, g0, g1, g1], axis=1)


def _layout_w_in(w_in):
    offs = np.cumsum((0,) + _IN_WIDTHS)
    cols = [w_in[:, offs[i]:offs[i + 1]] for i in range(len(_IN_WIDTHS))]
    (hq, hf, hi, hg, mq, mk, mv, mi, mf, mo, nq, nkc, nvc, nks, nvs, nkw, nvw, ng, rq, rk, rv, rg) = cols
    d = w_in.shape[0]
    pad = lambda c, n: jnp.concatenate([c, jnp.zeros((d, n - c.shape[1]), w_in.dtype)], axis=1)
    main = jnp.concatenate([hq, hf, hi, hg, mq, mk, mv, mo, rq, rk, rv, rg,
                            _dup_groups(nks), _dup_groups(nkw), nq, nvs, nvw,
                            pad(jnp.concatenate([mi, mf], axis=1), LANES), pad(ng, LANES)], axis=1)
    assert main.shape[1] == N_MAIN
    kcvc = jnp.concatenate([nkc, nvc], axis=1)
    return main.astype(BF16), kcvc.astype(BF16)


def kernel(x, norm1_g, w_in, hgrn_lb, hgrn_onorm_g, mlstm_conv_w, mlstm_conv_b, mlstm_gate_b, mlstm_onorm_g, nsa_qnorm_g, nsa_knorm_g, nsa_cmp_pe, nsa_cmp_w, ret_onorm_g, w_up, w_gate, w_out, norm2_g, peer_wq, peer_keys, peer_u, peer_v):
    bsz, seq, d = x.shape
    t = bsz * seq
    depth = w_in.shape[0]
    cos_t, sin_t = _rope_lane_tables(seq)
    cos4, sin4 = jnp.tile(cos_t, (1, 2)), jnp.tile(sin_t, (1, 2))
    lb_cum = jnp.cumsum(jax.nn.softmax(hgrn_lb.astype(F32), axis=0), axis=0)
    lb_all = lb_cum - lb_cum[0:1]
    weights = []
    for l in range(depth):
        w_main, w_kcvc = _layout_w_in(w_in[l])
        weights.append(dict(
            main=w_main, kcvc=w_kcvc, gate=w_gate[l].astype(BF16), up=w_up[l].astype(BF16),
            out=w_out[l].astype(BF16), lb=_hgrn_lb_rows(lb_all[l]), wq=peer_wq[l].astype(BF16),
            keys=peer_keys[l].astype(BF16), table=_pack_tables(peer_u, peer_v, l)))

    def mixer_steps(xh, l, nb):
        wl = weights[l]
        st = {}

        def proj(dep):
            st["proj"] = _norm_matmul(xh, _after(norm1_g[l], dep), wl["main"])
            return st["proj"]

        def gates(dep):
            st["gates"] = _norm_matmul(xh, _after(norm1_g[l], dep), wl["gate"], act="sigmoid", out_dtype=BF16)
            return st["gates"]

        def hgrn(dep):
            st["oh"] = _hgrn(st["proj"], wl["lb"], _after(hgrn_onorm_g[l], dep), nb, seq)
            return st["oh"]

        def mlstm(dep):
            st["om"] = _mlstm(st["proj"], mlstm_conv_w[l], mlstm_conv_b[l], mlstm_gate_b[l],
                              _after(mlstm_onorm_g[l], dep), nb, seq)
            return st["om"]

        def ret(dep):
            st["or"] = _ret(st["proj"], cos4, sin4, _after(ret_onorm_g[l], dep), nb, seq)
            return st["or"]

        def nsa_front(dep):
            kcvc = _norm_matmul(xh, _after(norm1_g[l], dep), wl["kcvc"])
            qn, qr, ks, kw, vst, vwt = _nsa_prep(st["proj"], cos4, sin4, nsa_qnorm_g[l], nsa_knorm_g[l], nb, seq)
            o_cmp, sel = _nsa_cmp(kcvc, qn, nsa_cmp_pe[l], nsa_cmp_w[l], nsa_knorm_g[l][0], nb, seq)
            st["nsa"] = (qr, ks, kw, vst, vwt, sel, o_cmp)
            return o_cmp

        def nsa_attn(dep):
            del dep
            st["on"] = _nsa_attn(st["proj"], *st["nsa"], nb, seq)
            return st["on"]

        def merge(dep):
            del dep
            st["xm"] = _merge(xh, st["gates"], (st["oh"], st["om"], st["on"], st["or"]), wl["up"], wl["out"])
            return st["xm"]

        def route(dep):
            st["idx"], st["pgates"] = _peer_route_stage(st["xm"], _after(norm2_g[l], dep), wl["wq"], wl["keys"])
            return st["pgates"]

        return [proj, gates, hgrn, mlstm, ret, nsa_front, nsa_attn, merge, route], st

    combine_slots = (0, 8, 8, 8, 8, 8, 8, 8)

    def combine_steps(l, xm, rows_list, pgates):
        box = {"x": xm}
        tc = xm.shape[0] // len(rows_list)

        def make(c):
            def step(dep):
                box["x"] = _peer_combine(box["x"], _after(norm2_g[l], dep), rows_list[c], pgates, c * tc)
                return box["x"]
            return step

        return [make(c) for c in range(len(rows_list))], box

    n_groups = next(n for n in (8, 4, 2, 1) if bsz % n == 0)
    nb = bsz // n_groups
    xs = [x[g * nb:(g + 1) * nb].reshape(nb * seq, d) for g in range(n_groups)]
    dep = None
    lag = min(2, n_groups - 1)
    pending = []
    for l in range(depth):
        for g in range(n_groups):
            msteps, st = mixer_steps(xs[g], l, nb)
            due = pending.pop(0) if len(pending) == lag and lag > 0 else None
            csteps = due[1] if due is not None else []
            ci = 0
            for si, mstep in enumerate(msteps):
                dep = mstep(dep)
                while ci < len(csteps) and (ci >= len(combine_slots) or combine_slots[ci] <= si):
                    dep = csteps[ci](dep)
                    ci += 1
            for cstep in csteps[ci:]:
                dep = cstep(dep)
            if due is not None:
                xs[due[0]] = due[2]["x"]
            rows_list = _peer_gather_stage(weights[l]["table"], st["idx"], nb * seq, _sc_gather)
            csteps, box = combine_steps(l, st["xm"], rows_list, st["pgates"])
            pending.append((g, csteps, box))
            if lag == 0:
                for cstep in pending.pop(0)[1]:
                    dep = cstep(dep)
                xs[g] = box["x"]
    for pg, csteps, box in pending:
        for cstep in csteps:
            dep = cstep(dep)
        xs[pg] = box["x"]
    return jnp.concatenate(xs, axis=0).reshape(bsz, seq, d)
```

```python
import functools
import math

import numpy as np
import jax
import jax.numpy as jnp
from jax import lax
from jax.experimental import pallas as pl
from jax.experimental.pallas import tpu as pltpu

F32 = jnp.float32
BF16 = jnp.bfloat16

HEAD_DIM = 64
N_HEADS = 4
MIX_WIDTH = N_HEADS * HEAD_DIM
CHUNK = 64
NORM_EPS = 1e-6
NEG_INF = -1e30
ROPE_THETA = 10000.0
CONV_W = 4
NSA_GROUPS = 2
CMP_LEN = 32
CMP_STRIDE = 16
SEL_BLOCK = 64
SEL_TOPK = 16
WINDOW = 512
FORCE_BONUS = 1e3
PEER_HEADS = 8
PEER_NKEYS = 128
PEER_TOPK = 16
PEER_KDIM = 128

LANES = 128
VMEM_LIMIT = 48 * 1024 * 1024

OFF_H, OFF_M, OFF_R, OFF_KD, OFF_NQ, OFF_V, OFF_MG, OFF_NG = 0, 1024, 2048, 3072, 3584, 3840, 4096, 4224
N_MAIN = 4352


def _cp(*sem):
    return pltpu.CompilerParams(dimension_semantics=sem, vmem_limit_bytes=VMEM_LIMIT)


def _dot(a, b):
    return jnp.dot(a, b, preferred_element_type=F32)


def _dot_nt(a, b):
    return lax.dot_general(a, b, (((1,), (1,)), ((), ())), preferred_element_type=F32)


def _dot_tn(a, b):
    return lax.dot_general(a, b, (((0,), (0,)), ((), ())), preferred_element_type=F32)


def _split3(x):
    hi = x.astype(BF16)
    r1 = x - hi.astype(F32)
    mid = r1.astype(BF16)
    lo = (r1 - mid.astype(F32)).astype(BF16)
    return hi, mid, lo


def _dot01_l(m01, x):
    hi, mid, lo = _split3(x)
    return _dot(m01, hi) + _dot(m01, mid) + _dot(m01, lo)


def _dot01_r(x, m01):
    hi, mid, lo = _split3(x)
    return _dot(hi, m01) + _dot(mid, m01) + _dot(lo, m01)


def _head_of_lane(shape, axis):
    return lax.broadcasted_iota(jnp.int32, shape, axis) // HEAD_DIM


def _block_ones(n, dtype=BF16):
    r = lax.broadcasted_iota(jnp.int32, (n, n), 0) // HEAD_DIM
    c = lax.broadcasted_iota(jnp.int32, (n, n), 1) // HEAD_DIM
    return (r == c).astype(dtype)


def _group_sum(x, ones_bd):
    hi = x.astype(BF16)
    lo = (x - hi.astype(F32)).astype(BF16)
    return _dot(hi, ones_bd) + _dot(lo, ones_bd)


def _head_rms(x, gain, ones_bd):
    ms = _group_sum(x * x, ones_bd) * (1.0 / HEAD_DIM)
    return x * lax.rsqrt(ms + NORM_EPS) * gain


def _sigmoid(x):
    return 1.0 / (1.0 + jnp.exp(-x))


def _silu(x):
    return x * _sigmoid(x)


def _log_sigmoid(x):
    return jnp.minimum(x, 0.0) - jnp.log(1.0 + jnp.exp(-jnp.abs(x)))


def _stack_heads(x, n_heads=N_HEADS):
    hl = _head_of_lane(x.shape, 1)
    return jnp.concatenate([jnp.where(hl == h, x, jnp.zeros_like(x)) for h in range(n_heads)], axis=0)


def _unstack_heads(r, c, n_heads=N_HEADS):
    hl = _head_of_lane((c, r.shape[1]), 1)
    out = jnp.zeros((c, r.shape[1]), F32)
    for h in range(n_heads):
        out = jnp.where(hl == h, r[h * c:(h + 1) * c, :], out)
    return out


def _rope(x, cos_t, sin_t):
    n = x.shape[1]
    first = (lax.broadcasted_iota(jnp.int32, x.shape, 1) % HEAD_DIM) < (HEAD_DIM // 2)
    partner = jnp.where(first, pltpu.roll(x, n - HEAD_DIM // 2, 1), pltpu.roll(x, HEAD_DIM // 2, 1))
    return x * cos_t + partner * sin_t


def _after_kernel(a_ref, dep_ref, o_ref):
    del dep_ref
    o_ref[...] = a_ref[...]


def _after(a, dep):
    if dep is None:
        return a
    a2 = a.reshape(1, a.size)
    out = pl.pallas_call(
        _after_kernel,
        in_specs=[pl.BlockSpec(a2.shape, lambda: (0, 0)), pl.BlockSpec(memory_space=pl.ANY)],
        out_specs=pl.BlockSpec(a2.shape, lambda: (0, 0)),
        out_shape=jax.ShapeDtypeStruct(a2.shape, a2.dtype),
        name="order_after",
    )(a2, dep)
    return out.reshape(a.shape)


def _norm_matmul_kernel(x_ref, g_ref, w_ref, o_ref, xn_ref, *, act):
    @pl.when(pl.program_id(1) == 0)
    def _():
        x = x_ref[...]
        ms = jnp.mean(x * x, axis=-1, keepdims=True)
        xn_ref[...] = (x * lax.rsqrt(ms + NORM_EPS) * g_ref[...]).astype(BF16)

    y = _dot(xn_ref[...], w_ref[...])
    if act == "sigmoid":
        y = _sigmoid(y)
    o_ref[...] = y.astype(o_ref.dtype)


def _norm_matmul(x, g, w, *, act=None, out_dtype=F32, tm=1024, tn=2176):
    t, d = x.shape
    w3 = w if w.ndim == 3 else w[None]
    n_per = w3.shape[2]
    tm = min(tm, t)
    tn = next(c for c in (tn, 2048, 1024, 512, 256, 128) if n_per % c == 0)
    per = n_per // tn
    n = w3.shape[0] * n_per
    assert t % tm == 0
    return pl.pallas_call(
        functools.partial(_norm_matmul_kernel, act=act),
        grid=(t // tm, n // tn),
        in_specs=[pl.BlockSpec((tm, d), lambda i, j: (i, 0)),
                  pl.BlockSpec((1, d), lambda i, j: (0, 0)),
                  pl.BlockSpec((None, d, tn), lambda i, j: (j // per, 0, j % per))],
        out_specs=pl.BlockSpec((tm, tn), lambda i, j: (i, j)),
        out_shape=jax.ShapeDtypeStruct((t, n), out_dtype),
        scratch_shapes=[pltpu.VMEM((tm, d), BF16)],
        compiler_params=_cp("parallel", "arbitrary"),
        name="norm_matmul",
    )(x, g.reshape(1, d), w3)


def _merge_kernel(x_ref, gate_ref, oh_ref, om_ref, on_ref, or_ref, wup_ref, wout_ref, o_ref):
    d = x_ref.shape[1]
    acc = None
    for m, r in enumerate((oh_ref, om_ref, on_ref, or_ref)):
        up = _dot(r[...].astype(BF16), wup_ref[m])
        term = gate_ref[:, m * d:(m + 1) * d].astype(F32) * up
        acc = term if acc is None else acc + term
    o_ref[...] = x_ref[...] + _dot(acc.astype(BF16), wout_ref[...])


def _merge(x, gates, outs, w_up, w_out, tm=512):
    t, d = x.shape
    tm = min(tm, t)
    mix = pl.BlockSpec((tm, MIX_WIDTH), lambda i: (i, 0))
    return pl.pallas_call(
        _merge_kernel,
        grid=(t // tm,),
        in_specs=[pl.BlockSpec((tm, d), lambda i: (i, 0)),
                  pl.BlockSpec((tm, 4 * d), lambda i: (i, 0)),
                  mix, mix, mix, mix,
                  pl.BlockSpec((4, MIX_WIDTH, d), lambda i: (0, 0, 0)),
                  pl.BlockSpec((d, d), lambda i: (0, 0))],
        out_specs=pl.BlockSpec((tm, d), lambda i: (i, 0)),
        out_shape=jax.ShapeDtypeStruct((t, d), F32),
        compiler_params=_cp("parallel"),
        name="merge",
    )(x, gates, *outs, w_up, w_out)


REC_BLOCK = 256


def _chunk_consts():
    t = lax.broadcasted_iota(jnp.int32, (CHUNK, CHUNK), 0)
    s = lax.broadcasted_iota(jnp.int32, (CHUNK, CHUNK), 1)
    return t, s


def _hgrn_levels():
    t = np.arange(CHUNK)
    rows = []
    masks = []
    h = CHUNK // 2
    while h >= 1:
        ref = (t // (2 * h)) * (2 * h) + h
        p = np.zeros((CHUNK, CHUNK), np.float32)
        p[t, np.minimum(ref, CHUNK - 1)] = 1.0
        rows.append(p)
        same = (t[:, None] // (2 * h)) == (t[None, :] // (2 * h))
        m = same & ((t[:, None] // h) % 2 == 1) & ((t[None, :] // h) % 2 == 0)
        masks.append(m.astype(np.float32))
        h //= 2
    masks.append(np.eye(CHUNK, dtype=np.float32))
    return np.concatenate(rows, 0), np.stack(masks, 0)


def _hgrn_kernel(p_ref, lb_ref, g_ref, psel_ref, lmask_ref, o_ref, st_ref):
    @pl.when(pl.program_id(1) == 0)
    def _():
        st_ref[...] = jnp.zeros_like(st_ref)

    c = CHUNK
    w = MIX_WIDTH
    ones_bd = _block_ones(w)
    bd_mask = _block_ones(w, F32)
    tri = (lax.broadcasted_iota(jnp.int32, (c, c), 0) >= lax.broadcasted_iota(jnp.int32, (c, c), 1)).astype(BF16)
    psel = psel_ref[...]
    n_lv = lmask_ref.shape[0]
    log_lb, log_1mlb, one_mlb = lb_ref[0:1, :], lb_ref[1:2, :], lb_ref[2:3, :]
    gain = g_ref[...]

    def chunk(ci, carry):
        r0 = pl.multiple_of(ci * c, c)
        q = _silu(p_ref[pl.ds(r0, c), 0:w])
        fl = p_ref[pl.ds(r0, c), w:2 * w]
        v = p_ref[pl.ds(r0, c), 2 * w:3 * w]
        gp = p_ref[pl.ds(r0, c), 3 * w:4 * w]
        a1 = jnp.broadcast_to(log_lb, fl.shape)
        a2 = log_1mlb + _log_sigmoid(fl)
        mx = jnp.maximum(a1, a2)
        log_f = mx + jnp.log(jnp.exp(a1 - mx) + jnp.exp(a2 - mx))
        k = one_mlb * _sigmoid(-fl)
        b = _dot01_l(tri, log_f)
        bref = _dot01_l(psel, b)
        vb = v.astype(BF16)
        a = jnp.zeros((N_HEADS * c, c), F32)
        for lv in range(n_lv):
            if lv < n_lv - 1:
                br = bref[lv * c:(lv + 1) * c, :]
                qs = q * jnp.exp(jnp.minimum(b - br, 0.0))
                ks = k * jnp.exp(jnp.minimum(br - b, 0.0))
            else:
                qs, ks = q, k
            s_lv = _dot_nt(_stack_heads(qs).astype(BF16), ks.astype(BF16))
            a = a + jnp.concatenate([lmask_ref[lv]] * N_HEADS, axis=0) * s_lv
        o = _unstack_heads(_dot(a.astype(BF16), vb), c)
        st = st_ref[...]
        o = o + _dot_nt((q * jnp.exp(b)).astype(BF16), st.astype(BF16))
        b_last = b[c - 1:c, :]
        kb = k * jnp.exp(b_last - b)
        st_ref[...] = st * jnp.exp(b_last) + bd_mask * _dot_tn(vb, kb.astype(BF16))
        y = _head_rms(o, gain, ones_bd) * _silu(gp)
        o_ref[pl.ds(r0, c), :] = y
        return carry

    lax.fori_loop(0, p_ref.shape[0] // c, chunk, 0)


def _hgrn(proj, lb_rows, gain, bsz, seq):
    psel, lmask = _hgrn_levels()
    tb = min(REC_BLOCK, seq)
    nb = seq // tb
    return pl.pallas_call(
        _hgrn_kernel,
        grid=(bsz, nb),
        in_specs=[pl.BlockSpec((tb, 4 * MIX_WIDTH), lambda b, i: (b * nb + i, OFF_H // (4 * MIX_WIDTH))),
                  pl.BlockSpec((8, MIX_WIDTH), lambda b, i: (0, 0)),
                  pl.BlockSpec((1, MIX_WIDTH), lambda b, i: (0, 0)),
                  pl.BlockSpec(psel.shape, lambda b, i: (0, 0)),
                  pl.BlockSpec(lmask.shape, lambda b, i: (0, 0, 0))],
        out_specs=pl.BlockSpec((tb, MIX_WIDTH), lambda b, i: (b * nb + i, 0)),
        out_shape=jax.ShapeDtypeStruct((bsz * seq, MIX_WIDTH), F32),
        scratch_shapes=[pltpu.VMEM((MIX_WIDTH, MIX_WIDTH), F32)],
        compiler_params=_cp("parallel", "arbitrary"),
        name="hgrn2",
    )(proj, lb_rows, gain.reshape(1, MIX_WIDTH), jnp.asarray(psel, BF16), jnp.asarray(lmask, F32))


def _ret_kernel(p_ref, cos_ref, sin_ref, dec_ref, decin_ref, g_ref, o_ref, st_ref):
    @pl.when(pl.program_id(1) == 0)
    def _():
        st_ref[...] = jnp.zeros_like(st_ref)

    c = CHUNK
    w = MIX_WIDTH
    ones_bd = _block_ones(w)
    bd_mask = _block_ones(w, F32)
    gain = g_ref[...]
    dec_q = dec_ref[0:c, :]
    dec_k = dec_ref[c:2 * c, :]
    dec_state = dec_ref[2 * c:2 * c + 1, :]
    dec_in = decin_ref[...]

    def chunk(ci, carry):
        r0 = pl.multiple_of(ci * c, c)
        cos_t = cos_ref[pl.ds(r0, c), :]
        sin_t = sin_ref[pl.ds(r0, c), :]
        q = _rope(p_ref[pl.ds(r0, c), 0:w], cos_t, sin_t)
        k = _rope(p_ref[pl.ds(r0, c), w:2 * w], cos_t, sin_t) * (HEAD_DIM ** -0.5)
        v = p_ref[pl.ds(r0, c), 2 * w:3 * w]
        gp = p_ref[pl.ds(r0, c), 3 * w:4 * w]
        vb = v.astype(BF16)
        a = _dot_nt(_stack_heads(q).astype(BF16), k.astype(BF16)) * dec_in
        o = _unstack_heads(_dot(a.astype(BF16), vb), c)
        st = st_ref[...]
        o = o + _dot_nt(q.astype(BF16), st.astype(BF16)) * dec_q
        st_ref[...] = st * dec_state + bd_mask * _dot_tn(vb, (k * dec_k).astype(BF16))
        o_ref[pl.ds(r0, c), :] = _head_rms(o, gain, ones_bd) * _silu(gp)
        return carry

    lax.fori_loop(0, p_ref.shape[0] // c, chunk, 0)


def _ret_consts():
    log_gamma = np.log1p(-np.exp2(-5.0 - np.arange(N_HEADS, dtype=np.float64)))
    t = np.arange(CHUNK, dtype=np.float64)
    lane_h = np.arange(MIX_WIDTH) // HEAD_DIM
    dec_q = np.exp(log_gamma[lane_h][None, :] * (t[:, None] + 1.0))
    dec_k = np.exp(log_gamma[lane_h][None, :] * (CHUNK - 1.0 - t[:, None]))
    dec_state = np.exp(log_gamma[lane_h] * CHUNK)[None, :]
    dec = np.concatenate([dec_q, dec_k, np.broadcast_to(dec_state, (8, MIX_WIDTH))], 0)
    diff = t[:, None] - t[None, :]
    dec_in = np.concatenate([np.where(diff >= 0, np.exp(log_gamma[h] * diff), 0.0) for h in range(N_HEADS)], 0)
    return dec.astype(np.float32), dec_in.astype(np.float32)


def _ret(proj, cos4, sin4, gain, bsz, seq):
    dec, dec_in = _ret_consts()
    tb = min(REC_BLOCK, seq)
    nb = seq // tb
    return pl.pallas_call(
        _ret_kernel,
        grid=(bsz, nb),
        in_specs=[pl.BlockSpec((tb, 4 * MIX_WIDTH), lambda b, i: (b * nb + i, OFF_R // (4 * MIX_WIDTH))),
                  pl.BlockSpec((tb, MIX_WIDTH), lambda b, i: (i, 0)),
                  pl.BlockSpec((tb, MIX_WIDTH), lambda b, i: (i, 0)),
                  pl.BlockSpec(dec.shape, lambda b, i: (0, 0)),
                  pl.BlockSpec(dec_in.shape, lambda b, i: (0, 0)),
                  pl.BlockSpec((1, MIX_WIDTH), lambda b, i: (0, 0))],
        out_specs=pl.BlockSpec((tb, MIX_WIDTH), lambda b, i: (b * nb + i, 0)),
        out_shape=jax.ShapeDtypeStruct((bsz * seq, MIX_WIDTH), F32),
        scratch_shapes=[pltpu.VMEM((MIX_WIDTH, MIX_WIDTH), F32)],
        compiler_params=_cp("parallel", "arbitrary"),
        name="retention",
    )(proj, cos4, sin4, jnp.asarray(dec), jnp.asarray(dec_in), gain.reshape(1, MIX_WIDTH))


def _hgrn_lb_rows(lb):
    lb = lb.astype(F32)
    rows = jnp.stack([jnp.log(lb), jnp.log1p(-lb), 1.0 - lb], 0)
    return jnp.concatenate([rows, jnp.zeros((5, lb.shape[0]), F32)], 0)


def _rope_lane_tables(seq):
    inv = 1.0 / (ROPE_THETA ** (jnp.arange(0, HEAD_DIM, 2, dtype=F32) / HEAD_DIM))
    ang = jnp.arange(seq, dtype=F32)[:, None] * inv[None, :]
    cos, sin = jnp.cos(ang), jnp.sin(ang)
    cos_t = jnp.tile(cos, (1, LANES // (HEAD_DIM // 2)))
    sin_t = jnp.tile(jnp.concatenate([-sin, sin], axis=1), (1, LANES // HEAD_DIM))
    return cos_t, sin_t


def _expand_heads(cols, shape):
    hl = _head_of_lane(shape, 1)
    out = jnp.broadcast_to(cols[-1], shape)
    for h in range(len(cols) - 2, -1, -1):
        out = jnp.where(hl == h, jnp.broadcast_to(cols[h], shape), out)
    return out


def _mlstm_kernel(p_ref, gcol_ref, grow_ref, cw_ref, cb_ref, gbr_ref, gbc_ref, g_ref, o_ref,
                  ct_ref, n_ref, m_ref, hist_ref, cbuf_ref, qk_ref):
    c = CHUNK
    w = MIX_WIDTH
    tb = p_ref.shape[0]

    @pl.when(pl.program_id(1) == 0)
    def _():
        ct_ref[...] = jnp.zeros_like(ct_ref)
        n_ref[...] = jnp.zeros_like(n_ref)
        m_ref[...] = jnp.zeros_like(m_ref)
        hist_ref[...] = jnp.zeros_like(hist_ref)

    cbuf_ref[0:8, :] = hist_ref[...]
    cbuf_ref[8:, :] = p_ref[:, 0:2 * w]
    hist_ref[...] = p_ref[tb - 8:tb, 0:2 * w]
    acc = jnp.broadcast_to(cb_ref[...], (tb, 2 * w))
    for j in range(CONV_W):
        acc = acc + cw_ref[j:j + 1, :] * cbuf_ref[pl.ds(8 - (CONV_W - 1) + j, tb), :]
    qk_ref[...] = _silu(acc)

    ones_bd = _block_ones(w)
    bd_mask = _block_ones(w, F32)
    ti = lax.broadcasted_iota(jnp.int32, (c, c), 0)
    si = lax.broadcasted_iota(jnp.int32, (c, c), 1)
    causal = ti >= si
    tri = causal.astype(BF16)
    tri_t = (ti <= si).astype(BF16)
    gain = g_ref[...]
    ones_ext = jnp.ones((c, LANES), BF16)

    def chunk(ci, carry):
        r0 = pl.multiple_of(ci * c, c)
        q = qk_ref[pl.ds(r0, c), 0:w]
        k = qk_ref[pl.ds(r0, c), w:2 * w] * (HEAD_DIM ** -0.5)
        v = p_ref[pl.ds(r0, c), 2 * w:3 * w]
        op = p_ref[pl.ds(r0, c), 3 * w:4 * w]
        gc = gcol_ref[pl.ds(r0, c), :] + gbr_ref[...]
        gr = grow_ref[ci] + gbc_ref[...]
        b_c = _dot01_l(tri, _log_sigmoid(gc))
        b_r = _dot01_r(_log_sigmoid(gr), tri_t)
        wd, s_inter, em, wk, decay = [], [], [], [], []
        for h in range(N_HEADS):
            bc = b_c[:, N_HEADS + h:N_HEADS + h + 1]
            lic = gc[:, h:h + 1]
            br = b_r[N_HEADS + h:N_HEADS + h + 1, :]
            lir = gr[h:h + 1, :]
            dmat = jnp.where(causal, bc - br + lir, -jnp.inf)
            m_prev = m_ref[h:h + 1, 0:1]
            inter = bc + m_prev
            mrow = jnp.maximum(inter, jnp.max(dmat, axis=1, keepdims=True))
            wd.append(jnp.exp(dmat - mrow))
            s_inter.append(jnp.exp(inter - mrow))
            em.append(jnp.exp(-mrow))
            b_last = br[:, c - 1:c]
            m_new = jnp.maximum(b_last + m_prev, jnp.max(b_last - br + lir, axis=1, keepdims=True))
            wk.append(jnp.exp(b_last - bc + lic - m_new))
            decay.append(jnp.exp(b_last + m_prev - m_new))
            m_ref[h:h + 1, :] = jnp.broadcast_to(m_new, (1, LANES))
        s_inter_l = _expand_heads(s_inter, (c, w))
        em_l = _expand_heads(em, (c, w))
        wk_l = _expand_heads(wk, (c, w))
        decay_l = _expand_heads(decay, (1, w))
        qk = _dot_nt(_stack_heads(q).astype(BF16), k.astype(BF16))
        wmat = jnp.concatenate(wd, axis=0) * qk
        vb = v.astype(BF16)
        r = _dot(wmat.astype(BF16), jnp.concatenate([vb, ones_ext], axis=1))
        num_intra = _unstack_heads(r[:, 0:w], c)
        rs_l = _expand_heads([r[h * c:(h + 1) * c, w:w + 1] for h in range(N_HEADS)], (c, w))
        ct = ct_ref[...]
        nrow = n_ref[0:1, :]
        num = s_inter_l * _dot_nt(q.astype(BF16), ct.astype(BF16)) + num_intra
        den = s_inter_l * _group_sum(q * nrow, ones_bd) + rs_l
        hval = num / jnp.maximum(jnp.abs(den), em_l)
        kw = wk_l * k
        ct_ref[...] = ct * decay_l + bd_mask * _dot_tn(vb, kw.astype(BF16))
        n_ref[0:1, :] = nrow * decay_l + jnp.sum(kw, axis=0, keepdims=True)
        o_ref[pl.ds(r0, c), :] = _head_rms(hval, gain, ones_bd) * _sigmoid(op)
        return carry

    lax.fori_loop(0, tb // c, chunk, 0)


def _mlstm(proj, conv_w, conv_b, gate_b, gain, bsz, seq):
    t = bsz * seq
    w = MIX_WIDTH
    tb = min(REC_BLOCK, seq)
    nb = seq // tb
    ncb = tb // CHUNK
    grow = proj[:, OFF_MG:OFF_MG + 8].reshape(t // CHUNK, CHUNK, 8).transpose(0, 2, 1)
    gb_row = jnp.zeros((1, LANES), F32).at[0, 0:8].set(gate_b.astype(F32))
    gb_col = gate_b.astype(F32).reshape(8, 1)
    return pl.pallas_call(
        _mlstm_kernel,
        grid=(bsz, nb),
        in_specs=[pl.BlockSpec((tb, 4 * w), lambda b, i: (b * nb + i, OFF_M // (4 * w))),
                  pl.BlockSpec((tb, LANES), lambda b, i: (b * nb + i, OFF_MG // LANES)),
                  pl.BlockSpec((ncb, 8, CHUNK), lambda b, i: (b * nb + i, 0, 0)),
                  pl.BlockSpec((CONV_W, 2 * w), lambda b, i: (0, 0)),
                  pl.BlockSpec((1, 2 * w), lambda b, i: (0, 0)),
                  pl.BlockSpec((1, LANES), lambda b, i: (0, 0)),
                  pl.BlockSpec((8, 1), lambda b, i: (0, 0)),
                  pl.BlockSpec((1, w), lambda b, i: (0, 0))],
        out_specs=pl.BlockSpec((tb, w), lambda b, i: (b * nb + i, 0)),
        out_shape=jax.ShapeDtypeStruct((t, w), F32),
        scratch_shapes=[pltpu.VMEM((w, w), F32), pltpu.VMEM((8, w), F32), pltpu.VMEM((8, LANES), F32),
                        pltpu.VMEM((8, 2 * w), F32), pltpu.VMEM((tb + 8, 2 * w), F32),
                        pltpu.VMEM((tb, 2 * w), F32)],
        compiler_params=_cp("parallel", "arbitrary"),
        name="mlstm",
    )(proj, proj, grow, conv_w.astype(F32), conv_b.astype(F32).reshape(1, 2 * w), gb_row, gb_col,
      gain.reshape(1, w))


NSA_TQ = 128
NSA_KC = 512
GW = 2 * HEAD_DIM


def _nsa_prep_kernel(pq_ref, pk_ref, pv_ref, cos_ref, sin_ref, qg_ref, kg_ref,
                     qn_ref, qr_ref, ks_ref, kw_ref, vst_ref, vwt_ref):
    w = MIX_WIDTH
    for src, dst in ((pv_ref[:, 0:GW], vst_ref), (pv_ref[:, GW:2 * GW], vwt_ref)):
        vt = src.T
        tk = dst.shape[4]
        for g in range(NSA_GROUPS):
            rows = vt[g * HEAD_DIM:(g + 1) * HEAD_DIM, :]
            dup = jnp.concatenate([rows, rows], axis=0).astype(BF16)
            for j in range(dst.shape[2]):
                dst[0, g, j] = dup[:, j * tk:(j + 1) * tk]
    ones_bd = _block_ones(w)
    cos_t, sin_t = cos_ref[...], sin_ref[...]
    scale = HEAD_DIM ** -0.5
    qh = _head_rms(pq_ref[...], qg_ref[...], ones_bd)
    qn_ref[...] = (qh * scale).astype(BF16)
    qr_ref[...] = (_rope(qh, cos_t, sin_t) * scale).astype(BF16)
    ks_ref[...] = _rope(_head_rms(pk_ref[:, 0:w], kg_ref[1:2, :], ones_bd), cos_t, sin_t).astype(BF16)
    kw_ref[...] = _rope(_head_rms(pk_ref[:, w:2 * w], kg_ref[2:3, :], ones_bd), cos_t, sin_t).astype(BF16)


def _nsa_prep(proj, cos4, sin4, qnorm_g, knorm_g, bsz, seq):
    t = bsz * seq
    w = MIX_WIDTH
    tm = min(NSA_KC, seq)
    tq = min(NSA_TQ, seq)
    ns = seq // tm
    qg = jnp.tile(qnorm_g.astype(F32), w // HEAD_DIM).reshape(1, w)
    kg = jnp.concatenate([jnp.tile(knorm_g.astype(F32), (1, w // HEAD_DIM)), jnp.zeros((5, w), F32)], axis=0)
    out = jax.ShapeDtypeStruct((t, w), BF16)
    row = pl.BlockSpec((tm, w), lambda i: (i, 0))
    return pl.pallas_call(
        _nsa_prep_kernel,
        grid=(t // tm,),
        in_specs=[pl.BlockSpec((tm, w), lambda i: (i, OFF_NQ // w)),
                  pl.BlockSpec((tm, 2 * w), lambda i: (i, OFF_KD // (2 * w))),
                  pl.BlockSpec((tm, 2 * GW), lambda i: (i, OFF_V // (2 * GW))),
                  pl.BlockSpec((tm, w), lambda i: (i % ns, 0)),
                  pl.BlockSpec((tm, w), lambda i: (i % ns, 0)),
                  pl.BlockSpec((1, w), lambda i: (0, 0)),
                  pl.BlockSpec((8, w), lambda i: (0, 0))],
        out_specs=[row, row, row, row,
                   pl.BlockSpec((1, NSA_GROUPS, 1, GW, tm), lambda i: (i // ns, 0, i % ns, 0, 0)),
                   pl.BlockSpec((1, NSA_GROUPS, tm // tq, GW, tq), lambda i: (i // ns, 0, i % ns, 0, 0))],
        out_shape=[out, out, out, out,
                   jax.ShapeDtypeStruct((bsz, NSA_GROUPS, seq // tm, GW, tm), BF16),
                   jax.ShapeDtypeStruct((bsz, NSA_GROUPS, seq // tq, GW, tq), BF16)],
        compiler_params=_cp("parallel"),
        name="nsa_prep",
    )(proj, proj, proj, cos4, sin4, qg, kg)


def _nsa_cmp_kernel(xr_ref, pe_ref, w0_ref, w1_ref, kg_ref, ovt_ref, qn_ref, ocmp_ref, sel_ref,
                    kc_ref, vc_ref, v_ref, *, n_top):
    tq = qn_ref.shape[0]
    nr = xr_ref.shape[0]
    nsel = sel_ref.shape[2]
    w = MIX_WIDTH

    @pl.when(pl.program_id(1) == 0)
    def _():
        xr = xr_ref[...]
        y0 = _dot((xr + pe_ref[0]).astype(BF16), w0_ref[...])
        y1 = _dot((xr + pe_ref[1]).astype(BF16), w1_ref[...])
        kv = y0 + pltpu.roll(y1, nr - 1, 0)
        kc_ref[...] = _head_rms(kv[:, 0:w], kg_ref[...], _block_ones(w)).astype(BF16)
        vc_ref[...] = kv[:, w:2 * w].astype(BF16)

    pos0 = pl.program_id(1) * tq
    hl = _head_of_lane((tq, GW), 1)
    pos_r = pos0 + lax.broadcasted_iota(jnp.int32, (tq, nr), 0)
    valid = lax.broadcasted_iota(jnp.int32, (tq, nr), 1) * CMP_STRIDE + (CMP_LEN - 1) <= pos_r
    pos_c = pos0 + lax.broadcasted_iota(jnp.int32, (nr, tq), 1)
    valid_t = lax.broadcasted_iota(jnp.int32, (nr, tq), 0) * CMP_STRIDE + (CMP_LEN - 1) <= pos_c
    jrow = lax.broadcasted_iota(jnp.int32, (nsel, tq), 0)
    cur = (pos0 + lax.broadcasted_iota(jnp.int32, (nsel, tq), 1)) // SEL_BLOCK
    forced = (jrow == 0) | (jrow == cur) | (jrow == cur - 1)
    ovt = ovt_ref[...]

    for g in range(NSA_GROUPS):
        qg = qn_ref[:, g * GW:(g + 1) * GW]
        kg = kc_ref[:, g * GW:(g + 1) * GW]
        vg = vc_ref[:, g * GW:(g + 1) * GW]
        o_g = jnp.zeros((tq, GW), F32)
        pt_sum = jnp.zeros((nr, tq), F32)
        for hh in range(2):
            qm = jnp.where(hl == hh, qg, jnp.zeros_like(qg))
            s = jnp.where(valid, _dot_nt(qm, kg), NEG_INF)
            e = jnp.exp(s - jnp.max(s, axis=1, keepdims=True))
            p = jnp.where(valid, e / jnp.sum(e, axis=1, keepdims=True), 0.0)
            o_g = jnp.where(hl == hh, _dot(p.astype(BF16), vg), o_g)
            st = jnp.where(valid_t, _dot_nt(kg, qm), NEG_INF)
            et = jnp.exp(st - jnp.max(st, axis=0, keepdims=True))
            pt_sum = pt_sum + jnp.where(valid_t, et / jnp.sum(et, axis=0, keepdims=True), 0.0)
        ocmp_ref[:, g * GW:(g + 1) * GW] = o_g
        p_hi = pt_sum.astype(BF16)
        p_lo = (pt_sum - p_hi.astype(F32)).astype(BF16)
        imp = _dot(ovt, p_hi) + _dot(ovt, p_lo)
        val = jnp.where(jrow <= cur, imp + FORCE_BONUS * forced.astype(F32), NEG_INF)
        v_ref[...] = val

        def rank(jp, cnt):
            row = v_ref[pl.ds(jp, 1), :]
            tie = jnp.where(jrow > jp, 1.0, 0.0)
            return cnt + jnp.where(row > val, 1.0, jnp.where(row == val, tie, 0.0))

        cnt = lax.fori_loop(0, nsel, rank, jnp.zeros((nsel, tq), F32))
        sel_ref[0, g] = ((cnt < n_top) & (jrow <= cur)).astype(F32)


def _nsa_cmp_weights(cmp_pe, cmp_w):
    half = CMP_LEN // 2
    wl = cmp_w.astype(F32).reshape(2, 2, half, HEAD_DIM, HEAD_DIM)
    eye2 = jnp.eye(2, dtype=F32)
    w2 = jnp.einsum('kardz,kK,gG,h->arkgdKGhz', wl, eye2, eye2, jnp.ones((2,), F32))
    w2 = w2.reshape(2, half * 4 * HEAD_DIM, 8 * HEAD_DIM)
    pl_ = cmp_pe.astype(F32).reshape(2, 2, half, HEAD_DIM)
    pe2 = jnp.broadcast_to(pl_.transpose(1, 2, 0, 3)[:, :, :, None, :], (2, half, 2, 2, HEAD_DIM))
    return w2.astype(BF16), pe2.reshape(2, 1, half * 4 * HEAD_DIM)


def _nsa_cmp(kcvc, qn, cmp_pe, cmp_w, knorm0, bsz, seq, tq=512):
    t = bsz * seq
    w = MIX_WIDTH
    tq = min(tq, seq)
    nq = seq // tq
    nr = seq // CMP_STRIDE
    nsel = seq // SEL_BLOCK
    n_top = min(SEL_TOPK, nsel)
    w2, pe2 = _nsa_cmp_weights(cmp_pe, cmp_w)
    xr = kcvc.reshape(t // CMP_STRIDE, CMP_STRIDE * w)
    kg = jnp.tile(knorm0.astype(F32), w // HEAD_DIM).reshape(1, w)
    n_i = np.arange(nr)[:, None] * CMP_STRIDE
    j_i = np.arange(nsel)[None, :] * SEL_BLOCK
    ov = ((n_i < j_i + SEL_BLOCK) & (n_i + CMP_LEN > j_i)).astype(np.float32)
    ov[nr - 1, :] = 0.0
    kin = CMP_STRIDE * w
    return pl.pallas_call(
        functools.partial(_nsa_cmp_kernel, n_top=n_top),
        grid=(bsz, nq),
        in_specs=[pl.BlockSpec((nr, kin), lambda b, i: (b, 0)),
                  pl.BlockSpec((2, 1, kin), lambda b, i: (0, 0, 0)),
                  pl.BlockSpec((None, kin, 2 * w), lambda b, i: (0, 0, 0)),
                  pl.BlockSpec((None, kin, 2 * w), lambda b, i: (1, 0, 0)),
                  pl.BlockSpec((1, w), lambda b, i: (0, 0)),
                  pl.BlockSpec((nsel, nr), lambda b, i: (0, 0)),
                  pl.BlockSpec((tq, w), lambda b, i: (b * nq + i, 0))],
        out_specs=[pl.BlockSpec((tq, w), lambda b, i: (b * nq + i, 0)),
                   pl.BlockSpec((1, NSA_GROUPS, nsel, tq), lambda b, i: (b, 0, 0, i))],
        out_shape=[jax.ShapeDtypeStruct((t, w), F32),
                   jax.ShapeDtypeStruct((bsz, NSA_GROUPS, nsel, seq), F32)],
        scratch_shapes=[pltpu.VMEM((nr, w), BF16), pltpu.VMEM((nr, w), BF16), pltpu.VMEM((nsel, tq), F32)],
        compiler_params=_cp("parallel", "arbitrary"),
        name="nsa_cmp",
    )(xr, pe2, w2, w2, kg, jnp.asarray(ov.T, BF16), qn)


def _nsa_attn_kernel(qr_ref, ks_ref, kw_ref, vs_ref, vw_ref, sel_ref, ocmp_ref, gate_ref, o_ref, *, kc, wt):
    tq = qr_ref.shape[0]
    i = pl.program_id(1)
    hl = _head_of_lane((tq, GW), 1)
    nbk = kc // SEL_BLOCK
    groups = range(NSA_GROUPS)

    def stacked_q(g):
        q = qr_ref[:, g * GW:(g + 1) * GW]
        return jnp.concatenate([jnp.where(hl == 0, q, jnp.zeros_like(q)), jnp.where(hl == 1, q, jnp.zeros_like(q))],
                               axis=0)

    qs_all = [stacked_q(g) for g in groups]

    def lane_qpos(rows):
        return i * tq + lax.broadcasted_iota(jnp.int32, (rows, 2 * tq), 1) % tq

    def finish(acc, l):
        ot = (acc / l).T
        return jnp.where(hl == 0, ot[0:tq, :], ot[tq:2 * tq, :])

    qpos_s = lane_qpos(kc)
    krow_s = lax.broadcasted_iota(jnp.int32, (kc, 2 * tq), 0)

    def sel_step(g, c, carry, diagonal):
        m, l, acc = carry
        k0 = pl.multiple_of(c * kc, kc)
        st = _dot_nt(ks_ref[pl.ds(k0, kc), g * GW:(g + 1) * GW], qs_all[g])
        srows = sel_ref[0, g, pl.ds(pl.multiple_of(c * nbk, nbk), nbk), :]
        srows = jnp.concatenate([srows, srows], axis=1)
        smask = jnp.concatenate([jnp.broadcast_to(srows[r:r + 1, :], (SEL_BLOCK, 2 * tq)) for r in range(nbk)],
                                axis=0)
        msk = smask > 0.5
        if diagonal:
            msk = msk & (k0 + krow_s <= qpos_s)
        st = jnp.where(msk, st, NEG_INF)
        m_new = jnp.maximum(m, jnp.max(st, axis=0, keepdims=True))
        p = jnp.exp(st - m_new)
        alpha = jnp.exp(m - m_new)
        l = l * alpha + jnp.sum(p, axis=0, keepdims=True)
        acc = acc * alpha + _dot(vs_ref[0, g, c], p.astype(BF16))
        return m_new, l, acc

    def sel_body(c, carries, diagonal):
        return tuple(sel_step(g, c, carries[g], diagonal) for g in groups)

    init = (jnp.full((1, 2 * tq), NEG_INF, F32), jnp.zeros((1, 2 * tq), F32), jnp.zeros((GW, 2 * tq), F32))
    n_before = (i * tq) // kc
    carries = lax.fori_loop(0, n_before, functools.partial(sel_body, diagonal=False), (init,) * NSA_GROUPS)
    carries = sel_body(n_before, carries, True)

    j0 = jnp.maximum(i - (wt - 1), 0)
    k0 = pl.multiple_of(j0 * tq, tq)
    span = wt * tq
    kpos = k0 + lax.broadcasted_iota(jnp.int32, (span, 2 * tq), 0)
    qpos_w = lane_qpos(span)
    wmask = (kpos <= qpos_w) & (kpos > qpos_w - WINDOW)
    gb = _sigmoid(gate_ref[...])
    for g in groups:
        _, l_s, acc_s = carries[g]
        o_sel = finish(acc_s, l_s)
        st = jnp.where(wmask, _dot_nt(kw_ref[pl.ds(k0, span), g * GW:(g + 1) * GW], qs_all[g]), NEG_INF)
        p = jnp.exp(st - jnp.max(st, axis=0, keepdims=True))
        vt = jnp.concatenate([vw_ref[0, g, j0 + r] for r in range(wt)], axis=1)
        o_win = finish(_dot(vt, p.astype(BF16)), jnp.sum(p, axis=0, keepdims=True))

        def gate(branch):
            cols = [gb[:, (2 * g + hh) * 3 + branch:(2 * g + hh) * 3 + branch + 1] for hh in range(2)]
            return _expand_heads(cols, (tq, GW))

        o_ref[:, g * GW:(g + 1) * GW] = (gate(0) * ocmp_ref[:, g * GW:(g + 1) * GW]
                                         + gate(1) * o_sel + gate(2) * o_win)


def _nsa_attn(proj, qr, ks, kw, vst, vwt, sel, o_cmp, bsz, seq):
    t = bsz * seq
    w = MIX_WIDTH
    tq = min(NSA_TQ, seq)
    nq = seq // tq
    nsel = seq // SEL_BLOCK
    kc = min(NSA_KC, seq)
    wt = min(WINDOW // tq + 1, nq)
    kspec = pl.BlockSpec((seq, w), lambda b, i: (b, 0))
    return pl.pallas_call(
        functools.partial(_nsa_attn_kernel, kc=kc, wt=wt),
        grid=(bsz, nq),
        in_specs=[pl.BlockSpec((tq, w), lambda b, i: (b * nq + i, 0)),
                  kspec, kspec,
                  pl.BlockSpec((1, NSA_GROUPS, seq // kc, GW, kc), lambda b, i: (b, 0, 0, 0, 0)),
                  pl.BlockSpec((1, NSA_GROUPS, nq, GW, tq), lambda b, i: (b, 0, 0, 0, 0)),
                  pl.BlockSpec((1, NSA_GROUPS, nsel, tq), lambda b, i: (b, 0, 0, i)),
                  pl.BlockSpec((tq, w), lambda b, i: (b * nq + i, 0)),
                  pl.BlockSpec((tq, LANES), lambda b, i: (b * nq + i, OFF_NG // LANES))],
        out_specs=pl.BlockSpec((tq, w), lambda b, i: (b * nq + i, 0)),
        out_shape=jax.ShapeDtypeStruct((t, w), F32),
        compiler_params=_cp("parallel", "arbitrary"),
        name="nsa_attn",
    )(qr, ks, kw, vst, vwt, sel, o_cmp, proj)


def _nsa(proj, kcvc, cos4, sin4, qnorm_g, knorm_g, cmp_pe, cmp_w, bsz, seq):
    qn, qr, ks, kw, vst, vwt = _nsa_prep(proj, cos4, sin4, qnorm_g, knorm_g, bsz, seq)
    o_cmp, sel = _nsa_cmp(kcvc, qn, cmp_pe, cmp_w, knorm_g[0], bsz, seq)
    return _nsa_attn(proj, qr, ks, kw, vst, vwt, sel, o_cmp, bsz, seq)


PEER_TT = 128
PEER_CT = 8
HALF_D = 512


SUBLANES = 8
CODE_BITS = 127
FAR_BELOW = -3.0e38


def _with_code(x, code):
    bits = lax.bitcast_convert_type(x, jnp.int32)
    return lax.bitcast_convert_type((bits & ~CODE_BITS) | code, F32)


def _split_code(x):
    bits = lax.bitcast_convert_type(x, jnp.int32)
    return lax.bitcast_convert_type(bits & ~CODE_BITS, F32), bits & CODE_BITS


def _sort16_desc(xs):
    xs = list(xs)
    n = len(xs)
    k = 2
    while k <= n:
        j = k // 2
        while j >= 1:
            for i in range(n):
                l = i ^ j
                if l > i:
                    hi, lo = jnp.maximum(xs[i], xs[l]), jnp.minimum(xs[i], xs[l])
                    xs[i], xs[l] = (hi, lo) if (i & k) == 0 else (lo, hi)
            j //= 2
        k *= 2
    return xs


def _merge16_desc(xs):
    xs = list(xs)
    j = len(xs) // 2
    while j >= 1:
        for i in range(len(xs)):
            l = i ^ j
            if l > i:
                xs[i], xs[l] = jnp.maximum(xs[i], xs[l]), jnp.minimum(xs[i], xs[l])
        j //= 2
    return xs


def _top16_columns(x):
    n = PEER_TOPK
    xs = _sort16_desc([x[SUBLANES * j:SUBLANES * (j + 1), :] for j in range(n)])
    shift = SUBLANES // 2
    while shift >= 1:
        rolled = [pltpu.roll(a, shift, 0) for a in xs]
        xs = _merge16_desc([jnp.maximum(xs[i], rolled[n - 1 - i]) for i in range(n)])
        shift //= 2
    return xs


_PEER_CAND_TILES = ((0, 0, 8), (0, 1, 8), (1, 0, 8), (2, 0, 5), (3, 0, 4), (4, 0, 3), (5, 0, 2), (6, 0, 2), (7, 0, 2))


ROUTE_HEADS_PER_STEP = 4


def _route_head(q_ref, key_ref, hh):
    tt = q_ref.shape[0]
    nk = PEER_NKEYS
    n = PEER_TOPK
    row = lax.broadcasted_iota(jnp.int32, (nk, tt), 0)
    sub = lax.broadcasted_iota(jnp.int32, (SUBLANES, tt), 0)
    vals, ids = [], []
    for p in range(2):
        c0 = (2 * hh + p) * PEER_KDIM
        st = _dot_nt(key_ref[hh, p], q_ref[:, c0:c0 + PEER_KDIM])
        top = [_split_code(a) for a in _top16_columns(_with_code(st, (nk - 1) - row))]
        vals.append([v for v, _ in top])
        ids.append([(nk - 1) - c for _, c in top])
    (v1, v2), (i1, i2) = vals, ids

    def stack(xs, lo):
        out = xs[lo]
        for s in range(1, SUBLANES):
            out = jnp.where(sub == s, xs[lo + s], out)
        return out

    v2t, i2t = (stack(v2, 0), stack(v2, SUBLANES)), (stack(i2, 0), stack(i2, SUBLANES))
    cand, cexp = [], []
    for a, tile, nvalid in _PEER_CAND_TILES:
        v = v1[a] + v2t[tile]
        cand.append(v if nvalid == SUBLANES else jnp.where(sub < nvalid, v, FAR_BELOW))
        cexp.append(i1[a] * nk + i2t[tile])
    cand.append(stack(v1, SUBLANES) + v2[0])
    cexp.append(stack(i1, SUBLANES) * nk + i2[0])
    n_tiles = len(cand)
    slot_code = [(nk - 1) - (c * SUBLANES + sub) for c in range(n_tiles)]
    coded = [_with_code(v, sc) for v, sc in zip(cand, slot_code)]
    coded += [jnp.full((SUBLANES, tt), FAR_BELOW, F32)] * (n - n_tiles)
    top = [_split_code(a) for a in _top16_columns(jnp.concatenate(coded, axis=0))]
    call = jnp.concatenate(cexp, axis=0)
    slot = (nk - 1) - lax.broadcasted_iota(jnp.int32, call.shape, 0)
    ex = [jnp.exp(v - top[0][0]) for v, _ in top]
    tot = ex[0]
    for k in range(1, n):
        tot = tot + ex[k]
    krow = lax.broadcasted_iota(jnp.int32, (n, tt), 0)
    e_tile = jnp.zeros((n, tt), F32)
    g_tile = jnp.zeros((n, tt), F32)
    for k in range(n):
        hit = slot == jnp.concatenate([top[k][1]] * n_tiles, axis=0)
        e_k = jnp.sum(jnp.where(hit, call, 0), axis=0, keepdims=True)
        e_tile = jnp.where(krow == k, e_k.astype(F32), e_tile)
        g_tile = jnp.where(krow == k, (ex[k] / tot)[0:1, :], g_tile)
    return e_tile, g_tile


def _peer_route_kernel(q_ref, key_ref, e_ref, g_ref, e_scr, g_scr):
    hps = key_ref.shape[0]
    tiles = [_route_head(q_ref, key_ref, hh) for hh in range(hps)]
    rows = hps * PEER_TOPK
    r0 = pl.multiple_of(pl.program_id(1) * rows, rows)
    e_scr[pl.ds(r0, rows), :] = jnp.concatenate([e for e, _ in tiles], axis=0)
    g_scr[pl.ds(r0, rows), :] = jnp.concatenate([g for _, g in tiles], axis=0)

    @pl.when(pl.program_id(1) == pl.num_programs(1) - 1)
    def _():
        e_ref[...] = e_scr[...].T.astype(jnp.int32)
        g_ref[...] = g_scr[...].T


def _peer_route(qp, keys):
    t = qp.shape[0]
    tt = min(PEER_TT, t)
    ne = PEER_HEADS * PEER_TOPK
    hps = ROUTE_HEADS_PER_STEP
    return pl.pallas_call(
        _peer_route_kernel,
        grid=(t // tt, PEER_HEADS // hps),
        in_specs=[pl.BlockSpec((tt, hps * 2 * PEER_KDIM), lambda i, h: (i, h)),
                  pl.BlockSpec((hps, 2, PEER_NKEYS, PEER_KDIM), lambda i, h: (h, 0, 0, 0))],
        out_specs=[pl.BlockSpec((tt, ne), lambda i, h: (i, 0)),
                   pl.BlockSpec((tt, ne), lambda i, h: (i, 0))],
        out_shape=[jax.ShapeDtypeStruct((t, ne), jnp.int32),
                   jax.ShapeDtypeStruct((t, ne), F32)],
        scratch_shapes=[pltpu.VMEM((ne, tt), F32), pltpu.VMEM((ne, tt), F32)],
        compiler_params=_cp("parallel", "arbitrary"),
        name="peer_route",
    )(qp, keys)


def _pack_tables_kernel(u_ref, v_ref, o_ref):
    def pack(x):
        lo = lax.bitcast_convert_type(x[:, 0:HALF_D].astype(BF16).astype(F32), jnp.int32)
        hi = lax.bitcast_convert_type(x[:, HALF_D:2 * HALF_D].astype(BF16).astype(F32), jnp.int32)
        return lax.shift_right_logical(lo, 16) | (hi & jnp.int32(-65536))

    o_ref[:, 0:HALF_D] = pack(u_ref[...])
    o_ref[:, HALF_D:2 * HALF_D] = pack(v_ref[...])


def _pack_tables(u_tabs, v_tabs, layer, tr=512):
    _, e, d = u_tabs.shape
    assert d == 2 * HALF_D
    spec_in = pl.BlockSpec((None, tr, d), lambda i: (layer, i, 0))
    spec = pl.BlockSpec((tr, d), lambda i: (i, 0))
    return pl.pallas_call(
        _pack_tables_kernel,
        grid=(e // tr,),
        in_specs=[spec_in, spec_in],
        out_specs=spec,
        out_shape=jax.ShapeDtypeStruct((e, d), jnp.int32),
        compiler_params=_cp("parallel"),
        name="peer_pack",
    )(u_tabs, v_tabs)


def _unpack_rows(wd):
    lo = lax.bitcast_convert_type(lax.shift_left(wd, 16), F32)
    hi = lax.bitcast_convert_type(lax.bitwise_and(wd, jnp.int32(-65536)), F32)
    return lo, hi


SC_WINDOWS = (16,)


def _sc_gather(table, idx, window):
    from jax.experimental.pallas import tpu_sc as plsc
    n = idx.shape[0]
    width = table.shape[1]
    mesh = plsc.VectorSubcoreMesh(core_axis_name="core", subcore_axis_name="subcore")

    @functools.partial(pl.kernel, out_type=jax.ShapeDtypeStruct((n, width), table.dtype), mesh=mesh)
    def gather(tab_hbm, idx_hbm, out_hbm):
        def body(idx_vmem, out_vmem):
            pltpu.sync_copy(tab_hbm.at[idx_vmem.at[0, pl.ds(0, window)]], out_vmem)

        pltpu.emit_pipeline(
            body,
            grid=(n // window,),
            in_specs=[pl.BlockSpec((1, LANES), lambda i: (0, i))],
            out_specs=[pl.BlockSpec((window, width), lambda i: (i, 0))],
            core_axis_name=("core", "subcore"),
            dimension_semantics=(pltpu.PARALLEL,),
            trace_scopes=False,
        )(idx_hbm, out_hbm)

    idx_pad = jnp.pad(idx.reshape(n // window, window), ((0, 0), (0, LANES - window)))
    return gather(table, idx_pad.reshape(1, (n // window) * LANES))


def _peer_combine_kernel(x_ref, g2_ref, rows_a_ref, rows_b_ref, gate_ref, o_ref):
    ne = PEER_HEADS * PEER_TOPK
    x = x_ref[...]
    ct = x.shape[0]
    xn = x * lax.rsqrt(jnp.mean(x * x, axis=-1, keepdims=True) + NORM_EPS) * g2_ref[...]
    gate_t = jnp.concatenate([gate_ref[...]] * (ne // ct), axis=0).T
    for jj in range(ct):
        rows_ref, j = (rows_a_ref, jj) if jj < ct // 2 else (rows_b_ref, jj - ct // 2)
        u_lo, u_hi = _unpack_rows(rows_ref[j * ne:(j + 1) * ne, 0:HALF_D])
        xr = xn[jj:jj + 1, :]
        h = jnp.sum(u_lo * xr[:, 0:HALF_D] + u_hi * xr[:, HALF_D:2 * HALF_D], axis=1, keepdims=True)
        act = 0.5 * h * (1.0 + lax.erf(h * (2.0 ** -0.5)))
        wgt = gate_t[:, jj:jj + 1] * act
        v_lo, v_hi = _unpack_rows(rows_ref[j * ne:(j + 1) * ne, HALF_D:2 * HALF_D])
        o_ref[jj:jj + 1, 0:HALF_D] = x[jj:jj + 1, 0:HALF_D] + jnp.sum(wgt * v_lo, axis=0, keepdims=True)
        o_ref[jj:jj + 1, HALF_D:2 * HALF_D] = (x[jj:jj + 1, HALF_D:2 * HALF_D]
                                               + jnp.sum(wgt * v_hi, axis=0, keepdims=True))


def _peer_combine(x, g2, rows, gates, first_token):
    t, d = x.shape
    ne = PEER_HEADS * PEER_TOPK
    ct = PEER_CT
    steps = rows.shape[0] // (ct * ne)
    off = first_token // ct
    deep = pl.Buffered(3)

    def streamed(x_hbm, g2_hbm, rows_a_hbm, rows_b_hbm, gates_hbm, o_hbm):
        pltpu.emit_pipeline(
            _peer_combine_kernel,
            grid=(steps,),
            in_specs=[pl.BlockSpec((ct, d), lambda i: (off + i, 0)),
                      pl.BlockSpec((1, d), lambda i: (0, 0)),
                      pl.BlockSpec((ct * ne // 2, d), lambda i: (2 * i, 0), pipeline_mode=deep),
                      pl.BlockSpec((ct * ne // 2, d), lambda i: (2 * i + 1, 0), pipeline_mode=deep),
                      pl.BlockSpec((ct, ne), lambda i: (off + i, 0))],
            out_specs=[pl.BlockSpec((ct, d), lambda i: (off + i, 0))],
        )(x_hbm, g2_hbm, rows_a_hbm, rows_b_hbm, gates_hbm, o_hbm)

    anywhere = pl.BlockSpec(memory_space=pl.ANY)
    return pl.pallas_call(
        streamed,
        in_specs=[anywhere] * 5,
        out_specs=anywhere,
        out_shape=jax.ShapeDtypeStruct((t, d), F32),
        input_output_aliases={0: 0},
        compiler_params=pltpu.CompilerParams(vmem_limit_bytes=VMEM_LIMIT),
        name="peer_combine",
    )(x, g2.reshape(1, d), rows, rows, gates)


PEER_TOKENS_PER_GATHER = 2048


def _peer_route_stage(x, g2, wq_b, keys_b):
    t = x.shape[0]
    ne = PEER_HEADS * PEER_TOPK
    qp = _norm_matmul(x, g2, wq_b, out_dtype=BF16)
    e_tok, g_tok = _peer_route(qp, keys_b)
    return e_tok.reshape(t * ne), g_tok


def _peer_gather_stage(table, idx, t, gather_fn):
    ne = PEER_HEADS * PEER_TOPK
    tc = min(PEER_TOKENS_PER_GATHER, t)
    return [gather_fn(table, idx[c * tc * ne:(c + 1) * tc * ne], SC_WINDOWS[c % len(SC_WINDOWS)])
            for c in range(t // tc)]


def _peer_combine_stage(x, g2, rows_list, gates):
    tc = x.shape[0] // len(rows_list)
    for c, rows in enumerate(rows_list):
        x = _peer_combine(x, g2, rows, gates, c * tc)
    return x


def _peer(x, g2, wq, keys, u_tabs, v_tabs, layer, gather_fn):
    idx, gates = _peer_route_stage(x, g2, wq.astype(BF16), keys.astype(BF16))
    table = _pack_tables(u_tabs, v_tabs, layer)
    rows_list = _peer_gather_stage(table, idx, x.shape[0], gather_fn)
    return _peer_combine_stage(x, g2, rows_list, gates)


_IN_WIDTHS = (256, 256, 256, 256, 256, 256, 256, 4, 4, 256, 256, 128, 128, 128, 128, 128, 128, 12,
              256, 256, 256, 256)


def _dup_groups(wcols):
    g0, g1 = wcols[:, :HEAD_DIM], wcols[:, HEAD_DIM:]
    return jnp.concatenate([g0, g0, g1, g1], axis=1)


def _layout_w_in(w_in):
    offs = np.cumsum((0,) + _IN_WIDTHS)
    cols = [w_in[:, offs[i]:offs[i + 1]] for i in range(len(_IN_WIDTHS))]
    (hq, hf, hi, hg, mq, mk, mv, mi, mf, mo, nq, nkc, nvc, nks, nvs, nkw, nvw, ng, rq, rk, rv, rg) = cols
    d = w_in.shape[0]
    pad = lambda c, n: jnp.concatenate([c, jnp.zeros((d, n - c.shape[1]), w_in.dtype)], axis=1)
    main = jnp.concatenate([hq, hf, hi, hg, mq, mk, mv, mo, rq, rk, rv, rg,
                            _dup_groups(nks), _dup_groups(nkw), nq, nvs, nvw,
                            pad(jnp.concatenate([mi, mf], axis=1), LANES), pad(ng, LANES)], axis=1)
    assert main.shape[1] == N_MAIN
    kcvc = jnp.concatenate([nkc, nvc], axis=1)
    return main.astype(BF16), kcvc.astype(BF16)


def kernel(x, norm1_g, w_in, hgrn_lb, hgrn_onorm_g, mlstm_conv_w, mlstm_conv_b, mlstm_gate_b, mlstm_onorm_g, nsa_qnorm_g, nsa_knorm_g, nsa_cmp_pe, nsa_cmp_w, ret_onorm_g, w_up, w_gate, w_out, norm2_g, peer_wq, peer_keys, peer_u, peer_v):
    bsz, seq, d = x.shape
    t = bsz * seq
    depth = w_in.shape[0]
    cos_t, sin_t = _rope_lane_tables(seq)
    cos4, sin4 = jnp.tile(cos_t, (1, 2)), jnp.tile(sin_t, (1, 2))
    lb_cum = jnp.cumsum(jax.nn.softmax(hgrn_lb.astype(F32), axis=0), axis=0)
    lb_all = lb_cum - lb_cum[0:1]
    weights = []
    for l in range(depth):
        w_main, w_kcvc = _layout_w_in(w_in[l])
        weights.append(dict(
            main=w_main, kcvc=w_kcvc, gate=w_gate[l].astype(BF16), up=w_up[l].astype(BF16),
            out=w_out[l].astype(BF16), lb=_hgrn_lb_rows(lb_all[l]), wq=peer_wq[l].astype(BF16),
            keys=peer_keys[l].astype(BF16), table=_pack_tables(peer_u, peer_v, l)))

    def mixer_steps(xh, l, nb):
        wl = weights[l]
        st = {}

        def proj(dep):
            st["proj"] = _norm_matmul(xh, _after(norm1_g[l], dep), wl["main"])
            return st["proj"]

        def gates(dep):
            st["gates"] = _norm_matmul(xh, _after(norm1_g[l], dep), wl["gate"], act="sigmoid", out_dtype=BF16)
            return st["gates"]

        def hgrn(dep):
            st["oh"] = _hgrn(st["proj"], wl["lb"], _after(hgrn_onorm_g[l], dep), nb, seq)
            return st["oh"]

        def mlstm(dep):
            st["om"] = _mlstm(st["proj"], mlstm_conv_w[l], mlstm_conv_b[l], mlstm_gate_b[l],
                              _after(mlstm_onorm_g[l], dep), nb, seq)
            return st["om"]

        def ret(dep):
            st["or"] = _ret(st["proj"], cos4, sin4, _after(ret_onorm_g[l], dep), nb, seq)
            return st["or"]

        def nsa_front(dep):
            kcvc = _norm_matmul(xh, _after(norm1_g[l], dep), wl["kcvc"])
            qn, qr, ks, kw, vst, vwt = _nsa_prep(st["proj"], cos4, sin4, nsa_qnorm_g[l], nsa_knorm_g[l], nb, seq)
            o_cmp, sel = _nsa_cmp(kcvc, qn, nsa_cmp_pe[l], nsa_cmp_w[l], nsa_knorm_g[l][0], nb, seq)
            st["nsa"] = (qr, ks, kw, vst, vwt, sel, o_cmp)
            return o_cmp

        def nsa_attn(dep):
            del dep
            st["on"] = _nsa_attn(st["proj"], *st["nsa"], nb, seq)
            return st["on"]

        def merge(dep):
            del dep
            st["xm"] = _merge(xh, st["gates"], (st["oh"], st["om"], st["on"], st["or"]), wl["up"], wl["out"])
            return st["xm"]

        def route(dep):
            st["idx"], st["pgates"] = _peer_route_stage(st["xm"], _after(norm2_g[l], dep), wl["wq"], wl["keys"])
            return st["pgates"]

        return [proj, gates, hgrn, mlstm, ret, nsa_front, nsa_attn, merge, route], st

    combine_slots = (0, 8, 8, 8, 8, 8, 8, 8)

    def combine_steps(l, xm, rows_list, pgates):
        box = {"x": xm}
        tc = xm.shape[0] // len(rows_list)

        def make(c):
            def step(dep):
                box["x"] = _peer_combine(box["x"], _after(norm2_g[l], dep), rows_list[c], pgates, c * tc)
                return box["x"]
            return step

        return [make(c) for c in range(len(rows_list))], box

    n_groups = next(n for n in (8, 4, 2, 1) if bsz % n == 0)
    nb = bsz // n_groups
    xs = [x[g * nb:(g + 1) * nb].reshape(nb * seq, d) for g in range(n_groups)]
    dep = None
    lag = min(2, n_groups - 1)
    pending = []
    for l in range(depth):
        for g in range(n_groups):
            msteps, st = mixer_steps(xs[g], l, nb)
            due = pending.pop(0) if len(pending) == lag and lag > 0 else None
            csteps = due[1] if due is not None else []
            ci = 0
            for si, mstep in enumerate(msteps):
                dep = mstep(dep)
                while ci < len(csteps) and (ci >= len(combine_slots) or combine_slots[ci] <= si):
                    dep = csteps[ci](dep)
                    ci += 1
            for cstep in csteps[ci:]:
                dep = cstep(dep)
            if due is not None:
                xs[due[0]] = due[2]["x"]
            rows_list = _peer_gather_stage(weights[l]["table"], st["idx"], nb * seq, _sc_gather)
            csteps, box = combine_steps(l, st["xm"], rows_list, st["pgates"])
            pending.append((g, csteps, box))
            if lag == 0:
                for cstep in pending.pop(0)[1]:
                    dep = cstep(dep)
                xs[g] = box["x"]
    for pg, csteps, box in pending:
        for cstep in csteps:
            dep = cstep(dep)
        xs[pg] = box["x"]
    return jnp.concatenate(xs, axis=0).reshape(bsz, seq, d)
```

```python
import functools
import math

import numpy as np
import jax
import jax.numpy as jnp
from jax import lax
from jax.experimental import pallas as pl
from jax.experimental.pallas import tpu as pltpu

F32 = jnp.float32
BF16 = jnp.bfloat16

HEAD_DIM = 64
N_HEADS = 4
MIX_WIDTH = N_HEADS * HEAD_DIM
CHUNK = 64
NORM_EPS = 1e-6
NEG_INF = -1e30
ROPE_THETA = 10000.0
CONV_W = 4
NSA_GROUPS = 2
CMP_LEN = 32
CMP_STRIDE = 16
SEL_BLOCK = 64
SEL_TOPK = 16
WINDOW = 512
FORCE_BONUS = 1e3
PEER_HEADS = 8
PEER_NKEYS = 128
PEER_TOPK = 16
PEER_KDIM = 128

LANES = 128
VMEM_LIMIT = 48 * 1024 * 1024

OFF_H, OFF_M, OFF_R, OFF_KD, OFF_NQ, OFF_V, OFF_MG, OFF_NG = 0, 1024, 2048, 3072, 3584, 3840, 4096, 4224
N_MAIN = 4352


def _cp(*sem):
    return pltpu.CompilerParams(dimension_semantics=sem, vmem_limit_bytes=VMEM_LIMIT)


def _dot(a, b):
    return jnp.dot(a, b, preferred_element_type=F32)


def _dot_nt(a, b):
    return lax.dot_general(a, b, (((1,), (1,)), ((), ())), preferred_element_type=F32)


def _dot_tn(a, b):
    return lax.dot_general(a, b, (((0,), (0,)), ((), ())), preferred_element_type=F32)


def _split3(x):
    hi = x.astype(BF16)
    r1 = x - hi.astype(F32)
    mid = r1.astype(BF16)
    lo = (r1 - mid.astype(F32)).astype(BF16)
    return hi, mid, lo


def _dot01_l(m01, x):
    hi, mid, lo = _split3(x)
    return _dot(m01, hi) + _dot(m01, mid) + _dot(m01, lo)


def _dot01_r(x, m01):
    hi, mid, lo = _split3(x)
    return _dot(hi, m01) + _dot(mid, m01) + _dot(lo, m01)


def _head_of_lane(shape, axis):
    return lax.broadcasted_iota(jnp.int32, shape, axis) // HEAD_DIM


def _block_ones(n, dtype=BF16):
    r = lax.broadcasted_iota(jnp.int32, (n, n), 0) // HEAD_DIM
    c = lax.broadcasted_iota(jnp.int32, (n, n), 1) // HEAD_DIM
    return (r == c).astype(dtype)


def _group_sum(x, ones_bd):
    hi = x.astype(BF16)
    lo = (x - hi.astype(F32)).astype(BF16)
    return _dot(hi, ones_bd) + _dot(lo, ones_bd)


def _head_rms(x, gain, ones_bd):
    ms = _group_sum(x * x, ones_bd) * (1.0 / HEAD_DIM)
    return x * lax.rsqrt(ms + NORM_EPS) * gain


def _sigmoid(x):
    return 1.0 / (1.0 + jnp.exp(-x))


def _silu(x):
    return x * _sigmoid(x)


def _log_sigmoid(x):
    return jnp.minimum(x, 0.0) - jnp.log(1.0 + jnp.exp(-jnp.abs(x)))


def _stack_heads(x, n_heads=N_HEADS):
    hl = _head_of_lane(x.shape, 1)
    return jnp.concatenate([jnp.where(hl == h, x, jnp.zeros_like(x)) for h in range(n_heads)], axis=0)


def _unstack_heads(r, c, n_heads=N_HEADS):
    hl = _head_of_lane((c, r.shape[1]), 1)
    out = jnp.zeros((c, r.shape[1]), F32)
    for h in range(n_heads):
        out = jnp.where(hl == h, r[h * c:(h + 1) * c, :], out)
    return out


def _rope(x, cos_t, sin_t):
    n = x.shape[1]
    first = (lax.broadcasted_iota(jnp.int32, x.shape, 1) % HEAD_DIM) < (HEAD_DIM // 2)
    partner = jnp.where(first, pltpu.roll(x, n - HEAD_DIM // 2, 1), pltpu.roll(x, HEAD_DIM // 2, 1))
    return x * cos_t + partner * sin_t


def _after_kernel(a_ref, dep_ref, o_ref):
    del dep_ref
    o_ref[...] = a_ref[...]


def _after(a, dep):
    if dep is None:
        return a
    a2 = a.reshape(1, a.size)
    out = pl.pallas_call(
        _after_kernel,
        in_specs=[pl.BlockSpec(a2.shape, lambda: (0, 0)), pl.BlockSpec(memory_space=pl.ANY)],
        out_specs=pl.BlockSpec(a2.shape, lambda: (0, 0)),
        out_shape=jax.ShapeDtypeStruct(a2.shape, a2.dtype),
        name="order_after",
    )(a2, dep)
    return out.reshape(a.shape)


def _norm_matmul_kernel(x_ref, g_ref, w_ref, o_ref, xn_ref, *, act):
    @pl.when(pl.program_id(1) == 0)
    def _():
        x = x_ref[...]
        ms = jnp.mean(x * x, axis=-1, keepdims=True)
        xn_ref[...] = (x * lax.rsqrt(ms + NORM_EPS) * g_ref[...]).astype(BF16)

    y = _dot(xn_ref[...], w_ref[...])
    if act == "sigmoid":
        y = _sigmoid(y)
    o_ref[...] = y.astype(o_ref.dtype)


def _norm_matmul(x, g, w, *, act=None, out_dtype=F32, tm=1024, tn=2176):
    t, d = x.shape
    w3 = w if w.ndim == 3 else w[None]
    n_per = w3.shape[2]
    tm = min(tm, t)
    tn = next(c for c in (tn, 2048, 1024, 512, 256, 128) if n_per % c == 0)
    per = n_per // tn
    n = w3.shape[0] * n_per
    assert t % tm == 0
    return pl.pallas_call(
        functools.partial(_norm_matmul_kernel, act=act),
        grid=(t // tm, n // tn),
        in_specs=[pl.BlockSpec((tm, d), lambda i, j: (i, 0)),
                  pl.BlockSpec((1, d), lambda i, j: (0, 0)),
                  pl.BlockSpec((None, d, tn), lambda i, j: (j // per, 0, j % per))],
        out_specs=pl.BlockSpec((tm, tn), lambda i, j: (i, j)),
        out_shape=jax.ShapeDtypeStruct((t, n), out_dtype),
        scratch_shapes=[pltpu.VMEM((tm, d), BF16)],
        compiler_params=_cp("parallel", "arbitrary"),
        name="norm_matmul",
    )(x, g.reshape(1, d), w3)


def _merge_kernel(x_ref, gate_ref, oh_ref, om_ref, on_ref, or_ref, wup_ref, wout_ref, o_ref):
    d = x_ref.shape[1]
    acc = None
    for m, r in enumerate((oh_ref, om_ref, on_ref, or_ref)):
        up = _dot(r[...].astype(BF16), wup_ref[m])
        term = gate_ref[:, m * d:(m + 1) * d].astype(F32) * up
        acc = term if acc is None else acc + term
    o_ref[...] = x_ref[...] + _dot(acc.astype(BF16), wout_ref[...])


def _merge(x, gates, outs, w_up, w_out, tm=512):
    t, d = x.shape
    tm = min(tm, t)
    mix = pl.BlockSpec((tm, MIX_WIDTH), lambda i: (i, 0))
    return pl.pallas_call(
        _merge_kernel,
        grid=(t // tm,),
        in_specs=[pl.BlockSpec((tm, d), lambda i: (i, 0)),
                  pl.BlockSpec((tm, 4 * d), lambda i: (i, 0)),
                  mix, mix, mix, mix,
                  pl.BlockSpec((4, MIX_WIDTH, d), lambda i: (0, 0, 0)),
                  pl.BlockSpec((d, d), lambda i: (0, 0))],
        out_specs=pl.BlockSpec((tm, d), lambda i: (i, 0)),
        out_shape=jax.ShapeDtypeStruct((t, d), F32),
        compiler_params=_cp("parallel"),
        name="merge",
    )(x, gates, *outs, w_up, w_out)


REC_BLOCK = 256


def _chunk_consts():
    t = lax.broadcasted_iota(jnp.int32, (CHUNK, CHUNK), 0)
    s = lax.broadcasted_iota(jnp.int32, (CHUNK, CHUNK), 1)
    return t, s


def _hgrn_levels():
    t = np.arange(CHUNK)
    rows = []
    masks = []
    h = CHUNK // 2
    while h >= 1:
        ref = (t // (2 * h)) * (2 * h) + h
        p = np.zeros((CHUNK, CHUNK), np.float32)
        p[t, np.minimum(ref, CHUNK - 1)] = 1.0
        rows.append(p)
        same = (t[:, None] // (2 * h)) == (t[None, :] // (2 * h))
        m = same & ((t[:, None] // h) % 2 == 1) & ((t[None, :] // h) % 2 == 0)
        masks.append(m.astype(np.float32))
        h //= 2
    masks.append(np.eye(CHUNK, dtype=np.float32))
    return np.concatenate(rows, 0), np.stack(masks, 0)


def _hgrn_kernel(p_ref, lb_ref, g_ref, psel_ref, lmask_ref, o_ref, st_ref):
    @pl.when(pl.program_id(1) == 0)
    def _():
        st_ref[...] = jnp.zeros_like(st_ref)

    c = CHUNK
    w = MIX_WIDTH
    ones_bd = _block_ones(w)
    bd_mask = _block_ones(w, F32)
    tri = (lax.broadcasted_iota(jnp.int32, (c, c), 0) >= lax.broadcasted_iota(jnp.int32, (c, c), 1)).astype(BF16)
    psel = psel_ref[...]
    n_lv = lmask_ref.shape[0]
    log_lb, log_1mlb, one_mlb = lb_ref[0:1, :], lb_ref[1:2, :], lb_ref[2:3, :]
    gain = g_ref[...]

    def chunk(ci, carry):
        r0 = pl.multiple_of(ci * c, c)
        q = _silu(p_ref[pl.ds(r0, c), 0:w])
        fl = p_ref[pl.ds(r0, c), w:2 * w]
        v = p_ref[pl.ds(r0, c), 2 * w:3 * w]
        gp = p_ref[pl.ds(r0, c), 3 * w:4 * w]
        a1 = jnp.broadcast_to(log_lb, fl.shape)
        a2 = log_1mlb + _log_sigmoid(fl)
        mx = jnp.maximum(a1, a2)
        log_f = mx + jnp.log(jnp.exp(a1 - mx) + jnp.exp(a2 - mx))
        k = one_mlb * _sigmoid(-fl)
        b = _dot01_l(tri, log_f)
        bref = _dot01_l(psel, b)
        vb = v.astype(BF16)
        a = jnp.zeros((N_HEADS * c, c), F32)
        for lv in range(n_lv):
            if lv < n_lv - 1:
                br = bref[lv * c:(lv + 1) * c, :]
                qs = q * jnp.exp(jnp.minimum(b - br, 0.0))
                ks = k * jnp.exp(jnp.minimum(br - b, 0.0))
            else:
                qs, ks = q, k
            s_lv = _dot_nt(_stack_heads(qs).astype(BF16), ks.astype(BF16))
            a = a + jnp.concatenate([lmask_ref[lv]] * N_HEADS, axis=0) * s_lv
        o = _unstack_heads(_dot(a.astype(BF16), vb), c)
        st = st_ref[...]
        o = o + _dot_nt((q * jnp.exp(b)).astype(BF16), st.astype(BF16))
        b_last = b[c - 1:c, :]
        kb = k * jnp.exp(b_last - b)
        st_ref[...] = st * jnp.exp(b_last) + bd_mask * _dot_tn(vb, kb.astype(BF16))
        y = _head_rms(o, gain, ones_bd) * _silu(gp)
        o_ref[pl.ds(r0, c), :] = y
        return carry

    lax.fori_loop(0, p_ref.shape[0] // c, chunk, 0)


def _hgrn(proj, lb_rows, gain, bsz, seq):
    psel, lmask = _hgrn_levels()
    tb = min(REC_BLOCK, seq)
    nb = seq // tb
    return pl.pallas_call(
        _hgrn_kernel,
        grid=(bsz, nb),
        in_specs=[pl.BlockSpec((tb, 4 * MIX_WIDTH), lambda b, i: (b * nb + i, OFF_H // (4 * MIX_WIDTH))),
                  pl.BlockSpec((8, MIX_WIDTH), lambda b, i: (0, 0)),
                  pl.BlockSpec((1, MIX_WIDTH), lambda b, i: (0, 0)),
                  pl.BlockSpec(psel.shape, lambda b, i: (0, 0)),
                  pl.BlockSpec(lmask.shape, lambda b, i: (0, 0, 0))],
        out_specs=pl.BlockSpec((tb, MIX_WIDTH), lambda b, i: (b * nb + i, 0)),
        out_shape=jax.ShapeDtypeStruct((bsz * seq, MIX_WIDTH), F32),
        scratch_shapes=[pltpu.VMEM((MIX_WIDTH, MIX_WIDTH), F32)],
        compiler_params=_cp("parallel", "arbitrary"),
        name="hgrn2",
    )(proj, lb_rows, gain.reshape(1, MIX_WIDTH), jnp.asarray(psel, BF16), jnp.asarray(lmask, F32))


def _ret_kernel(p_ref, cos_ref, sin_ref, dec_ref, decin_ref, g_ref, o_ref, st_ref):
    @pl.when(pl.program_id(1) == 0)
    def _():
        st_ref[...] = jnp.zeros_like(st_ref)

    c = CHUNK
    w = MIX_WIDTH
    ones_bd = _block_ones(w)
    bd_mask = _block_ones(w, F32)
    gain = g_ref[...]
    dec_q = dec_ref[0:c, :]
    dec_k = dec_ref[c:2 * c, :]
    dec_state = dec_ref[2 * c:2 * c + 1, :]
    dec_in = decin_ref[...]

    def chunk(ci, carry):
        r0 = pl.multiple_of(ci * c, c)
        cos_t = cos_ref[pl.ds(r0, c), :]
        sin_t = sin_ref[pl.ds(r0, c), :]
        q = _rope(p_ref[pl.ds(r0, c), 0:w], cos_t, sin_t)
        k = _rope(p_ref[pl.ds(r0, c), w:2 * w], cos_t, sin_t) * (HEAD_DIM ** -0.5)
        v = p_ref[pl.ds(r0, c), 2 * w:3 * w]
        gp = p_ref[pl.ds(r0, c), 3 * w:4 * w]
        vb = v.astype(BF16)
        a = _dot_nt(_stack_heads(q).astype(BF16), k.astype(BF16)) * dec_in
        o = _unstack_heads(_dot(a.astype(BF16), vb), c)
        st = st_ref[...]
        o = o + _dot_nt(q.astype(BF16), st.astype(BF16)) * dec_q
        st_ref[...] = st * dec_state + bd_mask * _dot_tn(vb, (k * dec_k).astype(BF16))
        o_ref[pl.ds(r0, c), :] = _head_rms(o, gain, ones_bd) * _silu(gp)
        return carry

    lax.fori_loop(0, p_ref.shape[0] // c, chunk, 0)


def _ret_consts():
    log_gamma = np.log1p(-np.exp2(-5.0 - np.arange(N_HEADS, dtype=np.float64)))
    t = np.arange(CHUNK, dtype=np.float64)
    lane_h = np.arange(MIX_WIDTH) // HEAD_DIM
    dec_q = np.exp(log_gamma[lane_h][None, :] * (t[:, None] + 1.0))
    dec_k = np.exp(log_gamma[lane_h][None, :] * (CHUNK - 1.0 - t[:, None]))
    dec_state = np.exp(log_gamma[lane_h] * CHUNK)[None, :]
    dec = np.concatenate([dec_q, dec_k, np.broadcast_to(dec_state, (8, MIX_WIDTH))], 0)
    diff = t[:, None] - t[None, :]
    dec_in = np.concatenate([np.where(diff >= 0, np.exp(log_gamma[h] * diff), 0.0) for h in range(N_HEADS)], 0)
    return dec.astype(np.float32), dec_in.astype(np.float32)


def _ret(proj, cos4, sin4, gain, bsz, seq):
    dec, dec_in = _ret_consts()
    tb = min(REC_BLOCK, seq)
    nb = seq // tb
    return pl.pallas_call(
        _ret_kernel,
        grid=(bsz, nb),
        in_specs=[pl.BlockSpec((tb, 4 * MIX_WIDTH), lambda b, i: (b * nb + i, OFF_R // (4 * MIX_WIDTH))),
                  pl.BlockSpec((tb, MIX_WIDTH), lambda b, i: (i, 0)),
                  pl.BlockSpec((tb, MIX_WIDTH), lambda b, i: (i, 0)),
                  pl.BlockSpec(dec.shape, lambda b, i: (0, 0)),
                  pl.BlockSpec(dec_in.shape, lambda b, i: (0, 0)),
                  pl.BlockSpec((1, MIX_WIDTH), lambda b, i: (0, 0))],
        out_specs=pl.BlockSpec((tb, MIX_WIDTH), lambda b, i: (b * nb + i, 0)),
        out_shape=jax.ShapeDtypeStruct((bsz * seq, MIX_WIDTH), F32),
        scratch_shapes=[pltpu.VMEM((MIX_WIDTH, MIX_WIDTH), F32)],
        compiler_params=_cp("parallel", "arbitrary"),
        name="retention",
    )(proj, cos4, sin4, jnp.asarray(dec), jnp.asarray(dec_in), gain.reshape(1, MIX_WIDTH))


def _hgrn_lb_rows(lb):
    lb = lb.astype(F32)
    rows = jnp.stack([jnp.log(lb), jnp.log1p(-lb), 1.0 - lb], 0)
    return jnp.concatenate([rows, jnp.zeros((5, lb.shape[0]), F32)], 0)


def _rope_lane_tables(seq):
    inv = 1.0 / (ROPE_THETA ** (jnp.arange(0, HEAD_DIM, 2, dtype=F32) / HEAD_DIM))
    ang = jnp.arange(seq, dtype=F32)[:, None] * inv[None, :]
    cos, sin = jnp.cos(ang), jnp.sin(ang)
    cos_t = jnp.tile(cos, (1, LANES // (HEAD_DIM // 2)))
    sin_t = jnp.tile(jnp.concatenate([-sin, sin], axis=1), (1, LANES // HEAD_DIM))
    return cos_t, sin_t


def _expand_heads(cols, shape):
    hl = _head_of_lane(shape, 1)
    out = jnp.broadcast_to(cols[-1], shape)
    for h in range(len(cols) - 2, -1, -1):
        out = jnp.where(hl == h, jnp.broadcast_to(cols[h], shape), out)
    return out


def _mlstm_kernel(p_ref, gcol_ref, grow_ref, cw_ref, cb_ref, gbr_ref, gbc_ref, g_ref, o_ref,
                  ct_ref, n_ref, m_ref, hist_ref, cbuf_ref, qk_ref):
    c = CHUNK
    w = MIX_WIDTH
    tb = p_ref.shape[0]

    @pl.when(pl.program_id(1) == 0)
    def _():
        ct_ref[...] = jnp.zeros_like(ct_ref)
        n_ref[...] = jnp.zeros_like(n_ref)
        m_ref[...] = jnp.zeros_like(m_ref)
        hist_ref[...] = jnp.zeros_like(hist_ref)

    cbuf_ref[0:8, :] = hist_ref[...]
    cbuf_ref[8:, :] = p_ref[:, 0:2 * w]
    hist_ref[...] = p_ref[tb - 8:tb, 0:2 * w]
    acc = jnp.broadcast_to(cb_ref[...], (tb, 2 * w))
    for j in range(CONV_W):
        acc = acc + cw_ref[j:j + 1, :] * cbuf_ref[pl.ds(8 - (CONV_W - 1) + j, tb), :]
    qk_ref[...] = _silu(acc)

    ones_bd = _block_ones(w)
    bd_mask = _block_ones(w, F32)
    ti = lax.broadcasted_iota(jnp.int32, (c, c), 0)
    si = lax.broadcasted_iota(jnp.int32, (c, c), 1)
    causal = ti >= si
    tri = causal.astype(BF16)
    tri_t = (ti <= si).astype(BF16)
    gain = g_ref[...]
    ones_ext = jnp.ones((c, LANES), BF16)

    def chunk(ci, carry):
        r0 = pl.multiple_of(ci * c, c)
        q = qk_ref[pl.ds(r0, c), 0:w]
        k = qk_ref[pl.ds(r0, c), w:2 * w] * (HEAD_DIM ** -0.5)
        v = p_ref[pl.ds(r0, c), 2 * w:3 * w]
        op = p_ref[pl.ds(r0, c), 3 * w:4 * w]
        gc = gcol_ref[pl.ds(r0, c), :] + gbr_ref[...]
        gr = grow_ref[ci] + gbc_ref[...]
        b_c = _dot01_l(tri, _log_sigmoid(gc))
        b_r = _dot01_r(_log_sigmoid(gr), tri_t)
        wd, s_inter, em, wk, decay = [], [], [], [], []
        for h in range(N_HEADS):
            bc = b_c[:, N_HEADS + h:N_HEADS + h + 1]
            lic = gc[:, h:h + 1]
            br = b_r[N_HEADS + h:N_HEADS + h + 1, :]
            lir = gr[h:h + 1, :]
            dmat = jnp.where(causal, bc - br + lir, -jnp.inf)
            m_prev = m_ref[h:h + 1, 0:1]
            inter = bc + m_prev
            mrow = jnp.maximum(inter, jnp.max(dmat, axis=1, keepdims=True))
            wd.append(jnp.exp(dmat - mrow))
            s_inter.append(jnp.exp(inter - mrow))
            em.append(jnp.exp(-mrow))
            b_last = br[:, c - 1:c]
            m_new = jnp.maximum(b_last + m_prev, jnp.max(b_last - br + lir, axis=1, keepdims=True))
            wk.append(jnp.exp(b_last - bc + lic - m_new))
            decay.append(jnp.exp(b_last + m_prev - m_new))
            m_ref[h:h + 1, :] = jnp.broadcast_to(m_new, (1, LANES))
        s_inter_l = _expand_heads(s_inter, (c, w))
        em_l = _expand_heads(em, (c, w))
        wk_l = _expand_heads(wk, (c, w))
        decay_l = _expand_heads(decay, (1, w))
        qk = _dot_nt(_stack_heads(q).astype(BF16), k.astype(BF16))
        wmat = jnp.concatenate(wd, axis=0) * qk
        vb = v.astype(BF16)
        r = _dot(wmat.astype(BF16), jnp.concatenate([vb, ones_ext], axis=1))
        num_intra = _unstack_heads(r[:, 0:w], c)
        rs_l = _expand_heads([r[h * c:(h + 1) * c, w:w + 1] for h in range(N_HEADS)], (c, w))
        ct = ct_ref[...]
        nrow = n_ref[0:1, :]
        num = s_inter_l * _dot_nt(q.astype(BF16), ct.astype(BF16)) + num_intra
        den = s_inter_l * _group_sum(q * nrow, ones_bd) + rs_l
        hval = num / jnp.maximum(jnp.abs(den), em_l)
        kw = wk_l * k
        ct_ref[...] = ct * decay_l + bd_mask * _dot_tn(vb, kw.astype(BF16))
        n_ref[0:1, :] = nrow * decay_l + jnp.sum(kw, axis=0, keepdims=True)
        o_ref[pl.ds(r0, c), :] = _head_rms(hval, gain, ones_bd) * _sigmoid(op)
        return carry

    lax.fori_loop(0, tb // c, chunk, 0)


def _mlstm(proj, conv_w, conv_b, gate_b, gain, bsz, seq):
    t = bsz * seq
    w = MIX_WIDTH
    tb = min(REC_BLOCK, seq)
    nb = seq // tb
    ncb = tb // CHUNK
    grow = proj[:, OFF_MG:OFF_MG + 8].reshape(t // CHUNK, CHUNK, 8).transpose(0, 2, 1)
    gb_row = jnp.zeros((1, LANES), F32).at[0, 0:8].set(gate_b.astype(F32))
    gb_col = gate_b.astype(F32).reshape(8, 1)
    return pl.pallas_call(
        _mlstm_kernel,
        grid=(bsz, nb),
        in_specs=[pl.BlockSpec((tb, 4 * w), lambda b, i: (b * nb + i, OFF_M // (4 * w))),
                  pl.BlockSpec((tb, LANES), lambda b, i: (b * nb + i, OFF_MG // LANES)),
                  pl.BlockSpec((ncb, 8, CHUNK), lambda b, i: (b * nb + i, 0, 0)),
                  pl.BlockSpec((CONV_W, 2 * w), lambda b, i: (0, 0)),
                  pl.BlockSpec((1, 2 * w), lambda b, i: (0, 0)),
                  pl.BlockSpec((1, LANES), lambda b, i: (0, 0)),
                  pl.BlockSpec((8, 1), lambda b, i: (0, 0)),
                  pl.BlockSpec((1, w), lambda b, i: (0, 0))],
        out_specs=pl.BlockSpec((tb, w), lambda b, i: (b * nb + i, 0)),
        out_shape=jax.ShapeDtypeStruct((t, w), F32),
        scratch_shapes=[pltpu.VMEM((w, w), F32), pltpu.VMEM((8, w), F32), pltpu.VMEM((8, LANES), F32),
                        pltpu.VMEM((8, 2 * w), F32), pltpu.VMEM((tb + 8, 2 * w), F32),
                        pltpu.VMEM((tb, 2 * w), F32)],
        compiler_params=_cp("parallel", "arbitrary"),
        name="mlstm",
    )(proj, proj, grow, conv_w.astype(F32), conv_b.astype(F32).reshape(1, 2 * w), gb_row, gb_col,
      gain.reshape(1, w))


NSA_TQ = 128
NSA_KC = 512
GW = 2 * HEAD_DIM


def _nsa_prep_kernel(pq_ref, pk_ref, pv_ref, cos_ref, sin_ref, qg_ref, kg_ref,
                     qn_ref, qr_ref, ks_ref, kw_ref, vst_ref, vwt_ref):
    w = MIX_WIDTH
    for src, dst in ((pv_ref[:, 0:GW], vst_ref), (pv_ref[:, GW:2 * GW], vwt_ref)):
        vt = src.T
        tk = dst.shape[4]
        for g in range(NSA_GROUPS):
            rows = vt[g * HEAD_DIM:(g + 1) * HEAD_DIM, :]
            dup = jnp.concatenate([rows, rows], axis=0).astype(BF16)
            for j in range(dst.shape[2]):
                dst[0, g, j] = dup[:, j * tk:(j + 1) * tk]
    ones_bd = _block_ones(w)
    cos_t, sin_t = cos_ref[...], sin_ref[...]
    scale = HEAD_DIM ** -0.5
    qh = _head_rms(pq_ref[...], qg_ref[...], ones_bd)
    qn_ref[...] = (qh * scale).astype(BF16)
    qr_ref[...] = (_rope(qh, cos_t, sin_t) * scale).astype(BF16)
    ks_ref[...] = _rope(_head_rms(pk_ref[:, 0:w], kg_ref[1:2, :], ones_bd), cos_t, sin_t).astype(BF16)
    kw_ref[...] = _rope(_head_rms(pk_ref[:, w:2 * w], kg_ref[2:3, :], ones_bd), cos_t, sin_t).astype(BF16)


def _nsa_prep(proj, cos4, sin4, qnorm_g, knorm_g, bsz, seq):
    t = bsz * seq
    w = MIX_WIDTH
    tm = min(NSA_KC, seq)
    tq = min(NSA_TQ, seq)
    ns = seq // tm
    qg = jnp.tile(qnorm_g.astype(F32), w // HEAD_DIM).reshape(1, w)
    kg = jnp.concatenate([jnp.tile(knorm_g.astype(F32), (1, w // HEAD_DIM)), jnp.zeros((5, w), F32)], axis=0)
    out = jax.ShapeDtypeStruct((t, w), BF16)
    row = pl.BlockSpec((tm, w), lambda i: (i, 0))
    return pl.pallas_call(
        _nsa_prep_kernel,
        grid=(t // tm,),
        in_specs=[pl.BlockSpec((tm, w), lambda i: (i, OFF_NQ // w)),
                  pl.BlockSpec((tm, 2 * w), lambda i: (i, OFF_KD // (2 * w))),
                  pl.BlockSpec((tm, 2 * GW), lambda i: (i, OFF_V // (2 * GW))),
                  pl.BlockSpec((tm, w), lambda i: (i % ns, 0)),
                  pl.BlockSpec((tm, w), lambda i: (i % ns, 0)),
                  pl.BlockSpec((1, w), lambda i: (0, 0)),
                  pl.BlockSpec((8, w), lambda i: (0, 0))],
        out_specs=[row, row, row, row,
                   pl.BlockSpec((1, NSA_GROUPS, 1, GW, tm), lambda i: (i // ns, 0, i % ns, 0, 0)),
                   pl.BlockSpec((1, NSA_GROUPS, tm // tq, GW, tq), lambda i: (i // ns, 0, i % ns, 0, 0))],
        out_shape=[out, out, out, out,
                   jax.ShapeDtypeStruct((bsz, NSA_GROUPS, seq // tm, GW, tm), BF16),
                   jax.ShapeDtypeStruct((bsz, NSA_GROUPS, seq // tq, GW, tq), BF16)],
        compiler_params=_cp("parallel"),
        name="nsa_prep",
    )(proj, proj, proj, cos4, sin4, qg, kg)


def _nsa_cmp_kernel(xr_ref, pe_ref, w0_ref, w1_ref, kg_ref, ovt_ref, qn_ref, ocmp_ref, sel_ref,
                    kc_ref, vc_ref, v_ref, *, n_top):
    tq = qn_ref.shape[0]
    nr = xr_ref.shape[0]
    nsel = sel_ref.shape[2]
    w = MIX_WIDTH

    @pl.when(pl.program_id(1) == 0)
    def _():
        xr = xr_ref[...]
        y0 = _dot((xr + pe_ref[0]).astype(BF16), w0_ref[...])
        y1 = _dot((xr + pe_ref[1]).astype(BF16), w1_ref[...])
        kv = y0 + pltpu.roll(y1, nr - 1, 0)
        kc_ref[...] = _head_rms(kv[:, 0:w], kg_ref[...], _block_ones(w)).astype(BF16)
        vc_ref[...] = kv[:, w:2 * w].astype(BF16)

    pos0 = pl.program_id(1) * tq
    hl = _head_of_lane((tq, GW), 1)
    pos_r = pos0 + lax.broadcasted_iota(jnp.int32, (tq, nr), 0)
    valid = lax.broadcasted_iota(jnp.int32, (tq, nr), 1) * CMP_STRIDE + (CMP_LEN - 1) <= pos_r
    pos_c = pos0 + lax.broadcasted_iota(jnp.int32, (nr, tq), 1)
    valid_t = lax.broadcasted_iota(jnp.int32, (nr, tq), 0) * CMP_STRIDE + (CMP_LEN - 1) <= pos_c
    jrow = lax.broadcasted_iota(jnp.int32, (nsel, tq), 0)
    cur = (pos0 + lax.broadcasted_iota(jnp.int32, (nsel, tq), 1)) // SEL_BLOCK
    forced = (jrow == 0) | (jrow == cur) | (jrow == cur - 1)
    ovt = ovt_ref[...]

    for g in range(NSA_GROUPS):
        qg = qn_ref[:, g * GW:(g + 1) * GW]
        kg = kc_ref[:, g * GW:(g + 1) * GW]
        vg = vc_ref[:, g * GW:(g + 1) * GW]
        o_g = jnp.zeros((tq, GW), F32)
        pt_sum = jnp.zeros((nr, tq), F32)
        for hh in range(2):
            qm = jnp.where(hl == hh, qg, jnp.zeros_like(qg))
            s = jnp.where(valid, _dot_nt(qm, kg), NEG_INF)
            e = jnp.exp(s - jnp.max(s, axis=1, keepdims=True))
            p = jnp.where(valid, e / jnp.sum(e, axis=1, keepdims=True), 0.0)
            o_g = jnp.where(hl == hh, _dot(p.astype(BF16), vg), o_g)
            st = jnp.where(valid_t, _dot_nt(kg, qm), NEG_INF)
            et = jnp.exp(st - jnp.max(st, axis=0, keepdims=True))
            pt_sum = pt_sum + jnp.where(valid_t, et / jnp.sum(et, axis=0, keepdims=True), 0.0)
        ocmp_ref[:, g * GW:(g + 1) * GW] = o_g
        p_hi = pt_sum.astype(BF16)
        p_lo = (pt_sum - p_hi.astype(F32)).astype(BF16)
        imp = _dot(ovt, p_hi) + _dot(ovt, p_lo)
        val = jnp.where(jrow <= cur, imp + FORCE_BONUS * forced.astype(F32), NEG_INF)
        v_ref[...] = val

        def rank(jp, cnt):
            row = v_ref[pl.ds(jp, 1), :]
            tie = jnp.where(jrow > jp, 1.0, 0.0)
            return cnt + jnp.where(row > val, 1.0, jnp.where(row == val, tie, 0.0))

        cnt = lax.fori_loop(0, nsel, rank, jnp.zeros((nsel, tq), F32))
        sel_ref[0, g] = ((cnt < n_top) & (jrow <= cur)).astype(F32)


def _nsa_cmp_weights(cmp_pe, cmp_w):
    half = CMP_LEN // 2
    wl = cmp_w.astype(F32).reshape(2, 2, half, HEAD_DIM, HEAD_DIM)
    eye2 = jnp.eye(2, dtype=F32)
    w2 = jnp.einsum('kardz,kK,gG,h->arkgdKGhz', wl, eye2, eye2, jnp.ones((2,), F32))
    w2 = w2.reshape(2, half * 4 * HEAD_DIM, 8 * HEAD_DIM)
    pl_ = cmp_pe.astype(F32).reshape(2, 2, half, HEAD_DIM)
    pe2 = jnp.broadcast_to(pl_.transpose(1, 2, 0, 3)[:, :, :, None, :], (2, half, 2, 2, HEAD_DIM))
    return w2.astype(BF16), pe2.reshape(2, 1, half * 4 * HEAD_DIM)


def _nsa_cmp(kcvc, qn, cmp_pe, cmp_w, knorm0, bsz, seq, tq=512):
    t = bsz * seq
    w = MIX_WIDTH
    tq = min(tq, seq)
    nq = seq // tq
    nr = seq // CMP_STRIDE
    nsel = seq // SEL_BLOCK
    n_top = min(SEL_TOPK, nsel)
    w2, pe2 = _nsa_cmp_weights(cmp_pe, cmp_w)
    xr = kcvc.reshape(t // CMP_STRIDE, CMP_STRIDE * w)
    kg = jnp.tile(knorm0.astype(F32), w // HEAD_DIM).reshape(1, w)
    n_i = np.arange(nr)[:, None] * CMP_STRIDE
    j_i = np.arange(nsel)[None, :] * SEL_BLOCK
    ov = ((n_i < j_i + SEL_BLOCK) & (n_i + CMP_LEN > j_i)).astype(np.float32)
    ov[nr - 1, :] = 0.0
    kin = CMP_STRIDE * w
    return pl.pallas_call(
        functools.partial(_nsa_cmp_kernel, n_top=n_top),
        grid=(bsz, nq),
        in_specs=[pl.BlockSpec((nr, kin), lambda b, i: (b, 0)),
                  pl.BlockSpec((2, 1, kin), lambda b, i: (0, 0, 0)),
                  pl.BlockSpec((None, kin, 2 * w), lambda b, i: (0, 0, 0)),
                  pl.BlockSpec((None, kin, 2 * w), lambda b, i: (1, 0, 0)),
                  pl.BlockSpec((1, w), lambda b, i: (0, 0)),
                  pl.BlockSpec((nsel, nr), lambda b, i: (0, 0)),
                  pl.BlockSpec((tq, w), lambda b, i: (b * nq + i, 0))],
        out_specs=[pl.BlockSpec((tq, w), lambda b, i: (b * nq + i, 0)),
                   pl.BlockSpec((1, NSA_GROUPS, nsel, tq), lambda b, i: (b, 0, 0, i))],
        out_shape=[jax.ShapeDtypeStruct((t, w), F32),
                   jax.ShapeDtypeStruct((bsz, NSA_GROUPS, nsel, seq), F32)],
        scratch_shapes=[pltpu.VMEM((nr, w), BF16), pltpu.VMEM((nr, w), BF16), pltpu.VMEM((nsel, tq), F32)],
        compiler_params=_cp("parallel", "arbitrary"),
        name="nsa_cmp",
    )(xr, pe2, w2, w2, kg, jnp.asarray(ov.T, BF16), qn)


def _nsa_attn_kernel(qr_ref, ks_ref, kw_ref, vs_ref, vw_ref, sel_ref, ocmp_ref, gate_ref, o_ref, *, kc, wt):
    tq = qr_ref.shape[0]
    i = pl.program_id(1)
    hl = _head_of_lane((tq, GW), 1)
    nbk = kc // SEL_BLOCK
    groups = range(NSA_GROUPS)

    def stacked_q(g):
        q = qr_ref[:, g * GW:(g + 1) * GW]
        return jnp.concatenate([jnp.where(hl == 0, q, jnp.zeros_like(q)), jnp.where(hl == 1, q, jnp.zeros_like(q))],
                               axis=0)

    qs_all = [stacked_q(g) for g in groups]

    def lane_qpos(rows):
        return i * tq + lax.broadcasted_iota(jnp.int32, (rows, 2 * tq), 1) % tq

    def finish(acc, l):
        ot = (acc / l).T
        return jnp.where(hl == 0, ot[0:tq, :], ot[tq:2 * tq, :])

    qpos_s = lane_qpos(kc)
    krow_s = lax.broadcasted_iota(jnp.int32, (kc, 2 * tq), 0)

    def sel_step(g, c, carry, diagonal):
        m, l, acc = carry
        k0 = pl.multiple_of(c * kc, kc)
        st = _dot_nt(ks_ref[pl.ds(k0, kc), g * GW:(g + 1) * GW], qs_all[g])
        srows = sel_ref[0, g, pl.ds(pl.multiple_of(c * nbk, nbk), nbk), :]
        srows = jnp.concatenate([srows, srows], axis=1)
        smask = jnp.concatenate([jnp.broadcast_to(srows[r:r + 1, :], (SEL_BLOCK, 2 * tq)) for r in range(nbk)],
                                axis=0)
        msk = smask > 0.5
        if diagonal:
            msk = msk & (k0 + krow_s <= qpos_s)
        st = jnp.where(msk, st, NEG_INF)
        m_new = jnp.maximum(m, jnp.max(st, axis=0, keepdims=True))
        p = jnp.exp(st - m_new)
        alpha = jnp.exp(m - m_new)
        l = l * alpha + jnp.sum(p, axis=0, keepdims=True)
        acc = acc * alpha + _dot(vs_ref[0, g, c], p.astype(BF16))
        return m_new, l, acc

    def sel_body(c, carries, diagonal):
        return tuple(sel_step(g, c, carries[g], diagonal) for g in groups)

    init = (jnp.full((1, 2 * tq), NEG_INF, F32), jnp.zeros((1, 2 * tq), F32), jnp.zeros((GW, 2 * tq), F32))
    n_before = (i * tq) // kc
    carries = lax.fori_loop(0, n_before, functools.partial(sel_body, diagonal=False), (init,) * NSA_GROUPS)
    carries = sel_body(n_before, carries, True)

    j0 = jnp.maximum(i - (wt - 1), 0)
    k0 = pl.multiple_of(j0 * tq, tq)
    span = wt * tq
    kpos = k0 + lax.broadcasted_iota(jnp.int32, (span, 2 * tq), 0)
    qpos_w = lane_qpos(span)
    wmask = (kpos <= qpos_w) & (kpos > qpos_w - WINDOW)
    gb = _sigmoid(gate_ref[...])
    for g in groups:
        _, l_s, acc_s = carries[g]
        o_sel = finish(acc_s, l_s)
        st = jnp.where(wmask, _dot_nt(kw_ref[pl.ds(k0, span), g * GW:(g + 1) * GW], qs_all[g]), NEG_INF)
        p = jnp.exp(st - jnp.max(st, axis=0, keepdims=True))
        vt = jnp.concatenate([vw_ref[0, g, j0 + r] for r in range(wt)], axis=1)
        o_win = finish(_dot(vt, p.astype(BF16)), jnp.sum(p, axis=0, keepdims=True))

        def gate(branch):
            cols = [gb[:, (2 * g + hh) * 3 + branch:(2 * g + hh) * 3 + branch + 1] for hh in range(2)]
            return _expand_heads(cols, (tq, GW))

        o_ref[:, g * GW:(g + 1) * GW] = (gate(0) * ocmp_ref[:, g * GW:(g + 1) * GW]
                                         + gate(1) * o_sel + gate(2) * o_win)


def _nsa_attn(proj, qr, ks, kw, vst, vwt, sel, o_cmp, bsz, seq):
    t = bsz * seq
    w = MIX_WIDTH
    tq = min(NSA_TQ, seq)
    nq = seq // tq
    nsel = seq // SEL_BLOCK
    kc = min(NSA_KC, seq)
    wt = min(WINDOW // tq + 1, nq)
    kspec = pl.BlockSpec((seq, w), lambda b, i: (b, 0))
    return pl.pallas_call(
        functools.partial(_nsa_attn_kernel, kc=kc, wt=wt),
        grid=(bsz, nq),
        in_specs=[pl.BlockSpec((tq, w), lambda b, i: (b * nq + i, 0)),
                  kspec, kspec,
                  pl.BlockSpec((1, NSA_GROUPS, seq // kc, GW, kc), lambda b, i: (b, 0, 0, 0, 0)),
                  pl.BlockSpec((1, NSA_GROUPS, nq, GW, tq), lambda b, i: (b, 0, 0, 0, 0)),
                  pl.BlockSpec((1, NSA_GROUPS, nsel, tq), lambda b, i: (b, 0, 0, i)),
                  pl.BlockSpec((tq, w), lambda b, i: (b * nq + i, 0)),
                  pl.BlockSpec((tq, LANES), lambda b, i: (b * nq + i, OFF_NG // LANES))],
        out_specs=pl.BlockSpec((tq, w), lambda b, i: (b * nq + i, 0)),
        out_shape=jax.ShapeDtypeStruct((t, w), F32),
        compiler_params=_cp("parallel", "arbitrary"),
        name="nsa_attn",
    )(qr, ks, kw, vst, vwt, sel, o_cmp, proj)


def _nsa(proj, kcvc, cos4, sin4, qnorm_g, knorm_g, cmp_pe, cmp_w, bsz, seq):
    qn, qr, ks, kw, vst, vwt = _nsa_prep(proj, cos4, sin4, qnorm_g, knorm_g, bsz, seq)
    o_cmp, sel = _nsa_cmp(kcvc, qn, cmp_pe, cmp_w, knorm_g[0], bsz, seq)
    return _nsa_attn(proj, qr, ks, kw, vst, vwt, sel, o_cmp, bsz, seq)


PEER_TT = 128
PEER_CT = 8
HALF_D = 512


SUBLANES = 8
CODE_BITS = 127
FAR_BELOW = -3.0e38


def _with_code(x, code):
    bits = lax.bitcast_convert_type(x, jnp.int32)
    return lax.bitcast_convert_type((bits & ~CODE_BITS) | code, F32)


def _split_code(x):
    bits = lax.bitcast_convert_type(x, jnp.int32)
    return lax.bitcast_convert_type(bits & ~CODE_BITS, F32), bits & CODE_BITS


def _sort16_desc(xs):
    xs = list(xs)
    n = len(xs)
    k = 2
    while k <= n:
        j = k // 2
        while j >= 1:
            for i in range(n):
                l = i ^ j
                if l > i:
                    hi, lo = jnp.maximum(xs[i], xs[l]), jnp.minimum(xs[i], xs[l])
                    xs[i], xs[l] = (hi, lo) if (i & k) == 0 else (lo, hi)
            j //= 2
        k *= 2
    return xs


def _merge16_desc(xs):
    xs = list(xs)
    j = len(xs) // 2
    while j >= 1:
        for i in range(len(xs)):
            l = i ^ j
            if l > i:
                xs[i], xs[l] = jnp.maximum(xs[i], xs[l]), jnp.minimum(xs[i], xs[l])
        j //= 2
    return xs


def _top16_columns(x):
    n = PEER_TOPK
    xs = _sort16_desc([x[SUBLANES * j:SUBLANES * (j + 1), :] for j in range(n)])
    shift = SUBLANES // 2
    while shift >= 1:
        rolled = [pltpu.roll(a, shift, 0) for a in xs]
        xs = _merge16_desc([jnp.maximum(xs[i], rolled[n - 1 - i]) for i in range(n)])
        shift //= 2
    return xs


_PEER_CAND_TILES = ((0, 0, 8), (0, 1, 8), (1, 0, 8), (2, 0, 5), (3, 0, 4), (4, 0, 3), (5, 0, 2), (6, 0, 2), (7, 0, 2))


ROUTE_HEADS_PER_STEP = 4


def _route_head(q_ref, key_ref, hh):
    tt = q_ref.shape[0]
    nk = PEER_NKEYS
    n = PEER_TOPK
    row = lax.broadcasted_iota(jnp.int32, (nk, tt), 0)
    sub = lax.broadcasted_iota(jnp.int32, (SUBLANES, tt), 0)
    vals, ids = [], []
    for p in range(2):
        c0 = (2 * hh + p) * PEER_KDIM
        st = _dot_nt(key_ref[hh, p], q_ref[:, c0:c0 + PEER_KDIM])
        top = [_split_code(a) for a in _top16_columns(_with_code(st, (nk - 1) - row))]
        vals.append([v for v, _ in top])
        ids.append([(nk - 1) - c for _, c in top])
    (v1, v2), (i1, i2) = vals, ids

    def stack(xs, lo):
        out = xs[lo]
        for s in range(1, SUBLANES):
            out = jnp.where(sub == s, xs[lo + s], out)
        return out

    v2t, i2t = (stack(v2, 0), stack(v2, SUBLANES)), (stack(i2, 0), stack(i2, SUBLANES))
    cand, cexp = [], []
    for a, tile, nvalid in _PEER_CAND_TILES:
        v = v1[a] + v2t[tile]
        cand.append(v if nvalid == SUBLANES else jnp.where(sub < nvalid, v, FAR_BELOW))
        cexp.append(i1[a] * nk + i2t[tile])
    cand.append(stack(v1, SUBLANES) + v2[0])
    cexp.append(stack(i1, SUBLANES) * nk + i2[0])
    n_tiles = len(cand)
    slot_code = [(nk - 1) - (c * SUBLANES + sub) for c in range(n_tiles)]
    coded = [_with_code(v, sc) for v, sc in zip(cand, slot_code)]
    coded += [jnp.full((SUBLANES, tt), FAR_BELOW, F32)] * (n - n_tiles)
    top = [_split_code(a) for a in _top16_columns(jnp.concatenate(coded, axis=0))]
    call = jnp.concatenate(cexp, axis=0)
    slot = (nk - 1) - lax.broadcasted_iota(jnp.int32, call.shape, 0)
    ex = [jnp.exp(v - top[0][0]) for v, _ in top]
    tot = ex[0]
    for k in range(1, n):
        tot = tot + ex[k]
    krow = lax.broadcasted_iota(jnp.int32, (n, tt), 0)
    e_tile = jnp.zeros((n, tt), F32)
    g_tile = jnp.zeros((n, tt), F32)
    for k in range(n):
        hit = slot == jnp.concatenate([top[k][1]] * n_tiles, axis=0)
        e_k = jnp.sum(jnp.where(hit, call, 0), axis=0, keepdims=True)
        e_tile = jnp.where(krow == k, e_k.astype(F32), e_tile)
        g_tile = jnp.where(krow == k, (ex[k] / tot)[0:1, :], g_tile)
    return e_tile, g_tile


def _peer_route_kernel(q_ref, key_ref, e_ref, g_ref, e_scr, g_scr):
    hps = key_ref.shape[0]
    tiles = [_route_head(q_ref, key_ref, hh) for hh in range(hps)]
    rows = hps * PEER_TOPK
    r0 = pl.multiple_of(pl.program_id(1) * rows, rows)
    e_scr[pl.ds(r0, rows), :] = jnp.concatenate([e for e, _ in tiles], axis=0)
    g_scr[pl.ds(r0, rows), :] = jnp.concatenate([g for _, g in tiles], axis=0)

    @pl.when(pl.program_id(1) == pl.num_programs(1) - 1)
    def _():
        e_ref[...] = e_scr[...].T.astype(jnp.int32)
        g_ref[...] = g_scr[...].T


def _peer_route(qp, keys):
    t = qp.shape[0]
    tt = min(PEER_TT, t)
    ne = PEER_HEADS * PEER_TOPK
    hps = ROUTE_HEADS_PER_STEP
    return pl.pallas_call(
        _peer_route_kernel,
        grid=(t // tt, PEER_HEADS // hps),
        in_specs=[pl.BlockSpec((tt, hps * 2 * PEER_KDIM), lambda i, h: (i, h)),
                  pl.BlockSpec((hps, 2, PEER_NKEYS, PEER_KDIM), lambda i, h: (h, 0, 0, 0))],
        out_specs=[pl.BlockSpec((tt, ne), lambda i, h: (i, 0)),
                   pl.BlockSpec((tt, ne), lambda i, h: (i, 0))],
        out_shape=[jax.ShapeDtypeStruct((t, ne), jnp.int32),
                   jax.ShapeDtypeStruct((t, ne), F32)],
        scratch_shapes=[pltpu.VMEM((ne, tt), F32), pltpu.VMEM((ne, tt), F32)],
        compiler_params=_cp("parallel", "arbitrary"),
        name="peer_route",
    )(qp, keys)


def _pack_tables_kernel(u_ref, v_ref, o_ref):
    def pack(x):
        lo = lax.bitcast_convert_type(x[:, 0:HALF_D].astype(BF16).astype(F32), jnp.int32)
        hi = lax.bitcast_convert_type(x[:, HALF_D:2 * HALF_D].astype(BF16).astype(F32), jnp.int32)
        return lax.shift_right_logical(lo, 16) | (hi & jnp.int32(-65536))

    o_ref[:, 0:HALF_D] = pack(u_ref[...])
    o_ref[:, HALF_D:2 * HALF_D] = pack(v_ref[...])


def _pack_tables(u_tabs, v_tabs, layer, tr=512):
    _, e, d = u_tabs.shape
    assert d == 2 * HALF_D
    spec_in = pl.BlockSpec((None, tr, d), lambda i: (layer, i, 0))
    spec = pl.BlockSpec((tr, d), lambda i: (i, 0))
    return pl.pallas_call(
        _pack_tables_kernel,
        grid=(e // tr,),
        in_specs=[spec_in, spec_in],
        out_specs=spec,
        out_shape=jax.ShapeDtypeStruct((e, d), jnp.int32),
        compiler_params=_cp("parallel"),
        name="peer_pack",
    )(u_tabs, v_tabs)


def _unpack_rows(wd):
    lo = lax.bitcast_convert_type(lax.shift_left(wd, 16), F32)
    hi = lax.bitcast_convert_type(lax.bitwise_and(wd, jnp.int32(-65536)), F32)
    return lo, hi


SC_WINDOWS = (16,)


def _sc_gather(table, idx, window):
    from jax.experimental.pallas import tpu_sc as plsc
    n = idx.shape[0]
    width = table.shape[1]
    mesh = plsc.VectorSubcoreMesh(core_axis_name="core", subcore_axis_name="subcore")

    @functools.partial(pl.kernel, out_type=jax.ShapeDtypeStruct((n, width), table.dtype), mesh=mesh)
    def gather(tab_hbm, idx_hbm, out_hbm):
        def body(idx_vmem, out_vmem):
            pltpu.sync_copy(tab_hbm.at[idx_vmem.at[0, pl.ds(0, window)]], out_vmem)

        pltpu.emit_pipeline(
            body,
            grid=(n // window,),
            in_specs=[pl.BlockSpec((1, LANES), lambda i: (0, i))],
            out_specs=[pl.BlockSpec((window, width), lambda i: (i, 0))],
            core_axis_name=("core", "subcore"),
            dimension_semantics=(pltpu.PARALLEL,),
            trace_scopes=False,
        )(idx_hbm, out_hbm)

    idx_pad = jnp.pad(idx.reshape(n // window, window), ((0, 0), (0, LANES - window)))
    return gather(table, idx_pad.reshape(1, (n // window) * LANES))


def _peer_combine_kernel(x_ref, g2_ref, rows_a_ref, rows_b_ref, gate_ref, o_ref):
    ne = PEER_HEADS * PEER_TOPK
    x = x_ref[...]
    ct = x.shape[0]
    xn = x * lax.rsqrt(jnp.mean(x * x, axis=-1, keepdims=True) + NORM_EPS) * g2_ref[...]
    gate_t = jnp.concatenate([gate_ref[...]] * (ne // ct), axis=0).T
    for jj in range(ct):
        rows_ref, j = (rows_a_ref, jj) if jj < ct // 2 else (rows_b_ref, jj - ct // 2)
        u_lo, u_hi = _unpack_rows(rows_ref[j * ne:(j + 1) * ne, 0:HALF_D])
        xr = xn[jj:jj + 1, :]
        h = jnp.sum(u_lo * xr[:, 0:HALF_D] + u_hi * xr[:, HALF_D:2 * HALF_D], axis=1, keepdims=True)
        act = 0.5 * h * (1.0 + lax.erf(h * (2.0 ** -0.5)))
        wgt = gate_t[:, jj:jj + 1] * act
        v_lo, v_hi = _unpack_rows(rows_ref[j * ne:(j + 1) * ne, HALF_D:2 * HALF_D])
        o_ref[jj:jj + 1, 0:HALF_D] = x[jj:jj + 1, 0:HALF_D] + jnp.sum(wgt * v_lo, axis=0, keepdims=True)
        o_ref[jj:jj + 1, HALF_D:2 * HALF_D] = (x[jj:jj + 1, HALF_D:2 * HALF_D]
                                               + jnp.sum(wgt * v_hi, axis=0, keepdims=True))


def _peer_combine(x, g2, rows, gates, first_token):
    t, d = x.shape
    ne = PEER_HEADS * PEER_TOPK
    ct = PEER_CT
    steps = rows.shape[0] // (ct * ne)
    off = first_token // ct
    deep = pl.Buffered(4)

    def streamed(x_hbm, g2_hbm, rows_a_hbm, rows_b_hbm, gates_hbm, o_hbm):
        pltpu.emit_pipeline(
            _peer_combine_kernel,
            grid=(steps,),
            in_specs=[pl.BlockSpec((ct, d), lambda i: (off + i, 0)),
                      pl.BlockSpec((1, d), lambda i: (0, 0)),
                      pl.BlockSpec((ct * ne // 2, d), lambda i: (2 * i, 0), pipeline_mode=deep),
                      pl.BlockSpec((ct * ne // 2, d), lambda i: (2 * i + 1, 0), pipeline_mode=deep),
                      pl.BlockSpec((ct, ne), lambda i: (off + i, 0))],
            out_specs=[pl.BlockSpec((ct, d), lambda i: (off + i, 0))],
        )(x_hbm, g2_hbm, rows_a_hbm, rows_b_hbm, gates_hbm, o_hbm)

    anywhere = pl.BlockSpec(memory_space=pl.ANY)
    return pl.pallas_call(
        streamed,
        in_specs=[anywhere] * 5,
        out_specs=anywhere,
        out_shape=jax.ShapeDtypeStruct((t, d), F32),
        input_output_aliases={0: 0},
        compiler_params=pltpu.CompilerParams(vmem_limit_bytes=VMEM_LIMIT),
        name="peer_combine",
    )(x, g2.reshape(1, d), rows, rows, gates)


PEER_TOKENS_PER_GATHER = 2048


def _peer_route_stage(x, g2, wq_b, keys_b):
    t = x.shape[0]
    ne = PEER_HEADS * PEER_TOPK
    qp = _norm_matmul(x, g2, wq_b, out_dtype=BF16)
    e_tok, g_tok = _peer_route(qp, keys_b)
    return e_tok.reshape(t * ne), g_tok


def _peer_gather_stage(table, idx, t, gather_fn):
    ne = PEER_HEADS * PEER_TOPK
    tc = min(PEER_TOKENS_PER_GATHER, t)
    return [gather_fn(table, idx[c * tc * ne:(c + 1) * tc * ne], SC_WINDOWS[c % len(SC_WINDOWS)])
            for c in range(t // tc)]


def _peer_combine_stage(x, g2, rows_list, gates):
    tc = x.shape[0] // len(rows_list)
    for c, rows in enumerate(rows_list):
        x = _peer_combine(x, g2, rows, gates, c * tc)
    return x


def _peer(x, g2, wq, keys, u_tabs, v_tabs, layer, gather_fn):
    idx, gates = _peer_route_stage(x, g2, wq.astype(BF16), keys.astype(BF16))
    table = _pack_tables(u_tabs, v_tabs, layer)
    rows_list = _peer_gather_stage(table, idx, x.shape[0], gather_fn)
    return _peer_combine_stage(x, g2, rows_list, gates)


_IN_WIDTHS = (256, 256, 256, 256, 256, 256, 256, 4, 4, 256, 256, 128, 128, 128, 128, 128, 128, 12,
              256, 256, 256, 256)


def _dup_groups(wcols):
    g0, g1 = wcols[:, :HEAD_DIM], wcols[:, HEAD_DIM:]
    return jnp.concatenate([g0, g0, g1, g1], axis=1)


def _layout_w_in(w_in):
    offs = np.cumsum((0,) + _IN_WIDTHS)
    cols = [w_in[:, offs[i]:offs[i + 1]] for i in range(len(_IN_WIDTHS))]
    (hq, hf, hi, hg, mq, mk, mv, mi, mf, mo, nq, nkc, nvc, nks, nvs, nkw, nvw, ng, rq, rk, rv, rg) = cols
    d = w_in.shape[0]
    pad = lambda c, n: jnp.concatenate([c, jnp.zeros((d, n - c.shape[1]), w_in.dtype)], axis=1)
    main = jnp.concatenate([hq, hf, hi, hg, mq, mk, mv, mo, rq, rk, rv, rg,
                            _dup_groups(nks), _dup_groups(nkw), nq, nvs, nvw,
                            pad(jnp.concatenate([mi, mf], axis=1), LANES), pad(ng, LANES)], axis=1)
    assert main.shape[1] == N_MAIN
    kcvc = jnp.concatenate([nkc, nvc], axis=1)
    return main.astype(BF16), kcvc.astype(BF16)


def kernel(x, norm1_g, w_in, hgrn_lb, hgrn_onorm_g, mlstm_conv_w, mlstm_conv_b, mlstm_gate_b, mlstm_onorm_g, nsa_qnorm_g, nsa_knorm_g, nsa_cmp_pe, nsa_cmp_w, ret_onorm_g, w_up, w_gate, w_out, norm2_g, peer_wq, peer_keys, peer_u, peer_v):
    bsz, seq, d = x.shape
    t = bsz * seq
    depth = w_in.shape[0]
    cos_t, sin_t = _rope_lane_tables(seq)
    cos4, sin4 = jnp.tile(cos_t, (1, 2)), jnp.tile(sin_t, (1, 2))
    lb_cum = jnp.cumsum(jax.nn.softmax(hgrn_lb.astype(F32), axis=0), axis=0)
    lb_all = lb_cum - lb_cum[0:1]
    weights = []
    for l in range(depth):
        w_main, w_kcvc = _layout_w_in(w_in[l])
        weights.append(dict(
            main=w_main, kcvc=w_kcvc, gate=w_gate[l].astype(BF16), up=w_up[l].astype(BF16),
            out=w_out[l].astype(BF16), lb=_hgrn_lb_rows(lb_all[l]), wq=peer_wq[l].astype(BF16),
            keys=peer_keys[l].astype(BF16), table=_pack_tables(peer_u, peer_v, l)))

    def mixer_steps(xh, l, nb):
        wl = weights[l]
        st = {}

        def proj(dep):
            st["proj"] = _norm_matmul(xh, _after(norm1_g[l], dep), wl["main"])
            return st["proj"]

        def gates(dep):
            st["gates"] = _norm_matmul(xh, _after(norm1_g[l], dep), wl["gate"], act="sigmoid", out_dtype=BF16)
            return st["gates"]

        def hgrn(dep):
            st["oh"] = _hgrn(st["proj"], wl["lb"], _after(hgrn_onorm_g[l], dep), nb, seq)
            return st["oh"]

        def mlstm(dep):
            st["om"] = _mlstm(st["proj"], mlstm_conv_w[l], mlstm_conv_b[l], mlstm_gate_b[l],
                              _after(mlstm_onorm_g[l], dep), nb, seq)
            return st["om"]

        def ret(dep):
            st["or"] = _ret(st["proj"], cos4, sin4, _after(ret_onorm_g[l], dep), nb, seq)
            return st["or"]

        def nsa_front(dep):
            kcvc = _norm_matmul(xh, _after(norm1_g[l], dep), wl["kcvc"])
            qn, qr, ks, kw, vst, vwt = _nsa_prep(st["proj"], cos4, sin4, nsa_qnorm_g[l], nsa_knorm_g[l], nb, seq)
            o_cmp, sel = _nsa_cmp(kcvc, qn, nsa_cmp_pe[l], nsa_cmp_w[l], nsa_knorm_g[l][0], nb, seq)
            st["nsa"] = (qr, ks, kw, vst, vwt, sel, o_cmp)
            return o_cmp

        def nsa_attn(dep):
            del dep
            st["on"] = _nsa_attn(st["proj"], *st["nsa"], nb, seq)
            return st["on"]

        def merge(dep):
            del dep
            st["xm"] = _merge(xh, st["gates"], (st["oh"], st["om"], st["on"], st["or"]), wl["up"], wl["out"])
            return st["xm"]

        def route(dep):
            st["idx"], st["pgates"] = _peer_route_stage(st["xm"], _after(norm2_g[l], dep), wl["wq"], wl["keys"])
            return st["pgates"]

        return [proj, gates, hgrn, mlstm, ret, nsa_front, nsa_attn, merge, route], st

    combine_slots = (0, 8, 8, 8, 8, 8, 8, 8)

    def combine_steps(l, xm, rows_list, pgates):
        box = {"x": xm}
        tc = xm.shape[0] // len(rows_list)

        def make(c):
            def step(dep):
                box["x"] = _peer_combine(box["x"], _after(norm2_g[l], dep), rows_list[c], pgates, c * tc)
                return box["x"]
            return step

        return [make(c) for c in range(len(rows_list))], box

    n_groups = next(n for n in (8, 4, 2, 1) if bsz % n == 0)
    nb = bsz // n_groups
    xs = [x[g * nb:(g + 1) * nb].reshape(nb * seq, d) for g in range(n_groups)]
    dep = None
    lag = min(2, n_groups - 1)
    pending = []
    for l in range(depth):
        for g in range(n_groups):
            msteps, st = mixer_steps(xs[g], l, nb)
            due = pending.pop(0) if len(pending) == lag and lag > 0 else None
            csteps = due[1] if due is not None else []
            ci = 0
            for si, mstep in enumerate(msteps):
                dep = mstep(dep)
                while ci < len(csteps) and (ci >= len(combine_slots) or combine_slots[ci] <= si):
                    dep = csteps[ci](dep)
                    ci += 1
            for cstep in csteps[ci:]:
                dep = cstep(dep)
            if due is not None:
                xs[due[0]] = due[2]["x"]
            rows_list = _peer_gather_stage(weights[l]["table"], st["idx"], nb * seq, _sc_gather)
            csteps, box = combine_steps(l, st["xm"], rows_list, st["pgates"])
            pending.append((g, csteps, box))
            if lag == 0:
                for cstep in pending.pop(0)[1]:
                    dep = cstep(dep)
                xs[g] = box["x"]
    for pg, csteps, box in pending:
        for cstep in csteps:
            dep = cstep(dep)
        xs[pg] = box["x"]
    return jnp.concatenate(xs, axis=0).reshape(bsz, seq, d)
```
